```python
import jax, jax.numpy as jnp
from jax import lax
import numpy as np

D_MODEL = 1024
BATCH = 8
SEQ = 4096
DEPTH = 4

N_META = 16
D_MIX = D_MODEL
LRU_WIDTH = D_MIX // 2
LRU_HEADS = 8
LRU_HEAD_DIM = LRU_WIDTH // LRU_HEADS
CONV_WIDTH = 4
LRU_C = 8.0
MLA_HEADS = 8
QK_NOPE = 64
QK_ROPE = 32
V_DIM = (D_MIX - LRU_WIDTH) // MLA_HEADS
Q_LORA = 3 * D_MODEL // 8
KV_LORA = D_MODEL // 4
IN_COLS = 2 * LRU_WIDTH + Q_LORA + KV_LORA + QK_ROPE
D_FF = 11 * D_MODEL // 4
ROPE_THETA = 10000.0
Q_BLOCK = 128
EPS = 1e-6

kernel_name = 'hymba_rglru_mla_macaron_sandwich'


def rms_norm(x, g):
    xf = x.astype(jnp.float32)
    y = xf * lax.rsqrt(jnp.mean(xf * xf, axis=-1, keepdims=True) + EPS)
    return (y * g.astype(jnp.float32)).astype(x.dtype)


def swiglu(u, w_gate, w_up, w_down):
    return (jax.nn.silu(u @ w_gate) * (u @ w_up)) @ w_down


def rope_tables(T):
    pos = jnp.arange(T, dtype=jnp.float32)
    inv_freq = 1.0 / (ROPE_THETA ** (jnp.arange(0, QK_ROPE, 2, dtype=jnp.float32) / QK_ROPE))
    ang = pos[:, None] * inv_freq[None, :]
    return jnp.cos(ang), jnp.sin(ang)


def apply_rope(x, cos, sin):
    x1, x2 = jnp.split(x, 2, axis=-1)
    cos = cos.astype(x.dtype)
    sin = sin.astype(x.dtype)
    return jnp.concatenate([x1 * cos - x2 * sin, x2 * cos + x1 * sin], axis=-1)


def _lru_combine(left, right):
    a_l, b_l = left
    a_r, b_r = right
    return a_l * a_r, a_r * b_l + b_r


def rglru_group(xr, gr, conv_w, conv_b, w_a, b_a, w_x, b_x, lam):
    B, T, W = xr.shape
    xp = jnp.pad(xr, ((0, 0), (CONV_WIDTH - 1, 0), (0, 0)))
    xc = xp[:, CONV_WIDTH - 1:] * conv_w[CONV_WIDTH - 1] + conv_b
    for k in range(CONV_WIDTH - 1):
        xc = xc + xp[:, k:k + T] * conv_w[k]
    xh = xc.reshape(B, T, LRU_HEADS, LRU_HEAD_DIM)
    r = jax.nn.sigmoid(jnp.einsum('bthi,hij->bthj', xh, w_a).reshape(B, T, W) + b_a)
    i = jax.nn.sigmoid(jnp.einsum('bthi,hij->bthj', xh, w_x).reshape(B, T, W) + b_x)
    log_a = -LRU_C * r.astype(jnp.float32) * jax.nn.softplus(-lam.astype(jnp.float32))
    a = jnp.exp(log_a)
    b = jnp.sqrt(-jnp.expm1(2.0 * log_a)) * (i * xc).astype(jnp.float32)
    _, h = lax.associative_scan(_lru_combine, (a, b), axis=1)
    return h.astype(xr.dtype) * jax.nn.gelu(gr)


def causal_block_attention(q, k, v):
    B, T, H, Dqk = q.shape
    n_blk = -(-T // Q_BLOCK)
    pad = n_blk * Q_BLOCK - T
    qb = jnp.pad(q, ((0, 0), (0, pad), (0, 0), (0, 0)))
    qb = qb.reshape(B, n_blk, Q_BLOCK, H, Dqk).transpose(1, 0, 2, 3, 4)
    scale = Dqk ** -0.5
    kpos = jnp.arange(T)

    def one_block(args):
        q_blk, blk = args
        s = jnp.einsum('bqhd,bkhd->bhqk', q_blk, k).astype(jnp.float32) * scale
        qpos = blk * Q_BLOCK + jnp.arange(Q_BLOCK)
        s = jnp.where(kpos[None, :] <= qpos[:, None], s, -jnp.inf)
        p = jax.nn.softmax(s, axis=-1).astype(v.dtype)
        return jnp.einsum('bhqk,bkhd->bqhd', p, v)

    o = lax.map(one_block, (qb, jnp.arange(n_blk)))
    o = o.transpose(1, 0, 2, 3, 4).reshape(B, n_blk * Q_BLOCK, H, v.shape[-1])
    return o[:, :T]


def mla_group(cq, ckv, kr, cos, sin, q_norm_g, w_uq, kv_norm_g, w_ukv):
    B, T, _ = cq.shape
    q = (rms_norm(cq, q_norm_g) @ w_uq).reshape(B, T, MLA_HEADS, QK_NOPE + QK_ROPE)
    q_nope, q_rope = jnp.split(q, [QK_NOPE], axis=-1)
    q_rope = apply_rope(q_rope, cos[None, :, None, :], sin[None, :, None, :])
    kv = (rms_norm(ckv, kv_norm_g) @ w_ukv).reshape(B, T, MLA_HEADS, QK_NOPE + V_DIM)
    k_nope, v = jnp.split(kv, [QK_NOPE], axis=-1)
    k_rope = apply_rope(kr, cos[None], sin[None])
    k_rope = jnp.broadcast_to(k_rope[:, :, None, :], (B, T, MLA_HEADS, QK_ROPE))
    qf = jnp.concatenate([q_nope, q_rope], axis=-1)
    kf = jnp.concatenate([k_nope, k_rope], axis=-1)
    return causal_block_attention(qf, kf, v).reshape(B, T, MLA_HEADS * V_DIM)


def _fwd_setup_inputs(seed: int = 0) -> dict:
    key = jax.random.key(seed)
    ks = jax.random.split(key, 32)

    def nrm(k, shape, scale):
        return jax.random.normal(k, shape, jnp.float32) * scale

    def gain(k, n):
        return 1.0 + 0.02 * jax.random.normal(k, (DEPTH, n), jnp.float32)

    a0 = jax.random.uniform(ks[15], (DEPTH, LRU_WIDTH), jnp.float32, 0.9, 0.999)
    lam = jnp.log(a0) - jnp.log1p(-a0)
    return {
        'x': nrm(ks[0], (BATCH, SEQ, D_MODEL), 1.0),
        'meta_tokens': nrm(ks[1], (N_META, D_MODEL), 1.0),
        'ffn1_pre_g': gain(ks[2], D_MODEL),
        'ffn1_w_gate': nrm(ks[3], (DEPTH, D_MODEL, D_FF), D_MODEL ** -0.5),
        'ffn1_w_up': nrm(ks[4], (DEPTH, D_MODEL, D_FF), D_MODEL ** -0.5),
        'ffn1_w_down': nrm(ks[5], (DEPTH, D_FF, D_MODEL), D_FF ** -0.5),
        'ffn1_post_g': gain(ks[6], D_MODEL),
        'mix_pre_g': gain(ks[7], D_MODEL),
        'w_in': nrm(ks[8], (DEPTH, D_MODEL, IN_COLS), D_MODEL ** -0.5),
        'lru_conv_w': nrm(ks[9], (DEPTH, CONV_WIDTH, LRU_WIDTH), CONV_WIDTH ** -0.5),
        'lru_conv_b': nrm(ks[10], (DEPTH, LRU_WIDTH), 0.01),
        'lru_w_a': nrm(ks[11], (DEPTH, LRU_HEADS, LRU_HEAD_DIM, LRU_HEAD_DIM), LRU_HEAD_DIM ** -0.5),
        'lru_b_a': nrm(ks[12], (DEPTH, LRU_WIDTH), 0.01),
        'lru_w_x': nrm(ks[13], (DEPTH, LRU_HEADS, LRU_HEAD_DIM, LRU_HEAD_DIM), LRU_HEAD_DIM ** -0.5),
        'lru_b_x': nrm(ks[14], (DEPTH, LRU_WIDTH), 0.01),
        'lru_lambda': lam,
        'mla_q_norm_g': gain(ks[16], Q_LORA),
        'mla_w_uq': nrm(ks[17], (DEPTH, Q_LORA, MLA_HEADS * (QK_NOPE + QK_ROPE)), Q_LORA ** -0.5),
        'mla_kv_norm_g': gain(ks[18], KV_LORA),
        'mla_w_ukv': nrm(ks[19], (DEPTH, KV_LORA, MLA_HEADS * (QK_NOPE + V_DIM)), KV_LORA ** -0.5),
        'lru_out_g': gain(ks[20], LRU_WIDTH),
        'mla_out_g': gain(ks[21], MLA_HEADS * V_DIM),
        'w_out': nrm(ks[22], (DEPTH, D_MIX, D_MODEL), D_MIX ** -0.5),
        'mix_post_g': gain(ks[23], D_MODEL),
        'ffn2_pre_g': gain(ks[24], D_MODEL),
        'ffn2_w_gate': nrm(ks[25], (DEPTH, D_MODEL, D_FF), D_MODEL ** -0.5),
        'ffn2_w_up': nrm(ks[26], (DEPTH, D_MODEL, D_FF), D_MODEL ** -0.5),
        'ffn2_w_down': nrm(ks[27], (DEPTH, D_FF, D_MODEL), D_FF ** -0.5),
        'ffn2_post_g': gain(ks[28], D_MODEL),
    }


def _fwd_reference(x, meta_tokens, ffn1_pre_g, ffn1_w_gate, ffn1_w_up, ffn1_w_down, ffn1_post_g,
              mix_pre_g, w_in, lru_conv_w, lru_conv_b, lru_w_a, lru_b_a, lru_w_x, lru_b_x,
              lru_lambda, mla_q_norm_g, mla_w_uq, mla_kv_norm_g, mla_w_ukv, lru_out_g,
              mla_out_g, w_out, mix_post_g, ffn2_pre_g, ffn2_w_gate, ffn2_w_up, ffn2_w_down,
              ffn2_post_g):
    B = x.shape[0]
    meta = jnp.broadcast_to(meta_tokens.astype(x.dtype)[None], (B, N_META, D_MODEL))
    h = jnp.concatenate([meta, x], axis=1)
    T = h.shape[1]
    cos, sin = rope_tables(T)
    splits = [LRU_WIDTH, 2 * LRU_WIDTH, 2 * LRU_WIDTH + Q_LORA, 2 * LRU_WIDTH + Q_LORA + KV_LORA]
    for l in range(DEPTH):
        f = swiglu(rms_norm(h, ffn1_pre_g[l]), ffn1_w_gate[l], ffn1_w_up[l], ffn1_w_down[l])
        h = h + 0.5 * rms_norm(f, ffn1_post_g[l])
        z = rms_norm(h, mix_pre_g[l]) @ w_in[l]
        xr, gr, cq, ckv, kr = jnp.split(z, splits, axis=-1)
        y_lru = rglru_group(xr, gr, lru_conv_w[l], lru_conv_b[l], lru_w_a[l], lru_b_a[l],
                            lru_w_x[l], lru_b_x[l], lru_lambda[l])
        y_mla = mla_group(cq, ckv, kr, cos, sin, mla_q_norm_g[l], mla_w_uq[l],
                          mla_kv_norm_g[l], mla_w_ukv[l])
        y = jnp.concatenate([rms_norm(y_lru, lru_out_g[l]), rms_norm(y_mla, mla_out_g[l])],
                            axis=-1) @ w_out[l]
        h = h + rms_norm(y, mix_post_g[l])
        f = swiglu(rms_norm(h, ffn2_pre_g[l]), ffn2_w_gate[l], ffn2_w_up[l], ffn2_w_down[l])
        h = h + 0.5 * rms_norm(f, ffn2_post_g[l])
    return h[:, N_META:]


import jax as _jax
import jax.numpy as _jnp

TWIN_FORMAT = 'train_step'
FWD_PARAMS = ['x', 'meta_tokens', 'ffn1_pre_g', 'ffn1_w_gate', 'ffn1_w_up', 'ffn1_w_down', 'ffn1_post_g', 'mix_pre_g', 'w_in', 'lru_conv_w', 'lru_conv_b', 'lru_w_a', 'lru_b_a', 'lru_w_x', 'lru_b_x', 'lru_lambda', 'mla_q_norm_g', 'mla_w_uq', 'mla_kv_norm_g', 'mla_w_ukv', 'lru_out_g', 'mla_out_g', 'w_out', 'mix_post_g', 'ffn2_pre_g', 'ffn2_w_gate', 'ffn2_w_up', 'ffn2_w_down', 'ffn2_post_g']
TWIN_WEIGHTS = ['meta_tokens', 'ffn1_pre_g', 'ffn1_w_gate', 'ffn1_w_up', 'ffn1_w_down', 'ffn1_post_g', 'mix_pre_g', 'w_in', 'lru_conv_w', 'lru_conv_b', 'lru_w_a', 'lru_b_a', 'lru_w_x', 'lru_b_x', 'lru_lambda', 'mla_q_norm_g', 'mla_w_uq', 'mla_kv_norm_g', 'mla_w_ukv', 'lru_out_g', 'mla_out_g', 'w_out', 'mix_post_g', 'ffn2_pre_g', 'ffn2_w_gate', 'ffn2_w_up', 'ffn2_w_down', 'ffn2_post_g']
TWIN_DIFF_INPUT = 'x'
TWIN_INPUTS = ['x', 'meta_tokens', 'ffn1_pre_g', 'ffn1_w_gate', 'ffn1_w_up', 'ffn1_w_down', 'ffn1_post_g', 'mix_pre_g', 'w_in', 'lru_conv_w', 'lru_conv_b', 'lru_w_a', 'lru_b_a', 'lru_w_x', 'lru_b_x', 'lru_lambda', 'mla_q_norm_g', 'mla_w_uq', 'mla_kv_norm_g', 'mla_w_ukv', 'lru_out_g', 'mla_out_g', 'w_out', 'mix_post_g', 'ffn2_pre_g', 'ffn2_w_gate', 'ffn2_w_up', 'ffn2_w_down', 'ffn2_post_g', 'loss_target', 'm_meta_tokens', 'm_ffn1_pre_g', 'm_ffn1_w_gate', 'm_ffn1_w_up', 'm_ffn1_w_down', 'm_ffn1_post_g', 'm_mix_pre_g', 'm_w_in', 'm_lru_conv_w', 'm_lru_conv_b', 'm_lru_w_a', 'm_lru_b_a', 'm_lru_w_x', 'm_lru_b_x', 'm_lru_lambda', 'm_mla_q_norm_g', 'm_mla_w_uq', 'm_mla_kv_norm_g', 'm_mla_w_ukv', 'm_lru_out_g', 'm_mla_out_g', 'm_w_out', 'm_mix_post_g', 'm_ffn2_pre_g', 'm_ffn2_w_gate', 'm_ffn2_w_up', 'm_ffn2_w_down', 'm_ffn2_post_g', 'v_meta_tokens', 'v_ffn1_pre_g', 'v_ffn1_w_gate', 'v_ffn1_w_up', 'v_ffn1_w_down', 'v_ffn1_post_g', 'v_mix_pre_g', 'v_w_in', 'v_lru_conv_w', 'v_lru_conv_b', 'v_lru_w_a', 'v_lru_b_a', 'v_lru_w_x', 'v_lru_b_x', 'v_lru_lambda', 'v_mla_q_norm_g', 'v_mla_w_uq', 'v_mla_kv_norm_g', 'v_mla_w_ukv', 'v_lru_out_g', 'v_mla_out_g', 'v_w_out', 'v_mix_post_g', 'v_ffn2_pre_g', 'v_ffn2_w_gate', 'v_ffn2_w_up', 'v_ffn2_w_down', 'v_ffn2_post_g']
TWIN_OUTPUTS = ['loss', 'grad_x', 'grad_meta_tokens', 'grad_ffn1_pre_g', 'grad_ffn1_w_gate', 'grad_ffn1_w_up', 'grad_ffn1_w_down', 'grad_ffn1_post_g', 'grad_mix_pre_g', 'grad_w_in', 'grad_lru_conv_w', 'grad_lru_conv_b', 'grad_lru_w_a', 'grad_lru_b_a', 'grad_lru_w_x', 'grad_lru_b_x', 'grad_lru_lambda', 'grad_mla_q_norm_g', 'grad_mla_w_uq', 'grad_mla_kv_norm_g', 'grad_mla_w_ukv', 'grad_lru_out_g', 'grad_mla_out_g', 'grad_w_out', 'grad_mix_post_g', 'grad_ffn2_pre_g', 'grad_ffn2_w_gate', 'grad_ffn2_w_up', 'grad_ffn2_w_down', 'grad_ffn2_post_g', 'delta_meta_tokens', 'delta_ffn1_pre_g', 'delta_ffn1_w_gate', 'delta_ffn1_w_up', 'delta_ffn1_w_down', 'delta_ffn1_post_g', 'delta_mix_pre_g', 'delta_w_in', 'delta_lru_conv_w', 'delta_lru_conv_b', 'delta_lru_w_a', 'delta_lru_b_a', 'delta_lru_w_x', 'delta_lru_b_x', 'delta_lru_lambda', 'delta_mla_q_norm_g', 'delta_mla_w_uq', 'delta_mla_kv_norm_g', 'delta_mla_w_ukv', 'delta_lru_out_g', 'delta_mla_out_g', 'delta_w_out', 'delta_mix_post_g', 'delta_ffn2_pre_g', 'delta_ffn2_w_gate', 'delta_ffn2_w_up', 'delta_ffn2_w_down', 'delta_ffn2_post_g', 'new_m_meta_tokens', 'new_m_ffn1_pre_g', 'new_m_ffn1_w_gate', 'new_m_ffn1_w_up', 'new_m_ffn1_w_down', 'new_m_ffn1_post_g', 'new_m_mix_pre_g', 'new_m_w_in', 'new_m_lru_conv_w', 'new_m_lru_conv_b', 'new_m_lru_w_a', 'new_m_lru_b_a', 'new_m_lru_w_x', 'new_m_lru_b_x', 'new_m_lru_lambda', 'new_m_mla_q_norm_g', 'new_m_mla_w_uq', 'new_m_mla_kv_norm_g', 'new_m_mla_w_ukv', 'new_m_lru_out_g', 'new_m_mla_out_g', 'new_m_w_out', 'new_m_mix_post_g', 'new_m_ffn2_pre_g', 'new_m_ffn2_w_gate', 'new_m_ffn2_w_up', 'new_m_ffn2_w_down', 'new_m_ffn2_post_g', 'new_v_meta_tokens', 'new_v_ffn1_pre_g', 'new_v_ffn1_w_gate', 'new_v_ffn1_w_up', 'new_v_ffn1_w_down', 'new_v_ffn1_post_g', 'new_v_mix_pre_g', 'new_v_w_in', 'new_v_lru_conv_w', 'new_v_lru_conv_b', 'new_v_lru_w_a', 'new_v_lru_b_a', 'new_v_lru_w_x', 'new_v_lru_b_x', 'new_v_lru_lambda', 'new_v_mla_q_norm_g', 'new_v_mla_w_uq', 'new_v_mla_kv_norm_g', 'new_v_mla_w_ukv', 'new_v_lru_out_g', 'new_v_mla_out_g', 'new_v_w_out', 'new_v_mix_post_g', 'new_v_ffn2_pre_g', 'new_v_ffn2_w_gate', 'new_v_ffn2_w_up', 'new_v_ffn2_w_down', 'new_v_ffn2_post_g']
TWIN_LEAF_KINDS = {'loss': 'loss', 'grad_x': 'grad_x', 'grad_meta_tokens': 'grad_w', 'grad_ffn1_pre_g': 'grad_w', 'grad_ffn1_w_gate': 'grad_w', 'grad_ffn1_w_up': 'grad_w', 'grad_ffn1_w_down': 'grad_w', 'grad_ffn1_post_g': 'grad_w', 'grad_mix_pre_g': 'grad_w', 'grad_w_in': 'grad_w', 'grad_lru_conv_w': 'grad_w', 'grad_lru_conv_b': 'grad_w', 'grad_lru_w_a': 'grad_w', 'grad_lru_b_a': 'grad_w', 'grad_lru_w_x': 'grad_w', 'grad_lru_b_x': 'grad_w', 'grad_lru_lambda': 'grad_w', 'grad_mla_q_norm_g': 'grad_w', 'grad_mla_w_uq': 'grad_w', 'grad_mla_kv_norm_g': 'grad_w', 'grad_mla_w_ukv': 'grad_w', 'grad_lru_out_g': 'grad_w', 'grad_mla_out_g': 'grad_w', 'grad_w_out': 'grad_w', 'grad_mix_post_g': 'grad_w', 'grad_ffn2_pre_g': 'grad_w', 'grad_ffn2_w_gate': 'grad_w', 'grad_ffn2_w_up': 'grad_w', 'grad_ffn2_w_down': 'grad_w', 'grad_ffn2_post_g': 'grad_w', 'delta_meta_tokens': 'delta_w', 'delta_ffn1_pre_g': 'delta_w', 'delta_ffn1_w_gate': 'delta_w', 'delta_ffn1_w_up': 'delta_w', 'delta_ffn1_w_down': 'delta_w', 'delta_ffn1_post_g': 'delta_w', 'delta_mix_pre_g': 'delta_w', 'delta_w_in': 'delta_w', 'delta_lru_conv_w': 'delta_w', 'delta_lru_conv_b': 'delta_w', 'delta_lru_w_a': 'delta_w', 'delta_lru_b_a': 'delta_w', 'delta_lru_w_x': 'delta_w', 'delta_lru_b_x': 'delta_w', 'delta_lru_lambda': 'delta_w', 'delta_mla_q_norm_g': 'delta_w', 'delta_mla_w_uq': 'delta_w', 'delta_mla_kv_norm_g': 'delta_w', 'delta_mla_w_ukv': 'delta_w', 'delta_lru_out_g': 'delta_w', 'delta_mla_out_g': 'delta_w', 'delta_w_out': 'delta_w', 'delta_mix_post_g': 'delta_w', 'delta_ffn2_pre_g': 'delta_w', 'delta_ffn2_w_gate': 'delta_w', 'delta_ffn2_w_up': 'delta_w', 'delta_ffn2_w_down': 'delta_w', 'delta_ffn2_post_g': 'delta_w', 'new_m_meta_tokens': 'new_m', 'new_m_ffn1_pre_g': 'new_m', 'new_m_ffn1_w_gate': 'new_m', 'new_m_ffn1_w_up': 'new_m', 'new_m_ffn1_w_down': 'new_m', 'new_m_ffn1_post_g': 'new_m', 'new_m_mix_pre_g': 'new_m', 'new_m_w_in': 'new_m', 'new_m_lru_conv_w': 'new_m', 'new_m_lru_conv_b': 'new_m', 'new_m_lru_w_a': 'new_m', 'new_m_lru_b_a': 'new_m', 'new_m_lru_w_x': 'new_m', 'new_m_lru_b_x': 'new_m', 'new_m_lru_lambda': 'new_m', 'new_m_mla_q_norm_g': 'new_m', 'new_m_mla_w_uq': 'new_m', 'new_m_mla_kv_norm_g': 'new_m', 'new_m_mla_w_ukv': 'new_m', 'new_m_lru_out_g': 'new_m', 'new_m_mla_out_g': 'new_m', 'new_m_w_out': 'new_m', 'new_m_mix_post_g': 'new_m', 'new_m_ffn2_pre_g': 'new_m', 'new_m_ffn2_w_gate': 'new_m', 'new_m_ffn2_w_up': 'new_m', 'new_m_ffn2_w_down': 'new_m', 'new_m_ffn2_post_g': 'new_m', 'new_v_meta_tokens': 'new_v', 'new_v_ffn1_pre_g': 'new_v', 'new_v_ffn1_w_gate': 'new_v', 'new_v_ffn1_w_up': 'new_v', 'new_v_ffn1_w_down': 'new_v', 'new_v_ffn1_post_g': 'new_v', 'new_v_mix_pre_g': 'new_v', 'new_v_w_in': 'new_v', 'new_v_lru_conv_w': 'new_v', 'new_v_lru_conv_b': 'new_v', 'new_v_lru_w_a': 'new_v', 'new_v_lru_b_a': 'new_v', 'new_v_lru_w_x': 'new_v', 'new_v_lru_b_x': 'new_v', 'new_v_lru_lambda': 'new_v', 'new_v_mla_q_norm_g': 'new_v', 'new_v_mla_w_uq': 'new_v', 'new_v_mla_kv_norm_g': 'new_v', 'new_v_mla_w_ukv': 'new_v', 'new_v_lru_out_g': 'new_v', 'new_v_mla_out_g': 'new_v', 'new_v_w_out': 'new_v', 'new_v_mix_post_g': 'new_v', 'new_v_ffn2_pre_g': 'new_v', 'new_v_ffn2_w_gate': 'new_v', 'new_v_ffn2_w_up': 'new_v', 'new_v_ffn2_w_down': 'new_v', 'new_v_ffn2_post_g': 'new_v'}


def _forward(args):
    return _fwd_reference(*[args[k] for k in FWD_PARAMS])


def _output_shape():
    out = _jax.eval_shape(lambda: _forward(_fwd_setup_inputs(0)))
    return out.shape, out.dtype

N_MICROBATCH = 1
ADAM_LR = 0.001
ADAM_B1 = 0.9
ADAM_B2 = 0.999
ADAM_EPS = 1e-08
ADAM_WD = 0.01
ADAM_STEP = 10
PER_EXAMPLE_BATCH_AXIS = {'x': 0, 'loss_target': 0}
SHARED_INPUTS = []
_WEIGHT_DTYPES = {'meta_tokens': _jnp.float32, 'ffn1_pre_g': _jnp.float32, 'ffn1_w_gate': _jnp.float32, 'ffn1_w_up': _jnp.float32, 'ffn1_w_down': _jnp.float32, 'ffn1_post_g': _jnp.float32, 'mix_pre_g': _jnp.float32, 'w_in': _jnp.float32, 'lru_conv_w': _jnp.float32, 'lru_conv_b': _jnp.float32, 'lru_w_a': _jnp.float32, 'lru_b_a': _jnp.float32, 'lru_w_x': _jnp.float32, 'lru_b_x': _jnp.float32, 'lru_lambda': _jnp.float32, 'mla_q_norm_g': _jnp.float32, 'mla_w_uq': _jnp.float32, 'mla_kv_norm_g': _jnp.float32, 'mla_w_ukv': _jnp.float32, 'lru_out_g': _jnp.float32, 'mla_out_g': _jnp.float32, 'w_out': _jnp.float32, 'mix_post_g': _jnp.float32, 'ffn2_pre_g': _jnp.float32, 'ffn2_w_gate': _jnp.float32, 'ffn2_w_up': _jnp.float32, 'ffn2_w_down': _jnp.float32, 'ffn2_post_g': _jnp.float32}
MOMENT_SCALE = {'meta_tokens': 2.129000e+00, 'ffn1_pre_g': 7.806424e+00, 'ffn1_w_gate': 2.820773e+00, 'ffn1_w_up': 3.004755e+00, 'ffn1_w_down': 4.968463e+00, 'ffn1_post_g': 8.578967e+00, 'mix_pre_g': 1.492031e+01, 'w_in': 1.202251e+01, 'lru_conv_w': 7.869688e+00, 'lru_conv_b': 7.672697e+01, 'lru_w_a': 2.529729e+00, 'lru_b_a': 1.893133e+00, 'lru_w_x': 5.127516e+00, 'lru_b_x': 2.680818e+00, 'lru_lambda': 3.630163e+00, 'mla_q_norm_g': 1.447867e+00, 'mla_w_uq': 1.154129e+00, 'mla_kv_norm_g': 3.607285e+01, 'mla_w_ukv': 1.495104e+01, 'lru_out_g': 7.698827e+00, 'mla_out_g': 1.882601e+01, 'w_out': 1.634265e+01, 'mix_post_g': 3.634680e+01, 'ffn2_pre_g': 3.026475e+00, 'ffn2_w_gate': 1.016281e+00, 'ffn2_w_up': 1.471677e+00, 'ffn2_w_down': 2.393722e+00, 'ffn2_post_g': 7.893639e+00}


def _to_microbatches(a, axis):
    t = _jnp.moveaxis(a, axis, 0)
    t = t.reshape((N_MICROBATCH, t.shape[0] // N_MICROBATCH) + t.shape[1:])
    return _jnp.moveaxis(t, 1, axis + 1)


def setup_inputs(seed: int = 0) -> dict:
    inp = _fwd_setup_inputs(seed)
    key = _jax.random.fold_in(_jax.random.key(seed), 7919)
    shape, _ = _output_shape()
    out = dict(inp)
    out["loss_target"] = _jax.random.normal(_jax.random.fold_in(key, 0), shape, _jnp.float32)
    for i, name in enumerate(TWIN_WEIGHTS):
        w = inp[name].astype(_jnp.float32)
        if MOMENT_SCALE is None:
            s = _jnp.sqrt(_jnp.mean(_jnp.square(w)) + 1e-30)
        else:
            s = MOMENT_SCALE[name]
        km, kv = _jax.random.split(_jax.random.fold_in(key, i + 1))
        out[name] = w
        out["m_" + name] = s * _jax.random.normal(km, w.shape, _jnp.float32)
        out["v_" + name] = (s * s) * _jax.random.uniform(kv, w.shape, _jnp.float32, 0.5, 1.5)
    if N_MICROBATCH > 1:
        for name, axis in PER_EXAMPLE_BATCH_AXIS.items():
            out[name] = _to_microbatches(out[name], axis)
    return {'x': out['x'], 'meta_tokens': out['meta_tokens'], 'ffn1_pre_g': out['ffn1_pre_g'], 'ffn1_w_gate': out['ffn1_w_gate'], 'ffn1_w_up': out['ffn1_w_up'], 'ffn1_w_down': out['ffn1_w_down'], 'ffn1_post_g': out['ffn1_post_g'], 'mix_pre_g': out['mix_pre_g'], 'w_in': out['w_in'], 'lru_conv_w': out['lru_conv_w'], 'lru_conv_b': out['lru_conv_b'], 'lru_w_a': out['lru_w_a'], 'lru_b_a': out['lru_b_a'], 'lru_w_x': out['lru_w_x'], 'lru_b_x': out['lru_b_x'], 'lru_lambda': out['lru_lambda'], 'mla_q_norm_g': out['mla_q_norm_g'], 'mla_w_uq': out['mla_w_uq'], 'mla_kv_norm_g': out['mla_kv_norm_g'], 'mla_w_ukv': out['mla_w_ukv'], 'lru_out_g': out['lru_out_g'], 'mla_out_g': out['mla_out_g'], 'w_out': out['w_out'], 'mix_post_g': out['mix_post_g'], 'ffn2_pre_g': out['ffn2_pre_g'], 'ffn2_w_gate': out['ffn2_w_gate'], 'ffn2_w_up': out['ffn2_w_up'], 'ffn2_w_down': out['ffn2_w_down'], 'ffn2_post_g': out['ffn2_post_g'], 'loss_target': out['loss_target'], 'm_meta_tokens': out['m_meta_tokens'], 'm_ffn1_pre_g': out['m_ffn1_pre_g'], 'm_ffn1_w_gate': out['m_ffn1_w_gate'], 'm_ffn1_w_up': out['m_ffn1_w_up'], 'm_ffn1_w_down': out['m_ffn1_w_down'], 'm_ffn1_post_g': out['m_ffn1_post_g'], 'm_mix_pre_g': out['m_mix_pre_g'], 'm_w_in': out['m_w_in'], 'm_lru_conv_w': out['m_lru_conv_w'], 'm_lru_conv_b': out['m_lru_conv_b'], 'm_lru_w_a': out['m_lru_w_a'], 'm_lru_b_a': out['m_lru_b_a'], 'm_lru_w_x': out['m_lru_w_x'], 'm_lru_b_x': out['m_lru_b_x'], 'm_lru_lambda': out['m_lru_lambda'], 'm_mla_q_norm_g': out['m_mla_q_norm_g'], 'm_mla_w_uq': out['m_mla_w_uq'], 'm_mla_kv_norm_g': out['m_mla_kv_norm_g'], 'm_mla_w_ukv': out['m_mla_w_ukv'], 'm_lru_out_g': out['m_lru_out_g'], 'm_mla_out_g': out['m_mla_out_g'], 'm_w_out': out['m_w_out'], 'm_mix_post_g': out['m_mix_post_g'], 'm_ffn2_pre_g': out['m_ffn2_pre_g'], 'm_ffn2_w_gate': out['m_ffn2_w_gate'], 'm_ffn2_w_up': out['m_ffn2_w_up'], 'm_ffn2_w_down': out['m_ffn2_w_down'], 'm_ffn2_post_g': out['m_ffn2_post_g'], 'v_meta_tokens': out['v_meta_tokens'], 'v_ffn1_pre_g': out['v_ffn1_pre_g'], 'v_ffn1_w_gate': out['v_ffn1_w_gate'], 'v_ffn1_w_up': out['v_ffn1_w_up'], 'v_ffn1_w_down': out['v_ffn1_w_down'], 'v_ffn1_post_g': out['v_ffn1_post_g'], 'v_mix_pre_g': out['v_mix_pre_g'], 'v_w_in': out['v_w_in'], 'v_lru_conv_w': out['v_lru_conv_w'], 'v_lru_conv_b': out['v_lru_conv_b'], 'v_lru_w_a': out['v_lru_w_a'], 'v_lru_b_a': out['v_lru_b_a'], 'v_lru_w_x': out['v_lru_w_x'], 'v_lru_b_x': out['v_lru_b_x'], 'v_lru_lambda': out['v_lru_lambda'], 'v_mla_q_norm_g': out['v_mla_q_norm_g'], 'v_mla_w_uq': out['v_mla_w_uq'], 'v_mla_kv_norm_g': out['v_mla_kv_norm_g'], 'v_mla_w_ukv': out['v_mla_w_ukv'], 'v_lru_out_g': out['v_lru_out_g'], 'v_mla_out_g': out['v_mla_out_g'], 'v_w_out': out['v_w_out'], 'v_mix_post_g': out['v_mix_post_g'], 'v_ffn2_pre_g': out['v_ffn2_pre_g'], 'v_ffn2_w_gate': out['v_ffn2_w_gate'], 'v_ffn2_w_up': out['v_ffn2_w_up'], 'v_ffn2_w_down': out['v_ffn2_w_down'], 'v_ffn2_post_g': out['v_ffn2_post_g']}


def _loss(weights, diff, rest, loss_target):
    with _jax.named_scope("forward"):
        args = {**rest, TWIN_DIFF_INPUT: diff, **{k: w.astype(_WEIGHT_DTYPES[k]) for k, w in weights.items()}}
        y = _forward(args)
    with _jax.named_scope("loss_head"):
        err = _jnp.square(y.astype(_jnp.float32) - loss_target)
        return 0.5 * _jnp.sum(_jnp.mean(err, axis=-1)) if err.ndim else 0.5 * err


def _adamw(w, g, m, v):
    m = ADAM_B1 * m + (1.0 - ADAM_B1) * g
    v = ADAM_B2 * v + (1.0 - ADAM_B2) * _jnp.square(g)
    m_hat = m / (1.0 - ADAM_B1 ** ADAM_STEP)
    v_hat = v / (1.0 - ADAM_B2 ** ADAM_STEP)
    delta = -ADAM_LR * (m_hat / (_jnp.sqrt(v_hat) + ADAM_EPS) + ADAM_WD * w)
    return delta, m, v


def reference(x, meta_tokens, ffn1_pre_g, ffn1_w_gate, ffn1_w_up, ffn1_w_down, ffn1_post_g, mix_pre_g, w_in, lru_conv_w, lru_conv_b, lru_w_a, lru_b_a, lru_w_x, lru_b_x, lru_lambda, mla_q_norm_g, mla_w_uq, mla_kv_norm_g, mla_w_ukv, lru_out_g, mla_out_g, w_out, mix_post_g, ffn2_pre_g, ffn2_w_gate, ffn2_w_up, ffn2_w_down, ffn2_post_g, loss_target, m_meta_tokens, m_ffn1_pre_g, m_ffn1_w_gate, m_ffn1_w_up, m_ffn1_w_down, m_ffn1_post_g, m_mix_pre_g, m_w_in, m_lru_conv_w, m_lru_conv_b, m_lru_w_a, m_lru_b_a, m_lru_w_x, m_lru_b_x, m_lru_lambda, m_mla_q_norm_g, m_mla_w_uq, m_mla_kv_norm_g, m_mla_w_ukv, m_lru_out_g, m_mla_out_g, m_w_out, m_mix_post_g, m_ffn2_pre_g, m_ffn2_w_gate, m_ffn2_w_up, m_ffn2_w_down, m_ffn2_post_g, v_meta_tokens, v_ffn1_pre_g, v_ffn1_w_gate, v_ffn1_w_up, v_ffn1_w_down, v_ffn1_post_g, v_mix_pre_g, v_w_in, v_lru_conv_w, v_lru_conv_b, v_lru_w_a, v_lru_b_a, v_lru_w_x, v_lru_b_x, v_lru_lambda, v_mla_q_norm_g, v_mla_w_uq, v_mla_kv_norm_g, v_mla_w_ukv, v_lru_out_g, v_mla_out_g, v_w_out, v_mix_post_g, v_ffn2_pre_g, v_ffn2_w_gate, v_ffn2_w_up, v_ffn2_w_down, v_ffn2_post_g):
    given = dict(x=x, meta_tokens=meta_tokens, ffn1_pre_g=ffn1_pre_g, ffn1_w_gate=ffn1_w_gate, ffn1_w_up=ffn1_w_up, ffn1_w_down=ffn1_w_down, ffn1_post_g=ffn1_post_g, mix_pre_g=mix_pre_g, w_in=w_in, lru_conv_w=lru_conv_w, lru_conv_b=lru_conv_b, lru_w_a=lru_w_a, lru_b_a=lru_b_a, lru_w_x=lru_w_x, lru_b_x=lru_b_x, lru_lambda=lru_lambda, mla_q_norm_g=mla_q_norm_g, mla_w_uq=mla_w_uq, mla_kv_norm_g=mla_kv_norm_g, mla_w_ukv=mla_w_ukv, lru_out_g=lru_out_g, mla_out_g=mla_out_g, w_out=w_out, mix_post_g=mix_post_g, ffn2_pre_g=ffn2_pre_g, ffn2_w_gate=ffn2_w_gate, ffn2_w_up=ffn2_w_up, ffn2_w_down=ffn2_w_down, ffn2_post_g=ffn2_post_g, loss_target=loss_target, m_meta_tokens=m_meta_tokens, m_ffn1_pre_g=m_ffn1_pre_g, m_ffn1_w_gate=m_ffn1_w_gate, m_ffn1_w_up=m_ffn1_w_up, m_ffn1_w_down=m_ffn1_w_down, m_ffn1_post_g=m_ffn1_post_g, m_mix_pre_g=m_mix_pre_g, m_w_in=m_w_in, m_lru_conv_w=m_lru_conv_w, m_lru_conv_b=m_lru_conv_b, m_lru_w_a=m_lru_w_a, m_lru_b_a=m_lru_b_a, m_lru_w_x=m_lru_w_x, m_lru_b_x=m_lru_b_x, m_lru_lambda=m_lru_lambda, m_mla_q_norm_g=m_mla_q_norm_g, m_mla_w_uq=m_mla_w_uq, m_mla_kv_norm_g=m_mla_kv_norm_g, m_mla_w_ukv=m_mla_w_ukv, m_lru_out_g=m_lru_out_g, m_mla_out_g=m_mla_out_g, m_w_out=m_w_out, m_mix_post_g=m_mix_post_g, m_ffn2_pre_g=m_ffn2_pre_g, m_ffn2_w_gate=m_ffn2_w_gate, m_ffn2_w_up=m_ffn2_w_up, m_ffn2_w_down=m_ffn2_w_down, m_ffn2_post_g=m_ffn2_post_g, v_meta_tokens=v_meta_tokens, v_ffn1_pre_g=v_ffn1_pre_g, v_ffn1_w_gate=v_ffn1_w_gate, v_ffn1_w_up=v_ffn1_w_up, v_ffn1_w_down=v_ffn1_w_down, v_ffn1_post_g=v_ffn1_post_g, v_mix_pre_g=v_mix_pre_g, v_w_in=v_w_in, v_lru_conv_w=v_lru_conv_w, v_lru_conv_b=v_lru_conv_b, v_lru_w_a=v_lru_w_a, v_lru_b_a=v_lru_b_a, v_lru_w_x=v_lru_w_x, v_lru_b_x=v_lru_b_x, v_lru_lambda=v_lru_lambda, v_mla_q_norm_g=v_mla_q_norm_g, v_mla_w_uq=v_mla_w_uq, v_mla_kv_norm_g=v_mla_kv_norm_g, v_mla_w_ukv=v_mla_w_ukv, v_lru_out_g=v_lru_out_g, v_mla_out_g=v_mla_out_g, v_w_out=v_w_out, v_mix_post_g=v_mix_post_g, v_ffn2_pre_g=v_ffn2_pre_g, v_ffn2_w_gate=v_ffn2_w_gate, v_ffn2_w_up=v_ffn2_w_up, v_ffn2_w_down=v_ffn2_w_down, v_ffn2_post_g=v_ffn2_post_g)
    weights = {n: given[n] for n in TWIN_WEIGHTS}
    shared = {n: given[n] for n in SHARED_INPUTS}
    per_example = {n: given[n] for n in ['x']}
    grad_fn = _jax.value_and_grad(_loss, argnums=(0, 1))

    def one_microbatch(ex, loss_target):
        ex = dict(ex)
        diff = ex.pop(TWIN_DIFF_INPUT)
        return grad_fn(weights, diff, {**shared, **ex}, loss_target)

    if N_MICROBATCH == 1:
        loss, (grad_w, grad_x) = one_microbatch(per_example, given["loss_target"])
    else:
        def body(carry, xs):
            loss_sum, grad_sum = carry
            l_k, (gw_k, gx_k) = one_microbatch(xs[0], xs[1])
            with _jax.named_scope("update"):
                return (loss_sum + l_k, _jax.tree.map(_jnp.add, grad_sum, gw_k)), gx_k

        init = (_jnp.zeros((), _jnp.float32), _jax.tree.map(_jnp.zeros_like, weights))
        (loss, grad_w), grad_x = _jax.lax.scan(body, init, (per_example, given["loss_target"]))
    with _jax.named_scope("update"):
        delta_w, new_m, new_v = {}, {}, {}
        for n in TWIN_WEIGHTS:
            delta_w[n], new_m[n], new_v[n] = _adamw(weights[n], grad_w[n], given["m_" + n], given["v_" + n])
    return (loss, grad_x, *[grad_w[n] for n in TWIN_WEIGHTS], *[delta_w[n] for n in TWIN_WEIGHTS],
            *[new_m[n] for n in TWIN_WEIGHTS], *[new_v[n] for n in TWIN_WEIGHTS])
```

```python
import functools
import math

import jax
import jax.numpy as jnp
from jax import lax
from jax.experimental import pallas as pl
from jax.experimental.pallas import tpu as pltpu

F32 = jnp.float32
BF16 = jnp.bfloat16
MESH = pl.DeviceIdType.MESH

EPS = 1e-6
N_META = 16
LRU_HEADS = 8
MLA_HEADS = 8
QK_NOPE = 64
QK_ROPE = 32
QK_DIM = QK_NOPE + QK_ROPE
LRU_C = 8.0
ROPE_THETA = 10000.0
CONV_WIDTH = 4
N_CHIPS = 4

ADAM_LR = 0.001
ADAM_B1 = 0.9
ADAM_B2 = 0.999
ADAM_EPS = 1e-08
ADAM_WD = 0.01
ADAM_STEP = 10

V7X_VMEM_LIMIT_BYTES = 56 * 1024 * 1024
NEG_BIG = -1e30


def _cparams(sem=None):
    return pltpu.CompilerParams(dimension_semantics=sem, vmem_limit_bytes=V7X_VMEM_LIMIT_BYTES)


def _round_up(a, b):
    return -(-a // b) * b


def _dot(a, b):
    return jnp.dot(a.astype(BF16), b.astype(BF16), preferred_element_type=F32)


def _dot_nt(a, b):
    return lax.dot_general(a.astype(BF16), b.astype(BF16), (((1,), (1,)), ((), ())), preferred_element_type=F32)


def _dot_tn(a, b):
    return lax.dot_general(a.astype(BF16), b.astype(BF16), (((0,), (0,)), ((), ())), preferred_element_type=F32)


def _dot_split(x, p):
    hi = x.astype(BF16)
    lo = (x - hi.astype(F32)).astype(BF16)
    return jnp.dot(hi, p, preferred_element_type=F32) + jnp.dot(lo, p, preferred_element_type=F32)


def _rms(x, g):
    r = lax.rsqrt(jnp.mean(x * x, axis=-1, keepdims=True) + EPS)
    return x * r * g, r


def _rms_bwd(x, g, dy):
    r = lax.rsqrt(jnp.mean(x * x, axis=-1, keepdims=True) + EPS)
    xh = x * r
    dyg = dy * g
    dx = r * (dyg - xh * jnp.mean(dyg * xh, axis=-1, keepdims=True))
    dg = jnp.sum(dy * xh, axis=0, keepdims=True)
    return dx, dg


def _sigmoid(x):
    return 1.0 / (1.0 + jnp.exp(-x))


def _neg_expm1(x):
    series = -x * (1.0 + x * (0.5 + x * (1.0 / 6.0 + x * (1.0 / 24.0))))
    return jnp.where(jnp.abs(x) < 0.03, series, 1.0 - jnp.exp(x))


_GELU_K = math.sqrt(2.0 / math.pi)


def _gelu(x):
    t = jnp.tanh(_GELU_K * (x + 0.044715 * x * x * x))
    return 0.5 * x * (1.0 + t), t


def _gelu_grad(x, t):
    return 0.5 * (1.0 + t) + 0.5 * x * (1.0 - t * t) * _GELU_K * (1.0 + 3.0 * 0.044715 * x * x)


def _rope(x, cos_t, sin_t, rot):
    return x * cos_t + _dot_split(x, rot) * sin_t


def _ffn_fwd(h, g_pre, g_post, w_gu, w_dn, layer, kind_gate, kind_up, kind_dn, tm, name):
    tp, d = h.shape
    n_slot, fs = w_gu.shape[0], w_gu.shape[-1]

    def body(h_ref, gpre_ref, gpost_ref, wg_ref, wu_ref, wd_ref, hout_ref, gate_ref, up_ref, f_ref, u_sc, acc_sc):
        j = pl.program_id(1)

        @pl.when(j == 0)
        def _():
            u, _ = _rms(h_ref[...], gpre_ref[...])
            u_sc[...] = u.astype(BF16)
            acc_sc[...] = jnp.zeros_like(acc_sc)

        u = u_sc[...]
        gate = jnp.dot(u, wg_ref[...], preferred_element_type=F32)
        up = jnp.dot(u, wu_ref[...], preferred_element_type=F32)
        act = gate * _sigmoid(gate) * up
        gate_ref[...] = gate.astype(BF16)
        up_ref[...] = up.astype(BF16)
        acc_sc[...] += jnp.dot(act.astype(BF16), wd_ref[...], preferred_element_type=F32)

        @pl.when(j == n_slot - 1)
        def _():
            f = acc_sc[...]
            f_ref[...] = f
            n, _ = _rms(f, gpost_ref[...])
            hout_ref[...] = h_ref[...] + 0.5 * n

    tok = pl.BlockSpec((tm, d), lambda i, j: (i, 0))
    gain = pl.BlockSpec((None, 1, d), lambda i, j: (layer, 0, 0))
    slot_act = pl.BlockSpec((None, tm, fs), lambda i, j: (j, i, 0))
    return pl.pallas_call(
        body, name=name, grid=(tp // tm, n_slot),
        in_specs=[tok, gain, gain,
                  pl.BlockSpec((None, None, None, d, fs), lambda i, j: (j, layer, kind_gate, 0, 0)),
                  pl.BlockSpec((None, None, None, d, fs), lambda i, j: (j, layer, kind_up, 0, 0)),
                  pl.BlockSpec((None, None, None, fs, d), lambda i, j: (j, layer, kind_dn, 0, 0))],
        out_specs=[tok, slot_act, slot_act, tok],
        out_shape=[jax.ShapeDtypeStruct((tp, d), F32), jax.ShapeDtypeStruct((n_slot, tp, fs), BF16),
                   jax.ShapeDtypeStruct((n_slot, tp, fs), BF16), jax.ShapeDtypeStruct((tp, d), F32)],
        scratch_shapes=[pltpu.VMEM((tm, d), BF16), pltpu.VMEM((tm, d), F32)],
        compiler_params=_cparams(("parallel", "arbitrary")),
    )(h, g_pre, g_post, w_gu, w_gu, w_dn)


def _ffn_bwd(dh_out, h, f, gate, up, g_pre, g_post, w_gu, w_dn, layer, kind_gate, kind_up, kind_dn, tm, name):
    tp, d = h.shape
    n_slot, fs = w_gu.shape[0], w_gu.shape[-1]

    def body(dho_ref, h_ref, f_ref, gate_ref, up_ref, gpre_ref, gpost_ref, wg_ref, wu_ref, wd_ref,
             dh_ref, dgate_ref, dup_ref, act_ref, df_ref, u_ref, dgpre_ref, dgpost_ref, du_sc):
        i, j = pl.program_id(0), pl.program_id(1)

        @pl.when((i == 0) & (j == 0))
        def _():
            dgpre_ref[...] = jnp.zeros_like(dgpre_ref)
            dgpost_ref[...] = jnp.zeros_like(dgpost_ref)

        @pl.when(j == 0)
        def _():
            df, dg = _rms_bwd(f_ref[...], gpost_ref[...], 0.5 * dho_ref[...])
            df_ref[...] = df.astype(BF16)
            dgpost_ref[...] += dg
            u, _ = _rms(h_ref[...], gpre_ref[...])
            u_ref[...] = u.astype(BF16)
            du_sc[...] = jnp.zeros_like(du_sc)

        g = gate_ref[...].astype(F32)
        u_ = up_ref[...].astype(F32)
        sg = _sigmoid(g)
        silu = g * sg
        dact = _dot_nt(df_ref[...], wd_ref[...])
        dup = dact * silu
        dgate = dact * u_ * (sg * (1.0 + g * (1.0 - sg)))
        act_ref[...] = (silu * u_).astype(BF16)
        dup_b = dup.astype(BF16)
        dgate_b = dgate.astype(BF16)
        dup_ref[...] = dup_b
        dgate_ref[...] = dgate_b
        du_sc[...] += _dot_nt(dgate_b, wg_ref[...]) + _dot_nt(dup_b, wu_ref[...])

        @pl.when(j == n_slot - 1)
        def _():
            dx, dg = _rms_bwd(h_ref[...], gpre_ref[...], du_sc[...])
            dh_ref[...] = dho_ref[...] + dx
            dgpre_ref[...] += dg

    tok = pl.BlockSpec((tm, d), lambda i, j: (i, 0))
    gain = pl.BlockSpec((None, 1, d), lambda i, j: (layer, 0, 0))
    acc = pl.BlockSpec((1, d), lambda i, j: (0, 0))
    slot_act = pl.BlockSpec((None, tm, fs), lambda i, j: (j, i, 0))
    act_shape = jax.ShapeDtypeStruct((n_slot, tp, fs), BF16)
    return pl.pallas_call(
        body, name=name, grid=(tp // tm, n_slot),
        in_specs=[tok, tok, tok, slot_act, slot_act, gain, gain,
                  pl.BlockSpec((None, None, None, d, fs), lambda i, j: (j, layer, kind_gate, 0, 0)),
                  pl.BlockSpec((None, None, None, d, fs), lambda i, j: (j, layer, kind_up, 0, 0)),
                  pl.BlockSpec((None, None, None, fs, d), lambda i, j: (j, layer, kind_dn, 0, 0))],
        out_specs=[tok, slot_act, slot_act, slot_act, tok, tok, acc, acc],
        out_shape=[jax.ShapeDtypeStruct((tp, d), F32), act_shape, act_shape, act_shape,
                   jax.ShapeDtypeStruct((tp, d), BF16), jax.ShapeDtypeStruct((tp, d), BF16),
                   jax.ShapeDtypeStruct((1, d), F32), jax.ShapeDtypeStruct((1, d), F32)],
        scratch_shapes=[pltpu.VMEM((tm, d), F32)],
        compiler_params=_cparams(("arbitrary", "arbitrary")),
    )(dh_out, h, f, gate, up, g_pre, g_post, w_gu, w_gu, w_dn)


def _mm_tn(a, b, tk, out_dtype, name):
    ga = a.shape[0] if a.ndim == 3 else None
    gb = b.shape[0] if b.ndim == 3 else None
    groups = ga or gb or 1
    t, m = a.shape[-2:]
    n = b.shape[-1]
    nk = t // tk

    def body(a_ref, b_ref, o_ref, acc_sc):
        k = pl.program_id(1)

        @pl.when(k == 0)
        def _():
            acc_sc[...] = jnp.zeros_like(acc_sc)

        acc_sc[...] += _dot_tn(a_ref[...], b_ref[...])

        @pl.when(k == nk - 1)
        def _():
            o_ref[...] = acc_sc[...].astype(out_dtype)

    a_spec = (pl.BlockSpec((None, tk, m), lambda g, k: (g, k, 0)) if ga else pl.BlockSpec((tk, m), lambda g, k: (k, 0)))
    b_spec = (pl.BlockSpec((None, tk, n), lambda g, k: (g, k, 0)) if gb else pl.BlockSpec((tk, n), lambda g, k: (k, 0)))
    out = pl.pallas_call(
        body, name=name, grid=(groups, nk),
        in_specs=[a_spec, b_spec],
        out_specs=pl.BlockSpec((None, m, n), lambda g, k: (g, 0, 0)),
        out_shape=jax.ShapeDtypeStruct((groups, m, n), out_dtype),
        scratch_shapes=[pltpu.VMEM((m, n), F32)],
        compiler_params=_cparams(("parallel", "arbitrary")),
    )(a, b)
    return out if (ga or gb) else out[0]


def _mix_in_fwd(h, g_pre, w_parts, g_q, g_kv, w_q, w_k, w_v, cos_t, sin_t, rot, layer, tm, name):
    tp, d = h.shape
    w_xr, w_gr, w_cq, w_ckv, w_kr = w_parts
    lw, ql, kvl = w_xr.shape[-1], w_cq.shape[-1], w_ckv.shape[-1]
    nh, vd = w_v.shape[1], w_v.shape[-1]

    def body(h_ref, gpre_ref, wxr_ref, wgr_ref, wcq_ref, wckv_ref, wkr_ref, gq_ref, gkv_ref, wq_ref, wk_ref, wv_ref,
             cos_ref, sin_ref, rot_ref, xr_ref, gr_ref, cq_ref, ckv_ref, q_ref, k_ref, v_ref):
        u, _ = _rms(h_ref[...], gpre_ref[...])
        u = u.astype(BF16)
        xr_ref[...] = jnp.dot(u, wxr_ref[...], preferred_element_type=F32)
        gr_ref[...] = jnp.dot(u, wgr_ref[...], preferred_element_type=F32)
        cq = jnp.dot(u, wcq_ref[...], preferred_element_type=F32)
        ckv = jnp.dot(u, wckv_ref[...], preferred_element_type=F32)
        kr = jnp.dot(u, wkr_ref[...], preferred_element_type=F32)
        cq_ref[...] = cq
        ckv_ref[...] = ckv
        cos_b, sin_b, rot_b = cos_ref[...], sin_ref[...], rot_ref[...]
        qn = _rms(cq, gq_ref[...])[0].astype(BF16)
        kvn = _rms(ckv, gkv_ref[...])[0].astype(BF16)
        k_rope = _rope(kr, cos_b, sin_b, rot_b)
        for hd in range(nh):
            q_pre = jnp.dot(qn, wq_ref[hd], preferred_element_type=F32)
            q_ref[hd] = _rope(q_pre, cos_b, sin_b, rot_b).astype(BF16)
            k_ref[hd] = (jnp.dot(kvn, wk_ref[hd], preferred_element_type=F32) + k_rope).astype(BF16)
            v_ref[hd] = jnp.dot(kvn, wv_ref[hd], preferred_element_type=F32).astype(BF16)

    def tok(n):
        return pl.BlockSpec((tm, n), lambda i: (i, 0))

    def lay(*shape):
        return pl.BlockSpec((None,) + shape, lambda i: (layer,) + (0,) * len(shape))

    def heads(n):
        return pl.BlockSpec((nh, tm, n), lambda i: (0, i, 0))

    return pl.pallas_call(
        body, name=name, grid=(tp // tm,),
        in_specs=[tok(d), lay(1, d), lay(d, lw), lay(d, lw), lay(d, ql), lay(d, kvl), lay(d, QK_DIM), lay(1, ql), lay(1, kvl),
                  lay(nh, ql, QK_DIM), lay(nh, kvl, QK_DIM), lay(nh, kvl, vd), tok(QK_DIM), tok(QK_DIM),
                  pl.BlockSpec((QK_DIM, QK_DIM), lambda i: (0, 0))],
        out_specs=[tok(lw), tok(lw), tok(ql), tok(kvl), heads(QK_DIM), heads(QK_DIM), heads(vd)],
        out_shape=[jax.ShapeDtypeStruct((tp, lw), F32), jax.ShapeDtypeStruct((tp, lw), F32),
                   jax.ShapeDtypeStruct((tp, ql), F32), jax.ShapeDtypeStruct((tp, kvl), F32),
                   jax.ShapeDtypeStruct((nh, tp, QK_DIM), BF16), jax.ShapeDtypeStruct((nh, tp, QK_DIM), BF16),
                   jax.ShapeDtypeStruct((nh, tp, vd), BF16)],
        compiler_params=_cparams(("parallel",)),
    )(h, g_pre, w_xr, w_gr, w_cq, w_ckv, w_kr, g_q, g_kv, w_q, w_k, w_v, cos_t, sin_t, rot)


def _mix_in_bwd(dh_res, h, cq, ckv, d_xr, d_gr, dq, dk, dv, g_pre, w_parts, g_q, g_kv, w_q, w_k, w_v,
                cos_t, sin_t, rot_t, layer, tm, name):
    tp, d = h.shape
    w_xr, w_gr, w_cq, w_ckv, w_kr = w_parts
    lw, ql, kvl = w_xr.shape[-1], w_cq.shape[-1], w_ckv.shape[-1]
    nh, vd = w_v.shape[1], w_v.shape[-1]

    def body(dhr_ref, h_ref, cq_ref, ckv_ref, dxr_ref, dgr_ref, dq_ref, dk_ref, dv_ref, gpre_ref,
             wxr_ref, wgr_ref, wcq_ref, wckv_ref, wkr_ref, gq_ref, gkv_ref, wq_ref, wk_ref, wv_ref,
             cos_ref, sin_ref, rott_ref,
             dh_ref, u_ref, qn_ref, kvn_ref, dqpre_ref, dcq_ref, dckv_ref, dkr_ref, dgpre_ref, dgq_ref, dgkv_ref):
        i = pl.program_id(0)

        @pl.when(i == 0)
        def _():
            dgpre_ref[...] = jnp.zeros_like(dgpre_ref)
            dgq_ref[...] = jnp.zeros_like(dgq_ref)
            dgkv_ref[...] = jnp.zeros_like(dgkv_ref)

        cos_b, sin_b, rott_b = cos_ref[...], sin_ref[...], rott_ref[...]
        hh = h_ref[...]
        u, _ = _rms(hh, gpre_ref[...])
        u_ref[...] = u.astype(BF16)
        cq, ckv = cq_ref[...], ckv_ref[...]
        qn = _rms(cq, gq_ref[...])[0]
        kvn = _rms(ckv, gkv_ref[...])[0]
        qn_ref[...] = qn.astype(BF16)
        kvn_ref[...] = kvn.astype(BF16)
        d_qn = jnp.zeros((tm, ql), F32)
        d_kvn = jnp.zeros((tm, kvl), F32)
        d_krope = jnp.zeros((tm, QK_DIM), F32)
        for hd in range(nh):
            dq_h = dq_ref[hd]
            dq_pre = dq_h * cos_b + _dot_split(dq_h * sin_b, rott_b)
            dq_pre_b = dq_pre.astype(BF16)
            dqpre_ref[hd] = dq_pre_b
            d_qn += _dot_nt(dq_pre_b, wq_ref[hd])
            dk_h = dk_ref[hd]
            d_krope += dk_h
            d_kvn += _dot_nt(dk_h, wk_ref[hd]) + _dot_nt(dv_ref[hd], wv_ref[hd])
        d_kr = d_krope * cos_b + _dot_split(d_krope * sin_b, rott_b)
        d_cq, dgq = _rms_bwd(cq, gq_ref[...], d_qn)
        d_ckv, dgkv = _rms_bwd(ckv, gkv_ref[...], d_kvn)
        dgq_ref[...] += dgq
        dgkv_ref[...] += dgkv
        d_cq_b, d_ckv_b, d_kr_b = d_cq.astype(BF16), d_ckv.astype(BF16), d_kr.astype(BF16)
        dcq_ref[...] = d_cq_b
        dckv_ref[...] = d_ckv_b
        dkr_ref[...] = d_kr_b
        du = (_dot_nt(dxr_ref[...], wxr_ref[...]) + _dot_nt(dgr_ref[...], wgr_ref[...]) + _dot_nt(d_cq_b, wcq_ref[...])
              + _dot_nt(d_ckv_b, wckv_ref[...]) + _dot_nt(d_kr_b, wkr_ref[...]))
        dx, dg = _rms_bwd(hh, gpre_ref[...], du)
        dh_ref[...] = dhr_ref[...] + dx
        dgpre_ref[...] += dg

    def tok(n):
        return pl.BlockSpec((tm, n), lambda i: (i, 0))

    def lay(*shape):
        return pl.BlockSpec((None,) + shape, lambda i: (layer,) + (0,) * len(shape))

    def heads(n):
        return pl.BlockSpec((nh, tm, n), lambda i: (0, i, 0))

    def acc(n):
        return pl.BlockSpec((1, n), lambda i: (0, 0))

    return pl.pallas_call(
        body, name=name, grid=(tp // tm,),
        in_specs=[tok(d), tok(d), tok(ql), tok(kvl), tok(lw), tok(lw), heads(QK_DIM), heads(QK_DIM), heads(vd), lay(1, d),
                  lay(d, lw), lay(d, lw), lay(d, ql), lay(d, kvl), lay(d, QK_DIM), lay(1, ql), lay(1, kvl),
                  lay(nh, ql, QK_DIM), lay(nh, kvl, QK_DIM), lay(nh, kvl, vd), tok(QK_DIM), tok(QK_DIM),
                  pl.BlockSpec((QK_DIM, QK_DIM), lambda i: (0, 0))],
        out_specs=[tok(d), tok(d), tok(ql), tok(kvl), heads(QK_DIM), tok(ql), tok(kvl), tok(QK_DIM), acc(d), acc(ql), acc(kvl)],
        out_shape=[jax.ShapeDtypeStruct((tp, d), F32), jax.ShapeDtypeStruct((tp, d), BF16),
                   jax.ShapeDtypeStruct((tp, ql), BF16), jax.ShapeDtypeStruct((tp, kvl), BF16),
                   jax.ShapeDtypeStruct((nh, tp, QK_DIM), BF16), jax.ShapeDtypeStruct((tp, ql), BF16),
                   jax.ShapeDtypeStruct((tp, kvl), BF16), jax.ShapeDtypeStruct((tp, QK_DIM), BF16),
                   jax.ShapeDtypeStruct((1, d), F32), jax.ShapeDtypeStruct((1, ql), F32), jax.ShapeDtypeStruct((1, kvl), F32)],
        compiler_params=_cparams(("arbitrary",)),
    )(dh_res, h, cq, ckv, d_xr, d_gr, dq, dk, dv, g_pre, w_xr, w_gr, w_cq, w_ckv, w_kr, g_q, g_kv, w_q, w_k, w_v,
      cos_t, sin_t, rot_t)


def _lru_gates(xc, wa_ref, wx_ref, ba, bx, sp):
    xcb = xc.astype(BF16)
    r = _sigmoid(jnp.dot(xcb, wa_ref[...], preferred_element_type=F32) + ba)
    ig = _sigmoid(jnp.dot(xcb, wx_ref[...], preferred_element_type=F32) + bx)
    log_a = -LRU_C * r * sp
    a = jnp.exp(log_a)
    sq = jnp.sqrt(_neg_expm1(2.0 * log_a))
    return r, ig, a, sq


def _softplus(x):
    return jnp.maximum(x, 0.0) + jnp.log(1.0 + jnp.exp(-jnp.abs(x)))


def _lru_fwd(xr, gr, conv_w, conv_b, wa_bd, wx_bd, b_a, b_x, lam, layer, tc, name):
    tp, w = xr.shape
    pad = 8

    def body(xr_ref, gr_ref, cw_ref, cb_ref, wa_ref, wx_ref, ba_ref, bx_ref, lam_ref, y_ref, xc_ref, hs_ref,
             xe_sc, a_sc, b_sc, st_sc):
        c = pl.program_id(0)

        @pl.when(c == 0)
        def _():
            xe_sc[pl.ds(0, pad), :] = jnp.zeros((pad, w), F32)
            st_sc[...] = jnp.zeros_like(st_sc)

        xe_sc[pl.ds(pad, tc), :] = xr_ref[...]
        xc = cb_ref[...] + xe_sc[pl.ds(pad, tc), :] * cw_ref[pl.ds(CONV_WIDTH - 1, 1), :]
        for k in range(CONV_WIDTH - 1):
            xc = xc + xe_sc[pl.ds(pad - (CONV_WIDTH - 1) + k, tc), :] * cw_ref[pl.ds(k, 1), :]
        xe_sc[pl.ds(0, pad), :] = xe_sc[pl.ds(tc, pad), :]
        xc_ref[...] = xc
        sp = _softplus(-lam_ref[...])
        _, ig, a, sq = _lru_gates(xc, wa_ref, wx_ref, ba_ref[...], bx_ref[...], sp)
        a_sc[...] = a
        b_sc[...] = sq * (ig * xc)

        def step(t, hcur):
            hnew = a_sc[pl.ds(t, 1), :] * hcur + b_sc[pl.ds(t, 1), :]
            hs_ref[pl.ds(t, 1), :] = hnew
            return hnew

        st_sc[...] = lax.fori_loop(0, tc, step, st_sc[...], unroll=8)
        y_ref[...] = hs_ref[...] * _gelu(gr_ref[...])[0]

    tok = pl.BlockSpec((tc, w), lambda c: (c, 0))

    def lay(*shape):
        return pl.BlockSpec((None,) + shape, lambda c: (layer,) + (0,) * len(shape))

    out = jax.ShapeDtypeStruct((tp, w), F32)
    return pl.pallas_call(
        body, name=name, grid=(tp // tc,),
        in_specs=[tok, tok, lay(CONV_WIDTH, w), lay(1, w), lay(w, w), lay(w, w), lay(1, w), lay(1, w), lay(1, w)],
        out_specs=[tok, tok, tok], out_shape=[out, out, out],
        scratch_shapes=[pltpu.VMEM((tc + pad, w), F32), pltpu.VMEM((tc, w), F32), pltpu.VMEM((tc, w), F32),
                        pltpu.VMEM((1, w), F32)],
        compiler_params=_cparams(("arbitrary",)),
    )(xr, gr, conv_w, conv_b, wa_bd, wx_bd, b_a, b_x, lam)


def _lru_bwd(dy, xr, gr, xc, hs, conv_w, wa_bd, wx_bd, b_a, b_x, lam, layer, tc, name):
    tp, w = xr.shape
    pad = 8
    nc = tp // tc
    per = tc // pad

    def body(dy_ref, xr_ref, gr_ref, xc_ref, hs_ref, xrp_ref, hsp_ref, cw_ref, wa_ref, wx_ref, ba_ref, bx_ref, lam_ref,
             dxr_ref, dgr_ref, dcw_ref, dcb_ref, dwa_ref, dwx_ref, dba_ref, dbx_ref, dlam_ref,
             a_sc, dh_sc, dxc_sc, he_sc, xe_sc, carry_sc):
        s = pl.program_id(0)
        first_chunk = s == nc - 1

        @pl.when(s == 0)
        def _():
            for ref in (dcw_ref, dcb_ref, dwa_ref, dwx_ref, dba_ref, dbx_ref, dlam_ref):
                ref[...] = jnp.zeros_like(ref)
            carry_sc[...] = jnp.zeros_like(carry_sc)
            dxc_sc[pl.ds(tc, pad), :] = jnp.zeros((pad, w), F32)

        keep = jnp.where(first_chunk, 0.0, 1.0)
        he_sc[pl.ds(0, pad), :] = hsp_ref[...] * keep
        he_sc[pl.ds(pad, tc), :] = hs_ref[...]
        xe_sc[pl.ds(0, pad), :] = xrp_ref[...] * keep
        xe_sc[pl.ds(pad, tc), :] = xr_ref[...]

        lam_v = lam_ref[...]
        sp = _softplus(-lam_v)
        xc = xc_ref[...]
        r, ig, a, sq = _lru_gates(xc, wa_ref, wx_ref, ba_ref[...], bx_ref[...], sp)
        a_sc[...] = a
        grv = gr_ref[...]
        gl, th = _gelu(grv)
        dyv = dy_ref[...]
        dgr_ref[...] = dyv * hs_ref[...] * _gelu_grad(grv, th)
        dh_sc[...] = dyv * gl

        def step(n, g):
            t = tc - 1 - n
            dh = dh_sc[pl.ds(t, 1), :] + g
            dh_sc[pl.ds(t, 1), :] = dh
            return a_sc[pl.ds(t, 1), :] * dh

        carry_sc[...] = lax.fori_loop(0, tc, step, carry_sc[...], unroll=8)

        dh = dh_sc[...]
        d_a = dh * he_sc[pl.ds(pad - 1, tc), :]
        d_ixc = dh * sq
        d_sq = dh * (ig * xc)
        a2 = a * a
        d_la = d_a * a - d_sq * a2 / sq
        d_r = d_la * (-LRU_C * sp)
        d_sp = jnp.sum(d_la * (-LRU_C * r), axis=0, keepdims=True)
        dlam_ref[...] += d_sp * (-_sigmoid(-lam_v))
        d_pa = d_r * r * (1.0 - r)
        d_px = d_ixc * xc * ig * (1.0 - ig)
        dba_ref[...] += jnp.sum(d_pa, axis=0, keepdims=True)
        dbx_ref[...] += jnp.sum(d_px, axis=0, keepdims=True)
        d_pa_b, d_px_b = d_pa.astype(BF16), d_px.astype(BF16)
        xcb = xc.astype(BF16)
        dwa_ref[...] += _dot_tn(xcb, d_pa_b)
        dwx_ref[...] += _dot_tn(xcb, d_px_b)
        d_xc = d_ixc * ig + _dot_nt(d_pa_b, wa_ref[...]) + _dot_nt(d_px_b, wx_ref[...])
        dcb_ref[...] += jnp.sum(d_xc, axis=0, keepdims=True)
        dxc_sc[pl.ds(0, tc), :] = d_xc
        d_xr = jnp.zeros((tc, w), F32)
        for k in range(CONV_WIDTH):
            off = CONV_WIDTH - 1 - k
            d_xr = d_xr + dxc_sc[pl.ds(off, tc), :] * cw_ref[pl.ds(k, 1), :]
            dcw_ref[pl.ds(k, 1), :] += jnp.sum(d_xc * xe_sc[pl.ds(pad - off, tc), :], axis=0, keepdims=True)
        dxr_ref[...] = d_xr
        dxc_sc[pl.ds(tc, pad), :] = dxc_sc[pl.ds(0, pad), :]

    def rev(cidx):
        return nc - 1 - cidx

    tok = pl.BlockSpec((tc, w), lambda c: (rev(c), 0))
    prev = pl.BlockSpec((pad, w), lambda c: (jnp.maximum(rev(c) * per - 1, 0), 0))

    def lay(*shape):
        return pl.BlockSpec((None,) + shape, lambda c: (layer,) + (0,) * len(shape))

    def acc(*shape):
        return pl.BlockSpec(shape, lambda c: (0,) * len(shape))

    big = jax.ShapeDtypeStruct((tp, w), F32)
    vec = jax.ShapeDtypeStruct((1, w), F32)
    return pl.pallas_call(
        body, name=name, grid=(nc,),
        in_specs=[tok, tok, tok, tok, tok, prev, prev, lay(CONV_WIDTH, w), lay(w, w), lay(w, w), lay(1, w), lay(1, w), lay(1, w)],
        out_specs=[tok, tok, acc(CONV_WIDTH, w), acc(1, w), acc(w, w), acc(w, w), acc(1, w), acc(1, w), acc(1, w)],
        out_shape=[big, big, jax.ShapeDtypeStruct((CONV_WIDTH, w), F32), vec, jax.ShapeDtypeStruct((w, w), F32),
                   jax.ShapeDtypeStruct((w, w), F32), vec, vec, vec],
        scratch_shapes=[pltpu.VMEM((tc, w), F32), pltpu.VMEM((tc, w), F32), pltpu.VMEM((tc + pad, w), F32),
                        pltpu.VMEM((tc + pad, w), F32), pltpu.VMEM((tc + pad, w), F32), pltpu.VMEM((1, w), F32)],
        compiler_params=_cparams(("arbitrary",)),
    )(dy, xr, gr, xc, hs, xr, hs, conv_w, wa_bd, wx_bd, b_a, b_x, lam)


def _attn_fwd(q, k, v, tq, name):
    nh, tp, dqk = q.shape
    vd = v.shape[-1]
    scale = dqk ** -0.5

    def body(q_ref, k_ref, v_ref, o_ref, lse_ref):
        i = pl.program_id(1)
        qb = q_ref[...]
        qpos = i * tq + lax.broadcasted_iota(jnp.int32, (tq, tq), 0)
        kidx = lax.broadcasted_iota(jnp.int32, (tq, tq), 1)

        def step(j, carry):
            m, l, acc = carry
            off = pl.multiple_of(j * tq, tq)
            kb = k_ref[pl.ds(off, tq), :]
            vb = v_ref[pl.ds(off, tq), :]
            s = _dot_nt(qb, kb) * scale
            s = jnp.where(kidx + j * tq <= qpos, s, NEG_BIG)
            m_new = jnp.maximum(m, jnp.max(s, axis=-1, keepdims=True))
            p = jnp.exp(s - m_new)
            alpha = jnp.exp(m - m_new)
            l = alpha * l + jnp.sum(p, axis=-1, keepdims=True)
            acc = alpha * acc + jnp.dot(p.astype(BF16), vb, preferred_element_type=F32)
            return m_new, l, acc

        init = (jnp.full((tq, 1), NEG_BIG, F32), jnp.zeros((tq, 1), F32), jnp.zeros((tq, vd), F32))
        m, l, acc = lax.fori_loop(0, i + 1, step, init)
        o_ref[...] = acc / l
        lse_ref[...] = m + jnp.log(l)

    return pl.pallas_call(
        body, name=name, grid=(nh, tp // tq),
        in_specs=[pl.BlockSpec((None, tq, dqk), lambda h, i: (h, i, 0)),
                  pl.BlockSpec((None, tp, dqk), lambda h, i: (h, 0, 0)),
                  pl.BlockSpec((None, tp, vd), lambda h, i: (h, 0, 0))],
        out_specs=[pl.BlockSpec((None, tq, vd), lambda h, i: (h, i, 0)),
                   pl.BlockSpec((None, tq, 1), lambda h, i: (h, i, 0))],
        out_shape=[jax.ShapeDtypeStruct((nh, tp, vd), F32), jax.ShapeDtypeStruct((nh, tp, 1), F32)],
        compiler_params=_cparams(("parallel", "parallel")),
    )(q, k, v)


def _attn_bwd(q, k, v, o, do, lse, tq, name):
    nh, tp, dqk = q.shape
    vd = v.shape[-1]
    scale = dqk ** -0.5
    nq = tp // tq

    def body(q_ref, k_ref, v_ref, o_ref, do_ref, lse_ref, dq_ref, dk_ref, dv_ref):
        kb_i = pl.program_id(1)

        @pl.when(kb_i == 0)
        def _():
            dq_ref[...] = jnp.zeros_like(dq_ref)

        kb = k_ref[...]
        vb = v_ref[...]
        kpos = kb_i * tq + lax.broadcasted_iota(jnp.int32, (tq, tq), 1)
        qidx = lax.broadcasted_iota(jnp.int32, (tq, tq), 0)

        def step(qi, carry):
            dk, dv = carry
            off = pl.multiple_of(qi * tq, tq)
            qb = q_ref[pl.ds(off, tq), :]
            dob = do_ref[pl.ds(off, tq), :]
            ob = o_ref[pl.ds(off, tq), :]
            s = _dot_nt(qb, kb) * scale
            s = jnp.where(kpos <= qidx + qi * tq, s, NEG_BIG)
            p = jnp.exp(s - lse_ref[pl.ds(off, tq), :])
            dob_b = dob.astype(BF16)
            dp = _dot_nt(dob_b, vb)
            delta = jnp.sum(dob * ob, axis=-1, keepdims=True)
            ds = (p * (dp - delta) * scale).astype(BF16)
            dv = dv + _dot_tn(p, dob_b)
            dk = dk + _dot_tn(ds, qb)
            dq_ref[pl.ds(off, tq), :] += jnp.dot(ds, kb, preferred_element_type=F32)
            return dk, dv

        dk, dv = lax.fori_loop(kb_i, nq, step, (jnp.zeros((tq, dqk), F32), jnp.zeros((tq, vd), F32)))
        dk_ref[...] = dk
        dv_ref[...] = dv

    def full(n):
        return pl.BlockSpec((None, tp, n), lambda h, j: (h, 0, 0))

    def blk(n):
        return pl.BlockSpec((None, tq, n), lambda h, j: (h, j, 0))

    return pl.pallas_call(
        body, name=name, grid=(nh, nq),
        in_specs=[full(dqk), blk(dqk), blk(vd), full(vd), full(vd), full(1)],
        out_specs=[full(dqk), blk(dqk), blk(vd)],
        out_shape=[jax.ShapeDtypeStruct((nh, tp, dqk), F32), jax.ShapeDtypeStruct((nh, tp, dqk), F32),
                   jax.ShapeDtypeStruct((nh, tp, vd), F32)],
        compiler_params=_cparams(("parallel", "arbitrary")),
    )(q, k, v, o, do, lse)


def _mix_out_fwd(h, y_lru, o, g_lo, g_mo, w_lo, w_mo, g_post, layer, tm, name):
    tp, d = h.shape
    lw = y_lru.shape[-1]
    nh, vd = o.shape[0], o.shape[-1]

    def body(h_ref, yl_ref, o_ref, glo_ref, gmo_ref, wlo_ref, wmo_ref, gpost_ref, hout_ref, y_ref):
        nl, _ = _rms(yl_ref[...], glo_ref[...])
        y = _dot(nl, wlo_ref[...])
        ssq = jnp.zeros((tm, 1), F32)
        for hd in range(nh):
            oh = o_ref[hd]
            ssq += jnp.sum(oh * oh, axis=-1, keepdims=True)
        r = lax.rsqrt(ssq / (nh * vd) + EPS)
        for hd in range(nh):
            y += _dot(o_ref[hd] * r * gmo_ref[hd], wmo_ref[hd])
        y_ref[...] = y
        hout_ref[...] = h_ref[...] + _rms(y, gpost_ref[...])[0]

    tok = pl.BlockSpec((tm, d), lambda i: (i, 0))

    def lay(*shape):
        return pl.BlockSpec((None,) + shape, lambda i: (layer,) + (0,) * len(shape))

    return pl.pallas_call(
        body, name=name, grid=(tp // tm,),
        in_specs=[tok, pl.BlockSpec((tm, lw), lambda i: (i, 0)), pl.BlockSpec((nh, tm, vd), lambda i: (0, i, 0)),
                  lay(1, lw), lay(nh, 1, vd), lay(lw, d), lay(nh, vd, d), lay(1, d)],
        out_specs=[tok, tok],
        out_shape=[jax.ShapeDtypeStruct((tp, d), F32), jax.ShapeDtypeStruct((tp, d), F32)],
        compiler_params=_cparams(("parallel",)),
    )(h, y_lru, o, g_lo, g_mo, w_lo, w_mo, g_post)


def _mix_out_bwd(dh_out, y, y_lru, o, g_lo, g_mo, w_lo, w_mo, g_post, layer, tm, name):
    tp, d = y.shape
    lw = y_lru.shape[-1]
    nh, vd = o.shape[0], o.shape[-1]

    def body(dho_ref, y_ref, yl_ref, o_ref, glo_ref, gmo_ref, wlo_ref, wmo_ref, gpost_ref,
             dyl_ref, do_ref, dy_ref, nl_ref, nm_ref, dgpost_ref, dglo_ref, dgmo_ref):
        i = pl.program_id(0)

        @pl.when(i == 0)
        def _():
            dgpost_ref[...] = jnp.zeros_like(dgpost_ref)
            dglo_ref[...] = jnp.zeros_like(dglo_ref)
            dgmo_ref[...] = jnp.zeros_like(dgmo_ref)

        dy, dgp = _rms_bwd(y_ref[...], gpost_ref[...], dho_ref[...])
        dgpost_ref[...] += dgp
        dy_b = dy.astype(BF16)
        dy_ref[...] = dy_b
        yl = yl_ref[...]
        nl_ref[...] = _rms(yl, glo_ref[...])[0].astype(BF16)
        dyl, dgl = _rms_bwd(yl, glo_ref[...], _dot_nt(dy_b, wlo_ref[...]))
        dyl_ref[...] = dyl
        dglo_ref[...] += dgl
        ssq = jnp.zeros((tm, 1), F32)
        for hd in range(nh):
            oh = o_ref[hd]
            ssq += jnp.sum(oh * oh, axis=-1, keepdims=True)
        r = lax.rsqrt(ssq / (nh * vd) + EPS)
        dn = []
        dot_sum = jnp.zeros((tm, 1), F32)
        for hd in range(nh):
            oh = o_ref[hd] * r
            nm_ref[hd] = (oh * gmo_ref[hd]).astype(BF16)
            dn_h = _dot_nt(dy_b, wmo_ref[hd])
            dgmo_ref[hd] += jnp.sum(dn_h * oh, axis=0, keepdims=True)
            dn_h = dn_h * gmo_ref[hd]
            dot_sum += jnp.sum(dn_h * oh, axis=-1, keepdims=True)
            dn.append(dn_h)
        dot_mean = dot_sum / (nh * vd)
        for hd in range(nh):
            do_ref[hd] = r * (dn[hd] - o_ref[hd] * r * dot_mean)

    tok = pl.BlockSpec((tm, d), lambda i: (i, 0))
    tokl = pl.BlockSpec((tm, lw), lambda i: (i, 0))
    heads = pl.BlockSpec((nh, tm, vd), lambda i: (0, i, 0))

    def lay(*shape):
        return pl.BlockSpec((None,) + shape, lambda i: (layer,) + (0,) * len(shape))

    def acc(*shape):
        return pl.BlockSpec(shape, lambda i: (0,) * len(shape))

    return pl.pallas_call(
        body, name=name, grid=(tp // tm,),
        in_specs=[tok, tok, tokl, heads, lay(1, lw), lay(nh, 1, vd), lay(lw, d), lay(nh, vd, d), lay(1, d)],
        out_specs=[tokl, heads, tok, tokl, heads, acc(1, d), acc(1, lw), acc(nh, 1, vd)],
        out_shape=[jax.ShapeDtypeStruct((tp, lw), F32), jax.ShapeDtypeStruct((nh, tp, vd), F32),
                   jax.ShapeDtypeStruct((tp, d), BF16), jax.ShapeDtypeStruct((tp, lw), BF16),
                   jax.ShapeDtypeStruct((nh, tp, vd), BF16), jax.ShapeDtypeStruct((1, d), F32),
                   jax.ShapeDtypeStruct((1, lw), F32), jax.ShapeDtypeStruct((nh, 1, vd), F32)],
        compiler_params=_cparams(("arbitrary",)),
    )(dh_out, y, y_lru, o, g_lo, g_mo, w_lo, w_mo, g_post)


def _loss_head(h, target, n_real, tm, name):
    tp, d = h.shape
    nt = tp // tm

    def body(h_ref, t_ref, dh_ref, part_ref):
        i = pl.program_id(0)
        row = i * tm + lax.broadcasted_iota(jnp.int32, (tm, 1), 0)
        real = (row >= N_META) & (row < N_META + n_real)
        err = jnp.where(real, h_ref[...] - t_ref[...], 0.0)
        dh_ref[...] = err * (1.0 / d)
        part_ref[...] = jnp.full((1, 1, 128), 0.5 / d * jnp.sum(err * err), F32)

    tok = pl.BlockSpec((tm, d), lambda i: (i, 0))
    return pl.pallas_call(
        body, name=name, grid=(nt,), in_specs=[tok, tok],
        out_specs=[tok, pl.BlockSpec((1, 1, 128), lambda i: (i, 0, 0))],
        out_shape=[jax.ShapeDtypeStruct((tp, d), F32), jax.ShapeDtypeStruct((nt, 1, 128), F32)],
        compiler_params=_cparams(("parallel",)),
    )(h, target)


def _mesh_pos():
    return lax.axis_index("x"), lax.axis_index("y"), lax.axis_index("c")


def _other_chips(x, y):
    return [(1 - x, y), (x, 1 - y), (1 - x, 1 - y)]


_ANY = pl.BlockSpec(memory_space=pl.ANY)


def _allgather_chips(arrays, name):
    n = len(arrays)

    def body(*refs):
        ins, outs = refs[:n], refs[n:2 * n]
        send_sems, recv_sems, loc_sems = refs[2 * n:]
        x, y, c = _mesh_pos()
        me = 2 * x + y
        chips = _other_chips(x, y)
        local, sends = [], []
        for a in range(n):
            cp = pltpu.make_async_copy(ins[a], outs[a].at[me], loc_sems.at[a])
            cp.start()
            local.append(cp)
            for p, (px, py) in enumerate(chips):
                rc = pltpu.make_async_remote_copy(src_ref=ins[a], dst_ref=outs[a].at[me], send_sem=send_sems.at[a, p],
                                                  recv_sem=recv_sems.at[a, p], device_id=(px, py, c), device_id_type=MESH)
                rc.start()
                sends.append(rc)
        for a in range(n):
            for p, (px, py) in enumerate(chips):
                pltpu.make_async_remote_copy(src_ref=ins[a], dst_ref=outs[a].at[2 * px + py], send_sem=send_sems.at[a, p],
                                             recv_sem=recv_sems.at[a, p], device_id=(px, py, c), device_id_type=MESH).wait_recv()
        for rc in sends:
            rc.wait_send()
        for cp in local:
            cp.wait()

    return pl.pallas_call(
        body, name=name, in_specs=[_ANY] * n, out_specs=[_ANY] * n,
        out_shape=[jax.ShapeDtypeStruct((N_CHIPS,) + a.shape, a.dtype) for a in arrays],
        scratch_shapes=[pltpu.SemaphoreType.DMA((n, 3)), pltpu.SemaphoreType.DMA((n, 3)), pltpu.SemaphoreType.DMA((n,))],
    )(*arrays)


def _swap_halves(arrays, name):
    n = len(arrays)

    def body(*refs):
        ins, outs = refs[:n], refs[n:2 * n]
        send_sems, recv_sems = refs[2 * n:]
        x, y, c = _mesh_pos()
        copies = []
        for a in range(n):
            rc = pltpu.make_async_remote_copy(src_ref=ins[a].at[1 - c], dst_ref=outs[a], send_sem=send_sems.at[a],
                                              recv_sem=recv_sems.at[a], device_id=(x, y, 1 - c), device_id_type=MESH)
            rc.start()
            copies.append(rc)
        for rc in copies:
            rc.wait()

    return pl.pallas_call(
        body, name=name, in_specs=[_ANY] * n, out_specs=[_ANY] * n,
        out_shape=[jax.ShapeDtypeStruct(a.shape[1:], a.dtype) for a in arrays],
        scratch_shapes=[pltpu.SemaphoreType.DMA((n,)), pltpu.SemaphoreType.DMA((n,))],
    )(*arrays)


def _scatter_chips(arrays, name):
    n = len(arrays)

    def body(*refs):
        ins, outs = refs[:n], refs[n:2 * n]
        send_sems, recv_sems, loc_sems = refs[2 * n:]
        x, y, c = _mesh_pos()
        me = 2 * x + y
        chips = _other_chips(x, y)
        local, sends = [], []
        for a in range(n):
            cp = pltpu.make_async_copy(ins[a].at[me], outs[a].at[me], loc_sems.at[a])
            cp.start()
            local.append(cp)
            for p, (px, py) in enumerate(chips):
                rc = pltpu.make_async_remote_copy(src_ref=ins[a].at[2 * px + py], dst_ref=outs[a].at[me],
                                                  send_sem=send_sems.at[a, p], recv_sem=recv_sems.at[a, p],
                                                  device_id=(px, py, c), device_id_type=MESH)
                rc.start()
                sends.append(rc)
        for a in range(n):
            for p, (px, py) in enumerate(chips):
                pltpu.make_async_remote_copy(src_ref=ins[a].at[me], dst_ref=outs[a].at[2 * px + py], send_sem=send_sems.at[a, p],
                                             recv_sem=recv_sems.at[a, p], device_id=(px, py, c), device_id_type=MESH).wait_recv()
        for rc in sends:
            rc.wait_send()
        for cp in local:
            cp.wait()

    return pl.pallas_call(
        body, name=name, in_specs=[_ANY] * n, out_specs=[_ANY] * n,
        out_shape=[jax.ShapeDtypeStruct(a.shape, a.dtype) for a in arrays],
        scratch_shapes=[pltpu.SemaphoreType.DMA((n, 3)), pltpu.SemaphoreType.DMA((n, 3)), pltpu.SemaphoreType.DMA((n,))],
    )(*arrays)


def _join_halves(arrays, name):
    n = len(arrays)

    def body(*refs):
        ins, outs = refs[:n], refs[n:2 * n]
        send_sems, recv_sems, loc_sems = refs[2 * n:]
        x, y, c = _mesh_pos()
        local, copies = [], []
        for a in range(n):
            cp = pltpu.make_async_copy(ins[a], outs[a].at[c], loc_sems.at[a])
            cp.start()
            local.append(cp)
            rc = pltpu.make_async_remote_copy(src_ref=ins[a], dst_ref=outs[a].at[c], send_sem=send_sems.at[a],
                                              recv_sem=recv_sems.at[a], device_id=(x, y, 1 - c), device_id_type=MESH)
            rc.start()
            copies.append(rc)
        for rc in copies:
            rc.wait()
        for cp in local:
            cp.wait()

    return pl.pallas_call(
        body, name=name, in_specs=[_ANY] * n, out_specs=[_ANY] * n,
        out_shape=[jax.ShapeDtypeStruct((2,) + a.shape, a.dtype) for a in arrays],
        scratch_shapes=[pltpu.SemaphoreType.DMA((n,)), pltpu.SemaphoreType.DMA((n,)), pltpu.SemaphoreType.DMA((n,))],
    )(*arrays)


def _allreduce_small(buf, name):
    rows, lanes = buf.shape
    n_dev = 8

    def body(in_ref, out_ref, gather, send_sems, recv_sems):
        x, y, c = _mesh_pos()
        me = 4 * x + 2 * y + c
        gather[me] = in_ref[...]
        sends = []
        for rel in range(1, n_dev):
            fx, fy, fc = (rel >> 2) & 1, (rel >> 1) & 1, rel & 1
            peer = (1 - x if fx else x, 1 - y if fy else y, 1 - c if fc else c)
            rc = pltpu.make_async_remote_copy(src_ref=in_ref, dst_ref=gather.at[me], send_sem=send_sems.at[rel - 1],
                                              recv_sem=recv_sems.at[rel - 1], device_id=peer, device_id_type=MESH)
            rc.start()
            sends.append(rc)
        for rc in sends:
            rc.wait()
        total = gather[0]
        for k in range(1, n_dev):
            total = total + gather[k]
        out_ref[...] = total

    vm = pl.BlockSpec(memory_space=pltpu.VMEM)
    return pl.pallas_call(
        body, name=name, in_specs=[vm], out_specs=vm, out_shape=jax.ShapeDtypeStruct((rows, lanes), F32),
        scratch_shapes=[pltpu.VMEM((n_dev, rows, lanes), F32), pltpu.SemaphoreType.DMA((n_dev - 1,)),
                        pltpu.SemaphoreType.DMA((n_dev - 1,))],
        compiler_params=pltpu.CompilerParams(vmem_limit_bytes=V7X_VMEM_LIMIT_BYTES),
    )(buf)


def _row_tile(rows, limit=512):
    best = None
    for t in range(16, min(rows, limit) + 1, 16):
        if rows % t == 0:
            best = t
    return best if best is not None else rows


def _add_halves(mine2, theirs, core, name):
    _, ns, r, ccol = mine2.shape
    tr = _row_tile(r)

    def body(c_ref, a_ref, b_ref, o_ref):
        o_ref[...] = (a_ref[...].astype(F32) + b_ref[...].astype(F32)).astype(o_ref.dtype)

    grid_spec = pltpu.PrefetchScalarGridSpec(
        num_scalar_prefetch=1, grid=(ns, r // tr),
        in_specs=[pl.BlockSpec((None, None, tr, ccol), lambda s, i, c_ref: (c_ref[0], s, i, 0)),
                  pl.BlockSpec((None, tr, ccol), lambda s, i, c_ref: (s, i, 0))],
        out_specs=pl.BlockSpec((None, tr, ccol), lambda s, i, c_ref: (s, i, 0)))
    return pl.pallas_call(body, name=name, grid_spec=grid_spec, out_shape=jax.ShapeDtypeStruct((ns, r, ccol), mine2.dtype),
                          compiler_params=_cparams(("parallel", "parallel")))(core, mine2, theirs)


def _sum_chips(parts, name):
    ns, r, ccol = parts.shape
    tr = _row_tile(r)

    def body(p_ref, o_ref):
        total = p_ref[0].astype(F32)
        for k in range(1, ns):
            total = total + p_ref[k].astype(F32)
        o_ref[...] = total

    return pl.pallas_call(
        body, name=name, grid=(r // tr,), in_specs=[pl.BlockSpec((ns, tr, ccol), lambda i: (0, i, 0))],
        out_specs=pl.BlockSpec((tr, ccol), lambda i: (i, 0)), out_shape=jax.ShapeDtypeStruct((r, ccol), F32),
        compiler_params=_cparams(("parallel",)))(parts)


def _adamw_math(w, g, m, v):
    m = ADAM_B1 * m + (1.0 - ADAM_B1) * g
    v = ADAM_B2 * v + (1.0 - ADAM_B2) * (g * g)
    m_hat = m / (1.0 - ADAM_B1 ** ADAM_STEP)
    v_hat = v / (1.0 - ADAM_B2 ** ADAM_STEP)
    delta = -ADAM_LR * (m_hat / (jnp.sqrt(v_hat) + ADAM_EPS) + ADAM_WD * w)
    return delta, m, v


def _adamw_sharded(w, m, v, g_all, kind, name):
    nl, r, ccol = w.shape
    tr = _row_tile(r)

    def body(w_ref, m_ref, v_ref, g_ref, go_ref, d_ref, mo_ref, vo_ref):
        g = g_ref[...]
        go_ref[...] = g
        d_ref[...], mo_ref[...], vo_ref[...] = _adamw_math(w_ref[...], g, m_ref[...], v_ref[...])

    blk = pl.BlockSpec((None, tr, ccol), lambda l, i: (l, i, 0))
    out = jax.ShapeDtypeStruct((nl, r, ccol), F32)
    return pl.pallas_call(
        body, name=name, grid=(nl, r // tr),
        in_specs=[blk, blk, blk, pl.BlockSpec((None, None, tr, ccol), lambda l, i: (l, kind, i, 0))],
        out_specs=[blk, blk, blk, blk], out_shape=[out, out, out, out],
        compiler_params=_cparams(("parallel", "parallel")))(w, m, v, g_all)


def _adamw_whole(w, g, m, v, name):
    def body(w_ref, g_ref, m_ref, v_ref, d_ref, mo_ref, vo_ref):
        d_ref[...], mo_ref[...], vo_ref[...] = _adamw_math(w_ref[...], g_ref[...], m_ref[...], v_ref[...])

    out = jax.ShapeDtypeStruct(w.shape, F32)
    return pl.pallas_call(body, name=name, out_shape=[out, out, out],
                          compiler_params=pltpu.CompilerParams(vmem_limit_bytes=V7X_VMEM_LIMIT_BYTES))(w, g, m, v)


_W_NAMES = ['meta_tokens', 'ffn1_pre_g', 'ffn1_w_gate', 'ffn1_w_up', 'ffn1_w_down', 'ffn1_post_g', 'mix_pre_g', 'w_in',
            'lru_conv_w', 'lru_conv_b', 'lru_w_a', 'lru_b_a', 'lru_w_x', 'lru_b_x', 'lru_lambda', 'mla_q_norm_g', 'mla_w_uq',
            'mla_kv_norm_g', 'mla_w_ukv', 'lru_out_g', 'mla_out_g', 'w_out', 'mix_post_g', 'ffn2_pre_g', 'ffn2_w_gate',
            'ffn2_w_up', 'ffn2_w_down', 'ffn2_post_g']
_REPLICATED = ['ffn1_pre_g', 'ffn1_post_g', 'mix_pre_g', 'lru_conv_b', 'lru_w_a', 'lru_b_a', 'lru_w_x', 'lru_b_x', 'lru_lambda',
               'mla_q_norm_g', 'mla_kv_norm_g', 'lru_out_g', 'mla_out_g', 'mix_post_g', 'ffn2_pre_g', 'ffn2_post_g']
_LANES = 128


def _pack_rows(arrays):
    flat = jnp.concatenate([a.reshape(-1) for a in arrays])
    total = _round_up(flat.shape[0], 8 * _LANES)
    return jnp.pad(flat, (0, total - flat.shape[0])).reshape(-1, _LANES)


def _unpack_rows(buf, shapes):
    flat = buf.reshape(-1)
    out, off = [], 0
    for shp in shapes:
        n = math.prod(shp)
        out.append(flat[off:off + n].reshape(shp))
        off += n
    return out


def _block_diag(w):
    nl, nh, n, _ = w.shape
    eye = jnp.eye(nh, dtype=w.dtype)
    return (w[:, :, :, None, :] * eye[None, :, None, :, None]).reshape(nl, nh * n, nh * n)


def _diag_blocks(bd, nh):
    n = bd.shape[0] // nh
    b4 = bd.reshape(nh, n, nh, n)
    return jnp.stack([b4[i, :, i, :] for i in range(nh)])


def _by_half_and_slot(per_layer, n_layers):
    shp = per_layer.shape
    x = per_layer.reshape((2, n_layers // 2) + shp[1:])
    x = jnp.swapaxes(x, 1, 2)
    return x.reshape(2, shp[1], -1, shp[-1])


def kernel(x, meta_tokens, ffn1_pre_g, ffn1_w_gate, ffn1_w_up, ffn1_w_down, ffn1_post_g, mix_pre_g, w_in, lru_conv_w, lru_conv_b, lru_w_a, lru_b_a, lru_w_x, lru_b_x, lru_lambda, mla_q_norm_g, mla_w_uq, mla_kv_norm_g, mla_w_ukv, lru_out_g, mla_out_g, w_out, mix_post_g, ffn2_pre_g, ffn2_w_gate, ffn2_w_up, ffn2_w_down, ffn2_post_g, loss_target, m_meta_tokens, m_ffn1_pre_g, m_ffn1_w_gate, m_ffn1_w_up, m_ffn1_w_down, m_ffn1_post_g, m_mix_pre_g, m_w_in, m_lru_conv_w, m_lru_conv_b, m_lru_w_a, m_lru_b_a, m_lru_w_x, m_lru_b_x, m_lru_lambda, m_mla_q_norm_g, m_mla_w_uq, m_mla_kv_norm_g, m_mla_w_ukv, m_lru_out_g, m_mla_out_g, m_w_out, m_mix_post_g, m_ffn2_pre_g, m_ffn2_w_gate, m_ffn2_w_up, m_ffn2_w_down, m_ffn2_post_g, v_meta_tokens, v_ffn1_pre_g, v_ffn1_w_gate, v_ffn1_w_up, v_ffn1_w_down, v_ffn1_post_g, v_mix_pre_g, v_w_in, v_lru_conv_w, v_lru_conv_b, v_lru_w_a, v_lru_b_a, v_lru_w_x, v_lru_b_x, v_lru_lambda, v_mla_q_norm_g, v_mla_w_uq, v_mla_kv_norm_g, v_mla_w_ukv, v_lru_out_g, v_mla_out_g, v_w_out, v_mix_post_g, v_ffn2_pre_g, v_ffn2_w_gate, v_ffn2_w_up, v_ffn2_w_down, v_ffn2_post_g):
    a = dict(locals())
    x2 = x[0]
    s_len, d = x2.shape
    nl = ffn1_pre_g.shape[0]
    lw = lru_conv_b.shape[-1]
    ql, kvl = mla_q_norm_g.shape[-1], mla_kv_norm_g.shape[-1]
    nh = MLA_HEADS
    vd = (d - lw) // nh
    t_real = N_META + s_len
    tp = _round_up(t_real, 384)
    tm = tp // 6
    tmb = tm // 2
    tq = 384 if tp >= 1536 else 128
    xi, yi, ci = _mesh_pos()
    chip = 2 * xi + yi
    core = ci.astype(jnp.int32).reshape(1)

    gu_loc = jnp.stack([ffn1_w_gate, ffn1_w_up, ffn2_w_gate, ffn2_w_up], axis=1).astype(BF16)
    dn_loc = jnp.stack([ffn1_w_down, ffn2_w_down], axis=1).astype(BF16)
    w_gu, w_dn, g_win, g_wuq, g_wukv, g_wout, g_conv, g_meta = _allgather_chips(
        [gu_loc, dn_loc, w_in.astype(BF16), mla_w_uq.astype(BF16), mla_w_ukv.astype(BF16), w_out.astype(BF16),
         lru_conv_w, meta_tokens], "gather_weights")
    fs = w_gu.shape[-1]

    def cols_full(g):
        return jnp.transpose(g, (1, 2, 0, 3)).reshape(g.shape[1], g.shape[2], -1)

    w_in_full = cols_full(g_win)
    o0, o1, o2, o3 = lw, 2 * lw, 2 * lw + ql, 2 * lw + ql + kvl
    w_kr = jnp.pad(w_in_full[..., o3:], ((0, 0), (0, 0), (QK_NOPE, 0)))
    w_parts = (w_in_full[..., :o0], w_in_full[..., o0:o1], w_in_full[..., o1:o2], w_in_full[..., o2:o3], w_kr)
    w_q = jnp.transpose(cols_full(g_wuq).reshape(nl, ql, nh, QK_DIM), (0, 2, 1, 3))
    w_ukv4 = cols_full(g_wukv).reshape(nl, kvl, nh, QK_NOPE + vd)
    w_k = jnp.transpose(jnp.pad(w_ukv4[..., :QK_NOPE], ((0, 0), (0, 0), (0, 0), (0, QK_ROPE))), (0, 2, 1, 3))
    w_v = jnp.transpose(w_ukv4[..., QK_NOPE:], (0, 2, 1, 3))
    w_out_full = jnp.transpose(g_wout, (1, 0, 2, 3)).reshape(nl, d, d)
    w_lo = w_out_full[:, :lw]
    w_mo = w_out_full[:, lw:].reshape(nl, nh, vd, d)
    conv_full = cols_full(g_conv)
    meta_full = jnp.transpose(g_meta, (1, 0, 2)).reshape(N_META, d)
    wa_bd = _block_diag(lru_w_a).astype(BF16)
    wx_bd = _block_diag(lru_w_x).astype(BF16)

    def gain3(g):
        return g.reshape(nl, 1, g.shape[-1])

    g3 = {n: gain3(a[n]) for n in ('ffn1_pre_g', 'ffn1_post_g', 'mix_pre_g', 'lru_conv_b', 'lru_b_a', 'lru_b_x', 'lru_lambda',
                                   'mla_q_norm_g', 'mla_kv_norm_g', 'lru_out_g', 'mix_post_g', 'ffn2_pre_g', 'ffn2_post_g')}
    g_mo = mla_out_g.reshape(nl, nh, 1, vd)

    pos = jnp.arange(tp, dtype=F32)
    inv_freq = 1.0 / (ROPE_THETA ** (jnp.arange(0, QK_ROPE, 2, dtype=F32) / QK_ROPE))
    ang = pos[:, None] * inv_freq[None, :]
    cos_t = jnp.concatenate([jnp.ones((tp, QK_NOPE), F32), jnp.cos(ang), jnp.cos(ang)], axis=1)
    sin_t = jnp.concatenate([jnp.zeros((tp, QK_NOPE), F32), jnp.sin(ang), jnp.sin(ang)], axis=1)
    half = QK_ROPE // 2
    idx = jnp.arange(half)
    rot = jnp.zeros((QK_DIM, QK_DIM), F32)
    rot = rot.at[QK_NOPE + half + idx, QK_NOPE + idx].set(-1.0).at[QK_NOPE + idx, QK_NOPE + half + idx].set(1.0)
    rot_b, rot_t = rot.astype(BF16), rot.T.astype(BF16)

    h = jnp.concatenate([meta_full, x2, jnp.zeros((tp - t_real, d), F32)], axis=0)
    target = jnp.pad(loss_target[0], ((N_META, tp - t_real), (0, 0)))
    saved = []
    for l in range(nl):
        h1, gate1, up1, f1 = _ffn_fwd(h, g3['ffn1_pre_g'], g3['ffn1_post_g'], w_gu, w_dn, l, 0, 1, 0, tm, f"ffn1_fwd_{l}")
        xr, gr, cq, ckv, q, k, v = _mix_in_fwd(h1, g3['mix_pre_g'], w_parts, g3['mla_q_norm_g'], g3['mla_kv_norm_g'],
                                               w_q, w_k, w_v, cos_t, sin_t, rot_b, l, tm, f"mix_in_fwd_{l}")
        y_lru, xc, hs = _lru_fwd(xr, gr, conv_full, g3['lru_conv_b'], wa_bd, wx_bd, g3['lru_b_a'], g3['lru_b_x'],
                                 g3['lru_lambda'], l, tm, f"lru_fwd_{l}")
        o, lse = _attn_fwd(q, k, v, tq, f"attn_fwd_{l}")
        h2, y = _mix_out_fwd(h1, y_lru, o, g3['lru_out_g'], g_mo, w_lo, w_mo, g3['mix_post_g'], l, tm, f"mix_out_fwd_{l}")
        h3, gate2, up2, f2 = _ffn_fwd(h2, g3['ffn2_pre_g'], g3['ffn2_post_g'], w_gu, w_dn, l, 2, 3, 1, tm, f"ffn2_fwd_{l}")
        saved.append((h, gate1, up1, f1, h1, xr, gr, cq, ckv, q, k, v, y_lru, xc, hs, o, lse, y, h2, gate2, up2, f2))
        h = h3

    dh, loss_parts = _loss_head(h, target, s_len, tm, "loss_head")
    loss = lax.psum(jnp.sum(loss_parts[:, 0, 0]), ("x", "y", "c"))

    gu_l, dn_l, win_l, wuq_l, wukv_l, wout_l = [None] * nl, [None] * nl, [None] * nl, [None] * nl, [None] * nl, [None] * nl
    small = {n: [None] * nl for n in _REPLICATED + ['lru_conv_w']}
    for l in reversed(range(nl)):
        (h0, gate1, up1, f1, h1, xr, gr, cq, ckv, q, k, v, y_lru, xc, hs, o, lse, y, h2, gate2, up2, f2) = saved[l]
        dh, dgate, dup, act, df, u, dgpre, dgpost = _ffn_bwd(dh, h2, f2, gate2, up2, g3['ffn2_pre_g'], g3['ffn2_post_g'],
                                                            w_gu, w_dn, l, 2, 3, 1, tmb, f"ffn2_bwd_{l}")
        small['ffn2_pre_g'][l], small['ffn2_post_g'][l] = dgpre, dgpost
        d_wg2 = _mm_tn(u, dgate, tm, BF16, f"dw_gate2_{l}")
        d_wu2 = _mm_tn(u, dup, tm, BF16, f"dw_up2_{l}")
        dn2 = _mm_tn(act, df, tm, BF16, f"dw_down2_{l}")

        d_ylru, d_o, dy_b, n_lo, n_mo, dgpost, dglo, dgmo = _mix_out_bwd(dh, y, y_lru, o, g3['lru_out_g'], g_mo, w_lo, w_mo,
                                                                       g3['mix_post_g'], l, tm, f"mix_out_bwd_{l}")
        small['mix_post_g'][l], small['lru_out_g'][l], small['mla_out_g'][l] = dgpost, dglo, dgmo
        d_wlo = _mm_tn(n_lo, dy_b, tm, F32, f"dw_out_lru_{l}")
        d_wmo = _mm_tn(n_mo, dy_b, tm, F32, f"dw_out_mla_{l}")
        wout_l[l] = jnp.concatenate([d_wlo, d_wmo.reshape(nh * vd, d)], axis=0).reshape(N_CHIPS, d // N_CHIPS, d)

        dq, dk, dv = _attn_bwd(q, k, v, o, d_o, lse, tq, f"attn_bwd_{l}")
        d_xr, d_gr, dcw, dcb, dwa, dwx, dba, dbx, dlam = _lru_bwd(d_ylru, xr, gr, xc, hs, conv_full, wa_bd, wx_bd, g3['lru_b_a'],
                                                                 g3['lru_b_x'], g3['lru_lambda'], l, tm, f"lru_bwd_{l}")
        small['lru_conv_w'][l], small['lru_conv_b'][l] = dcw, dcb
        small['lru_w_a'][l], small['lru_w_x'][l] = _diag_blocks(dwa, LRU_HEADS), _diag_blocks(dwx, LRU_HEADS)
        small['lru_b_a'][l], small['lru_b_x'][l], small['lru_lambda'][l] = dba, dbx, dlam

        dh, u, qn, kvn, dq_pre, d_cq, d_ckv, d_kr, dgpre, dgq, dgkv = _mix_in_bwd(
            dh, h1, cq, ckv, d_xr, d_gr, dq, dk, dv, g3['mix_pre_g'], w_parts, g3['mla_q_norm_g'], g3['mla_kv_norm_g'],
            w_q, w_k, w_v, cos_t, sin_t, rot_t, l, tmb, f"mix_in_bwd_{l}")
        small['mix_pre_g'][l], small['mla_q_norm_g'][l], small['mla_kv_norm_g'][l] = dgpre, dgq, dgkv
        d_win = jnp.concatenate([_mm_tn(u, d_xr, tm, F32, f"dw_in_xr_{l}"), _mm_tn(u, d_gr, tm, F32, f"dw_in_gr_{l}"),
                                 _mm_tn(u, d_cq, tm, F32, f"dw_in_cq_{l}"), _mm_tn(u, d_ckv, tm, F32, f"dw_in_ckv_{l}"),
                                 _mm_tn(u, d_kr, tm, F32, f"dw_in_kr_{l}")[:, QK_NOPE:]], axis=1)
        win_l[l] = jnp.transpose(d_win.reshape(d, N_CHIPS, -1), (1, 0, 2))
        d_wq = _mm_tn(qn, dq_pre, tm, F32, f"dw_uq_{l}")
        wuq_l[l] = jnp.transpose(jnp.transpose(d_wq, (1, 0, 2)).reshape(ql, N_CHIPS, -1), (1, 0, 2))
        d_wk = _mm_tn(kvn, dk, tm, F32, f"dw_uk_{l}")[..., :QK_NOPE]
        d_wv = _mm_tn(kvn, dv, tm, F32, f"dw_uv_{l}")
        d_wukv = jnp.transpose(jnp.concatenate([d_wk, d_wv], axis=-1), (1, 0, 2))
        wukv_l[l] = jnp.transpose(d_wukv.reshape(kvl, N_CHIPS, -1), (1, 0, 2))

        dh, dgate, dup, act, df, u, dgpre, dgpost = _ffn_bwd(dh, h0, f1, gate1, up1, g3['ffn1_pre_g'], g3['ffn1_post_g'],
                                                            w_gu, w_dn, l, 0, 1, 0, tmb, f"ffn1_bwd_{l}")
        small['ffn1_pre_g'][l], small['ffn1_post_g'][l] = dgpre, dgpost
        d_wg1 = _mm_tn(u, dgate, tm, BF16, f"dw_gate1_{l}")
        d_wu1 = _mm_tn(u, dup, tm, BF16, f"dw_up1_{l}")
        dn1 = _mm_tn(act, df, tm, BF16, f"dw_down1_{l}")
        gu_l[l] = jnp.stack([d_wg1, d_wu1, d_wg2, d_wu2], axis=1)
        dn_l[l] = jnp.stack([dn1, dn2], axis=1)

    grad_x = dh[N_META:t_real][None]

    stacks = [_by_half_and_slot(jnp.stack(t).astype(BF16), nl) for t in (gu_l, dn_l, win_l, wuq_l, wukv_l, wout_l)]
    from_core = _swap_halves(stacks, "rs_core_swap")
    pair_sums = [_add_halves(s, r, core, f"rs_pair_add_{i}") for i, (s, r) in enumerate(zip(stacks, from_core))]
    from_chips = _scatter_chips(pair_sums, "rs_chip_scatter")
    half_sums = [_sum_chips(p, f"rs_chip_sum_{i}") for i, p in enumerate(from_chips)]
    g_gu, g_dn, g_win_s, g_wuq_s, g_wukv_s, g_wout_s = _join_halves(half_sums, "rs_core_join")
    g_gu = g_gu.reshape(nl, 4, d, fs)
    g_dn = g_dn.reshape(nl, 2, fs, d)

    def one_kind(g, w):
        return g.reshape((nl, 1) + w.shape[1:])

    res = {}
    for name, g_all, kind in (('ffn1_w_gate', g_gu, 0), ('ffn1_w_up', g_gu, 1), ('ffn2_w_gate', g_gu, 2), ('ffn2_w_up', g_gu, 3),
                              ('ffn1_w_down', g_dn, 0), ('ffn2_w_down', g_dn, 1), ('w_in', one_kind(g_win_s, w_in), 0),
                              ('mla_w_uq', one_kind(g_wuq_s, mla_w_uq), 0), ('mla_w_ukv', one_kind(g_wukv_s, mla_w_ukv), 0),
                              ('w_out', one_kind(g_wout_s, w_out), 0)):
        res[name] = _adamw_sharded(a[name], a['m_' + name], a['v_' + name], g_all, kind, f"adamw_{name}")

    small_full = {n: jnp.stack(small[n]).reshape(a[n].shape if n != 'lru_conv_w' else conv_full.shape)
                  for n in _REPLICATED + ['lru_conv_w']}
    order = _REPLICATED + ['lru_conv_w']
    packed = _pack_rows([small_full[n] for n in order] + [dh[:N_META]])
    summed = _allreduce_small(packed, "allreduce_small")
    pieces = _unpack_rows(summed, [small_full[n].shape for n in order] + [(N_META, d)])
    g_small = dict(zip(order, pieces[:-1]))
    g_conv_loc = lax.dynamic_slice_in_dim(g_small.pop('lru_conv_w'), chip * lru_conv_w.shape[-1], lru_conv_w.shape[-1], axis=2)
    g_meta_loc = lax.dynamic_slice_in_dim(pieces[-1], chip * meta_tokens.shape[-1], meta_tokens.shape[-1], axis=1)

    shapes = [a[n].shape for n in _REPLICATED]
    dl, mo, vo = _adamw_whole(_pack_rows([a[n] for n in _REPLICATED]), _pack_rows([g_small[n] for n in _REPLICATED]),
                              _pack_rows([a['m_' + n] for n in _REPLICATED]), _pack_rows([a['v_' + n] for n in _REPLICATED]),
                              "adamw_replicated")
    for n, dd, mm, vv in zip(_REPLICATED, _unpack_rows(dl, shapes), _unpack_rows(mo, shapes), _unpack_rows(vo, shapes)):
        res[n] = (g_small[n], dd, mm, vv)
    cshape = lru_conv_w.shape
    c2 = (cshape[0] * cshape[1], cshape[2])
    dd, mm, vv = _adamw_whole(lru_conv_w.reshape(c2), g_conv_loc.reshape(c2), m_lru_conv_w.reshape(c2), v_lru_conv_w.reshape(c2),
                              "adamw_conv_w")
    res['lru_conv_w'] = (g_conv_loc, dd.reshape(cshape), mm.reshape(cshape), vv.reshape(cshape))
    res['meta_tokens'] = (g_meta_loc,) + tuple(_adamw_whole(meta_tokens, g_meta_loc, m_meta_tokens, v_meta_tokens, "adamw_meta"))

    return (loss, grad_x, *[res[n][0] for n in _W_NAMES], *[res[n][1] for n in _W_NAMES],
            *[res[n][2] for n in _W_NAMES], *[res[n][3] for n in _W_NAMES])
```

```python
import functools
import math

import jax
import jax.numpy as jnp
import numpy as np
from jax import lax
from jax.experimental import pallas as pl
from jax.experimental.pallas import tpu as pltpu

F32 = jnp.float32
BF16 = jnp.bfloat16
MESH = pl.DeviceIdType.MESH

EPS = 1e-6
N_META = 16
LRU_HEADS = 8
MLA_HEADS = 8
QK_NOPE = 64
QK_ROPE = 32
QK_DIM = QK_NOPE + QK_ROPE
LRU_C = 8.0
ROPE_THETA = 10000.0
CONV_WIDTH = 4
N_CHIPS = 4

ADAM_LR = 0.001
ADAM_B1 = 0.9
ADAM_B2 = 0.999
ADAM_EPS = 1e-08
ADAM_WD = 0.01
ADAM_STEP = 10

V7X_VMEM_LIMIT_BYTES = 56 * 1024 * 1024
NEG_BIG = -1e30


def _cparams(sem=None):
    return pltpu.CompilerParams(dimension_semantics=sem, vmem_limit_bytes=V7X_VMEM_LIMIT_BYTES)


def _round_up(a, b):
    return -(-a // b) * b


def _whole(*shape):
    return pl.BlockSpec(shape, lambda *_: (0,) * len(shape))


_ANY = pl.BlockSpec(memory_space=pl.ANY)


def _dot(a, b):
    return jnp.dot(a.astype(BF16), b.astype(BF16), preferred_element_type=F32)


def _dot_nt(a, b):
    return lax.dot_general(a.astype(BF16), b.astype(BF16), (((1,), (1,)), ((), ())), preferred_element_type=F32)


def _dot_tn(a, b):
    return lax.dot_general(a.astype(BF16), b.astype(BF16), (((0,), (0,)), ((), ())), preferred_element_type=F32)


def _dot_split(x, p):
    hi = x.astype(BF16)
    lo = (x - hi.astype(F32)).astype(BF16)
    return jnp.dot(hi, p, preferred_element_type=F32) + jnp.dot(lo, p, preferred_element_type=F32)


def _rms(x, g):
    r = lax.rsqrt(jnp.mean(x * x, axis=-1, keepdims=True) + EPS)
    return x * r * g, r


def _rms_bwd(x, g, dy):
    r = lax.rsqrt(jnp.mean(x * x, axis=-1, keepdims=True) + EPS)
    xh = x * r
    dyg = dy * g
    dx = r * (dyg - xh * jnp.mean(dyg * xh, axis=-1, keepdims=True))
    dg = jnp.sum(dy * xh, axis=0, keepdims=True)
    return dx, dg


def _sigmoid(x):
    return 1.0 / (1.0 + jnp.exp(-x))


def _neg_expm1(x):
    series = -x * (1.0 + x * (0.5 + x * (1.0 / 6.0 + x * (1.0 / 24.0))))
    return jnp.where(jnp.abs(x) < 0.03, series, 1.0 - jnp.exp(x))


_GELU_K = math.sqrt(2.0 / math.pi)


def _gelu(x):
    t = jnp.tanh(_GELU_K * (x + 0.044715 * x * x * x))
    return 0.5 * x * (1.0 + t), t


def _gelu_grad(x, t):
    return 0.5 * (1.0 + t) + 0.5 * x * (1.0 - t * t) * _GELU_K * (1.0 + 3.0 * 0.044715 * x * x)


def _rope(x, cos_t, sin_t, rot):
    return x * cos_t + _dot_split(x, rot) * sin_t


def _ffn_fwd(h, g_pre, g_post, w_gu, w_dn, layer, kind_gate, kind_up, kind_dn, tm, name):
    tp, d = h.shape
    n_slot, fs = w_gu.shape[0], w_gu.shape[-1]

    def body(h_ref, gpre_ref, gpost_ref, wg_ref, wu_ref, wd_ref, hout_ref, gate_ref, up_ref, f_ref, u_sc, acc_sc):
        j = pl.program_id(1)

        @pl.when(j == 0)
        def _():
            u, _ = _rms(h_ref[...], gpre_ref[...])
            u_sc[...] = u.astype(BF16)
            acc_sc[...] = jnp.zeros_like(acc_sc)

        u = u_sc[...]
        gate = jnp.dot(u, wg_ref[...], preferred_element_type=F32)
        up = jnp.dot(u, wu_ref[...], preferred_element_type=F32)
        act = gate * _sigmoid(gate) * up
        gate_ref[...] = gate.astype(BF16)
        up_ref[...] = up.astype(BF16)
        acc_sc[...] += jnp.dot(act.astype(BF16), wd_ref[...], preferred_element_type=F32)

        @pl.when(j == n_slot - 1)
        def _():
            f = acc_sc[...]
            f_ref[...] = f
            n, _ = _rms(f, gpost_ref[...])
            hout_ref[...] = h_ref[...] + 0.5 * n

    tok = pl.BlockSpec((tm, d), lambda i, j: (i, 0))
    gain = pl.BlockSpec((None, 1, d), lambda i, j: (layer, 0, 0))
    slot_act = pl.BlockSpec((None, tm, fs), lambda i, j: (j, i, 0))
    return pl.pallas_call(
        body, name=name, grid=(tp // tm, n_slot),
        in_specs=[tok, gain, gain,
                  pl.BlockSpec((None, None, d, fs), lambda i, j: (j, kind_gate, 0, 0)),
                  pl.BlockSpec((None, None, d, fs), lambda i, j: (j, kind_up, 0, 0)),
                  pl.BlockSpec((None, None, fs, d), lambda i, j: (j, kind_dn, 0, 0))],
        out_specs=[tok, slot_act, slot_act, tok],
        out_shape=[jax.ShapeDtypeStruct((tp, d), F32), jax.ShapeDtypeStruct((n_slot, tp, fs), BF16),
                   jax.ShapeDtypeStruct((n_slot, tp, fs), BF16), jax.ShapeDtypeStruct((tp, d), F32)],
        scratch_shapes=[pltpu.VMEM((tm, d), BF16), pltpu.VMEM((tm, d), F32)],
        compiler_params=_cparams(("parallel", "arbitrary")),
    )(h, g_pre, g_post, w_gu, w_gu, w_dn)


def _ffn_bwd(dh_out, h, f, gate, up, g_pre, g_post, w_gu, w_dn, layer, kind_gate, kind_up, kind_dn, tm, name):
    tp, d = h.shape
    n_slot, fs = w_gu.shape[0], w_gu.shape[-1]

    def body(dho_ref, h_ref, f_ref, gate_ref, up_ref, gpre_ref, gpost_ref, wg_ref, wu_ref, wd_ref,
             dh_ref, dgate_ref, dup_ref, act_ref, df_ref, u_ref, dgpre_ref, dgpost_ref, du_sc):
        i, j = pl.program_id(0), pl.program_id(1)

        @pl.when((i == 0) & (j == 0))
        def _():
            dgpre_ref[...] = jnp.zeros_like(dgpre_ref)
            dgpost_ref[...] = jnp.zeros_like(dgpost_ref)

        @pl.when(j == 0)
        def _():
            df, dg = _rms_bwd(f_ref[...], gpost_ref[...], 0.5 * dho_ref[...])
            df_ref[...] = df.astype(BF16)
            dgpost_ref[...] += dg
            u, _ = _rms(h_ref[...], gpre_ref[...])
            u_ref[...] = u.astype(BF16)
            du_sc[...] = jnp.zeros_like(du_sc)

        g = gate_ref[...].astype(F32)
        u_ = up_ref[...].astype(F32)
        sg = _sigmoid(g)
        silu = g * sg
        dact = _dot_nt(df_ref[...], wd_ref[...])
        dup = dact * silu
        dgate = dact * u_ * (sg * (1.0 + g * (1.0 - sg)))
        act_ref[...] = (silu * u_).astype(BF16)
        dup_b = dup.astype(BF16)
        dgate_b = dgate.astype(BF16)
        dup_ref[...] = dup_b
        dgate_ref[...] = dgate_b
        du_sc[...] += _dot_nt(dgate_b, wg_ref[...]) + _dot_nt(dup_b, wu_ref[...])

        @pl.when(j == n_slot - 1)
        def _():
            dx, dg = _rms_bwd(h_ref[...], gpre_ref[...], du_sc[...])
            dh_ref[...] = dho_ref[...] + dx
            dgpre_ref[...] += dg

    tok = pl.BlockSpec((tm, d), lambda i, j: (i, 0))
    gain = pl.BlockSpec((None, 1, d), lambda i, j: (layer, 0, 0))
    acc = pl.BlockSpec((1, d), lambda i, j: (0, 0))
    slot_act = pl.BlockSpec((None, tm, fs), lambda i, j: (j, i, 0))
    act_shape = jax.ShapeDtypeStruct((n_slot, tp, fs), BF16)
    return pl.pallas_call(
        body, name=name, grid=(tp // tm, n_slot),
        in_specs=[tok, tok, tok, slot_act, slot_act, gain, gain,
                  pl.BlockSpec((None, None, d, fs), lambda i, j: (j, kind_gate, 0, 0)),
                  pl.BlockSpec((None, None, d, fs), lambda i, j: (j, kind_up, 0, 0)),
                  pl.BlockSpec((None, None, fs, d), lambda i, j: (j, kind_dn, 0, 0))],
        out_specs=[tok, slot_act, slot_act, slot_act, tok, tok, acc, acc],
        out_shape=[jax.ShapeDtypeStruct((tp, d), F32), act_shape, act_shape, act_shape,
                   jax.ShapeDtypeStruct((tp, d), BF16), jax.ShapeDtypeStruct((tp, d), BF16),
                   jax.ShapeDtypeStruct((1, d), F32), jax.ShapeDtypeStruct((1, d), F32)],
        scratch_shapes=[pltpu.VMEM((tm, d), F32)],
        compiler_params=_cparams(("arbitrary", "arbitrary")),
    )(dh_out, h, f, gate, up, g_pre, g_post, w_gu, w_gu, w_dn)


def _mm_tn(a, b, tk, out_dtype, name):
    ga = a.shape[0] if a.ndim == 3 else None
    gb = b.shape[0] if b.ndim == 3 else None
    groups = ga or gb or 1
    t, m = a.shape[-2:]
    n = b.shape[-1]
    nk = t // tk

    def body(a_ref, b_ref, o_ref, acc_sc):
        k = pl.program_id(1)

        @pl.when(k == 0)
        def _():
            acc_sc[...] = jnp.zeros_like(acc_sc)

        acc_sc[...] += _dot_tn(a_ref[...], b_ref[...])

        @pl.when(k == nk - 1)
        def _():
            o_ref[...] = acc_sc[...].astype(out_dtype)

    a_spec = (pl.BlockSpec((None, tk, m), lambda g, k: (g, k, 0)) if ga else pl.BlockSpec((tk, m), lambda g, k: (k, 0)))
    b_spec = (pl.BlockSpec((None, tk, n), lambda g, k: (g, k, 0)) if gb else pl.BlockSpec((tk, n), lambda g, k: (k, 0)))
    out = pl.pallas_call(
        body, name=name, grid=(groups, nk),
        in_specs=[a_spec, b_spec],
        out_specs=pl.BlockSpec((None, m, n), lambda g, k: (g, 0, 0)),
        out_shape=jax.ShapeDtypeStruct((groups, m, n), out_dtype),
        scratch_shapes=[pltpu.VMEM((m, n), F32)],
        compiler_params=_cparams(("parallel", "arbitrary")),
    )(a, b)
    return out if (ga or gb) else out[0]


def _mix_in_fwd(h, g_pre, w_parts, g_q, g_kv, w_q, w_k, w_v, cos_t, sin_t, rot, layer, tm, name):
    tp, d = h.shape
    w_xr, w_gr, w_cq, w_ckv, w_kr = w_parts
    lw, ql, kvl = w_xr.shape[-1], w_cq.shape[-1], w_ckv.shape[-1]
    nh, vd = w_v.shape[0], w_v.shape[-1]

    def body(h_ref, gpre_ref, wxr_ref, wgr_ref, wcq_ref, wckv_ref, wkr_ref, gq_ref, gkv_ref, wq_ref, wk_ref, wv_ref,
             cos_ref, sin_ref, rot_ref, xr_ref, gr_ref, cq_ref, ckv_ref, q_ref, k_ref, v_ref):
        u, _ = _rms(h_ref[...], gpre_ref[...])
        u = u.astype(BF16)
        xr_ref[...] = jnp.dot(u, wxr_ref[...], preferred_element_type=F32)
        gr_ref[...] = jnp.dot(u, wgr_ref[...], preferred_element_type=F32)
        cq = jnp.dot(u, wcq_ref[...], preferred_element_type=F32)
        ckv = jnp.dot(u, wckv_ref[...], preferred_element_type=F32)
        kr = jnp.dot(u, wkr_ref[...], preferred_element_type=F32)
        cq_ref[...] = cq
        ckv_ref[...] = ckv
        cos_b, sin_b, rot_b = cos_ref[...], sin_ref[...], rot_ref[...]
        qn = _rms(cq, gq_ref[...])[0].astype(BF16)
        kvn = _rms(ckv, gkv_ref[...])[0].astype(BF16)
        k_rope = _rope(kr, cos_b, sin_b, rot_b)
        for hd in range(nh):
            q_pre = jnp.dot(qn, wq_ref[hd], preferred_element_type=F32)
            q_ref[hd] = _rope(q_pre, cos_b, sin_b, rot_b).astype(BF16)
            k_ref[hd] = (jnp.dot(kvn, wk_ref[hd], preferred_element_type=F32) + k_rope).astype(BF16)
            v_ref[hd] = jnp.dot(kvn, wv_ref[hd], preferred_element_type=F32).astype(BF16)

    def tok(n):
        return pl.BlockSpec((tm, n), lambda i: (i, 0))

    def lay(*shape):
        return pl.BlockSpec((None,) + shape, lambda i: (layer,) + (0,) * len(shape))

    def heads(n):
        return pl.BlockSpec((nh, tm, n), lambda i: (0, i, 0))

    return pl.pallas_call(
        body, name=name, grid=(tp // tm,),
        in_specs=[tok(d), lay(1, d), _whole(d, lw), _whole(d, lw), _whole(d, ql), _whole(d, kvl), _whole(d, QK_DIM),
                  lay(1, ql), lay(1, kvl),
                  _whole(nh, ql, QK_DIM), _whole(nh, kvl, QK_DIM), _whole(nh, kvl, vd), tok(QK_DIM), tok(QK_DIM),
                  pl.BlockSpec((QK_DIM, QK_DIM), lambda i: (0, 0))],
        out_specs=[tok(lw), tok(lw), tok(ql), tok(kvl), heads(QK_DIM), heads(QK_DIM), heads(vd)],
        out_shape=[jax.ShapeDtypeStruct((tp, lw), F32), jax.ShapeDtypeStruct((tp, lw), F32),
                   jax.ShapeDtypeStruct((tp, ql), F32), jax.ShapeDtypeStruct((tp, kvl), F32),
                   jax.ShapeDtypeStruct((nh, tp, QK_DIM), BF16), jax.ShapeDtypeStruct((nh, tp, QK_DIM), BF16),
                   jax.ShapeDtypeStruct((nh, tp, vd), BF16)],
        compiler_params=_cparams(("parallel",)),
    )(h, g_pre, w_xr, w_gr, w_cq, w_ckv, w_kr, g_q, g_kv, w_q, w_k, w_v, cos_t, sin_t, rot)


def _mix_in_bwd(dh_res, h, cq, ckv, d_xr, d_gr, dq, dk, dv, g_pre, w_parts, g_q, g_kv, w_q, w_k, w_v,
                cos_t, sin_t, rot_t, layer, tm, name):
    tp, d = h.shape
    w_xr, w_gr, w_cq, w_ckv, w_kr = w_parts
    lw, ql, kvl = w_xr.shape[-1], w_cq.shape[-1], w_ckv.shape[-1]
    nh, vd = w_v.shape[0], w_v.shape[-1]

    def body(dhr_ref, h_ref, cq_ref, ckv_ref, dxr_ref, dgr_ref, dq_ref, dk_ref, dv_ref, gpre_ref,
             wxr_ref, wgr_ref, wcq_ref, wckv_ref, wkr_ref, gq_ref, gkv_ref, wq_ref, wk_ref, wv_ref,
             cos_ref, sin_ref, rott_ref,
             dh_ref, u_ref, qn_ref, kvn_ref, dqpre_ref, dcq_ref, dckv_ref, dkr_ref, dgpre_ref, dgq_ref, dgkv_ref):
        i = pl.program_id(0)

        @pl.when(i == 0)
        def _():
            dgpre_ref[...] = jnp.zeros_like(dgpre_ref)
            dgq_ref[...] = jnp.zeros_like(dgq_ref)
            dgkv_ref[...] = jnp.zeros_like(dgkv_ref)

        cos_b, sin_b, rott_b = cos_ref[...], sin_ref[...], rott_ref[...]
        hh = h_ref[...]
        u, _ = _rms(hh, gpre_ref[...])
        u_ref[...] = u.astype(BF16)
        cq, ckv = cq_ref[...], ckv_ref[...]
        qn = _rms(cq, gq_ref[...])[0]
        kvn = _rms(ckv, gkv_ref[...])[0]
        qn_ref[...] = qn.astype(BF16)
        kvn_ref[...] = kvn.astype(BF16)
        d_qn = jnp.zeros((tm, ql), F32)
        d_kvn = jnp.zeros((tm, kvl), F32)
        d_krope = jnp.zeros((tm, QK_DIM), F32)
        for hd in range(nh):
            dq_h = dq_ref[hd]
            dq_pre = dq_h * cos_b + _dot_split(dq_h * sin_b, rott_b)
            dq_pre_b = dq_pre.astype(BF16)
            dqpre_ref[hd] = dq_pre_b
            d_qn += _dot_nt(dq_pre_b, wq_ref[hd])
            dk_h = dk_ref[hd]
            d_krope += dk_h
            d_kvn += _dot_nt(dk_h, wk_ref[hd]) + _dot_nt(dv_ref[hd], wv_ref[hd])
        d_kr = d_krope * cos_b + _dot_split(d_krope * sin_b, rott_b)
        d_cq, dgq = _rms_bwd(cq, gq_ref[...], d_qn)
        d_ckv, dgkv = _rms_bwd(ckv, gkv_ref[...], d_kvn)
        dgq_ref[...] += dgq
        dgkv_ref[...] += dgkv
        d_cq_b, d_ckv_b, d_kr_b = d_cq.astype(BF16), d_ckv.astype(BF16), d_kr.astype(BF16)
        dcq_ref[...] = d_cq_b
        dckv_ref[...] = d_ckv_b
        dkr_ref[...] = d_kr_b
        du = (_dot_nt(dxr_ref[...], wxr_ref[...]) + _dot_nt(dgr_ref[...], wgr_ref[...]) + _dot_nt(d_cq_b, wcq_ref[...])
              + _dot_nt(d_ckv_b, wckv_ref[...]) + _dot_nt(d_kr_b, wkr_ref[...]))
        dx, dg = _rms_bwd(hh, gpre_ref[...], du)
        dh_ref[...] = dhr_ref[...] + dx
        dgpre_ref[...] += dg

    def tok(n):
        return pl.BlockSpec((tm, n), lambda i: (i, 0))

    def lay(*shape):
        return pl.BlockSpec((None,) + shape, lambda i: (layer,) + (0,) * len(shape))

    def heads(n):
        return pl.BlockSpec((nh, tm, n), lambda i: (0, i, 0))

    def acc(n):
        return pl.BlockSpec((1, n), lambda i: (0, 0))

    return pl.pallas_call(
        body, name=name, grid=(tp // tm,),
        in_specs=[tok(d), tok(d), tok(ql), tok(kvl), tok(lw), tok(lw), heads(QK_DIM), heads(QK_DIM), heads(vd), lay(1, d),
                  _whole(d, lw), _whole(d, lw), _whole(d, ql), _whole(d, kvl), _whole(d, QK_DIM), lay(1, ql), lay(1, kvl),
                  _whole(nh, ql, QK_DIM), _whole(nh, kvl, QK_DIM), _whole(nh, kvl, vd), tok(QK_DIM), tok(QK_DIM),
                  pl.BlockSpec((QK_DIM, QK_DIM), lambda i: (0, 0))],
        out_specs=[tok(d), tok(d), tok(ql), tok(kvl), heads(QK_DIM), tok(ql), tok(kvl), tok(QK_DIM), acc(d), acc(ql), acc(kvl)],
        out_shape=[jax.ShapeDtypeStruct((tp, d), F32), jax.ShapeDtypeStruct((tp, d), BF16),
                   jax.ShapeDtypeStruct((tp, ql), BF16), jax.ShapeDtypeStruct((tp, kvl), BF16),
                   jax.ShapeDtypeStruct((nh, tp, QK_DIM), BF16), jax.ShapeDtypeStruct((tp, ql), BF16),
                   jax.ShapeDtypeStruct((tp, kvl), BF16), jax.ShapeDtypeStruct((tp, QK_DIM), BF16),
                   jax.ShapeDtypeStruct((1, d), F32), jax.ShapeDtypeStruct((1, ql), F32), jax.ShapeDtypeStruct((1, kvl), F32)],
        compiler_params=_cparams(("arbitrary",)),
    )(dh_res, h, cq, ckv, d_xr, d_gr, dq, dk, dv, g_pre, w_xr, w_gr, w_cq, w_ckv, w_kr, g_q, g_kv, w_q, w_k, w_v,
      cos_t, sin_t, rot_t)


def _lru_gates(xc, wa_ref, wx_ref, ba, bx, sp):
    xcb = xc.astype(BF16)
    r = _sigmoid(jnp.dot(xcb, wa_ref[...], preferred_element_type=F32) + ba)
    ig = _sigmoid(jnp.dot(xcb, wx_ref[...], preferred_element_type=F32) + bx)
    log_a = -LRU_C * r * sp
    a = jnp.exp(log_a)
    sq = jnp.sqrt(_neg_expm1(2.0 * log_a))
    return r, ig, a, sq


def _softplus(x):
    return jnp.maximum(x, 0.0) + jnp.log(1.0 + jnp.exp(-jnp.abs(x)))


def _lru_fwd(xr, gr, conv_w, conv_b, wa_bd, wx_bd, b_a, b_x, lam, layer, tc, name):
    tp, w = xr.shape
    pad = 8

    def body(xr_ref, gr_ref, cw_ref, cb_ref, wa_ref, wx_ref, ba_ref, bx_ref, lam_ref, y_ref, xc_ref, hs_ref,
             xe_sc, a_sc, b_sc, st_sc):
        c = pl.program_id(0)

        @pl.when(c == 0)
        def _():
            xe_sc[pl.ds(0, pad), :] = jnp.zeros((pad, w), F32)
            st_sc[...] = jnp.zeros_like(st_sc)

        xe_sc[pl.ds(pad, tc), :] = xr_ref[...]
        xc = cb_ref[...] + xe_sc[pl.ds(pad, tc), :] * cw_ref[pl.ds(CONV_WIDTH - 1, 1), :]
        for k in range(CONV_WIDTH - 1):
            xc = xc + xe_sc[pl.ds(pad - (CONV_WIDTH - 1) + k, tc), :] * cw_ref[pl.ds(k, 1), :]
        xe_sc[pl.ds(0, pad), :] = xe_sc[pl.ds(tc, pad), :]
        xc_ref[...] = xc
        sp = _softplus(-lam_ref[...])
        _, ig, a, sq = _lru_gates(xc, wa_ref, wx_ref, ba_ref[...], bx_ref[...], sp)
        a_sc[...] = a
        b_sc[...] = sq * (ig * xc)

        def step(t, hcur):
            hnew = a_sc[pl.ds(t, 1), :] * hcur + b_sc[pl.ds(t, 1), :]
            hs_ref[pl.ds(t, 1), :] = hnew
            return hnew

        st_sc[...] = lax.fori_loop(0, tc, step, st_sc[...], unroll=8)
        y_ref[...] = hs_ref[...] * _gelu(gr_ref[...])[0]

    tok = pl.BlockSpec((tc, w), lambda c: (c, 0))

    def lay(*shape):
        return pl.BlockSpec((None,) + shape, lambda c: (layer,) + (0,) * len(shape))

    out = jax.ShapeDtypeStruct((tp, w), F32)
    return pl.pallas_call(
        body, name=name, grid=(tp // tc,),
        in_specs=[tok, tok, lay(CONV_WIDTH, w), lay(1, w), lay(w, w), lay(w, w), lay(1, w), lay(1, w), lay(1, w)],
        out_specs=[tok, tok, tok], out_shape=[out, out, out],
        scratch_shapes=[pltpu.VMEM((tc + pad, w), F32), pltpu.VMEM((tc, w), F32), pltpu.VMEM((tc, w), F32),
                        pltpu.VMEM((1, w), F32)],
        compiler_params=_cparams(("arbitrary",)),
    )(xr, gr, conv_w, conv_b, wa_bd, wx_bd, b_a, b_x, lam)


def _lru_bwd(dy, xr, gr, xc, hs, conv_w, wa_bd, wx_bd, b_a, b_x, lam, layer, tc, name):
    tp, w = xr.shape
    pad = 8
    nc = tp // tc
    per = tc // pad

    def body(dy_ref, xr_ref, gr_ref, xc_ref, hs_ref, xrp_ref, hsp_ref, cw_ref, wa_ref, wx_ref, ba_ref, bx_ref, lam_ref,
             dxr_ref, dgr_ref, dcw_ref, dcb_ref, dwa_ref, dwx_ref, dba_ref, dbx_ref, dlam_ref,
             a_sc, dh_sc, dxc_sc, he_sc, xe_sc, carry_sc):
        s = pl.program_id(0)
        first_chunk = s == nc - 1

        @pl.when(s == 0)
        def _():
            for ref in (dcw_ref, dcb_ref, dwa_ref, dwx_ref, dba_ref, dbx_ref, dlam_ref):
                ref[...] = jnp.zeros_like(ref)
            carry_sc[...] = jnp.zeros_like(carry_sc)
            dxc_sc[pl.ds(tc, pad), :] = jnp.zeros((pad, w), F32)

        keep = jnp.where(first_chunk, 0.0, 1.0)
        he_sc[pl.ds(0, pad), :] = hsp_ref[...] * keep
        he_sc[pl.ds(pad, tc), :] = hs_ref[...]
        xe_sc[pl.ds(0, pad), :] = xrp_ref[...] * keep
        xe_sc[pl.ds(pad, tc), :] = xr_ref[...]

        lam_v = lam_ref[...]
        sp = _softplus(-lam_v)
        xc = xc_ref[...]
        r, ig, a, sq = _lru_gates(xc, wa_ref, wx_ref, ba_ref[...], bx_ref[...], sp)
        a_sc[...] = a
        grv = gr_ref[...]
        gl, th = _gelu(grv)
        dyv = dy_ref[...]
        dgr_ref[...] = dyv * hs_ref[...] * _gelu_grad(grv, th)
        dh_sc[...] = dyv * gl

        def step(n, g):
            t = tc - 1 - n
            dh = dh_sc[pl.ds(t, 1), :] + g
            dh_sc[pl.ds(t, 1), :] = dh
            return a_sc[pl.ds(t, 1), :] * dh

        carry_sc[...] = lax.fori_loop(0, tc, step, carry_sc[...], unroll=8)

        dh = dh_sc[...]
        d_a = dh * he_sc[pl.ds(pad - 1, tc), :]
        d_ixc = dh * sq
        d_sq = dh * (ig * xc)
        a2 = a * a
        d_la = d_a * a - d_sq * a2 / sq
        d_r = d_la * (-LRU_C * sp)
        d_sp = jnp.sum(d_la * (-LRU_C * r), axis=0, keepdims=True)
        dlam_ref[...] += d_sp * (-_sigmoid(-lam_v))
        d_pa = d_r * r * (1.0 - r)
        d_px = d_ixc * xc * ig * (1.0 - ig)
        dba_ref[...] += jnp.sum(d_pa, axis=0, keepdims=True)
        dbx_ref[...] += jnp.sum(d_px, axis=0, keepdims=True)
        d_pa_b, d_px_b = d_pa.astype(BF16), d_px.astype(BF16)
        xcb = xc.astype(BF16)
        dwa_ref[...] += _dot_tn(xcb, d_pa_b)
        dwx_ref[...] += _dot_tn(xcb, d_px_b)
        d_xc = d_ixc * ig + _dot_nt(d_pa_b, wa_ref[...]) + _dot_nt(d_px_b, wx_ref[...])
        dcb_ref[...] += jnp.sum(d_xc, axis=0, keepdims=True)
        dxc_sc[pl.ds(0, tc), :] = d_xc
        d_xr = jnp.zeros((tc, w), F32)
        for k in range(CONV_WIDTH):
            off = CONV_WIDTH - 1 - k
            d_xr = d_xr + dxc_sc[pl.ds(off, tc), :] * cw_ref[pl.ds(k, 1), :]
            dcw_ref[pl.ds(k, 1), :] += jnp.sum(d_xc * xe_sc[pl.ds(pad - off, tc), :], axis=0, keepdims=True)
        dxr_ref[...] = d_xr
        dxc_sc[pl.ds(tc, pad), :] = dxc_sc[pl.ds(0, pad), :]

    def rev(cidx):
        return nc - 1 - cidx

    tok = pl.BlockSpec((tc, w), lambda c: (rev(c), 0))
    prev = pl.BlockSpec((pad, w), lambda c: (jnp.maximum(rev(c) * per - 1, 0), 0))

    def lay(*shape):
        return pl.BlockSpec((None,) + shape, lambda c: (layer,) + (0,) * len(shape))

    def acc(*shape):
        return pl.BlockSpec(shape, lambda c: (0,) * len(shape))

    big = jax.ShapeDtypeStruct((tp, w), F32)
    vec = jax.ShapeDtypeStruct((1, w), F32)
    return pl.pallas_call(
        body, name=name, grid=(nc,),
        in_specs=[tok, tok, tok, tok, tok, prev, prev, lay(CONV_WIDTH, w), lay(w, w), lay(w, w), lay(1, w), lay(1, w), lay(1, w)],
        out_specs=[tok, tok, acc(CONV_WIDTH, w), acc(1, w), acc(w, w), acc(w, w), acc(1, w), acc(1, w), acc(1, w)],
        out_shape=[big, big, jax.ShapeDtypeStruct((CONV_WIDTH, w), F32), vec, jax.ShapeDtypeStruct((w, w), F32),
                   jax.ShapeDtypeStruct((w, w), F32), vec, vec, vec],
        scratch_shapes=[pltpu.VMEM((tc, w), F32), pltpu.VMEM((tc, w), F32), pltpu.VMEM((tc + pad, w), F32),
                        pltpu.VMEM((tc + pad, w), F32), pltpu.VMEM((tc + pad, w), F32), pltpu.VMEM((1, w), F32)],
        compiler_params=_cparams(("arbitrary",)),
    )(dy, xr, gr, xc, hs, xr, hs, conv_w, wa_bd, wx_bd, b_a, b_x, lam)


def _attn_fwd(q, k, v, tq, name, job=None):
    nh, tp, dqk = q.shape
    vd = v.shape[-1]
    scale = dqk ** -0.5
    nq = tp // tq
    n_ji = len(job.inputs) if job else 0
    n_jo = len(job.out_shape) if job else 0

    def body(q_ref, k_ref, v_ref, *rest):
        job_in, (o_ref, lse_ref), rest = rest[:n_ji], rest[n_ji:n_ji + 2], rest[n_ji + 2:]
        job_out, sems = rest[:n_jo], rest[n_jo:]
        i = pl.program_id(1)
        if job:
            first = (pl.program_id(0) == 0) & (i == 0)
            pl.when(first)(lambda: job.start(job_in, job_out, sems))
        qb = q_ref[...]
        qpos = i * tq + lax.broadcasted_iota(jnp.int32, (tq, tq), 0)
        kidx = lax.broadcasted_iota(jnp.int32, (tq, tq), 1)

        def step(j, carry):
            m, l, acc = carry
            off = pl.multiple_of(j * tq, tq)
            kb = k_ref[pl.ds(off, tq), :]
            vb = v_ref[pl.ds(off, tq), :]
            s = _dot_nt(qb, kb) * scale
            s = jnp.where(kidx + j * tq <= qpos, s, NEG_BIG)
            m_new = jnp.maximum(m, jnp.max(s, axis=-1, keepdims=True))
            p = jnp.exp(s - m_new)
            alpha = jnp.exp(m - m_new)
            l = alpha * l + jnp.sum(p, axis=-1, keepdims=True)
            acc = alpha * acc + jnp.dot(p.astype(BF16), vb, preferred_element_type=F32)
            return m_new, l, acc

        init = (jnp.full((tq, 1), NEG_BIG, F32), jnp.zeros((tq, 1), F32), jnp.zeros((tq, vd), F32))
        m, l, acc = lax.fori_loop(0, i + 1, step, init)
        o_ref[...] = acc / l
        lse_ref[...] = m + jnp.log(l)
        if job:
            last = (pl.program_id(0) == nh - 1) & (i == nq - 1)
            pl.when(last)(lambda: job.finish(job_in, job_out, sems))

    outs = pl.pallas_call(
        body, name=name, grid=(nh, nq),
        in_specs=[pl.BlockSpec((None, tq, dqk), lambda h, i: (h, i, 0)),
                  pl.BlockSpec((None, tp, dqk), lambda h, i: (h, 0, 0)),
                  pl.BlockSpec((None, tp, vd), lambda h, i: (h, 0, 0))] + [_ANY] * n_ji,
        out_specs=[pl.BlockSpec((None, tq, vd), lambda h, i: (h, i, 0)),
                   pl.BlockSpec((None, tq, 1), lambda h, i: (h, i, 0))] + [_ANY] * n_jo,
        out_shape=[jax.ShapeDtypeStruct((nh, tp, vd), F32), jax.ShapeDtypeStruct((nh, tp, 1), F32)]
        + (list(job.out_shape) if job else []),
        scratch_shapes=list(job.scratch) if job else [],
        compiler_params=_cparams(("arbitrary", "arbitrary") if job else ("parallel", "parallel")),
    )(q, k, v, *(job.inputs if job else ()))
    return outs[0], outs[1], list(outs[2:])


def _attn_bwd(q, k, v, o, do, lse, tq, name, job=None):
    nh, tp, dqk = q.shape
    vd = v.shape[-1]
    scale = dqk ** -0.5
    nq = tp // tq
    n_ji = len(job.inputs) if job else 0
    n_jo = len(job.out_shape) if job else 0

    def body(q_ref, k_ref, v_ref, o_ref, do_ref, lse_ref, *rest):
        job_in, (dq_ref, dk_ref, dv_ref), rest = rest[:n_ji], rest[n_ji:n_ji + 3], rest[n_ji + 3:]
        job_out, sems = rest[:n_jo], rest[n_jo:]
        kb_i = pl.program_id(1)
        if job:
            first = (pl.program_id(0) == 0) & (kb_i == 0)
            pl.when(first)(lambda: job.start(job_in, job_out, sems))

        @pl.when(kb_i == 0)
        def _():
            dq_ref[...] = jnp.zeros_like(dq_ref)

        kb = k_ref[...]
        vb = v_ref[...]
        kpos = kb_i * tq + lax.broadcasted_iota(jnp.int32, (tq, tq), 1)
        qidx = lax.broadcasted_iota(jnp.int32, (tq, tq), 0)

        def step(qi, carry):
            dk, dv = carry
            off = pl.multiple_of(qi * tq, tq)
            qb = q_ref[pl.ds(off, tq), :]
            dob = do_ref[pl.ds(off, tq), :]
            ob = o_ref[pl.ds(off, tq), :]
            s = _dot_nt(qb, kb) * scale
            s = jnp.where(kpos <= qidx + qi * tq, s, NEG_BIG)
            p = jnp.exp(s - lse_ref[pl.ds(off, tq), :])
            dob_b = dob.astype(BF16)
            dp = _dot_nt(dob_b, vb)
            delta = jnp.sum(dob * ob, axis=-1, keepdims=True)
            ds = (p * (dp - delta) * scale).astype(BF16)
            dv = dv + _dot_tn(p, dob_b)
            dk = dk + _dot_tn(ds, qb)
            dq_ref[pl.ds(off, tq), :] += jnp.dot(ds, kb, preferred_element_type=F32)
            return dk, dv

        dk, dv = lax.fori_loop(kb_i, nq, step, (jnp.zeros((tq, dqk), F32), jnp.zeros((tq, vd), F32)))
        dk_ref[...] = dk
        dv_ref[...] = dv
        if job:
            last = (pl.program_id(0) == nh - 1) & (kb_i == nq - 1)
            pl.when(last)(lambda: job.finish(job_in, job_out, sems))

    def full(n):
        return pl.BlockSpec((None, tp, n), lambda h, j: (h, 0, 0))

    def blk(n):
        return pl.BlockSpec((None, tq, n), lambda h, j: (h, j, 0))

    outs = pl.pallas_call(
        body, name=name, grid=(nh, nq),
        in_specs=[full(dqk), blk(dqk), blk(vd), full(vd), full(vd), full(1)] + [_ANY] * n_ji,
        out_specs=[full(dqk), blk(dqk), blk(vd)] + [_ANY] * n_jo,
        out_shape=[jax.ShapeDtypeStruct((nh, tp, dqk), F32), jax.ShapeDtypeStruct((nh, tp, dqk), F32),
                   jax.ShapeDtypeStruct((nh, tp, vd), F32)] + (list(job.out_shape) if job else []),
        scratch_shapes=list(job.scratch) if job else [],
        compiler_params=_cparams(("arbitrary", "arbitrary") if job else ("parallel", "arbitrary")),
    )(q, k, v, o, do, lse, *(job.inputs if job else ()))
    return outs[0], outs[1], outs[2], list(outs[3:])


def _mix_out_fwd(h, y_lru, o, g_lo, g_mo, w_lo, w_mo, g_post, layer, tm, name):
    tp, d = h.shape
    lw = y_lru.shape[-1]
    nh, vd = o.shape[0], o.shape[-1]

    def body(h_ref, yl_ref, o_ref, glo_ref, gmo_ref, wlo_ref, wmo_ref, gpost_ref, hout_ref, y_ref):
        nl, _ = _rms(yl_ref[...], glo_ref[...])
        y = _dot(nl, wlo_ref[...])
        ssq = jnp.zeros((tm, 1), F32)
        for hd in range(nh):
            oh = o_ref[hd]
            ssq += jnp.sum(oh * oh, axis=-1, keepdims=True)
        r = lax.rsqrt(ssq / (nh * vd) + EPS)
        for hd in range(nh):
            y += _dot(o_ref[hd] * r * gmo_ref[hd], wmo_ref[hd])
        y_ref[...] = y
        hout_ref[...] = h_ref[...] + _rms(y, gpost_ref[...])[0]

    tok = pl.BlockSpec((tm, d), lambda i: (i, 0))

    def lay(*shape):
        return pl.BlockSpec((None,) + shape, lambda i: (layer,) + (0,) * len(shape))

    return pl.pallas_call(
        body, name=name, grid=(tp // tm,),
        in_specs=[tok, pl.BlockSpec((tm, lw), lambda i: (i, 0)), pl.BlockSpec((nh, tm, vd), lambda i: (0, i, 0)),
                  lay(1, lw), lay(nh, 1, vd), _whole(lw, d), _whole(nh, vd, d), lay(1, d)],
        out_specs=[tok, tok],
        out_shape=[jax.ShapeDtypeStruct((tp, d), F32), jax.ShapeDtypeStruct((tp, d), F32)],
        compiler_params=_cparams(("parallel",)),
    )(h, y_lru, o, g_lo, g_mo, w_lo, w_mo, g_post)


def _mix_out_bwd(dh_out, y, y_lru, o, g_lo, g_mo, w_lo, w_mo, g_post, layer, tm, name):
    tp, d = y.shape
    lw = y_lru.shape[-1]
    nh, vd = o.shape[0], o.shape[-1]

    def body(dho_ref, y_ref, yl_ref, o_ref, glo_ref, gmo_ref, wlo_ref, wmo_ref, gpost_ref,
             dyl_ref, do_ref, dy_ref, nl_ref, nm_ref, dgpost_ref, dglo_ref, dgmo_ref):
        i = pl.program_id(0)

        @pl.when(i == 0)
        def _():
            dgpost_ref[...] = jnp.zeros_like(dgpost_ref)
            dglo_ref[...] = jnp.zeros_like(dglo_ref)
            dgmo_ref[...] = jnp.zeros_like(dgmo_ref)

        dy, dgp = _rms_bwd(y_ref[...], gpost_ref[...], dho_ref[...])
        dgpost_ref[...] += dgp
        dy_b = dy.astype(BF16)
        dy_ref[...] = dy_b
        yl = yl_ref[...]
        nl_ref[...] = _rms(yl, glo_ref[...])[0].astype(BF16)
        dyl, dgl = _rms_bwd(yl, glo_ref[...], _dot_nt(dy_b, wlo_ref[...]))
        dyl_ref[...] = dyl
        dglo_ref[...] += dgl
        ssq = jnp.zeros((tm, 1), F32)
        for hd in range(nh):
            oh = o_ref[hd]
            ssq += jnp.sum(oh * oh, axis=-1, keepdims=True)
        r = lax.rsqrt(ssq / (nh * vd) + EPS)
        dn = []
        dot_sum = jnp.zeros((tm, 1), F32)
        for hd in range(nh):
            oh = o_ref[hd] * r
            nm_ref[hd] = (oh * gmo_ref[hd]).astype(BF16)
            dn_h = _dot_nt(dy_b, wmo_ref[hd])
            dgmo_ref[hd] += jnp.sum(dn_h * oh, axis=0, keepdims=True)
            dn_h = dn_h * gmo_ref[hd]
            dot_sum += jnp.sum(dn_h * oh, axis=-1, keepdims=True)
            dn.append(dn_h)
        dot_mean = dot_sum / (nh * vd)
        for hd in range(nh):
            do_ref[hd] = r * (dn[hd] - o_ref[hd] * r * dot_mean)

    tok = pl.BlockSpec((tm, d), lambda i: (i, 0))
    tokl = pl.BlockSpec((tm, lw), lambda i: (i, 0))
    heads = pl.BlockSpec((nh, tm, vd), lambda i: (0, i, 0))

    def lay(*shape):
        return pl.BlockSpec((None,) + shape, lambda i: (layer,) + (0,) * len(shape))

    def acc(*shape):
        return pl.BlockSpec(shape, lambda i: (0,) * len(shape))

    return pl.pallas_call(
        body, name=name, grid=(tp // tm,),
        in_specs=[tok, tok, tokl, heads, lay(1, lw), lay(nh, 1, vd), _whole(lw, d), _whole(nh, vd, d), lay(1, d)],
        out_specs=[tokl, heads, tok, tokl, heads, acc(1, d), acc(1, lw), acc(nh, 1, vd)],
        out_shape=[jax.ShapeDtypeStruct((tp, lw), F32), jax.ShapeDtypeStruct((nh, tp, vd), F32),
                   jax.ShapeDtypeStruct((tp, d), BF16), jax.ShapeDtypeStruct((tp, lw), BF16),
                   jax.ShapeDtypeStruct((nh, tp, vd), BF16), jax.ShapeDtypeStruct((1, d), F32),
                   jax.ShapeDtypeStruct((1, lw), F32), jax.ShapeDtypeStruct((nh, 1, vd), F32)],
        compiler_params=_cparams(("arbitrary",)),
    )(dh_out, y, y_lru, o, g_lo, g_mo, w_lo, w_mo, g_post)


def _loss_head(h, target, n_real, tm, name):
    tp, d = h.shape
    nt = tp // tm

    def body(h_ref, t_ref, dh_ref, part_ref):
        i = pl.program_id(0)
        row = i * tm + lax.broadcasted_iota(jnp.int32, (tm, 1), 0)
        real = (row >= N_META) & (row < N_META + n_real)
        err = jnp.where(real, h_ref[...] - t_ref[...], 0.0)
        dh_ref[...] = err * (1.0 / d)
        part_ref[...] = jnp.full((1, 1, 128), 0.5 / d * jnp.sum(err * err), F32)

    tok = pl.BlockSpec((tm, d), lambda i: (i, 0))
    return pl.pallas_call(
        body, name=name, grid=(nt,), in_specs=[tok, tok],
        out_specs=[tok, pl.BlockSpec((1, 1, 128), lambda i: (i, 0, 0))],
        out_shape=[jax.ShapeDtypeStruct((tp, d), F32), jax.ShapeDtypeStruct((nt, 1, 128), F32)],
        compiler_params=_cparams(("parallel",)),
    )(h, target)


def _mesh_pos():
    return lax.axis_index("x"), lax.axis_index("y"), lax.axis_index("c")


def _other_chips(x, y):
    return [(1 - x, y), (x, 1 - y), (1 - x, 1 - y)]


class _GatherJob:
    def __init__(self, inputs, picks, shard_shapes, dtypes):
        self.inputs = list(inputs)
        self.picks = list(picks)
        n = len(self.inputs)
        self.out_shape = [jax.ShapeDtypeStruct((N_CHIPS,) + tuple(s), dt) for s, dt in zip(shard_shapes, dtypes)]
        self.halves = [s[0] // 2 for s in shard_shapes]
        self.scratch = [pltpu.SemaphoreType.DMA((n, 3)), pltpu.SemaphoreType.DMA((n, 3)), pltpu.SemaphoreType.DMA((n, 3)),
                        pltpu.SemaphoreType.DMA((n, 3)), pltpu.SemaphoreType.DMA((n,))]

    def _copies(self, ins, outs, sems):
        ici_s, ici_r, core_s, core_r, loc = sems
        x, y, c = _mesh_pos()
        me = 2 * x + y
        chips = _other_chips(x, y)
        local, ici, fwd = [], [], []
        for a in range(len(ins)):
            src = self.picks[a](ins[a])
            hv = self.halves[a]
            mine = pl.ds(c * hv, hv)
            local.append(pltpu.make_async_copy(src, outs[a].at[me], loc.at[a]))
            for p, (px, py) in enumerate(chips):
                ici.append((pltpu.make_async_remote_copy(src_ref=src.at[mine], dst_ref=outs[a].at[me, mine],
                                                         send_sem=ici_s.at[a, p], recv_sem=ici_r.at[a, p],
                                                         device_id=(px, py, c), device_id_type=MESH),
                            pltpu.make_async_remote_copy(src_ref=src.at[mine], dst_ref=outs[a].at[2 * px + py, mine],
                                                         send_sem=ici_s.at[a, p], recv_sem=ici_r.at[a, p],
                                                         device_id=(px, py, c), device_id_type=MESH)))
                landed = outs[a].at[2 * px + py, mine]
                theirs = outs[a].at[2 * px + py, pl.ds((1 - c) * hv, hv)]
                fwd.append((pltpu.make_async_remote_copy(src_ref=landed, dst_ref=landed, send_sem=core_s.at[a, p],
                                                         recv_sem=core_r.at[a, p], device_id=(x, y, 1 - c), device_id_type=MESH),
                            pltpu.make_async_remote_copy(src_ref=theirs, dst_ref=theirs, send_sem=core_s.at[a, p],
                                                         recv_sem=core_r.at[a, p], device_id=(x, y, 1 - c), device_id_type=MESH)))
        return local, ici, fwd

    def start(self, ins, outs, sems):
        local, ici, _ = self._copies(ins, outs, sems)
        for cp in local:
            cp.start()
        for send, _ in ici:
            send.start()

    def finish(self, ins, outs, sems):
        local, ici, fwd = self._copies(ins, outs, sems)
        for (_, arrive), (hand_on, _) in zip(ici, fwd):
            arrive.wait_recv()
            hand_on.start()
        for _, arrive in fwd:
            arrive.wait_recv()
        for send, _ in ici:
            send.wait_send()
        for hand_on, _ in fwd:
            hand_on.wait_send()
        for cp in local:
            cp.wait()


class _ReduceJob:
    def __init__(self, inputs):
        self.inputs = list(inputs)
        n = len(self.inputs)
        self.out_shape = [jax.ShapeDtypeStruct(a.shape, a.dtype) for a in self.inputs] * 2
        self.scratch = [pltpu.SemaphoreType.DMA((n, 3)), pltpu.SemaphoreType.DMA((n, 3)), pltpu.SemaphoreType.DMA((n, 4)),
                        pltpu.SemaphoreType.DMA((n, 4)), pltpu.SemaphoreType.DMA((n,))]

    def _copies(self, ins, outs, sems):
        ici_s, ici_r, core_s, core_r, loc = sems
        n = len(ins)
        same, other = outs[:n], outs[n:]
        x, y, c = _mesh_pos()
        me = 2 * x + y
        chips = _other_chips(x, y)
        sib = (x, y, 1 - c)
        local, ici, fwd = [], [], []
        for a in range(n):
            local.append(pltpu.make_async_copy(ins[a].at[me], same[a].at[me], loc.at[a]))
            fwd.append((pltpu.make_async_remote_copy(src_ref=ins[a].at[me], dst_ref=other[a].at[me], send_sem=core_s.at[a, 3],
                                                     recv_sem=core_r.at[a, 3], device_id=sib, device_id_type=MESH), None))
            for p, (px, py) in enumerate(chips):
                cp = 2 * px + py
                ici.append((pltpu.make_async_remote_copy(src_ref=ins[a].at[cp], dst_ref=same[a].at[me], send_sem=ici_s.at[a, p],
                                                         recv_sem=ici_r.at[a, p], device_id=(px, py, c), device_id_type=MESH),
                            pltpu.make_async_remote_copy(src_ref=ins[a].at[cp], dst_ref=same[a].at[cp], send_sem=ici_s.at[a, p],
                                                         recv_sem=ici_r.at[a, p], device_id=(px, py, c), device_id_type=MESH)))
                fwd.append((pltpu.make_async_remote_copy(src_ref=same[a].at[cp], dst_ref=other[a].at[cp], send_sem=core_s.at[a, p],
                                                         recv_sem=core_r.at[a, p], device_id=sib, device_id_type=MESH), p))
        return local, ici, fwd

    def start(self, ins, outs, sems):
        local, ici, fwd = self._copies(ins, outs, sems)
        for cp in local:
            cp.start()
        for send, _ in ici:
            send.start()
        for hand_on, p in fwd:
            if p is None:
                hand_on.start()

    def finish(self, ins, outs, sems):
        local, ici, fwd = self._copies(ins, outs, sems)
        chip_fwd = [f for f in fwd if f[1] is not None]
        for (_, arrive), (hand_on, _) in zip(ici, chip_fwd):
            arrive.wait_recv()
            hand_on.start()
        for hand_on, _ in fwd:
            hand_on.wait()
        for send, _ in ici:
            send.wait_send()
        for cp in local:
            cp.wait()


def _run_job(job, name):
    n_in, n_out = len(job.inputs), len(job.out_shape)

    def body(*refs):
        ins, outs, sems = refs[:n_in], refs[n_in:n_in + n_out], refs[n_in + n_out:]
        job.start(ins, outs, sems)
        job.finish(ins, outs, sems)

    return list(pl.pallas_call(body, name=name, in_specs=[_ANY] * n_in, out_specs=[_ANY] * n_out, out_shape=list(job.out_shape),
                               scratch_shapes=list(job.scratch))(*job.inputs))


def _allreduce_small(buf, name):
    rows, lanes = buf.shape
    n_dev = 8

    def body(in_ref, out_ref, gather, send_sems, recv_sems):
        x, y, c = _mesh_pos()
        me = 4 * x + 2 * y + c
        gather[me] = in_ref[...]
        sends = []
        for rel in range(1, n_dev):
            fx, fy, fc = (rel >> 2) & 1, (rel >> 1) & 1, rel & 1
            peer = (1 - x if fx else x, 1 - y if fy else y, 1 - c if fc else c)
            rc = pltpu.make_async_remote_copy(src_ref=in_ref, dst_ref=gather.at[me], send_sem=send_sems.at[rel - 1],
                                              recv_sem=recv_sems.at[rel - 1], device_id=peer, device_id_type=MESH)
            rc.start()
            sends.append(rc)
        for rc in sends:
            rc.wait()
        total = gather[0]
        for k in range(1, n_dev):
            total = total + gather[k]
        out_ref[...] = total

    vm = pl.BlockSpec(memory_space=pltpu.VMEM)
    return pl.pallas_call(
        body, name=name, in_specs=[vm], out_specs=vm, out_shape=jax.ShapeDtypeStruct((rows, lanes), F32),
        scratch_shapes=[pltpu.VMEM((n_dev, rows, lanes), F32), pltpu.SemaphoreType.DMA((n_dev - 1,)),
                        pltpu.SemaphoreType.DMA((n_dev - 1,))],
        compiler_params=pltpu.CompilerParams(vmem_limit_bytes=V7X_VMEM_LIMIT_BYTES),
    )(buf)


def _row_tile(rows, limit=512):
    best = None
    for t in range(16, min(rows, limit) + 1, 16):
        if rows % t == 0:
            best = t
    return best if best is not None else rows


def _adamw_math(w, g, m, v):
    m = ADAM_B1 * m + (1.0 - ADAM_B1) * g
    v = ADAM_B2 * v + (1.0 - ADAM_B2) * (g * g)
    m_hat = m / (1.0 - ADAM_B1 ** ADAM_STEP)
    v_hat = v / (1.0 - ADAM_B2 ** ADAM_STEP)
    delta = -ADAM_LR * (m_hat / (jnp.sqrt(v_hat) + ADAM_EPS) + ADAM_WD * w)
    return delta, m, v


def _adamw_layer(w, m, v, same_plane, other_plane, layer, prev, name):
    nl, r, ccol = w.shape
    ns = same_plane.shape[0]
    tr = _row_tile(r, 256)

    def body(w_ref, m_ref, v_ref, a_ref, b_ref, *rest):
        go_ref, d_ref, mo_ref, vo_ref = rest[-4:]
        sa = a_ref[0].astype(F32)
        sb = b_ref[0].astype(F32)
        for k in range(1, ns):
            sa = sa + a_ref[k].astype(F32)
            sb = sb + b_ref[k].astype(F32)
        g = sa + sb
        go_ref[...] = g
        d_ref[...], mo_ref[...], vo_ref[...] = _adamw_math(w_ref[...], g, m_ref[...], v_ref[...])

    blk = pl.BlockSpec((None, tr, ccol), lambda i: (layer, i, 0))
    plane = pl.BlockSpec((ns, tr, ccol), lambda i: (0, i, 0))
    out = jax.ShapeDtypeStruct((nl, r, ccol), F32)
    n_prev = 4 if prev is not None else 0
    return pl.pallas_call(
        body, name=name, grid=(r // tr,),
        in_specs=[blk, blk, blk, plane, plane] + [_ANY] * n_prev,
        out_specs=[blk, blk, blk, blk], out_shape=[out, out, out, out],
        input_output_aliases={5 + k: k for k in range(n_prev)},
        compiler_params=_cparams(("parallel",)))(w, m, v, same_plane, other_plane, *(prev or ()))


def _adamw_whole(w, g, m, v, name):
    def body(w_ref, g_ref, m_ref, v_ref, d_ref, mo_ref, vo_ref):
        d_ref[...], mo_ref[...], vo_ref[...] = _adamw_math(w_ref[...], g_ref[...], m_ref[...], v_ref[...])

    out = jax.ShapeDtypeStruct(w.shape, F32)
    return pl.pallas_call(body, name=name, out_shape=[out, out, out],
                          compiler_params=pltpu.CompilerParams(vmem_limit_bytes=V7X_VMEM_LIMIT_BYTES))(w, g, m, v)


_W_NAMES = ['meta_tokens', 'ffn1_pre_g', 'ffn1_w_gate', 'ffn1_w_up', 'ffn1_w_down', 'ffn1_post_g', 'mix_pre_g', 'w_in',
            'lru_conv_w', 'lru_conv_b', 'lru_w_a', 'lru_b_a', 'lru_w_x', 'lru_b_x', 'lru_lambda', 'mla_q_norm_g', 'mla_w_uq',
            'mla_kv_norm_g', 'mla_w_ukv', 'lru_out_g', 'mla_out_g', 'w_out', 'mix_post_g', 'ffn2_pre_g', 'ffn2_w_gate',
            'ffn2_w_up', 'ffn2_w_down', 'ffn2_post_g']
_REPLICATED = ['ffn1_pre_g', 'ffn1_post_g', 'mix_pre_g', 'lru_conv_b', 'lru_w_a', 'lru_b_a', 'lru_w_x', 'lru_b_x', 'lru_lambda',
               'mla_q_norm_g', 'mla_kv_norm_g', 'lru_out_g', 'mla_out_g', 'mix_post_g', 'ffn2_pre_g', 'ffn2_post_g']
_LANES = 128


def _pack_rows(arrays):
    flat = jnp.concatenate([a.reshape(-1) for a in arrays])
    total = _round_up(flat.shape[0], 8 * _LANES)
    return jnp.pad(flat, (0, total - flat.shape[0])).reshape(-1, _LANES)


def _unpack_rows(buf, shapes):
    flat = buf.reshape(-1)
    out, off = [], 0
    for shp in shapes:
        n = math.prod(shp)
        out.append(flat[off:off + n].reshape(shp))
        off += n
    return out


def _block_diag(w):
    nl, nh, n, _ = w.shape
    eye = jnp.eye(nh, dtype=w.dtype)
    return (w[:, :, :, None, :] * eye[None, :, None, :, None]).reshape(nl, nh * n, nh * n)


def _diag_blocks(bd, nh):
    n = bd.shape[0] // nh
    b4 = bd.reshape(nh, n, nh, n)
    return jnp.stack([b4[i, :, i, :] for i in range(nh)])


def kernel(x, meta_tokens, ffn1_pre_g, ffn1_w_gate, ffn1_w_up, ffn1_w_down, ffn1_post_g, mix_pre_g, w_in, lru_conv_w, lru_conv_b, lru_w_a, lru_b_a, lru_w_x, lru_b_x, lru_lambda, mla_q_norm_g, mla_w_uq, mla_kv_norm_g, mla_w_ukv, lru_out_g, mla_out_g, w_out, mix_post_g, ffn2_pre_g, ffn2_w_gate, ffn2_w_up, ffn2_w_down, ffn2_post_g, loss_target, m_meta_tokens, m_ffn1_pre_g, m_ffn1_w_gate, m_ffn1_w_up, m_ffn1_w_down, m_ffn1_post_g, m_mix_pre_g, m_w_in, m_lru_conv_w, m_lru_conv_b, m_lru_w_a, m_lru_b_a, m_lru_w_x, m_lru_b_x, m_lru_lambda, m_mla_q_norm_g, m_mla_w_uq, m_mla_kv_norm_g, m_mla_w_ukv, m_lru_out_g, m_mla_out_g, m_w_out, m_mix_post_g, m_ffn2_pre_g, m_ffn2_w_gate, m_ffn2_w_up, m_ffn2_w_down, m_ffn2_post_g, v_meta_tokens, v_ffn1_pre_g, v_ffn1_w_gate, v_ffn1_w_up, v_ffn1_w_down, v_ffn1_post_g, v_mix_pre_g, v_w_in, v_lru_conv_w, v_lru_conv_b, v_lru_w_a, v_lru_b_a, v_lru_w_x, v_lru_b_x, v_lru_lambda, v_mla_q_norm_g, v_mla_w_uq, v_mla_kv_norm_g, v_mla_w_ukv, v_lru_out_g, v_mla_out_g, v_w_out, v_mix_post_g, v_ffn2_pre_g, v_ffn2_w_gate, v_ffn2_w_up, v_ffn2_w_down, v_ffn2_post_g):
    a = dict(locals())
    x2 = x[0]
    s_len, d = x2.shape
    nl = ffn1_pre_g.shape[0]
    lw = lru_conv_b.shape[-1]
    ql, kvl = mla_q_norm_g.shape[-1], mla_kv_norm_g.shape[-1]
    nh = MLA_HEADS
    vd = (d - lw) // nh
    t_real = N_META + s_len
    tp = _round_up(t_real, 384)
    tm = tp // 6
    tmb = tm // 2
    tq = 384 if tp >= 1536 else 128
    xi, yi, _ = _mesh_pos()
    chip = 2 * xi + yi

    gu_loc = jnp.stack([ffn1_w_gate, ffn1_w_up, ffn2_w_gate, ffn2_w_up], axis=1).astype(BF16)
    dn_loc = jnp.stack([ffn1_w_down, ffn2_w_down], axis=1).astype(BF16)
    loc = [gu_loc, dn_loc, w_in.astype(BF16), mla_w_uq.astype(BF16), mla_w_ukv.astype(BF16), w_out.astype(BF16)]
    loc_shapes, loc_dtypes = [t.shape[1:] for t in loc], [t.dtype for t in loc]

    def layer_gather(l):
        return _GatherJob(loc, [lambda r, l=l: r.at[l]] * len(loc), loc_shapes, loc_dtypes)

    first = _GatherJob(loc + [lru_conv_w, meta_tokens], [lambda r: r.at[0]] * len(loc) + [lambda r: r] * 2,
                       loc_shapes + [lru_conv_w.shape, meta_tokens.shape], loc_dtypes + [F32, F32])
    *gathered, g_conv, g_meta = _run_job(first, "gather_layer0")
    o0, o1, o2, o3 = lw, 2 * lw, 2 * lw + ql, 2 * lw + ql + kvl

    def cols_full(g):
        return jnp.transpose(g, (1, 0, 2)).reshape(g.shape[1], -1)

    def assemble(gw):
        w_gu, w_dn, g_win, g_wuq, g_wukv, g_wout = gw
        w_in_full = cols_full(g_win)
        w_kr = jnp.pad(w_in_full[:, o3:], ((0, 0), (QK_NOPE, 0)))
        w_ukv3 = cols_full(g_wukv).reshape(kvl, nh, QK_NOPE + vd)
        w_out_full = g_wout.reshape(d, d)
        return dict(gu=w_gu, dn=w_dn,
                    parts=(w_in_full[:, :o0], w_in_full[:, o0:o1], w_in_full[:, o1:o2], w_in_full[:, o2:o3], w_kr),
                    q=jnp.transpose(cols_full(g_wuq).reshape(ql, nh, QK_DIM), (1, 0, 2)),
                    k=jnp.transpose(jnp.pad(w_ukv3[..., :QK_NOPE], ((0, 0), (0, 0), (0, QK_ROPE))), (1, 0, 2)),
                    v=jnp.transpose(w_ukv3[..., QK_NOPE:], (1, 0, 2)),
                    lo=w_out_full[:lw], mo=w_out_full[lw:].reshape(nh, vd, d))

    conv_full = jnp.transpose(g_conv, (1, 2, 0, 3)).reshape(nl, CONV_WIDTH, lw)
    meta_full = jnp.transpose(g_meta, (1, 0, 2)).reshape(N_META, d)
    wa_bd = _block_diag(lru_w_a).astype(BF16)
    wx_bd = _block_diag(lru_w_x).astype(BF16)

    def gain3(g):
        return g.reshape(nl, 1, g.shape[-1])

    g3 = {n: gain3(a[n]) for n in ('ffn1_pre_g', 'ffn1_post_g', 'mix_pre_g', 'lru_conv_b', 'lru_b_a', 'lru_b_x', 'lru_lambda',
                                   'mla_q_norm_g', 'mla_kv_norm_g', 'lru_out_g', 'mix_post_g', 'ffn2_pre_g', 'ffn2_post_g')}
    g_mo = mla_out_g.reshape(nl, nh, 1, vd)

    pos = jnp.arange(tp, dtype=F32)
    inv_freq = 1.0 / (ROPE_THETA ** (jnp.arange(0, QK_ROPE, 2, dtype=F32) / QK_ROPE))
    ang = pos[:, None] * inv_freq[None, :]
    cos_t = jnp.concatenate([jnp.ones((tp, QK_NOPE), F32), jnp.cos(ang), jnp.cos(ang)], axis=1)
    sin_t = jnp.concatenate([jnp.zeros((tp, QK_NOPE), F32), jnp.sin(ang), jnp.sin(ang)], axis=1)
    half = QK_ROPE // 2
    rot = np.zeros((QK_DIM, QK_DIM), np.float32)
    for i in range(half):
        rot[QK_NOPE + half + i, QK_NOPE + i] = -1.0
        rot[QK_NOPE + i, QK_NOPE + half + i] = 1.0
    rot_b, rot_t = jnp.asarray(rot, BF16), jnp.asarray(rot.T, BF16)

    h = jnp.concatenate([meta_full, x2, jnp.zeros((tp - t_real, d), F32)], axis=0)
    target = jnp.pad(loss_target[0], ((N_META, tp - t_real), (0, 0)))
    saved, weights = [], []
    for l in range(nl):
        wl = assemble(gathered)
        weights.append(wl)
        h1, gate1, up1, f1 = _ffn_fwd(h, g3['ffn1_pre_g'], g3['ffn1_post_g'], wl['gu'], wl['dn'], l, 0, 1, 0, tm, f"ffn1_fwd_{l}")
        xr, gr, cq, ckv, q, k, v = _mix_in_fwd(h1, g3['mix_pre_g'], wl['parts'], g3['mla_q_norm_g'], g3['mla_kv_norm_g'],
                                               wl['q'], wl['k'], wl['v'], cos_t, sin_t, rot_b, l, tm, f"mix_in_fwd_{l}")
        y_lru, xc, hs = _lru_fwd(xr, gr, conv_full, g3['lru_conv_b'], wa_bd, wx_bd, g3['lru_b_a'], g3['lru_b_x'],
                                 g3['lru_lambda'], l, tm, f"lru_fwd_{l}")
        o, lse, gathered = _attn_fwd(q, k, v, tq, f"attn_fwd_{l}", layer_gather(l + 1) if l + 1 < nl else None)
        h2, y = _mix_out_fwd(h1, y_lru, o, g3['lru_out_g'], g_mo, wl['lo'], wl['mo'], g3['mix_post_g'], l, tm, f"mix_out_fwd_{l}")
        h3, gate2, up2, f2 = _ffn_fwd(h2, g3['ffn2_pre_g'], g3['ffn2_post_g'], wl['gu'], wl['dn'], l, 2, 3, 1, tm, f"ffn2_fwd_{l}")
        saved.append((h, gate1, up1, f1, h1, xr, gr, cq, ckv, q, k, v, y_lru, xc, hs, o, lse, y, h2, gate2, up2, f2))
        h = h3

    dh, loss_parts = _loss_head(h, target, s_len, tm, "loss_head")
    loss = lax.psum(jnp.sum(loss_parts[:, 0, 0]), ("x", "y", "c"))

    small = {n: [None] * nl for n in _REPLICATED + ['lru_conv_w']}
    planes = {}
    late_names = ['w_in', 'mla_w_uq', 'mla_w_ukv', 'ffn1_w_gate', 'ffn1_w_up', 'ffn1_w_down']
    early_names = ['ffn2_w_gate', 'ffn2_w_up', 'ffn2_w_down', 'w_out']
    late = None

    def keep_planes(keys, outs):
        for i, key in enumerate(keys):
            planes[key] = (outs[i], outs[len(keys) + i])

    for l in reversed(range(nl)):
        (h0, gate1, up1, f1, h1, xr, gr, cq, ckv, q, k, v, y_lru, xc, hs, o, lse, y, h2, gate2, up2, f2) = saved[l]
        wl = weights[l]
        dh, dgate, dup, act, df, u, dgpre, dgpost = _ffn_bwd(dh, h2, f2, gate2, up2, g3['ffn2_pre_g'], g3['ffn2_post_g'],
                                                            wl['gu'], wl['dn'], l, 2, 3, 1, tmb, f"ffn2_bwd_{l}")
        small['ffn2_pre_g'][l], small['ffn2_post_g'][l] = dgpre, dgpost
        d_wg2 = _mm_tn(u, dgate, tm, BF16, f"dw_gate2_{l}")
        d_wu2 = _mm_tn(u, dup, tm, BF16, f"dw_up2_{l}")
        dn2 = _mm_tn(act, df, tm, BF16, f"dw_down2_{l}")

        d_ylru, d_o, dy_b, n_lo, n_mo, dgpost, dglo, dgmo = _mix_out_bwd(dh, y, y_lru, o, g3['lru_out_g'], g_mo, wl['lo'], wl['mo'],
                                                                       g3['mix_post_g'], l, tm, f"mix_out_bwd_{l}")
        small['mix_post_g'][l], small['lru_out_g'][l], small['mla_out_g'][l] = dgpost, dglo, dgmo
        d_wlo = _mm_tn(n_lo, dy_b, tm, F32, f"dw_out_lru_{l}")
        d_wmo = _mm_tn(n_mo, dy_b, tm, F32, f"dw_out_mla_{l}")
        d_wout = jnp.concatenate([d_wlo, d_wmo.reshape(nh * vd, d)], axis=0).reshape(N_CHIPS, d // N_CHIPS, d).astype(BF16)

        keys = [(n, l) for n in early_names] + ([(n, l + 1) for n in late_names] if late else [])
        dq, dk, dv, outs = _attn_bwd(q, k, v, o, d_o, lse, tq, f"attn_bwd_{l}",
                                     _ReduceJob([d_wg2, d_wu2, dn2, d_wout] + (late or [])))
        keep_planes(keys, outs)
        d_xr, d_gr, dcw, dcb, dwa, dwx, dba, dbx, dlam = _lru_bwd(d_ylru, xr, gr, xc, hs, conv_full, wa_bd, wx_bd, g3['lru_b_a'],
                                                                 g3['lru_b_x'], g3['lru_lambda'], l, tm, f"lru_bwd_{l}")
        small['lru_conv_w'][l], small['lru_conv_b'][l] = dcw, dcb
        small['lru_w_a'][l], small['lru_w_x'][l] = _diag_blocks(dwa, LRU_HEADS), _diag_blocks(dwx, LRU_HEADS)
        small['lru_b_a'][l], small['lru_b_x'][l], small['lru_lambda'][l] = dba, dbx, dlam

        dh, u, qn, kvn, dq_pre, d_cq, d_ckv, d_kr, dgpre, dgq, dgkv = _mix_in_bwd(
            dh, h1, cq, ckv, d_xr, d_gr, dq, dk, dv, g3['mix_pre_g'], wl['parts'], g3['mla_q_norm_g'], g3['mla_kv_norm_g'],
            wl['q'], wl['k'], wl['v'], cos_t, sin_t, rot_t, l, tmb, f"mix_in_bwd_{l}")
        small['mix_pre_g'][l], small['mla_q_norm_g'][l], small['mla_kv_norm_g'][l] = dgpre, dgq, dgkv
        d_win = jnp.concatenate([_mm_tn(u, d_xr, tm, F32, f"dw_in_xr_{l}"), _mm_tn(u, d_gr, tm, F32, f"dw_in_gr_{l}"),
                                 _mm_tn(u, d_cq, tm, F32, f"dw_in_cq_{l}"), _mm_tn(u, d_ckv, tm, F32, f"dw_in_ckv_{l}"),
                                 _mm_tn(u, d_kr, tm, F32, f"dw_in_kr_{l}")[:, QK_NOPE:]], axis=1)
        d_win = jnp.transpose(d_win.reshape(d, N_CHIPS, -1), (1, 0, 2)).astype(BF16)
        d_wq = _mm_tn(qn, dq_pre, tm, F32, f"dw_uq_{l}")
        d_wuq = jnp.transpose(jnp.transpose(d_wq, (1, 0, 2)).reshape(ql, N_CHIPS, -1), (1, 0, 2)).astype(BF16)
        d_wk = _mm_tn(kvn, dk, tm, F32, f"dw_uk_{l}")[..., :QK_NOPE]
        d_wv = _mm_tn(kvn, dv, tm, F32, f"dw_uv_{l}")
        d_wukv = jnp.transpose(jnp.concatenate([d_wk, d_wv], axis=-1), (1, 0, 2))
        d_wukv = jnp.transpose(d_wukv.reshape(kvl, N_CHIPS, -1), (1, 0, 2)).astype(BF16)

        dh, dgate, dup, act, df, u, dgpre, dgpost = _ffn_bwd(dh, h0, f1, gate1, up1, g3['ffn1_pre_g'], g3['ffn1_post_g'],
                                                            wl['gu'], wl['dn'], l, 0, 1, 0, tmb, f"ffn1_bwd_{l}")
        small['ffn1_pre_g'][l], small['ffn1_post_g'][l] = dgpre, dgpost
        d_wg1 = _mm_tn(u, dgate, tm, BF16, f"dw_gate1_{l}")
        d_wu1 = _mm_tn(u, dup, tm, BF16, f"dw_up1_{l}")
        dn1 = _mm_tn(act, df, tm, BF16, f"dw_down1_{l}")
        late = [d_win, d_wuq, d_wukv, d_wg1, d_wu1, dn1]

    grad_x = dh[N_META:t_real][None]
    keep_planes([(n, 0) for n in late_names], _run_job(_ReduceJob(late), "reduce_last"))

    res = {}
    for name in early_names + late_names:
        prev = None
        for l in reversed(range(nl)):
            same_plane, other_plane = planes[(name, l)]
            prev = _adamw_layer(a[name], a['m_' + name], a['v_' + name], same_plane, other_plane, l, prev, f"adamw_{name}_{l}")
        res[name] = prev

    small_full = {n: jnp.stack(small[n]).reshape(a[n].shape if n != 'lru_conv_w' else conv_full.shape)
                  for n in _REPLICATED + ['lru_conv_w']}
    order = _REPLICATED + ['lru_conv_w']
    packed = _pack_rows([small_full[n] for n in order] + [dh[:N_META]])
    summed = _allreduce_small(packed, "allreduce_small")
    pieces = _unpack_rows(summed, [small_full[n].shape for n in order] + [(N_META, d)])
    g_small = dict(zip(order, pieces[:-1]))
    g_conv_loc = lax.dynamic_slice_in_dim(g_small.pop('lru_conv_w'), chip * lru_conv_w.shape[-1], lru_conv_w.shape[-1], axis=2)
    g_meta_loc = lax.dynamic_slice_in_dim(pieces[-1], chip * meta_tokens.shape[-1], meta_tokens.shape[-1], axis=1)

    shapes = [a[n].shape for n in _REPLICATED]
    dl, mo, vo = _adamw_whole(_pack_rows([a[n] for n in _REPLICATED]), _pack_rows([g_small[n] for n in _REPLICATED]),
                              _pack_rows([a['m_' + n] for n in _REPLICATED]), _pack_rows([a['v_' + n] for n in _REPLICATED]),
                              "adamw_replicated")
    for n, dd, mm, vv in zip(_REPLICATED, _unpack_rows(dl, shapes), _unpack_rows(mo, shapes), _unpack_rows(vo, shapes)):
        res[n] = (g_small[n], dd, mm, vv)
    cshape = lru_conv_w.shape
    c2 = (cshape[0] * cshape[1], cshape[2])
    dd, mm, vv = _adamw_whole(lru_conv_w.reshape(c2), g_conv_loc.reshape(c2), m_lru_conv_w.reshape(c2), v_lru_conv_w.reshape(c2),
                              "adamw_conv_w")
    res['lru_conv_w'] = (g_conv_loc, dd.reshape(cshape), mm.reshape(cshape), vv.reshape(cshape))
    res['meta_tokens'] = (g_meta_loc,) + tuple(_adamw_whole(meta_tokens, g_meta_loc, m_meta_tokens, v_meta_tokens, "adamw_meta"))

    return (loss, grad_x, *[res[n][0] for n in _W_NAMES], *[res[n][1] for n in _W_NAMES],
            *[res[n][2] for n in _W_NAMES], *[res[n][3] for n in _W_NAMES])
```

```python
import functools
import math

import jax
import jax.numpy as jnp
import numpy as np
from jax import lax
from jax.experimental import pallas as pl
from jax.experimental.pallas import tpu as pltpu

F32 = jnp.float32
BF16 = jnp.bfloat16
MESH = pl.DeviceIdType.MESH

EPS = 1e-6
N_META = 16
LRU_HEADS = 8
MLA_HEADS = 8
QK_NOPE = 64
QK_ROPE = 32
QK_DIM = QK_NOPE + QK_ROPE
LRU_C = 8.0
ROPE_THETA = 10000.0
CONV_WIDTH = 4
N_CHIPS = 4

ADAM_LR = 0.001
ADAM_B1 = 0.9
ADAM_B2 = 0.999
ADAM_EPS = 1e-08
ADAM_WD = 0.01
ADAM_STEP = 10

V7X_VMEM_LIMIT_BYTES = 56 * 1024 * 1024
NEG_BIG = -1e30
Q_PRESCALE = QK_DIM ** -0.5 * math.log2(math.e)


def _cparams(sem=None):
    return pltpu.CompilerParams(dimension_semantics=sem, vmem_limit_bytes=V7X_VMEM_LIMIT_BYTES)


def _round_up(a, b):
    return -(-a // b) * b


def _whole(*shape):
    return pl.BlockSpec(shape, lambda *_: (0,) * len(shape))


_ANY = pl.BlockSpec(memory_space=pl.ANY)


def _dot(a, b):
    return jnp.dot(a.astype(BF16), b.astype(BF16), preferred_element_type=F32)


def _dot_nt(a, b):
    return lax.dot_general(a.astype(BF16), b.astype(BF16), (((1,), (1,)), ((), ())), preferred_element_type=F32)


def _dot_tn(a, b):
    return lax.dot_general(a.astype(BF16), b.astype(BF16), (((0,), (0,)), ((), ())), preferred_element_type=F32)


def _dot_split(x, p):
    hi = x.astype(BF16)
    lo = (x - hi.astype(F32)).astype(BF16)
    return jnp.dot(hi, p, preferred_element_type=F32) + jnp.dot(lo, p, preferred_element_type=F32)


def _rms(x, g):
    r = lax.rsqrt(jnp.mean(x * x, axis=-1, keepdims=True) + EPS)
    return x * r * g, r


def _rms_bwd(x, g, dy):
    r = lax.rsqrt(jnp.mean(x * x, axis=-1, keepdims=True) + EPS)
    xh = x * r
    dyg = dy * g
    dx = r * (dyg - xh * jnp.mean(dyg * xh, axis=-1, keepdims=True))
    dg = jnp.sum(dy * xh, axis=0, keepdims=True)
    return dx, dg


def _sigmoid(x):
    return 1.0 / (1.0 + jnp.exp(-x))


def _neg_expm1(x):
    series = -x * (1.0 + x * (0.5 + x * (1.0 / 6.0 + x * (1.0 / 24.0))))
    return jnp.where(jnp.abs(x) < 0.03, series, 1.0 - jnp.exp(x))


_GELU_K = math.sqrt(2.0 / math.pi)


def _gelu(x):
    t = jnp.tanh(_GELU_K * (x + 0.044715 * x * x * x))
    return 0.5 * x * (1.0 + t), t


def _gelu_grad(x, t):
    return 0.5 * (1.0 + t) + 0.5 * x * (1.0 - t * t) * _GELU_K * (1.0 + 3.0 * 0.044715 * x * x)


def _rope(x, cos_t, sin_t, rot):
    return x * cos_t + _dot_split(x, rot) * sin_t


def _ffn_fwd(h, g_pre, g_post, w_gu, w_dn, layer, kind_gate, kind_up, kind_dn, tm, name):
    tp, d = h.shape
    n_slot, fs = w_gu.shape[0], w_gu.shape[-1]

    def body(h_ref, gpre_ref, gpost_ref, wg_ref, wu_ref, wd_ref, hout_ref, gate_ref, up_ref, f_ref, u_sc, acc_sc):
        j = pl.program_id(1)

        @pl.when(j == 0)
        def _():
            u, _ = _rms(h_ref[...], gpre_ref[...])
            u_sc[...] = u.astype(BF16)
            acc_sc[...] = jnp.zeros_like(acc_sc)

        u = u_sc[...]
        gate = jnp.dot(u, wg_ref[...], preferred_element_type=F32)
        up = jnp.dot(u, wu_ref[...], preferred_element_type=F32)
        act = gate * _sigmoid(gate) * up
        gate_ref[...] = gate.astype(BF16)
        up_ref[...] = up.astype(BF16)
        acc_sc[...] += jnp.dot(act.astype(BF16), wd_ref[...], preferred_element_type=F32)

        @pl.when(j == n_slot - 1)
        def _():
            f = acc_sc[...]
            f_ref[...] = f
            n, _ = _rms(f, gpost_ref[...])
            hout_ref[...] = h_ref[...] + 0.5 * n

    tok = pl.BlockSpec((tm, d), lambda i, j: (i, 0))
    gain = pl.BlockSpec((None, 1, d), lambda i, j: (layer, 0, 0))
    slot_act = pl.BlockSpec((None, tm, fs), lambda i, j: (j, i, 0))
    return pl.pallas_call(
        body, name=name, grid=(tp // tm, n_slot),
        in_specs=[tok, gain, gain,
                  pl.BlockSpec((None, None, d, fs), lambda i, j: (j, kind_gate, 0, 0)),
                  pl.BlockSpec((None, None, d, fs), lambda i, j: (j, kind_up, 0, 0)),
                  pl.BlockSpec((None, None, fs, d), lambda i, j: (j, kind_dn, 0, 0))],
        out_specs=[tok, slot_act, slot_act, tok],
        out_shape=[jax.ShapeDtypeStruct((tp, d), F32), jax.ShapeDtypeStruct((n_slot, tp, fs), BF16),
                   jax.ShapeDtypeStruct((n_slot, tp, fs), BF16), jax.ShapeDtypeStruct((tp, d), F32)],
        scratch_shapes=[pltpu.VMEM((tm, d), BF16), pltpu.VMEM((tm, d), F32)],
        compiler_params=_cparams(("parallel", "arbitrary")),
    )(h, g_pre, g_post, w_gu, w_gu, w_dn)


def _ffn_bwd(dh_out, h, f, gate, up, g_pre, g_post, w_gu, w_dn, layer, kind_gate, kind_up, kind_dn, tm, name):
    tp, d = h.shape
    n_slot, fs = w_gu.shape[0], w_gu.shape[-1]

    def body(dho_ref, h_ref, f_ref, gate_ref, up_ref, gpre_ref, gpost_ref, wg_ref, wu_ref, wd_ref,
             dh_ref, dgate_ref, dup_ref, act_ref, df_ref, u_ref, dgpre_ref, dgpost_ref, du_sc):
        i, j = pl.program_id(0), pl.program_id(1)

        @pl.when((i == 0) & (j == 0))
        def _():
            dgpre_ref[...] = jnp.zeros_like(dgpre_ref)
            dgpost_ref[...] = jnp.zeros_like(dgpost_ref)

        @pl.when(j == 0)
        def _():
            df, dg = _rms_bwd(f_ref[...], gpost_ref[...], 0.5 * dho_ref[...])
            df_ref[...] = df.astype(BF16)
            dgpost_ref[...] += dg
            u, _ = _rms(h_ref[...], gpre_ref[...])
            u_ref[...] = u.astype(BF16)
            du_sc[...] = jnp.zeros_like(du_sc)

        g = gate_ref[...].astype(F32)
        u_ = up_ref[...].astype(F32)
        sg = _sigmoid(g)
        silu = g * sg
        dact = _dot_nt(df_ref[...], wd_ref[...])
        dup = dact * silu
        dgate = dact * u_ * (sg * (1.0 + g * (1.0 - sg)))
        act_ref[...] = (silu * u_).astype(BF16)
        dup_b = dup.astype(BF16)
        dgate_b = dgate.astype(BF16)
        dup_ref[...] = dup_b
        dgate_ref[...] = dgate_b
        du_sc[...] += _dot_nt(dgate_b, wg_ref[...]) + _dot_nt(dup_b, wu_ref[...])

        @pl.when(j == n_slot - 1)
        def _():
            dx, dg = _rms_bwd(h_ref[...], gpre_ref[...], du_sc[...])
            dh_ref[...] = dho_ref[...] + dx
            dgpre_ref[...] += dg

    tok = pl.BlockSpec((tm, d), lambda i, j: (i, 0))
    gain = pl.BlockSpec((None, 1, d), lambda i, j: (layer, 0, 0))
    acc = pl.BlockSpec((1, d), lambda i, j: (0, 0))
    slot_act = pl.BlockSpec((None, tm, fs), lambda i, j: (j, i, 0))
    act_shape = jax.ShapeDtypeStruct((n_slot, tp, fs), BF16)
    return pl.pallas_call(
        body, name=name, grid=(tp // tm, n_slot),
        in_specs=[tok, tok, tok, slot_act, slot_act, gain, gain,
                  pl.BlockSpec((None, None, d, fs), lambda i, j: (j, kind_gate, 0, 0)),
                  pl.BlockSpec((None, None, d, fs), lambda i, j: (j, kind_up, 0, 0)),
                  pl.BlockSpec((None, None, fs, d), lambda i, j: (j, kind_dn, 0, 0))],
        out_specs=[tok, slot_act, slot_act, slot_act, tok, tok, acc, acc],
        out_shape=[jax.ShapeDtypeStruct((tp, d), F32), act_shape, act_shape, act_shape,
                   jax.ShapeDtypeStruct((tp, d), BF16), jax.ShapeDtypeStruct((tp, d), BF16),
                   jax.ShapeDtypeStruct((1, d), F32), jax.ShapeDtypeStruct((1, d), F32)],
        scratch_shapes=[pltpu.VMEM((tm, d), F32)],
        compiler_params=_cparams(("arbitrary", "arbitrary")),
    )(dh_out, h, f, gate, up, g_pre, g_post, w_gu, w_gu, w_dn)


def _mm_tn(a, b, tk, out_dtype, name):
    ga = a.shape[0] if a.ndim == 3 else None
    gb = b.shape[0] if b.ndim == 3 else None
    groups = ga or gb or 1
    t, m = a.shape[-2:]
    n = b.shape[-1]
    nk = t // tk

    def body(a_ref, b_ref, o_ref, acc_sc):
        k = pl.program_id(1)

        @pl.when(k == 0)
        def _():
            acc_sc[...] = jnp.zeros_like(acc_sc)

        acc_sc[...] += _dot_tn(a_ref[...], b_ref[...])

        @pl.when(k == nk - 1)
        def _():
            o_ref[...] = acc_sc[...].astype(out_dtype)

    a_spec = (pl.BlockSpec((None, tk, m), lambda g, k: (g, k, 0)) if ga else pl.BlockSpec((tk, m), lambda g, k: (k, 0)))
    b_spec = (pl.BlockSpec((None, tk, n), lambda g, k: (g, k, 0)) if gb else pl.BlockSpec((tk, n), lambda g, k: (k, 0)))
    out = pl.pallas_call(
        body, name=name, grid=(groups, nk),
        in_specs=[a_spec, b_spec],
        out_specs=pl.BlockSpec((None, m, n), lambda g, k: (g, 0, 0)),
        out_shape=jax.ShapeDtypeStruct((groups, m, n), out_dtype),
        scratch_shapes=[pltpu.VMEM((m, n), F32)],
        compiler_params=_cparams(("parallel", "arbitrary")),
    )(a, b)
    return out if (ga or gb) else out[0]


def _mix_in_fwd(h, g_pre, w_parts, g_q, g_kv, w_q, w_k, w_v, cos_t, sin_t, rot, layer, tm, name):
    tp, d = h.shape
    w_xr, w_gr, w_cq, w_ckv, w_kr = w_parts
    lw, ql, kvl = w_xr.shape[-1], w_cq.shape[-1], w_ckv.shape[-1]
    nh, vd = w_v.shape[0], w_v.shape[-1]

    def body(h_ref, gpre_ref, wxr_ref, wgr_ref, wcq_ref, wckv_ref, wkr_ref, gq_ref, gkv_ref, wq_ref, wk_ref, wv_ref,
             cos_ref, sin_ref, rot_ref, xr_ref, gr_ref, cq_ref, ckv_ref, q_ref, k_ref, v_ref):
        u, _ = _rms(h_ref[...], gpre_ref[...])
        u = u.astype(BF16)
        xr_ref[...] = jnp.dot(u, wxr_ref[...], preferred_element_type=F32)
        gr_ref[...] = jnp.dot(u, wgr_ref[...], preferred_element_type=F32)
        cq = jnp.dot(u, wcq_ref[...], preferred_element_type=F32)
        ckv = jnp.dot(u, wckv_ref[...], preferred_element_type=F32)
        kr = jnp.dot(u, wkr_ref[...], preferred_element_type=F32)
        cq_ref[...] = cq
        ckv_ref[...] = ckv
        cos_b, sin_b, rot_b = cos_ref[...], sin_ref[...], rot_ref[...]
        qn = _rms(cq, gq_ref[...])[0].astype(BF16)
        kvn = _rms(ckv, gkv_ref[...])[0].astype(BF16)
        k_rope = _rope(kr, cos_b, sin_b, rot_b)
        for hd in range(nh):
            q_pre = jnp.dot(qn, wq_ref[hd], preferred_element_type=F32)
            q_ref[hd] = (_rope(q_pre, cos_b, sin_b, rot_b) * Q_PRESCALE).astype(BF16)
            k_ref[hd] = (jnp.dot(kvn, wk_ref[hd], preferred_element_type=F32) + k_rope).astype(BF16)
            v_ref[hd] = jnp.dot(kvn, wv_ref[hd], preferred_element_type=F32).astype(BF16)

    def tok(n):
        return pl.BlockSpec((tm, n), lambda i: (i, 0))

    def lay(*shape):
        return pl.BlockSpec((None,) + shape, lambda i: (layer,) + (0,) * len(shape))

    def heads(n):
        return pl.BlockSpec((nh, tm, n), lambda i: (0, i, 0))

    return pl.pallas_call(
        body, name=name, grid=(tp // tm,),
        in_specs=[tok(d), lay(1, d), _whole(d, lw), _whole(d, lw), _whole(d, ql), _whole(d, kvl), _whole(d, QK_DIM),
                  lay(1, ql), lay(1, kvl),
                  _whole(nh, ql, QK_DIM), _whole(nh, kvl, QK_DIM), _whole(nh, kvl, vd), tok(QK_DIM), tok(QK_DIM),
                  pl.BlockSpec((QK_DIM, QK_DIM), lambda i: (0, 0))],
        out_specs=[tok(lw), tok(lw), tok(ql), tok(kvl), heads(QK_DIM), heads(QK_DIM), heads(vd)],
        out_shape=[jax.ShapeDtypeStruct((tp, lw), F32), jax.ShapeDtypeStruct((tp, lw), F32),
                   jax.ShapeDtypeStruct((tp, ql), F32), jax.ShapeDtypeStruct((tp, kvl), F32),
                   jax.ShapeDtypeStruct((nh, tp, QK_DIM), BF16), jax.ShapeDtypeStruct((nh, tp, QK_DIM), BF16),
                   jax.ShapeDtypeStruct((nh, tp, vd), BF16)],
        compiler_params=_cparams(("parallel",)),
    )(h, g_pre, w_xr, w_gr, w_cq, w_ckv, w_kr, g_q, g_kv, w_q, w_k, w_v, cos_t, sin_t, rot)


def _mix_in_bwd(dh_res, h, cq, ckv, d_xr, d_gr, dq, dk, dv, g_pre, w_parts, g_q, g_kv, w_q, w_k, w_v,
                cos_t, sin_t, rot_t, layer, tm, name):
    tp, d = h.shape
    w_xr, w_gr, w_cq, w_ckv, w_kr = w_parts
    lw, ql, kvl = w_xr.shape[-1], w_cq.shape[-1], w_ckv.shape[-1]
    nh, vd = w_v.shape[0], w_v.shape[-1]

    def body(dhr_ref, h_ref, cq_ref, ckv_ref, dxr_ref, dgr_ref, dq_ref, dk_ref, dv_ref, gpre_ref,
             wxr_ref, wgr_ref, wcq_ref, wckv_ref, wkr_ref, gq_ref, gkv_ref, wq_ref, wk_ref, wv_ref,
             cos_ref, sin_ref, rott_ref,
             dh_ref, u_ref, qn_ref, kvn_ref, dqpre_ref, dcq_ref, dckv_ref, dkr_ref, dgpre_ref, dgq_ref, dgkv_ref):
        i = pl.program_id(0)

        @pl.when(i == 0)
        def _():
            dgpre_ref[...] = jnp.zeros_like(dgpre_ref)
            dgq_ref[...] = jnp.zeros_like(dgq_ref)
            dgkv_ref[...] = jnp.zeros_like(dgkv_ref)

        cos_b, sin_b, rott_b = cos_ref[...], sin_ref[...], rott_ref[...]
        hh = h_ref[...]
        u, _ = _rms(hh, gpre_ref[...])
        u_ref[...] = u.astype(BF16)
        cq, ckv = cq_ref[...], ckv_ref[...]
        qn = _rms(cq, gq_ref[...])[0]
        kvn = _rms(ckv, gkv_ref[...])[0]
        qn_ref[...] = qn.astype(BF16)
        kvn_ref[...] = kvn.astype(BF16)
        d_qn = jnp.zeros((tm, ql), F32)
        d_kvn = jnp.zeros((tm, kvl), F32)
        d_krope = jnp.zeros((tm, QK_DIM), F32)
        for hd in range(nh):
            dq_h = dq_ref[hd]
            dq_pre = dq_h * cos_b + _dot_split(dq_h * sin_b, rott_b)
            dq_pre_b = dq_pre.astype(BF16)
            dqpre_ref[hd] = dq_pre_b
            d_qn += _dot_nt(dq_pre_b, wq_ref[hd])
            dk_h = dk_ref[hd]
            d_krope += dk_h
            d_kvn += _dot_nt(dk_h, wk_ref[hd]) + _dot_nt(dv_ref[hd], wv_ref[hd])
        d_kr = d_krope * cos_b + _dot_split(d_krope * sin_b, rott_b)
        d_cq, dgq = _rms_bwd(cq, gq_ref[...], d_qn)
        d_ckv, dgkv = _rms_bwd(ckv, gkv_ref[...], d_kvn)
        dgq_ref[...] += dgq
        dgkv_ref[...] += dgkv
        d_cq_b, d_ckv_b, d_kr_b = d_cq.astype(BF16), d_ckv.astype(BF16), d_kr.astype(BF16)
        dcq_ref[...] = d_cq_b
        dckv_ref[...] = d_ckv_b
        dkr_ref[...] = d_kr_b
        du = (_dot_nt(dxr_ref[...], wxr_ref[...]) + _dot_nt(dgr_ref[...], wgr_ref[...]) + _dot_nt(d_cq_b, wcq_ref[...])
              + _dot_nt(d_ckv_b, wckv_ref[...]) + _dot_nt(d_kr_b, wkr_ref[...]))
        dx, dg = _rms_bwd(hh, gpre_ref[...], du)
        dh_ref[...] = dhr_ref[...] + dx
        dgpre_ref[...] += dg

    def tok(n):
        return pl.BlockSpec((tm, n), lambda i: (i, 0))

    def lay(*shape):
        return pl.BlockSpec((None,) + shape, lambda i: (layer,) + (0,) * len(shape))

    def heads(n):
        return pl.BlockSpec((nh, tm, n), lambda i: (0, i, 0))

    def acc(n):
        return pl.BlockSpec((1, n), lambda i: (0, 0))

    return pl.pallas_call(
        body, name=name, grid=(tp // tm,),
        in_specs=[tok(d), tok(d), tok(ql), tok(kvl), tok(lw), tok(lw), heads(QK_DIM), heads(QK_DIM), heads(vd), lay(1, d),
                  _whole(d, lw), _whole(d, lw), _whole(d, ql), _whole(d, kvl), _whole(d, QK_DIM), lay(1, ql), lay(1, kvl),
                  _whole(nh, ql, QK_DIM), _whole(nh, kvl, QK_DIM), _whole(nh, kvl, vd), tok(QK_DIM), tok(QK_DIM),
                  pl.BlockSpec((QK_DIM, QK_DIM), lambda i: (0, 0))],
        out_specs=[tok(d), tok(d), tok(ql), tok(kvl), heads(QK_DIM), tok(ql), tok(kvl), tok(QK_DIM), acc(d), acc(ql), acc(kvl)],
        out_shape=[jax.ShapeDtypeStruct((tp, d), F32), jax.ShapeDtypeStruct((tp, d), BF16),
                   jax.ShapeDtypeStruct((tp, ql), BF16), jax.ShapeDtypeStruct((tp, kvl), BF16),
                   jax.ShapeDtypeStruct((nh, tp, QK_DIM), BF16), jax.ShapeDtypeStruct((tp, ql), BF16),
                   jax.ShapeDtypeStruct((tp, kvl), BF16), jax.ShapeDtypeStruct((tp, QK_DIM), BF16),
                   jax.ShapeDtypeStruct((1, d), F32), jax.ShapeDtypeStruct((1, ql), F32), jax.ShapeDtypeStruct((1, kvl), F32)],
        compiler_params=_cparams(("arbitrary",)),
    )(dh_res, h, cq, ckv, d_xr, d_gr, dq, dk, dv, g_pre, w_xr, w_gr, w_cq, w_ckv, w_kr, g_q, g_kv, w_q, w_k, w_v,
      cos_t, sin_t, rot_t)


def _lru_gates(xc, wa_ref, wx_ref, ba, bx, sp):
    xcb = xc.astype(BF16)
    r = _sigmoid(jnp.dot(xcb, wa_ref[...], preferred_element_type=F32) + ba)
    ig = _sigmoid(jnp.dot(xcb, wx_ref[...], preferred_element_type=F32) + bx)
    log_a = -LRU_C * r * sp
    a = jnp.exp(log_a)
    sq = jnp.sqrt(_neg_expm1(2.0 * log_a))
    return r, ig, a, sq


def _softplus(x):
    return jnp.maximum(x, 0.0) + jnp.log(1.0 + jnp.exp(-jnp.abs(x)))


def _lru_fwd(xr, gr, conv_w, conv_b, wa_bd, wx_bd, b_a, b_x, lam, layer, tc, name):
    tp, w = xr.shape
    pad = 8

    def body(xr_ref, gr_ref, cw_ref, cb_ref, wa_ref, wx_ref, ba_ref, bx_ref, lam_ref, y_ref, xc_ref, hs_ref,
             xe_sc, a_sc, b_sc, st_sc):
        c = pl.program_id(0)

        @pl.when(c == 0)
        def _():
            xe_sc[pl.ds(0, pad), :] = jnp.zeros((pad, w), F32)
            st_sc[...] = jnp.zeros_like(st_sc)

        xe_sc[pl.ds(pad, tc), :] = xr_ref[...]
        xc = cb_ref[...] + xe_sc[pl.ds(pad, tc), :] * cw_ref[pl.ds(CONV_WIDTH - 1, 1), :]
        for k in range(CONV_WIDTH - 1):
            xc = xc + xe_sc[pl.ds(pad - (CONV_WIDTH - 1) + k, tc), :] * cw_ref[pl.ds(k, 1), :]
        xe_sc[pl.ds(0, pad), :] = xe_sc[pl.ds(tc, pad), :]
        xc_ref[...] = xc
        sp = _softplus(-lam_ref[...])
        _, ig, a, sq = _lru_gates(xc, wa_ref, wx_ref, ba_ref[...], bx_ref[...], sp)
        a_sc[...] = a
        b_sc[...] = sq * (ig * xc)

        def step(t, hcur):
            hnew = a_sc[pl.ds(t, 1), :] * hcur + b_sc[pl.ds(t, 1), :]
            hs_ref[pl.ds(t, 1), :] = hnew
            return hnew

        st_sc[...] = lax.fori_loop(0, tc, step, st_sc[...], unroll=8)
        y_ref[...] = hs_ref[...] * _gelu(gr_ref[...])[0]

    tok = pl.BlockSpec((tc, w), lambda c: (c, 0))

    def lay(*shape):
        return pl.BlockSpec((None,) + shape, lambda c: (layer,) + (0,) * len(shape))

    out = jax.ShapeDtypeStruct((tp, w), F32)
    return pl.pallas_call(
        body, name=name, grid=(tp // tc,),
        in_specs=[tok, tok, lay(CONV_WIDTH, w), lay(1, w), lay(w, w), lay(w, w), lay(1, w), lay(1, w), lay(1, w)],
        out_specs=[tok, tok, tok], out_shape=[out, out, out],
        scratch_shapes=[pltpu.VMEM((tc + pad, w), F32), pltpu.VMEM((tc, w), F32), pltpu.VMEM((tc, w), F32),
                        pltpu.VMEM((1, w), F32)],
        compiler_params=_cparams(("arbitrary",)),
    )(xr, gr, conv_w, conv_b, wa_bd, wx_bd, b_a, b_x, lam)


def _lru_bwd(dy, xr, gr, xc, hs, conv_w, wa_bd, wx_bd, b_a, b_x, lam, layer, tc, name):
    tp, w = xr.shape
    pad = 8
    nc = tp // tc
    per = tc // pad

    def body(dy_ref, xr_ref, gr_ref, xc_ref, hs_ref, xrp_ref, hsp_ref, cw_ref, wa_ref, wx_ref, ba_ref, bx_ref, lam_ref,
             dxr_ref, dgr_ref, dcw_ref, dcb_ref, dwa_ref, dwx_ref, dba_ref, dbx_ref, dlam_ref,
             a_sc, dh_sc, dxc_sc, he_sc, xe_sc, carry_sc):
        s = pl.program_id(0)
        first_chunk = s == nc - 1

        @pl.when(s == 0)
        def _():
            for ref in (dcw_ref, dcb_ref, dwa_ref, dwx_ref, dba_ref, dbx_ref, dlam_ref):
                ref[...] = jnp.zeros_like(ref)
            carry_sc[...] = jnp.zeros_like(carry_sc)
            dxc_sc[pl.ds(tc, pad), :] = jnp.zeros((pad, w), F32)

        keep = jnp.where(first_chunk, 0.0, 1.0)
        he_sc[pl.ds(0, pad), :] = hsp_ref[...] * keep
        he_sc[pl.ds(pad, tc), :] = hs_ref[...]
        xe_sc[pl.ds(0, pad), :] = xrp_ref[...] * keep
        xe_sc[pl.ds(pad, tc), :] = xr_ref[...]

        lam_v = lam_ref[...]
        sp = _softplus(-lam_v)
        xc = xc_ref[...]
        r, ig, a, sq = _lru_gates(xc, wa_ref, wx_ref, ba_ref[...], bx_ref[...], sp)
        a_sc[...] = a
        grv = gr_ref[...]
        gl, th = _gelu(grv)
        dyv = dy_ref[...]
        dgr_ref[...] = dyv * hs_ref[...] * _gelu_grad(grv, th)
        dh_sc[...] = dyv * gl

        def step(n, g):
            t = tc - 1 - n
            dh = dh_sc[pl.ds(t, 1), :] + g
            dh_sc[pl.ds(t, 1), :] = dh
            return a_sc[pl.ds(t, 1), :] * dh

        carry_sc[...] = lax.fori_loop(0, tc, step, carry_sc[...], unroll=8)

        dh = dh_sc[...]
        d_a = dh * he_sc[pl.ds(pad - 1, tc), :]
        d_ixc = dh * sq
        d_sq = dh * (ig * xc)
        a2 = a * a
        d_la = d_a * a - d_sq * a2 / sq
        d_r = d_la * (-LRU_C * sp)
        d_sp = jnp.sum(d_la * (-LRU_C * r), axis=0, keepdims=True)
        dlam_ref[...] += d_sp * (-_sigmoid(-lam_v))
        d_pa = d_r * r * (1.0 - r)
        d_px = d_ixc * xc * ig * (1.0 - ig)
        dba_ref[...] += jnp.sum(d_pa, axis=0, keepdims=True)
        dbx_ref[...] += jnp.sum(d_px, axis=0, keepdims=True)
        d_pa_b, d_px_b = d_pa.astype(BF16), d_px.astype(BF16)
        xcb = xc.astype(BF16)
        dwa_ref[...] += _dot_tn(xcb, d_pa_b)
        dwx_ref[...] += _dot_tn(xcb, d_px_b)
        d_xc = d_ixc * ig + _dot_nt(d_pa_b, wa_ref[...]) + _dot_nt(d_px_b, wx_ref[...])
        dcb_ref[...] += jnp.sum(d_xc, axis=0, keepdims=True)
        dxc_sc[pl.ds(0, tc), :] = d_xc
        d_xr = jnp.zeros((tc, w), F32)
        for k in range(CONV_WIDTH):
            off = CONV_WIDTH - 1 - k
            d_xr = d_xr + dxc_sc[pl.ds(off, tc), :] * cw_ref[pl.ds(k, 1), :]
            dcw_ref[pl.ds(k, 1), :] += jnp.sum(d_xc * xe_sc[pl.ds(pad - off, tc), :], axis=0, keepdims=True)
        dxr_ref[...] = d_xr
        dxc_sc[pl.ds(tc, pad), :] = dxc_sc[pl.ds(0, pad), :]

    def rev(cidx):
        return nc - 1 - cidx

    tok = pl.BlockSpec((tc, w), lambda c: (rev(c), 0))
    prev = pl.BlockSpec((pad, w), lambda c: (jnp.maximum(rev(c) * per - 1, 0), 0))

    def lay(*shape):
        return pl.BlockSpec((None,) + shape, lambda c: (layer,) + (0,) * len(shape))

    def acc(*shape):
        return pl.BlockSpec(shape, lambda c: (0,) * len(shape))

    big = jax.ShapeDtypeStruct((tp, w), F32)
    vec = jax.ShapeDtypeStruct((1, w), F32)
    return pl.pallas_call(
        body, name=name, grid=(nc,),
        in_specs=[tok, tok, tok, tok, tok, prev, prev, lay(CONV_WIDTH, w), lay(w, w), lay(w, w), lay(1, w), lay(1, w), lay(1, w)],
        out_specs=[tok, tok, acc(CONV_WIDTH, w), acc(1, w), acc(w, w), acc(w, w), acc(1, w), acc(1, w), acc(1, w)],
        out_shape=[big, big, jax.ShapeDtypeStruct((CONV_WIDTH, w), F32), vec, jax.ShapeDtypeStruct((w, w), F32),
                   jax.ShapeDtypeStruct((w, w), F32), vec, vec, vec],
        scratch_shapes=[pltpu.VMEM((tc, w), F32), pltpu.VMEM((tc, w), F32), pltpu.VMEM((tc + pad, w), F32),
                        pltpu.VMEM((tc + pad, w), F32), pltpu.VMEM((tc + pad, w), F32), pltpu.VMEM((1, w), F32)],
        compiler_params=_cparams(("arbitrary",)),
    )(dy, xr, gr, xc, hs, xr, hs, conv_w, wa_bd, wx_bd, b_a, b_x, lam)


def _attn_fwd(q, k, v, tq, name, job=None):
    nh, tp, dqk = q.shape
    vd = v.shape[-1]
    nq = tp // tq
    ts = tq
    n_sub = tq // ts
    n_ji = len(job.inputs) if job else 0
    n_jo = len(job.out_shape) if job else 0

    def body(q_ref, k_ref, v_ref, *rest):
        job_in, (o_ref, lse_ref), rest = rest[:n_ji], rest[n_ji:n_ji + 2], rest[n_ji + 2:]
        job_out, sems = rest[:n_jo], rest[n_jo:]
        i = pl.program_id(1)
        if job:
            first = (pl.program_id(0) == 0) & (i == 0)
            pl.when(first)(lambda: job.start(job_in, job_out, sems))
        q_rows = [q_ref[pl.ds(g * ts, ts), :] for g in range(n_sub)]

        def step(j, carry, diagonal):
            off = pl.multiple_of(j * tq, tq)
            kb = k_ref[pl.ds(off, tq), :]
            vb = v_ref[pl.ds(off, tq), :]
            out = []
            for g in range(n_sub):
                m, l, acc = carry[g]
                s = _dot_nt(q_rows[g], kb)
                if diagonal:
                    keep = (lax.broadcasted_iota(jnp.int32, (ts, tq), 1)
                            <= lax.broadcasted_iota(jnp.int32, (ts, tq), 0) + g * ts)
                    s = jnp.where(keep, s, NEG_BIG)
                m_new = jnp.maximum(m, jnp.max(s, axis=-1, keepdims=True))
                p = jnp.exp2(s - m_new)
                alpha = jnp.exp2(m - m_new)
                l = alpha * l + jnp.sum(p, axis=-1, keepdims=True)
                acc = alpha * acc + jnp.dot(p.astype(BF16), vb, preferred_element_type=F32)
                out.append((m_new, l, acc))
            return tuple(out)

        init = tuple((jnp.full((ts, 1), NEG_BIG, F32), jnp.zeros((ts, 1), F32), jnp.zeros((ts, vd), F32))
                     for _ in range(n_sub))
        below = lax.fori_loop(0, i, lambda j, c: step(j, c, False), init)
        for g, (m, l, acc) in enumerate(step(i, below, True)):
            o_ref[pl.ds(g * ts, ts), :] = acc / l
            lse_ref[pl.ds(g * ts, ts), :] = m + jnp.log2(l)
        if job:
            last = (pl.program_id(0) == nh - 1) & (i == nq - 1)
            pl.when(last)(lambda: job.finish(job_in, job_out, sems))

    outs = pl.pallas_call(
        body, name=name, grid=(nh, nq),
        in_specs=[pl.BlockSpec((None, tq, dqk), lambda h, i: (h, i, 0)),
                  pl.BlockSpec((None, tp, dqk), lambda h, i: (h, 0, 0)),
                  pl.BlockSpec((None, tp, vd), lambda h, i: (h, 0, 0))] + [_ANY] * n_ji,
        out_specs=[pl.BlockSpec((None, tq, vd), lambda h, i: (h, i, 0)),
                   pl.BlockSpec((None, tq, 1), lambda h, i: (h, i, 0))] + [_ANY] * n_jo,
        out_shape=[jax.ShapeDtypeStruct((nh, tp, vd), F32), jax.ShapeDtypeStruct((nh, tp, 1), F32)]
        + (list(job.out_shape) if job else []),
        scratch_shapes=list(job.scratch) if job else [],
        compiler_params=_cparams(("arbitrary", "arbitrary") if job else ("parallel", "parallel")),
    )(q, k, v, *(job.inputs if job else ()))
    return outs[0], outs[1], list(outs[2:])


def _attn_bwd(q, k, v, o, do, lse, tq, name, job=None):
    nh, tp, dqk = q.shape
    vd = v.shape[-1]
    scale = dqk ** -0.5
    nq = tp // tq
    ts = tq
    n_sub = tq // ts
    n_ji = len(job.inputs) if job else 0
    n_jo = len(job.out_shape) if job else 0

    def body(q_ref, k_ref, v_ref, o_ref, do_ref, lse_ref, *rest):
        job_in, (dq_ref, dk_ref, dv_ref), rest = rest[:n_ji], rest[n_ji:n_ji + 3], rest[n_ji + 3:]
        job_out, (lse_rows, delta_rows, do_bf, dqt_sc), sems = rest[:n_jo], rest[n_jo:n_jo + 4], rest[n_jo + 4:]
        kb_i = pl.program_id(1)
        if job:
            first = (pl.program_id(0) == 0) & (kb_i == 0)
            pl.when(first)(lambda: job.start(job_in, job_out, sems))

        def as_rows(col):
            return jnp.transpose(jnp.broadcast_to(col, (tq, _LANES)))[0:8, :]

        @pl.when(kb_i == 0)
        def _():
            dqt_sc[...] = jnp.zeros_like(dqt_sc)

            def prep(qi, _):
                off = pl.multiple_of(qi * tq, tq)
                dob = do_ref[pl.ds(off, tq), :]
                do_bf[pl.ds(off, tq), :] = dob.astype(BF16)
                delta_rows[qi] = as_rows(jnp.sum(dob * o_ref[pl.ds(off, tq), :], axis=-1, keepdims=True))
                lse_rows[qi] = as_rows(lse_ref[pl.ds(off, tq), :])
                return 0

            lax.fori_loop(0, nq, prep, 0)

        k_rows = [k_ref[pl.ds(g * ts, ts), :] for g in range(n_sub)]
        v_rows = [v_ref[pl.ds(g * ts, ts), :] for g in range(n_sub)]

        def pair(qi, carry, diagonal):
            off = pl.multiple_of(qi * tq, tq)
            qb = q_ref[pl.ds(off, tq), :]
            dob = do_bf[pl.ds(off, tq), :]
            lse_row = lse_rows[qi, 0:1, :]
            delta_row = delta_rows[qi, 0:1, :]
            out, dqt = [], None
            for g in range(n_sub):
                dk, dv = carry[g]
                st = _dot_nt(k_rows[g], qb)
                if diagonal:
                    keep = (lax.broadcasted_iota(jnp.int32, (ts, tq), 0) + g * ts
                            <= lax.broadcasted_iota(jnp.int32, (ts, tq), 1))
                    st = jnp.where(keep, st, NEG_BIG)
                pt = jnp.exp2(st - lse_row)
                dpt = _dot_nt(v_rows[g], dob)
                dst = (pt * (dpt - delta_row)).astype(BF16)
                dv = dv + jnp.dot(pt.astype(BF16), dob, preferred_element_type=F32)
                dk = dk + jnp.dot(dst, qb, preferred_element_type=F32)
                part = _dot_tn(k_rows[g], dst)
                dqt = part if dqt is None else dqt + part
                out.append((dk, dv))
            dqt_sc[qi] += dqt
            return tuple(out)

        zero = tuple((jnp.zeros((ts, dqk), F32), jnp.zeros((ts, vd), F32)) for _ in range(n_sub))
        done = lax.fori_loop(kb_i + 1, nq, lambda qi, c: pair(qi, c, False), pair(kb_i, zero, True))
        for g, (dk, dv) in enumerate(done):
            dk_ref[pl.ds(g * ts, ts), :] = dk * math.log(2.0)
            dv_ref[pl.ds(g * ts, ts), :] = dv

        @pl.when(kb_i == nq - 1)
        def _():
            eye = (lax.broadcasted_iota(jnp.int32, (dqk, dqk), 0) == lax.broadcasted_iota(jnp.int32, (dqk, dqk), 1))
            eye = jnp.where(eye, 1.0, 0.0).astype(BF16)
            for qi in range(nq):
                t = dqt_sc[qi]
                hi = t.astype(BF16)
                lo = (t - hi.astype(F32)).astype(BF16)
                dq_ref[pl.ds(qi * tq, tq), :] = scale * (_dot_tn(hi, eye) + _dot_tn(lo, eye))

        if job:
            last = (pl.program_id(0) == nh - 1) & (kb_i == nq - 1)
            pl.when(last)(lambda: job.finish(job_in, job_out, sems))

    def full(n):
        return pl.BlockSpec((None, tp, n), lambda h, j: (h, 0, 0))

    def blk(n):
        return pl.BlockSpec((None, tq, n), lambda h, j: (h, j, 0))

    outs = pl.pallas_call(
        body, name=name, grid=(nh, nq),
        in_specs=[full(dqk), blk(dqk), blk(vd), full(vd), full(vd), full(1)] + [_ANY] * n_ji,
        out_specs=[full(dqk), blk(dqk), blk(vd)] + [_ANY] * n_jo,
        out_shape=[jax.ShapeDtypeStruct((nh, tp, dqk), F32), jax.ShapeDtypeStruct((nh, tp, dqk), F32),
                   jax.ShapeDtypeStruct((nh, tp, vd), F32)] + (list(job.out_shape) if job else []),
        scratch_shapes=[pltpu.VMEM((nq, 8, tq), F32), pltpu.VMEM((nq, 8, tq), F32), pltpu.VMEM((tp, vd), BF16),
                        pltpu.VMEM((nq, dqk, tq), F32)] + (list(job.scratch) if job else []),
        compiler_params=_cparams(("arbitrary", "arbitrary") if job else ("parallel", "arbitrary")),
    )(q, k, v, o, do, lse, *(job.inputs if job else ()))
    return outs[0], outs[1], outs[2], list(outs[3:])


def _mix_out_fwd(h, y_lru, o, g_lo, g_mo, w_lo, w_mo, g_post, layer, tm, name):
    tp, d = h.shape
    lw = y_lru.shape[-1]
    nh, vd = o.shape[0], o.shape[-1]

    def body(h_ref, yl_ref, o_ref, glo_ref, gmo_ref, wlo_ref, wmo_ref, gpost_ref, hout_ref, y_ref):
        nl, _ = _rms(yl_ref[...], glo_ref[...])
        y = _dot(nl, wlo_ref[...])
        ssq = jnp.zeros((tm, 1), F32)
        for hd in range(nh):
            oh = o_ref[hd]
            ssq += jnp.sum(oh * oh, axis=-1, keepdims=True)
        r = lax.rsqrt(ssq / (nh * vd) + EPS)
        for hd in range(nh):
            y += _dot(o_ref[hd] * r * gmo_ref[hd], wmo_ref[hd])
        y_ref[...] = y
        hout_ref[...] = h_ref[...] + _rms(y, gpost_ref[...])[0]

    tok = pl.BlockSpec((tm, d), lambda i: (i, 0))

    def lay(*shape):
        return pl.BlockSpec((None,) + shape, lambda i: (layer,) + (0,) * len(shape))

    return pl.pallas_call(
        body, name=name, grid=(tp // tm,),
        in_specs=[tok, pl.BlockSpec((tm, lw), lambda i: (i, 0)), pl.BlockSpec((nh, tm, vd), lambda i: (0, i, 0)),
                  lay(1, lw), lay(nh, 1, vd), _whole(lw, d), _whole(nh, vd, d), lay(1, d)],
        out_specs=[tok, tok],
        out_shape=[jax.ShapeDtypeStruct((tp, d), F32), jax.ShapeDtypeStruct((tp, d), F32)],
        compiler_params=_cparams(("parallel",)),
    )(h, y_lru, o, g_lo, g_mo, w_lo, w_mo, g_post)


def _mix_out_bwd(dh_out, y, y_lru, o, g_lo, g_mo, w_lo, w_mo, g_post, layer, tm, name):
    tp, d = y.shape
    lw = y_lru.shape[-1]
    nh, vd = o.shape[0], o.shape[-1]

    def body(dho_ref, y_ref, yl_ref, o_ref, glo_ref, gmo_ref, wlo_ref, wmo_ref, gpost_ref,
             dyl_ref, do_ref, dy_ref, nl_ref, nm_ref, dgpost_ref, dglo_ref, dgmo_ref):
        i = pl.program_id(0)

        @pl.when(i == 0)
        def _():
            dgpost_ref[...] = jnp.zeros_like(dgpost_ref)
            dglo_ref[...] = jnp.zeros_like(dglo_ref)
            dgmo_ref[...] = jnp.zeros_like(dgmo_ref)

        dy, dgp = _rms_bwd(y_ref[...], gpost_ref[...], dho_ref[...])
        dgpost_ref[...] += dgp
        dy_b = dy.astype(BF16)
        dy_ref[...] = dy_b
        yl = yl_ref[...]
        nl_ref[...] = _rms(yl, glo_ref[...])[0].astype(BF16)
        dyl, dgl = _rms_bwd(yl, glo_ref[...], _dot_nt(dy_b, wlo_ref[...]))
        dyl_ref[...] = dyl
        dglo_ref[...] += dgl
        ssq = jnp.zeros((tm, 1), F32)
        for hd in range(nh):
            oh = o_ref[hd]
            ssq += jnp.sum(oh * oh, axis=-1, keepdims=True)
        r = lax.rsqrt(ssq / (nh * vd) + EPS)
        dn = []
        dot_sum = jnp.zeros((tm, 1), F32)
        for hd in range(nh):
            oh = o_ref[hd] * r
            nm_ref[hd] = (oh * gmo_ref[hd]).astype(BF16)
            dn_h = _dot_nt(dy_b, wmo_ref[hd])
            dgmo_ref[hd] += jnp.sum(dn_h * oh, axis=0, keepdims=True)
            dn_h = dn_h * gmo_ref[hd]
            dot_sum += jnp.sum(dn_h * oh, axis=-1, keepdims=True)
            dn.append(dn_h)
        dot_mean = dot_sum / (nh * vd)
        for hd in range(nh):
            do_ref[hd] = r * (dn[hd] - o_ref[hd] * r * dot_mean)

    tok = pl.BlockSpec((tm, d), lambda i: (i, 0))
    tokl = pl.BlockSpec((tm, lw), lambda i: (i, 0))
    heads = pl.BlockSpec((nh, tm, vd), lambda i: (0, i, 0))

    def lay(*shape):
        return pl.BlockSpec((None,) + shape, lambda i: (layer,) + (0,) * len(shape))

    def acc(*shape):
        return pl.BlockSpec(shape, lambda i: (0,) * len(shape))

    return pl.pallas_call(
        body, name=name, grid=(tp // tm,),
        in_specs=[tok, tok, tokl, heads, lay(1, lw), lay(nh, 1, vd), _whole(lw, d), _whole(nh, vd, d), lay(1, d)],
        out_specs=[tokl, heads, tok, tokl, heads, acc(1, d), acc(1, lw), acc(nh, 1, vd)],
        out_shape=[jax.ShapeDtypeStruct((tp, lw), F32), jax.ShapeDtypeStruct((nh, tp, vd), F32),
                   jax.ShapeDtypeStruct((tp, d), BF16), jax.ShapeDtypeStruct((tp, lw), BF16),
                   jax.ShapeDtypeStruct((nh, tp, vd), BF16), jax.ShapeDtypeStruct((1, d), F32),
                   jax.ShapeDtypeStruct((1, lw), F32), jax.ShapeDtypeStruct((nh, 1, vd), F32)],
        compiler_params=_cparams(("arbitrary",)),
    )(dh_out, y, y_lru, o, g_lo, g_mo, w_lo, w_mo, g_post)


def _loss_head(h, target, n_real, tm, name):
    tp, d = h.shape
    nt = tp // tm

    def body(h_ref, t_ref, dh_ref, part_ref):
        i = pl.program_id(0)
        row = i * tm + lax.broadcasted_iota(jnp.int32, (tm, 1), 0)
        real = (row >= N_META) & (row < N_META + n_real)
        err = jnp.where(real, h_ref[...] - t_ref[...], 0.0)
        dh_ref[...] = err * (1.0 / d)
        part_ref[...] = jnp.full((1, 1, 128), 0.5 / d * jnp.sum(err * err), F32)

    tok = pl.BlockSpec((tm, d), lambda i: (i, 0))
    return pl.pallas_call(
        body, name=name, grid=(nt,), in_specs=[tok, tok],
        out_specs=[tok, pl.BlockSpec((1, 1, 128), lambda i: (i, 0, 0))],
        out_shape=[jax.ShapeDtypeStruct((tp, d), F32), jax.ShapeDtypeStruct((nt, 1, 128), F32)],
        compiler_params=_cparams(("parallel",)),
    )(h, target)


def _mesh_pos():
    return lax.axis_index("x"), lax.axis_index("y"), lax.axis_index("c")


def _other_chips(x, y):
    return [(1 - x, y), (x, 1 - y), (1 - x, 1 - y)]


class _GatherJob:
    def __init__(self, inputs, picks, shard_shapes, dtypes):
        self.inputs = list(inputs)
        self.picks = list(picks)
        n = len(self.inputs)
        self.out_shape = [jax.ShapeDtypeStruct((N_CHIPS,) + tuple(s), dt) for s, dt in zip(shard_shapes, dtypes)]
        self.halves = [s[0] // 2 for s in shard_shapes]
        self.scratch = [pltpu.SemaphoreType.DMA((n, 3)), pltpu.SemaphoreType.DMA((n, 3)), pltpu.SemaphoreType.DMA((n, 3)),
                        pltpu.SemaphoreType.DMA((n, 3)), pltpu.SemaphoreType.DMA((n,))]

    def _copies(self, ins, outs, sems):
        ici_s, ici_r, core_s, core_r, loc = sems
        x, y, c = _mesh_pos()
        me = 2 * x + y
        chips = _other_chips(x, y)
        local, ici, fwd = [], [], []
        for a in range(len(ins)):
            src = self.picks[a](ins[a])
            hv = self.halves[a]
            mine = pl.ds(c * hv, hv)
            local.append(pltpu.make_async_copy(src, outs[a].at[me], loc.at[a]))
            for p, (px, py) in enumerate(chips):
                ici.append((pltpu.make_async_remote_copy(src_ref=src.at[mine], dst_ref=outs[a].at[me, mine],
                                                         send_sem=ici_s.at[a, p], recv_sem=ici_r.at[a, p],
                                                         device_id=(px, py, c), device_id_type=MESH),
                            pltpu.make_async_remote_copy(src_ref=src.at[mine], dst_ref=outs[a].at[2 * px + py, mine],
                                                         send_sem=ici_s.at[a, p], recv_sem=ici_r.at[a, p],
                                                         device_id=(px, py, c), device_id_type=MESH)))
                landed = outs[a].at[2 * px + py, mine]
                theirs = outs[a].at[2 * px + py, pl.ds((1 - c) * hv, hv)]
                fwd.append((pltpu.make_async_remote_copy(src_ref=landed, dst_ref=landed, send_sem=core_s.at[a, p],
                                                         recv_sem=core_r.at[a, p], device_id=(x, y, 1 - c), device_id_type=MESH),
                            pltpu.make_async_remote_copy(src_ref=theirs, dst_ref=theirs, send_sem=core_s.at[a, p],
                                                         recv_sem=core_r.at[a, p], device_id=(x, y, 1 - c), device_id_type=MESH)))
        return local, ici, fwd

    def start(self, ins, outs, sems):
        local, ici, _ = self._copies(ins, outs, sems)
        for cp in local:
            cp.start()
        for send, _ in ici:
            send.start()

    def finish(self, ins, outs, sems):
        local, ici, fwd = self._copies(ins, outs, sems)
        for (_, arrive), (hand_on, _) in zip(ici, fwd):
            arrive.wait_recv()
            hand_on.start()
        for _, arrive in fwd:
            arrive.wait_recv()
        for send, _ in ici:
            send.wait_send()
        for hand_on, _ in fwd:
            hand_on.wait_send()
        for cp in local:
            cp.wait()


class _ReduceJob:
    def __init__(self, inputs):
        self.inputs = list(inputs)
        n = len(self.inputs)
        self.out_shape = [jax.ShapeDtypeStruct(a.shape, a.dtype) for a in self.inputs] * 2
        self.scratch = [pltpu.SemaphoreType.DMA((n, 3)), pltpu.SemaphoreType.DMA((n, 3)), pltpu.SemaphoreType.DMA((n, 4)),
                        pltpu.SemaphoreType.DMA((n, 4)), pltpu.SemaphoreType.DMA((n,))]

    def _copies(self, ins, outs, sems):
        ici_s, ici_r, core_s, core_r, loc = sems
        n = len(ins)
        same, other = outs[:n], outs[n:]
        x, y, c = _mesh_pos()
        me = 2 * x + y
        chips = _other_chips(x, y)
        sib = (x, y, 1 - c)
        local, ici, fwd = [], [], []
        for a in range(n):
            local.append(pltpu.make_async_copy(ins[a].at[me], same[a].at[me], loc.at[a]))
            fwd.append((pltpu.make_async_remote_copy(src_ref=ins[a].at[me], dst_ref=other[a].at[me], send_sem=core_s.at[a, 3],
                                                     recv_sem=core_r.at[a, 3], device_id=sib, device_id_type=MESH), None))
            for p, (px, py) in enumerate(chips):
                cp = 2 * px + py
                ici.append((pltpu.make_async_remote_copy(src_ref=ins[a].at[cp], dst_ref=same[a].at[me], send_sem=ici_s.at[a, p],
                                                         recv_sem=ici_r.at[a, p], device_id=(px, py, c), device_id_type=MESH),
                            pltpu.make_async_remote_copy(src_ref=ins[a].at[cp], dst_ref=same[a].at[cp], send_sem=ici_s.at[a, p],
                                                         recv_sem=ici_r.at[a, p], device_id=(px, py, c), device_id_type=MESH)))
                fwd.append((pltpu.make_async_remote_copy(src_ref=same[a].at[cp], dst_ref=other[a].at[cp], send_sem=core_s.at[a, p],
                                                         recv_sem=core_r.at[a, p], device_id=sib, device_id_type=MESH), p))
        return local, ici, fwd

    def start(self, ins, outs, sems):
        local, ici, fwd = self._copies(ins, outs, sems)
        for cp in local:
            cp.start()
        for send, _ in ici:
            send.start()
        for hand_on, p in fwd:
            if p is None:
                hand_on.start()

    def finish(self, ins, outs, sems):
        local, ici, fwd = self._copies(ins, outs, sems)
        chip_fwd = [f for f in fwd if f[1] is not None]
        for (_, arrive), (hand_on, _) in zip(ici, chip_fwd):
            arrive.wait_recv()
            hand_on.start()
        for hand_on, _ in fwd:
            hand_on.wait()
        for send, _ in ici:
            send.wait_send()
        for cp in local:
            cp.wait()


def _run_job(job, name):
    n_in, n_out = len(job.inputs), len(job.out_shape)

    def body(*refs):
        ins, outs, sems = refs[:n_in], refs[n_in:n_in + n_out], refs[n_in + n_out:]
        job.start(ins, outs, sems)
        job.finish(ins, outs, sems)

    return list(pl.pallas_call(body, name=name, in_specs=[_ANY] * n_in, out_specs=[_ANY] * n_out, out_shape=list(job.out_shape),
                               scratch_shapes=list(job.scratch))(*job.inputs))


def _allreduce_small(buf, name):
    rows, lanes = buf.shape
    n_dev = 8

    def body(in_ref, out_ref, gather, send_sems, recv_sems):
        x, y, c = _mesh_pos()
        me = 4 * x + 2 * y + c
        gather[me] = in_ref[...]
        sends = []
        for rel in range(1, n_dev):
            fx, fy, fc = (rel >> 2) & 1, (rel >> 1) & 1, rel & 1
            peer = (1 - x if fx else x, 1 - y if fy else y, 1 - c if fc else c)
            rc = pltpu.make_async_remote_copy(src_ref=in_ref, dst_ref=gather.at[me], send_sem=send_sems.at[rel - 1],
                                              recv_sem=recv_sems.at[rel - 1], device_id=peer, device_id_type=MESH)
            rc.start()
            sends.append(rc)
        for rc in sends:
            rc.wait()
        total = gather[0]
        for k in range(1, n_dev):
            total = total + gather[k]
        out_ref[...] = total

    vm = pl.BlockSpec(memory_space=pltpu.VMEM)
    return pl.pallas_call(
        body, name=name, in_specs=[vm], out_specs=vm, out_shape=jax.ShapeDtypeStruct((rows, lanes), F32),
        scratch_shapes=[pltpu.VMEM((n_dev, rows, lanes), F32), pltpu.SemaphoreType.DMA((n_dev - 1,)),
                        pltpu.SemaphoreType.DMA((n_dev - 1,))],
        compiler_params=pltpu.CompilerParams(vmem_limit_bytes=V7X_VMEM_LIMIT_BYTES),
    )(buf)


def _row_tile(rows, limit=512):
    best = None
    for t in range(16, min(rows, limit) + 1, 16):
        if rows % t == 0:
            best = t
    return best if best is not None else rows


def _adamw_math(w, g, m, v):
    m = ADAM_B1 * m + (1.0 - ADAM_B1) * g
    v = ADAM_B2 * v + (1.0 - ADAM_B2) * (g * g)
    m_hat = m / (1.0 - ADAM_B1 ** ADAM_STEP)
    v_hat = v / (1.0 - ADAM_B2 ** ADAM_STEP)
    delta = -ADAM_LR * (m_hat / (jnp.sqrt(v_hat) + ADAM_EPS) + ADAM_WD * w)
    return delta, m, v


def _adamw_layer(w, m, v, same_plane, other_plane, layer, prev, name):
    nl, r, ccol = w.shape
    ns = same_plane.shape[0]
    tr = _row_tile(r, 256)

    def body(w_ref, m_ref, v_ref, a_ref, b_ref, *rest):
        go_ref, d_ref, mo_ref, vo_ref = rest[-4:]
        sa = a_ref[0].astype(F32)
        sb = b_ref[0].astype(F32)
        for k in range(1, ns):
            sa = sa + a_ref[k].astype(F32)
            sb = sb + b_ref[k].astype(F32)
        g = sa + sb
        go_ref[...] = g
        d_ref[...], mo_ref[...], vo_ref[...] = _adamw_math(w_ref[...], g, m_ref[...], v_ref[...])

    blk = pl.BlockSpec((None, tr, ccol), lambda i: (layer, i, 0))
    plane = pl.BlockSpec((ns, tr, ccol), lambda i: (0, i, 0))
    out = jax.ShapeDtypeStruct((nl, r, ccol), F32)
    n_prev = 4 if prev is not None else 0
    return pl.pallas_call(
        body, name=name, grid=(r // tr,),
        in_specs=[blk, blk, blk, plane, plane] + [_ANY] * n_prev,
        out_specs=[blk, blk, blk, blk], out_shape=[out, out, out, out],
        input_output_aliases={5 + k: k for k in range(n_prev)},
        compiler_params=_cparams(("parallel",)))(w, m, v, same_plane, other_plane, *(prev or ()))


def _adamw_whole(w, g, m, v, name):
    def body(w_ref, g_ref, m_ref, v_ref, d_ref, mo_ref, vo_ref):
        d_ref[...], mo_ref[...], vo_ref[...] = _adamw_math(w_ref[...], g_ref[...], m_ref[...], v_ref[...])

    out = jax.ShapeDtypeStruct(w.shape, F32)
    return pl.pallas_call(body, name=name, out_shape=[out, out, out],
                          compiler_params=pltpu.CompilerParams(vmem_limit_bytes=V7X_VMEM_LIMIT_BYTES))(w, g, m, v)


_W_NAMES = ['meta_tokens', 'ffn1_pre_g', 'ffn1_w_gate', 'ffn1_w_up', 'ffn1_w_down', 'ffn1_post_g', 'mix_pre_g', 'w_in',
            'lru_conv_w', 'lru_conv_b', 'lru_w_a', 'lru_b_a', 'lru_w_x', 'lru_b_x', 'lru_lambda', 'mla_q_norm_g', 'mla_w_uq',
            'mla_kv_norm_g', 'mla_w_ukv', 'lru_out_g', 'mla_out_g', 'w_out', 'mix_post_g', 'ffn2_pre_g', 'ffn2_w_gate',
            'ffn2_w_up', 'ffn2_w_down', 'ffn2_post_g']
_REPLICATED = ['ffn1_pre_g', 'ffn1_post_g', 'mix_pre_g', 'lru_conv_b', 'lru_w_a', 'lru_b_a', 'lru_w_x', 'lru_b_x', 'lru_lambda',
               'mla_q_norm_g', 'mla_kv_norm_g', 'lru_out_g', 'mla_out_g', 'mix_post_g', 'ffn2_pre_g', 'ffn2_post_g']
_LANES = 128


def _pack_rows(arrays):
    flat = jnp.concatenate([a.reshape(-1) for a in arrays])
    total = _round_up(flat.shape[0], 8 * _LANES)
    return jnp.pad(flat, (0, total - flat.shape[0])).reshape(-1, _LANES)


def _unpack_rows(buf, shapes):
    flat = buf.reshape(-1)
    out, off = [], 0
    for shp in shapes:
        n = math.prod(shp)
        out.append(flat[off:off + n].reshape(shp))
        off += n
    return out


def _block_diag(w):
    nl, nh, n, _ = w.shape
    eye = jnp.eye(nh, dtype=w.dtype)
    return (w[:, :, :, None, :] * eye[None, :, None, :, None]).reshape(nl, nh * n, nh * n)


def _diag_blocks(bd, nh):
    n = bd.shape[0] // nh
    b4 = bd.reshape(nh, n, nh, n)
    return jnp.stack([b4[i, :, i, :] for i in range(nh)])


def kernel(x, meta_tokens, ffn1_pre_g, ffn1_w_gate, ffn1_w_up, ffn1_w_down, ffn1_post_g, mix_pre_g, w_in, lru_conv_w, lru_conv_b, lru_w_a, lru_b_a, lru_w_x, lru_b_x, lru_lambda, mla_q_norm_g, mla_w_uq, mla_kv_norm_g, mla_w_ukv, lru_out_g, mla_out_g, w_out, mix_post_g, ffn2_pre_g, ffn2_w_gate, ffn2_w_up, ffn2_w_down, ffn2_post_g, loss_target, m_meta_tokens, m_ffn1_pre_g, m_ffn1_w_gate, m_ffn1_w_up, m_ffn1_w_down, m_ffn1_post_g, m_mix_pre_g, m_w_in, m_lru_conv_w, m_lru_conv_b, m_lru_w_a, m_lru_b_a, m_lru_w_x, m_lru_b_x, m_lru_lambda, m_mla_q_norm_g, m_mla_w_uq, m_mla_kv_norm_g, m_mla_w_ukv, m_lru_out_g, m_mla_out_g, m_w_out, m_mix_post_g, m_ffn2_pre_g, m_ffn2_w_gate, m_ffn2_w_up, m_ffn2_w_down, m_ffn2_post_g, v_meta_tokens, v_ffn1_pre_g, v_ffn1_w_gate, v_ffn1_w_up, v_ffn1_w_down, v_ffn1_post_g, v_mix_pre_g, v_w_in, v_lru_conv_w, v_lru_conv_b, v_lru_w_a, v_lru_b_a, v_lru_w_x, v_lru_b_x, v_lru_lambda, v_mla_q_norm_g, v_mla_w_uq, v_mla_kv_norm_g, v_mla_w_ukv, v_lru_out_g, v_mla_out_g, v_w_out, v_mix_post_g, v_ffn2_pre_g, v_ffn2_w_gate, v_ffn2_w_up, v_ffn2_w_down, v_ffn2_post_g):
    a = dict(locals())
    x2 = x[0]
    s_len, d = x2.shape
    nl = ffn1_pre_g.shape[0]
    lw = lru_conv_b.shape[-1]
    ql, kvl = mla_q_norm_g.shape[-1], mla_kv_norm_g.shape[-1]
    nh = MLA_HEADS
    vd = (d - lw) // nh
    t_real = N_META + s_len
    tp = _round_up(t_real, 384)
    tm = tp // 6
    tmb = tm // 2
    tq = 384 if tp >= 1536 else 128
    xi, yi, _ = _mesh_pos()
    chip = 2 * xi + yi

    gu_loc = jnp.stack([ffn1_w_gate, ffn1_w_up, ffn2_w_gate, ffn2_w_up], axis=1).astype(BF16)
    dn_loc = jnp.stack([ffn1_w_down, ffn2_w_down], axis=1).astype(BF16)
    loc = [gu_loc, dn_loc, w_in.astype(BF16), mla_w_uq.astype(BF16), mla_w_ukv.astype(BF16), w_out.astype(BF16)]
    loc_shapes, loc_dtypes = [t.shape[1:] for t in loc], [t.dtype for t in loc]

    def layer_gather(l):
        return _GatherJob(loc, [lambda r, l=l: r.at[l]] * len(loc), loc_shapes, loc_dtypes)

    first = _GatherJob(loc + [lru_conv_w, meta_tokens], [lambda r: r.at[0]] * len(loc) + [lambda r: r] * 2,
                       loc_shapes + [lru_conv_w.shape, meta_tokens.shape], loc_dtypes + [F32, F32])
    *gathered, g_conv, g_meta = _run_job(first, "gather_layer0")
    o0, o1, o2, o3 = lw, 2 * lw, 2 * lw + ql, 2 * lw + ql + kvl

    def cols_full(g):
        return jnp.transpose(g, (1, 0, 2)).reshape(g.shape[1], -1)

    def assemble(gw):
        w_gu, w_dn, g_win, g_wuq, g_wukv, g_wout = gw
        w_in_full = cols_full(g_win)
        w_kr = jnp.pad(w_in_full[:, o3:], ((0, 0), (QK_NOPE, 0)))
        w_ukv3 = cols_full(g_wukv).reshape(kvl, nh, QK_NOPE + vd)
        w_out_full = g_wout.reshape(d, d)
        return dict(gu=w_gu, dn=w_dn,
                    parts=(w_in_full[:, :o0], w_in_full[:, o0:o1], w_in_full[:, o1:o2], w_in_full[:, o2:o3], w_kr),
                    q=jnp.transpose(cols_full(g_wuq).reshape(ql, nh, QK_DIM), (1, 0, 2)),
                    k=jnp.transpose(jnp.pad(w_ukv3[..., :QK_NOPE], ((0, 0), (0, 0), (0, QK_ROPE))), (1, 0, 2)),
                    v=jnp.transpose(w_ukv3[..., QK_NOPE:], (1, 0, 2)),
                    lo=w_out_full[:lw], mo=w_out_full[lw:].reshape(nh, vd, d))

    conv_full = jnp.transpose(g_conv, (1, 2, 0, 3)).reshape(nl, CONV_WIDTH, lw)
    meta_full = jnp.transpose(g_meta, (1, 0, 2)).reshape(N_META, d)
    wa_bd = _block_diag(lru_w_a).astype(BF16)
    wx_bd = _block_diag(lru_w_x).astype(BF16)

    def gain3(g):
        return g.reshape(nl, 1, g.shape[-1])

    g3 = {n: gain3(a[n]) for n in ('ffn1_pre_g', 'ffn1_post_g', 'mix_pre_g', 'lru_conv_b', 'lru_b_a', 'lru_b_x', 'lru_lambda',
                                   'mla_q_norm_g', 'mla_kv_norm_g', 'lru_out_g', 'mix_post_g', 'ffn2_pre_g', 'ffn2_post_g')}
    g_mo = mla_out_g.reshape(nl, nh, 1, vd)

    pos = jnp.arange(tp, dtype=F32)
    inv_freq = 1.0 / (ROPE_THETA ** (jnp.arange(0, QK_ROPE, 2, dtype=F32) / QK_ROPE))
    ang = pos[:, None] * inv_freq[None, :]
    cos_t = jnp.concatenate([jnp.ones((tp, QK_NOPE), F32), jnp.cos(ang), jnp.cos(ang)], axis=1)
    sin_t = jnp.concatenate([jnp.zeros((tp, QK_NOPE), F32), jnp.sin(ang), jnp.sin(ang)], axis=1)
    half = QK_ROPE // 2
    rot = np.zeros((QK_DIM, QK_DIM), np.float32)
    for i in range(half):
        rot[QK_NOPE + half + i, QK_NOPE + i] = -1.0
        rot[QK_NOPE + i, QK_NOPE + half + i] = 1.0
    rot_b, rot_t = jnp.asarray(rot, BF16), jnp.asarray(rot.T, BF16)

    h = jnp.concatenate([meta_full, x2, jnp.zeros((tp - t_real, d), F32)], axis=0)
    target = jnp.pad(loss_target[0], ((N_META, tp - t_real), (0, 0)))
    saved, weights = [], []
    for l in range(nl):
        wl = assemble(gathered)
        weights.append(wl)
        h1, gate1, up1, f1 = _ffn_fwd(h, g3['ffn1_pre_g'], g3['ffn1_post_g'], wl['gu'], wl['dn'], l, 0, 1, 0, tm, f"ffn1_fwd_{l}")
        xr, gr, cq, ckv, q, k, v = _mix_in_fwd(h1, g3['mix_pre_g'], wl['parts'], g3['mla_q_norm_g'], g3['mla_kv_norm_g'],
                                               wl['q'], wl['k'], wl['v'], cos_t, sin_t, rot_b, l, tm, f"mix_in_fwd_{l}")
        y_lru, xc, hs = _lru_fwd(xr, gr, conv_full, g3['lru_conv_b'], wa_bd, wx_bd, g3['lru_b_a'], g3['lru_b_x'],
                                 g3['lru_lambda'], l, tm, f"lru_fwd_{l}")
        o, lse, gathered = _attn_fwd(q, k, v, tq, f"attn_fwd_{l}", layer_gather(l + 1) if l + 1 < nl else None)
        h2, y = _mix_out_fwd(h1, y_lru, o, g3['lru_out_g'], g_mo, wl['lo'], wl['mo'], g3['mix_post_g'], l, tm, f"mix_out_fwd_{l}")
        h3, gate2, up2, f2 = _ffn_fwd(h2, g3['ffn2_pre_g'], g3['ffn2_post_g'], wl['gu'], wl['dn'], l, 2, 3, 1, tm, f"ffn2_fwd_{l}")
        saved.append((h, gate1, up1, f1, h1, xr, gr, cq, ckv, q, k, v, y_lru, xc, hs, o, lse, y, h2, gate2, up2, f2))
        h = h3

    dh, loss_parts = _loss_head(h, target, s_len, tm, "loss_head")
    loss = lax.psum(jnp.sum(loss_parts[:, 0, 0]), ("x", "y", "c"))

    small = {n: [None] * nl for n in _REPLICATED + ['lru_conv_w']}
    planes = {}
    late_names = ['w_in', 'mla_w_uq', 'mla_w_ukv', 'ffn1_w_gate', 'ffn1_w_up', 'ffn1_w_down']
    early_names = ['ffn2_w_gate', 'ffn2_w_up', 'ffn2_w_down', 'w_out']
    late = None

    def keep_planes(keys, outs):
        for i, key in enumerate(keys):
            planes[key] = (outs[i], outs[len(keys) + i])

    for l in reversed(range(nl)):
        (h0, gate1, up1, f1, h1, xr, gr, cq, ckv, q, k, v, y_lru, xc, hs, o, lse, y, h2, gate2, up2, f2) = saved[l]
        wl = weights[l]
        dh, dgate, dup, act, df, u, dgpre, dgpost = _ffn_bwd(dh, h2, f2, gate2, up2, g3['ffn2_pre_g'], g3['ffn2_post_g'],
                                                            wl['gu'], wl['dn'], l, 2, 3, 1, tmb, f"ffn2_bwd_{l}")
        small['ffn2_pre_g'][l], small['ffn2_post_g'][l] = dgpre, dgpost
        d_wg2 = _mm_tn(u, dgate, tm, BF16, f"dw_gate2_{l}")
        d_wu2 = _mm_tn(u, dup, tm, BF16, f"dw_up2_{l}")
        dn2 = _mm_tn(act, df, tm, BF16, f"dw_down2_{l}")

        d_ylru, d_o, dy_b, n_lo, n_mo, dgpost, dglo, dgmo = _mix_out_bwd(dh, y, y_lru, o, g3['lru_out_g'], g_mo, wl['lo'], wl['mo'],
                                                                       g3['mix_post_g'], l, tm, f"mix_out_bwd_{l}")
        small['mix_post_g'][l], small['lru_out_g'][l], small['mla_out_g'][l] = dgpost, dglo, dgmo
        d_wlo = _mm_tn(n_lo, dy_b, tm, F32, f"dw_out_lru_{l}")
        d_wmo = _mm_tn(n_mo, dy_b, tm, F32, f"dw_out_mla_{l}")
        d_wout = jnp.concatenate([d_wlo, d_wmo.reshape(nh * vd, d)], axis=0).reshape(N_CHIPS, d // N_CHIPS, d).astype(BF16)

        keys = [(n, l) for n in early_names] + ([(n, l + 1) for n in late_names] if late else [])
        dq, dk, dv, outs = _attn_bwd(q, k, v, o, d_o, lse, tq, f"attn_bwd_{l}",
                                     _ReduceJob([d_wg2, d_wu2, dn2, d_wout] + (late or [])))
        keep_planes(keys, outs)
        d_xr, d_gr, dcw, dcb, dwa, dwx, dba, dbx, dlam = _lru_bwd(d_ylru, xr, gr, xc, hs, conv_full, wa_bd, wx_bd, g3['lru_b_a'],
                                                                 g3['lru_b_x'], g3['lru_lambda'], l, tm, f"lru_bwd_{l}")
        small['lru_conv_w'][l], small['lru_conv_b'][l] = dcw, dcb
        small['lru_w_a'][l], small['lru_w_x'][l] = _diag_blocks(dwa, LRU_HEADS), _diag_blocks(dwx, LRU_HEADS)
        small['lru_b_a'][l], small['lru_b_x'][l], small['lru_lambda'][l] = dba, dbx, dlam

        dh, u, qn, kvn, dq_pre, d_cq, d_ckv, d_kr, dgpre, dgq, dgkv = _mix_in_bwd(
            dh, h1, cq, ckv, d_xr, d_gr, dq, dk, dv, g3['mix_pre_g'], wl['parts'], g3['mla_q_norm_g'], g3['mla_kv_norm_g'],
            wl['q'], wl['k'], wl['v'], cos_t, sin_t, rot_t, l, tmb, f"mix_in_bwd_{l}")
        small['mix_pre_g'][l], small['mla_q_norm_g'][l], small['mla_kv_norm_g'][l] = dgpre, dgq, dgkv
        d_win = jnp.concatenate([_mm_tn(u, d_xr, tm, F32, f"dw_in_xr_{l}"), _mm_tn(u, d_gr, tm, F32, f"dw_in_gr_{l}"),
                                 _mm_tn(u, d_cq, tm, F32, f"dw_in_cq_{l}"), _mm_tn(u, d_ckv, tm, F32, f"dw_in_ckv_{l}"),
                                 _mm_tn(u, d_kr, tm, F32, f"dw_in_kr_{l}")[:, QK_NOPE:]], axis=1)
        d_win = jnp.transpose(d_win.reshape(d, N_CHIPS, -1), (1, 0, 2)).astype(BF16)
        d_wq = _mm_tn(qn, dq_pre, tm, F32, f"dw_uq_{l}")
        d_wuq = jnp.transpose(jnp.transpose(d_wq, (1, 0, 2)).reshape(ql, N_CHIPS, -1), (1, 0, 2)).astype(BF16)
        d_wk = _mm_tn(kvn, dk, tm, F32, f"dw_uk_{l}")[..., :QK_NOPE]
        d_wv = _mm_tn(kvn, dv, tm, F32, f"dw_uv_{l}")
        d_wukv = jnp.transpose(jnp.concatenate([d_wk, d_wv], axis=-1), (1, 0, 2))
        d_wukv = jnp.transpose(d_wukv.reshape(kvl, N_CHIPS, -1), (1, 0, 2)).astype(BF16)

        dh, dgate, dup, act, df, u, dgpre, dgpost = _ffn_bwd(dh, h0, f1, gate1, up1, g3['ffn1_pre_g'], g3['ffn1_post_g'],
                                                            wl['gu'], wl['dn'], l, 0, 1, 0, tmb, f"ffn1_bwd_{l}")
        small['ffn1_pre_g'][l], small['ffn1_post_g'][l] = dgpre, dgpost
        d_wg1 = _mm_tn(u, dgate, tm, BF16, f"dw_gate1_{l}")
        d_wu1 = _mm_tn(u, dup, tm, BF16, f"dw_up1_{l}")
        dn1 = _mm_tn(act, df, tm, BF16, f"dw_down1_{l}")
        late = [d_win, d_wuq, d_wukv, d_wg1, d_wu1, dn1]

    grad_x = dh[N_META:t_real][None]
    keep_planes([(n, 0) for n in late_names], _run_job(_ReduceJob(late), "reduce_last"))

    res = {}
    for name in early_names + late_names:
        prev = None
        for l in reversed(range(nl)):
            same_plane, other_plane = planes[(name, l)]
            prev = _adamw_layer(a[name], a['m_' + name], a['v_' + name], same_plane, other_plane, l, prev, f"adamw_{name}_{l}")
        res[name] = prev

    small_full = {n: jnp.stack(small[n]).reshape(a[n].shape if n != 'lru_conv_w' else conv_full.shape)
                  for n in _REPLICATED + ['lru_conv_w']}
    order = _REPLICATED + ['lru_conv_w']
    packed = _pack_rows([small_full[n] for n in order] + [dh[:N_META]])
    summed = _allreduce_small(packed, "allreduce_small")
    pieces = _unpack_rows(summed, [small_full[n].shape for n in order] + [(N_META, d)])
    g_small = dict(zip(order, pieces[:-1]))
    g_conv_loc = lax.dynamic_slice_in_dim(g_small.pop('lru_conv_w'), chip * lru_conv_w.shape[-1], lru_conv_w.shape[-1], axis=2)
    g_meta_loc = lax.dynamic_slice_in_dim(pieces[-1], chip * meta_tokens.shape[-1], meta_tokens.shape[-1], axis=1)

    shapes = [a[n].shape for n in _REPLICATED]
    dl, mo, vo = _adamw_whole(_pack_rows([a[n] for n in _REPLICATED]), _pack_rows([g_small[n] for n in _REPLICATED]),
                              _pack_rows([a['m_' + n] for n in _REPLICATED]), _pack_rows([a['v_' + n] for n in _REPLICATED]),
                              "adamw_replicated")
    for n, dd, mm, vv in zip(_REPLICATED, _unpack_rows(dl, shapes), _unpack_rows(mo, shapes), _unpack_rows(vo, shapes)):
        res[n] = (g_small[n], dd, mm, vv)
    cshape = lru_conv_w.shape
    c2 = (cshape[0] * cshape[1], cshape[2])
    dd, mm, vv = _adamw_whole(lru_conv_w.reshape(c2), g_conv_loc.reshape(c2), m_lru_conv_w.reshape(c2), v_lru_conv_w.reshape(c2),
                              "adamw_conv_w")
    res['lru_conv_w'] = (g_conv_loc, dd.reshape(cshape), mm.reshape(cshape), vv.reshape(cshape))
    res['meta_tokens'] = (g_meta_loc,) + tuple(_adamw_whole(meta_tokens, g_meta_loc, m_meta_tokens, v_meta_tokens, "adamw_meta"))

    return (loss, grad_x, *[res[n][0] for n in _W_NAMES], *[res[n][1] for n in _W_NAMES],
            *[res[n][2] for n in _W_NAMES], *[res[n][3] for n in _W_NAMES])
```

```python
import functools
import math

import jax
import jax.numpy as jnp
import numpy as np
from jax import lax
from jax.experimental import pallas as pl
from jax.experimental.pallas import tpu as pltpu

F32 = jnp.float32
BF16 = jnp.bfloat16
MESH = pl.DeviceIdType.MESH

EPS = 1e-6
N_META = 16
LRU_HEADS = 8
MLA_HEADS = 8
QK_NOPE = 64
QK_ROPE = 32
QK_DIM = QK_NOPE + QK_ROPE
LRU_C = 8.0
ROPE_THETA = 10000.0
CONV_WIDTH = 4
N_CHIPS = 4

ADAM_LR = 0.001
ADAM_B1 = 0.9
ADAM_B2 = 0.999
ADAM_EPS = 1e-08
ADAM_WD = 0.01
ADAM_STEP = 10

V7X_VMEM_LIMIT_BYTES = 56 * 1024 * 1024
NEG_BIG = -1e30
Q_PRESCALE = QK_DIM ** -0.5 * math.log2(math.e)


def _cparams(sem=None):
    return pltpu.CompilerParams(dimension_semantics=sem, vmem_limit_bytes=V7X_VMEM_LIMIT_BYTES)


def _round_up(a, b):
    return -(-a // b) * b


def _whole(*shape):
    return pl.BlockSpec(shape, lambda *_: (0,) * len(shape))


_ANY = pl.BlockSpec(memory_space=pl.ANY)


def _dot(a, b):
    return jnp.dot(a.astype(BF16), b.astype(BF16), preferred_element_type=F32)


def _dot_nt(a, b):
    return lax.dot_general(a.astype(BF16), b.astype(BF16), (((1,), (1,)), ((), ())), preferred_element_type=F32)


def _dot_tn(a, b):
    return lax.dot_general(a.astype(BF16), b.astype(BF16), (((0,), (0,)), ((), ())), preferred_element_type=F32)


def _dot_split(x, p):
    hi = x.astype(BF16)
    lo = (x - hi.astype(F32)).astype(BF16)
    return jnp.dot(hi, p, preferred_element_type=F32) + jnp.dot(lo, p, preferred_element_type=F32)


def _rms(x, g):
    r = lax.rsqrt(jnp.mean(x * x, axis=-1, keepdims=True) + EPS)
    return x * r * g, r


def _rms_bwd(x, g, dy):
    r = lax.rsqrt(jnp.mean(x * x, axis=-1, keepdims=True) + EPS)
    xh = x * r
    dyg = dy * g
    dx = r * (dyg - xh * jnp.mean(dyg * xh, axis=-1, keepdims=True))
    dg = jnp.sum(dy * xh, axis=0, keepdims=True)
    return dx, dg


def _sigmoid(x):
    return 1.0 / (1.0 + jnp.exp(-x))


def _neg_expm1(x):
    series = -x * (1.0 + x * (0.5 + x * (1.0 / 6.0 + x * (1.0 / 24.0))))
    return jnp.where(jnp.abs(x) < 0.03, series, 1.0 - jnp.exp(x))


_GELU_K = math.sqrt(2.0 / math.pi)


def _gelu(x):
    t = jnp.tanh(_GELU_K * (x + 0.044715 * x * x * x))
    return 0.5 * x * (1.0 + t), t


def _gelu_grad(x, t):
    return 0.5 * (1.0 + t) + 0.5 * x * (1.0 - t * t) * _GELU_K * (1.0 + 3.0 * 0.044715 * x * x)


def _rope(x, cos_t, sin_t, rot):
    return x * cos_t + _dot_split(x, rot) * sin_t


def _ffn_fwd(h, g_pre, g_post, w_ffn, layer, kind_gate, kind_up, kind_dn, tm, name):
    tp, d = h.shape
    n_slot, fs = w_ffn.shape[0], w_ffn.shape[-2]

    def body(h_ref, gpre_ref, gpost_ref, wg_ref, wu_ref, wd_ref, hout_ref, gate_ref, up_ref, f_ref, u_sc, acc_sc):
        j = pl.program_id(1)

        @pl.when(j == 0)
        def _():
            u, _ = _rms(h_ref[...], gpre_ref[...])
            u_sc[...] = u.astype(BF16)
            acc_sc[...] = jnp.zeros_like(acc_sc)

        u = u_sc[...]
        gate = _dot_nt(u, wg_ref[...])
        up = _dot_nt(u, wu_ref[...])
        act = gate * _sigmoid(gate) * up
        gate_ref[...] = gate.astype(BF16)
        up_ref[...] = up.astype(BF16)
        acc_sc[...] += jnp.dot(act.astype(BF16), wd_ref[...], preferred_element_type=F32)

        @pl.when(j == n_slot - 1)
        def _():
            f = acc_sc[...]
            f_ref[...] = f
            n, _ = _rms(f, gpost_ref[...])
            hout_ref[...] = h_ref[...] + 0.5 * n

    tok = pl.BlockSpec((tm, d), lambda i, j: (i, 0))
    gain = pl.BlockSpec((None, 1, d), lambda i, j: (layer, 0, 0))
    slot_act = pl.BlockSpec((None, tm, fs), lambda i, j: (j, i, 0))
    return pl.pallas_call(
        body, name=name, grid=(tp // tm, n_slot),
        in_specs=[tok, gain, gain] + [pl.BlockSpec((None, None, fs, d), lambda i, j, kind=kind: (j, kind, 0, 0))
                                      for kind in (kind_gate, kind_up, kind_dn)],
        out_specs=[tok, slot_act, slot_act, tok],
        out_shape=[jax.ShapeDtypeStruct((tp, d), F32), jax.ShapeDtypeStruct((n_slot, tp, fs), BF16),
                   jax.ShapeDtypeStruct((n_slot, tp, fs), BF16), jax.ShapeDtypeStruct((tp, d), F32)],
        scratch_shapes=[pltpu.VMEM((tm, d), BF16), pltpu.VMEM((tm, d), F32)],
        compiler_params=_cparams(("parallel", "arbitrary")),
    )(h, g_pre, g_post, w_ffn, w_ffn, w_ffn)


def _ffn_bwd(dh_out, h, f, gate, up, g_pre, g_post, w_ffn, layer, kind_gate, kind_up, kind_dn, tm, name):
    tp, d = h.shape
    n_slot, fs = w_ffn.shape[0], w_ffn.shape[-2]

    def body(dho_ref, h_ref, f_ref, gate_ref, up_ref, gpre_ref, gpost_ref, wg_ref, wu_ref, wd_ref,
             dh_ref, dgate_ref, dup_ref, act_ref, df_ref, u_ref, dgpre_ref, dgpost_ref, du_sc):
        i, j = pl.program_id(0), pl.program_id(1)

        @pl.when((i == 0) & (j == 0))
        def _():
            dgpre_ref[...] = jnp.zeros_like(dgpre_ref)
            dgpost_ref[...] = jnp.zeros_like(dgpost_ref)

        @pl.when(j == 0)
        def _():
            df, dg = _rms_bwd(f_ref[...], gpost_ref[...], 0.5 * dho_ref[...])
            df_ref[...] = df.astype(BF16)
            dgpost_ref[...] += dg
            u, _ = _rms(h_ref[...], gpre_ref[...])
            u_ref[...] = u.astype(BF16)
            du_sc[...] = jnp.zeros_like(du_sc)

        g = gate_ref[...].astype(F32)
        u_ = up_ref[...].astype(F32)
        sg = _sigmoid(g)
        silu = g * sg
        dact = _dot_nt(df_ref[...], wd_ref[...])
        dup = dact * silu
        dgate = dact * u_ * (sg * (1.0 + g * (1.0 - sg)))
        act_ref[...] = (silu * u_).astype(BF16)
        dup_b = dup.astype(BF16)
        dgate_b = dgate.astype(BF16)
        dup_ref[...] = dup_b
        dgate_ref[...] = dgate_b
        du_sc[...] += _dot(dgate_b, wg_ref[...]) + _dot(dup_b, wu_ref[...])

        @pl.when(j == n_slot - 1)
        def _():
            dx, dg = _rms_bwd(h_ref[...], gpre_ref[...], du_sc[...])
            dh_ref[...] = dho_ref[...] + dx
            dgpre_ref[...] += dg

    tok = pl.BlockSpec((tm, d), lambda i, j: (i, 0))
    gain = pl.BlockSpec((None, 1, d), lambda i, j: (layer, 0, 0))
    acc = pl.BlockSpec((1, d), lambda i, j: (0, 0))
    slot_act = pl.BlockSpec((None, tm, fs), lambda i, j: (j, i, 0))
    act_shape = jax.ShapeDtypeStruct((n_slot, tp, fs), BF16)
    return pl.pallas_call(
        body, name=name, grid=(tp // tm, n_slot),
        in_specs=[tok, tok, tok, slot_act, slot_act, gain, gain]
        + [pl.BlockSpec((None, None, fs, d), lambda i, j, kind=kind: (j, kind, 0, 0)) for kind in (kind_gate, kind_up, kind_dn)],
        out_specs=[tok, slot_act, slot_act, slot_act, tok, tok, acc, acc],
        out_shape=[jax.ShapeDtypeStruct((tp, d), F32), act_shape, act_shape, act_shape,
                   jax.ShapeDtypeStruct((tp, d), BF16), jax.ShapeDtypeStruct((tp, d), BF16),
                   jax.ShapeDtypeStruct((1, d), F32), jax.ShapeDtypeStruct((1, d), F32)],
        scratch_shapes=[pltpu.VMEM((tm, d), F32)],
        compiler_params=_cparams(("arbitrary", "arbitrary")),
    )(dh_out, h, f, gate, up, g_pre, g_post, w_ffn, w_ffn, w_ffn)


def _mm_tn(a, b, tk, out_dtype, name):
    ga = a.shape[0] if a.ndim == 3 else None
    gb = b.shape[0] if b.ndim == 3 else None
    groups = ga or gb or 1
    t, m = a.shape[-2:]
    n = b.shape[-1]
    nk = t // tk

    def body(a_ref, b_ref, o_ref, acc_sc):
        k = pl.program_id(1)

        @pl.when(k == 0)
        def _():
            acc_sc[...] = jnp.zeros_like(acc_sc)

        acc_sc[...] += _dot_tn(a_ref[...], b_ref[...])

        @pl.when(k == nk - 1)
        def _():
            o_ref[...] = acc_sc[...].astype(out_dtype)

    a_spec = (pl.BlockSpec((None, tk, m), lambda g, k: (g, k, 0)) if ga else pl.BlockSpec((tk, m), lambda g, k: (k, 0)))
    b_spec = (pl.BlockSpec((None, tk, n), lambda g, k: (g, k, 0)) if gb else pl.BlockSpec((tk, n), lambda g, k: (k, 0)))
    out = pl.pallas_call(
        body, name=name, grid=(groups, nk),
        in_specs=[a_spec, b_spec],
        out_specs=pl.BlockSpec((None, m, n), lambda g, k: (g, 0, 0)),
        out_shape=jax.ShapeDtypeStruct((groups, m, n), out_dtype),
        scratch_shapes=[pltpu.VMEM((m, n), F32)],
        compiler_params=_cparams(("parallel", "arbitrary")),
    )(a, b)
    return out if (ga or gb) else out[0]


def _mm_tn_heads(a, b, tk, name):
    groups, t, m = a.shape
    n = b.shape[-1]
    nk = t // tk

    def body(a_ref, b_ref, o_ref):
        k = pl.program_id(0)

        @pl.when(k == 0)
        def _():
            o_ref[...] = jnp.zeros_like(o_ref)

        bb = b_ref[...]
        for g in range(groups):
            o_ref[g] += _dot_tn(a_ref[g], bb)

    return pl.pallas_call(
        body, name=name, grid=(nk,),
        in_specs=[pl.BlockSpec((groups, tk, m), lambda k: (0, k, 0)), pl.BlockSpec((tk, n), lambda k: (k, 0))],
        out_specs=pl.BlockSpec((groups, m, n), lambda k: (0, 0, 0)),
        out_shape=jax.ShapeDtypeStruct((groups, m, n), F32),
        compiler_params=_cparams(("arbitrary",)),
    )(a, b)


def _mix_in_fwd(h, g_pre, w_parts, g_q, g_kv, w_q, w_k, w_v, cos_t, sin_t, rot, layer, tm, name):
    tp, d = h.shape
    w_xr, w_gr, w_cq, w_ckv, w_kr = w_parts
    lw, ql, kvl = w_xr.shape[0], w_cq.shape[0], w_ckv.shape[0]
    nh, vd = w_v.shape[0], w_v.shape[-1]

    def body(h_ref, gpre_ref, wxr_ref, wgr_ref, wcq_ref, wckv_ref, wkr_ref, gq_ref, gkv_ref, wq_ref, wk_ref, wv_ref,
             cos_ref, sin_ref, rot_ref, xr_ref, gr_ref, cq_ref, ckv_ref, q_ref, k_ref, v_ref):
        u, _ = _rms(h_ref[...], gpre_ref[...])
        u = u.astype(BF16)
        xr_ref[...] = _dot_nt(u, wxr_ref[...])
        gr_ref[...] = _dot_nt(u, wgr_ref[...])
        cq = _dot_nt(u, wcq_ref[...])
        ckv = _dot_nt(u, wckv_ref[...])
        kr = _dot_nt(u, wkr_ref[...])
        cq_ref[...] = cq
        ckv_ref[...] = ckv
        cos_b, sin_b, rot_b = cos_ref[...], sin_ref[...], rot_ref[...]
        qn = _rms(cq, gq_ref[...])[0].astype(BF16)
        kvn = _rms(ckv, gkv_ref[...])[0].astype(BF16)
        k_rope = _rope(kr, cos_b, sin_b, rot_b)
        for hd in range(nh):
            q_pre = _dot_nt(qn, wq_ref[hd])
            q_ref[hd] = (_rope(q_pre, cos_b, sin_b, rot_b) * Q_PRESCALE).astype(BF16)
            k_ref[hd] = (jnp.dot(kvn, wk_ref[hd], preferred_element_type=F32) + k_rope).astype(BF16)
            v_ref[hd] = jnp.dot(kvn, wv_ref[hd], preferred_element_type=F32).astype(BF16)

    def tok(n):
        return pl.BlockSpec((tm, n), lambda i: (i, 0))

    def lay(*shape):
        return pl.BlockSpec((None,) + shape, lambda i: (layer,) + (0,) * len(shape))

    def heads(n):
        return pl.BlockSpec((nh, tm, n), lambda i: (0, i, 0))

    return pl.pallas_call(
        body, name=name, grid=(tp // tm,),
        in_specs=[tok(d), lay(1, d), _whole(lw, d), _whole(lw, d), _whole(ql, d), _whole(kvl, d), _whole(QK_DIM, d),
                  lay(1, ql), lay(1, kvl),
                  _whole(nh, QK_DIM, ql), _whole(nh, kvl, QK_DIM), _whole(nh, kvl, vd), tok(QK_DIM), tok(QK_DIM),
                  pl.BlockSpec((QK_DIM, QK_DIM), lambda i: (0, 0))],
        out_specs=[tok(lw), tok(lw), tok(ql), tok(kvl), heads(QK_DIM), heads(QK_DIM), heads(vd)],
        out_shape=[jax.ShapeDtypeStruct((tp, lw), F32), jax.ShapeDtypeStruct((tp, lw), F32),
                   jax.ShapeDtypeStruct((tp, ql), F32), jax.ShapeDtypeStruct((tp, kvl), F32),
                   jax.ShapeDtypeStruct((nh, tp, QK_DIM), BF16), jax.ShapeDtypeStruct((nh, tp, QK_DIM), BF16),
                   jax.ShapeDtypeStruct((nh, tp, vd), BF16)],
        compiler_params=_cparams(("parallel",)),
    )(h, g_pre, w_xr, w_gr, w_cq, w_ckv, w_kr, g_q, g_kv, w_q, w_k, w_v, cos_t, sin_t, rot)


def _mix_in_bwd(dh_res, h, cq, ckv, d_xr, d_gr, dq, dk, dv, g_pre, w_parts, g_q, g_kv, w_q, w_k, w_v,
                cos_t, sin_t, rot_t, layer, tm, name):
    tp, d = h.shape
    w_xr, w_gr, w_cq, w_ckv, w_kr = w_parts
    lw, ql, kvl = w_xr.shape[0], w_cq.shape[0], w_ckv.shape[0]
    nh, vd = w_v.shape[0], w_v.shape[-1]

    def body(dhr_ref, h_ref, cq_ref, ckv_ref, dxr_ref, dgr_ref, dq_ref, dk_ref, dv_ref, gpre_ref,
             wxr_ref, wgr_ref, wcq_ref, wckv_ref, wkr_ref, gq_ref, gkv_ref, wq_ref, wk_ref, wv_ref,
             cos_ref, sin_ref, rott_ref,
             dh_ref, u_ref, qn_ref, kvn_ref, dqpre_ref, dcq_ref, dckv_ref, dkr_ref, dgpre_ref, dgq_ref, dgkv_ref):
        i = pl.program_id(0)

        @pl.when(i == 0)
        def _():
            dgpre_ref[...] = jnp.zeros_like(dgpre_ref)
            dgq_ref[...] = jnp.zeros_like(dgq_ref)
            dgkv_ref[...] = jnp.zeros_like(dgkv_ref)

        cos_b, sin_b, rott_b = cos_ref[...], sin_ref[...], rott_ref[...]
        hh = h_ref[...]
        u, _ = _rms(hh, gpre_ref[...])
        u_ref[...] = u.astype(BF16)
        cq, ckv = cq_ref[...], ckv_ref[...]
        qn = _rms(cq, gq_ref[...])[0]
        kvn = _rms(ckv, gkv_ref[...])[0]
        qn_ref[...] = qn.astype(BF16)
        kvn_ref[...] = kvn.astype(BF16)
        d_qn = jnp.zeros((tm, ql), F32)
        d_kvn = jnp.zeros((tm, kvl), F32)
        d_krope = jnp.zeros((tm, QK_DIM), F32)
        for hd in range(nh):
            dq_h = dq_ref[hd]
            dq_pre = dq_h * cos_b + _dot_split(dq_h * sin_b, rott_b)
            dq_pre_b = dq_pre.astype(BF16)
            dqpre_ref[hd] = dq_pre_b
            d_qn += _dot(dq_pre_b, wq_ref[hd])
            dk_h = dk_ref[hd]
            d_krope += dk_h
            d_kvn += _dot_nt(dk_h, wk_ref[hd]) + _dot_nt(dv_ref[hd], wv_ref[hd])
        d_kr = d_krope * cos_b + _dot_split(d_krope * sin_b, rott_b)
        d_cq, dgq = _rms_bwd(cq, gq_ref[...], d_qn)
        d_ckv, dgkv = _rms_bwd(ckv, gkv_ref[...], d_kvn)
        dgq_ref[...] += dgq
        dgkv_ref[...] += dgkv
        d_cq_b, d_ckv_b, d_kr_b = d_cq.astype(BF16), d_ckv.astype(BF16), d_kr.astype(BF16)
        dcq_ref[...] = d_cq_b
        dckv_ref[...] = d_ckv_b
        dkr_ref[...] = d_kr_b
        du = (_dot(dxr_ref[...], wxr_ref[...]) + _dot(dgr_ref[...], wgr_ref[...]) + _dot(d_cq_b, wcq_ref[...])
              + _dot(d_ckv_b, wckv_ref[...]) + _dot(d_kr_b, wkr_ref[...]))
        dx, dg = _rms_bwd(hh, gpre_ref[...], du)
        dh_ref[...] = dhr_ref[...] + dx
        dgpre_ref[...] += dg

    def tok(n):
        return pl.BlockSpec((tm, n), lambda i: (i, 0))

    def lay(*shape):
        return pl.BlockSpec((None,) + shape, lambda i: (layer,) + (0,) * len(shape))

    def heads(n):
        return pl.BlockSpec((nh, tm, n), lambda i: (0, i, 0))

    def acc(n):
        return pl.BlockSpec((1, n), lambda i: (0, 0))

    return pl.pallas_call(
        body, name=name, grid=(tp // tm,),
        in_specs=[tok(d), tok(d), tok(ql), tok(kvl), tok(lw), tok(lw), heads(QK_DIM), heads(QK_DIM), heads(vd), lay(1, d),
                  _whole(lw, d), _whole(lw, d), _whole(ql, d), _whole(kvl, d), _whole(QK_DIM, d), lay(1, ql), lay(1, kvl),
                  _whole(nh, QK_DIM, ql), _whole(nh, kvl, QK_DIM), _whole(nh, kvl, vd), tok(QK_DIM), tok(QK_DIM),
                  pl.BlockSpec((QK_DIM, QK_DIM), lambda i: (0, 0))],
        out_specs=[tok(d), tok(d), tok(ql), tok(kvl), heads(QK_DIM), tok(ql), tok(kvl), tok(QK_DIM), acc(d), acc(ql), acc(kvl)],
        out_shape=[jax.ShapeDtypeStruct((tp, d), F32), jax.ShapeDtypeStruct((tp, d), BF16),
                   jax.ShapeDtypeStruct((tp, ql), BF16), jax.ShapeDtypeStruct((tp, kvl), BF16),
                   jax.ShapeDtypeStruct((nh, tp, QK_DIM), BF16), jax.ShapeDtypeStruct((tp, ql), BF16),
                   jax.ShapeDtypeStruct((tp, kvl), BF16), jax.ShapeDtypeStruct((tp, QK_DIM), BF16),
                   jax.ShapeDtypeStruct((1, d), F32), jax.ShapeDtypeStruct((1, ql), F32), jax.ShapeDtypeStruct((1, kvl), F32)],
        compiler_params=_cparams(("arbitrary",)),
    )(dh_res, h, cq, ckv, d_xr, d_gr, dq, dk, dv, g_pre, w_xr, w_gr, w_cq, w_ckv, w_kr, g_q, g_kv, w_q, w_k, w_v,
      cos_t, sin_t, rot_t)


def _lru_gates(xc, wa_ref, wx_ref, ba, bx, sp):
    xcb = xc.astype(BF16)
    r = _sigmoid(jnp.dot(xcb, wa_ref[...], preferred_element_type=F32) + ba)
    ig = _sigmoid(jnp.dot(xcb, wx_ref[...], preferred_element_type=F32) + bx)
    log_a = -LRU_C * r * sp
    a = jnp.exp(log_a)
    sq = jnp.sqrt(_neg_expm1(2.0 * log_a))
    return r, ig, a, sq


def _softplus(x):
    return jnp.maximum(x, 0.0) + jnp.log(1.0 + jnp.exp(-jnp.abs(x)))


def _lru_fwd(xr, gr, conv_w, conv_b, wa_bd, wx_bd, b_a, b_x, lam, layer, tc, name):
    tp, w = xr.shape
    pad = 8

    def body(xr_ref, gr_ref, cw_ref, cb_ref, wa_ref, wx_ref, ba_ref, bx_ref, lam_ref, y_ref, xc_ref, hs_ref,
             xe_sc, a_sc, b_sc, st_sc):
        c = pl.program_id(0)

        @pl.when(c == 0)
        def _():
            xe_sc[pl.ds(0, pad), :] = jnp.zeros((pad, w), F32)
            st_sc[...] = jnp.zeros_like(st_sc)

        xe_sc[pl.ds(pad, tc), :] = xr_ref[...]
        xc = cb_ref[...] + xe_sc[pl.ds(pad, tc), :] * cw_ref[pl.ds(CONV_WIDTH - 1, 1), :]
        for k in range(CONV_WIDTH - 1):
            xc = xc + xe_sc[pl.ds(pad - (CONV_WIDTH - 1) + k, tc), :] * cw_ref[pl.ds(k, 1), :]
        xe_sc[pl.ds(0, pad), :] = xe_sc[pl.ds(tc, pad), :]
        xc_ref[...] = xc
        sp = _softplus(-lam_ref[...])
        _, ig, a, sq = _lru_gates(xc, wa_ref, wx_ref, ba_ref[...], bx_ref[...], sp)
        a_sc[...] = a
        b_sc[...] = sq * (ig * xc)

        def step(t, hcur):
            hnew = a_sc[pl.ds(t, 1), :] * hcur + b_sc[pl.ds(t, 1), :]
            hs_ref[pl.ds(t, 1), :] = hnew
            return hnew

        st_sc[...] = lax.fori_loop(0, tc, step, st_sc[...], unroll=8)
        y_ref[...] = hs_ref[...] * _gelu(gr_ref[...])[0]

    tok = pl.BlockSpec((tc, w), lambda c: (c, 0))

    def lay(*shape):
        return pl.BlockSpec((None,) + shape, lambda c: (layer,) + (0,) * len(shape))

    out = jax.ShapeDtypeStruct((tp, w), F32)
    return pl.pallas_call(
        body, name=name, grid=(tp // tc,),
        in_specs=[tok, tok, lay(CONV_WIDTH, w), lay(1, w), lay(w, w), lay(w, w), lay(1, w), lay(1, w), lay(1, w)],
        out_specs=[tok, tok, tok], out_shape=[out, out, out],
        scratch_shapes=[pltpu.VMEM((tc + pad, w), F32), pltpu.VMEM((tc, w), F32), pltpu.VMEM((tc, w), F32),
                        pltpu.VMEM((1, w), F32)],
        compiler_params=_cparams(("arbitrary",)),
    )(xr, gr, conv_w, conv_b, wa_bd, wx_bd, b_a, b_x, lam)


def _lru_bwd(dy, xr, gr, xc, hs, conv_w, wa_bd, wx_bd, b_a, b_x, lam, layer, tc, name):
    tp, w = xr.shape
    pad = 8
    nc = tp // tc
    per = tc // pad

    def body(dy_ref, xr_ref, gr_ref, xc_ref, hs_ref, xrp_ref, hsp_ref, cw_ref, wa_ref, wx_ref, ba_ref, bx_ref, lam_ref,
             dxr_ref, dgr_ref, dcw_ref, dcb_ref, dwa_ref, dwx_ref, dba_ref, dbx_ref, dlam_ref,
             a_sc, dh_sc, dxc_sc, he_sc, xe_sc, carry_sc):
        s = pl.program_id(0)
        first_chunk = s == nc - 1

        @pl.when(s == 0)
        def _():
            for ref in (dcw_ref, dcb_ref, dwa_ref, dwx_ref, dba_ref, dbx_ref, dlam_ref):
                ref[...] = jnp.zeros_like(ref)
            carry_sc[...] = jnp.zeros_like(carry_sc)
            dxc_sc[pl.ds(tc, pad), :] = jnp.zeros((pad, w), F32)

        keep = jnp.where(first_chunk, 0.0, 1.0)
        he_sc[pl.ds(0, pad), :] = hsp_ref[...] * keep
        he_sc[pl.ds(pad, tc), :] = hs_ref[...]
        xe_sc[pl.ds(0, pad), :] = xrp_ref[...] * keep
        xe_sc[pl.ds(pad, tc), :] = xr_ref[...]

        lam_v = lam_ref[...]
        sp = _softplus(-lam_v)
        xc = xc_ref[...]
        r, ig, a, sq = _lru_gates(xc, wa_ref, wx_ref, ba_ref[...], bx_ref[...], sp)
        a_sc[...] = a
        grv = gr_ref[...]
        gl, th = _gelu(grv)
        dyv = dy_ref[...]
        dgr_ref[...] = dyv * hs_ref[...] * _gelu_grad(grv, th)
        dh_sc[...] = dyv * gl

        def step(n, g):
            t = tc - 1 - n
            dh = dh_sc[pl.ds(t, 1), :] + g
            dh_sc[pl.ds(t, 1), :] = dh
            return a_sc[pl.ds(t, 1), :] * dh

        carry_sc[...] = lax.fori_loop(0, tc, step, carry_sc[...], unroll=8)

        dh = dh_sc[...]
        d_a = dh * he_sc[pl.ds(pad - 1, tc), :]
        d_ixc = dh * sq
        d_sq = dh * (ig * xc)
        a2 = a * a
        d_la = d_a * a - d_sq * a2 / sq
        d_r = d_la * (-LRU_C * sp)
        d_sp = jnp.sum(d_la * (-LRU_C * r), axis=0, keepdims=True)
        dlam_ref[...] += d_sp * (-_sigmoid(-lam_v))
        d_pa = d_r * r * (1.0 - r)
        d_px = d_ixc * xc * ig * (1.0 - ig)
        dba_ref[...] += jnp.sum(d_pa, axis=0, keepdims=True)
        dbx_ref[...] += jnp.sum(d_px, axis=0, keepdims=True)
        d_pa_b, d_px_b = d_pa.astype(BF16), d_px.astype(BF16)
        xcb = xc.astype(BF16)
        dwa_ref[...] += _dot_tn(xcb, d_pa_b)
        dwx_ref[...] += _dot_tn(xcb, d_px_b)
        d_xc = d_ixc * ig + _dot_nt(d_pa_b, wa_ref[...]) + _dot_nt(d_px_b, wx_ref[...])
        dcb_ref[...] += jnp.sum(d_xc, axis=0, keepdims=True)
        dxc_sc[pl.ds(0, tc), :] = d_xc
        d_xr = jnp.zeros((tc, w), F32)
        for k in range(CONV_WIDTH):
            off = CONV_WIDTH - 1 - k
            d_xr = d_xr + dxc_sc[pl.ds(off, tc), :] * cw_ref[pl.ds(k, 1), :]
            dcw_ref[pl.ds(k, 1), :] += jnp.sum(d_xc * xe_sc[pl.ds(pad - off, tc), :], axis=0, keepdims=True)
        dxr_ref[...] = d_xr
        dxc_sc[pl.ds(tc, pad), :] = dxc_sc[pl.ds(0, pad), :]

    def rev(cidx):
        return nc - 1 - cidx

    tok = pl.BlockSpec((tc, w), lambda c: (rev(c), 0))
    prev = pl.BlockSpec((pad, w), lambda c: (jnp.maximum(rev(c) * per - 1, 0), 0))

    def lay(*shape):
        return pl.BlockSpec((None,) + shape, lambda c: (layer,) + (0,) * len(shape))

    def acc(*shape):
        return pl.BlockSpec(shape, lambda c: (0,) * len(shape))

    big = jax.ShapeDtypeStruct((tp, w), F32)
    vec = jax.ShapeDtypeStruct((1, w), F32)
    return pl.pallas_call(
        body, name=name, grid=(nc,),
        in_specs=[tok, tok, tok, tok, tok, prev, prev, lay(CONV_WIDTH, w), lay(w, w), lay(w, w), lay(1, w), lay(1, w), lay(1, w)],
        out_specs=[tok, tok, acc(CONV_WIDTH, w), acc(1, w), acc(w, w), acc(w, w), acc(1, w), acc(1, w), acc(1, w)],
        out_shape=[big, big, jax.ShapeDtypeStruct((CONV_WIDTH, w), F32), vec, jax.ShapeDtypeStruct((w, w), F32),
                   jax.ShapeDtypeStruct((w, w), F32), vec, vec, vec],
        scratch_shapes=[pltpu.VMEM((tc, w), F32), pltpu.VMEM((tc, w), F32), pltpu.VMEM((tc + pad, w), F32),
                        pltpu.VMEM((tc + pad, w), F32), pltpu.VMEM((tc + pad, w), F32), pltpu.VMEM((1, w), F32)],
        compiler_params=_cparams(("arbitrary",)),
    )(dy, xr, gr, xc, hs, xr, hs, conv_w, wa_bd, wx_bd, b_a, b_x, lam)


def _attn_fwd(q, k, v, tq, name, job=None):
    nh, tp, dqk = q.shape
    vd = v.shape[-1]
    nq = tp // tq
    ts = tq
    n_sub = tq // ts
    n_ji = len(job.inputs) if job else 0
    n_jo = len(job.out_shape) if job else 0

    def body(q_ref, k_ref, v_ref, *rest):
        job_in, (o_ref, lse_ref), rest = rest[:n_ji], rest[n_ji:n_ji + 2], rest[n_ji + 2:]
        job_out, sems = rest[:n_jo], rest[n_jo:]
        i = pl.program_id(1)
        if job:
            first = (pl.program_id(0) == 0) & (i == 0)
            pl.when(first)(lambda: job.start(job_in, job_out, sems))
        q_rows = [q_ref[pl.ds(g * ts, ts), :] for g in range(n_sub)]

        def step(j, carry, diagonal):
            off = pl.multiple_of(j * tq, tq)
            kb = k_ref[pl.ds(off, tq), :]
            vb = v_ref[pl.ds(off, tq), :]
            out = []
            for g in range(n_sub):
                m, l, acc = carry[g]
                s = _dot_nt(q_rows[g], kb)
                if diagonal:
                    keep = (lax.broadcasted_iota(jnp.int32, (ts, tq), 1)
                            <= lax.broadcasted_iota(jnp.int32, (ts, tq), 0) + g * ts)
                    s = jnp.where(keep, s, NEG_BIG)
                m_new = jnp.maximum(m, jnp.max(s, axis=-1, keepdims=True))
                p = jnp.exp2(s - m_new)
                alpha = jnp.exp2(m - m_new)
                l = alpha * l + jnp.sum(p, axis=-1, keepdims=True)
                acc = alpha * acc + jnp.dot(p.astype(BF16), vb, preferred_element_type=F32)
                out.append((m_new, l, acc))
            return tuple(out)

        init = tuple((jnp.full((ts, 1), NEG_BIG, F32), jnp.zeros((ts, 1), F32), jnp.zeros((ts, vd), F32))
                     for _ in range(n_sub))
        below = lax.fori_loop(0, i, lambda j, c: step(j, c, False), init)
        for g, (m, l, acc) in enumerate(step(i, below, True)):
            o_ref[pl.ds(g * ts, ts), :] = acc / l
            lse_ref[pl.ds(g * ts, ts), :] = m + jnp.log2(l)
        if job:
            last = (pl.program_id(0) == nh - 1) & (i == nq - 1)
            pl.when(last)(lambda: job.finish(job_in, job_out, sems))

    outs = pl.pallas_call(
        body, name=name, grid=(nh, nq),
        in_specs=[pl.BlockSpec((None, tq, dqk), lambda h, i: (h, i, 0)),
                  pl.BlockSpec((None, tp, dqk), lambda h, i: (h, 0, 0)),
                  pl.BlockSpec((None, tp, vd), lambda h, i: (h, 0, 0))] + [_ANY] * n_ji,
        out_specs=[pl.BlockSpec((None, tq, vd), lambda h, i: (h, i, 0)),
                   pl.BlockSpec((None, tq, 1), lambda h, i: (h, i, 0))] + [_ANY] * n_jo,
        out_shape=[jax.ShapeDtypeStruct((nh, tp, vd), F32), jax.ShapeDtypeStruct((nh, tp, 1), F32)]
        + (list(job.out_shape) if job else []),
        scratch_shapes=list(job.scratch) if job else [],
        compiler_params=_cparams(("arbitrary", "arbitrary") if job else ("parallel", "parallel")),
    )(q, k, v, *(job.inputs if job else ()))
    return outs[0], outs[1], list(outs[2:])


def _attn_bwd(q, k, v, o, do, lse, tq, name, job=None):
    nh, tp, dqk = q.shape
    vd = v.shape[-1]
    scale = dqk ** -0.5
    nq = tp // tq
    ts = tq
    n_sub = tq // ts
    n_ji = len(job.inputs) if job else 0
    n_jo = len(job.out_shape) if job else 0

    def body(q_ref, k_ref, v_ref, o_ref, do_ref, lse_ref, *rest):
        job_in, (dq_ref, dk_ref, dv_ref), rest = rest[:n_ji], rest[n_ji:n_ji + 3], rest[n_ji + 3:]
        job_out, (lse_rows, delta_rows, do_bf, dqt_sc), sems = rest[:n_jo], rest[n_jo:n_jo + 4], rest[n_jo + 4:]
        kb_i = pl.program_id(1)
        if job:
            first = (pl.program_id(0) == 0) & (kb_i == 0)
            pl.when(first)(lambda: job.start(job_in, job_out, sems))

        def as_rows(col):
            return jnp.transpose(jnp.broadcast_to(col, (tq, _LANES)))[0:8, :]

        @pl.when(kb_i == 0)
        def _():
            dqt_sc[...] = jnp.zeros_like(dqt_sc)

            def prep(qi, _):
                off = pl.multiple_of(qi * tq, tq)
                dob = do_ref[pl.ds(off, tq), :]
                do_bf[pl.ds(off, tq), :] = dob.astype(BF16)
                delta_rows[qi] = as_rows(jnp.sum(dob * o_ref[pl.ds(off, tq), :], axis=-1, keepdims=True))
                lse_rows[qi] = as_rows(lse_ref[pl.ds(off, tq), :])
                return 0

            lax.fori_loop(0, nq, prep, 0)

        k_rows = [k_ref[pl.ds(g * ts, ts), :] for g in range(n_sub)]
        v_rows = [v_ref[pl.ds(g * ts, ts), :] for g in range(n_sub)]

        def pair(qi, carry, diagonal):
            off = pl.multiple_of(qi * tq, tq)
            qb = q_ref[pl.ds(off, tq), :]
            dob = do_bf[pl.ds(off, tq), :]
            lse_row = lse_rows[qi, 0:1, :]
            delta_row = delta_rows[qi, 0:1, :]
            out, dqt = [], None
            for g in range(n_sub):
                dk, dv = carry[g]
                st = _dot_nt(k_rows[g], qb)
                if diagonal:
                    keep = (lax.broadcasted_iota(jnp.int32, (ts, tq), 0) + g * ts
                            <= lax.broadcasted_iota(jnp.int32, (ts, tq), 1))
                    st = jnp.where(keep, st, NEG_BIG)
                pt = jnp.exp2(st - lse_row)
                dpt = _dot_nt(v_rows[g], dob)
                dst = (pt * (dpt - delta_row)).astype(BF16)
                dv = dv + jnp.dot(pt.astype(BF16), dob, preferred_element_type=F32)
                dk = dk + jnp.dot(dst, qb, preferred_element_type=F32)
                part = _dot_tn(k_rows[g], dst)
                dqt = part if dqt is None else dqt + part
                out.append((dk, dv))
            dqt_sc[qi] += dqt
            return tuple(out)

        zero = tuple((jnp.zeros((ts, dqk), F32), jnp.zeros((ts, vd), F32)) for _ in range(n_sub))
        done = lax.fori_loop(kb_i + 1, nq, lambda qi, c: pair(qi, c, False), pair(kb_i, zero, True))
        for g, (dk, dv) in enumerate(done):
            dk_ref[pl.ds(g * ts, ts), :] = dk * math.log(2.0)
            dv_ref[pl.ds(g * ts, ts), :] = dv

        @pl.when(kb_i == nq - 1)
        def _():
            eye = (lax.broadcasted_iota(jnp.int32, (dqk, dqk), 0) == lax.broadcasted_iota(jnp.int32, (dqk, dqk), 1))
            eye = jnp.where(eye, 1.0, 0.0).astype(BF16)
            for qi in range(nq):
                t = dqt_sc[qi]
                hi = t.astype(BF16)
                lo = (t - hi.astype(F32)).astype(BF16)
                dq_ref[pl.ds(qi * tq, tq), :] = scale * (_dot_tn(hi, eye) + _dot_tn(lo, eye))

        if job:
            last = (pl.program_id(0) == nh - 1) & (kb_i == nq - 1)
            pl.when(last)(lambda: job.finish(job_in, job_out, sems))

    def full(n):
        return pl.BlockSpec((None, tp, n), lambda h, j: (h, 0, 0))

    def blk(n):
        return pl.BlockSpec((None, tq, n), lambda h, j: (h, j, 0))

    outs = pl.pallas_call(
        body, name=name, grid=(nh, nq),
        in_specs=[full(dqk), blk(dqk), blk(vd), full(vd), full(vd), full(1)] + [_ANY] * n_ji,
        out_specs=[full(dqk), blk(dqk), blk(vd)] + [_ANY] * n_jo,
        out_shape=[jax.ShapeDtypeStruct((nh, tp, dqk), F32), jax.ShapeDtypeStruct((nh, tp, dqk), F32),
                   jax.ShapeDtypeStruct((nh, tp, vd), F32)] + (list(job.out_shape) if job else []),
        scratch_shapes=[pltpu.VMEM((nq, 8, tq), F32), pltpu.VMEM((nq, 8, tq), F32), pltpu.VMEM((tp, vd), BF16),
                        pltpu.VMEM((nq, dqk, tq), F32)] + (list(job.scratch) if job else []),
        compiler_params=_cparams(("arbitrary", "arbitrary") if job else ("parallel", "arbitrary")),
    )(q, k, v, o, do, lse, *(job.inputs if job else ()))
    return outs[0], outs[1], outs[2], list(outs[3:])


def _mix_out_fwd(h, y_lru, o, g_lo, g_mo, w_lo, w_mo, g_post, layer, tm, name):
    tp, d = h.shape
    lw = y_lru.shape[-1]
    nh, vd = o.shape[0], o.shape[-1]

    def body(h_ref, yl_ref, o_ref, glo_ref, gmo_ref, wlo_ref, wmo_ref, gpost_ref, hout_ref, y_ref):
        nl, _ = _rms(yl_ref[...], glo_ref[...])
        y = _dot(nl, wlo_ref[...])
        ssq = jnp.zeros((tm, 1), F32)
        for hd in range(nh):
            oh = o_ref[hd]
            ssq += jnp.sum(oh * oh, axis=-1, keepdims=True)
        r = lax.rsqrt(ssq / (nh * vd) + EPS)
        for hd in range(nh):
            y += _dot(o_ref[hd] * r * gmo_ref[hd], wmo_ref[hd])
        y_ref[...] = y
        hout_ref[...] = h_ref[...] + _rms(y, gpost_ref[...])[0]

    tok = pl.BlockSpec((tm, d), lambda i: (i, 0))

    def lay(*shape):
        return pl.BlockSpec((None,) + shape, lambda i: (layer,) + (0,) * len(shape))

    return pl.pallas_call(
        body, name=name, grid=(tp // tm,),
        in_specs=[tok, pl.BlockSpec((tm, lw), lambda i: (i, 0)), pl.BlockSpec((nh, tm, vd), lambda i: (0, i, 0)),
                  lay(1, lw), lay(nh, 1, vd), _whole(lw, d), _whole(nh, vd, d), lay(1, d)],
        out_specs=[tok, tok],
        out_shape=[jax.ShapeDtypeStruct((tp, d), F32), jax.ShapeDtypeStruct((tp, d), F32)],
        compiler_params=_cparams(("parallel",)),
    )(h, y_lru, o, g_lo, g_mo, w_lo, w_mo, g_post)


def _mix_out_bwd(dh_out, y, y_lru, o, g_lo, g_mo, w_lo, w_mo, g_post, layer, tm, name):
    tp, d = y.shape
    lw = y_lru.shape[-1]
    nh, vd = o.shape[0], o.shape[-1]

    def body(dho_ref, y_ref, yl_ref, o_ref, glo_ref, gmo_ref, wlo_ref, wmo_ref, gpost_ref,
             dyl_ref, do_ref, dy_ref, nl_ref, nm_ref, dgpost_ref, dglo_ref, dgmo_ref):
        i = pl.program_id(0)

        @pl.when(i == 0)
        def _():
            dgpost_ref[...] = jnp.zeros_like(dgpost_ref)
            dglo_ref[...] = jnp.zeros_like(dglo_ref)
            dgmo_ref[...] = jnp.zeros_like(dgmo_ref)

        dy, dgp = _rms_bwd(y_ref[...], gpost_ref[...], dho_ref[...])
        dgpost_ref[...] += dgp
        dy_b = dy.astype(BF16)
        dy_ref[...] = dy_b
        yl = yl_ref[...]
        nl_ref[...] = _rms(yl, glo_ref[...])[0].astype(BF16)
        dyl, dgl = _rms_bwd(yl, glo_ref[...], _dot_nt(dy_b, wlo_ref[...]))
        dyl_ref[...] = dyl
        dglo_ref[...] += dgl
        ssq = jnp.zeros((tm, 1), F32)
        for hd in range(nh):
            oh = o_ref[hd]
            ssq += jnp.sum(oh * oh, axis=-1, keepdims=True)
        r = lax.rsqrt(ssq / (nh * vd) + EPS)
        dn = []
        dot_sum = jnp.zeros((tm, 1), F32)
        for hd in range(nh):
            oh = o_ref[hd] * r
            nm_ref[hd] = (oh * gmo_ref[hd]).astype(BF16)
            dn_h = _dot_nt(dy_b, wmo_ref[hd])
            dgmo_ref[hd] += jnp.sum(dn_h * oh, axis=0, keepdims=True)
            dn_h = dn_h * gmo_ref[hd]
            dot_sum += jnp.sum(dn_h * oh, axis=-1, keepdims=True)
            dn.append(dn_h)
        dot_mean = dot_sum / (nh * vd)
        for hd in range(nh):
            do_ref[hd] = r * (dn[hd] - o_ref[hd] * r * dot_mean)

    tok = pl.BlockSpec((tm, d), lambda i: (i, 0))
    tokl = pl.BlockSpec((tm, lw), lambda i: (i, 0))
    heads = pl.BlockSpec((nh, tm, vd), lambda i: (0, i, 0))

    def lay(*shape):
        return pl.BlockSpec((None,) + shape, lambda i: (layer,) + (0,) * len(shape))

    def acc(*shape):
        return pl.BlockSpec(shape, lambda i: (0,) * len(shape))

    return pl.pallas_call(
        body, name=name, grid=(tp // tm,),
        in_specs=[tok, tok, tokl, heads, lay(1, lw), lay(nh, 1, vd), _whole(lw, d), _whole(nh, vd, d), lay(1, d)],
        out_specs=[tokl, heads, tok, tokl, heads, acc(1, d), acc(1, lw), acc(nh, 1, vd)],
        out_shape=[jax.ShapeDtypeStruct((tp, lw), F32), jax.ShapeDtypeStruct((nh, tp, vd), F32),
                   jax.ShapeDtypeStruct((tp, d), BF16), jax.ShapeDtypeStruct((tp, lw), BF16),
                   jax.ShapeDtypeStruct((nh, tp, vd), BF16), jax.ShapeDtypeStruct((1, d), F32),
                   jax.ShapeDtypeStruct((1, lw), F32), jax.ShapeDtypeStruct((nh, 1, vd), F32)],
        compiler_params=_cparams(("arbitrary",)),
    )(dh_out, y, y_lru, o, g_lo, g_mo, w_lo, w_mo, g_post)


def _loss_head(h, target, n_real, tm, name):
    tp, d = h.shape
    nt = tp // tm

    def body(h_ref, t_ref, dh_ref, part_ref):
        i = pl.program_id(0)
        row = i * tm + lax.broadcasted_iota(jnp.int32, (tm, 1), 0)
        real = (row >= N_META) & (row < N_META + n_real)
        err = jnp.where(real, h_ref[...] - t_ref[...], 0.0)
        dh_ref[...] = err * (1.0 / d)
        part_ref[...] = jnp.full((1, 1, 128), 0.5 / d * jnp.sum(err * err), F32)

    tok = pl.BlockSpec((tm, d), lambda i: (i, 0))
    return pl.pallas_call(
        body, name=name, grid=(nt,), in_specs=[tok, tok],
        out_specs=[tok, pl.BlockSpec((1, 1, 128), lambda i: (i, 0, 0))],
        out_shape=[jax.ShapeDtypeStruct((tp, d), F32), jax.ShapeDtypeStruct((nt, 1, 128), F32)],
        compiler_params=_cparams(("parallel",)),
    )(h, target)


def _mesh_pos():
    return lax.axis_index("x"), lax.axis_index("y"), lax.axis_index("c")


def _other_chips(x, y):
    return [(1 - x, y), (x, 1 - y), (1 - x, 1 - y)]


class _GatherJob:
    def __init__(self, inputs, picks, shard_shapes, dtypes, split_last=()):
        self.inputs = list(inputs)
        self.picks = list(picks)
        n = len(self.inputs)
        self.out_shape = [jax.ShapeDtypeStruct((N_CHIPS,) + tuple(s), dt) for s, dt in zip(shard_shapes, dtypes)]
        self.split_last = [a in split_last for a in range(n)]
        self.halves = [(s[-1] if last else s[0]) // 2 for s, last in zip(shard_shapes, self.split_last)]
        self.scratch = [pltpu.SemaphoreType.DMA((n, 3)), pltpu.SemaphoreType.DMA((n, 3)), pltpu.SemaphoreType.DMA((n, 3)),
                        pltpu.SemaphoreType.DMA((n, 3)), pltpu.SemaphoreType.DMA((n,))]

    def _copies(self, ins, outs, sems):
        ici_s, ici_r, core_s, core_r, loc = sems
        x, y, c = _mesh_pos()
        me = 2 * x + y
        chips = _other_chips(x, y)
        local, ici, fwd = [], [], []
        for a in range(len(ins)):
            src = self.picks[a](ins[a])
            hv = self.halves[a]

            def half(ref, which, a=a, hv=hv):
                return ref.at[:, pl.ds(which * hv, hv)] if self.split_last[a] else ref.at[pl.ds(which * hv, hv)]

            local.append(pltpu.make_async_copy(src, outs[a].at[me], loc.at[a]))
            for p, (px, py) in enumerate(chips):
                ici.append((pltpu.make_async_remote_copy(src_ref=half(src, c), dst_ref=half(outs[a].at[me], c),
                                                         send_sem=ici_s.at[a, p], recv_sem=ici_r.at[a, p],
                                                         device_id=(px, py, c), device_id_type=MESH),
                            pltpu.make_async_remote_copy(src_ref=half(src, c), dst_ref=half(outs[a].at[2 * px + py], c),
                                                         send_sem=ici_s.at[a, p], recv_sem=ici_r.at[a, p],
                                                         device_id=(px, py, c), device_id_type=MESH)))
                landed = half(outs[a].at[2 * px + py], c)
                theirs = half(outs[a].at[2 * px + py], 1 - c)
                fwd.append((pltpu.make_async_remote_copy(src_ref=landed, dst_ref=landed, send_sem=core_s.at[a, p],
                                                         recv_sem=core_r.at[a, p], device_id=(x, y, 1 - c), device_id_type=MESH),
                            pltpu.make_async_remote_copy(src_ref=theirs, dst_ref=theirs, send_sem=core_s.at[a, p],
                                                         recv_sem=core_r.at[a, p], device_id=(x, y, 1 - c), device_id_type=MESH)))
        return local, ici, fwd

    def start(self, ins, outs, sems):
        local, ici, _ = self._copies(ins, outs, sems)
        for cp in local:
            cp.start()
        for send, _ in ici:
            send.start()

    def finish(self, ins, outs, sems):
        local, ici, fwd = self._copies(ins, outs, sems)
        for (_, arrive), (hand_on, _) in zip(ici, fwd):
            arrive.wait_recv()
            hand_on.start()
        for _, arrive in fwd:
            arrive.wait_recv()
        for send, _ in ici:
            send.wait_send()
        for hand_on, _ in fwd:
            hand_on.wait_send()
        for cp in local:
            cp.wait()


class _ReduceJob:
    def __init__(self, inputs):
        self.inputs = list(inputs)
        n = len(self.inputs)
        self.out_shape = [jax.ShapeDtypeStruct(a.shape, a.dtype) for a in self.inputs] * 2
        self.scratch = [pltpu.SemaphoreType.DMA((n, 3)), pltpu.SemaphoreType.DMA((n, 3)), pltpu.SemaphoreType.DMA((n, 4)),
                        pltpu.SemaphoreType.DMA((n, 4)), pltpu.SemaphoreType.DMA((n,))]

    def _copies(self, ins, outs, sems):
        ici_s, ici_r, core_s, core_r, loc = sems
        n = len(ins)
        same, other = outs[:n], outs[n:]
        x, y, c = _mesh_pos()
        me = 2 * x + y
        chips = _other_chips(x, y)
        sib = (x, y, 1 - c)
        local, ici, fwd = [], [], []
        for a in range(n):
            local.append(pltpu.make_async_copy(ins[a].at[me], same[a].at[me], loc.at[a]))
            fwd.append((pltpu.make_async_remote_copy(src_ref=ins[a].at[me], dst_ref=other[a].at[me], send_sem=core_s.at[a, 3],
                                                     recv_sem=core_r.at[a, 3], device_id=sib, device_id_type=MESH), None))
            for p, (px, py) in enumerate(chips):
                cp = 2 * px + py
                ici.append((pltpu.make_async_remote_copy(src_ref=ins[a].at[cp], dst_ref=same[a].at[me], send_sem=ici_s.at[a, p],
                                                         recv_sem=ici_r.at[a, p], device_id=(px, py, c), device_id_type=MESH),
                            pltpu.make_async_remote_copy(src_ref=ins[a].at[cp], dst_ref=same[a].at[cp], send_sem=ici_s.at[a, p],
                                                         recv_sem=ici_r.at[a, p], device_id=(px, py, c), device_id_type=MESH)))
                fwd.append((pltpu.make_async_remote_copy(src_ref=same[a].at[cp], dst_ref=other[a].at[cp], send_sem=core_s.at[a, p],
                                                         recv_sem=core_r.at[a, p], device_id=sib, device_id_type=MESH), p))
        return local, ici, fwd

    def start(self, ins, outs, sems):
        local, ici, fwd = self._copies(ins, outs, sems)
        for cp in local:
            cp.start()
        for send, _ in ici:
            send.start()
        for hand_on, p in fwd:
            if p is None:
                hand_on.start()

    def finish(self, ins, outs, sems):
        local, ici, fwd = self._copies(ins, outs, sems)
        chip_fwd = [f for f in fwd if f[1] is not None]
        for (_, arrive), (hand_on, _) in zip(ici, chip_fwd):
            arrive.wait_recv()
            hand_on.start()
        for hand_on, _ in fwd:
            hand_on.wait()
        for send, _ in ici:
            send.wait_send()
        for cp in local:
            cp.wait()


def _run_job(job, name):
    n_in, n_out = len(job.inputs), len(job.out_shape)

    def body(*refs):
        ins, outs, sems = refs[:n_in], refs[n_in:n_in + n_out], refs[n_in + n_out:]
        job.start(ins, outs, sems)
        job.finish(ins, outs, sems)

    return list(pl.pallas_call(body, name=name, in_specs=[_ANY] * n_in, out_specs=[_ANY] * n_out, out_shape=list(job.out_shape),
                               scratch_shapes=list(job.scratch))(*job.inputs))


def _allreduce_small(buf, name):
    rows, lanes = buf.shape
    n_dev = 8

    def body(in_ref, out_ref, gather, send_sems, recv_sems):
        x, y, c = _mesh_pos()
        me = 4 * x + 2 * y + c
        gather[me] = in_ref[...]
        sends = []
        for rel in range(1, n_dev):
            fx, fy, fc = (rel >> 2) & 1, (rel >> 1) & 1, rel & 1
            peer = (1 - x if fx else x, 1 - y if fy else y, 1 - c if fc else c)
            rc = pltpu.make_async_remote_copy(src_ref=in_ref, dst_ref=gather.at[me], send_sem=send_sems.at[rel - 1],
                                              recv_sem=recv_sems.at[rel - 1], device_id=peer, device_id_type=MESH)
            rc.start()
            sends.append(rc)
        for rc in sends:
            rc.wait()
        total = gather[0]
        for k in range(1, n_dev):
            total = total + gather[k]
        out_ref[...] = total

    vm = pl.BlockSpec(memory_space=pltpu.VMEM)
    return pl.pallas_call(
        body, name=name, in_specs=[vm], out_specs=vm, out_shape=jax.ShapeDtypeStruct((rows, lanes), F32),
        scratch_shapes=[pltpu.VMEM((n_dev, rows, lanes), F32), pltpu.SemaphoreType.DMA((n_dev - 1,)),
                        pltpu.SemaphoreType.DMA((n_dev - 1,))],
        compiler_params=pltpu.CompilerParams(vmem_limit_bytes=V7X_VMEM_LIMIT_BYTES),
    )(buf)


def _row_tile(rows, limit=512):
    best = None
    for t in range(16, min(rows, limit) + 1, 16):
        if rows % t == 0:
            best = t
    return best if best is not None else rows


def _adamw_math(w, g, m, v):
    m = ADAM_B1 * m + (1.0 - ADAM_B1) * g
    v = ADAM_B2 * v + (1.0 - ADAM_B2) * (g * g)
    m_hat = m / (1.0 - ADAM_B1 ** ADAM_STEP)
    v_hat = v / (1.0 - ADAM_B2 ** ADAM_STEP)
    delta = -ADAM_LR * (m_hat / (jnp.sqrt(v_hat) + ADAM_EPS) + ADAM_WD * w)
    return delta, m, v


def _adamw_layer(w, m, v, same_plane, other_plane, layer, prev, name):
    nl, r, ccol = w.shape
    ns = same_plane.shape[0]
    tr = _row_tile(r, 256)

    def body(w_ref, m_ref, v_ref, a_ref, b_ref, *rest):
        go_ref, d_ref, mo_ref, vo_ref = rest[-4:]
        sa = a_ref[0].astype(F32)
        sb = b_ref[0].astype(F32)
        for k in range(1, ns):
            sa = sa + a_ref[k].astype(F32)
            sb = sb + b_ref[k].astype(F32)
        g = sa + sb
        go_ref[...] = g
        d_ref[...], mo_ref[...], vo_ref[...] = _adamw_math(w_ref[...], g, m_ref[...], v_ref[...])

    blk = pl.BlockSpec((None, tr, ccol), lambda i: (layer, i, 0))
    plane = pl.BlockSpec((ns, tr, ccol), lambda i: (0, i, 0))
    out = jax.ShapeDtypeStruct((nl, r, ccol), F32)
    n_prev = 4 if prev is not None else 0
    return pl.pallas_call(
        body, name=name, grid=(r // tr,),
        in_specs=[blk, blk, blk, plane, plane] + [_ANY] * n_prev,
        out_specs=[blk, blk, blk, blk], out_shape=[out, out, out, out],
        input_output_aliases={5 + k: k for k in range(n_prev)},
        compiler_params=_cparams(("parallel",)))(w, m, v, same_plane, other_plane, *(prev or ()))


def _adamw_whole(w, g, m, v, name):
    def body(w_ref, g_ref, m_ref, v_ref, d_ref, mo_ref, vo_ref):
        d_ref[...], mo_ref[...], vo_ref[...] = _adamw_math(w_ref[...], g_ref[...], m_ref[...], v_ref[...])

    out = jax.ShapeDtypeStruct(w.shape, F32)
    return pl.pallas_call(body, name=name, out_shape=[out, out, out],
                          compiler_params=pltpu.CompilerParams(vmem_limit_bytes=V7X_VMEM_LIMIT_BYTES))(w, g, m, v)


_W_NAMES = ['meta_tokens', 'ffn1_pre_g', 'ffn1_w_gate', 'ffn1_w_up', 'ffn1_w_down', 'ffn1_post_g', 'mix_pre_g', 'w_in',
            'lru_conv_w', 'lru_conv_b', 'lru_w_a', 'lru_b_a', 'lru_w_x', 'lru_b_x', 'lru_lambda', 'mla_q_norm_g', 'mla_w_uq',
            'mla_kv_norm_g', 'mla_w_ukv', 'lru_out_g', 'mla_out_g', 'w_out', 'mix_post_g', 'ffn2_pre_g', 'ffn2_w_gate',
            'ffn2_w_up', 'ffn2_w_down', 'ffn2_post_g']
_REPLICATED = ['ffn1_pre_g', 'ffn1_post_g', 'mix_pre_g', 'lru_conv_b', 'lru_w_a', 'lru_b_a', 'lru_w_x', 'lru_b_x', 'lru_lambda',
               'mla_q_norm_g', 'mla_kv_norm_g', 'lru_out_g', 'mla_out_g', 'mix_post_g', 'ffn2_pre_g', 'ffn2_post_g']
_LANES = 128


def _pack_rows(arrays):
    flat = jnp.concatenate([a.reshape(-1) for a in arrays])
    total = _round_up(flat.shape[0], 8 * _LANES)
    return jnp.pad(flat, (0, total - flat.shape[0])).reshape(-1, _LANES)


def _unpack_rows(buf, shapes):
    flat = buf.reshape(-1)
    out, off = [], 0
    for shp in shapes:
        n = math.prod(shp)
        out.append(flat[off:off + n].reshape(shp))
        off += n
    return out


def _block_diag(w):
    nl, nh, n, _ = w.shape
    eye = jnp.eye(nh, dtype=w.dtype)
    return (w[:, :, :, None, :] * eye[None, :, None, :, None]).reshape(nl, nh * n, nh * n)


def _diag_blocks(bd, nh):
    n = bd.shape[0] // nh
    b4 = bd.reshape(nh, n, nh, n)
    return jnp.stack([b4[i, :, i, :] for i in range(nh)])


def kernel(x, meta_tokens, ffn1_pre_g, ffn1_w_gate, ffn1_w_up, ffn1_w_down, ffn1_post_g, mix_pre_g, w_in, lru_conv_w, lru_conv_b, lru_w_a, lru_b_a, lru_w_x, lru_b_x, lru_lambda, mla_q_norm_g, mla_w_uq, mla_kv_norm_g, mla_w_ukv, lru_out_g, mla_out_g, w_out, mix_post_g, ffn2_pre_g, ffn2_w_gate, ffn2_w_up, ffn2_w_down, ffn2_post_g, loss_target, m_meta_tokens, m_ffn1_pre_g, m_ffn1_w_gate, m_ffn1_w_up, m_ffn1_w_down, m_ffn1_post_g, m_mix_pre_g, m_w_in, m_lru_conv_w, m_lru_conv_b, m_lru_w_a, m_lru_b_a, m_lru_w_x, m_lru_b_x, m_lru_lambda, m_mla_q_norm_g, m_mla_w_uq, m_mla_kv_norm_g, m_mla_w_ukv, m_lru_out_g, m_mla_out_g, m_w_out, m_mix_post_g, m_ffn2_pre_g, m_ffn2_w_gate, m_ffn2_w_up, m_ffn2_w_down, m_ffn2_post_g, v_meta_tokens, v_ffn1_pre_g, v_ffn1_w_gate, v_ffn1_w_up, v_ffn1_w_down, v_ffn1_post_g, v_mix_pre_g, v_w_in, v_lru_conv_w, v_lru_conv_b, v_lru_w_a, v_lru_b_a, v_lru_w_x, v_lru_b_x, v_lru_lambda, v_mla_q_norm_g, v_mla_w_uq, v_mla_kv_norm_g, v_mla_w_ukv, v_lru_out_g, v_mla_out_g, v_w_out, v_mix_post_g, v_ffn2_pre_g, v_ffn2_w_gate, v_ffn2_w_up, v_ffn2_w_down, v_ffn2_post_g):
    a = dict(locals())
    x2 = x[0]
    s_len, d = x2.shape
    nl = ffn1_pre_g.shape[0]
    lw = lru_conv_b.shape[-1]
    ql, kvl = mla_q_norm_g.shape[-1], mla_kv_norm_g.shape[-1]
    nh = MLA_HEADS
    vd = (d - lw) // nh
    t_real = N_META + s_len
    tp = _round_up(t_real, 384)
    tm = tp // 6
    tmb = tm // 2
    tq = 384 if tp >= 1536 else 128
    xi, yi, _ = _mesh_pos()
    chip = 2 * xi + yi

    def tr(w):
        return jnp.swapaxes(w, 1, 2)

    ffn_loc = jnp.stack([tr(ffn1_w_gate), tr(ffn1_w_up), ffn1_w_down, tr(ffn2_w_gate), tr(ffn2_w_up), ffn2_w_down],
                        axis=1).astype(BF16)
    loc = [ffn_loc, tr(w_in).astype(BF16), tr(mla_w_uq).astype(BF16), mla_w_ukv.astype(BF16), w_out.astype(BF16)]
    loc_shapes, loc_dtypes = [t.shape[1:] for t in loc], [t.dtype for t in loc]

    def layer_gather(l):
        return _GatherJob(loc, [lambda r, l=l: r.at[l]] * len(loc), loc_shapes, loc_dtypes, split_last=(1,))

    first = _GatherJob(loc + [lru_conv_w, meta_tokens], [lambda r: r.at[0]] * len(loc) + [lambda r: r] * 2,
                       loc_shapes + [lru_conv_w.shape, meta_tokens.shape], loc_dtypes + [F32, F32], split_last=(1,))
    *gathered, g_conv, g_meta = _run_job(first, "gather_layer0")
    o0, o1, o2, o3 = lw, 2 * lw, 2 * lw + ql, 2 * lw + ql + kvl

    def assemble(gw):
        w_ffn, g_win, g_wuq, g_wukv, g_wout = gw
        w_in_t = g_win.reshape(-1, d)
        w_kr = jnp.pad(w_in_t[o3:], ((QK_NOPE, 0), (0, 0)))
        w_ukv3 = jnp.transpose(g_wukv, (1, 0, 2)).reshape(kvl, nh, QK_NOPE + vd)
        w_out_full = g_wout.reshape(d, d)
        return dict(ffn=w_ffn, parts=(w_in_t[:o0], w_in_t[o0:o1], w_in_t[o1:o2], w_in_t[o2:o3], w_kr),
                    q=g_wuq.reshape(nh, QK_DIM, ql),
                    k=jnp.transpose(jnp.pad(w_ukv3[..., :QK_NOPE], ((0, 0), (0, 0), (0, QK_ROPE))), (1, 0, 2)),
                    v=jnp.transpose(w_ukv3[..., QK_NOPE:], (1, 0, 2)),
                    lo=w_out_full[:lw], mo=w_out_full[lw:].reshape(nh, vd, d))

    conv_full = jnp.transpose(g_conv, (1, 2, 0, 3)).reshape(nl, CONV_WIDTH, lw)
    meta_full = jnp.transpose(g_meta, (1, 0, 2)).reshape(N_META, d)
    wa_bd = _block_diag(lru_w_a).astype(BF16)
    wx_bd = _block_diag(lru_w_x).astype(BF16)

    def gain3(g):
        return g.reshape(nl, 1, g.shape[-1])

    g3 = {n: gain3(a[n]) for n in ('ffn1_pre_g', 'ffn1_post_g', 'mix_pre_g', 'lru_conv_b', 'lru_b_a', 'lru_b_x', 'lru_lambda',
                                   'mla_q_norm_g', 'mla_kv_norm_g', 'lru_out_g', 'mix_post_g', 'ffn2_pre_g', 'ffn2_post_g')}
    g_mo = mla_out_g.reshape(nl, nh, 1, vd)

    pos = jnp.arange(tp, dtype=F32)
    inv_freq = 1.0 / (ROPE_THETA ** (jnp.arange(0, QK_ROPE, 2, dtype=F32) / QK_ROPE))
    ang = pos[:, None] * inv_freq[None, :]
    cos_t = jnp.concatenate([jnp.ones((tp, QK_NOPE), F32), jnp.cos(ang), jnp.cos(ang)], axis=1)
    sin_t = jnp.concatenate([jnp.zeros((tp, QK_NOPE), F32), jnp.sin(ang), jnp.sin(ang)], axis=1)
    half = QK_ROPE // 2
    rot = np.zeros((QK_DIM, QK_DIM), np.float32)
    for i in range(half):
        rot[QK_NOPE + half + i, QK_NOPE + i] = -1.0
        rot[QK_NOPE + i, QK_NOPE + half + i] = 1.0
    rot_b, rot_t = jnp.asarray(rot, BF16), jnp.asarray(rot.T, BF16)

    h = jnp.concatenate([meta_full, x2, jnp.zeros((tp - t_real, d), F32)], axis=0)
    target = jnp.pad(loss_target[0], ((N_META, tp - t_real), (0, 0)))
    saved, weights = [], []
    for l in range(nl):
        wl = assemble(gathered)
        weights.append(wl)
        h1, gate1, up1, f1 = _ffn_fwd(h, g3['ffn1_pre_g'], g3['ffn1_post_g'], wl['ffn'], l, 0, 1, 2, tm, f"ffn1_fwd_{l}")
        xr, gr, cq, ckv, q, k, v = _mix_in_fwd(h1, g3['mix_pre_g'], wl['parts'], g3['mla_q_norm_g'], g3['mla_kv_norm_g'],
                                               wl['q'], wl['k'], wl['v'], cos_t, sin_t, rot_b, l, tm, f"mix_in_fwd_{l}")
        y_lru, xc, hs = _lru_fwd(xr, gr, conv_full, g3['lru_conv_b'], wa_bd, wx_bd, g3['lru_b_a'], g3['lru_b_x'],
                                 g3['lru_lambda'], l, tm, f"lru_fwd_{l}")
        o, lse, gathered = _attn_fwd(q, k, v, tq, f"attn_fwd_{l}", layer_gather(l + 1) if l + 1 < nl else None)
        h2, y = _mix_out_fwd(h1, y_lru, o, g3['lru_out_g'], g_mo, wl['lo'], wl['mo'], g3['mix_post_g'], l, tm, f"mix_out_fwd_{l}")
        h3, gate2, up2, f2 = _ffn_fwd(h2, g3['ffn2_pre_g'], g3['ffn2_post_g'], wl['ffn'], l, 3, 4, 5, tm, f"ffn2_fwd_{l}")
        saved.append((h, gate1, up1, f1, h1, xr, gr, cq, ckv, q, k, v, y_lru, xc, hs, o, lse, y, h2, gate2, up2, f2))
        h = h3

    dh, loss_parts = _loss_head(h, target, s_len, tm, "loss_head")
    loss = lax.psum(jnp.sum(loss_parts[:, 0, 0]), ("x", "y", "c"))

    small = {n: [None] * nl for n in _REPLICATED + ['lru_conv_w']}
    planes = {}
    late_names = ['w_in', 'mla_w_uq', 'mla_w_ukv', 'ffn1_w_gate', 'ffn1_w_up', 'ffn1_w_down']
    early_names = ['ffn2_w_gate', 'ffn2_w_up', 'ffn2_w_down', 'w_out']
    late = None

    def keep_planes(keys, outs):
        for i, key in enumerate(keys):
            planes[key] = (outs[i], outs[len(keys) + i])

    for l in reversed(range(nl)):
        (h0, gate1, up1, f1, h1, xr, gr, cq, ckv, q, k, v, y_lru, xc, hs, o, lse, y, h2, gate2, up2, f2) = saved[l]
        wl = weights[l]
        dh, dgate, dup, act, df, u, dgpre, dgpost = _ffn_bwd(dh, h2, f2, gate2, up2, g3['ffn2_pre_g'], g3['ffn2_post_g'],
                                                            wl['ffn'], l, 3, 4, 5, tmb, f"ffn2_bwd_{l}")
        small['ffn2_pre_g'][l], small['ffn2_post_g'][l] = dgpre, dgpost
        d_wg2 = _mm_tn(dgate, u, tm, BF16, f"dw_gate2_{l}")
        d_wu2 = _mm_tn(dup, u, tm, BF16, f"dw_up2_{l}")
        dn2 = _mm_tn(act, df, tm, BF16, f"dw_down2_{l}")

        d_ylru, d_o, dy_b, n_lo, n_mo, dgpost, dglo, dgmo = _mix_out_bwd(dh, y, y_lru, o, g3['lru_out_g'], g_mo, wl['lo'], wl['mo'],
                                                                       g3['mix_post_g'], l, tm, f"mix_out_bwd_{l}")
        small['mix_post_g'][l], small['lru_out_g'][l], small['mla_out_g'][l] = dgpost, dglo, dgmo
        d_wlo = _mm_tn(n_lo, dy_b, tm, F32, f"dw_out_lru_{l}")
        d_wmo = _mm_tn_heads(n_mo, dy_b, tm, f"dw_out_mla_{l}")
        d_wout = jnp.concatenate([d_wlo, d_wmo.reshape(nh * vd, d)], axis=0).reshape(N_CHIPS, d // N_CHIPS, d).astype(BF16)

        keys = [(n, l) for n in early_names] + ([(n, l + 1) for n in late_names] if late else [])
        dq, dk, dv, outs = _attn_bwd(q, k, v, o, d_o, lse, tq, f"attn_bwd_{l}",
                                     _ReduceJob([d_wg2, d_wu2, dn2, d_wout] + (late or [])))
        keep_planes(keys, outs)
        d_xr, d_gr, dcw, dcb, dwa, dwx, dba, dbx, dlam = _lru_bwd(d_ylru, xr, gr, xc, hs, conv_full, wa_bd, wx_bd, g3['lru_b_a'],
                                                                 g3['lru_b_x'], g3['lru_lambda'], l, tm, f"lru_bwd_{l}")
        small['lru_conv_w'][l], small['lru_conv_b'][l] = dcw, dcb
        small['lru_w_a'][l], small['lru_w_x'][l] = _diag_blocks(dwa, LRU_HEADS), _diag_blocks(dwx, LRU_HEADS)
        small['lru_b_a'][l], small['lru_b_x'][l], small['lru_lambda'][l] = dba, dbx, dlam

        dh, u, qn, kvn, dq_pre, d_cq, d_ckv, d_kr, dgpre, dgq, dgkv = _mix_in_bwd(
            dh, h1, cq, ckv, d_xr, d_gr, dq, dk, dv, g3['mix_pre_g'], wl['parts'], g3['mla_q_norm_g'], g3['mla_kv_norm_g'],
            wl['q'], wl['k'], wl['v'], cos_t, sin_t, rot_t, l, tmb, f"mix_in_bwd_{l}")
        small['mix_pre_g'][l], small['mla_q_norm_g'][l], small['mla_kv_norm_g'][l] = dgpre, dgq, dgkv
        d_win = jnp.concatenate([_mm_tn(d_xr, u, tm, F32, f"dw_in_xr_{l}"), _mm_tn(d_gr, u, tm, F32, f"dw_in_gr_{l}"),
                                 _mm_tn(d_cq, u, tm, F32, f"dw_in_cq_{l}"), _mm_tn(d_ckv, u, tm, F32, f"dw_in_ckv_{l}"),
                                 _mm_tn(d_kr, u, tm, F32, f"dw_in_kr_{l}")[QK_NOPE:]], axis=0)
        d_win = d_win.reshape(N_CHIPS, -1, d).astype(BF16)
        d_wuq = _mm_tn_heads(dq_pre, qn, tm, f"dw_uq_{l}").reshape(N_CHIPS, -1, ql).astype(BF16)
        d_wk = _mm_tn_heads(dk, kvn, tm, f"dw_uk_{l}")[:, :QK_NOPE]
        d_wv = _mm_tn_heads(dv, kvn, tm, f"dw_uv_{l}")
        d_wukv = jnp.transpose(jnp.concatenate([d_wk, d_wv], axis=1).reshape(N_CHIPS, -1, kvl), (0, 2, 1)).astype(BF16)

        dh, dgate, dup, act, df, u, dgpre, dgpost = _ffn_bwd(dh, h0, f1, gate1, up1, g3['ffn1_pre_g'], g3['ffn1_post_g'],
                                                            wl['ffn'], l, 0, 1, 2, tmb, f"ffn1_bwd_{l}")
        small['ffn1_pre_g'][l], small['ffn1_post_g'][l] = dgpre, dgpost
        d_wg1 = _mm_tn(dgate, u, tm, BF16, f"dw_gate1_{l}")
        d_wu1 = _mm_tn(dup, u, tm, BF16, f"dw_up1_{l}")
        dn1 = _mm_tn(act, df, tm, BF16, f"dw_down1_{l}")
        late = [d_win, d_wuq, d_wukv, d_wg1, d_wu1, dn1]

    grad_x = dh[N_META:t_real][None]
    keep_planes([(n, 0) for n in late_names], _run_job(_ReduceJob(late), "reduce_last"))

    res = {}
    stored_transposed = ('ffn1_w_gate', 'ffn1_w_up', 'ffn2_w_gate', 'ffn2_w_up', 'w_in', 'mla_w_uq')
    for name in early_names + late_names:
        view = tr if name in stored_transposed else (lambda t: t)
        w_v, m_v, v_v = view(a[name]), view(a['m_' + name]), view(a['v_' + name])
        prev = None
        for l in reversed(range(nl)):
            same_plane, other_plane = planes[(name, l)]
            prev = _adamw_layer(w_v, m_v, v_v, same_plane, other_plane, l, prev, f"adamw_{name}_{l}")
        res[name] = [view(t) for t in prev]

    small_full = {n: jnp.stack(small[n]).reshape(a[n].shape if n != 'lru_conv_w' else conv_full.shape)
                  for n in _REPLICATED + ['lru_conv_w']}
    order = _REPLICATED + ['lru_conv_w']
    packed = _pack_rows([small_full[n] for n in order] + [dh[:N_META]])
    summed = _allreduce_small(packed, "allreduce_small")
    pieces = _unpack_rows(summed, [small_full[n].shape for n in order] + [(N_META, d)])
    g_small = dict(zip(order, pieces[:-1]))
    g_conv_loc = lax.dynamic_slice_in_dim(g_small.pop('lru_conv_w'), chip * lru_conv_w.shape[-1], lru_conv_w.shape[-1], axis=2)
    g_meta_loc = lax.dynamic_slice_in_dim(pieces[-1], chip * meta_tokens.shape[-1], meta_tokens.shape[-1], axis=1)

    shapes = [a[n].shape for n in _REPLICATED]
    dl, mo, vo = _adamw_whole(_pack_rows([a[n] for n in _REPLICATED]), _pack_rows([g_small[n] for n in _REPLICATED]),
                              _pack_rows([a['m_' + n] for n in _REPLICATED]), _pack_rows([a['v_' + n] for n in _REPLICATED]),
                              "adamw_replicated")
    for n, dd, mm, vv in zip(_REPLICATED, _unpack_rows(dl, shapes), _unpack_rows(mo, shapes), _unpack_rows(vo, shapes)):
        res[n] = (g_small[n], dd, mm, vv)
    cshape = lru_conv_w.shape
    c2 = (cshape[0] * cshape[1], cshape[2])
    dd, mm, vv = _adamw_whole(lru_conv_w.reshape(c2), g_conv_loc.reshape(c2), m_lru_conv_w.reshape(c2), v_lru_conv_w.reshape(c2),
                              "adamw_conv_w")
    res['lru_conv_w'] = (g_conv_loc, dd.reshape(cshape), mm.reshape(cshape), vv.reshape(cshape))
    res['meta_tokens'] = (g_meta_loc,) + tuple(_adamw_whole(meta_tokens, g_meta_loc, m_meta_tokens, v_meta_tokens, "adamw_meta"))

    return (loss, grad_x, *[res[n][0] for n in _W_NAMES], *[res[n][1] for n in _W_NAMES],
            *[res[n][2] for n in _W_NAMES], *[res[n][3] for n in _W_NAMES])
```

```python
import functools
import math

import jax
import jax.numpy as jnp
import numpy as np
from jax import lax
from jax.experimental import pallas as pl
from jax.experimental.pallas import tpu as pltpu

F32 = jnp.float32
BF16 = jnp.bfloat16
MESH = pl.DeviceIdType.MESH

EPS = 1e-6
N_META = 16
LRU_HEADS = 8
MLA_HEADS = 8
QK_NOPE = 64
QK_ROPE = 32
QK_DIM = QK_NOPE + QK_ROPE
LRU_C = 8.0
ROPE_THETA = 10000.0
CONV_WIDTH = 4
N_CHIPS = 4

ADAM_LR = 0.001
ADAM_B1 = 0.9
ADAM_B2 = 0.999
ADAM_EPS = 1e-08
ADAM_WD = 0.01
ADAM_STEP = 10

V7X_VMEM_LIMIT_BYTES = 56 * 1024 * 1024
NEG_BIG = -1e30
Q_PRESCALE = QK_DIM ** -0.5 * math.log2(math.e)


def _cparams(sem=None):
    return pltpu.CompilerParams(dimension_semantics=sem, vmem_limit_bytes=V7X_VMEM_LIMIT_BYTES)


def _round_up(a, b):
    return -(-a // b) * b


def _whole(*shape):
    return pl.BlockSpec(shape, lambda *_: (0,) * len(shape))


_ANY = pl.BlockSpec(memory_space=pl.ANY)


def _dot(a, b):
    return jnp.dot(a.astype(BF16), b.astype(BF16), preferred_element_type=F32)


def _dot_nt(a, b):
    return lax.dot_general(a.astype(BF16), b.astype(BF16), (((1,), (1,)), ((), ())), preferred_element_type=F32)


def _dot_tn(a, b):
    return lax.dot_general(a.astype(BF16), b.astype(BF16), (((0,), (0,)), ((), ())), preferred_element_type=F32)


def _dot_split(x, p):
    hi = x.astype(BF16)
    lo = (x - hi.astype(F32)).astype(BF16)
    return jnp.dot(hi, p, preferred_element_type=F32) + jnp.dot(lo, p, preferred_element_type=F32)


def _rms(x, g):
    r = lax.rsqrt(jnp.mean(x * x, axis=-1, keepdims=True) + EPS)
    return x * r * g, r


def _rms_bwd(x, g, dy):
    r = lax.rsqrt(jnp.mean(x * x, axis=-1, keepdims=True) + EPS)
    xh = x * r
    dyg = dy * g
    dx = r * (dyg - xh * jnp.mean(dyg * xh, axis=-1, keepdims=True))
    dg = jnp.sum(dy * xh, axis=0, keepdims=True)
    return dx, dg


def _sigmoid(x):
    return 1.0 / (1.0 + jnp.exp(-x))


def _neg_expm1(x):
    series = -x * (1.0 + x * (0.5 + x * (1.0 / 6.0 + x * (1.0 / 24.0))))
    return jnp.where(jnp.abs(x) < 0.03, series, 1.0 - jnp.exp(x))


_GELU_K = math.sqrt(2.0 / math.pi)


def _gelu(x):
    t = jnp.tanh(_GELU_K * (x + 0.044715 * x * x * x))
    return 0.5 * x * (1.0 + t), t


def _gelu_grad(x, t):
    return 0.5 * (1.0 + t) + 0.5 * x * (1.0 - t * t) * _GELU_K * (1.0 + 3.0 * 0.044715 * x * x)


def _rope(x, cos_t, sin_t, rot):
    return x * cos_t + _dot_split(x, rot) * sin_t


def _ffn_fwd(h, g_pre, g_post, w_ffn, layer, kind_gate, kind_up, kind_dn, tm, name):
    tp, d = h.shape
    n_slot, fs = w_ffn.shape[0], w_ffn.shape[-2]

    def body(h_ref, gpre_ref, gpost_ref, wg_ref, wu_ref, wd_ref, hout_ref, gate_ref, up_ref, f_ref, u_sc, acc_sc):
        j = pl.program_id(1)

        @pl.when(j == 0)
        def _():
            u, _ = _rms(h_ref[...], gpre_ref[...])
            u_sc[...] = u.astype(BF16)
            acc_sc[...] = jnp.zeros_like(acc_sc)

        u = u_sc[...]
        gate = _dot_nt(u, wg_ref[...])
        up = _dot_nt(u, wu_ref[...])
        act = gate * _sigmoid(gate) * up
        gate_ref[...] = gate.astype(BF16)
        up_ref[...] = up.astype(BF16)
        acc_sc[...] += jnp.dot(act.astype(BF16), wd_ref[...], preferred_element_type=F32)

        @pl.when(j == n_slot - 1)
        def _():
            f = acc_sc[...]
            f_ref[...] = f
            n, _ = _rms(f, gpost_ref[...])
            hout_ref[...] = h_ref[...] + 0.5 * n

    tok = pl.BlockSpec((tm, d), lambda i, j: (i, 0))
    gain = pl.BlockSpec((None, 1, d), lambda i, j: (layer, 0, 0))
    slot_act = pl.BlockSpec((None, tm, fs), lambda i, j: (j, i, 0))
    return pl.pallas_call(
        body, name=name, grid=(tp // tm, n_slot),
        in_specs=[tok, gain, gain] + [pl.BlockSpec((None, None, fs, d), lambda i, j, kind=kind: (j, kind, 0, 0))
                                      for kind in (kind_gate, kind_up, kind_dn)],
        out_specs=[tok, slot_act, slot_act, tok],
        out_shape=[jax.ShapeDtypeStruct((tp, d), F32), jax.ShapeDtypeStruct((n_slot, tp, fs), BF16),
                   jax.ShapeDtypeStruct((n_slot, tp, fs), BF16), jax.ShapeDtypeStruct((tp, d), F32)],
        scratch_shapes=[pltpu.VMEM((tm, d), BF16), pltpu.VMEM((tm, d), F32)],
        compiler_params=_cparams(("parallel", "arbitrary")),
    )(h, g_pre, g_post, w_ffn, w_ffn, w_ffn)


def _ffn_bwd(dh_out, h, f, gate, up, g_pre, g_post, w_ffn, layer, kind_gate, kind_up, kind_dn, tm, name):
    tp, d = h.shape
    n_slot, fs = w_ffn.shape[0], w_ffn.shape[-2]

    def body(dho_ref, h_ref, f_ref, gate_ref, up_ref, gpre_ref, gpost_ref, wg_ref, wu_ref, wd_ref,
             dh_ref, dgate_ref, dup_ref, act_ref, df_ref, u_ref, dgpre_ref, dgpost_ref, du_sc):
        i, j = pl.program_id(0), pl.program_id(1)

        @pl.when((i == 0) & (j == 0))
        def _():
            dgpre_ref[...] = jnp.zeros_like(dgpre_ref)
            dgpost_ref[...] = jnp.zeros_like(dgpost_ref)

        @pl.when(j == 0)
        def _():
            df, dg = _rms_bwd(f_ref[...], gpost_ref[...], 0.5 * dho_ref[...])
            df_ref[...] = df.astype(BF16)
            dgpost_ref[...] += dg
            u, _ = _rms(h_ref[...], gpre_ref[...])
            u_ref[...] = u.astype(BF16)
            du_sc[...] = jnp.zeros_like(du_sc)

        g = gate_ref[...].astype(F32)
        u_ = up_ref[...].astype(F32)
        sg = _sigmoid(g)
        silu = g * sg
        dact = _dot_nt(df_ref[...], wd_ref[...])
        dup = dact * silu
        dgate = dact * u_ * (sg * (1.0 + g * (1.0 - sg)))
        act_ref[...] = (silu * u_).astype(BF16)
        dup_b = dup.astype(BF16)
        dgate_b = dgate.astype(BF16)
        dup_ref[...] = dup_b
        dgate_ref[...] = dgate_b
        du_sc[...] += _dot(dgate_b, wg_ref[...]) + _dot(dup_b, wu_ref[...])

        @pl.when(j == n_slot - 1)
        def _():
            dx, dg = _rms_bwd(h_ref[...], gpre_ref[...], du_sc[...])
            dh_ref[...] = dho_ref[...] + dx
            dgpre_ref[...] += dg

    tok = pl.BlockSpec((tm, d), lambda i, j: (i, 0))
    gain = pl.BlockSpec((None, 1, d), lambda i, j: (layer, 0, 0))
    acc = pl.BlockSpec((1, d), lambda i, j: (0, 0))
    slot_act = pl.BlockSpec((None, tm, fs), lambda i, j: (j, i, 0))
    act_shape = jax.ShapeDtypeStruct((n_slot, tp, fs), BF16)
    return pl.pallas_call(
        body, name=name, grid=(tp // tm, n_slot),
        in_specs=[tok, tok, tok, slot_act, slot_act, gain, gain]
        + [pl.BlockSpec((None, None, fs, d), lambda i, j, kind=kind: (j, kind, 0, 0)) for kind in (kind_gate, kind_up, kind_dn)],
        out_specs=[tok, slot_act, slot_act, slot_act, tok, tok, acc, acc],
        out_shape=[jax.ShapeDtypeStruct((tp, d), F32), act_shape, act_shape, act_shape,
                   jax.ShapeDtypeStruct((tp, d), BF16), jax.ShapeDtypeStruct((tp, d), BF16),
                   jax.ShapeDtypeStruct((1, d), F32), jax.ShapeDtypeStruct((1, d), F32)],
        scratch_shapes=[pltpu.VMEM((tm, d), F32)],
        compiler_params=_cparams(("arbitrary", "arbitrary")),
    )(dh_out, h, f, gate, up, g_pre, g_post, w_ffn, w_ffn, w_ffn)


def _mm_tn(a, b, tk, out_dtype, name):
    ga = a.shape[0] if a.ndim == 3 else None
    gb = b.shape[0] if b.ndim == 3 else None
    groups = ga or gb or 1
    t, m = a.shape[-2:]
    n = b.shape[-1]
    nk = t // tk

    def body(a_ref, b_ref, o_ref, acc_sc):
        k = pl.program_id(1)

        @pl.when(k == 0)
        def _():
            acc_sc[...] = jnp.zeros_like(acc_sc)

        acc_sc[...] += _dot_tn(a_ref[...], b_ref[...])

        @pl.when(k == nk - 1)
        def _():
            o_ref[...] = acc_sc[...].astype(out_dtype)

    a_spec = (pl.BlockSpec((None, tk, m), lambda g, k: (g, k, 0)) if ga else pl.BlockSpec((tk, m), lambda g, k: (k, 0)))
    b_spec = (pl.BlockSpec((None, tk, n), lambda g, k: (g, k, 0)) if gb else pl.BlockSpec((tk, n), lambda g, k: (k, 0)))
    out = pl.pallas_call(
        body, name=name, grid=(groups, nk),
        in_specs=[a_spec, b_spec],
        out_specs=pl.BlockSpec((None, m, n), lambda g, k: (g, 0, 0)),
        out_shape=jax.ShapeDtypeStruct((groups, m, n), out_dtype),
        scratch_shapes=[pltpu.VMEM((m, n), F32)],
        compiler_params=_cparams(("parallel", "arbitrary")),
    )(a, b)
    return out if (ga or gb) else out[0]


def _mm_tn_heads(a, b, tk, name):
    groups, t, m = a.shape
    n = b.shape[-1]
    nk = t // tk

    def body(a_ref, b_ref, o_ref):
        k = pl.program_id(0)

        @pl.when(k == 0)
        def _():
            o_ref[...] = jnp.zeros_like(o_ref)

        bb = b_ref[...]
        for g in range(groups):
            o_ref[g] += _dot_tn(a_ref[g], bb)

    return pl.pallas_call(
        body, name=name, grid=(nk,),
        in_specs=[pl.BlockSpec((groups, tk, m), lambda k: (0, k, 0)), pl.BlockSpec((tk, n), lambda k: (k, 0))],
        out_specs=pl.BlockSpec((groups, m, n), lambda k: (0, 0, 0)),
        out_shape=jax.ShapeDtypeStruct((groups, m, n), F32),
        compiler_params=_cparams(("arbitrary",)),
    )(a, b)


def _mix_in_fwd(h, g_pre, w_parts, g_q, g_kv, w_q, w_k, w_v, cos_t, sin_t, rot, layer, tm, name):
    tp, d = h.shape
    w_xr, w_gr, w_cq, w_ckv, w_kr = w_parts
    lw, ql, kvl = w_xr.shape[0], w_cq.shape[0], w_ckv.shape[0]
    nh, vd = w_v.shape[0], w_v.shape[-1]

    def body(h_ref, gpre_ref, wxr_ref, wgr_ref, wcq_ref, wckv_ref, wkr_ref, gq_ref, gkv_ref, wq_ref, wk_ref, wv_ref,
             cos_ref, sin_ref, rot_ref, xr_ref, gr_ref, cq_ref, ckv_ref, q_ref, k_ref, v_ref):
        u, _ = _rms(h_ref[...], gpre_ref[...])
        u = u.astype(BF16)
        xr_ref[...] = _dot_nt(u, wxr_ref[...])
        gr_ref[...] = _dot_nt(u, wgr_ref[...])
        cq = _dot_nt(u, wcq_ref[...])
        ckv = _dot_nt(u, wckv_ref[...])
        kr = _dot_nt(u, wkr_ref[...])
        cq_ref[...] = cq
        ckv_ref[...] = ckv
        cos_b, sin_b, rot_b = cos_ref[...], sin_ref[...], rot_ref[...]
        qn = _rms(cq, gq_ref[...])[0].astype(BF16)
        kvn = _rms(ckv, gkv_ref[...])[0].astype(BF16)
        k_rope = _rope(kr, cos_b, sin_b, rot_b)
        for hd in range(nh):
            q_pre = _dot_nt(qn, wq_ref[hd])
            q_ref[hd] = (_rope(q_pre, cos_b, sin_b, rot_b) * Q_PRESCALE).astype(BF16)
            k_ref[hd] = (jnp.dot(kvn, wk_ref[hd], preferred_element_type=F32) + k_rope).astype(BF16)
            v_ref[hd] = jnp.dot(kvn, wv_ref[hd], preferred_element_type=F32).astype(BF16)

    def tok(n):
        return pl.BlockSpec((tm, n), lambda i: (i, 0))

    def lay(*shape):
        return pl.BlockSpec((None,) + shape, lambda i: (layer,) + (0,) * len(shape))

    def heads(n):
        return pl.BlockSpec((nh, tm, n), lambda i: (0, i, 0))

    return pl.pallas_call(
        body, name=name, grid=(tp // tm,),
        in_specs=[tok(d), lay(1, d), _whole(lw, d), _whole(lw, d), _whole(ql, d), _whole(kvl, d), _whole(QK_DIM, d),
                  lay(1, ql), lay(1, kvl),
                  _whole(nh, QK_DIM, ql), _whole(nh, kvl, QK_DIM), _whole(nh, kvl, vd), tok(QK_DIM), tok(QK_DIM),
                  pl.BlockSpec((QK_DIM, QK_DIM), lambda i: (0, 0))],
        out_specs=[tok(lw), tok(lw), tok(ql), tok(kvl), heads(QK_DIM), heads(QK_DIM), heads(vd)],
        out_shape=[jax.ShapeDtypeStruct((tp, lw), F32), jax.ShapeDtypeStruct((tp, lw), F32),
                   jax.ShapeDtypeStruct((tp, ql), F32), jax.ShapeDtypeStruct((tp, kvl), F32),
                   jax.ShapeDtypeStruct((nh, tp, QK_DIM), BF16), jax.ShapeDtypeStruct((nh, tp, QK_DIM), BF16),
                   jax.ShapeDtypeStruct((nh, tp, vd), BF16)],
        compiler_params=_cparams(("parallel",)),
    )(h, g_pre, w_xr, w_gr, w_cq, w_ckv, w_kr, g_q, g_kv, w_q, w_k, w_v, cos_t, sin_t, rot)


def _mix_in_bwd(dh_res, h, cq, ckv, d_xr, d_gr, dq, dk, dv, g_pre, w_parts, g_q, g_kv, w_q, w_k, w_v,
                cos_t, sin_t, rot_t, layer, tm, name):
    tp, d = h.shape
    w_xr, w_gr, w_cq, w_ckv, w_kr = w_parts
    lw, ql, kvl = w_xr.shape[0], w_cq.shape[0], w_ckv.shape[0]
    nh, vd = w_v.shape[0], w_v.shape[-1]

    def body(dhr_ref, h_ref, cq_ref, ckv_ref, dxr_ref, dgr_ref, dq_ref, dk_ref, dv_ref, gpre_ref,
             wxr_ref, wgr_ref, wcq_ref, wckv_ref, wkr_ref, gq_ref, gkv_ref, wq_ref, wk_ref, wv_ref,
             cos_ref, sin_ref, rott_ref,
             dh_ref, u_ref, qn_ref, kvn_ref, dqpre_ref, dcq_ref, dckv_ref, dkr_ref, dgpre_ref, dgq_ref, dgkv_ref):
        i = pl.program_id(0)

        @pl.when(i == 0)
        def _():
            dgpre_ref[...] = jnp.zeros_like(dgpre_ref)
            dgq_ref[...] = jnp.zeros_like(dgq_ref)
            dgkv_ref[...] = jnp.zeros_like(dgkv_ref)

        cos_b, sin_b, rott_b = cos_ref[...], sin_ref[...], rott_ref[...]
        hh = h_ref[...]
        u, _ = _rms(hh, gpre_ref[...])
        u_ref[...] = u.astype(BF16)
        cq, ckv = cq_ref[...], ckv_ref[...]
        qn = _rms(cq, gq_ref[...])[0]
        kvn = _rms(ckv, gkv_ref[...])[0]
        qn_ref[...] = qn.astype(BF16)
        kvn_ref[...] = kvn.astype(BF16)
        d_qn = jnp.zeros((tm, ql), F32)
        d_kvn = jnp.zeros((tm, kvl), F32)
        d_krope = jnp.zeros((tm, QK_DIM), F32)
        for hd in range(nh):
            dq_h = dq_ref[hd]
            dq_pre = dq_h * cos_b + _dot_split(dq_h * sin_b, rott_b)
            dq_pre_b = dq_pre.astype(BF16)
            dqpre_ref[hd] = dq_pre_b
            d_qn += _dot(dq_pre_b, wq_ref[hd])
            dk_h = dk_ref[hd]
            d_krope += dk_h
            d_kvn += _dot_nt(dk_h, wk_ref[hd]) + _dot_nt(dv_ref[hd], wv_ref[hd])
        d_kr = d_krope * cos_b + _dot_split(d_krope * sin_b, rott_b)
        d_cq, dgq = _rms_bwd(cq, gq_ref[...], d_qn)
        d_ckv, dgkv = _rms_bwd(ckv, gkv_ref[...], d_kvn)
        dgq_ref[...] += dgq
        dgkv_ref[...] += dgkv
        d_cq_b, d_ckv_b, d_kr_b = d_cq.astype(BF16), d_ckv.astype(BF16), d_kr.astype(BF16)
        dcq_ref[...] = d_cq_b
        dckv_ref[...] = d_ckv_b
        dkr_ref[...] = d_kr_b
        du = (_dot(dxr_ref[...], wxr_ref[...]) + _dot(dgr_ref[...], wgr_ref[...]) + _dot(d_cq_b, wcq_ref[...])
              + _dot(d_ckv_b, wckv_ref[...]) + _dot(d_kr_b, wkr_ref[...]))
        dx, dg = _rms_bwd(hh, gpre_ref[...], du)
        dh_ref[...] = dhr_ref[...] + dx
        dgpre_ref[...] += dg

    def tok(n):
        return pl.BlockSpec((tm, n), lambda i: (i, 0))

    def lay(*shape):
        return pl.BlockSpec((None,) + shape, lambda i: (layer,) + (0,) * len(shape))

    def heads(n):
        return pl.BlockSpec((nh, tm, n), lambda i: (0, i, 0))

    def acc(n):
        return pl.BlockSpec((1, n), lambda i: (0, 0))

    return pl.pallas_call(
        body, name=name, grid=(tp // tm,),
        in_specs=[tok(d), tok(d), tok(ql), tok(kvl), tok(lw), tok(lw), heads(QK_DIM), heads(QK_DIM), heads(vd), lay(1, d),
                  _whole(lw, d), _whole(lw, d), _whole(ql, d), _whole(kvl, d), _whole(QK_DIM, d), lay(1, ql), lay(1, kvl),
                  _whole(nh, QK_DIM, ql), _whole(nh, kvl, QK_DIM), _whole(nh, kvl, vd), tok(QK_DIM), tok(QK_DIM),
                  pl.BlockSpec((QK_DIM, QK_DIM), lambda i: (0, 0))],
        out_specs=[tok(d), tok(d), tok(ql), tok(kvl), heads(QK_DIM), tok(ql), tok(kvl), tok(QK_DIM), acc(d), acc(ql), acc(kvl)],
        out_shape=[jax.ShapeDtypeStruct((tp, d), F32), jax.ShapeDtypeStruct((tp, d), BF16),
                   jax.ShapeDtypeStruct((tp, ql), BF16), jax.ShapeDtypeStruct((tp, kvl), BF16),
                   jax.ShapeDtypeStruct((nh, tp, QK_DIM), BF16), jax.ShapeDtypeStruct((tp, ql), BF16),
                   jax.ShapeDtypeStruct((tp, kvl), BF16), jax.ShapeDtypeStruct((tp, QK_DIM), BF16),
                   jax.ShapeDtypeStruct((1, d), F32), jax.ShapeDtypeStruct((1, ql), F32), jax.ShapeDtypeStruct((1, kvl), F32)],
        compiler_params=_cparams(("arbitrary",)),
    )(dh_res, h, cq, ckv, d_xr, d_gr, dq, dk, dv, g_pre, w_xr, w_gr, w_cq, w_ckv, w_kr, g_q, g_kv, w_q, w_k, w_v,
      cos_t, sin_t, rot_t)


def _lru_gates(xc, wa_ref, wx_ref, ba, bx, sp):
    xcb = xc.astype(BF16)
    r = _sigmoid(jnp.dot(xcb, wa_ref[...], preferred_element_type=F32) + ba)
    ig = _sigmoid(jnp.dot(xcb, wx_ref[...], preferred_element_type=F32) + bx)
    log_a = -LRU_C * r * sp
    a = jnp.exp(log_a)
    sq = jnp.sqrt(_neg_expm1(2.0 * log_a))
    return r, ig, a, sq


def _softplus(x):
    return jnp.maximum(x, 0.0) + jnp.log(1.0 + jnp.exp(-jnp.abs(x)))


def _lru_fwd(xr, gr, conv_w, conv_b, wa_bd, wx_bd, b_a, b_x, lam, layer, tc, name):
    tp, w = xr.shape
    pad = 8

    def body(xr_ref, gr_ref, cw_ref, cb_ref, wa_ref, wx_ref, ba_ref, bx_ref, lam_ref, y_ref, xc_ref, hs_ref,
             xe_sc, a_sc, b_sc, st_sc):
        c = pl.program_id(0)

        @pl.when(c == 0)
        def _():
            xe_sc[pl.ds(0, pad), :] = jnp.zeros((pad, w), F32)
            st_sc[...] = jnp.zeros_like(st_sc)

        xe_sc[pl.ds(pad, tc), :] = xr_ref[...]
        xc = cb_ref[...] + xe_sc[pl.ds(pad, tc), :] * cw_ref[pl.ds(CONV_WIDTH - 1, 1), :]
        for k in range(CONV_WIDTH - 1):
            xc = xc + xe_sc[pl.ds(pad - (CONV_WIDTH - 1) + k, tc), :] * cw_ref[pl.ds(k, 1), :]
        xe_sc[pl.ds(0, pad), :] = xe_sc[pl.ds(tc, pad), :]
        xc_ref[...] = xc
        sp = _softplus(-lam_ref[...])
        _, ig, a, sq = _lru_gates(xc, wa_ref, wx_ref, ba_ref[...], bx_ref[...], sp)
        a_sc[...] = a
        b_sc[...] = sq * (ig * xc)

        def step(t, hcur):
            hnew = a_sc[pl.ds(t, 1), :] * hcur + b_sc[pl.ds(t, 1), :]
            hs_ref[pl.ds(t, 1), :] = hnew
            return hnew

        st_sc[...] = lax.fori_loop(0, tc, step, st_sc[...], unroll=8)
        y_ref[...] = hs_ref[...] * _gelu(gr_ref[...])[0]

    tok = pl.BlockSpec((tc, w), lambda c: (c, 0))

    def lay(*shape):
        return pl.BlockSpec((None,) + shape, lambda c: (layer,) + (0,) * len(shape))

    out = jax.ShapeDtypeStruct((tp, w), F32)
    return pl.pallas_call(
        body, name=name, grid=(tp // tc,),
        in_specs=[tok, tok, lay(CONV_WIDTH, w), lay(1, w), lay(w, w), lay(w, w), lay(1, w), lay(1, w), lay(1, w)],
        out_specs=[tok, tok, tok], out_shape=[out, out, out],
        scratch_shapes=[pltpu.VMEM((tc + pad, w), F32), pltpu.VMEM((tc, w), F32), pltpu.VMEM((tc, w), F32),
                        pltpu.VMEM((1, w), F32)],
        compiler_params=_cparams(("arbitrary",)),
    )(xr, gr, conv_w, conv_b, wa_bd, wx_bd, b_a, b_x, lam)


def _lru_bwd(dy, xr, gr, xc, hs, conv_w, wa_bd, wx_bd, b_a, b_x, lam, layer, tc, name):
    tp, w = xr.shape
    pad = 8
    nc = tp // tc
    per = tc // pad

    def body(dy_ref, xr_ref, gr_ref, xc_ref, hs_ref, xrp_ref, hsp_ref, cw_ref, wa_ref, wx_ref, ba_ref, bx_ref, lam_ref,
             dxr_ref, dgr_ref, dcw_ref, dcb_ref, dwa_ref, dwx_ref, dba_ref, dbx_ref, dlam_ref,
             a_sc, dh_sc, dxc_sc, he_sc, xe_sc, carry_sc):
        s = pl.program_id(0)
        first_chunk = s == nc - 1

        @pl.when(s == 0)
        def _():
            for ref in (dcw_ref, dcb_ref, dwa_ref, dwx_ref, dba_ref, dbx_ref, dlam_ref):
                ref[...] = jnp.zeros_like(ref)
            carry_sc[...] = jnp.zeros_like(carry_sc)
            dxc_sc[pl.ds(tc, pad), :] = jnp.zeros((pad, w), F32)

        keep = jnp.where(first_chunk, 0.0, 1.0)
        he_sc[pl.ds(0, pad), :] = hsp_ref[...] * keep
        he_sc[pl.ds(pad, tc), :] = hs_ref[...]
        xe_sc[pl.ds(0, pad), :] = xrp_ref[...] * keep
        xe_sc[pl.ds(pad, tc), :] = xr_ref[...]

        lam_v = lam_ref[...]
        sp = _softplus(-lam_v)
        xc = xc_ref[...]
        r, ig, a, sq = _lru_gates(xc, wa_ref, wx_ref, ba_ref[...], bx_ref[...], sp)
        a_sc[...] = a
        grv = gr_ref[...]
        gl, th = _gelu(grv)
        dyv = dy_ref[...]
        dgr_ref[...] = dyv * hs_ref[...] * _gelu_grad(grv, th)
        dh_sc[...] = dyv * gl

        def step(n, g):
            t = tc - 1 - n
            dh = dh_sc[pl.ds(t, 1), :] + g
            dh_sc[pl.ds(t, 1), :] = dh
            return a_sc[pl.ds(t, 1), :] * dh

        carry_sc[...] = lax.fori_loop(0, tc, step, carry_sc[...], unroll=8)

        dh = dh_sc[...]
        d_a = dh * he_sc[pl.ds(pad - 1, tc), :]
        d_ixc = dh * sq
        d_sq = dh * (ig * xc)
        a2 = a * a
        d_la = d_a * a - d_sq * a2 / sq
        d_r = d_la * (-LRU_C * sp)
        d_sp = jnp.sum(d_la * (-LRU_C * r), axis=0, keepdims=True)
        dlam_ref[...] += d_sp * (-_sigmoid(-lam_v))
        d_pa = d_r * r * (1.0 - r)
        d_px = d_ixc * xc * ig * (1.0 - ig)
        dba_ref[...] += jnp.sum(d_pa, axis=0, keepdims=True)
        dbx_ref[...] += jnp.sum(d_px, axis=0, keepdims=True)
        d_pa_b, d_px_b = d_pa.astype(BF16), d_px.astype(BF16)
        xcb = xc.astype(BF16)
        dwa_ref[...] += _dot_tn(xcb, d_pa_b)
        dwx_ref[...] += _dot_tn(xcb, d_px_b)
        d_xc = d_ixc * ig + _dot_nt(d_pa_b, wa_ref[...]) + _dot_nt(d_px_b, wx_ref[...])
        dcb_ref[...] += jnp.sum(d_xc, axis=0, keepdims=True)
        dxc_sc[pl.ds(0, tc), :] = d_xc
        d_xr = jnp.zeros((tc, w), F32)
        for k in range(CONV_WIDTH):
            off = CONV_WIDTH - 1 - k
            d_xr = d_xr + dxc_sc[pl.ds(off, tc), :] * cw_ref[pl.ds(k, 1), :]
            dcw_ref[pl.ds(k, 1), :] += jnp.sum(d_xc * xe_sc[pl.ds(pad - off, tc), :], axis=0, keepdims=True)
        dxr_ref[...] = d_xr
        dxc_sc[pl.ds(tc, pad), :] = dxc_sc[pl.ds(0, pad), :]

    def rev(cidx):
        return nc - 1 - cidx

    tok = pl.BlockSpec((tc, w), lambda c: (rev(c), 0))
    prev = pl.BlockSpec((pad, w), lambda c: (jnp.maximum(rev(c) * per - 1, 0), 0))

    def lay(*shape):
        return pl.BlockSpec((None,) + shape, lambda c: (layer,) + (0,) * len(shape))

    def acc(*shape):
        return pl.BlockSpec(shape, lambda c: (0,) * len(shape))

    big = jax.ShapeDtypeStruct((tp, w), F32)
    vec = jax.ShapeDtypeStruct((1, w), F32)
    return pl.pallas_call(
        body, name=name, grid=(nc,),
        in_specs=[tok, tok, tok, tok, tok, prev, prev, lay(CONV_WIDTH, w), lay(w, w), lay(w, w), lay(1, w), lay(1, w), lay(1, w)],
        out_specs=[tok, tok, acc(CONV_WIDTH, w), acc(1, w), acc(w, w), acc(w, w), acc(1, w), acc(1, w), acc(1, w)],
        out_shape=[big, big, jax.ShapeDtypeStruct((CONV_WIDTH, w), F32), vec, jax.ShapeDtypeStruct((w, w), F32),
                   jax.ShapeDtypeStruct((w, w), F32), vec, vec, vec],
        scratch_shapes=[pltpu.VMEM((tc, w), F32), pltpu.VMEM((tc, w), F32), pltpu.VMEM((tc + pad, w), F32),
                        pltpu.VMEM((tc + pad, w), F32), pltpu.VMEM((tc + pad, w), F32), pltpu.VMEM((1, w), F32)],
        compiler_params=_cparams(("arbitrary",)),
    )(dy, xr, gr, xc, hs, xr, hs, conv_w, wa_bd, wx_bd, b_a, b_x, lam)


def _attn_fwd(q, k, v, tq, name, job=None):
    nh, tp, dqk = q.shape
    vd = v.shape[-1]
    nq = tp // tq
    n_ji = len(job.inputs) if job else 0
    n_jo = len(job.out_shape) if job else 0

    def body(q_ref, k_ref, v_ref, *rest):
        job_in, (o_ref, lse_ref), rest = rest[:n_ji], rest[n_ji:n_ji + 2], rest[n_ji + 2:]
        job_out, sems = rest[:n_jo], rest[n_jo:]
        i = pl.program_id(1)
        if job:
            first = (pl.program_id(0) == 0) & (i == 0)
            pl.when(first)(lambda: job.start(job_in, job_out, sems))
        qb = q_ref[...]

        def scores(j):
            return _dot_nt(qb, k_ref[pl.ds(pl.multiple_of(j * tq, tq), tq), :])

        def absorb(j, s, m, l, acc):
            m_new = jnp.maximum(m, jnp.max(s, axis=-1, keepdims=True))
            p = jnp.exp2(s - m_new)
            alpha = jnp.exp2(m - m_new)
            l = alpha * l + jnp.sum(p, axis=-1, keepdims=True)
            vb = v_ref[pl.ds(pl.multiple_of(j * tq, tq), tq), :]
            acc = alpha * acc + jnp.dot(p.astype(BF16), vb, preferred_element_type=F32)
            return m_new, l, acc

        def step(j, carry):
            s, m, l, acc = carry
            s_next = scores(j + 1)
            return (s_next,) + absorb(j, s, m, l, acc)

        init = (scores(0), jnp.full((tq, 1), NEG_BIG, F32), jnp.zeros((tq, 1), F32), jnp.zeros((tq, vd), F32))
        s, m, l, acc = lax.fori_loop(0, i, step, init)
        keep = lax.broadcasted_iota(jnp.int32, (tq, tq), 1) <= lax.broadcasted_iota(jnp.int32, (tq, tq), 0)
        m, l, acc = absorb(i, jnp.where(keep, s, NEG_BIG), m, l, acc)
        o_ref[...] = acc / l
        lse_ref[...] = m + jnp.log2(l)
        if job:
            last = (pl.program_id(0) == nh - 1) & (i == nq - 1)
            pl.when(last)(lambda: job.finish(job_in, job_out, sems))

    outs = pl.pallas_call(
        body, name=name, grid=(nh, nq),
        in_specs=[pl.BlockSpec((None, tq, dqk), lambda h, i: (h, i, 0)),
                  pl.BlockSpec((None, tp, dqk), lambda h, i: (h, 0, 0)),
                  pl.BlockSpec((None, tp, vd), lambda h, i: (h, 0, 0))] + [_ANY] * n_ji,
        out_specs=[pl.BlockSpec((None, tq, vd), lambda h, i: (h, i, 0)),
                   pl.BlockSpec((None, tq, 1), lambda h, i: (h, i, 0))] + [_ANY] * n_jo,
        out_shape=[jax.ShapeDtypeStruct((nh, tp, vd), F32), jax.ShapeDtypeStruct((nh, tp, 1), F32)]
        + (list(job.out_shape) if job else []),
        scratch_shapes=list(job.scratch) if job else [],
        compiler_params=_cparams(("arbitrary", "arbitrary") if job else ("parallel", "parallel")),
    )(q, k, v, *(job.inputs if job else ()))
    return outs[0], outs[1], list(outs[2:])


def _attn_bwd(q, k, v, o, do, lse, tq, name, job=None):
    nh, tp, dqk = q.shape
    vd = v.shape[-1]
    scale = dqk ** -0.5
    nq = tp // tq
    n_ji = len(job.inputs) if job else 0
    n_jo = len(job.out_shape) if job else 0

    def body(q_ref, k_ref, v_ref, o_ref, do_ref, lse_ref, *rest):
        job_in, (dq_ref, dk_ref, dv_ref), rest = rest[:n_ji], rest[n_ji:n_ji + 3], rest[n_ji + 3:]
        job_out, (lse_rows, delta_rows, do_bf, dqt_sc), sems = rest[:n_jo], rest[n_jo:n_jo + 4], rest[n_jo + 4:]
        kb_i = pl.program_id(1)
        if job:
            first = (pl.program_id(0) == 0) & (kb_i == 0)
            pl.when(first)(lambda: job.start(job_in, job_out, sems))

        def as_rows(col):
            return jnp.transpose(jnp.broadcast_to(col, (tq, _LANES)))[0:8, :]

        @pl.when(kb_i == 0)
        def _():
            dqt_sc[...] = jnp.zeros_like(dqt_sc)

            def prep(qi, _):
                off = pl.multiple_of(qi * tq, tq)
                dob = do_ref[pl.ds(off, tq), :]
                do_bf[pl.ds(off, tq), :] = dob.astype(BF16)
                delta_rows[qi] = as_rows(jnp.sum(dob * o_ref[pl.ds(off, tq), :], axis=-1, keepdims=True))
                lse_rows[qi] = as_rows(lse_ref[pl.ds(off, tq), :])
                return 0

            lax.fori_loop(0, nq, prep, 0)

        kb = k_ref[...]
        vb = v_ref[...]

        def pair(qi, carry, diagonal):
            dk, dv = carry
            off = pl.multiple_of(qi * tq, tq)
            qb = q_ref[pl.ds(off, tq), :]
            dob = do_bf[pl.ds(off, tq), :]
            st = _dot_nt(kb, qb)
            if diagonal:
                keep = lax.broadcasted_iota(jnp.int32, (tq, tq), 0) <= lax.broadcasted_iota(jnp.int32, (tq, tq), 1)
                st = jnp.where(keep, st, NEG_BIG)
            pt = jnp.exp2(st - lse_rows[qi, 0:1, :])
            dpt = _dot_nt(vb, dob)
            dst = (pt * (dpt - delta_rows[qi, 0:1, :])).astype(BF16)
            dv = dv + jnp.dot(pt.astype(BF16), dob, preferred_element_type=F32)
            dk = dk + jnp.dot(dst, qb, preferred_element_type=F32)
            dqt_sc[qi] += _dot_tn(kb, dst)
            return dk, dv

        zero = (jnp.zeros((tq, dqk), F32), jnp.zeros((tq, vd), F32))
        dk, dv = lax.fori_loop(kb_i + 1, nq, lambda qi, c: pair(qi, c, False), pair(kb_i, zero, True))
        dk_ref[...] = dk * math.log(2.0)
        dv_ref[...] = dv

        @pl.when(kb_i == nq - 1)
        def _():
            eye = (lax.broadcasted_iota(jnp.int32, (dqk, dqk), 0) == lax.broadcasted_iota(jnp.int32, (dqk, dqk), 1))
            eye = jnp.where(eye, 1.0, 0.0).astype(BF16)
            for qi in range(nq):
                t = dqt_sc[qi]
                hi = t.astype(BF16)
                lo = (t - hi.astype(F32)).astype(BF16)
                dq_ref[pl.ds(qi * tq, tq), :] = scale * (_dot_tn(hi, eye) + _dot_tn(lo, eye))

        if job:
            last = (pl.program_id(0) == nh - 1) & (kb_i == nq - 1)
            pl.when(last)(lambda: job.finish(job_in, job_out, sems))

    def full(n):
        return pl.BlockSpec((None, tp, n), lambda h, j: (h, 0, 0))

    def blk(n):
        return pl.BlockSpec((None, tq, n), lambda h, j: (h, j, 0))

    outs = pl.pallas_call(
        body, name=name, grid=(nh, nq),
        in_specs=[full(dqk), blk(dqk), blk(vd), full(vd), full(vd), full(1)] + [_ANY] * n_ji,
        out_specs=[full(dqk), blk(dqk), blk(vd)] + [_ANY] * n_jo,
        out_shape=[jax.ShapeDtypeStruct((nh, tp, dqk), F32), jax.ShapeDtypeStruct((nh, tp, dqk), F32),
                   jax.ShapeDtypeStruct((nh, tp, vd), F32)] + (list(job.out_shape) if job else []),
        scratch_shapes=[pltpu.VMEM((nq, 8, tq), F32), pltpu.VMEM((nq, 8, tq), F32), pltpu.VMEM((tp, vd), BF16),
                        pltpu.VMEM((nq, dqk, tq), F32)] + (list(job.scratch) if job else []),
        compiler_params=_cparams(("arbitrary", "arbitrary") if job else ("parallel", "arbitrary")),
    )(q, k, v, o, do, lse, *(job.inputs if job else ()))
    return outs[0], outs[1], outs[2], list(outs[3:])


def _mix_out_fwd(h, y_lru, o, g_lo, g_mo, w_lo, w_mo, g_post, layer, tm, name):
    tp, d = h.shape
    lw = y_lru.shape[-1]
    nh, vd = o.shape[0], o.shape[-1]

    def body(h_ref, yl_ref, o_ref, glo_ref, gmo_ref, wlo_ref, wmo_ref, gpost_ref, hout_ref, y_ref):
        nl, _ = _rms(yl_ref[...], glo_ref[...])
        y = _dot(nl, wlo_ref[...])
        ssq = jnp.zeros((tm, 1), F32)
        for hd in range(nh):
            oh = o_ref[hd]
            ssq += jnp.sum(oh * oh, axis=-1, keepdims=True)
        r = lax.rsqrt(ssq / (nh * vd) + EPS)
        for hd in range(nh):
            y += _dot(o_ref[hd] * r * gmo_ref[hd], wmo_ref[hd])
        y_ref[...] = y
        hout_ref[...] = h_ref[...] + _rms(y, gpost_ref[...])[0]

    tok = pl.BlockSpec((tm, d), lambda i: (i, 0))

    def lay(*shape):
        return pl.BlockSpec((None,) + shape, lambda i: (layer,) + (0,) * len(shape))

    return pl.pallas_call(
        body, name=name, grid=(tp // tm,),
        in_specs=[tok, pl.BlockSpec((tm, lw), lambda i: (i, 0)), pl.BlockSpec((nh, tm, vd), lambda i: (0, i, 0)),
                  lay(1, lw), lay(nh, 1, vd), _whole(lw, d), _whole(nh, vd, d), lay(1, d)],
        out_specs=[tok, tok],
        out_shape=[jax.ShapeDtypeStruct((tp, d), F32), jax.ShapeDtypeStruct((tp, d), F32)],
        compiler_params=_cparams(("parallel",)),
    )(h, y_lru, o, g_lo, g_mo, w_lo, w_mo, g_post)


def _mix_out_bwd(dh_out, y, y_lru, o, g_lo, g_mo, w_lo, w_mo, g_post, layer, tm, name):
    tp, d = y.shape
    lw = y_lru.shape[-1]
    nh, vd = o.shape[0], o.shape[-1]

    def body(dho_ref, y_ref, yl_ref, o_ref, glo_ref, gmo_ref, wlo_ref, wmo_ref, gpost_ref,
             dyl_ref, do_ref, dy_ref, nl_ref, nm_ref, dgpost_ref, dglo_ref, dgmo_ref):
        i = pl.program_id(0)

        @pl.when(i == 0)
        def _():
            dgpost_ref[...] = jnp.zeros_like(dgpost_ref)
            dglo_ref[...] = jnp.zeros_like(dglo_ref)
            dgmo_ref[...] = jnp.zeros_like(dgmo_ref)

        dy, dgp = _rms_bwd(y_ref[...], gpost_ref[...], dho_ref[...])
        dgpost_ref[...] += dgp
        dy_b = dy.astype(BF16)
        dy_ref[...] = dy_b
        yl = yl_ref[...]
        nl_ref[...] = _rms(yl, glo_ref[...])[0].astype(BF16)
        dyl, dgl = _rms_bwd(yl, glo_ref[...], _dot_nt(dy_b, wlo_ref[...]))
        dyl_ref[...] = dyl
        dglo_ref[...] += dgl
        ssq = jnp.zeros((tm, 1), F32)
        for hd in range(nh):
            oh = o_ref[hd]
            ssq += jnp.sum(oh * oh, axis=-1, keepdims=True)
        r = lax.rsqrt(ssq / (nh * vd) + EPS)
        dn = []
        dot_sum = jnp.zeros((tm, 1), F32)
        for hd in range(nh):
            oh = o_ref[hd] * r
            nm_ref[hd] = (oh * gmo_ref[hd]).astype(BF16)
            dn_h = _dot_nt(dy_b, wmo_ref[hd])
            dgmo_ref[hd] += jnp.sum(dn_h * oh, axis=0, keepdims=True)
            dn_h = dn_h * gmo_ref[hd]
            dot_sum += jnp.sum(dn_h * oh, axis=-1, keepdims=True)
            dn.append(dn_h)
        dot_mean = dot_sum / (nh * vd)
        for hd in range(nh):
            do_ref[hd] = r * (dn[hd] - o_ref[hd] * r * dot_mean)

    tok = pl.BlockSpec((tm, d), lambda i: (i, 0))
    tokl = pl.BlockSpec((tm, lw), lambda i: (i, 0))
    heads = pl.BlockSpec((nh, tm, vd), lambda i: (0, i, 0))

    def lay(*shape):
        return pl.BlockSpec((None,) + shape, lambda i: (layer,) + (0,) * len(shape))

    def acc(*shape):
        return pl.BlockSpec(shape, lambda i: (0,) * len(shape))

    return pl.pallas_call(
        body, name=name, grid=(tp // tm,),
        in_specs=[tok, tok, tokl, heads, lay(1, lw), lay(nh, 1, vd), _whole(lw, d), _whole(nh, vd, d), lay(1, d)],
        out_specs=[tokl, heads, tok, tokl, heads, acc(1, d), acc(1, lw), acc(nh, 1, vd)],
        out_shape=[jax.ShapeDtypeStruct((tp, lw), F32), jax.ShapeDtypeStruct((nh, tp, vd), F32),
                   jax.ShapeDtypeStruct((tp, d), BF16), jax.ShapeDtypeStruct((tp, lw), BF16),
                   jax.ShapeDtypeStruct((nh, tp, vd), BF16), jax.ShapeDtypeStruct((1, d), F32),
                   jax.ShapeDtypeStruct((1, lw), F32), jax.ShapeDtypeStruct((nh, 1, vd), F32)],
        compiler_params=_cparams(("arbitrary",)),
    )(dh_out, y, y_lru, o, g_lo, g_mo, w_lo, w_mo, g_post)


def _loss_head(h, target, n_real, tm, name):
    tp, d = h.shape
    nt = tp // tm

    def body(h_ref, t_ref, dh_ref, part_ref):
        i = pl.program_id(0)
        row = i * tm + lax.broadcasted_iota(jnp.int32, (tm, 1), 0)
        real = (row >= N_META) & (row < N_META + n_real)
        err = jnp.where(real, h_ref[...] - t_ref[...], 0.0)
        dh_ref[...] = err * (1.0 / d)
        part_ref[...] = jnp.full((1, 1, 128), 0.5 / d * jnp.sum(err * err), F32)

    tok = pl.BlockSpec((tm, d), lambda i: (i, 0))
    return pl.pallas_call(
        body, name=name, grid=(nt,), in_specs=[tok, tok],
        out_specs=[tok, pl.BlockSpec((1, 1, 128), lambda i: (i, 0, 0))],
        out_shape=[jax.ShapeDtypeStruct((tp, d), F32), jax.ShapeDtypeStruct((nt, 1, 128), F32)],
        compiler_params=_cparams(("parallel",)),
    )(h, target)


def _mesh_pos():
    return lax.axis_index("x"), lax.axis_index("y"), lax.axis_index("c")


def _other_chips(x, y):
    return [(1 - x, y), (x, 1 - y), (1 - x, 1 - y)]


class _GatherJob:
    def __init__(self, inputs, picks, shard_shapes, dtypes, split_last=()):
        self.inputs = list(inputs)
        self.picks = list(picks)
        n = len(self.inputs)
        self.out_shape = [jax.ShapeDtypeStruct((N_CHIPS,) + tuple(s), dt) for s, dt in zip(shard_shapes, dtypes)]
        self.split_last = [a in split_last for a in range(n)]
        self.halves = [(s[-1] if last else s[0]) // 2 for s, last in zip(shard_shapes, self.split_last)]
        self.scratch = [pltpu.SemaphoreType.DMA((n, 3)), pltpu.SemaphoreType.DMA((n, 3)), pltpu.SemaphoreType.DMA((n, 3)),
                        pltpu.SemaphoreType.DMA((n, 3)), pltpu.SemaphoreType.DMA((n,))]

    def _copies(self, ins, outs, sems):
        ici_s, ici_r, core_s, core_r, loc = sems
        x, y, c = _mesh_pos()
        me = 2 * x + y
        chips = _other_chips(x, y)
        local, ici, fwd = [], [], []
        for a in range(len(ins)):
            src = self.picks[a](ins[a])
            hv = self.halves[a]

            def half(ref, which, a=a, hv=hv):
                return ref.at[:, pl.ds(which * hv, hv)] if self.split_last[a] else ref.at[pl.ds(which * hv, hv)]

            local.append(pltpu.make_async_copy(src, outs[a].at[me], loc.at[a]))
            for p, (px, py) in enumerate(chips):
                ici.append((pltpu.make_async_remote_copy(src_ref=half(src, c), dst_ref=half(outs[a].at[me], c),
                                                         send_sem=ici_s.at[a, p], recv_sem=ici_r.at[a, p],
                                                         device_id=(px, py, c), device_id_type=MESH),
                            pltpu.make_async_remote_copy(src_ref=half(src, c), dst_ref=half(outs[a].at[2 * px + py], c),
                                                         send_sem=ici_s.at[a, p], recv_sem=ici_r.at[a, p],
                                                         device_id=(px, py, c), device_id_type=MESH)))
                landed = half(outs[a].at[2 * px + py], c)
                theirs = half(outs[a].at[2 * px + py], 1 - c)
                fwd.append((pltpu.make_async_remote_copy(src_ref=landed, dst_ref=landed, send_sem=core_s.at[a, p],
                                                         recv_sem=core_r.at[a, p], device_id=(x, y, 1 - c), device_id_type=MESH),
                            pltpu.make_async_remote_copy(src_ref=theirs, dst_ref=theirs, send_sem=core_s.at[a, p],
                                                         recv_sem=core_r.at[a, p], device_id=(x, y, 1 - c), device_id_type=MESH)))
        return local, ici, fwd

    def start(self, ins, outs, sems):
        local, ici, _ = self._copies(ins, outs, sems)
        for cp in local:
            cp.start()
        for send, _ in ici:
            send.start()

    def finish(self, ins, outs, sems):
        local, ici, fwd = self._copies(ins, outs, sems)
        for (_, arrive), (hand_on, _) in zip(ici, fwd):
            arrive.wait_recv()
            hand_on.start()
        for _, arrive in fwd:
            arrive.wait_recv()
        for send, _ in ici:
            send.wait_send()
        for hand_on, _ in fwd:
            hand_on.wait_send()
        for cp in local:
            cp.wait()


class _ReduceJob:
    def __init__(self, inputs):
        self.inputs = list(inputs)
        n = len(self.inputs)
        self.out_shape = [jax.ShapeDtypeStruct(a.shape, a.dtype) for a in self.inputs] * 2
        self.scratch = [pltpu.SemaphoreType.DMA((n, 3)), pltpu.SemaphoreType.DMA((n, 3)), pltpu.SemaphoreType.DMA((n, 4)),
                        pltpu.SemaphoreType.DMA((n, 4)), pltpu.SemaphoreType.DMA((n,))]

    def _copies(self, ins, outs, sems):
        ici_s, ici_r, core_s, core_r, loc = sems
        n = len(ins)
        same, other = outs[:n], outs[n:]
        x, y, c = _mesh_pos()
        me = 2 * x + y
        chips = _other_chips(x, y)
        sib = (x, y, 1 - c)
        local, ici, fwd = [], [], []
        for a in range(n):
            local.append(pltpu.make_async_copy(ins[a].at[me], same[a].at[me], loc.at[a]))
            fwd.append((pltpu.make_async_remote_copy(src_ref=ins[a].at[me], dst_ref=other[a].at[me], send_sem=core_s.at[a, 3],
                                                     recv_sem=core_r.at[a, 3], device_id=sib, device_id_type=MESH), None))
            for p, (px, py) in enumerate(chips):
                cp = 2 * px + py
                ici.append((pltpu.make_async_remote_copy(src_ref=ins[a].at[cp], dst_ref=same[a].at[me], send_sem=ici_s.at[a, p],
                                                         recv_sem=ici_r.at[a, p], device_id=(px, py, c), device_id_type=MESH),
                            pltpu.make_async_remote_copy(src_ref=ins[a].at[cp], dst_ref=same[a].at[cp], send_sem=ici_s.at[a, p],
                                                         recv_sem=ici_r.at[a, p], device_id=(px, py, c), device_id_type=MESH)))
                fwd.append((pltpu.make_async_remote_copy(src_ref=same[a].at[cp], dst_ref=other[a].at[cp], send_sem=core_s.at[a, p],
                                                         recv_sem=core_r.at[a, p], device_id=sib, device_id_type=MESH), p))
        return local, ici, fwd

    def start(self, ins, outs, sems):
        local, ici, fwd = self._copies(ins, outs, sems)
        for cp in local:
            cp.start()
        for send, _ in ici:
            send.start()
        for hand_on, p in fwd:
            if p is None:
                hand_on.start()

    def finish(self, ins, outs, sems):
        local, ici, fwd = self._copies(ins, outs, sems)
        chip_fwd = [f for f in fwd if f[1] is not None]
        for (_, arrive), (hand_on, _) in zip(ici, chip_fwd):
            arrive.wait_recv()
            hand_on.start()
        for hand_on, _ in fwd:
            hand_on.wait()
        for send, _ in ici:
            send.wait_send()
        for cp in local:
            cp.wait()


def _run_job(job, name):
    n_in, n_out = len(job.inputs), len(job.out_shape)

    def body(*refs):
        ins, outs, sems = refs[:n_in], refs[n_in:n_in + n_out], refs[n_in + n_out:]
        job.start(ins, outs, sems)
        job.finish(ins, outs, sems)

    return list(pl.pallas_call(body, name=name, in_specs=[_ANY] * n_in, out_specs=[_ANY] * n_out, out_shape=list(job.out_shape),
                               scratch_shapes=list(job.scratch))(*job.inputs))


def _allreduce_small(buf, name):
    rows, lanes = buf.shape
    n_dev = 8

    def body(in_ref, out_ref, gather, send_sems, recv_sems):
        x, y, c = _mesh_pos()
        me = 4 * x + 2 * y + c
        gather[me] = in_ref[...]
        sends = []
        for rel in range(1, n_dev):
            fx, fy, fc = (rel >> 2) & 1, (rel >> 1) & 1, rel & 1
            peer = (1 - x if fx else x, 1 - y if fy else y, 1 - c if fc else c)
            rc = pltpu.make_async_remote_copy(src_ref=in_ref, dst_ref=gather.at[me], send_sem=send_sems.at[rel - 1],
                                              recv_sem=recv_sems.at[rel - 1], device_id=peer, device_id_type=MESH)
            rc.start()
            sends.append(rc)
        for rc in sends:
            rc.wait()
        total = gather[0]
        for k in range(1, n_dev):
            total = total + gather[k]
        out_ref[...] = total

    vm = pl.BlockSpec(memory_space=pltpu.VMEM)
    return pl.pallas_call(
        body, name=name, in_specs=[vm], out_specs=vm, out_shape=jax.ShapeDtypeStruct((rows, lanes), F32),
        scratch_shapes=[pltpu.VMEM((n_dev, rows, lanes), F32), pltpu.SemaphoreType.DMA((n_dev - 1,)),
                        pltpu.SemaphoreType.DMA((n_dev - 1,))],
        compiler_params=pltpu.CompilerParams(vmem_limit_bytes=V7X_VMEM_LIMIT_BYTES),
    )(buf)


def _row_tile(rows, limit=512):
    best = None
    for t in range(16, min(rows, limit) + 1, 16):
        if rows % t == 0:
            best = t
    return best if best is not None else rows


def _adamw_math(w, g, m, v):
    m = ADAM_B1 * m + (1.0 - ADAM_B1) * g
    v = ADAM_B2 * v + (1.0 - ADAM_B2) * (g * g)
    m_hat = m / (1.0 - ADAM_B1 ** ADAM_STEP)
    v_hat = v / (1.0 - ADAM_B2 ** ADAM_STEP)
    delta = -ADAM_LR * (m_hat / (jnp.sqrt(v_hat) + ADAM_EPS) + ADAM_WD * w)
    return delta, m, v


def _adamw_layer(w, m, v, same_plane, other_plane, layer, prev, name):
    nl, r, ccol = w.shape
    ns = same_plane.shape[0]
    tr = _row_tile(r, 256)

    def body(w_ref, m_ref, v_ref, a_ref, b_ref, *rest):
        go_ref, d_ref, mo_ref, vo_ref = rest[-4:]
        sa = a_ref[0].astype(F32)
        sb = b_ref[0].astype(F32)
        for k in range(1, ns):
            sa = sa + a_ref[k].astype(F32)
            sb = sb + b_ref[k].astype(F32)
        g = sa + sb
        go_ref[...] = g
        d_ref[...], mo_ref[...], vo_ref[...] = _adamw_math(w_ref[...], g, m_ref[...], v_ref[...])

    blk = pl.BlockSpec((None, tr, ccol), lambda i: (layer, i, 0))
    plane = pl.BlockSpec((ns, tr, ccol), lambda i: (0, i, 0))
    out = jax.ShapeDtypeStruct((nl, r, ccol), F32)
    n_prev = 4 if prev is not None else 0
    return pl.pallas_call(
        body, name=name, grid=(r // tr,),
        in_specs=[blk, blk, blk, plane, plane] + [_ANY] * n_prev,
        out_specs=[blk, blk, blk, blk], out_shape=[out, out, out, out],
        input_output_aliases={5 + k: k for k in range(n_prev)},
        compiler_params=_cparams(("parallel",)))(w, m, v, same_plane, other_plane, *(prev or ()))


def _adamw_whole(w, g, m, v, name):
    def body(w_ref, g_ref, m_ref, v_ref, d_ref, mo_ref, vo_ref):
        d_ref[...], mo_ref[...], vo_ref[...] = _adamw_math(w_ref[...], g_ref[...], m_ref[...], v_ref[...])

    out = jax.ShapeDtypeStruct(w.shape, F32)
    return pl.pallas_call(body, name=name, out_shape=[out, out, out],
                          compiler_params=pltpu.CompilerParams(vmem_limit_bytes=V7X_VMEM_LIMIT_BYTES))(w, g, m, v)


_W_NAMES = ['meta_tokens', 'ffn1_pre_g', 'ffn1_w_gate', 'ffn1_w_up', 'ffn1_w_down', 'ffn1_post_g', 'mix_pre_g', 'w_in',
            'lru_conv_w', 'lru_conv_b', 'lru_w_a', 'lru_b_a', 'lru_w_x', 'lru_b_x', 'lru_lambda', 'mla_q_norm_g', 'mla_w_uq',
            'mla_kv_norm_g', 'mla_w_ukv', 'lru_out_g', 'mla_out_g', 'w_out', 'mix_post_g', 'ffn2_pre_g', 'ffn2_w_gate',
            'ffn2_w_up', 'ffn2_w_down', 'ffn2_post_g']
_REPLICATED = ['ffn1_pre_g', 'ffn1_post_g', 'mix_pre_g', 'lru_conv_b', 'lru_w_a', 'lru_b_a', 'lru_w_x', 'lru_b_x', 'lru_lambda',
               'mla_q_norm_g', 'mla_kv_norm_g', 'lru_out_g', 'mla_out_g', 'mix_post_g', 'ffn2_pre_g', 'ffn2_post_g']
_LANES = 128


def _pack_rows(arrays):
    flat = jnp.concatenate([a.reshape(-1) for a in arrays])
    total = _round_up(flat.shape[0], 8 * _LANES)
    return jnp.pad(flat, (0, total - flat.shape[0])).reshape(-1, _LANES)


def _unpack_rows(buf, shapes):
    flat = buf.reshape(-1)
    out, off = [], 0
    for shp in shapes:
        n = math.prod(shp)
        out.append(flat[off:off + n].reshape(shp))
        off += n
    return out


def _block_diag(w):
    nl, nh, n, _ = w.shape
    eye = jnp.eye(nh, dtype=w.dtype)
    return (w[:, :, :, None, :] * eye[None, :, None, :, None]).reshape(nl, nh * n, nh * n)


def _diag_blocks(bd, nh):
    n = bd.shape[0] // nh
    b4 = bd.reshape(nh, n, nh, n)
    return jnp.stack([b4[i, :, i, :] for i in range(nh)])


def kernel(x, meta_tokens, ffn1_pre_g, ffn1_w_gate, ffn1_w_up, ffn1_w_down, ffn1_post_g, mix_pre_g, w_in, lru_conv_w, lru_conv_b, lru_w_a, lru_b_a, lru_w_x, lru_b_x, lru_lambda, mla_q_norm_g, mla_w_uq, mla_kv_norm_g, mla_w_ukv, lru_out_g, mla_out_g, w_out, mix_post_g, ffn2_pre_g, ffn2_w_gate, ffn2_w_up, ffn2_w_down, ffn2_post_g, loss_target, m_meta_tokens, m_ffn1_pre_g, m_ffn1_w_gate, m_ffn1_w_up, m_ffn1_w_down, m_ffn1_post_g, m_mix_pre_g, m_w_in, m_lru_conv_w, m_lru_conv_b, m_lru_w_a, m_lru_b_a, m_lru_w_x, m_lru_b_x, m_lru_lambda, m_mla_q_norm_g, m_mla_w_uq, m_mla_kv_norm_g, m_mla_w_ukv, m_lru_out_g, m_mla_out_g, m_w_out, m_mix_post_g, m_ffn2_pre_g, m_ffn2_w_gate, m_ffn2_w_up, m_ffn2_w_down, m_ffn2_post_g, v_meta_tokens, v_ffn1_pre_g, v_ffn1_w_gate, v_ffn1_w_up, v_ffn1_w_down, v_ffn1_post_g, v_mix_pre_g, v_w_in, v_lru_conv_w, v_lru_conv_b, v_lru_w_a, v_lru_b_a, v_lru_w_x, v_lru_b_x, v_lru_lambda, v_mla_q_norm_g, v_mla_w_uq, v_mla_kv_norm_g, v_mla_w_ukv, v_lru_out_g, v_mla_out_g, v_w_out, v_mix_post_g, v_ffn2_pre_g, v_ffn2_w_gate, v_ffn2_w_up, v_ffn2_w_down, v_ffn2_post_g):
    a = dict(locals())
    x2 = x[0]
    s_len, d = x2.shape
    nl = ffn1_pre_g.shape[0]
    lw = lru_conv_b.shape[-1]
    ql, kvl = mla_q_norm_g.shape[-1], mla_kv_norm_g.shape[-1]
    nh = MLA_HEADS
    vd = (d - lw) // nh
    t_real = N_META + s_len
    tp = _round_up(t_real, 384)
    tm = tp // 6
    tmb = tm // 2
    tm_ffn_fwd = tp // 4
    tm_ffn_bwd = tp // 8
    tq = 384 if tp >= 1536 else 128
    xi, yi, _ = _mesh_pos()
    chip = 2 * xi + yi

    def tr(w):
        return jnp.swapaxes(w, 1, 2)

    ffn_loc = jnp.stack([tr(ffn1_w_gate), tr(ffn1_w_up), ffn1_w_down, tr(ffn2_w_gate), tr(ffn2_w_up), ffn2_w_down],
                        axis=1).astype(BF16)
    loc = [ffn_loc, tr(w_in).astype(BF16), tr(mla_w_uq).astype(BF16), mla_w_ukv.astype(BF16), w_out.astype(BF16)]
    loc_shapes, loc_dtypes = [t.shape[1:] for t in loc], [t.dtype for t in loc]

    def layer_gather(l):
        return _GatherJob(loc, [lambda r, l=l: r.at[l]] * len(loc), loc_shapes, loc_dtypes, split_last=(1,))

    first = _GatherJob(loc + [lru_conv_w, meta_tokens], [lambda r: r.at[0]] * len(loc) + [lambda r: r] * 2,
                       loc_shapes + [lru_conv_w.shape, meta_tokens.shape], loc_dtypes + [F32, F32], split_last=(1,))
    *gathered, g_conv, g_meta = _run_job(first, "gather_layer0")
    o0, o1, o2, o3 = lw, 2 * lw, 2 * lw + ql, 2 * lw + ql + kvl

    def assemble(gw):
        w_ffn, g_win, g_wuq, g_wukv, g_wout = gw
        w_in_t = g_win.reshape(-1, d)
        w_kr = jnp.pad(w_in_t[o3:], ((QK_NOPE, 0), (0, 0)))
        w_ukv3 = jnp.transpose(g_wukv, (1, 0, 2)).reshape(kvl, nh, QK_NOPE + vd)
        w_out_full = g_wout.reshape(d, d)
        return dict(ffn=w_ffn, parts=(w_in_t[:o0], w_in_t[o0:o1], w_in_t[o1:o2], w_in_t[o2:o3], w_kr),
                    q=g_wuq.reshape(nh, QK_DIM, ql),
                    k=jnp.transpose(jnp.pad(w_ukv3[..., :QK_NOPE], ((0, 0), (0, 0), (0, QK_ROPE))), (1, 0, 2)),
                    v=jnp.transpose(w_ukv3[..., QK_NOPE:], (1, 0, 2)),
                    lo=w_out_full[:lw], mo=w_out_full[lw:].reshape(nh, vd, d))

    conv_full = jnp.transpose(g_conv, (1, 2, 0, 3)).reshape(nl, CONV_WIDTH, lw)
    meta_full = jnp.transpose(g_meta, (1, 0, 2)).reshape(N_META, d)
    wa_bd = _block_diag(lru_w_a).astype(BF16)
    wx_bd = _block_diag(lru_w_x).astype(BF16)

    def gain3(g):
        return g.reshape(nl, 1, g.shape[-1])

    g3 = {n: gain3(a[n]) for n in ('ffn1_pre_g', 'ffn1_post_g', 'mix_pre_g', 'lru_conv_b', 'lru_b_a', 'lru_b_x', 'lru_lambda',
                                   'mla_q_norm_g', 'mla_kv_norm_g', 'lru_out_g', 'mix_post_g', 'ffn2_pre_g', 'ffn2_post_g')}
    g_mo = mla_out_g.reshape(nl, nh, 1, vd)

    pos = jnp.arange(tp, dtype=F32)
    inv_freq = 1.0 / (ROPE_THETA ** (jnp.arange(0, QK_ROPE, 2, dtype=F32) / QK_ROPE))
    ang = pos[:, None] * inv_freq[None, :]
    cos_t = jnp.concatenate([jnp.ones((tp, QK_NOPE), F32), jnp.cos(ang), jnp.cos(ang)], axis=1)
    sin_t = jnp.concatenate([jnp.zeros((tp, QK_NOPE), F32), jnp.sin(ang), jnp.sin(ang)], axis=1)
    half = QK_ROPE // 2
    rot = np.zeros((QK_DIM, QK_DIM), np.float32)
    for i in range(half):
        rot[QK_NOPE + half + i, QK_NOPE + i] = -1.0
        rot[QK_NOPE + i, QK_NOPE + half + i] = 1.0
    rot_b, rot_t = jnp.asarray(rot, BF16), jnp.asarray(rot.T, BF16)

    h = jnp.concatenate([meta_full, x2, jnp.zeros((tp - t_real, d), F32)], axis=0)
    target = jnp.pad(loss_target[0], ((N_META, tp - t_real), (0, 0)))
    saved, weights = [], []
    for l in range(nl):
        wl = assemble(gathered)
        weights.append(wl)
        h1, gate1, up1, f1 = _ffn_fwd(h, g3['ffn1_pre_g'], g3['ffn1_post_g'], wl['ffn'], l, 0, 1, 2, tm_ffn_fwd,
                                      f"ffn1_fwd_{l}")
        xr, gr, cq, ckv, q, k, v = _mix_in_fwd(h1, g3['mix_pre_g'], wl['parts'], g3['mla_q_norm_g'], g3['mla_kv_norm_g'],
                                               wl['q'], wl['k'], wl['v'], cos_t, sin_t, rot_b, l, tm, f"mix_in_fwd_{l}")
        y_lru, xc, hs = _lru_fwd(xr, gr, conv_full, g3['lru_conv_b'], wa_bd, wx_bd, g3['lru_b_a'], g3['lru_b_x'],
                                 g3['lru_lambda'], l, tm, f"lru_fwd_{l}")
        o, lse, gathered = _attn_fwd(q, k, v, tq, f"attn_fwd_{l}", layer_gather(l + 1) if l + 1 < nl else None)
        h2, y = _mix_out_fwd(h1, y_lru, o, g3['lru_out_g'], g_mo, wl['lo'], wl['mo'], g3['mix_post_g'], l, tm, f"mix_out_fwd_{l}")
        h3, gate2, up2, f2 = _ffn_fwd(h2, g3['ffn2_pre_g'], g3['ffn2_post_g'], wl['ffn'], l, 3, 4, 5, tm_ffn_fwd,
                                      f"ffn2_fwd_{l}")
        saved.append((h, gate1, up1, f1, h1, xr, gr, cq, ckv, q, k, v, y_lru, xc, hs, o, lse, y, h2, gate2, up2, f2))
        h = h3

    dh, loss_parts = _loss_head(h, target, s_len, tm, "loss_head")
    loss = lax.psum(jnp.sum(loss_parts[:, 0, 0]), ("x", "y", "c"))

    small = {n: [None] * nl for n in _REPLICATED + ['lru_conv_w']}
    planes = {}
    late_names = ['w_in', 'mla_w_uq', 'mla_w_ukv', 'ffn1_w_gate', 'ffn1_w_up', 'ffn1_w_down']
    early_names = ['ffn2_w_gate', 'ffn2_w_up', 'ffn2_w_down', 'w_out']
    late = None

    def keep_planes(keys, outs):
        for i, key in enumerate(keys):
            planes[key] = (outs[i], outs[len(keys) + i])

    for l in reversed(range(nl)):
        (h0, gate1, up1, f1, h1, xr, gr, cq, ckv, q, k, v, y_lru, xc, hs, o, lse, y, h2, gate2, up2, f2) = saved[l]
        wl = weights[l]
        dh, dgate, dup, act, df, u, dgpre, dgpost = _ffn_bwd(dh, h2, f2, gate2, up2, g3['ffn2_pre_g'], g3['ffn2_post_g'],
                                                            wl['ffn'], l, 3, 4, 5, tm_ffn_bwd, f"ffn2_bwd_{l}")
        small['ffn2_pre_g'][l], small['ffn2_post_g'][l] = dgpre, dgpost
        d_wg2 = _mm_tn(dgate, u, tm, BF16, f"dw_gate2_{l}")
        d_wu2 = _mm_tn(dup, u, tm, BF16, f"dw_up2_{l}")
        dn2 = _mm_tn(act, df, tm, BF16, f"dw_down2_{l}")

        d_ylru, d_o, dy_b, n_lo, n_mo, dgpost, dglo, dgmo = _mix_out_bwd(dh, y, y_lru, o, g3['lru_out_g'], g_mo, wl['lo'], wl['mo'],
                                                                       g3['mix_post_g'], l, tm, f"mix_out_bwd_{l}")
        small['mix_post_g'][l], small['lru_out_g'][l], small['mla_out_g'][l] = dgpost, dglo, dgmo
        d_wlo = _mm_tn(n_lo, dy_b, tm, F32, f"dw_out_lru_{l}")
        d_wmo = _mm_tn_heads(n_mo, dy_b, tm, f"dw_out_mla_{l}")
        d_wout = jnp.concatenate([d_wlo, d_wmo.reshape(nh * vd, d)], axis=0).reshape(N_CHIPS, d // N_CHIPS, d).astype(BF16)

        keys = [(n, l) for n in early_names] + ([(n, l + 1) for n in late_names] if late else [])
        dq, dk, dv, outs = _attn_bwd(q, k, v, o, d_o, lse, tq, f"attn_bwd_{l}",
                                     _ReduceJob([d_wg2, d_wu2, dn2, d_wout] + (late or [])))
        keep_planes(keys, outs)
        d_xr, d_gr, dcw, dcb, dwa, dwx, dba, dbx, dlam = _lru_bwd(d_ylru, xr, gr, xc, hs, conv_full, wa_bd, wx_bd, g3['lru_b_a'],
                                                                 g3['lru_b_x'], g3['lru_lambda'], l, tm, f"lru_bwd_{l}")
        small['lru_conv_w'][l], small['lru_conv_b'][l] = dcw, dcb
        small['lru_w_a'][l], small['lru_w_x'][l] = _diag_blocks(dwa, LRU_HEADS), _diag_blocks(dwx, LRU_HEADS)
        small['lru_b_a'][l], small['lru_b_x'][l], small['lru_lambda'][l] = dba, dbx, dlam

        dh, u, qn, kvn, dq_pre, d_cq, d_ckv, d_kr, dgpre, dgq, dgkv = _mix_in_bwd(
            dh, h1, cq, ckv, d_xr, d_gr, dq, dk, dv, g3['mix_pre_g'], wl['parts'], g3['mla_q_norm_g'], g3['mla_kv_norm_g'],
            wl['q'], wl['k'], wl['v'], cos_t, sin_t, rot_t, l, tmb, f"mix_in_bwd_{l}")
        small['mix_pre_g'][l], small['mla_q_norm_g'][l], small['mla_kv_norm_g'][l] = dgpre, dgq, dgkv
        d_win = jnp.concatenate([_mm_tn(d_xr, u, tm, F32, f"dw_in_xr_{l}"), _mm_tn(d_gr, u, tm, F32, f"dw_in_gr_{l}"),
                                 _mm_tn(d_cq, u, tm, F32, f"dw_in_cq_{l}"), _mm_tn(d_ckv, u, tm, F32, f"dw_in_ckv_{l}"),
                                 _mm_tn(d_kr, u, tm, F32, f"dw_in_kr_{l}")[QK_NOPE:]], axis=0)
        d_win = d_win.reshape(N_CHIPS, -1, d).astype(BF16)
        d_wuq = _mm_tn_heads(dq_pre, qn, tm, f"dw_uq_{l}").reshape(N_CHIPS, -1, ql).astype(BF16)
        d_wk = _mm_tn_heads(dk, kvn, tm, f"dw_uk_{l}")[:, :QK_NOPE]
        d_wv = _mm_tn_heads(dv, kvn, tm, f"dw_uv_{l}")
        d_wukv = jnp.transpose(jnp.concatenate([d_wk, d_wv], axis=1).reshape(N_CHIPS, -1, kvl), (0, 2, 1)).astype(BF16)

        dh, dgate, dup, act, df, u, dgpre, dgpost = _ffn_bwd(dh, h0, f1, gate1, up1, g3['ffn1_pre_g'], g3['ffn1_post_g'],
                                                            wl['ffn'], l, 0, 1, 2, tm_ffn_bwd, f"ffn1_bwd_{l}")
        small['ffn1_pre_g'][l], small['ffn1_post_g'][l] = dgpre, dgpost
        d_wg1 = _mm_tn(dgate, u, tm, BF16, f"dw_gate1_{l}")
        d_wu1 = _mm_tn(dup, u, tm, BF16, f"dw_up1_{l}")
        dn1 = _mm_tn(act, df, tm, BF16, f"dw_down1_{l}")
        late = [d_win, d_wuq, d_wukv, d_wg1, d_wu1, dn1]

    grad_x = dh[N_META:t_real][None]
    keep_planes([(n, 0) for n in late_names], _run_job(_ReduceJob(late), "reduce_last"))

    res = {}
    stored_transposed = ('ffn1_w_gate', 'ffn1_w_up', 'ffn2_w_gate', 'ffn2_w_up', 'w_in', 'mla_w_uq')
    for name in early_names + late_names:
        view = tr if name in stored_transposed else (lambda t: t)
        w_v, m_v, v_v = view(a[name]), view(a['m_' + name]), view(a['v_' + name])
        prev = None
        for l in reversed(range(nl)):
            same_plane, other_plane = planes[(name, l)]
            prev = _adamw_layer(w_v, m_v, v_v, same_plane, other_plane, l, prev, f"adamw_{name}_{l}")
        res[name] = [view(t) for t in prev]

    small_full = {n: jnp.stack(small[n]).reshape(a[n].shape if n != 'lru_conv_w' else conv_full.shape)
                  for n in _REPLICATED + ['lru_conv_w']}
    order = _REPLICATED + ['lru_conv_w']
    packed = _pack_rows([small_full[n] for n in order] + [dh[:N_META]])
    summed = _allreduce_small(packed, "allreduce_small")
    pieces = _unpack_rows(summed, [small_full[n].shape for n in order] + [(N_META, d)])
    g_small = dict(zip(order, pieces[:-1]))
    g_conv_loc = lax.dynamic_slice_in_dim(g_small.pop('lru_conv_w'), chip * lru_conv_w.shape[-1], lru_conv_w.shape[-1], axis=2)
    g_meta_loc = lax.dynamic_slice_in_dim(pieces[-1], chip * meta_tokens.shape[-1], meta_tokens.shape[-1], axis=1)

    shapes = [a[n].shape for n in _REPLICATED]
    dl, mo, vo = _adamw_whole(_pack_rows([a[n] for n in _REPLICATED]), _pack_rows([g_small[n] for n in _REPLICATED]),
                              _pack_rows([a['m_' + n] for n in _REPLICATED]), _pack_rows([a['v_' + n] for n in _REPLICATED]),
                              "adamw_replicated")
    for n, dd, mm, vv in zip(_REPLICATED, _unpack_rows(dl, shapes), _unpack_rows(mo, shapes), _unpack_rows(vo, shapes)):
        res[n] = (g_small[n], dd, mm, vv)
    cshape = lru_conv_w.shape
    c2 = (cshape[0] * cshape[1], cshape[2])
    dd, mm, vv = _adamw_whole(lru_conv_w.reshape(c2), g_conv_loc.reshape(c2), m_lru_conv_w.reshape(c2), v_lru_conv_w.reshape(c2),
                              "adamw_conv_w")
    res['lru_conv_w'] = (g_conv_loc, dd.reshape(cshape), mm.reshape(cshape), vv.reshape(cshape))
    res['meta_tokens'] = (g_meta_loc,) + tuple(_adamw_whole(meta_tokens, g_meta_loc, m_meta_tokens, v_meta_tokens, "adamw_meta"))

    return (loss, grad_x, *[res[n][0] for n in _W_NAMES], *[res[n][1] for n in _W_NAMES],
            *[res[n][2] for n in _W_NAMES], *[res[n][3] for n in _W_NAMES])
```

```python
import functools
import math

import jax
import jax.numpy as jnp
import numpy as np
from jax import lax
from jax.experimental import pallas as pl
from jax.experimental.pallas import tpu as pltpu

F32 = jnp.float32
BF16 = jnp.bfloat16
MESH = pl.DeviceIdType.MESH

EPS = 1e-6
N_META = 16
LRU_HEADS = 8
MLA_HEADS = 8
QK_NOPE = 64
QK_ROPE = 32
QK_DIM = QK_NOPE + QK_ROPE
LRU_C = 8.0
ROPE_THETA = 10000.0
CONV_WIDTH = 4
N_CHIPS = 4

ADAM_LR = 0.001
ADAM_B1 = 0.9
ADAM_B2 = 0.999
ADAM_EPS = 1e-08
ADAM_WD = 0.01
ADAM_STEP = 10

V7X_VMEM_LIMIT_BYTES = 56 * 1024 * 1024
NEG_BIG = -1e30
Q_PRESCALE = QK_DIM ** -0.5 * math.log2(math.e)


def _cparams(sem=None):
    return pltpu.CompilerParams(dimension_semantics=sem, vmem_limit_bytes=V7X_VMEM_LIMIT_BYTES)


def _round_up(a, b):
    return -(-a // b) * b


def _whole(*shape):
    return pl.BlockSpec(shape, lambda *_: (0,) * len(shape))


_ANY = pl.BlockSpec(memory_space=pl.ANY)


def _dot(a, b):
    return jnp.dot(a.astype(BF16), b.astype(BF16), preferred_element_type=F32)


def _dot_nt(a, b):
    return lax.dot_general(a.astype(BF16), b.astype(BF16), (((1,), (1,)), ((), ())), preferred_element_type=F32)


def _dot_tn(a, b):
    return lax.dot_general(a.astype(BF16), b.astype(BF16), (((0,), (0,)), ((), ())), preferred_element_type=F32)


def _dot_split(x, p):
    hi = x.astype(BF16)
    lo = (x - hi.astype(F32)).astype(BF16)
    return jnp.dot(hi, p, preferred_element_type=F32) + jnp.dot(lo, p, preferred_element_type=F32)


def _rms(x, g):
    r = lax.rsqrt(jnp.mean(x * x, axis=-1, keepdims=True) + EPS)
    return x * r * g, r


def _rms_bwd(x, g, dy):
    r = lax.rsqrt(jnp.mean(x * x, axis=-1, keepdims=True) + EPS)
    xh = x * r
    dyg = dy * g
    dx = r * (dyg - xh * jnp.mean(dyg * xh, axis=-1, keepdims=True))
    dg = jnp.sum(dy * xh, axis=0, keepdims=True)
    return dx, dg


def _sigmoid(x):
    return 1.0 / (1.0 + jnp.exp(-x))


def _neg_expm1(x):
    series = -x * (1.0 + x * (0.5 + x * (1.0 / 6.0 + x * (1.0 / 24.0))))
    return jnp.where(jnp.abs(x) < 0.03, series, 1.0 - jnp.exp(x))


_GELU_K = math.sqrt(2.0 / math.pi)


def _gelu(x):
    t = jnp.tanh(_GELU_K * (x + 0.044715 * x * x * x))
    return 0.5 * x * (1.0 + t), t


def _gelu_grad(x, t):
    return 0.5 * (1.0 + t) + 0.5 * x * (1.0 - t * t) * _GELU_K * (1.0 + 3.0 * 0.044715 * x * x)


def _rope(x, cos_t, sin_t, rot):
    return x * cos_t + _dot_split(x, rot) * sin_t


def _ffn_fwd(h, g_pre, g_post, w_ffn, layer, kind_gate, kind_up, kind_dn, tm, name):
    tp, d = h.shape
    n_slot, fs = w_ffn.shape[0], w_ffn.shape[-2]

    def body(h_ref, gpre_ref, gpost_ref, wg_ref, wu_ref, wd_ref, hout_ref, gate_ref, up_ref, f_ref, u_sc, acc_sc):
        j = pl.program_id(1)

        @pl.when(j == 0)
        def _():
            u, _ = _rms(h_ref[...], gpre_ref[...])
            u_sc[...] = u.astype(BF16)
            acc_sc[...] = jnp.zeros_like(acc_sc)

        u = u_sc[...]
        gate = _dot_nt(u, wg_ref[...])
        up = _dot_nt(u, wu_ref[...])
        act = gate * _sigmoid(gate) * up
        gate_ref[...] = gate.astype(BF16)
        up_ref[...] = up.astype(BF16)
        acc_sc[...] += jnp.dot(act.astype(BF16), wd_ref[...], preferred_element_type=F32)

        @pl.when(j == n_slot - 1)
        def _():
            f = acc_sc[...]
            f_ref[...] = f
            n, _ = _rms(f, gpost_ref[...])
            hout_ref[...] = h_ref[...] + 0.5 * n

    tok = pl.BlockSpec((tm, d), lambda i, j: (i, 0))
    gain = pl.BlockSpec((None, 1, d), lambda i, j: (layer, 0, 0))
    slot_act = pl.BlockSpec((None, tm, fs), lambda i, j: (j, i, 0))
    return pl.pallas_call(
        body, name=name, grid=(tp // tm, n_slot),
        in_specs=[tok, gain, gain] + [pl.BlockSpec((None, None, fs, d), lambda i, j, kind=kind: (j, kind, 0, 0))
                                      for kind in (kind_gate, kind_up, kind_dn)],
        out_specs=[tok, slot_act, slot_act, tok],
        out_shape=[jax.ShapeDtypeStruct((tp, d), F32), jax.ShapeDtypeStruct((n_slot, tp, fs), BF16),
                   jax.ShapeDtypeStruct((n_slot, tp, fs), BF16), jax.ShapeDtypeStruct((tp, d), F32)],
        scratch_shapes=[pltpu.VMEM((tm, d), BF16), pltpu.VMEM((tm, d), F32)],
        compiler_params=_cparams(("parallel", "arbitrary")),
    )(h, g_pre, g_post, w_ffn, w_ffn, w_ffn)


def _ffn_bwd(dh_out, h, f, gate, up, g_pre, g_post, w_ffn, layer, kind_gate, kind_up, kind_dn, tm, name):
    tp, d = h.shape
    n_slot, fs = w_ffn.shape[0], w_ffn.shape[-2]

    def body(dho_ref, h_ref, f_ref, gate_ref, up_ref, gpre_ref, gpost_ref, wg_ref, wu_ref, wd_ref,
             dh_ref, dgate_ref, dup_ref, act_ref, df_ref, u_ref, dgpre_ref, dgpost_ref, du_sc):
        i, j = pl.program_id(0), pl.program_id(1)

        @pl.when((i == 0) & (j == 0))
        def _():
            dgpre_ref[...] = jnp.zeros_like(dgpre_ref)
            dgpost_ref[...] = jnp.zeros_like(dgpost_ref)

        @pl.when(j == 0)
        def _():
            df, dg = _rms_bwd(f_ref[...], gpost_ref[...], 0.5 * dho_ref[...])
            df_ref[...] = df.astype(BF16)
            dgpost_ref[...] += dg
            u, _ = _rms(h_ref[...], gpre_ref[...])
            u_ref[...] = u.astype(BF16)
            du_sc[...] = jnp.zeros_like(du_sc)

        g = gate_ref[...].astype(F32)
        u_ = up_ref[...].astype(F32)
        sg = _sigmoid(g)
        silu = g * sg
        dact = _dot_nt(df_ref[...], wd_ref[...])
        dup = dact * silu
        dgate = dact * u_ * (sg * (1.0 + g * (1.0 - sg)))
        act_ref[...] = (silu * u_).astype(BF16)
        dup_b = dup.astype(BF16)
        dgate_b = dgate.astype(BF16)
        dup_ref[...] = dup_b
        dgate_ref[...] = dgate_b
        du_sc[...] += _dot(dgate_b, wg_ref[...]) + _dot(dup_b, wu_ref[...])

        @pl.when(j == n_slot - 1)
        def _():
            dx, dg = _rms_bwd(h_ref[...], gpre_ref[...], du_sc[...])
            dh_ref[...] = dho_ref[...] + dx
            dgpre_ref[...] += dg

    tok = pl.BlockSpec((tm, d), lambda i, j: (i, 0))
    gain = pl.BlockSpec((None, 1, d), lambda i, j: (layer, 0, 0))
    acc = pl.BlockSpec((1, d), lambda i, j: (0, 0))
    slot_act = pl.BlockSpec((None, tm, fs), lambda i, j: (j, i, 0))
    act_shape = jax.ShapeDtypeStruct((n_slot, tp, fs), BF16)
    return pl.pallas_call(
        body, name=name, grid=(tp // tm, n_slot),
        in_specs=[tok, tok, tok, slot_act, slot_act, gain, gain]
        + [pl.BlockSpec((None, None, fs, d), lambda i, j, kind=kind: (j, kind, 0, 0)) for kind in (kind_gate, kind_up, kind_dn)],
        out_specs=[tok, slot_act, slot_act, slot_act, tok, tok, acc, acc],
        out_shape=[jax.ShapeDtypeStruct((tp, d), F32), act_shape, act_shape, act_shape,
                   jax.ShapeDtypeStruct((tp, d), BF16), jax.ShapeDtypeStruct((tp, d), BF16),
                   jax.ShapeDtypeStruct((1, d), F32), jax.ShapeDtypeStruct((1, d), F32)],
        scratch_shapes=[pltpu.VMEM((tm, d), F32)],
        compiler_params=_cparams(("arbitrary", "arbitrary")),
    )(dh_out, h, f, gate, up, g_pre, g_post, w_ffn, w_ffn, w_ffn)


def _mm_tn(a, b, tk, out_dtype, name):
    ga = a.shape[0] if a.ndim == 3 else None
    gb = b.shape[0] if b.ndim == 3 else None
    groups = ga or gb or 1
    t, m = a.shape[-2:]
    n = b.shape[-1]
    nk = t // tk

    def body(a_ref, b_ref, o_ref, acc_sc):
        k = pl.program_id(1)

        @pl.when(k == 0)
        def _():
            acc_sc[...] = jnp.zeros_like(acc_sc)

        acc_sc[...] += _dot_tn(a_ref[...], b_ref[...])

        @pl.when(k == nk - 1)
        def _():
            o_ref[...] = acc_sc[...].astype(out_dtype)

    a_spec = (pl.BlockSpec((None, tk, m), lambda g, k: (g, k, 0)) if ga else pl.BlockSpec((tk, m), lambda g, k: (k, 0)))
    b_spec = (pl.BlockSpec((None, tk, n), lambda g, k: (g, k, 0)) if gb else pl.BlockSpec((tk, n), lambda g, k: (k, 0)))
    out = pl.pallas_call(
        body, name=name, grid=(groups, nk),
        in_specs=[a_spec, b_spec],
        out_specs=pl.BlockSpec((None, m, n), lambda g, k: (g, 0, 0)),
        out_shape=jax.ShapeDtypeStruct((groups, m, n), out_dtype),
        scratch_shapes=[pltpu.VMEM((m, n), F32)],
        compiler_params=_cparams(("parallel", "arbitrary")),
    )(a, b)
    return out if (ga or gb) else out[0]


def _mm_tn_heads(a, b, tk, name):
    groups, t, m = a.shape
    n = b.shape[-1]
    nk = t // tk

    def body(a_ref, b_ref, o_ref):
        k = pl.program_id(0)

        @pl.when(k == 0)
        def _():
            o_ref[...] = jnp.zeros_like(o_ref)

        bb = b_ref[...]
        for g in range(groups):
            o_ref[g] += _dot_tn(a_ref[g], bb)

    return pl.pallas_call(
        body, name=name, grid=(nk,),
        in_specs=[pl.BlockSpec((groups, tk, m), lambda k: (0, k, 0)), pl.BlockSpec((tk, n), lambda k: (k, 0))],
        out_specs=pl.BlockSpec((groups, m, n), lambda k: (0, 0, 0)),
        out_shape=jax.ShapeDtypeStruct((groups, m, n), F32),
        compiler_params=_cparams(("arbitrary",)),
    )(a, b)


def _mix_in_fwd(h, g_pre, w_parts, g_q, g_kv, w_q, w_k, w_v, cos_t, sin_t, rot, layer, tm, name):
    tp, d = h.shape
    w_xr, w_gr, w_cq, w_ckv, w_kr = w_parts
    lw, ql, kvl = w_xr.shape[0], w_cq.shape[0], w_ckv.shape[0]
    nh, vd = w_v.shape[0], w_v.shape[-1]

    def body(h_ref, gpre_ref, wxr_ref, wgr_ref, wcq_ref, wckv_ref, wkr_ref, gq_ref, gkv_ref, wq_ref, wk_ref, wv_ref,
             cos_ref, sin_ref, rot_ref, xr_ref, gr_ref, cq_ref, ckv_ref, q_ref, k_ref, v_ref):
        u, _ = _rms(h_ref[...], gpre_ref[...])
        u = u.astype(BF16)
        xr_ref[...] = _dot_nt(u, wxr_ref[...])
        gr_ref[...] = _dot_nt(u, wgr_ref[...])
        cq = _dot_nt(u, wcq_ref[...])
        ckv = _dot_nt(u, wckv_ref[...])
        kr = _dot_nt(u, wkr_ref[...])
        cq_ref[...] = cq
        ckv_ref[...] = ckv
        cos_b, sin_b, rot_b = cos_ref[...], sin_ref[...], rot_ref[...]
        qn = _rms(cq, gq_ref[...])[0].astype(BF16)
        kvn = _rms(ckv, gkv_ref[...])[0].astype(BF16)
        k_rope = _rope(kr, cos_b, sin_b, rot_b)
        for hd in range(nh):
            q_pre = _dot_nt(qn, wq_ref[hd])
            q_ref[hd] = (_rope(q_pre, cos_b, sin_b, rot_b) * Q_PRESCALE).astype(BF16)
            k_ref[hd] = (jnp.dot(kvn, wk_ref[hd], preferred_element_type=F32) + k_rope).astype(BF16)
            v_ref[hd] = jnp.dot(kvn, wv_ref[hd], preferred_element_type=F32).astype(BF16)

    def tok(n):
        return pl.BlockSpec((tm, n), lambda i: (i, 0))

    def lay(*shape):
        return pl.BlockSpec((None,) + shape, lambda i: (layer,) + (0,) * len(shape))

    def heads(n):
        return pl.BlockSpec((nh, tm, n), lambda i: (0, i, 0))

    return pl.pallas_call(
        body, name=name, grid=(tp // tm,),
        in_specs=[tok(d), lay(1, d), _whole(lw, d), _whole(lw, d), _whole(ql, d), _whole(kvl, d), _whole(QK_DIM, d),
                  lay(1, ql), lay(1, kvl),
                  _whole(nh, QK_DIM, ql), _whole(nh, kvl, QK_DIM), _whole(nh, kvl, vd), tok(QK_DIM), tok(QK_DIM),
                  pl.BlockSpec((QK_DIM, QK_DIM), lambda i: (0, 0))],
        out_specs=[tok(lw), tok(lw), tok(ql), tok(kvl), heads(QK_DIM), heads(QK_DIM), heads(vd)],
        out_shape=[jax.ShapeDtypeStruct((tp, lw), F32), jax.ShapeDtypeStruct((tp, lw), F32),
                   jax.ShapeDtypeStruct((tp, ql), F32), jax.ShapeDtypeStruct((tp, kvl), F32),
                   jax.ShapeDtypeStruct((nh, tp, QK_DIM), BF16), jax.ShapeDtypeStruct((nh, tp, QK_DIM), BF16),
                   jax.ShapeDtypeStruct((nh, tp, vd), BF16)],
        compiler_params=_cparams(("parallel",)),
    )(h, g_pre, w_xr, w_gr, w_cq, w_ckv, w_kr, g_q, g_kv, w_q, w_k, w_v, cos_t, sin_t, rot)


def _mix_in_bwd(dh_res, h, cq, ckv, d_xr, d_gr, dq, dk, dv, g_pre, w_parts, g_q, g_kv, w_q, w_k, w_v,
                cos_t, sin_t, rot_t, layer, tm, name):
    tp, d = h.shape
    w_xr, w_gr, w_cq, w_ckv, w_kr = w_parts
    lw, ql, kvl = w_xr.shape[0], w_cq.shape[0], w_ckv.shape[0]
    nh, vd = w_v.shape[0], w_v.shape[-1]

    def body(dhr_ref, h_ref, cq_ref, ckv_ref, dxr_ref, dgr_ref, dq_ref, dk_ref, dv_ref, gpre_ref,
             wxr_ref, wgr_ref, wcq_ref, wckv_ref, wkr_ref, gq_ref, gkv_ref, wq_ref, wk_ref, wv_ref,
             cos_ref, sin_ref, rott_ref,
             dh_ref, u_ref, qn_ref, kvn_ref, dqpre_ref, dcq_ref, dckv_ref, dkr_ref, dgpre_ref, dgq_ref, dgkv_ref):
        i = pl.program_id(0)

        @pl.when(i == 0)
        def _():
            dgpre_ref[...] = jnp.zeros_like(dgpre_ref)
            dgq_ref[...] = jnp.zeros_like(dgq_ref)
            dgkv_ref[...] = jnp.zeros_like(dgkv_ref)

        cos_b, sin_b, rott_b = cos_ref[...], sin_ref[...], rott_ref[...]
        hh = h_ref[...]
        u, _ = _rms(hh, gpre_ref[...])
        u_ref[...] = u.astype(BF16)
        cq, ckv = cq_ref[...], ckv_ref[...]
        qn = _rms(cq, gq_ref[...])[0]
        kvn = _rms(ckv, gkv_ref[...])[0]
        qn_ref[...] = qn.astype(BF16)
        kvn_ref[...] = kvn.astype(BF16)
        d_qn = jnp.zeros((tm, ql), F32)
        d_kvn = jnp.zeros((tm, kvl), F32)
        d_krope = jnp.zeros((tm, QK_DIM), F32)
        for hd in range(nh):
            dq_h = dq_ref[hd]
            dq_pre = dq_h * cos_b + _dot_split(dq_h * sin_b, rott_b)
            dq_pre_b = dq_pre.astype(BF16)
            dqpre_ref[hd] = dq_pre_b
            d_qn += _dot(dq_pre_b, wq_ref[hd])
            dk_h = dk_ref[hd]
            d_krope += dk_h
            d_kvn += _dot_nt(dk_h, wk_ref[hd]) + _dot_nt(dv_ref[hd], wv_ref[hd])
        d_kr = d_krope * cos_b + _dot_split(d_krope * sin_b, rott_b)
        d_cq, dgq = _rms_bwd(cq, gq_ref[...], d_qn)
        d_ckv, dgkv = _rms_bwd(ckv, gkv_ref[...], d_kvn)
        dgq_ref[...] += dgq
        dgkv_ref[...] += dgkv
        d_cq_b, d_ckv_b, d_kr_b = d_cq.astype(BF16), d_ckv.astype(BF16), d_kr.astype(BF16)
        dcq_ref[...] = d_cq_b
        dckv_ref[...] = d_ckv_b
        dkr_ref[...] = d_kr_b
        du = (_dot(dxr_ref[...], wxr_ref[...]) + _dot(dgr_ref[...], wgr_ref[...]) + _dot(d_cq_b, wcq_ref[...])
              + _dot(d_ckv_b, wckv_ref[...]) + _dot(d_kr_b, wkr_ref[...]))
        dx, dg = _rms_bwd(hh, gpre_ref[...], du)
        dh_ref[...] = dhr_ref[...] + dx
        dgpre_ref[...] += dg

    def tok(n):
        return pl.BlockSpec((tm, n), lambda i: (i, 0))

    def lay(*shape):
        return pl.BlockSpec((None,) + shape, lambda i: (layer,) + (0,) * len(shape))

    def heads(n):
        return pl.BlockSpec((nh, tm, n), lambda i: (0, i, 0))

    def acc(n):
        return pl.BlockSpec((1, n), lambda i: (0, 0))

    return pl.pallas_call(
        body, name=name, grid=(tp // tm,),
        in_specs=[tok(d), tok(d), tok(ql), tok(kvl), tok(lw), tok(lw), heads(QK_DIM), heads(QK_DIM), heads(vd), lay(1, d),
                  _whole(lw, d), _whole(lw, d), _whole(ql, d), _whole(kvl, d), _whole(QK_DIM, d), lay(1, ql), lay(1, kvl),
                  _whole(nh, QK_DIM, ql), _whole(nh, kvl, QK_DIM), _whole(nh, kvl, vd), tok(QK_DIM), tok(QK_DIM),
                  pl.BlockSpec((QK_DIM, QK_DIM), lambda i: (0, 0))],
        out_specs=[tok(d), tok(d), tok(ql), tok(kvl), heads(QK_DIM), tok(ql), tok(kvl), tok(QK_DIM), acc(d), acc(ql), acc(kvl)],
        out_shape=[jax.ShapeDtypeStruct((tp, d), F32), jax.ShapeDtypeStruct((tp, d), BF16),
                   jax.ShapeDtypeStruct((tp, ql), BF16), jax.ShapeDtypeStruct((tp, kvl), BF16),
                   jax.ShapeDtypeStruct((nh, tp, QK_DIM), BF16), jax.ShapeDtypeStruct((tp, ql), BF16),
                   jax.ShapeDtypeStruct((tp, kvl), BF16), jax.ShapeDtypeStruct((tp, QK_DIM), BF16),
                   jax.ShapeDtypeStruct((1, d), F32), jax.ShapeDtypeStruct((1, ql), F32), jax.ShapeDtypeStruct((1, kvl), F32)],
        compiler_params=_cparams(("arbitrary",)),
    )(dh_res, h, cq, ckv, d_xr, d_gr, dq, dk, dv, g_pre, w_xr, w_gr, w_cq, w_ckv, w_kr, g_q, g_kv, w_q, w_k, w_v,
      cos_t, sin_t, rot_t)


def _lru_gates(xc, wa_ref, wx_ref, ba, bx, sp):
    xcb = xc.astype(BF16)
    r = _sigmoid(jnp.dot(xcb, wa_ref[...], preferred_element_type=F32) + ba)
    ig = _sigmoid(jnp.dot(xcb, wx_ref[...], preferred_element_type=F32) + bx)
    log_a = -LRU_C * r * sp
    a = jnp.exp(log_a)
    sq = jnp.sqrt(_neg_expm1(2.0 * log_a))
    return r, ig, a, sq


def _softplus(x):
    return jnp.maximum(x, 0.0) + jnp.log(1.0 + jnp.exp(-jnp.abs(x)))


def _lru_fwd(xr, gr, conv_w, conv_b, wa_bd, wx_bd, b_a, b_x, lam, layer, tc, name):
    tp, w = xr.shape
    pad = 8

    def body(xr_ref, gr_ref, cw_ref, cb_ref, wa_ref, wx_ref, ba_ref, bx_ref, lam_ref, y_ref, xc_ref, hs_ref,
             xe_sc, a_sc, b_sc, st_sc):
        c = pl.program_id(0)

        @pl.when(c == 0)
        def _():
            xe_sc[pl.ds(0, pad), :] = jnp.zeros((pad, w), F32)
            st_sc[...] = jnp.zeros_like(st_sc)

        xe_sc[pl.ds(pad, tc), :] = xr_ref[...]
        xc = cb_ref[...] + xe_sc[pl.ds(pad, tc), :] * cw_ref[pl.ds(CONV_WIDTH - 1, 1), :]
        for k in range(CONV_WIDTH - 1):
            xc = xc + xe_sc[pl.ds(pad - (CONV_WIDTH - 1) + k, tc), :] * cw_ref[pl.ds(k, 1), :]
        xe_sc[pl.ds(0, pad), :] = xe_sc[pl.ds(tc, pad), :]
        xc_ref[...] = xc
        sp = _softplus(-lam_ref[...])
        _, ig, a, sq = _lru_gates(xc, wa_ref, wx_ref, ba_ref[...], bx_ref[...], sp)
        a_sc[...] = a
        b_sc[...] = sq * (ig * xc)

        def step(t, hcur):
            hnew = a_sc[pl.ds(t, 1), :] * hcur + b_sc[pl.ds(t, 1), :]
            hs_ref[pl.ds(t, 1), :] = hnew
            return hnew

        st_sc[...] = lax.fori_loop(0, tc, step, st_sc[...], unroll=8)
        y_ref[...] = hs_ref[...] * _gelu(gr_ref[...])[0]

    tok = pl.BlockSpec((tc, w), lambda c: (c, 0))

    def lay(*shape):
        return pl.BlockSpec((None,) + shape, lambda c: (layer,) + (0,) * len(shape))

    out = jax.ShapeDtypeStruct((tp, w), F32)
    return pl.pallas_call(
        body, name=name, grid=(tp // tc,),
        in_specs=[tok, tok, lay(CONV_WIDTH, w), lay(1, w), lay(w, w), lay(w, w), lay(1, w), lay(1, w), lay(1, w)],
        out_specs=[tok, tok, tok], out_shape=[out, out, out],
        scratch_shapes=[pltpu.VMEM((tc + pad, w), F32), pltpu.VMEM((tc, w), F32), pltpu.VMEM((tc, w), F32),
                        pltpu.VMEM((1, w), F32)],
        compiler_params=_cparams(("arbitrary",)),
    )(xr, gr, conv_w, conv_b, wa_bd, wx_bd, b_a, b_x, lam)


def _lru_bwd(dy, xr, gr, xc, hs, conv_w, wa_bd, wx_bd, b_a, b_x, lam, layer, tc, name):
    tp, w = xr.shape
    pad = 8
    nc = tp // tc
    per = tc // pad

    def body(dy_ref, xr_ref, gr_ref, xc_ref, hs_ref, xrp_ref, hsp_ref, cw_ref, wa_ref, wx_ref, ba_ref, bx_ref, lam_ref,
             dxr_ref, dgr_ref, dcw_ref, dcb_ref, dwa_ref, dwx_ref, dba_ref, dbx_ref, dlam_ref,
             a_sc, dh_sc, dxc_sc, he_sc, xe_sc, carry_sc):
        s = pl.program_id(0)
        first_chunk = s == nc - 1

        @pl.when(s == 0)
        def _():
            for ref in (dcw_ref, dcb_ref, dwa_ref, dwx_ref, dba_ref, dbx_ref, dlam_ref):
                ref[...] = jnp.zeros_like(ref)
            carry_sc[...] = jnp.zeros_like(carry_sc)
            dxc_sc[pl.ds(tc, pad), :] = jnp.zeros((pad, w), F32)

        keep = jnp.where(first_chunk, 0.0, 1.0)
        he_sc[pl.ds(0, pad), :] = hsp_ref[...] * keep
        he_sc[pl.ds(pad, tc), :] = hs_ref[...]
        xe_sc[pl.ds(0, pad), :] = xrp_ref[...] * keep
        xe_sc[pl.ds(pad, tc), :] = xr_ref[...]

        lam_v = lam_ref[...]
        sp = _softplus(-lam_v)
        xc = xc_ref[...]
        r, ig, a, sq = _lru_gates(xc, wa_ref, wx_ref, ba_ref[...], bx_ref[...], sp)
        a_sc[...] = a
        grv = gr_ref[...]
        gl, th = _gelu(grv)
        dyv = dy_ref[...]
        dgr_ref[...] = dyv * hs_ref[...] * _gelu_grad(grv, th)
        dh_sc[...] = dyv * gl

        def step(n, g):
            t = tc - 1 - n
            dh = dh_sc[pl.ds(t, 1), :] + g
            dh_sc[pl.ds(t, 1), :] = dh
            return a_sc[pl.ds(t, 1), :] * dh

        carry_sc[...] = lax.fori_loop(0, tc, step, carry_sc[...], unroll=8)

        dh = dh_sc[...]
        d_a = dh * he_sc[pl.ds(pad - 1, tc), :]
        d_ixc = dh * sq
        d_sq = dh * (ig * xc)
        a2 = a * a
        d_la = d_a * a - d_sq * a2 / sq
        d_r = d_la * (-LRU_C * sp)
        d_sp = jnp.sum(d_la * (-LRU_C * r), axis=0, keepdims=True)
        dlam_ref[...] += d_sp * (-_sigmoid(-lam_v))
        d_pa = d_r * r * (1.0 - r)
        d_px = d_ixc * xc * ig * (1.0 - ig)
        dba_ref[...] += jnp.sum(d_pa, axis=0, keepdims=True)
        dbx_ref[...] += jnp.sum(d_px, axis=0, keepdims=True)
        d_pa_b, d_px_b = d_pa.astype(BF16), d_px.astype(BF16)
        xcb = xc.astype(BF16)
        dwa_ref[...] += _dot_tn(xcb, d_pa_b)
        dwx_ref[...] += _dot_tn(xcb, d_px_b)
        d_xc = d_ixc * ig + _dot_nt(d_pa_b, wa_ref[...]) + _dot_nt(d_px_b, wx_ref[...])
        dcb_ref[...] += jnp.sum(d_xc, axis=0, keepdims=True)
        dxc_sc[pl.ds(0, tc), :] = d_xc
        d_xr = jnp.zeros((tc, w), F32)
        for k in range(CONV_WIDTH):
            off = CONV_WIDTH - 1 - k
            d_xr = d_xr + dxc_sc[pl.ds(off, tc), :] * cw_ref[pl.ds(k, 1), :]
            dcw_ref[pl.ds(k, 1), :] += jnp.sum(d_xc * xe_sc[pl.ds(pad - off, tc), :], axis=0, keepdims=True)
        dxr_ref[...] = d_xr
        dxc_sc[pl.ds(tc, pad), :] = dxc_sc[pl.ds(0, pad), :]

    def rev(cidx):
        return nc - 1 - cidx

    tok = pl.BlockSpec((tc, w), lambda c: (rev(c), 0))
    prev = pl.BlockSpec((pad, w), lambda c: (jnp.maximum(rev(c) * per - 1, 0), 0))

    def lay(*shape):
        return pl.BlockSpec((None,) + shape, lambda c: (layer,) + (0,) * len(shape))

    def acc(*shape):
        return pl.BlockSpec(shape, lambda c: (0,) * len(shape))

    big = jax.ShapeDtypeStruct((tp, w), F32)
    vec = jax.ShapeDtypeStruct((1, w), F32)
    return pl.pallas_call(
        body, name=name, grid=(nc,),
        in_specs=[tok, tok, tok, tok, tok, prev, prev, lay(CONV_WIDTH, w), lay(w, w), lay(w, w), lay(1, w), lay(1, w), lay(1, w)],
        out_specs=[tok, tok, acc(CONV_WIDTH, w), acc(1, w), acc(w, w), acc(w, w), acc(1, w), acc(1, w), acc(1, w)],
        out_shape=[big, big, jax.ShapeDtypeStruct((CONV_WIDTH, w), F32), vec, jax.ShapeDtypeStruct((w, w), F32),
                   jax.ShapeDtypeStruct((w, w), F32), vec, vec, vec],
        scratch_shapes=[pltpu.VMEM((tc, w), F32), pltpu.VMEM((tc, w), F32), pltpu.VMEM((tc + pad, w), F32),
                        pltpu.VMEM((tc + pad, w), F32), pltpu.VMEM((tc + pad, w), F32), pltpu.VMEM((1, w), F32)],
        compiler_params=_cparams(("arbitrary",)),
    )(dy, xr, gr, xc, hs, xr, hs, conv_w, wa_bd, wx_bd, b_a, b_x, lam)


def _attn_fwd(q, k, v, tq, name, job=None):
    nh, tp, dqk = q.shape
    vd = v.shape[-1]
    nq = tp // tq
    n_ji = len(job.inputs) if job else 0
    n_jo = len(job.out_shape) if job else 0

    def body(q_ref, k_ref, v_ref, *rest):
        job_in, (o_ref, lse_ref), rest = rest[:n_ji], rest[n_ji:n_ji + 2], rest[n_ji + 2:]
        job_out, sems = rest[:n_jo], rest[n_jo:]
        i = pl.program_id(1)
        if job:
            first = (pl.program_id(0) == 0) & (i == 0)
            pl.when(first)(lambda: job.start(job_in, job_out, sems))
        qb = q_ref[...]

        def absorb(j, width, carry, diagonal=False):
            m, l, acc = carry
            off = pl.multiple_of(j * tq, tq)
            s = _dot_nt(qb, k_ref[pl.ds(off, width), :])
            if diagonal:
                keep = lax.broadcasted_iota(jnp.int32, (tq, width), 1) <= lax.broadcasted_iota(jnp.int32, (tq, width), 0)
                s = jnp.where(keep, s, NEG_BIG)
            m_new = jnp.maximum(m, jnp.max(s, axis=-1, keepdims=True))
            p = jnp.exp2(s - m_new)
            alpha = jnp.exp2(m - m_new)
            l = alpha * l + jnp.sum(p, axis=-1, keepdims=True)
            acc = alpha * acc + jnp.dot(p.astype(BF16), v_ref[pl.ds(off, width), :], preferred_element_type=F32)
            return m_new, l, acc

        carry = (jnp.full((tq, 1), NEG_BIG, F32), jnp.zeros((tq, 1), F32), jnp.zeros((tq, vd), F32))
        carry = lax.fori_loop(0, i // 2, lambda jj, c: absorb(2 * jj, 2 * tq, c), carry)
        carry = lax.cond(i % 2 == 1, lambda c: absorb(i - 1, tq, c), lambda c: c, carry)
        m, l, acc = absorb(i, tq, carry, diagonal=True)
        o_ref[...] = acc / l
        lse_ref[...] = m + jnp.log2(l)
        if job:
            last = (pl.program_id(0) == nh - 1) & (i == nq - 1)
            pl.when(last)(lambda: job.finish(job_in, job_out, sems))

    outs = pl.pallas_call(
        body, name=name, grid=(nh, nq),
        in_specs=[pl.BlockSpec((None, tq, dqk), lambda h, i: (h, i, 0)),
                  pl.BlockSpec((None, tp, dqk), lambda h, i: (h, 0, 0)),
                  pl.BlockSpec((None, tp, vd), lambda h, i: (h, 0, 0))] + [_ANY] * n_ji,
        out_specs=[pl.BlockSpec((None, tq, vd), lambda h, i: (h, i, 0)),
                   pl.BlockSpec((None, tq, 1), lambda h, i: (h, i, 0))] + [_ANY] * n_jo,
        out_shape=[jax.ShapeDtypeStruct((nh, tp, vd), F32), jax.ShapeDtypeStruct((nh, tp, 1), F32)]
        + (list(job.out_shape) if job else []),
        scratch_shapes=list(job.scratch) if job else [],
        compiler_params=_cparams(("arbitrary", "arbitrary") if job else ("parallel", "parallel")),
    )(q, k, v, *(job.inputs if job else ()))
    return outs[0], outs[1], list(outs[2:])


def _attn_bwd(q, k, v, o, do, lse, tq, name, job=None):
    nh, tp, dqk = q.shape
    vd = v.shape[-1]
    scale = dqk ** -0.5
    nq = tp // tq
    n_ji = len(job.inputs) if job else 0
    n_jo = len(job.out_shape) if job else 0

    def body(q_ref, k_ref, v_ref, o_ref, do_ref, lse_ref, *rest):
        job_in, (dq_ref, dk_ref, dv_ref), rest = rest[:n_ji], rest[n_ji:n_ji + 3], rest[n_ji + 3:]
        job_out, (lse_rows, delta_rows, do_bf, dqt_sc), sems = rest[:n_jo], rest[n_jo:n_jo + 4], rest[n_jo + 4:]
        kb_i = pl.program_id(1)
        if job:
            first = (pl.program_id(0) == 0) & (kb_i == 0)
            pl.when(first)(lambda: job.start(job_in, job_out, sems))

        def as_rows(col):
            return jnp.transpose(jnp.broadcast_to(col, (tq, _LANES)))[0:8, :]

        @pl.when(kb_i == 0)
        def _():
            dqt_sc[...] = jnp.zeros_like(dqt_sc)

            def prep(qi, _):
                off = pl.multiple_of(qi * tq, tq)
                dob = do_ref[pl.ds(off, tq), :]
                do_bf[pl.ds(off, tq), :] = dob.astype(BF16)
                delta_rows[qi] = as_rows(jnp.sum(dob * o_ref[pl.ds(off, tq), :], axis=-1, keepdims=True))
                lse_rows[qi] = as_rows(lse_ref[pl.ds(off, tq), :])
                return 0

            lax.fori_loop(0, nq, prep, 0)

        kb = k_ref[...]
        vb = v_ref[...]

        def absorb(qi, n_blk, carry, diagonal=False):
            dk, dv = carry
            off = pl.multiple_of(qi * tq, tq)
            qb = q_ref[pl.ds(off, n_blk * tq), :]
            dob = do_bf[pl.ds(off, n_blk * tq), :]
            lse_row = jnp.concatenate([lse_rows[qi + b, 0:1, :] for b in range(n_blk)], axis=1)
            delta_row = jnp.concatenate([delta_rows[qi + b, 0:1, :] for b in range(n_blk)], axis=1)
            st = _dot_nt(kb, qb)
            if diagonal:
                keep = lax.broadcasted_iota(jnp.int32, (tq, tq), 0) <= lax.broadcasted_iota(jnp.int32, (tq, tq), 1)
                st = jnp.where(keep, st, NEG_BIG)
            pt = jnp.exp2(st - lse_row)
            dpt = _dot_nt(vb, dob)
            dst = (pt * (dpt - delta_row)).astype(BF16)
            dv = dv + jnp.dot(pt.astype(BF16), dob, preferred_element_type=F32)
            dk = dk + jnp.dot(dst, qb, preferred_element_type=F32)
            dqt = _dot_tn(kb, dst)
            for b in range(n_blk):
                dqt_sc[qi + b] += dqt[:, b * tq:(b + 1) * tq]
            return dk, dv

        carry = absorb(kb_i, 1, (jnp.zeros((tq, dqk), F32), jnp.zeros((tq, vd), F32)), diagonal=True)
        later = nq - 1 - kb_i
        carry = lax.fori_loop(0, later // 2, lambda pp, c: absorb(kb_i + 1 + 2 * pp, 2, c), carry)
        dk, dv = lax.cond(later % 2 == 1, lambda c: absorb(nq - 1, 1, c), lambda c: c, carry)
        dk_ref[...] = dk * math.log(2.0)
        dv_ref[...] = dv

        @pl.when(kb_i == nq - 1)
        def _():
            eye = (lax.broadcasted_iota(jnp.int32, (dqk, dqk), 0) == lax.broadcasted_iota(jnp.int32, (dqk, dqk), 1))
            eye = jnp.where(eye, 1.0, 0.0).astype(BF16)
            for qi in range(nq):
                t = dqt_sc[qi]
                hi = t.astype(BF16)
                lo = (t - hi.astype(F32)).astype(BF16)
                dq_ref[pl.ds(qi * tq, tq), :] = scale * (_dot_tn(hi, eye) + _dot_tn(lo, eye))

        if job:
            last = (pl.program_id(0) == nh - 1) & (kb_i == nq - 1)
            pl.when(last)(lambda: job.finish(job_in, job_out, sems))

    def full(n):
        return pl.BlockSpec((None, tp, n), lambda h, j: (h, 0, 0))

    def blk(n):
        return pl.BlockSpec((None, tq, n), lambda h, j: (h, j, 0))

    outs = pl.pallas_call(
        body, name=name, grid=(nh, nq),
        in_specs=[full(dqk), blk(dqk), blk(vd), full(vd), full(vd), full(1)] + [_ANY] * n_ji,
        out_specs=[full(dqk), blk(dqk), blk(vd)] + [_ANY] * n_jo,
        out_shape=[jax.ShapeDtypeStruct((nh, tp, dqk), F32), jax.ShapeDtypeStruct((nh, tp, dqk), F32),
                   jax.ShapeDtypeStruct((nh, tp, vd), F32)] + (list(job.out_shape) if job else []),
        scratch_shapes=[pltpu.VMEM((nq, 8, tq), F32), pltpu.VMEM((nq, 8, tq), F32), pltpu.VMEM((tp, vd), BF16),
                        pltpu.VMEM((nq, dqk, tq), F32)] + (list(job.scratch) if job else []),
        compiler_params=_cparams(("arbitrary", "arbitrary") if job else ("parallel", "arbitrary")),
    )(q, k, v, o, do, lse, *(job.inputs if job else ()))
    return outs[0], outs[1], outs[2], list(outs[3:])


def _mix_out_fwd(h, y_lru, o, g_lo, g_mo, w_lo, w_mo, g_post, layer, tm, name):
    tp, d = h.shape
    lw = y_lru.shape[-1]
    nh, vd = o.shape[0], o.shape[-1]

    def body(h_ref, yl_ref, o_ref, glo_ref, gmo_ref, wlo_ref, wmo_ref, gpost_ref, hout_ref, y_ref):
        nl, _ = _rms(yl_ref[...], glo_ref[...])
        y = _dot(nl, wlo_ref[...])
        ssq = jnp.zeros((tm, 1), F32)
        for hd in range(nh):
            oh = o_ref[hd]
            ssq += jnp.sum(oh * oh, axis=-1, keepdims=True)
        r = lax.rsqrt(ssq / (nh * vd) + EPS)
        for hd in range(nh):
            y += _dot(o_ref[hd] * r * gmo_ref[hd], wmo_ref[hd])
        y_ref[...] = y
        hout_ref[...] = h_ref[...] + _rms(y, gpost_ref[...])[0]

    tok = pl.BlockSpec((tm, d), lambda i: (i, 0))

    def lay(*shape):
        return pl.BlockSpec((None,) + shape, lambda i: (layer,) + (0,) * len(shape))

    return pl.pallas_call(
        body, name=name, grid=(tp // tm,),
        in_specs=[tok, pl.BlockSpec((tm, lw), lambda i: (i, 0)), pl.BlockSpec((nh, tm, vd), lambda i: (0, i, 0)),
                  lay(1, lw), lay(nh, 1, vd), _whole(lw, d), _whole(nh, vd, d), lay(1, d)],
        out_specs=[tok, tok],
        out_shape=[jax.ShapeDtypeStruct((tp, d), F32), jax.ShapeDtypeStruct((tp, d), F32)],
        compiler_params=_cparams(("parallel",)),
    )(h, y_lru, o, g_lo, g_mo, w_lo, w_mo, g_post)


def _mix_out_bwd(dh_out, y, y_lru, o, g_lo, g_mo, w_lo, w_mo, g_post, layer, tm, name):
    tp, d = y.shape
    lw = y_lru.shape[-1]
    nh, vd = o.shape[0], o.shape[-1]

    def body(dho_ref, y_ref, yl_ref, o_ref, glo_ref, gmo_ref, wlo_ref, wmo_ref, gpost_ref,
             dyl_ref, do_ref, dy_ref, nl_ref, nm_ref, dgpost_ref, dglo_ref, dgmo_ref):
        i = pl.program_id(0)

        @pl.when(i == 0)
        def _():
            dgpost_ref[...] = jnp.zeros_like(dgpost_ref)
            dglo_ref[...] = jnp.zeros_like(dglo_ref)
            dgmo_ref[...] = jnp.zeros_like(dgmo_ref)

        dy, dgp = _rms_bwd(y_ref[...], gpost_ref[...], dho_ref[...])
        dgpost_ref[...] += dgp
        dy_b = dy.astype(BF16)
        dy_ref[...] = dy_b
        yl = yl_ref[...]
        nl_ref[...] = _rms(yl, glo_ref[...])[0].astype(BF16)
        dyl, dgl = _rms_bwd(yl, glo_ref[...], _dot_nt(dy_b, wlo_ref[...]))
        dyl_ref[...] = dyl
        dglo_ref[...] += dgl
        ssq = jnp.zeros((tm, 1), F32)
        for hd in range(nh):
            oh = o_ref[hd]
            ssq += jnp.sum(oh * oh, axis=-1, keepdims=True)
        r = lax.rsqrt(ssq / (nh * vd) + EPS)
        dn = []
        dot_sum = jnp.zeros((tm, 1), F32)
        for hd in range(nh):
            oh = o_ref[hd] * r
            nm_ref[hd] = (oh * gmo_ref[hd]).astype(BF16)
            dn_h = _dot_nt(dy_b, wmo_ref[hd])
            dgmo_ref[hd] += jnp.sum(dn_h * oh, axis=0, keepdims=True)
            dn_h = dn_h * gmo_ref[hd]
            dot_sum += jnp.sum(dn_h * oh, axis=-1, keepdims=True)
            dn.append(dn_h)
        dot_mean = dot_sum / (nh * vd)
        for hd in range(nh):
            do_ref[hd] = r * (dn[hd] - o_ref[hd] * r * dot_mean)

    tok = pl.BlockSpec((tm, d), lambda i: (i, 0))
    tokl = pl.BlockSpec((tm, lw), lambda i: (i, 0))
    heads = pl.BlockSpec((nh, tm, vd), lambda i: (0, i, 0))

    def lay(*shape):
        return pl.BlockSpec((None,) + shape, lambda i: (layer,) + (0,) * len(shape))

    def acc(*shape):
        return pl.BlockSpec(shape, lambda i: (0,) * len(shape))

    return pl.pallas_call(
        body, name=name, grid=(tp // tm,),
        in_specs=[tok, tok, tokl, heads, lay(1, lw), lay(nh, 1, vd), _whole(lw, d), _whole(nh, vd, d), lay(1, d)],
        out_specs=[tokl, heads, tok, tokl, heads, acc(1, d), acc(1, lw), acc(nh, 1, vd)],
        out_shape=[jax.ShapeDtypeStruct((tp, lw), F32), jax.ShapeDtypeStruct((nh, tp, vd), F32),
                   jax.ShapeDtypeStruct((tp, d), BF16), jax.ShapeDtypeStruct((tp, lw), BF16),
                   jax.ShapeDtypeStruct((nh, tp, vd), BF16), jax.ShapeDtypeStruct((1, d), F32),
                   jax.ShapeDtypeStruct((1, lw), F32), jax.ShapeDtypeStruct((nh, 1, vd), F32)],
        compiler_params=_cparams(("arbitrary",)),
    )(dh_out, y, y_lru, o, g_lo, g_mo, w_lo, w_mo, g_post)


def _loss_head(h, target, n_real, tm, name):
    tp, d = h.shape
    nt = tp // tm

    def body(h_ref, t_ref, dh_ref, part_ref):
        i = pl.program_id(0)
        row = i * tm + lax.broadcasted_iota(jnp.int32, (tm, 1), 0)
        real = (row >= N_META) & (row < N_META + n_real)
        err = jnp.where(real, h_ref[...] - t_ref[...], 0.0)
        dh_ref[...] = err * (1.0 / d)
        part_ref[...] = jnp.full((1, 1, 128), 0.5 / d * jnp.sum(err * err), F32)

    tok = pl.BlockSpec((tm, d), lambda i: (i, 0))
    return pl.pallas_call(
        body, name=name, grid=(nt,), in_specs=[tok, tok],
        out_specs=[tok, pl.BlockSpec((1, 1, 128), lambda i: (i, 0, 0))],
        out_shape=[jax.ShapeDtypeStruct((tp, d), F32), jax.ShapeDtypeStruct((nt, 1, 128), F32)],
        compiler_params=_cparams(("parallel",)),
    )(h, target)


def _mesh_pos():
    return lax.axis_index("x"), lax.axis_index("y"), lax.axis_index("c")


def _other_chips(x, y):
    return [(1 - x, y), (x, 1 - y), (1 - x, 1 - y)]


class _GatherJob:
    def __init__(self, inputs, picks, shard_shapes, dtypes, split_last=()):
        self.inputs = list(inputs)
        self.picks = list(picks)
        n = len(self.inputs)
        self.out_shape = [jax.ShapeDtypeStruct((N_CHIPS,) + tuple(s), dt) for s, dt in zip(shard_shapes, dtypes)]
        self.split_last = [a in split_last for a in range(n)]
        self.halves = [(s[-1] if last else s[0]) // 2 for s, last in zip(shard_shapes, self.split_last)]
        self.scratch = [pltpu.SemaphoreType.DMA((n, 3)), pltpu.SemaphoreType.DMA((n, 3)), pltpu.SemaphoreType.DMA((n, 3)),
                        pltpu.SemaphoreType.DMA((n, 3)), pltpu.SemaphoreType.DMA((n,))]

    def _copies(self, ins, outs, sems):
        ici_s, ici_r, core_s, core_r, loc = sems
        x, y, c = _mesh_pos()
        me = 2 * x + y
        chips = _other_chips(x, y)
        local, ici, fwd = [], [], []
        for a in range(len(ins)):
            src = self.picks[a](ins[a])
            hv = self.halves[a]

            def half(ref, which, a=a, hv=hv):
                return ref.at[:, pl.ds(which * hv, hv)] if self.split_last[a] else ref.at[pl.ds(which * hv, hv)]

            local.append(pltpu.make_async_copy(src, outs[a].at[me], loc.at[a]))
            for p, (px, py) in enumerate(chips):
                ici.append((pltpu.make_async_remote_copy(src_ref=half(src, c), dst_ref=half(outs[a].at[me], c),
                                                         send_sem=ici_s.at[a, p], recv_sem=ici_r.at[a, p],
                                                         device_id=(px, py, c), device_id_type=MESH),
                            pltpu.make_async_remote_copy(src_ref=half(src, c), dst_ref=half(outs[a].at[2 * px + py], c),
                                                         send_sem=ici_s.at[a, p], recv_sem=ici_r.at[a, p],
                                                         device_id=(px, py, c), device_id_type=MESH)))
                landed = half(outs[a].at[2 * px + py], c)
                theirs = half(outs[a].at[2 * px + py], 1 - c)
                fwd.append((pltpu.make_async_remote_copy(src_ref=landed, dst_ref=landed, send_sem=core_s.at[a, p],
                                                         recv_sem=core_r.at[a, p], device_id=(x, y, 1 - c), device_id_type=MESH),
                            pltpu.make_async_remote_copy(src_ref=theirs, dst_ref=theirs, send_sem=core_s.at[a, p],
                                                         recv_sem=core_r.at[a, p], device_id=(x, y, 1 - c), device_id_type=MESH)))
        return local, ici, fwd

    def start(self, ins, outs, sems):
        local, ici, _ = self._copies(ins, outs, sems)
        for cp in local:
            cp.start()
        for send, _ in ici:
            send.start()

    def finish(self, ins, outs, sems):
        local, ici, fwd = self._copies(ins, outs, sems)
        for (_, arrive), (hand_on, _) in zip(ici, fwd):
            arrive.wait_recv()
            hand_on.start()
        for _, arrive in fwd:
            arrive.wait_recv()
        for send, _ in ici:
            send.wait_send()
        for hand_on, _ in fwd:
            hand_on.wait_send()
        for cp in local:
            cp.wait()


class _ReduceJob:
    def __init__(self, inputs):
        self.inputs = list(inputs)
        n = len(self.inputs)
        self.out_shape = [jax.ShapeDtypeStruct(a.shape, a.dtype) for a in self.inputs] * 2
        self.scratch = [pltpu.SemaphoreType.DMA((n, 3)), pltpu.SemaphoreType.DMA((n, 3)), pltpu.SemaphoreType.DMA((n, 4)),
                        pltpu.SemaphoreType.DMA((n, 4)), pltpu.SemaphoreType.DMA((n,))]

    def _copies(self, ins, outs, sems):
        ici_s, ici_r, core_s, core_r, loc = sems
        n = len(ins)
        same, other = outs[:n], outs[n:]
        x, y, c = _mesh_pos()
        me = 2 * x + y
        chips = _other_chips(x, y)
        sib = (x, y, 1 - c)
        local, ici, fwd = [], [], []
        for a in range(n):
            local.append(pltpu.make_async_copy(ins[a].at[me], same[a].at[me], loc.at[a]))
            fwd.append((pltpu.make_async_remote_copy(src_ref=ins[a].at[me], dst_ref=other[a].at[me], send_sem=core_s.at[a, 3],
                                                     recv_sem=core_r.at[a, 3], device_id=sib, device_id_type=MESH), None))
            for p, (px, py) in enumerate(chips):
                cp = 2 * px + py
                ici.append((pltpu.make_async_remote_copy(src_ref=ins[a].at[cp], dst_ref=same[a].at[me], send_sem=ici_s.at[a, p],
                                                         recv_sem=ici_r.at[a, p], device_id=(px, py, c), device_id_type=MESH),
                            pltpu.make_async_remote_copy(src_ref=ins[a].at[cp], dst_ref=same[a].at[cp], send_sem=ici_s.at[a, p],
                                                         recv_sem=ici_r.at[a, p], device_id=(px, py, c), device_id_type=MESH)))
                fwd.append((pltpu.make_async_remote_copy(src_ref=same[a].at[cp], dst_ref=other[a].at[cp], send_sem=core_s.at[a, p],
                                                         recv_sem=core_r.at[a, p], device_id=sib, device_id_type=MESH), p))
        return local, ici, fwd

    def start(self, ins, outs, sems):
        local, ici, fwd = self._copies(ins, outs, sems)
        for cp in local:
            cp.start()
        for send, _ in ici:
            send.start()
        for hand_on, p in fwd:
            if p is None:
                hand_on.start()

    def finish(self, ins, outs, sems):
        local, ici, fwd = self._copies(ins, outs, sems)
        chip_fwd = [f for f in fwd if f[1] is not None]
        for (_, arrive), (hand_on, _) in zip(ici, chip_fwd):
            arrive.wait_recv()
            hand_on.start()
        for hand_on, _ in fwd:
            hand_on.wait()
        for send, _ in ici:
            send.wait_send()
        for cp in local:
            cp.wait()


def _run_job(job, name):
    n_in, n_out = len(job.inputs), len(job.out_shape)

    def body(*refs):
        ins, outs, sems = refs[:n_in], refs[n_in:n_in + n_out], refs[n_in + n_out:]
        job.start(ins, outs, sems)
        job.finish(ins, outs, sems)

    return list(pl.pallas_call(body, name=name, in_specs=[_ANY] * n_in, out_specs=[_ANY] * n_out, out_shape=list(job.out_shape),
                               scratch_shapes=list(job.scratch))(*job.inputs))


def _allreduce_small(buf, name):
    rows, lanes = buf.shape
    n_dev = 8

    def body(in_ref, out_ref, gather, send_sems, recv_sems):
        x, y, c = _mesh_pos()
        me = 4 * x + 2 * y + c
        gather[me] = in_ref[...]
        sends = []
        for rel in range(1, n_dev):
            fx, fy, fc = (rel >> 2) & 1, (rel >> 1) & 1, rel & 1
            peer = (1 - x if fx else x, 1 - y if fy else y, 1 - c if fc else c)
            rc = pltpu.make_async_remote_copy(src_ref=in_ref, dst_ref=gather.at[me], send_sem=send_sems.at[rel - 1],
                                              recv_sem=recv_sems.at[rel - 1], device_id=peer, device_id_type=MESH)
            rc.start()
            sends.append(rc)
        for rc in sends:
            rc.wait()
        total = gather[0]
        for k in range(1, n_dev):
            total = total + gather[k]
        out_ref[...] = total

    vm = pl.BlockSpec(memory_space=pltpu.VMEM)
    return pl.pallas_call(
        body, name=name, in_specs=[vm], out_specs=vm, out_shape=jax.ShapeDtypeStruct((rows, lanes), F32),
        scratch_shapes=[pltpu.VMEM((n_dev, rows, lanes), F32), pltpu.SemaphoreType.DMA((n_dev - 1,)),
                        pltpu.SemaphoreType.DMA((n_dev - 1,))],
        compiler_params=pltpu.CompilerParams(vmem_limit_bytes=V7X_VMEM_LIMIT_BYTES),
    )(buf)


def _row_tile(rows, limit=512):
    best = None
    for t in range(16, min(rows, limit) + 1, 16):
        if rows % t == 0:
            best = t
    return best if best is not None else rows


def _adamw_math(w, g, m, v):
    m = ADAM_B1 * m + (1.0 - ADAM_B1) * g
    v = ADAM_B2 * v + (1.0 - ADAM_B2) * (g * g)
    m_hat = m / (1.0 - ADAM_B1 ** ADAM_STEP)
    v_hat = v / (1.0 - ADAM_B2 ** ADAM_STEP)
    delta = -ADAM_LR * (m_hat / (jnp.sqrt(v_hat) + ADAM_EPS) + ADAM_WD * w)
    return delta, m, v


def _adamw_layer(w, m, v, same_plane, other_plane, layer, prev, name):
    nl, r, ccol = w.shape
    ns = same_plane.shape[0]
    tr = _row_tile(r, 256)

    def body(w_ref, m_ref, v_ref, a_ref, b_ref, *rest):
        go_ref, d_ref, mo_ref, vo_ref = rest[-4:]
        sa = a_ref[0].astype(F32)
        sb = b_ref[0].astype(F32)
        for k in range(1, ns):
            sa = sa + a_ref[k].astype(F32)
            sb = sb + b_ref[k].astype(F32)
        g = sa + sb
        go_ref[...] = g
        d_ref[...], mo_ref[...], vo_ref[...] = _adamw_math(w_ref[...], g, m_ref[...], v_ref[...])

    blk = pl.BlockSpec((None, tr, ccol), lambda i: (layer, i, 0))
    plane = pl.BlockSpec((ns, tr, ccol), lambda i: (0, i, 0))
    out = jax.ShapeDtypeStruct((nl, r, ccol), F32)
    n_prev = 4 if prev is not None else 0
    return pl.pallas_call(
        body, name=name, grid=(r // tr,),
        in_specs=[blk, blk, blk, plane, plane] + [_ANY] * n_prev,
        out_specs=[blk, blk, blk, blk], out_shape=[out, out, out, out],
        input_output_aliases={5 + k: k for k in range(n_prev)},
        compiler_params=_cparams(("parallel",)))(w, m, v, same_plane, other_plane, *(prev or ()))


def _adamw_whole(w, g, m, v, name):
    def body(w_ref, g_ref, m_ref, v_ref, d_ref, mo_ref, vo_ref):
        d_ref[...], mo_ref[...], vo_ref[...] = _adamw_math(w_ref[...], g_ref[...], m_ref[...], v_ref[...])

    out = jax.ShapeDtypeStruct(w.shape, F32)
    return pl.pallas_call(body, name=name, out_shape=[out, out, out],
                          compiler_params=pltpu.CompilerParams(vmem_limit_bytes=V7X_VMEM_LIMIT_BYTES))(w, g, m, v)


_W_NAMES = ['meta_tokens', 'ffn1_pre_g', 'ffn1_w_gate', 'ffn1_w_up', 'ffn1_w_down', 'ffn1_post_g', 'mix_pre_g', 'w_in',
            'lru_conv_w', 'lru_conv_b', 'lru_w_a', 'lru_b_a', 'lru_w_x', 'lru_b_x', 'lru_lambda', 'mla_q_norm_g', 'mla_w_uq',
            'mla_kv_norm_g', 'mla_w_ukv', 'lru_out_g', 'mla_out_g', 'w_out', 'mix_post_g', 'ffn2_pre_g', 'ffn2_w_gate',
            'ffn2_w_up', 'ffn2_w_down', 'ffn2_post_g']
_REPLICATED = ['ffn1_pre_g', 'ffn1_post_g', 'mix_pre_g', 'lru_conv_b', 'lru_w_a', 'lru_b_a', 'lru_w_x', 'lru_b_x', 'lru_lambda',
               'mla_q_norm_g', 'mla_kv_norm_g', 'lru_out_g', 'mla_out_g', 'mix_post_g', 'ffn2_pre_g', 'ffn2_post_g']
_LANES = 128


def _pack_rows(arrays):
    flat = jnp.concatenate([a.reshape(-1) for a in arrays])
    total = _round_up(flat.shape[0], 8 * _LANES)
    return jnp.pad(flat, (0, total - flat.shape[0])).reshape(-1, _LANES)


def _unpack_rows(buf, shapes):
    flat = buf.reshape(-1)
    out, off = [], 0
    for shp in shapes:
        n = math.prod(shp)
        out.append(flat[off:off + n].reshape(shp))
        off += n
    return out


def _block_diag(w):
    nl, nh, n, _ = w.shape
    eye = jnp.eye(nh, dtype=w.dtype)
    return (w[:, :, :, None, :] * eye[None, :, None, :, None]).reshape(nl, nh * n, nh * n)


def _diag_blocks(bd, nh):
    n = bd.shape[0] // nh
    b4 = bd.reshape(nh, n, nh, n)
    return jnp.stack([b4[i, :, i, :] for i in range(nh)])


def kernel(x, meta_tokens, ffn1_pre_g, ffn1_w_gate, ffn1_w_up, ffn1_w_down, ffn1_post_g, mix_pre_g, w_in, lru_conv_w, lru_conv_b, lru_w_a, lru_b_a, lru_w_x, lru_b_x, lru_lambda, mla_q_norm_g, mla_w_uq, mla_kv_norm_g, mla_w_ukv, lru_out_g, mla_out_g, w_out, mix_post_g, ffn2_pre_g, ffn2_w_gate, ffn2_w_up, ffn2_w_down, ffn2_post_g, loss_target, m_meta_tokens, m_ffn1_pre_g, m_ffn1_w_gate, m_ffn1_w_up, m_ffn1_w_down, m_ffn1_post_g, m_mix_pre_g, m_w_in, m_lru_conv_w, m_lru_conv_b, m_lru_w_a, m_lru_b_a, m_lru_w_x, m_lru_b_x, m_lru_lambda, m_mla_q_norm_g, m_mla_w_uq, m_mla_kv_norm_g, m_mla_w_ukv, m_lru_out_g, m_mla_out_g, m_w_out, m_mix_post_g, m_ffn2_pre_g, m_ffn2_w_gate, m_ffn2_w_up, m_ffn2_w_down, m_ffn2_post_g, v_meta_tokens, v_ffn1_pre_g, v_ffn1_w_gate, v_ffn1_w_up, v_ffn1_w_down, v_ffn1_post_g, v_mix_pre_g, v_w_in, v_lru_conv_w, v_lru_conv_b, v_lru_w_a, v_lru_b_a, v_lru_w_x, v_lru_b_x, v_lru_lambda, v_mla_q_norm_g, v_mla_w_uq, v_mla_kv_norm_g, v_mla_w_ukv, v_lru_out_g, v_mla_out_g, v_w_out, v_mix_post_g, v_ffn2_pre_g, v_ffn2_w_gate, v_ffn2_w_up, v_ffn2_w_down, v_ffn2_post_g):
    a = dict(locals())
    x2 = x[0]
    s_len, d = x2.shape
    nl = ffn1_pre_g.shape[0]
    lw = lru_conv_b.shape[-1]
    ql, kvl = mla_q_norm_g.shape[-1], mla_kv_norm_g.shape[-1]
    nh = MLA_HEADS
    vd = (d - lw) // nh
    t_real = N_META + s_len
    tp = _round_up(t_real, 384)
    tm = tp // 6
    tmb = tm // 2
    tm_ffn_fwd = tp // 4
    tm_ffn_bwd = tp // 8
    tq = 384 if tp >= 1536 else 128
    xi, yi, _ = _mesh_pos()
    chip = 2 * xi + yi

    def tr(w):
        return jnp.swapaxes(w, 1, 2)

    ffn_loc = jnp.stack([tr(ffn1_w_gate), tr(ffn1_w_up), ffn1_w_down, tr(ffn2_w_gate), tr(ffn2_w_up), ffn2_w_down],
                        axis=1).astype(BF16)
    loc = [ffn_loc, tr(w_in).astype(BF16), tr(mla_w_uq).astype(BF16), mla_w_ukv.astype(BF16), w_out.astype(BF16)]
    loc_shapes, loc_dtypes = [t.shape[1:] for t in loc], [t.dtype for t in loc]

    def layer_gather(l):
        return _GatherJob(loc, [lambda r, l=l: r.at[l]] * len(loc), loc_shapes, loc_dtypes, split_last=(1,))

    first = _GatherJob(loc + [lru_conv_w, meta_tokens], [lambda r: r.at[0]] * len(loc) + [lambda r: r] * 2,
                       loc_shapes + [lru_conv_w.shape, meta_tokens.shape], loc_dtypes + [F32, F32], split_last=(1,))
    *gathered, g_conv, g_meta = _run_job(first, "gather_layer0")
    o0, o1, o2, o3 = lw, 2 * lw, 2 * lw + ql, 2 * lw + ql + kvl

    def assemble(gw):
        w_ffn, g_win, g_wuq, g_wukv, g_wout = gw
        w_in_t = g_win.reshape(-1, d)
        w_kr = jnp.pad(w_in_t[o3:], ((QK_NOPE, 0), (0, 0)))
        w_ukv3 = jnp.transpose(g_wukv, (1, 0, 2)).reshape(kvl, nh, QK_NOPE + vd)
        w_out_full = g_wout.reshape(d, d)
        return dict(ffn=w_ffn, parts=(w_in_t[:o0], w_in_t[o0:o1], w_in_t[o1:o2], w_in_t[o2:o3], w_kr),
                    q=g_wuq.reshape(nh, QK_DIM, ql),
                    k=jnp.transpose(jnp.pad(w_ukv3[..., :QK_NOPE], ((0, 0), (0, 0), (0, QK_ROPE))), (1, 0, 2)),
                    v=jnp.transpose(w_ukv3[..., QK_NOPE:], (1, 0, 2)),
                    lo=w_out_full[:lw], mo=w_out_full[lw:].reshape(nh, vd, d))

    conv_full = jnp.transpose(g_conv, (1, 2, 0, 3)).reshape(nl, CONV_WIDTH, lw)
    meta_full = jnp.transpose(g_meta, (1, 0, 2)).reshape(N_META, d)
    wa_bd = _block_diag(lru_w_a).astype(BF16)
    wx_bd = _block_diag(lru_w_x).astype(BF16)

    def gain3(g):
        return g.reshape(nl, 1, g.shape[-1])

    g3 = {n: gain3(a[n]) for n in ('ffn1_pre_g', 'ffn1_post_g', 'mix_pre_g', 'lru_conv_b', 'lru_b_a', 'lru_b_x', 'lru_lambda',
                                   'mla_q_norm_g', 'mla_kv_norm_g', 'lru_out_g', 'mix_post_g', 'ffn2_pre_g', 'ffn2_post_g')}
    g_mo = mla_out_g.reshape(nl, nh, 1, vd)

    pos = jnp.arange(tp, dtype=F32)
    inv_freq = 1.0 / (ROPE_THETA ** (jnp.arange(0, QK_ROPE, 2, dtype=F32) / QK_ROPE))
    ang = pos[:, None] * inv_freq[None, :]
    cos_t = jnp.concatenate([jnp.ones((tp, QK_NOPE), F32), jnp.cos(ang), jnp.cos(ang)], axis=1)
    sin_t = jnp.concatenate([jnp.zeros((tp, QK_NOPE), F32), jnp.sin(ang), jnp.sin(ang)], axis=1)
    half = QK_ROPE // 2
    rot = np.zeros((QK_DIM, QK_DIM), np.float32)
    for i in range(half):
        rot[QK_NOPE + half + i, QK_NOPE + i] = -1.0
        rot[QK_NOPE + i, QK_NOPE + half + i] = 1.0
    rot_b, rot_t = jnp.asarray(rot, BF16), jnp.asarray(rot.T, BF16)

    h = jnp.concatenate([meta_full, x2, jnp.zeros((tp - t_real, d), F32)], axis=0)
    target = jnp.pad(loss_target[0], ((N_META, tp - t_real), (0, 0)))
    saved, weights = [], []
    for l in range(nl):
        wl = assemble(gathered)
        weights.append(wl)
        h1, gate1, up1, f1 = _ffn_fwd(h, g3['ffn1_pre_g'], g3['ffn1_post_g'], wl['ffn'], l, 0, 1, 2, tm_ffn_fwd,
                                      f"ffn1_fwd_{l}")
        xr, gr, cq, ckv, q, k, v = _mix_in_fwd(h1, g3['mix_pre_g'], wl['parts'], g3['mla_q_norm_g'], g3['mla_kv_norm_g'],
                                               wl['q'], wl['k'], wl['v'], cos_t, sin_t, rot_b, l, tm, f"mix_in_fwd_{l}")
        y_lru, xc, hs = _lru_fwd(xr, gr, conv_full, g3['lru_conv_b'], wa_bd, wx_bd, g3['lru_b_a'], g3['lru_b_x'],
                                 g3['lru_lambda'], l, tm, f"lru_fwd_{l}")
        o, lse, gathered = _attn_fwd(q, k, v, tq, f"attn_fwd_{l}", layer_gather(l + 1) if l + 1 < nl else None)
        h2, y = _mix_out_fwd(h1, y_lru, o, g3['lru_out_g'], g_mo, wl['lo'], wl['mo'], g3['mix_post_g'], l, tm, f"mix_out_fwd_{l}")
        h3, gate2, up2, f2 = _ffn_fwd(h2, g3['ffn2_pre_g'], g3['ffn2_post_g'], wl['ffn'], l, 3, 4, 5, tm_ffn_fwd,
                                      f"ffn2_fwd_{l}")
        saved.append((h, gate1, up1, f1, h1, xr, gr, cq, ckv, q, k, v, y_lru, xc, hs, o, lse, y, h2, gate2, up2, f2))
        h = h3

    dh, loss_parts = _loss_head(h, target, s_len, tm, "loss_head")
    loss = lax.psum(jnp.sum(loss_parts[:, 0, 0]), ("x", "y", "c"))

    small = {n: [None] * nl for n in _REPLICATED + ['lru_conv_w']}
    planes = {}
    late_names = ['w_in', 'mla_w_uq', 'mla_w_ukv', 'ffn1_w_gate', 'ffn1_w_up', 'ffn1_w_down']
    early_names = ['ffn2_w_gate', 'ffn2_w_up', 'ffn2_w_down', 'w_out']
    late = None

    def keep_planes(keys, outs):
        for i, key in enumerate(keys):
            planes[key] = (outs[i], outs[len(keys) + i])

    for l in reversed(range(nl)):
        (h0, gate1, up1, f1, h1, xr, gr, cq, ckv, q, k, v, y_lru, xc, hs, o, lse, y, h2, gate2, up2, f2) = saved[l]
        wl = weights[l]
        dh, dgate, dup, act, df, u, dgpre, dgpost = _ffn_bwd(dh, h2, f2, gate2, up2, g3['ffn2_pre_g'], g3['ffn2_post_g'],
                                                            wl['ffn'], l, 3, 4, 5, tm_ffn_bwd, f"ffn2_bwd_{l}")
        small['ffn2_pre_g'][l], small['ffn2_post_g'][l] = dgpre, dgpost
        d_wg2 = _mm_tn(dgate, u, tm, BF16, f"dw_gate2_{l}")
        d_wu2 = _mm_tn(dup, u, tm, BF16, f"dw_up2_{l}")
        dn2 = _mm_tn(act, df, tm, BF16, f"dw_down2_{l}")

        d_ylru, d_o, dy_b, n_lo, n_mo, dgpost, dglo, dgmo = _mix_out_bwd(dh, y, y_lru, o, g3['lru_out_g'], g_mo, wl['lo'], wl['mo'],
                                                                       g3['mix_post_g'], l, tm, f"mix_out_bwd_{l}")
        small['mix_post_g'][l], small['lru_out_g'][l], small['mla_out_g'][l] = dgpost, dglo, dgmo
        d_wlo = _mm_tn(n_lo, dy_b, tm, F32, f"dw_out_lru_{l}")
        d_wmo = _mm_tn_heads(n_mo, dy_b, tm, f"dw_out_mla_{l}")
        d_wout = jnp.concatenate([d_wlo, d_wmo.reshape(nh * vd, d)], axis=0).reshape(N_CHIPS, d // N_CHIPS, d).astype(BF16)

        keys = [(n, l) for n in early_names] + ([(n, l + 1) for n in late_names] if late else [])
        dq, dk, dv, outs = _attn_bwd(q, k, v, o, d_o, lse, tq, f"attn_bwd_{l}",
                                     _ReduceJob([d_wg2, d_wu2, dn2, d_wout] + (late or [])))
        keep_planes(keys, outs)
        d_xr, d_gr, dcw, dcb, dwa, dwx, dba, dbx, dlam = _lru_bwd(d_ylru, xr, gr, xc, hs, conv_full, wa_bd, wx_bd, g3['lru_b_a'],
                                                                 g3['lru_b_x'], g3['lru_lambda'], l, tm, f"lru_bwd_{l}")
        small['lru_conv_w'][l], small['lru_conv_b'][l] = dcw, dcb
        small['lru_w_a'][l], small['lru_w_x'][l] = _diag_blocks(dwa, LRU_HEADS), _diag_blocks(dwx, LRU_HEADS)
        small['lru_b_a'][l], small['lru_b_x'][l], small['lru_lambda'][l] = dba, dbx, dlam

        dh, u, qn, kvn, dq_pre, d_cq, d_ckv, d_kr, dgpre, dgq, dgkv = _mix_in_bwd(
            dh, h1, cq, ckv, d_xr, d_gr, dq, dk, dv, g3['mix_pre_g'], wl['parts'], g3['mla_q_norm_g'], g3['mla_kv_norm_g'],
            wl['q'], wl['k'], wl['v'], cos_t, sin_t, rot_t, l, tmb, f"mix_in_bwd_{l}")
        small['mix_pre_g'][l], small['mla_q_norm_g'][l], small['mla_kv_norm_g'][l] = dgpre, dgq, dgkv
        d_win = jnp.concatenate([_mm_tn(d_xr, u, tm, F32, f"dw_in_xr_{l}"), _mm_tn(d_gr, u, tm, F32, f"dw_in_gr_{l}"),
                                 _mm_tn(d_cq, u, tm, F32, f"dw_in_cq_{l}"), _mm_tn(d_ckv, u, tm, F32, f"dw_in_ckv_{l}"),
                                 _mm_tn(d_kr, u, tm, F32, f"dw_in_kr_{l}")[QK_NOPE:]], axis=0)
        d_win = d_win.reshape(N_CHIPS, -1, d).astype(BF16)
        d_wuq = _mm_tn_heads(dq_pre, qn, tm, f"dw_uq_{l}").reshape(N_CHIPS, -1, ql).astype(BF16)
        d_wk = _mm_tn_heads(dk, kvn, tm, f"dw_uk_{l}")[:, :QK_NOPE]
        d_wv = _mm_tn_heads(dv, kvn, tm, f"dw_uv_{l}")
        d_wukv = jnp.transpose(jnp.concatenate([d_wk, d_wv], axis=1).reshape(N_CHIPS, -1, kvl), (0, 2, 1)).astype(BF16)

        dh, dgate, dup, act, df, u, dgpre, dgpost = _ffn_bwd(dh, h0, f1, gate1, up1, g3['ffn1_pre_g'], g3['ffn1_post_g'],
                                                            wl['ffn'], l, 0, 1, 2, tm_ffn_bwd, f"ffn1_bwd_{l}")
        small['ffn1_pre_g'][l], small['ffn1_post_g'][l] = dgpre, dgpost
        d_wg1 = _mm_tn(dgate, u, tm, BF16, f"dw_gate1_{l}")
        d_wu1 = _mm_tn(dup, u, tm, BF16, f"dw_up1_{l}")
        dn1 = _mm_tn(act, df, tm, BF16, f"dw_down1_{l}")
        late = [d_win, d_wuq, d_wukv, d_wg1, d_wu1, dn1]

    grad_x = dh[N_META:t_real][None]
    keep_planes([(n, 0) for n in late_names], _run_job(_ReduceJob(late), "reduce_last"))

    res = {}
    stored_transposed = ('ffn1_w_gate', 'ffn1_w_up', 'ffn2_w_gate', 'ffn2_w_up', 'w_in', 'mla_w_uq')
    for name in early_names + late_names:
        view = tr if name in stored_transposed else (lambda t: t)
        w_v, m_v, v_v = view(a[name]), view(a['m_' + name]), view(a['v_' + name])
        prev = None
        for l in reversed(range(nl)):
            same_plane, other_plane = planes[(name, l)]
            prev = _adamw_layer(w_v, m_v, v_v, same_plane, other_plane, l, prev, f"adamw_{name}_{l}")
        res[name] = [view(t) for t in prev]

    small_full = {n: jnp.stack(small[n]).reshape(a[n].shape if n != 'lru_conv_w' else conv_full.shape)
                  for n in _REPLICATED + ['lru_conv_w']}
    order = _REPLICATED + ['lru_conv_w']
    packed = _pack_rows([small_full[n] for n in order] + [dh[:N_META]])
    summed = _allreduce_small(packed, "allreduce_small")
    pieces = _unpack_rows(summed, [small_full[n].shape for n in order] + [(N_META, d)])
    g_small = dict(zip(order, pieces[:-1]))
    g_conv_loc = lax.dynamic_slice_in_dim(g_small.pop('lru_conv_w'), chip * lru_conv_w.shape[-1], lru_conv_w.shape[-1], axis=2)
    g_meta_loc = lax.dynamic_slice_in_dim(pieces[-1], chip * meta_tokens.shape[-1], meta_tokens.shape[-1], axis=1)

    shapes = [a[n].shape for n in _REPLICATED]
    dl, mo, vo = _adamw_whole(_pack_rows([a[n] for n in _REPLICATED]), _pack_rows([g_small[n] for n in _REPLICATED]),
                              _pack_rows([a['m_' + n] for n in _REPLICATED]), _pack_rows([a['v_' + n] for n in _REPLICATED]),
                              "adamw_replicated")
    for n, dd, mm, vv in zip(_REPLICATED, _unpack_rows(dl, shapes), _unpack_rows(mo, shapes), _unpack_rows(vo, shapes)):
        res[n] = (g_small[n], dd, mm, vv)
    cshape = lru_conv_w.shape
    c2 = (cshape[0] * cshape[1], cshape[2])
    dd, mm, vv = _adamw_whole(lru_conv_w.reshape(c2), g_conv_loc.reshape(c2), m_lru_conv_w.reshape(c2), v_lru_conv_w.reshape(c2),
                              "adamw_conv_w")
    res['lru_conv_w'] = (g_conv_loc, dd.reshape(cshape), mm.reshape(cshape), vv.reshape(cshape))
    res['meta_tokens'] = (g_meta_loc,) + tuple(_adamw_whole(meta_tokens, g_meta_loc, m_meta_tokens, v_meta_tokens, "adamw_meta"))

    return (loss, grad_x, *[res[n][0] for n in _W_NAMES], *[res[n][1] for n in _W_NAMES],
            *[res[n][2] for n in _W_NAMES], *[res[n][3] for n in _W_NAMES])
```

```python
import functools
import math

import jax
import jax.numpy as jnp
import numpy as np
from jax import lax
from jax.experimental import pallas as pl
from jax.experimental.pallas import tpu as pltpu

F32 = jnp.float32
BF16 = jnp.bfloat16
MESH = pl.DeviceIdType.MESH

EPS = 1e-6
N_META = 16
LRU_HEADS = 8
MLA_HEADS = 8
QK_NOPE = 64
QK_ROPE = 32
QK_DIM = QK_NOPE + QK_ROPE
LRU_C = 8.0
ROPE_THETA = 10000.0
CONV_WIDTH = 4
N_CHIPS = 4

ADAM_LR = 0.001
ADAM_B1 = 0.9
ADAM_B2 = 0.999
ADAM_EPS = 1e-08
ADAM_WD = 0.01
ADAM_STEP = 10

V7X_VMEM_LIMIT_BYTES = 56 * 1024 * 1024
NEG_BIG = -1e30
Q_PRESCALE = QK_DIM ** -0.5 * math.log2(math.e)


def _cparams(sem=None):
    return pltpu.CompilerParams(dimension_semantics=sem, vmem_limit_bytes=V7X_VMEM_LIMIT_BYTES)


def _round_up(a, b):
    return -(-a // b) * b


def _whole(*shape):
    return pl.BlockSpec(shape, lambda *_: (0,) * len(shape))


_ANY = pl.BlockSpec(memory_space=pl.ANY)


def _dot(a, b):
    return jnp.dot(a.astype(BF16), b.astype(BF16), preferred_element_type=F32)


def _dot_nt(a, b):
    return lax.dot_general(a.astype(BF16), b.astype(BF16), (((1,), (1,)), ((), ())), preferred_element_type=F32)


def _dot_tn(a, b):
    return lax.dot_general(a.astype(BF16), b.astype(BF16), (((0,), (0,)), ((), ())), preferred_element_type=F32)


def _dot_split(x, p):
    hi = x.astype(BF16)
    lo = (x - hi.astype(F32)).astype(BF16)
    return jnp.dot(hi, p, preferred_element_type=F32) + jnp.dot(lo, p, preferred_element_type=F32)


def _rms(x, g):
    r = lax.rsqrt(jnp.mean(x * x, axis=-1, keepdims=True) + EPS)
    return x * r * g, r


def _rms_bwd(x, g, dy):
    r = lax.rsqrt(jnp.mean(x * x, axis=-1, keepdims=True) + EPS)
    xh = x * r
    dyg = dy * g
    dx = r * (dyg - xh * jnp.mean(dyg * xh, axis=-1, keepdims=True))
    dg = jnp.sum(dy * xh, axis=0, keepdims=True)
    return dx, dg


def _sigmoid(x):
    return 1.0 / (1.0 + jnp.exp(-x))


def _neg_expm1(x):
    series = -x * (1.0 + x * (0.5 + x * (1.0 / 6.0 + x * (1.0 / 24.0))))
    return jnp.where(jnp.abs(x) < 0.03, series, 1.0 - jnp.exp(x))


_GELU_K = math.sqrt(2.0 / math.pi)


def _gelu(x):
    t = jnp.tanh(_GELU_K * (x + 0.044715 * x * x * x))
    return 0.5 * x * (1.0 + t), t


def _gelu_grad(x, t):
    return 0.5 * (1.0 + t) + 0.5 * x * (1.0 - t * t) * _GELU_K * (1.0 + 3.0 * 0.044715 * x * x)


def _rope(x, cos_t, sin_t, rot):
    return x * cos_t + _dot_split(x, rot) * sin_t


def _ffn_fwd(h, g_pre, g_post, w_ffn, layer, kind_gate, kind_up, kind_dn, tm, name, job=None):
    tp, d = h.shape
    n_slot, fs = w_ffn.shape[0], w_ffn.shape[-2]
    n_tok = tp // tm
    n_ji = len(job.inputs) if job else 0
    n_jo = len(job.out_shape) if job else 0

    def body(h_ref, gpre_ref, gpost_ref, wg_ref, wu_ref, wd_ref, *rest):
        job_in, (hout_ref, gate_ref, up_ref, f_ref), rest = rest[:n_ji], rest[n_ji:n_ji + 4], rest[n_ji + 4:]
        job_out, (u_sc, acc_sc), sems = rest[:n_jo], rest[n_jo:n_jo + 2], rest[n_jo + 2:]
        j = pl.program_id(1)
        if job:
            pl.when((pl.program_id(0) == 0) & (j == 0))(lambda: job.start(job_in, job_out, sems))

        @pl.when(j == 0)
        def _():
            u, _ = _rms(h_ref[...], gpre_ref[...])
            u_sc[...] = u.astype(BF16)
            acc_sc[...] = jnp.zeros_like(acc_sc)

        u = u_sc[...]
        gate = _dot_nt(u, wg_ref[...])
        up = _dot_nt(u, wu_ref[...])
        act = gate * _sigmoid(gate) * up
        gate_ref[...] = gate.astype(BF16)
        up_ref[...] = up.astype(BF16)
        acc_sc[...] += jnp.dot(act.astype(BF16), wd_ref[...], preferred_element_type=F32)

        @pl.when(j == n_slot - 1)
        def _():
            f = acc_sc[...]
            f_ref[...] = f
            n, _ = _rms(f, gpost_ref[...])
            hout_ref[...] = h_ref[...] + 0.5 * n

        if job:
            pl.when((pl.program_id(0) == n_tok - 1) & (j == n_slot - 1))(lambda: job.finish(job_in, job_out, sems))

    tok = pl.BlockSpec((tm, d), lambda i, j: (i, 0))
    gain = pl.BlockSpec((None, 1, d), lambda i, j: (layer, 0, 0))
    slot_act = pl.BlockSpec((None, tm, fs), lambda i, j: (j, i, 0))
    outs = pl.pallas_call(
        body, name=name, grid=(n_tok, n_slot),
        in_specs=[tok, gain, gain] + [pl.BlockSpec((None, None, fs, d), lambda i, j, kind=kind: (j, kind, 0, 0))
                                      for kind in (kind_gate, kind_up, kind_dn)] + [_ANY] * n_ji,
        out_specs=[tok, slot_act, slot_act, tok] + [_ANY] * n_jo,
        out_shape=[jax.ShapeDtypeStruct((tp, d), F32), jax.ShapeDtypeStruct((n_slot, tp, fs), BF16),
                   jax.ShapeDtypeStruct((n_slot, tp, fs), BF16), jax.ShapeDtypeStruct((tp, d), F32)]
        + (list(job.out_shape) if job else []),
        scratch_shapes=[pltpu.VMEM((tm, d), BF16), pltpu.VMEM((tm, d), F32)] + (list(job.scratch) if job else []),
        compiler_params=_cparams(("arbitrary", "arbitrary") if job else ("parallel", "arbitrary")),
    )(h, g_pre, g_post, w_ffn, w_ffn, w_ffn, *(job.inputs if job else ()))
    return outs[0], outs[1], outs[2], outs[3], list(outs[4:])


def _ffn_bwd(dh_out, h, f, gate, up, g_pre, g_post, w_ffn, layer, kind_gate, kind_up, kind_dn, tm, name):
    tp, d = h.shape
    n_slot, fs = w_ffn.shape[0], w_ffn.shape[-2]

    def body(dho_ref, h_ref, f_ref, gate_ref, up_ref, gpre_ref, gpost_ref, wg_ref, wu_ref, wd_ref,
             dh_ref, dgate_ref, dup_ref, act_ref, df_ref, u_ref, dgpre_ref, dgpost_ref, du_sc):
        i, j = pl.program_id(0), pl.program_id(1)

        @pl.when((i == 0) & (j == 0))
        def _():
            dgpre_ref[...] = jnp.zeros_like(dgpre_ref)
            dgpost_ref[...] = jnp.zeros_like(dgpost_ref)

        @pl.when(j == 0)
        def _():
            df, dg = _rms_bwd(f_ref[...], gpost_ref[...], 0.5 * dho_ref[...])
            df_ref[...] = df.astype(BF16)
            dgpost_ref[...] += dg
            u, _ = _rms(h_ref[...], gpre_ref[...])
            u_ref[...] = u.astype(BF16)
            du_sc[...] = jnp.zeros_like(du_sc)

        g = gate_ref[...].astype(F32)
        u_ = up_ref[...].astype(F32)
        sg = _sigmoid(g)
        silu = g * sg
        dact = _dot_nt(df_ref[...], wd_ref[...])
        dup = dact * silu
        dgate = dact * u_ * (sg * (1.0 + g * (1.0 - sg)))
        act_ref[...] = (silu * u_).astype(BF16)
        dup_b = dup.astype(BF16)
        dgate_b = dgate.astype(BF16)
        dup_ref[...] = dup_b
        dgate_ref[...] = dgate_b
        du_sc[...] += _dot(dgate_b, wg_ref[...]) + _dot(dup_b, wu_ref[...])

        @pl.when(j == n_slot - 1)
        def _():
            dx, dg = _rms_bwd(h_ref[...], gpre_ref[...], du_sc[...])
            dh_ref[...] = dho_ref[...] + dx
            dgpre_ref[...] += dg

    tok = pl.BlockSpec((tm, d), lambda i, j: (i, 0))
    gain = pl.BlockSpec((None, 1, d), lambda i, j: (layer, 0, 0))
    acc = pl.BlockSpec((1, d), lambda i, j: (0, 0))
    slot_act = pl.BlockSpec((None, tm, fs), lambda i, j: (j, i, 0))
    act_shape = jax.ShapeDtypeStruct((n_slot, tp, fs), BF16)
    return pl.pallas_call(
        body, name=name, grid=(tp // tm, n_slot),
        in_specs=[tok, tok, tok, slot_act, slot_act, gain, gain]
        + [pl.BlockSpec((None, None, fs, d), lambda i, j, kind=kind: (j, kind, 0, 0)) for kind in (kind_gate, kind_up, kind_dn)],
        out_specs=[tok, slot_act, slot_act, slot_act, tok, tok, acc, acc],
        out_shape=[jax.ShapeDtypeStruct((tp, d), F32), act_shape, act_shape, act_shape,
                   jax.ShapeDtypeStruct((tp, d), BF16), jax.ShapeDtypeStruct((tp, d), BF16),
                   jax.ShapeDtypeStruct((1, d), F32), jax.ShapeDtypeStruct((1, d), F32)],
        scratch_shapes=[pltpu.VMEM((tm, d), F32)],
        compiler_params=_cparams(("arbitrary", "arbitrary")),
    )(dh_out, h, f, gate, up, g_pre, g_post, w_ffn, w_ffn, w_ffn)


def _mm_tn(a, b, tk, out_dtype, name):
    ga = a.shape[0] if a.ndim == 3 else None
    gb = b.shape[0] if b.ndim == 3 else None
    groups = ga or gb or 1
    t, m = a.shape[-2:]
    n = b.shape[-1]
    nk = t // tk

    def body(a_ref, b_ref, o_ref, acc_sc):
        k = pl.program_id(1)

        @pl.when(k == 0)
        def _():
            acc_sc[...] = jnp.zeros_like(acc_sc)

        acc_sc[...] += _dot_tn(a_ref[...], b_ref[...])

        @pl.when(k == nk - 1)
        def _():
            o_ref[...] = acc_sc[...].astype(out_dtype)

    a_spec = (pl.BlockSpec((None, tk, m), lambda g, k: (g, k, 0)) if ga else pl.BlockSpec((tk, m), lambda g, k: (k, 0)))
    b_spec = (pl.BlockSpec((None, tk, n), lambda g, k: (g, k, 0)) if gb else pl.BlockSpec((tk, n), lambda g, k: (k, 0)))
    out = pl.pallas_call(
        body, name=name, grid=(groups, nk),
        in_specs=[a_spec, b_spec],
        out_specs=pl.BlockSpec((None, m, n), lambda g, k: (g, 0, 0)),
        out_shape=jax.ShapeDtypeStruct((groups, m, n), out_dtype),
        scratch_shapes=[pltpu.VMEM((m, n), F32)],
        compiler_params=_cparams(("parallel", "arbitrary")),
    )(a, b)
    return out if (ga or gb) else out[0]


def _mm_tn_heads(a, b, tk, name):
    groups, t, m = a.shape
    n = b.shape[-1]
    nk = t // tk

    def body(a_ref, b_ref, o_ref):
        k = pl.program_id(0)

        @pl.when(k == 0)
        def _():
            o_ref[...] = jnp.zeros_like(o_ref)

        bb = b_ref[...]
        for g in range(groups):
            o_ref[g] += _dot_tn(a_ref[g], bb)

    return pl.pallas_call(
        body, name=name, grid=(nk,),
        in_specs=[pl.BlockSpec((groups, tk, m), lambda k: (0, k, 0)), pl.BlockSpec((tk, n), lambda k: (k, 0))],
        out_specs=pl.BlockSpec((groups, m, n), lambda k: (0, 0, 0)),
        out_shape=jax.ShapeDtypeStruct((groups, m, n), F32),
        compiler_params=_cparams(("arbitrary",)),
    )(a, b)


def _mix_in_fwd(h, g_pre, w_parts, g_q, g_kv, w_q, w_k, w_v, cos_t, sin_t, rot, layer, tm, name):
    tp, d = h.shape
    w_xr, w_gr, w_cq, w_ckv, w_kr = w_parts
    lw, ql, kvl = w_xr.shape[0], w_cq.shape[0], w_ckv.shape[0]
    nh, vd = w_v.shape[0], w_v.shape[-1]

    def body(h_ref, gpre_ref, wxr_ref, wgr_ref, wcq_ref, wckv_ref, wkr_ref, gq_ref, gkv_ref, wq_ref, wk_ref, wv_ref,
             cos_ref, sin_ref, rot_ref, xr_ref, gr_ref, cq_ref, ckv_ref, q_ref, k_ref, v_ref):
        u, _ = _rms(h_ref[...], gpre_ref[...])
        u = u.astype(BF16)
        xr_ref[...] = _dot_nt(u, wxr_ref[...])
        gr_ref[...] = _dot_nt(u, wgr_ref[...])
        cq = _dot_nt(u, wcq_ref[...])
        ckv = _dot_nt(u, wckv_ref[...])
        kr = _dot_nt(u, wkr_ref[...])
        cq_ref[...] = cq
        ckv_ref[...] = ckv
        cos_b, sin_b, rot_b = cos_ref[...], sin_ref[...], rot_ref[...]
        qn = _rms(cq, gq_ref[...])[0].astype(BF16)
        kvn = _rms(ckv, gkv_ref[...])[0].astype(BF16)
        k_rope = _rope(kr, cos_b, sin_b, rot_b)
        for hd in range(nh):
            q_pre = _dot_nt(qn, wq_ref[hd])
            q_ref[hd] = (_rope(q_pre, cos_b, sin_b, rot_b) * Q_PRESCALE).astype(BF16)
            k_ref[hd] = (jnp.dot(kvn, wk_ref[hd], preferred_element_type=F32) + k_rope).astype(BF16)
            v_ref[hd] = jnp.dot(kvn, wv_ref[hd], preferred_element_type=F32).astype(BF16)

    def tok(n):
        return pl.BlockSpec((tm, n), lambda i: (i, 0))

    def lay(*shape):
        return pl.BlockSpec((None,) + shape, lambda i: (layer,) + (0,) * len(shape))

    def heads(n):
        return pl.BlockSpec((nh, tm, n), lambda i: (0, i, 0))

    return pl.pallas_call(
        body, name=name, grid=(tp // tm,),
        in_specs=[tok(d), lay(1, d), _whole(lw, d), _whole(lw, d), _whole(ql, d), _whole(kvl, d), _whole(QK_DIM, d),
                  lay(1, ql), lay(1, kvl),
                  _whole(nh, QK_DIM, ql), _whole(nh, kvl, QK_DIM), _whole(nh, kvl, vd), tok(QK_DIM), tok(QK_DIM),
                  pl.BlockSpec((QK_DIM, QK_DIM), lambda i: (0, 0))],
        out_specs=[tok(lw), tok(lw), tok(ql), tok(kvl), heads(QK_DIM), heads(QK_DIM), heads(vd)],
        out_shape=[jax.ShapeDtypeStruct((tp, lw), F32), jax.ShapeDtypeStruct((tp, lw), F32),
                   jax.ShapeDtypeStruct((tp, ql), F32), jax.ShapeDtypeStruct((tp, kvl), F32),
                   jax.ShapeDtypeStruct((nh, tp, QK_DIM), BF16), jax.ShapeDtypeStruct((nh, tp, QK_DIM), BF16),
                   jax.ShapeDtypeStruct((nh, tp, vd), BF16)],
        compiler_params=_cparams(("parallel",)),
    )(h, g_pre, w_xr, w_gr, w_cq, w_ckv, w_kr, g_q, g_kv, w_q, w_k, w_v, cos_t, sin_t, rot)


def _mix_in_bwd(dh_res, h, cq, ckv, d_xr, d_gr, dq, dk, dv, g_pre, w_parts, g_q, g_kv, w_q, w_k, w_v,
                cos_t, sin_t, rot_t, layer, tm, name):
    tp, d = h.shape
    w_xr, w_gr, w_cq, w_ckv, w_kr = w_parts
    lw, ql, kvl = w_xr.shape[0], w_cq.shape[0], w_ckv.shape[0]
    nh, vd = w_v.shape[0], w_v.shape[-1]

    def body(dhr_ref, h_ref, cq_ref, ckv_ref, dxr_ref, dgr_ref, dq_ref, dk_ref, dv_ref, gpre_ref,
             wxr_ref, wgr_ref, wcq_ref, wckv_ref, wkr_ref, gq_ref, gkv_ref, wq_ref, wk_ref, wv_ref,
             cos_ref, sin_ref, rott_ref,
             dh_ref, u_ref, qn_ref, kvn_ref, dqpre_ref, dcq_ref, dckv_ref, dkr_ref, dgpre_ref, dgq_ref, dgkv_ref):
        i = pl.program_id(0)

        @pl.when(i == 0)
        def _():
            dgpre_ref[...] = jnp.zeros_like(dgpre_ref)
            dgq_ref[...] = jnp.zeros_like(dgq_ref)
            dgkv_ref[...] = jnp.zeros_like(dgkv_ref)

        cos_b, sin_b, rott_b = cos_ref[...], sin_ref[...], rott_ref[...]
        hh = h_ref[...]
        u, _ = _rms(hh, gpre_ref[...])
        u_ref[...] = u.astype(BF16)
        cq, ckv = cq_ref[...], ckv_ref[...]
        qn = _rms(cq, gq_ref[...])[0]
        kvn = _rms(ckv, gkv_ref[...])[0]
        qn_ref[...] = qn.astype(BF16)
        kvn_ref[...] = kvn.astype(BF16)
        d_qn = jnp.zeros((tm, ql), F32)
        d_kvn = jnp.zeros((tm, kvl), F32)
        d_krope = jnp.zeros((tm, QK_DIM), F32)
        for hd in range(nh):
            dq_h = dq_ref[hd]
            dq_pre = dq_h * cos_b + _dot_split(dq_h * sin_b, rott_b)
            dq_pre_b = dq_pre.astype(BF16)
            dqpre_ref[hd] = dq_pre_b
            d_qn += _dot(dq_pre_b, wq_ref[hd])
            dk_h = dk_ref[hd]
            d_krope += dk_h
            d_kvn += _dot_nt(dk_h, wk_ref[hd]) + _dot_nt(dv_ref[hd], wv_ref[hd])
        d_kr = d_krope * cos_b + _dot_split(d_krope * sin_b, rott_b)
        d_cq, dgq = _rms_bwd(cq, gq_ref[...], d_qn)
        d_ckv, dgkv = _rms_bwd(ckv, gkv_ref[...], d_kvn)
        dgq_ref[...] += dgq
        dgkv_ref[...] += dgkv
        d_cq_b, d_ckv_b, d_kr_b = d_cq.astype(BF16), d_ckv.astype(BF16), d_kr.astype(BF16)
        dcq_ref[...] = d_cq_b
        dckv_ref[...] = d_ckv_b
        dkr_ref[...] = d_kr_b
        du = (_dot(dxr_ref[...], wxr_ref[...]) + _dot(dgr_ref[...], wgr_ref[...]) + _dot(d_cq_b, wcq_ref[...])
              + _dot(d_ckv_b, wckv_ref[...]) + _dot(d_kr_b, wkr_ref[...]))
        dx, dg = _rms_bwd(hh, gpre_ref[...], du)
        dh_ref[...] = dhr_ref[...] + dx
        dgpre_ref[...] += dg

    def tok(n):
        return pl.BlockSpec((tm, n), lambda i: (i, 0))

    def lay(*shape):
        return pl.BlockSpec((None,) + shape, lambda i: (layer,) + (0,) * len(shape))

    def heads(n):
        return pl.BlockSpec((nh, tm, n), lambda i: (0, i, 0))

    def acc(n):
        return pl.BlockSpec((1, n), lambda i: (0, 0))

    return pl.pallas_call(
        body, name=name, grid=(tp // tm,),
        in_specs=[tok(d), tok(d), tok(ql), tok(kvl), tok(lw), tok(lw), heads(QK_DIM), heads(QK_DIM), heads(vd), lay(1, d),
                  _whole(lw, d), _whole(lw, d), _whole(ql, d), _whole(kvl, d), _whole(QK_DIM, d), lay(1, ql), lay(1, kvl),
                  _whole(nh, QK_DIM, ql), _whole(nh, kvl, QK_DIM), _whole(nh, kvl, vd), tok(QK_DIM), tok(QK_DIM),
                  pl.BlockSpec((QK_DIM, QK_DIM), lambda i: (0, 0))],
        out_specs=[tok(d), tok(d), tok(ql), tok(kvl), heads(QK_DIM), tok(ql), tok(kvl), tok(QK_DIM), acc(d), acc(ql), acc(kvl)],
        out_shape=[jax.ShapeDtypeStruct((tp, d), F32), jax.ShapeDtypeStruct((tp, d), BF16),
                   jax.ShapeDtypeStruct((tp, ql), BF16), jax.ShapeDtypeStruct((tp, kvl), BF16),
                   jax.ShapeDtypeStruct((nh, tp, QK_DIM), BF16), jax.ShapeDtypeStruct((tp, ql), BF16),
                   jax.ShapeDtypeStruct((tp, kvl), BF16), jax.ShapeDtypeStruct((tp, QK_DIM), BF16),
                   jax.ShapeDtypeStruct((1, d), F32), jax.ShapeDtypeStruct((1, ql), F32), jax.ShapeDtypeStruct((1, kvl), F32)],
        compiler_params=_cparams(("arbitrary",)),
    )(dh_res, h, cq, ckv, d_xr, d_gr, dq, dk, dv, g_pre, w_xr, w_gr, w_cq, w_ckv, w_kr, g_q, g_kv, w_q, w_k, w_v,
      cos_t, sin_t, rot_t)


def _lru_gates(xc, wa_ref, wx_ref, ba, bx, sp):
    xcb = xc.astype(BF16)
    r = _sigmoid(jnp.dot(xcb, wa_ref[...], preferred_element_type=F32) + ba)
    ig = _sigmoid(jnp.dot(xcb, wx_ref[...], preferred_element_type=F32) + bx)
    log_a = -LRU_C * r * sp
    a = jnp.exp(log_a)
    sq = jnp.sqrt(_neg_expm1(2.0 * log_a))
    return r, ig, a, sq


def _softplus(x):
    return jnp.maximum(x, 0.0) + jnp.log(1.0 + jnp.exp(-jnp.abs(x)))


def _lru_fwd(xr, gr, conv_w, conv_b, wa_bd, wx_bd, b_a, b_x, lam, layer, tc, name):
    tp, w = xr.shape
    pad = 8

    def body(xr_ref, gr_ref, cw_ref, cb_ref, wa_ref, wx_ref, ba_ref, bx_ref, lam_ref, y_ref, xc_ref, hs_ref,
             xe_sc, a_sc, b_sc, st_sc):
        c = pl.program_id(0)

        @pl.when(c == 0)
        def _():
            xe_sc[pl.ds(0, pad), :] = jnp.zeros((pad, w), F32)
            st_sc[...] = jnp.zeros_like(st_sc)

        xe_sc[pl.ds(pad, tc), :] = xr_ref[...]
        xc = cb_ref[...] + xe_sc[pl.ds(pad, tc), :] * cw_ref[pl.ds(CONV_WIDTH - 1, 1), :]
        for k in range(CONV_WIDTH - 1):
            xc = xc + xe_sc[pl.ds(pad - (CONV_WIDTH - 1) + k, tc), :] * cw_ref[pl.ds(k, 1), :]
        xe_sc[pl.ds(0, pad), :] = xe_sc[pl.ds(tc, pad), :]
        xc_ref[...] = xc
        sp = _softplus(-lam_ref[...])
        _, ig, a, sq = _lru_gates(xc, wa_ref, wx_ref, ba_ref[...], bx_ref[...], sp)
        a_sc[...] = a
        b_sc[...] = sq * (ig * xc)

        def step(t, hcur):
            hnew = a_sc[pl.ds(t, 1), :] * hcur + b_sc[pl.ds(t, 1), :]
            hs_ref[pl.ds(t, 1), :] = hnew
            return hnew

        st_sc[...] = lax.fori_loop(0, tc, step, st_sc[...], unroll=8)
        y_ref[...] = hs_ref[...] * _gelu(gr_ref[...])[0]

    tok = pl.BlockSpec((tc, w), lambda c: (c, 0))

    def lay(*shape):
        return pl.BlockSpec((None,) + shape, lambda c: (layer,) + (0,) * len(shape))

    out = jax.ShapeDtypeStruct((tp, w), F32)
    return pl.pallas_call(
        body, name=name, grid=(tp // tc,),
        in_specs=[tok, tok, lay(CONV_WIDTH, w), lay(1, w), lay(w, w), lay(w, w), lay(1, w), lay(1, w), lay(1, w)],
        out_specs=[tok, tok, tok], out_shape=[out, out, out],
        scratch_shapes=[pltpu.VMEM((tc + pad, w), F32), pltpu.VMEM((tc, w), F32), pltpu.VMEM((tc, w), F32),
                        pltpu.VMEM((1, w), F32)],
        compiler_params=_cparams(("arbitrary",)),
    )(xr, gr, conv_w, conv_b, wa_bd, wx_bd, b_a, b_x, lam)


def _lru_bwd(dy, xr, gr, xc, hs, conv_w, wa_bd, wx_bd, b_a, b_x, lam, layer, tc, name):
    tp, w = xr.shape
    pad = 8
    nc = tp // tc
    per = tc // pad

    def body(dy_ref, xr_ref, gr_ref, xc_ref, hs_ref, xrp_ref, hsp_ref, cw_ref, wa_ref, wx_ref, ba_ref, bx_ref, lam_ref,
             dxr_ref, dgr_ref, dcw_ref, dcb_ref, dwa_ref, dwx_ref, dba_ref, dbx_ref, dlam_ref,
             a_sc, dh_sc, dxc_sc, he_sc, xe_sc, carry_sc):
        s = pl.program_id(0)
        first_chunk = s == nc - 1

        @pl.when(s == 0)
        def _():
            for ref in (dcw_ref, dcb_ref, dwa_ref, dwx_ref, dba_ref, dbx_ref, dlam_ref):
                ref[...] = jnp.zeros_like(ref)
            carry_sc[...] = jnp.zeros_like(carry_sc)
            dxc_sc[pl.ds(tc, pad), :] = jnp.zeros((pad, w), F32)

        keep = jnp.where(first_chunk, 0.0, 1.0)
        he_sc[pl.ds(0, pad), :] = hsp_ref[...] * keep
        he_sc[pl.ds(pad, tc), :] = hs_ref[...]
        xe_sc[pl.ds(0, pad), :] = xrp_ref[...] * keep
        xe_sc[pl.ds(pad, tc), :] = xr_ref[...]

        lam_v = lam_ref[...]
        sp = _softplus(-lam_v)
        xc = xc_ref[...]
        r, ig, a, sq = _lru_gates(xc, wa_ref, wx_ref, ba_ref[...], bx_ref[...], sp)
        a_sc[...] = a
        grv = gr_ref[...]
        gl, th = _gelu(grv)
        dyv = dy_ref[...]
        dgr_ref[...] = dyv * hs_ref[...] * _gelu_grad(grv, th)
        dh_sc[...] = dyv * gl

        def step(n, g):
            t = tc - 1 - n
            dh = dh_sc[pl.ds(t, 1), :] + g
            dh_sc[pl.ds(t, 1), :] = dh
            return a_sc[pl.ds(t, 1), :] * dh

        carry_sc[...] = lax.fori_loop(0, tc, step, carry_sc[...], unroll=8)

        dh = dh_sc[...]
        d_a = dh * he_sc[pl.ds(pad - 1, tc), :]
        d_ixc = dh * sq
        d_sq = dh * (ig * xc)
        a2 = a * a
        d_la = d_a * a - d_sq * a2 / sq
        d_r = d_la * (-LRU_C * sp)
        d_sp = jnp.sum(d_la * (-LRU_C * r), axis=0, keepdims=True)
        dlam_ref[...] += d_sp * (-_sigmoid(-lam_v))
        d_pa = d_r * r * (1.0 - r)
        d_px = d_ixc * xc * ig * (1.0 - ig)
        dba_ref[...] += jnp.sum(d_pa, axis=0, keepdims=True)
        dbx_ref[...] += jnp.sum(d_px, axis=0, keepdims=True)
        d_pa_b, d_px_b = d_pa.astype(BF16), d_px.astype(BF16)
        xcb = xc.astype(BF16)
        dwa_ref[...] += _dot_tn(xcb, d_pa_b)
        dwx_ref[...] += _dot_tn(xcb, d_px_b)
        d_xc = d_ixc * ig + _dot_nt(d_pa_b, wa_ref[...]) + _dot_nt(d_px_b, wx_ref[...])
        dcb_ref[...] += jnp.sum(d_xc, axis=0, keepdims=True)
        dxc_sc[pl.ds(0, tc), :] = d_xc
        d_xr = jnp.zeros((tc, w), F32)
        for k in range(CONV_WIDTH):
            off = CONV_WIDTH - 1 - k
            d_xr = d_xr + dxc_sc[pl.ds(off, tc), :] * cw_ref[pl.ds(k, 1), :]
            dcw_ref[pl.ds(k, 1), :] += jnp.sum(d_xc * xe_sc[pl.ds(pad - off, tc), :], axis=0, keepdims=True)
        dxr_ref[...] = d_xr
        dxc_sc[pl.ds(tc, pad), :] = dxc_sc[pl.ds(0, pad), :]

    def rev(cidx):
        return nc - 1 - cidx

    tok = pl.BlockSpec((tc, w), lambda c: (rev(c), 0))
    prev = pl.BlockSpec((pad, w), lambda c: (jnp.maximum(rev(c) * per - 1, 0), 0))

    def lay(*shape):
        return pl.BlockSpec((None,) + shape, lambda c: (layer,) + (0,) * len(shape))

    def acc(*shape):
        return pl.BlockSpec(shape, lambda c: (0,) * len(shape))

    big = jax.ShapeDtypeStruct((tp, w), F32)
    vec = jax.ShapeDtypeStruct((1, w), F32)
    return pl.pallas_call(
        body, name=name, grid=(nc,),
        in_specs=[tok, tok, tok, tok, tok, prev, prev, lay(CONV_WIDTH, w), lay(w, w), lay(w, w), lay(1, w), lay(1, w), lay(1, w)],
        out_specs=[tok, tok, acc(CONV_WIDTH, w), acc(1, w), acc(w, w), acc(w, w), acc(1, w), acc(1, w), acc(1, w)],
        out_shape=[big, big, jax.ShapeDtypeStruct((CONV_WIDTH, w), F32), vec, jax.ShapeDtypeStruct((w, w), F32),
                   jax.ShapeDtypeStruct((w, w), F32), vec, vec, vec],
        scratch_shapes=[pltpu.VMEM((tc, w), F32), pltpu.VMEM((tc, w), F32), pltpu.VMEM((tc + pad, w), F32),
                        pltpu.VMEM((tc + pad, w), F32), pltpu.VMEM((tc + pad, w), F32), pltpu.VMEM((1, w), F32)],
        compiler_params=_cparams(("arbitrary",)),
    )(dy, xr, gr, xc, hs, xr, hs, conv_w, wa_bd, wx_bd, b_a, b_x, lam)


def _attn_fwd(q, k, v, tq, name, job=None):
    nh, tp, dqk = q.shape
    vd = v.shape[-1]
    nq = tp // tq
    n_ji = len(job.inputs) if job else 0
    n_jo = len(job.out_shape) if job else 0

    def body(q_ref, k_ref, v_ref, *rest):
        job_in, (o_ref, lse_ref), rest = rest[:n_ji], rest[n_ji:n_ji + 2], rest[n_ji + 2:]
        job_out, sems = rest[:n_jo], rest[n_jo:]
        i = pl.program_id(1)
        if job:
            first = (pl.program_id(0) == 0) & (i == 0)
            pl.when(first)(lambda: job.start(job_in, job_out, sems))
        qb = q_ref[...]

        def absorb(j, width, carry, diagonal=False):
            m, l, acc = carry
            off = pl.multiple_of(j * tq, tq)
            s = _dot_nt(qb, k_ref[pl.ds(off, width), :])
            if diagonal:
                keep = lax.broadcasted_iota(jnp.int32, (tq, width), 1) <= lax.broadcasted_iota(jnp.int32, (tq, width), 0)
                s = jnp.where(keep, s, NEG_BIG)
            m_new = jnp.maximum(m, jnp.max(s, axis=-1, keepdims=True))
            p = jnp.exp2(s - m_new)
            alpha = jnp.exp2(m - m_new)
            l = alpha * l + jnp.sum(p, axis=-1, keepdims=True)
            acc = alpha * acc + jnp.dot(p.astype(BF16), v_ref[pl.ds(off, width), :], preferred_element_type=F32)
            return m_new, l, acc

        carry = (jnp.full((tq, 1), NEG_BIG, F32), jnp.zeros((tq, 1), F32), jnp.zeros((tq, vd), F32))
        carry = lax.fori_loop(0, i // 2, lambda jj, c: absorb(2 * jj, 2 * tq, c), carry)
        carry = lax.cond(i % 2 == 1, lambda c: absorb(i - 1, tq, c), lambda c: c, carry)
        m, l, acc = absorb(i, tq, carry, diagonal=True)
        o_ref[...] = acc / l
        lse_ref[...] = m + jnp.log2(l)
        if job:
            last = (pl.program_id(0) == nh - 1) & (i == nq - 1)
            pl.when(last)(lambda: job.finish(job_in, job_out, sems))

    outs = pl.pallas_call(
        body, name=name, grid=(nh, nq),
        in_specs=[pl.BlockSpec((None, tq, dqk), lambda h, i: (h, i, 0)),
                  pl.BlockSpec((None, tp, dqk), lambda h, i: (h, 0, 0)),
                  pl.BlockSpec((None, tp, vd), lambda h, i: (h, 0, 0))] + [_ANY] * n_ji,
        out_specs=[pl.BlockSpec((None, tq, vd), lambda h, i: (h, i, 0)),
                   pl.BlockSpec((None, tq, 1), lambda h, i: (h, i, 0))] + [_ANY] * n_jo,
        out_shape=[jax.ShapeDtypeStruct((nh, tp, vd), F32), jax.ShapeDtypeStruct((nh, tp, 1), F32)]
        + (list(job.out_shape) if job else []),
        scratch_shapes=list(job.scratch) if job else [],
        compiler_params=_cparams(("arbitrary", "arbitrary") if job else ("parallel", "parallel")),
    )(q, k, v, *(job.inputs if job else ()))
    return outs[0], outs[1], list(outs[2:])


def _attn_bwd(q, k, v, o, do, lse, tq, name, job=None):
    nh, tp, dqk = q.shape
    vd = v.shape[-1]
    scale = dqk ** -0.5
    nq = tp // tq
    n_ji = len(job.inputs) if job else 0
    n_jo = len(job.out_shape) if job else 0

    def body(q_ref, k_ref, v_ref, o_ref, do_ref, lse_ref, *rest):
        job_in, (dq_ref, dk_ref, dv_ref), rest = rest[:n_ji], rest[n_ji:n_ji + 3], rest[n_ji + 3:]
        job_out, (lse_rows, delta_rows, do_bf, dqt_sc), sems = rest[:n_jo], rest[n_jo:n_jo + 4], rest[n_jo + 4:]
        kb_i = pl.program_id(1)
        if job:
            first = (pl.program_id(0) == 0) & (kb_i == 0)
            pl.when(first)(lambda: job.start(job_in, job_out, sems))

        def as_rows(col):
            return jnp.transpose(jnp.broadcast_to(col, (tq, _LANES)))[0:8, :]

        @pl.when(kb_i == 0)
        def _():
            dqt_sc[...] = jnp.zeros_like(dqt_sc)

            def prep(qi, _):
                off = pl.multiple_of(qi * tq, tq)
                dob = do_ref[pl.ds(off, tq), :]
                do_bf[pl.ds(off, tq), :] = dob.astype(BF16)
                delta_rows[qi] = as_rows(jnp.sum(dob * o_ref[pl.ds(off, tq), :], axis=-1, keepdims=True))
                lse_rows[qi] = as_rows(lse_ref[pl.ds(off, tq), :])
                return 0

            lax.fori_loop(0, nq, prep, 0)

        kb = k_ref[...]
        vb = v_ref[...]

        def absorb(qi, n_blk, carry, diagonal=False):
            dk, dv = carry
            off = pl.multiple_of(qi * tq, tq)
            qb = q_ref[pl.ds(off, n_blk * tq), :]
            dob = do_bf[pl.ds(off, n_blk * tq), :]
            lse_row = jnp.concatenate([lse_rows[qi + b, 0:1, :] for b in range(n_blk)], axis=1)
            delta_row = jnp.concatenate([delta_rows[qi + b, 0:1, :] for b in range(n_blk)], axis=1)
            st = _dot_nt(kb, qb)
            if diagonal:
                keep = lax.broadcasted_iota(jnp.int32, (tq, tq), 0) <= lax.broadcasted_iota(jnp.int32, (tq, tq), 1)
                st = jnp.where(keep, st, NEG_BIG)
            pt = jnp.exp2(st - lse_row)
            dpt = _dot_nt(vb, dob)
            dst = (pt * (dpt - delta_row)).astype(BF16)
            dv = dv + jnp.dot(pt.astype(BF16), dob, preferred_element_type=F32)
            dk = dk + jnp.dot(dst, qb, preferred_element_type=F32)
            dqt = _dot_tn(kb, dst)
            for b in range(n_blk):
                dqt_sc[qi + b] += dqt[:, b * tq:(b + 1) * tq]
            return dk, dv

        carry = absorb(kb_i, 1, (jnp.zeros((tq, dqk), F32), jnp.zeros((tq, vd), F32)), diagonal=True)
        later = nq - 1 - kb_i
        carry = lax.fori_loop(0, later // 2, lambda pp, c: absorb(kb_i + 1 + 2 * pp, 2, c), carry)
        dk, dv = lax.cond(later % 2 == 1, lambda c: absorb(nq - 1, 1, c), lambda c: c, carry)
        dk_ref[...] = dk * math.log(2.0)
        dv_ref[...] = dv

        @pl.when(kb_i == nq - 1)
        def _():
            eye = (lax.broadcasted_iota(jnp.int32, (dqk, dqk), 0) == lax.broadcasted_iota(jnp.int32, (dqk, dqk), 1))
            eye = jnp.where(eye, 1.0, 0.0).astype(BF16)
            for qi in range(nq):
                t = dqt_sc[qi]
                hi = t.astype(BF16)
                lo = (t - hi.astype(F32)).astype(BF16)
                dq_ref[pl.ds(qi * tq, tq), :] = scale * (_dot_tn(hi, eye) + _dot_tn(lo, eye))

        if job:
            last = (pl.program_id(0) == nh - 1) & (kb_i == nq - 1)
            pl.when(last)(lambda: job.finish(job_in, job_out, sems))

    def full(n):
        return pl.BlockSpec((None, tp, n), lambda h, j: (h, 0, 0))

    def blk(n):
        return pl.BlockSpec((None, tq, n), lambda h, j: (h, j, 0))

    outs = pl.pallas_call(
        body, name=name, grid=(nh, nq),
        in_specs=[full(dqk), blk(dqk), blk(vd), full(vd), full(vd), full(1)] + [_ANY] * n_ji,
        out_specs=[full(dqk), blk(dqk), blk(vd)] + [_ANY] * n_jo,
        out_shape=[jax.ShapeDtypeStruct((nh, tp, dqk), F32), jax.ShapeDtypeStruct((nh, tp, dqk), F32),
                   jax.ShapeDtypeStruct((nh, tp, vd), F32)] + (list(job.out_shape) if job else []),
        scratch_shapes=[pltpu.VMEM((nq, 8, tq), F32), pltpu.VMEM((nq, 8, tq), F32), pltpu.VMEM((tp, vd), BF16),
                        pltpu.VMEM((nq, dqk, tq), F32)] + (list(job.scratch) if job else []),
        compiler_params=_cparams(("arbitrary", "arbitrary") if job else ("parallel", "arbitrary")),
    )(q, k, v, o, do, lse, *(job.inputs if job else ()))
    return outs[0], outs[1], outs[2], list(outs[3:])


def _mix_out_fwd(h, y_lru, o, g_lo, g_mo, w_lo, w_mo, g_post, layer, tm, name):
    tp, d = h.shape
    lw = y_lru.shape[-1]
    nh, vd = o.shape[0], o.shape[-1]

    def body(h_ref, yl_ref, o_ref, glo_ref, gmo_ref, wlo_ref, wmo_ref, gpost_ref, hout_ref, y_ref):
        nl, _ = _rms(yl_ref[...], glo_ref[...])
        y = _dot(nl, wlo_ref[...])
        ssq = jnp.zeros((tm, 1), F32)
        for hd in range(nh):
            oh = o_ref[hd]
            ssq += jnp.sum(oh * oh, axis=-1, keepdims=True)
        r = lax.rsqrt(ssq / (nh * vd) + EPS)
        for hd in range(nh):
            y += _dot(o_ref[hd] * r * gmo_ref[hd], wmo_ref[hd])
        y_ref[...] = y
        hout_ref[...] = h_ref[...] + _rms(y, gpost_ref[...])[0]

    tok = pl.BlockSpec((tm, d), lambda i: (i, 0))

    def lay(*shape):
        return pl.BlockSpec((None,) + shape, lambda i: (layer,) + (0,) * len(shape))

    return pl.pallas_call(
        body, name=name, grid=(tp // tm,),
        in_specs=[tok, pl.BlockSpec((tm, lw), lambda i: (i, 0)), pl.BlockSpec((nh, tm, vd), lambda i: (0, i, 0)),
                  lay(1, lw), lay(nh, 1, vd), _whole(lw, d), _whole(nh, vd, d), lay(1, d)],
        out_specs=[tok, tok],
        out_shape=[jax.ShapeDtypeStruct((tp, d), F32), jax.ShapeDtypeStruct((tp, d), F32)],
        compiler_params=_cparams(("parallel",)),
    )(h, y_lru, o, g_lo, g_mo, w_lo, w_mo, g_post)


def _mix_out_bwd(dh_out, y, y_lru, o, g_lo, g_mo, w_lo, w_mo, g_post, layer, tm, name):
    tp, d = y.shape
    lw = y_lru.shape[-1]
    nh, vd = o.shape[0], o.shape[-1]

    def body(dho_ref, y_ref, yl_ref, o_ref, glo_ref, gmo_ref, wlo_ref, wmo_ref, gpost_ref,
             dyl_ref, do_ref, dy_ref, nl_ref, nm_ref, dgpost_ref, dglo_ref, dgmo_ref):
        i = pl.program_id(0)

        @pl.when(i == 0)
        def _():
            dgpost_ref[...] = jnp.zeros_like(dgpost_ref)
            dglo_ref[...] = jnp.zeros_like(dglo_ref)
            dgmo_ref[...] = jnp.zeros_like(dgmo_ref)

        dy, dgp = _rms_bwd(y_ref[...], gpost_ref[...], dho_ref[...])
        dgpost_ref[...] += dgp
        dy_b = dy.astype(BF16)
        dy_ref[...] = dy_b
        yl = yl_ref[...]
        nl_ref[...] = _rms(yl, glo_ref[...])[0].astype(BF16)
        dyl, dgl = _rms_bwd(yl, glo_ref[...], _dot_nt(dy_b, wlo_ref[...]))
        dyl_ref[...] = dyl
        dglo_ref[...] += dgl
        ssq = jnp.zeros((tm, 1), F32)
        for hd in range(nh):
            oh = o_ref[hd]
            ssq += jnp.sum(oh * oh, axis=-1, keepdims=True)
        r = lax.rsqrt(ssq / (nh * vd) + EPS)
        dn = []
        dot_sum = jnp.zeros((tm, 1), F32)
        for hd in range(nh):
            oh = o_ref[hd] * r
            nm_ref[hd] = (oh * gmo_ref[hd]).astype(BF16)
            dn_h = _dot_nt(dy_b, wmo_ref[hd])
            dgmo_ref[hd] += jnp.sum(dn_h * oh, axis=0, keepdims=True)
            dn_h = dn_h * gmo_ref[hd]
            dot_sum += jnp.sum(dn_h * oh, axis=-1, keepdims=True)
            dn.append(dn_h)
        dot_mean = dot_sum / (nh * vd)
        for hd in range(nh):
            do_ref[hd] = r * (dn[hd] - o_ref[hd] * r * dot_mean)

    tok = pl.BlockSpec((tm, d), lambda i: (i, 0))
    tokl = pl.BlockSpec((tm, lw), lambda i: (i, 0))
    heads = pl.BlockSpec((nh, tm, vd), lambda i: (0, i, 0))

    def lay(*shape):
        return pl.BlockSpec((None,) + shape, lambda i: (layer,) + (0,) * len(shape))

    def acc(*shape):
        return pl.BlockSpec(shape, lambda i: (0,) * len(shape))

    return pl.pallas_call(
        body, name=name, grid=(tp // tm,),
        in_specs=[tok, tok, tokl, heads, lay(1, lw), lay(nh, 1, vd), _whole(lw, d), _whole(nh, vd, d), lay(1, d)],
        out_specs=[tokl, heads, tok, tokl, heads, acc(1, d), acc(1, lw), acc(nh, 1, vd)],
        out_shape=[jax.ShapeDtypeStruct((tp, lw), F32), jax.ShapeDtypeStruct((nh, tp, vd), F32),
                   jax.ShapeDtypeStruct((tp, d), BF16), jax.ShapeDtypeStruct((tp, lw), BF16),
                   jax.ShapeDtypeStruct((nh, tp, vd), BF16), jax.ShapeDtypeStruct((1, d), F32),
                   jax.ShapeDtypeStruct((1, lw), F32), jax.ShapeDtypeStruct((nh, 1, vd), F32)],
        compiler_params=_cparams(("arbitrary",)),
    )(dh_out, y, y_lru, o, g_lo, g_mo, w_lo, w_mo, g_post)


def _loss_head(h, target, n_real, tm, name):
    tp, d = h.shape
    nt = tp // tm

    def body(h_ref, t_ref, dh_ref, part_ref):
        i = pl.program_id(0)
        row = i * tm + lax.broadcasted_iota(jnp.int32, (tm, 1), 0)
        real = (row >= N_META) & (row < N_META + n_real)
        err = jnp.where(real, h_ref[...] - t_ref[...], 0.0)
        dh_ref[...] = err * (1.0 / d)
        part_ref[...] = jnp.full((1, 1, 128), 0.5 / d * jnp.sum(err * err), F32)

    tok = pl.BlockSpec((tm, d), lambda i: (i, 0))
    return pl.pallas_call(
        body, name=name, grid=(nt,), in_specs=[tok, tok],
        out_specs=[tok, pl.BlockSpec((1, 1, 128), lambda i: (i, 0, 0))],
        out_shape=[jax.ShapeDtypeStruct((tp, d), F32), jax.ShapeDtypeStruct((nt, 1, 128), F32)],
        compiler_params=_cparams(("parallel",)),
    )(h, target)


def _mesh_pos():
    return lax.axis_index("x"), lax.axis_index("y"), lax.axis_index("c")


def _other_chips(x, y):
    return [(1 - x, y), (x, 1 - y), (1 - x, 1 - y)]


class _GatherJob:
    def __init__(self, inputs, picks, shard_shapes, dtypes, split_last=()):
        self.inputs = list(inputs)
        self.picks = list(picks)
        n = len(self.inputs)
        self.out_shape = [jax.ShapeDtypeStruct((N_CHIPS,) + tuple(s), dt) for s, dt in zip(shard_shapes, dtypes)]
        self.split_last = [a in split_last for a in range(n)]
        self.halves = [(s[1] if second else s[0]) // 2 for s, second in zip(shard_shapes, self.split_last)]
        self.scratch = [pltpu.SemaphoreType.DMA((n, 3)), pltpu.SemaphoreType.DMA((n, 3)), pltpu.SemaphoreType.DMA((n, 3)),
                        pltpu.SemaphoreType.DMA((n, 3)), pltpu.SemaphoreType.DMA((n,))]

    def _copies(self, ins, outs, sems):
        ici_s, ici_r, core_s, core_r, loc = sems
        x, y, c = _mesh_pos()
        me = 2 * x + y
        chips = _other_chips(x, y)
        local, ici, fwd = [], [], []
        for a in range(len(ins)):
            src = self.picks[a](ins[a])
            hv = self.halves[a]

            def half(ref, which, a=a, hv=hv):
                return ref.at[:, pl.ds(which * hv, hv)] if self.split_last[a] else ref.at[pl.ds(which * hv, hv)]

            local.append(pltpu.make_async_copy(src, outs[a].at[me], loc.at[a]))
            for p, (px, py) in enumerate(chips):
                ici.append((pltpu.make_async_remote_copy(src_ref=half(src, c), dst_ref=half(outs[a].at[me], c),
                                                         send_sem=ici_s.at[a, p], recv_sem=ici_r.at[a, p],
                                                         device_id=(px, py, c), device_id_type=MESH),
                            pltpu.make_async_remote_copy(src_ref=half(src, c), dst_ref=half(outs[a].at[2 * px + py], c),
                                                         send_sem=ici_s.at[a, p], recv_sem=ici_r.at[a, p],
                                                         device_id=(px, py, c), device_id_type=MESH)))
                landed = half(outs[a].at[2 * px + py], c)
                theirs = half(outs[a].at[2 * px + py], 1 - c)
                fwd.append((pltpu.make_async_remote_copy(src_ref=landed, dst_ref=landed, send_sem=core_s.at[a, p],
                                                         recv_sem=core_r.at[a, p], device_id=(x, y, 1 - c), device_id_type=MESH),
                            pltpu.make_async_remote_copy(src_ref=theirs, dst_ref=theirs, send_sem=core_s.at[a, p],
                                                         recv_sem=core_r.at[a, p], device_id=(x, y, 1 - c), device_id_type=MESH)))
        return local, ici, fwd

    def start(self, ins, outs, sems):
        local, ici, _ = self._copies(ins, outs, sems)
        for cp in local:
            cp.start()
        for send, _ in ici:
            send.start()

    def finish(self, ins, outs, sems):
        local, ici, fwd = self._copies(ins, outs, sems)
        for (_, arrive), (hand_on, _) in zip(ici, fwd):
            arrive.wait_recv()
            hand_on.start()
        for _, arrive in fwd:
            arrive.wait_recv()
        for send, _ in ici:
            send.wait_send()
        for hand_on, _ in fwd:
            hand_on.wait_send()
        for cp in local:
            cp.wait()


class _ReduceJob:
    def __init__(self, inputs):
        self.inputs = list(inputs)
        n = len(self.inputs)
        self.out_shape = [jax.ShapeDtypeStruct(a.shape, a.dtype) for a in self.inputs] * 2
        self.scratch = [pltpu.SemaphoreType.DMA((n, 3)), pltpu.SemaphoreType.DMA((n, 3)), pltpu.SemaphoreType.DMA((n, 4)),
                        pltpu.SemaphoreType.DMA((n, 4)), pltpu.SemaphoreType.DMA((n,))]

    def _copies(self, ins, outs, sems):
        ici_s, ici_r, core_s, core_r, loc = sems
        n = len(ins)
        same, other = outs[:n], outs[n:]
        x, y, c = _mesh_pos()
        me = 2 * x + y
        chips = _other_chips(x, y)
        sib = (x, y, 1 - c)
        local, ici, fwd = [], [], []
        for a in range(n):
            local.append(pltpu.make_async_copy(ins[a].at[me], same[a].at[me], loc.at[a]))
            fwd.append((pltpu.make_async_remote_copy(src_ref=ins[a].at[me], dst_ref=other[a].at[me], send_sem=core_s.at[a, 3],
                                                     recv_sem=core_r.at[a, 3], device_id=sib, device_id_type=MESH), None))
            for p, (px, py) in enumerate(chips):
                cp = 2 * px + py
                ici.append((pltpu.make_async_remote_copy(src_ref=ins[a].at[cp], dst_ref=same[a].at[me], send_sem=ici_s.at[a, p],
                                                         recv_sem=ici_r.at[a, p], device_id=(px, py, c), device_id_type=MESH),
                            pltpu.make_async_remote_copy(src_ref=ins[a].at[cp], dst_ref=same[a].at[cp], send_sem=ici_s.at[a, p],
                                                         recv_sem=ici_r.at[a, p], device_id=(px, py, c), device_id_type=MESH)))
                fwd.append((pltpu.make_async_remote_copy(src_ref=same[a].at[cp], dst_ref=other[a].at[cp], send_sem=core_s.at[a, p],
                                                         recv_sem=core_r.at[a, p], device_id=sib, device_id_type=MESH), p))
        return local, ici, fwd

    def start(self, ins, outs, sems):
        local, ici, fwd = self._copies(ins, outs, sems)
        for cp in local:
            cp.start()
        for send, _ in ici:
            send.start()
        for hand_on, p in fwd:
            if p is None:
                hand_on.start()

    def finish(self, ins, outs, sems):
        local, ici, fwd = self._copies(ins, outs, sems)
        chip_fwd = [f for f in fwd if f[1] is not None]
        for (_, arrive), (hand_on, _) in zip(ici, chip_fwd):
            arrive.wait_recv()
            hand_on.start()
        for hand_on, _ in fwd:
            hand_on.wait()
        for send, _ in ici:
            send.wait_send()
        for cp in local:
            cp.wait()


class _DeviceGatherJob:
    def __init__(self, buf):
        self.inputs = [buf]
        self.out_shape = [jax.ShapeDtypeStruct((8,) + buf.shape, buf.dtype)]
        self.scratch = [pltpu.SemaphoreType.DMA((7,)), pltpu.SemaphoreType.DMA((7,)), pltpu.SemaphoreType.DMA]

    def _copies(self, ins, outs, sems):
        send_s, recv_s, loc = sems
        x, y, c = _mesh_pos()
        me = 4 * x + 2 * y + c
        local = pltpu.make_async_copy(ins[0], outs[0].at[me], loc)
        remote = []
        for rel in range(1, 8):
            fx, fy, fc = (rel >> 2) & 1, (rel >> 1) & 1, rel & 1
            peer = (1 - x if fx else x, 1 - y if fy else y, 1 - c if fc else c)
            remote.append(pltpu.make_async_remote_copy(src_ref=ins[0], dst_ref=outs[0].at[me], send_sem=send_s.at[rel - 1],
                                                       recv_sem=recv_s.at[rel - 1], device_id=peer, device_id_type=MESH))
        return local, remote

    def start(self, ins, outs, sems):
        local, remote = self._copies(ins, outs, sems)
        local.start()
        for rc in remote:
            rc.start()

    def finish(self, ins, outs, sems):
        local, remote = self._copies(ins, outs, sems)
        for rc in remote:
            rc.wait()
        local.wait()


def _run_jobs(jobs, name):
    n_in = [len(j.inputs) for j in jobs]
    n_out = [len(j.out_shape) for j in jobs]
    n_sem = [len(j.scratch) for j in jobs]

    def body(*refs):
        ins, outs, sems = refs[:sum(n_in)], refs[sum(n_in):sum(n_in) + sum(n_out)], refs[sum(n_in) + sum(n_out):]

        def parts(k):
            return (ins[sum(n_in[:k]):sum(n_in[:k + 1])], outs[sum(n_out[:k]):sum(n_out[:k + 1])],
                    sems[sum(n_sem[:k]):sum(n_sem[:k + 1])])

        for k, job in enumerate(jobs):
            job.start(*parts(k))
        for k, job in enumerate(jobs):
            job.finish(*parts(k))

    res = pl.pallas_call(body, name=name, in_specs=[_ANY] * sum(n_in), out_specs=[_ANY] * sum(n_out),
                         out_shape=[s for j in jobs for s in j.out_shape],
                         scratch_shapes=[s for j in jobs for s in j.scratch])(*[a for j in jobs for a in j.inputs])
    return [list(res[sum(n_out[:k]):sum(n_out[:k + 1])]) for k in range(len(jobs))]


def _sum_devices(parts, name):
    n_dev, rows, lanes = parts.shape

    def body(p_ref, o_ref):
        total = p_ref[0]
        for k in range(1, n_dev):
            total = total + p_ref[k]
        o_ref[...] = total

    return pl.pallas_call(body, name=name, out_shape=jax.ShapeDtypeStruct((rows, lanes), F32),
                          compiler_params=pltpu.CompilerParams(vmem_limit_bytes=V7X_VMEM_LIMIT_BYTES))(parts)


def _row_tile(rows, limit=512):
    best = None
    for t in range(16, min(rows, limit) + 1, 16):
        if rows % t == 0:
            best = t
    return best if best is not None else rows


def _adamw_math(w, g, m, v):
    m = ADAM_B1 * m + (1.0 - ADAM_B1) * g
    v = ADAM_B2 * v + (1.0 - ADAM_B2) * (g * g)
    m_hat = m / (1.0 - ADAM_B1 ** ADAM_STEP)
    v_hat = v / (1.0 - ADAM_B2 ** ADAM_STEP)
    delta = -ADAM_LR * (m_hat / (jnp.sqrt(v_hat) + ADAM_EPS) + ADAM_WD * w)
    return delta, m, v


def _adamw_layer(w, m, v, same_plane, other_plane, layer, prev, name):
    nl, r, ccol = w.shape
    ns = same_plane.shape[0]
    tr = _row_tile(r, 256)

    def body(w_ref, m_ref, v_ref, a_ref, b_ref, *rest):
        go_ref, d_ref, mo_ref, vo_ref = rest[-4:]
        sa = a_ref[0].astype(F32)
        sb = b_ref[0].astype(F32)
        for k in range(1, ns):
            sa = sa + a_ref[k].astype(F32)
            sb = sb + b_ref[k].astype(F32)
        g = sa + sb
        go_ref[...] = g
        d_ref[...], mo_ref[...], vo_ref[...] = _adamw_math(w_ref[...], g, m_ref[...], v_ref[...])

    blk = pl.BlockSpec((None, tr, ccol), lambda i: (layer, i, 0))
    plane = pl.BlockSpec((ns, tr, ccol), lambda i: (0, i, 0))
    out = jax.ShapeDtypeStruct((nl, r, ccol), F32)
    n_prev = 4 if prev is not None else 0
    return pl.pallas_call(
        body, name=name, grid=(r // tr,),
        in_specs=[blk, blk, blk, plane, plane] + [_ANY] * n_prev,
        out_specs=[blk, blk, blk, blk], out_shape=[out, out, out, out],
        input_output_aliases={5 + k: k for k in range(n_prev)},
        compiler_params=_cparams(("parallel",)))(w, m, v, same_plane, other_plane, *(prev or ()))


def _adamw_whole(w, g, m, v, name):
    def body(w_ref, g_ref, m_ref, v_ref, d_ref, mo_ref, vo_ref):
        d_ref[...], mo_ref[...], vo_ref[...] = _adamw_math(w_ref[...], g_ref[...], m_ref[...], v_ref[...])

    out = jax.ShapeDtypeStruct(w.shape, F32)
    return pl.pallas_call(body, name=name, out_shape=[out, out, out],
                          compiler_params=pltpu.CompilerParams(vmem_limit_bytes=V7X_VMEM_LIMIT_BYTES))(w, g, m, v)


_W_NAMES = ['meta_tokens', 'ffn1_pre_g', 'ffn1_w_gate', 'ffn1_w_up', 'ffn1_w_down', 'ffn1_post_g', 'mix_pre_g', 'w_in',
            'lru_conv_w', 'lru_conv_b', 'lru_w_a', 'lru_b_a', 'lru_w_x', 'lru_b_x', 'lru_lambda', 'mla_q_norm_g', 'mla_w_uq',
            'mla_kv_norm_g', 'mla_w_ukv', 'lru_out_g', 'mla_out_g', 'w_out', 'mix_post_g', 'ffn2_pre_g', 'ffn2_w_gate',
            'ffn2_w_up', 'ffn2_w_down', 'ffn2_post_g']
_REPLICATED = ['ffn1_pre_g', 'ffn1_post_g', 'mix_pre_g', 'lru_conv_b', 'lru_w_a', 'lru_b_a', 'lru_w_x', 'lru_b_x', 'lru_lambda',
               'mla_q_norm_g', 'mla_kv_norm_g', 'lru_out_g', 'mla_out_g', 'mix_post_g', 'ffn2_pre_g', 'ffn2_post_g']
_LANES = 128


def _pack_rows(arrays):
    flat = jnp.concatenate([a.reshape(-1) for a in arrays])
    total = _round_up(flat.shape[0], 8 * _LANES)
    return jnp.pad(flat, (0, total - flat.shape[0])).reshape(-1, _LANES)


def _unpack_rows(buf, shapes):
    flat = buf.reshape(-1)
    out, off = [], 0
    for shp in shapes:
        n = math.prod(shp)
        out.append(flat[off:off + n].reshape(shp))
        off += n
    return out


def _block_diag(w):
    nl, nh, n, _ = w.shape
    eye = jnp.eye(nh, dtype=w.dtype)
    return (w[:, :, :, None, :] * eye[None, :, None, :, None]).reshape(nl, nh * n, nh * n)


def _diag_blocks(bd, nh):
    n = bd.shape[0] // nh
    b4 = bd.reshape(nh, n, nh, n)
    return jnp.stack([b4[i, :, i, :] for i in range(nh)])


def kernel(x, meta_tokens, ffn1_pre_g, ffn1_w_gate, ffn1_w_up, ffn1_w_down, ffn1_post_g, mix_pre_g, w_in, lru_conv_w, lru_conv_b, lru_w_a, lru_b_a, lru_w_x, lru_b_x, lru_lambda, mla_q_norm_g, mla_w_uq, mla_kv_norm_g, mla_w_ukv, lru_out_g, mla_out_g, w_out, mix_post_g, ffn2_pre_g, ffn2_w_gate, ffn2_w_up, ffn2_w_down, ffn2_post_g, loss_target, m_meta_tokens, m_ffn1_pre_g, m_ffn1_w_gate, m_ffn1_w_up, m_ffn1_w_down, m_ffn1_post_g, m_mix_pre_g, m_w_in, m_lru_conv_w, m_lru_conv_b, m_lru_w_a, m_lru_b_a, m_lru_w_x, m_lru_b_x, m_lru_lambda, m_mla_q_norm_g, m_mla_w_uq, m_mla_kv_norm_g, m_mla_w_ukv, m_lru_out_g, m_mla_out_g, m_w_out, m_mix_post_g, m_ffn2_pre_g, m_ffn2_w_gate, m_ffn2_w_up, m_ffn2_w_down, m_ffn2_post_g, v_meta_tokens, v_ffn1_pre_g, v_ffn1_w_gate, v_ffn1_w_up, v_ffn1_w_down, v_ffn1_post_g, v_mix_pre_g, v_w_in, v_lru_conv_w, v_lru_conv_b, v_lru_w_a, v_lru_b_a, v_lru_w_x, v_lru_b_x, v_lru_lambda, v_mla_q_norm_g, v_mla_w_uq, v_mla_kv_norm_g, v_mla_w_ukv, v_lru_out_g, v_mla_out_g, v_w_out, v_mix_post_g, v_ffn2_pre_g, v_ffn2_w_gate, v_ffn2_w_up, v_ffn2_w_down, v_ffn2_post_g):
    a = dict(locals())
    x2 = x[0]
    s_len, d = x2.shape
    nl = ffn1_pre_g.shape[0]
    lw = lru_conv_b.shape[-1]
    ql, kvl = mla_q_norm_g.shape[-1], mla_kv_norm_g.shape[-1]
    nh = MLA_HEADS
    vd = (d - lw) // nh
    t_real = N_META + s_len
    tp = _round_up(t_real, 384)
    tm = tp // 6
    tmb = tm // 2
    tm_ffn_fwd = tp // 4
    tm_ffn_bwd = tp // 8
    tq = 384 if tp >= 1536 else 128
    xi, yi, _ = _mesh_pos()
    chip = 2 * xi + yi

    def tr(w):
        return jnp.swapaxes(w, 1, 2)

    ffn_loc = jnp.stack([tr(ffn1_w_gate), tr(ffn1_w_up), ffn1_w_down, tr(ffn2_w_gate), tr(ffn2_w_up), ffn2_w_down],
                        axis=1).astype(BF16)
    loc = [ffn_loc, tr(w_in).astype(BF16), tr(mla_w_uq).astype(BF16), mla_w_ukv.astype(BF16), w_out.astype(BF16)]
    loc_shapes, loc_dtypes = [t.shape[1:] for t in loc], [t.dtype for t in loc]

    def layer_gather(l):
        return _GatherJob(loc, [lambda r, l=l: r.at[l]] * len(loc), loc_shapes, loc_dtypes, split_last=(1,))

    half_ffn = (ffn_loc.shape[1] // 2,) + ffn_loc.shape[2:]
    first = _GatherJob([ffn_loc, lru_conv_w, meta_tokens], [lambda r: r.at[0, pl.ds(0, half_ffn[0])], lambda r: r, lambda r: r],
                       [half_ffn, lru_conv_w.shape, meta_tokens.shape], [BF16, F32, F32], split_last=(0,))
    (w_ffn1_l0, g_conv, g_meta), = _run_jobs([first], "gather_first")
    rest_l0 = _GatherJob(loc, [lambda r: r.at[0, pl.ds(half_ffn[0], half_ffn[0])]] + [lambda r: r.at[0]] * (len(loc) - 1),
                         [half_ffn] + loc_shapes[1:], loc_dtypes, split_last=(0, 1))
    o0, o1, o2, o3 = lw, 2 * lw, 2 * lw + ql, 2 * lw + ql + kvl

    def assemble(gw, ffn1=None):
        w_ffn, g_win, g_wuq, g_wukv, g_wout = gw
        w_in_t = g_win.reshape(-1, d)
        w_kr = jnp.pad(w_in_t[o3:], ((QK_NOPE, 0), (0, 0)))
        w_ukv3 = jnp.transpose(g_wukv, (1, 0, 2)).reshape(kvl, nh, QK_NOPE + vd)
        w_out_full = g_wout.reshape(d, d)
        return dict(ffn1=ffn1 or (w_ffn, 0), ffn2=(w_ffn, 0 if ffn1 else 3),
                    parts=(w_in_t[:o0], w_in_t[o0:o1], w_in_t[o1:o2], w_in_t[o2:o3], w_kr),
                    q=g_wuq.reshape(nh, QK_DIM, ql),
                    k=jnp.transpose(jnp.pad(w_ukv3[..., :QK_NOPE], ((0, 0), (0, 0), (0, QK_ROPE))), (1, 0, 2)),
                    v=jnp.transpose(w_ukv3[..., QK_NOPE:], (1, 0, 2)),
                    lo=w_out_full[:lw], mo=w_out_full[lw:].reshape(nh, vd, d))

    conv_full = jnp.transpose(g_conv, (1, 2, 0, 3)).reshape(nl, CONV_WIDTH, lw)
    meta_full = jnp.transpose(g_meta, (1, 0, 2)).reshape(N_META, d)
    wa_bd = _block_diag(lru_w_a).astype(BF16)
    wx_bd = _block_diag(lru_w_x).astype(BF16)

    def gain3(g):
        return g.reshape(nl, 1, g.shape[-1])

    g3 = {n: gain3(a[n]) for n in ('ffn1_pre_g', 'ffn1_post_g', 'mix_pre_g', 'lru_conv_b', 'lru_b_a', 'lru_b_x', 'lru_lambda',
                                   'mla_q_norm_g', 'mla_kv_norm_g', 'lru_out_g', 'mix_post_g', 'ffn2_pre_g', 'ffn2_post_g')}
    g_mo = mla_out_g.reshape(nl, nh, 1, vd)

    pos = jnp.arange(tp, dtype=F32)
    inv_freq = 1.0 / (ROPE_THETA ** (jnp.arange(0, QK_ROPE, 2, dtype=F32) / QK_ROPE))
    ang = pos[:, None] * inv_freq[None, :]
    cos_t = jnp.concatenate([jnp.ones((tp, QK_NOPE), F32), jnp.cos(ang), jnp.cos(ang)], axis=1)
    sin_t = jnp.concatenate([jnp.zeros((tp, QK_NOPE), F32), jnp.sin(ang), jnp.sin(ang)], axis=1)
    half = QK_ROPE // 2
    rot = np.zeros((QK_DIM, QK_DIM), np.float32)
    for i in range(half):
        rot[QK_NOPE + half + i, QK_NOPE + i] = -1.0
        rot[QK_NOPE + i, QK_NOPE + half + i] = 1.0
    rot_b, rot_t = jnp.asarray(rot, BF16), jnp.asarray(rot.T, BF16)

    h = jnp.concatenate([meta_full, x2, jnp.zeros((tp - t_real, d), F32)], axis=0)
    target = jnp.pad(loss_target[0], ((N_META, tp - t_real), (0, 0)))
    saved, weights = [], []
    for l in range(nl):
        if l == 0:
            h1, gate1, up1, f1, gathered = _ffn_fwd(h, g3['ffn1_pre_g'], g3['ffn1_post_g'], w_ffn1_l0, l, 0, 1, 2, tm_ffn_fwd,
                                                    f"ffn1_fwd_{l}", rest_l0)
            wl = assemble(gathered, ffn1=(w_ffn1_l0, 0))
        else:
            wl = assemble(gathered)
            wf, k0 = wl['ffn1']
            h1, gate1, up1, f1, _ = _ffn_fwd(h, g3['ffn1_pre_g'], g3['ffn1_post_g'], wf, l, k0, k0 + 1, k0 + 2, tm_ffn_fwd,
                                             f"ffn1_fwd_{l}")
        weights.append(wl)
        xr, gr, cq, ckv, q, k, v = _mix_in_fwd(h1, g3['mix_pre_g'], wl['parts'], g3['mla_q_norm_g'], g3['mla_kv_norm_g'],
                                               wl['q'], wl['k'], wl['v'], cos_t, sin_t, rot_b, l, tm, f"mix_in_fwd_{l}")
        y_lru, xc, hs = _lru_fwd(xr, gr, conv_full, g3['lru_conv_b'], wa_bd, wx_bd, g3['lru_b_a'], g3['lru_b_x'],
                                 g3['lru_lambda'], l, tm, f"lru_fwd_{l}")
        o, lse, gathered = _attn_fwd(q, k, v, tq, f"attn_fwd_{l}", layer_gather(l + 1) if l + 1 < nl else None)
        h2, y = _mix_out_fwd(h1, y_lru, o, g3['lru_out_g'], g_mo, wl['lo'], wl['mo'], g3['mix_post_g'], l, tm, f"mix_out_fwd_{l}")
        wf, k0 = wl['ffn2']
        h3, gate2, up2, f2, _ = _ffn_fwd(h2, g3['ffn2_pre_g'], g3['ffn2_post_g'], wf, l, k0, k0 + 1, k0 + 2, tm_ffn_fwd,
                                         f"ffn2_fwd_{l}")
        saved.append((h, gate1, up1, f1, h1, xr, gr, cq, ckv, q, k, v, y_lru, xc, hs, o, lse, y, h2, gate2, up2, f2))
        h = h3

    dh, loss_parts = _loss_head(h, target, s_len, tm, "loss_head")
    loss = lax.psum(jnp.sum(loss_parts[:, 0, 0]), ("x", "y", "c"))

    small = {n: [None] * nl for n in _REPLICATED + ['lru_conv_w']}
    planes = {}
    late_names = ['w_in', 'mla_w_uq', 'mla_w_ukv', 'ffn1_w_gate', 'ffn1_w_up', 'ffn1_w_down']
    early_names = ['ffn2_w_gate', 'ffn2_w_up', 'ffn2_w_down', 'w_out']
    late = None

    def keep_planes(keys, outs):
        for i, key in enumerate(keys):
            planes[key] = (outs[i], outs[len(keys) + i])

    for l in reversed(range(nl)):
        (h0, gate1, up1, f1, h1, xr, gr, cq, ckv, q, k, v, y_lru, xc, hs, o, lse, y, h2, gate2, up2, f2) = saved[l]
        wl = weights[l]
        wf, k0 = wl['ffn2']
        dh, dgate, dup, act, df, u, dgpre, dgpost = _ffn_bwd(dh, h2, f2, gate2, up2, g3['ffn2_pre_g'], g3['ffn2_post_g'],
                                                            wf, l, k0, k0 + 1, k0 + 2, tm_ffn_bwd, f"ffn2_bwd_{l}")
        small['ffn2_pre_g'][l], small['ffn2_post_g'][l] = dgpre, dgpost
        d_wg2 = _mm_tn(dgate, u, tm, BF16, f"dw_gate2_{l}")
        d_wu2 = _mm_tn(dup, u, tm, BF16, f"dw_up2_{l}")
        dn2 = _mm_tn(act, df, tm, BF16, f"dw_down2_{l}")

        d_ylru, d_o, dy_b, n_lo, n_mo, dgpost, dglo, dgmo = _mix_out_bwd(dh, y, y_lru, o, g3['lru_out_g'], g_mo, wl['lo'], wl['mo'],
                                                                       g3['mix_post_g'], l, tm, f"mix_out_bwd_{l}")
        small['mix_post_g'][l], small['lru_out_g'][l], small['mla_out_g'][l] = dgpost, dglo, dgmo
        d_wlo = _mm_tn(n_lo, dy_b, tm, F32, f"dw_out_lru_{l}")
        d_wmo = _mm_tn_heads(n_mo, dy_b, tm, f"dw_out_mla_{l}")
        d_wout = jnp.concatenate([d_wlo, d_wmo.reshape(nh * vd, d)], axis=0).reshape(N_CHIPS, d // N_CHIPS, d).astype(BF16)

        keys = [(n, l) for n in early_names] + ([(n, l + 1) for n in late_names] if late else [])
        dq, dk, dv, outs = _attn_bwd(q, k, v, o, d_o, lse, tq, f"attn_bwd_{l}",
                                     _ReduceJob([d_wg2, d_wu2, dn2, d_wout] + (late or [])))
        keep_planes(keys, outs)
        d_xr, d_gr, dcw, dcb, dwa, dwx, dba, dbx, dlam = _lru_bwd(d_ylru, xr, gr, xc, hs, conv_full, wa_bd, wx_bd, g3['lru_b_a'],
                                                                 g3['lru_b_x'], g3['lru_lambda'], l, tm, f"lru_bwd_{l}")
        small['lru_conv_w'][l], small['lru_conv_b'][l] = dcw, dcb
        small['lru_w_a'][l], small['lru_w_x'][l] = _diag_blocks(dwa, LRU_HEADS), _diag_blocks(dwx, LRU_HEADS)
        small['lru_b_a'][l], small['lru_b_x'][l], small['lru_lambda'][l] = dba, dbx, dlam

        dh, u, qn, kvn, dq_pre, d_cq, d_ckv, d_kr, dgpre, dgq, dgkv = _mix_in_bwd(
            dh, h1, cq, ckv, d_xr, d_gr, dq, dk, dv, g3['mix_pre_g'], wl['parts'], g3['mla_q_norm_g'], g3['mla_kv_norm_g'],
            wl['q'], wl['k'], wl['v'], cos_t, sin_t, rot_t, l, tmb, f"mix_in_bwd_{l}")
        small['mix_pre_g'][l], small['mla_q_norm_g'][l], small['mla_kv_norm_g'][l] = dgpre, dgq, dgkv
        d_win = jnp.concatenate([_mm_tn(d_xr, u, tm, F32, f"dw_in_xr_{l}"), _mm_tn(d_gr, u, tm, F32, f"dw_in_gr_{l}"),
                                 _mm_tn(d_cq, u, tm, F32, f"dw_in_cq_{l}"), _mm_tn(d_ckv, u, tm, F32, f"dw_in_ckv_{l}"),
                                 _mm_tn(d_kr, u, tm, F32, f"dw_in_kr_{l}")[QK_NOPE:]], axis=0)
        d_win = d_win.reshape(N_CHIPS, -1, d).astype(BF16)
        d_wuq = _mm_tn_heads(dq_pre, qn, tm, f"dw_uq_{l}").reshape(N_CHIPS, -1, ql).astype(BF16)
        d_wk = _mm_tn_heads(dk, kvn, tm, f"dw_uk_{l}")[:, :QK_NOPE]
        d_wv = _mm_tn_heads(dv, kvn, tm, f"dw_uv_{l}")
        d_wukv = jnp.transpose(jnp.concatenate([d_wk, d_wv], axis=1).reshape(N_CHIPS, -1, kvl), (0, 2, 1)).astype(BF16)

        wf, k0 = wl['ffn1']
        dh, dgate, dup, act, df, u, dgpre, dgpost = _ffn_bwd(dh, h0, f1, gate1, up1, g3['ffn1_pre_g'], g3['ffn1_post_g'],
                                                            wf, l, k0, k0 + 1, k0 + 2, tm_ffn_bwd, f"ffn1_bwd_{l}")
        small['ffn1_pre_g'][l], small['ffn1_post_g'][l] = dgpre, dgpost
        d_wg1 = _mm_tn(dgate, u, tm, BF16, f"dw_gate1_{l}")
        d_wu1 = _mm_tn(dup, u, tm, BF16, f"dw_up1_{l}")
        dn1 = _mm_tn(act, df, tm, BF16, f"dw_down1_{l}")
        late = [d_win, d_wuq, d_wukv, d_wg1, d_wu1, dn1]

    grad_x = dh[N_META:t_real][None]

    small_full = {n: jnp.stack(small[n]).reshape(a[n].shape if n != 'lru_conv_w' else conv_full.shape)
                  for n in _REPLICATED + ['lru_conv_w']}
    order = _REPLICATED + ['lru_conv_w']
    packed = _pack_rows([small_full[n] for n in order] + [dh[:N_META]])
    late_planes, (all_small,) = _run_jobs([_ReduceJob(late), _DeviceGatherJob(packed)], "reduce_last")
    keep_planes([(n, 0) for n in late_names], late_planes)

    res = {}
    stored_transposed = ('ffn1_w_gate', 'ffn1_w_up', 'ffn2_w_gate', 'ffn2_w_up', 'w_in', 'mla_w_uq')
    for name in early_names + late_names:
        view = tr if name in stored_transposed else (lambda t: t)
        w_v, m_v, v_v = view(a[name]), view(a['m_' + name]), view(a['v_' + name])
        prev = None
        for l in reversed(range(nl)):
            same_plane, other_plane = planes[(name, l)]
            prev = _adamw_layer(w_v, m_v, v_v, same_plane, other_plane, l, prev, f"adamw_{name}_{l}")
        res[name] = [view(t) for t in prev]

    summed = _sum_devices(all_small, "sum_small")
    pieces = _unpack_rows(summed, [small_full[n].shape for n in order] + [(N_META, d)])
    g_small = dict(zip(order, pieces[:-1]))
    g_conv_loc = lax.dynamic_slice_in_dim(g_small.pop('lru_conv_w'), chip * lru_conv_w.shape[-1], lru_conv_w.shape[-1], axis=2)
    g_meta_loc = lax.dynamic_slice_in_dim(pieces[-1], chip * meta_tokens.shape[-1], meta_tokens.shape[-1], axis=1)

    shapes = [a[n].shape for n in _REPLICATED]
    dl, mo, vo = _adamw_whole(_pack_rows([a[n] for n in _REPLICATED]), _pack_rows([g_small[n] for n in _REPLICATED]),
                              _pack_rows([a['m_' + n] for n in _REPLICATED]), _pack_rows([a['v_' + n] for n in _REPLICATED]),
                              "adamw_replicated")
    for n, dd, mm, vv in zip(_REPLICATED, _unpack_rows(dl, shapes), _unpack_rows(mo, shapes), _unpack_rows(vo, shapes)):
        res[n] = (g_small[n], dd, mm, vv)
    cshape = lru_conv_w.shape
    c2 = (cshape[0] * cshape[1], cshape[2])
    dd, mm, vv = _adamw_whole(lru_conv_w.reshape(c2), g_conv_loc.reshape(c2), m_lru_conv_w.reshape(c2), v_lru_conv_w.reshape(c2),
                              "adamw_conv_w")
    res['lru_conv_w'] = (g_conv_loc, dd.reshape(cshape), mm.reshape(cshape), vv.reshape(cshape))
    res['meta_tokens'] = (g_meta_loc,) + tuple(_adamw_whole(meta_tokens, g_meta_loc, m_meta_tokens, v_meta_tokens, "adamw_meta"))

    return (loss, grad_x, *[res[n][0] for n in _W_NAMES], *[res[n][1] for n in _W_NAMES],
            *[res[n][2] for n in _W_NAMES], *[res[n][3] for n in _W_NAMES])
```

```python
import functools
import math

import jax
import jax.numpy as jnp
import numpy as np
from jax import lax
from jax.experimental import pallas as pl
from jax.experimental.pallas import tpu as pltpu

F32 = jnp.float32
BF16 = jnp.bfloat16
MESH = pl.DeviceIdType.MESH

EPS = 1e-6
N_META = 16
LRU_HEADS = 8
MLA_HEADS = 8
QK_NOPE = 64
QK_ROPE = 32
QK_DIM = QK_NOPE + QK_ROPE
LRU_C = 8.0
ROPE_THETA = 10000.0
CONV_WIDTH = 4
N_CHIPS = 4

ADAM_LR = 0.001
ADAM_B1 = 0.9
ADAM_B2 = 0.999
ADAM_EPS = 1e-08
ADAM_WD = 0.01
ADAM_STEP = 10

V7X_VMEM_LIMIT_BYTES = 56 * 1024 * 1024
NEG_BIG = -1e30
Q_PRESCALE = QK_DIM ** -0.5 * math.log2(math.e)


def _cparams(sem=None):
    return pltpu.CompilerParams(dimension_semantics=sem, vmem_limit_bytes=V7X_VMEM_LIMIT_BYTES)


def _round_up(a, b):
    return -(-a // b) * b


def _whole(*shape):
    return pl.BlockSpec(shape, lambda *_: (0,) * len(shape))


_ANY = pl.BlockSpec(memory_space=pl.ANY)


def _dot(a, b):
    return jnp.dot(a.astype(BF16), b.astype(BF16), preferred_element_type=F32)


def _dot_nt(a, b):
    return lax.dot_general(a.astype(BF16), b.astype(BF16), (((1,), (1,)), ((), ())), preferred_element_type=F32)


def _dot_tn(a, b):
    return lax.dot_general(a.astype(BF16), b.astype(BF16), (((0,), (0,)), ((), ())), preferred_element_type=F32)


def _dot_split(x, p):
    hi = x.astype(BF16)
    lo = (x - hi.astype(F32)).astype(BF16)
    return jnp.dot(hi, p, preferred_element_type=F32) + jnp.dot(lo, p, preferred_element_type=F32)


def _rms(x, g):
    r = lax.rsqrt(jnp.mean(x * x, axis=-1, keepdims=True) + EPS)
    return x * r * g, r


def _rms_bwd(x, g, dy):
    r = lax.rsqrt(jnp.mean(x * x, axis=-1, keepdims=True) + EPS)
    xh = x * r
    dyg = dy * g
    dx = r * (dyg - xh * jnp.mean(dyg * xh, axis=-1, keepdims=True))
    dg = jnp.sum(dy * xh, axis=0, keepdims=True)
    return dx, dg


def _sigmoid(x):
    return 1.0 / (1.0 + jnp.exp(-x))


def _neg_expm1(x):
    series = -x * (1.0 + x * (0.5 + x * (1.0 / 6.0 + x * (1.0 / 24.0))))
    return jnp.where(jnp.abs(x) < 0.03, series, 1.0 - jnp.exp(x))


_GELU_K = math.sqrt(2.0 / math.pi)


def _gelu(x):
    t = jnp.tanh(_GELU_K * (x + 0.044715 * x * x * x))
    return 0.5 * x * (1.0 + t), t


def _gelu_grad(x, t):
    return 0.5 * (1.0 + t) + 0.5 * x * (1.0 - t * t) * _GELU_K * (1.0 + 3.0 * 0.044715 * x * x)


def _rope(x, cos_t, sin_t, rot):
    return x * cos_t + _dot_split(x, rot) * sin_t


def _ffn_fwd(h, g_pre, g_post, w_ffn, layer, kind_gate, kind_up, kind_dn, tm, name, job=None):
    tp, d = h.shape
    n_slot, fs = w_ffn.shape[0], w_ffn.shape[-2]
    n_tok = tp // tm
    n_ji = len(job.inputs) if job else 0
    n_jo = len(job.out_shape) if job else 0

    def body(h_ref, gpre_ref, gpost_ref, wg_ref, wu_ref, wd_ref, *rest):
        job_in, (hout_ref, gate_ref, up_ref, f_ref), rest = rest[:n_ji], rest[n_ji:n_ji + 4], rest[n_ji + 4:]
        job_out, (u_sc, acc_sc), sems = rest[:n_jo], rest[n_jo:n_jo + 2], rest[n_jo + 2:]
        j = pl.program_id(1)
        if job:
            pl.when((pl.program_id(0) == 0) & (j == 0))(lambda: job.start(job_in, job_out, sems))

        @pl.when(j == 0)
        def _():
            u, _ = _rms(h_ref[...], gpre_ref[...])
            u_sc[...] = u.astype(BF16)
            acc_sc[...] = jnp.zeros_like(acc_sc)

        u = u_sc[...]
        gate = _dot_nt(u, wg_ref[...])
        up = _dot_nt(u, wu_ref[...])
        act = gate * _sigmoid(gate) * up
        gate_ref[...] = gate.astype(BF16)
        up_ref[...] = up.astype(BF16)
        acc_sc[...] += jnp.dot(act.astype(BF16), wd_ref[...], preferred_element_type=F32)

        @pl.when(j == n_slot - 1)
        def _():
            f = acc_sc[...]
            f_ref[...] = f
            n, _ = _rms(f, gpost_ref[...])
            hout_ref[...] = h_ref[...] + 0.5 * n

        if job:
            pl.when((pl.program_id(0) == n_tok - 1) & (j == n_slot - 1))(lambda: job.finish(job_in, job_out, sems))

    tok = pl.BlockSpec((tm, d), lambda i, j: (i, 0))
    gain = pl.BlockSpec((None, 1, d), lambda i, j: (layer, 0, 0))
    slot_act = pl.BlockSpec((None, tm, fs), lambda i, j: (j, i, 0))
    outs = pl.pallas_call(
        body, name=name, grid=(n_tok, n_slot),
        in_specs=[tok, gain, gain] + [pl.BlockSpec((None, None, fs, d), lambda i, j, kind=kind: (j, kind, 0, 0))
                                      for kind in (kind_gate, kind_up, kind_dn)] + [_ANY] * n_ji,
        out_specs=[tok, slot_act, slot_act, tok] + [_ANY] * n_jo,
        out_shape=[jax.ShapeDtypeStruct((tp, d), F32), jax.ShapeDtypeStruct((n_slot, tp, fs), BF16),
                   jax.ShapeDtypeStruct((n_slot, tp, fs), BF16), jax.ShapeDtypeStruct((tp, d), F32)]
        + (list(job.out_shape) if job else []),
        scratch_shapes=[pltpu.VMEM((tm, d), BF16), pltpu.VMEM((tm, d), F32)] + (list(job.scratch) if job else []),
        compiler_params=_cparams(("arbitrary", "arbitrary") if job else ("parallel", "arbitrary")),
    )(h, g_pre, g_post, w_ffn, w_ffn, w_ffn, *(job.inputs if job else ()))
    return outs[0], outs[1], outs[2], outs[3], list(outs[4:])


def _ffn_bwd(dh_out, h, f, gate, up, g_pre, g_post, w_ffn, layer, kind_gate, kind_up, kind_dn, tm, name, job=None):
    tp, d = h.shape
    n_slot, fs = w_ffn.shape[0], w_ffn.shape[-2]
    n_tok = tp // tm
    n_ji = len(job.inputs) if job else 0
    n_jo = len(job.out_shape) if job else 0

    def body(dho_ref, h_ref, f_ref, gate_ref, up_ref, gpre_ref, gpost_ref, wg_ref, wu_ref, wd_ref, *rest):
        job_in, rest = rest[:n_ji], rest[n_ji:]
        (dh_ref, dgate_ref, dup_ref, act_ref, df_ref, u_ref, dgpre_ref, dgpost_ref), rest = rest[:8], rest[8:]
        job_out, du_sc, sems = rest[:n_jo], rest[n_jo], rest[n_jo + 1:]
        i, j = pl.program_id(0), pl.program_id(1)
        if job:
            pl.when((i == 0) & (j == 0))(lambda: job.start(job_in, job_out, sems))

        @pl.when((i == 0) & (j == 0))
        def _():
            dgpre_ref[...] = jnp.zeros_like(dgpre_ref)
            dgpost_ref[...] = jnp.zeros_like(dgpost_ref)

        @pl.when(j == 0)
        def _():
            df, dg = _rms_bwd(f_ref[...], gpost_ref[...], 0.5 * dho_ref[...])
            df_ref[...] = df.astype(BF16)
            dgpost_ref[...] += dg
            u, _ = _rms(h_ref[...], gpre_ref[...])
            u_ref[...] = u.astype(BF16)
            du_sc[...] = jnp.zeros_like(du_sc)

        g = gate_ref[...].astype(F32)
        u_ = up_ref[...].astype(F32)
        sg = _sigmoid(g)
        silu = g * sg
        dact = _dot_nt(df_ref[...], wd_ref[...])
        dup = dact * silu
        dgate = dact * u_ * (sg * (1.0 + g * (1.0 - sg)))
        act_ref[...] = (silu * u_).astype(BF16)
        dup_b = dup.astype(BF16)
        dgate_b = dgate.astype(BF16)
        dup_ref[...] = dup_b
        dgate_ref[...] = dgate_b
        du_sc[...] += _dot(dgate_b, wg_ref[...]) + _dot(dup_b, wu_ref[...])

        @pl.when(j == n_slot - 1)
        def _():
            dx, dg = _rms_bwd(h_ref[...], gpre_ref[...], du_sc[...])
            dh_ref[...] = dho_ref[...] + dx
            dgpre_ref[...] += dg

        if job:
            pl.when((i == n_tok - 1) & (j == n_slot - 1))(lambda: job.finish(job_in, job_out, sems))

    tok = pl.BlockSpec((tm, d), lambda i, j: (i, 0))
    gain = pl.BlockSpec((None, 1, d), lambda i, j: (layer, 0, 0))
    acc = pl.BlockSpec((1, d), lambda i, j: (0, 0))
    slot_act = pl.BlockSpec((None, tm, fs), lambda i, j: (j, i, 0))
    act_shape = jax.ShapeDtypeStruct((n_slot, tp, fs), BF16)
    outs = pl.pallas_call(
        body, name=name, grid=(n_tok, n_slot),
        in_specs=[tok, tok, tok, slot_act, slot_act, gain, gain]
        + [pl.BlockSpec((None, None, fs, d), lambda i, j, kind=kind: (j, kind, 0, 0)) for kind in (kind_gate, kind_up, kind_dn)]
        + [_ANY] * n_ji,
        out_specs=[tok, slot_act, slot_act, slot_act, tok, tok, acc, acc] + [_ANY] * n_jo,
        out_shape=[jax.ShapeDtypeStruct((tp, d), F32), act_shape, act_shape, act_shape,
                   jax.ShapeDtypeStruct((tp, d), BF16), jax.ShapeDtypeStruct((tp, d), BF16),
                   jax.ShapeDtypeStruct((1, d), F32), jax.ShapeDtypeStruct((1, d), F32)] + (list(job.out_shape) if job else []),
        scratch_shapes=[pltpu.VMEM((tm, d), F32)] + (list(job.scratch) if job else []),
        compiler_params=_cparams(("arbitrary", "arbitrary")),
    )(dh_out, h, f, gate, up, g_pre, g_post, w_ffn, w_ffn, w_ffn, *(job.inputs if job else ()))
    return tuple(outs[:8]) + (list(outs[8:]),)


def _mm_tn(a, b, tk, out_dtype, name, job=None):
    ga = a.shape[0] if a.ndim == 3 else None
    gb = b.shape[0] if b.ndim == 3 else None
    groups = ga or gb or 1
    t, m = a.shape[-2:]
    n = b.shape[-1]
    nk = t // tk
    n_ji = len(job.inputs) if job else 0
    n_jo = len(job.out_shape) if job else 0

    def body(a_ref, b_ref, *rest):
        job_in, o_ref, rest = rest[:n_ji], rest[n_ji], rest[n_ji + 1:]
        job_out, acc_sc, sems = rest[:n_jo], rest[n_jo], rest[n_jo + 1:]
        k = pl.program_id(1)
        if job:
            pl.when((pl.program_id(0) == 0) & (k == 0))(lambda: job.start(job_in, job_out, sems))

        @pl.when(k == 0)
        def _():
            acc_sc[...] = jnp.zeros_like(acc_sc)

        acc_sc[...] += _dot_tn(a_ref[...], b_ref[...])

        @pl.when(k == nk - 1)
        def _():
            o_ref[...] = acc_sc[...].astype(out_dtype)

        if job:
            pl.when((pl.program_id(0) == groups - 1) & (k == nk - 1))(lambda: job.finish(job_in, job_out, sems))

    a_spec = (pl.BlockSpec((None, tk, m), lambda g, k: (g, k, 0)) if ga else pl.BlockSpec((tk, m), lambda g, k: (k, 0)))
    b_spec = (pl.BlockSpec((None, tk, n), lambda g, k: (g, k, 0)) if gb else pl.BlockSpec((tk, n), lambda g, k: (k, 0)))
    outs = pl.pallas_call(
        body, name=name, grid=(groups, nk),
        in_specs=[a_spec, b_spec] + [_ANY] * n_ji,
        out_specs=[pl.BlockSpec((None, m, n), lambda g, k: (g, 0, 0))] + [_ANY] * n_jo,
        out_shape=[jax.ShapeDtypeStruct((groups, m, n), out_dtype)] + (list(job.out_shape) if job else []),
        scratch_shapes=[pltpu.VMEM((m, n), F32)] + (list(job.scratch) if job else []),
        compiler_params=_cparams(("arbitrary", "arbitrary") if job else ("parallel", "arbitrary")),
    )(a, b, *(job.inputs if job else ()))
    out = outs[0] if (ga or gb) else outs[0][0]
    return (out, list(outs[1:])) if job else out


def _mm_tn_heads(a, b, tk, name):
    groups, t, m = a.shape
    n = b.shape[-1]
    nk = t // tk

    def body(a_ref, b_ref, o_ref):
        k = pl.program_id(0)

        @pl.when(k == 0)
        def _():
            o_ref[...] = jnp.zeros_like(o_ref)

        bb = b_ref[...]
        for g in range(groups):
            o_ref[g] += _dot_tn(a_ref[g], bb)

    return pl.pallas_call(
        body, name=name, grid=(nk,),
        in_specs=[pl.BlockSpec((groups, tk, m), lambda k: (0, k, 0)), pl.BlockSpec((tk, n), lambda k: (k, 0))],
        out_specs=pl.BlockSpec((groups, m, n), lambda k: (0, 0, 0)),
        out_shape=jax.ShapeDtypeStruct((groups, m, n), F32),
        compiler_params=_cparams(("arbitrary",)),
    )(a, b)


def _mix_in_fwd(h, g_pre, w_parts, g_q, g_kv, w_q, w_k, w_v, cos_t, sin_t, rot, layer, tm, name):
    tp, d = h.shape
    w_xr, w_gr, w_cq, w_ckv, w_kr = w_parts
    lw, ql, kvl = w_xr.shape[0], w_cq.shape[0], w_ckv.shape[0]
    nh, vd = w_v.shape[0], w_v.shape[-1]

    def body(h_ref, gpre_ref, wxr_ref, wgr_ref, wcq_ref, wckv_ref, wkr_ref, gq_ref, gkv_ref, wq_ref, wk_ref, wv_ref,
             cos_ref, sin_ref, rot_ref, xr_ref, gr_ref, cq_ref, ckv_ref, q_ref, k_ref, v_ref):
        u, _ = _rms(h_ref[...], gpre_ref[...])
        u = u.astype(BF16)
        xr_ref[...] = _dot_nt(u, wxr_ref[...])
        gr_ref[...] = _dot_nt(u, wgr_ref[...])
        cq = _dot_nt(u, wcq_ref[...])
        ckv = _dot_nt(u, wckv_ref[...])
        kr = _dot_nt(u, wkr_ref[...])
        cq_ref[...] = cq
        ckv_ref[...] = ckv
        cos_b, sin_b, rot_b = cos_ref[...], sin_ref[...], rot_ref[...]
        qn = _rms(cq, gq_ref[...])[0].astype(BF16)
        kvn = _rms(ckv, gkv_ref[...])[0].astype(BF16)
        k_rope = _rope(kr, cos_b, sin_b, rot_b)
        for hd in range(nh):
            q_pre = _dot_nt(qn, wq_ref[hd])
            q_ref[hd] = (_rope(q_pre, cos_b, sin_b, rot_b) * Q_PRESCALE).astype(BF16)
            k_ref[hd] = (jnp.dot(kvn, wk_ref[hd], preferred_element_type=F32) + k_rope).astype(BF16)
            v_ref[hd] = jnp.dot(kvn, wv_ref[hd], preferred_element_type=F32).astype(BF16)

    def tok(n):
        return pl.BlockSpec((tm, n), lambda i: (i, 0))

    def lay(*shape):
        return pl.BlockSpec((None,) + shape, lambda i: (layer,) + (0,) * len(shape))

    def heads(n):
        return pl.BlockSpec((nh, tm, n), lambda i: (0, i, 0))

    return pl.pallas_call(
        body, name=name, grid=(tp // tm,),
        in_specs=[tok(d), lay(1, d), _whole(lw, d), _whole(lw, d), _whole(ql, d), _whole(kvl, d), _whole(QK_DIM, d),
                  lay(1, ql), lay(1, kvl),
                  _whole(nh, QK_DIM, ql), _whole(nh, kvl, QK_DIM), _whole(nh, kvl, vd), tok(QK_DIM), tok(QK_DIM),
                  pl.BlockSpec((QK_DIM, QK_DIM), lambda i: (0, 0))],
        out_specs=[tok(lw), tok(lw), tok(ql), tok(kvl), heads(QK_DIM), heads(QK_DIM), heads(vd)],
        out_shape=[jax.ShapeDtypeStruct((tp, lw), F32), jax.ShapeDtypeStruct((tp, lw), F32),
                   jax.ShapeDtypeStruct((tp, ql), F32), jax.ShapeDtypeStruct((tp, kvl), F32),
                   jax.ShapeDtypeStruct((nh, tp, QK_DIM), BF16), jax.ShapeDtypeStruct((nh, tp, QK_DIM), BF16),
                   jax.ShapeDtypeStruct((nh, tp, vd), BF16)],
        compiler_params=_cparams(("parallel",)),
    )(h, g_pre, w_xr, w_gr, w_cq, w_ckv, w_kr, g_q, g_kv, w_q, w_k, w_v, cos_t, sin_t, rot)


def _mix_in_bwd(dh_res, h, cq, ckv, d_xr, d_gr, dq, dk, dv, g_pre, w_parts, g_q, g_kv, w_q, w_k, w_v,
                cos_t, sin_t, rot_t, layer, tm, name):
    tp, d = h.shape
    w_xr, w_gr, w_cq, w_ckv, w_kr = w_parts
    lw, ql, kvl = w_xr.shape[0], w_cq.shape[0], w_ckv.shape[0]
    nh, vd = w_v.shape[0], w_v.shape[-1]

    def body(dhr_ref, h_ref, cq_ref, ckv_ref, dxr_ref, dgr_ref, dq_ref, dk_ref, dv_ref, gpre_ref,
             wxr_ref, wgr_ref, wcq_ref, wckv_ref, wkr_ref, gq_ref, gkv_ref, wq_ref, wk_ref, wv_ref,
             cos_ref, sin_ref, rott_ref,
             dh_ref, u_ref, qn_ref, kvn_ref, dqpre_ref, dcq_ref, dckv_ref, dkr_ref, dgpre_ref, dgq_ref, dgkv_ref):
        i = pl.program_id(0)

        @pl.when(i == 0)
        def _():
            dgpre_ref[...] = jnp.zeros_like(dgpre_ref)
            dgq_ref[...] = jnp.zeros_like(dgq_ref)
            dgkv_ref[...] = jnp.zeros_like(dgkv_ref)

        cos_b, sin_b, rott_b = cos_ref[...], sin_ref[...], rott_ref[...]
        hh = h_ref[...]
        u, _ = _rms(hh, gpre_ref[...])
        u_ref[...] = u.astype(BF16)
        cq, ckv = cq_ref[...], ckv_ref[...]
        qn = _rms(cq, gq_ref[...])[0]
        kvn = _rms(ckv, gkv_ref[...])[0]
        qn_ref[...] = qn.astype(BF16)
        kvn_ref[...] = kvn.astype(BF16)
        d_qn = jnp.zeros((tm, ql), F32)
        d_kvn = jnp.zeros((tm, kvl), F32)
        d_krope = jnp.zeros((tm, QK_DIM), F32)
        for hd in range(nh):
            dq_h = dq_ref[hd]
            dq_pre = dq_h * cos_b + _dot_split(dq_h * sin_b, rott_b)
            dq_pre_b = dq_pre.astype(BF16)
            dqpre_ref[hd] = dq_pre_b
            d_qn += _dot(dq_pre_b, wq_ref[hd])
            dk_h = dk_ref[hd]
            d_krope += dk_h
            d_kvn += _dot_nt(dk_h, wk_ref[hd]) + _dot_nt(dv_ref[hd], wv_ref[hd])
        d_kr = d_krope * cos_b + _dot_split(d_krope * sin_b, rott_b)
        d_cq, dgq = _rms_bwd(cq, gq_ref[...], d_qn)
        d_ckv, dgkv = _rms_bwd(ckv, gkv_ref[...], d_kvn)
        dgq_ref[...] += dgq
        dgkv_ref[...] += dgkv
        d_cq_b, d_ckv_b, d_kr_b = d_cq.astype(BF16), d_ckv.astype(BF16), d_kr.astype(BF16)
        dcq_ref[...] = d_cq_b
        dckv_ref[...] = d_ckv_b
        dkr_ref[...] = d_kr_b
        du = (_dot(dxr_ref[...], wxr_ref[...]) + _dot(dgr_ref[...], wgr_ref[...]) + _dot(d_cq_b, wcq_ref[...])
              + _dot(d_ckv_b, wckv_ref[...]) + _dot(d_kr_b, wkr_ref[...]))
        dx, dg = _rms_bwd(hh, gpre_ref[...], du)
        dh_ref[...] = dhr_ref[...] + dx
        dgpre_ref[...] += dg

    def tok(n):
        return pl.BlockSpec((tm, n), lambda i: (i, 0))

    def lay(*shape):
        return pl.BlockSpec((None,) + shape, lambda i: (layer,) + (0,) * len(shape))

    def heads(n):
        return pl.BlockSpec((nh, tm, n), lambda i: (0, i, 0))

    def acc(n):
        return pl.BlockSpec((1, n), lambda i: (0, 0))

    return pl.pallas_call(
        body, name=name, grid=(tp // tm,),
        in_specs=[tok(d), tok(d), tok(ql), tok(kvl), tok(lw), tok(lw), heads(QK_DIM), heads(QK_DIM), heads(vd), lay(1, d),
                  _whole(lw, d), _whole(lw, d), _whole(ql, d), _whole(kvl, d), _whole(QK_DIM, d), lay(1, ql), lay(1, kvl),
                  _whole(nh, QK_DIM, ql), _whole(nh, kvl, QK_DIM), _whole(nh, kvl, vd), tok(QK_DIM), tok(QK_DIM),
                  pl.BlockSpec((QK_DIM, QK_DIM), lambda i: (0, 0))],
        out_specs=[tok(d), tok(d), tok(ql), tok(kvl), heads(QK_DIM), tok(ql), tok(kvl), tok(QK_DIM), acc(d), acc(ql), acc(kvl)],
        out_shape=[jax.ShapeDtypeStruct((tp, d), F32), jax.ShapeDtypeStruct((tp, d), BF16),
                   jax.ShapeDtypeStruct((tp, ql), BF16), jax.ShapeDtypeStruct((tp, kvl), BF16),
                   jax.ShapeDtypeStruct((nh, tp, QK_DIM), BF16), jax.ShapeDtypeStruct((tp, ql), BF16),
                   jax.ShapeDtypeStruct((tp, kvl), BF16), jax.ShapeDtypeStruct((tp, QK_DIM), BF16),
                   jax.ShapeDtypeStruct((1, d), F32), jax.ShapeDtypeStruct((1, ql), F32), jax.ShapeDtypeStruct((1, kvl), F32)],
        compiler_params=_cparams(("arbitrary",)),
    )(dh_res, h, cq, ckv, d_xr, d_gr, dq, dk, dv, g_pre, w_xr, w_gr, w_cq, w_ckv, w_kr, g_q, g_kv, w_q, w_k, w_v,
      cos_t, sin_t, rot_t)


def _lru_gates(xc, wa_ref, wx_ref, ba, bx, sp):
    xcb = xc.astype(BF16)
    r = _sigmoid(jnp.dot(xcb, wa_ref[...], preferred_element_type=F32) + ba)
    ig = _sigmoid(jnp.dot(xcb, wx_ref[...], preferred_element_type=F32) + bx)
    log_a = -LRU_C * r * sp
    a = jnp.exp(log_a)
    sq = jnp.sqrt(_neg_expm1(2.0 * log_a))
    return r, ig, a, sq


def _softplus(x):
    return jnp.maximum(x, 0.0) + jnp.log(1.0 + jnp.exp(-jnp.abs(x)))


def _lru_fwd(xr, gr, conv_w, conv_b, wa_bd, wx_bd, b_a, b_x, lam, layer, tc, name):
    tp, w = xr.shape
    pad = 8

    def body(xr_ref, gr_ref, cw_ref, cb_ref, wa_ref, wx_ref, ba_ref, bx_ref, lam_ref, y_ref, xc_ref, hs_ref,
             xe_sc, a_sc, b_sc, st_sc):
        c = pl.program_id(0)

        @pl.when(c == 0)
        def _():
            xe_sc[pl.ds(0, pad), :] = jnp.zeros((pad, w), F32)
            st_sc[...] = jnp.zeros_like(st_sc)

        xe_sc[pl.ds(pad, tc), :] = xr_ref[...]
        xc = cb_ref[...] + xe_sc[pl.ds(pad, tc), :] * cw_ref[pl.ds(CONV_WIDTH - 1, 1), :]
        for k in range(CONV_WIDTH - 1):
            xc = xc + xe_sc[pl.ds(pad - (CONV_WIDTH - 1) + k, tc), :] * cw_ref[pl.ds(k, 1), :]
        xe_sc[pl.ds(0, pad), :] = xe_sc[pl.ds(tc, pad), :]
        xc_ref[...] = xc
        sp = _softplus(-lam_ref[...])
        _, ig, a, sq = _lru_gates(xc, wa_ref, wx_ref, ba_ref[...], bx_ref[...], sp)
        a_sc[...] = a
        b_sc[...] = sq * (ig * xc)

        def step(t, hcur):
            hnew = a_sc[pl.ds(t, 1), :] * hcur + b_sc[pl.ds(t, 1), :]
            hs_ref[pl.ds(t, 1), :] = hnew
            return hnew

        st_sc[...] = lax.fori_loop(0, tc, step, st_sc[...], unroll=8)
        y_ref[...] = hs_ref[...] * _gelu(gr_ref[...])[0]

    tok = pl.BlockSpec((tc, w), lambda c: (c, 0))

    def lay(*shape):
        return pl.BlockSpec((None,) + shape, lambda c: (layer,) + (0,) * len(shape))

    out = jax.ShapeDtypeStruct((tp, w), F32)
    return pl.pallas_call(
        body, name=name, grid=(tp // tc,),
        in_specs=[tok, tok, lay(CONV_WIDTH, w), lay(1, w), lay(w, w), lay(w, w), lay(1, w), lay(1, w), lay(1, w)],
        out_specs=[tok, tok, tok], out_shape=[out, out, out],
        scratch_shapes=[pltpu.VMEM((tc + pad, w), F32), pltpu.VMEM((tc, w), F32), pltpu.VMEM((tc, w), F32),
                        pltpu.VMEM((1, w), F32)],
        compiler_params=_cparams(("arbitrary",)),
    )(xr, gr, conv_w, conv_b, wa_bd, wx_bd, b_a, b_x, lam)


def _lru_bwd(dy, xr, gr, xc, hs, conv_w, wa_bd, wx_bd, b_a, b_x, lam, layer, tc, name):
    tp, w = xr.shape
    pad = 8
    nc = tp // tc
    per = tc // pad

    def body(dy_ref, xr_ref, gr_ref, xc_ref, hs_ref, xrp_ref, hsp_ref, cw_ref, wa_ref, wx_ref, ba_ref, bx_ref, lam_ref,
             dxr_ref, dgr_ref, dcw_ref, dcb_ref, dwa_ref, dwx_ref, dba_ref, dbx_ref, dlam_ref,
             a_sc, dh_sc, dxc_sc, he_sc, xe_sc, carry_sc):
        s = pl.program_id(0)
        first_chunk = s == nc - 1

        @pl.when(s == 0)
        def _():
            for ref in (dcw_ref, dcb_ref, dwa_ref, dwx_ref, dba_ref, dbx_ref, dlam_ref):
                ref[...] = jnp.zeros_like(ref)
            carry_sc[...] = jnp.zeros_like(carry_sc)
            dxc_sc[pl.ds(tc, pad), :] = jnp.zeros((pad, w), F32)

        keep = jnp.where(first_chunk, 0.0, 1.0)
        he_sc[pl.ds(0, pad), :] = hsp_ref[...] * keep
        he_sc[pl.ds(pad, tc), :] = hs_ref[...]
        xe_sc[pl.ds(0, pad), :] = xrp_ref[...] * keep
        xe_sc[pl.ds(pad, tc), :] = xr_ref[...]

        lam_v = lam_ref[...]
        sp = _softplus(-lam_v)
        xc = xc_ref[...]
        r, ig, a, sq = _lru_gates(xc, wa_ref, wx_ref, ba_ref[...], bx_ref[...], sp)
        a_sc[...] = a
        grv = gr_ref[...]
        gl, th = _gelu(grv)
        dyv = dy_ref[...]
        dgr_ref[...] = dyv * hs_ref[...] * _gelu_grad(grv, th)
        dh_sc[...] = dyv * gl

        def step(n, g):
            t = tc - 1 - n
            dh = dh_sc[pl.ds(t, 1), :] + g
            dh_sc[pl.ds(t, 1), :] = dh
            return a_sc[pl.ds(t, 1), :] * dh

        carry_sc[...] = lax.fori_loop(0, tc, step, carry_sc[...], unroll=8)

        dh = dh_sc[...]
        d_a = dh * he_sc[pl.ds(pad - 1, tc), :]
        d_ixc = dh * sq
        d_sq = dh * (ig * xc)
        a2 = a * a
        d_la = d_a * a - d_sq * a2 / sq
        d_r = d_la * (-LRU_C * sp)
        d_sp = jnp.sum(d_la * (-LRU_C * r), axis=0, keepdims=True)
        dlam_ref[...] += d_sp * (-_sigmoid(-lam_v))
        d_pa = d_r * r * (1.0 - r)
        d_px = d_ixc * xc * ig * (1.0 - ig)
        dba_ref[...] += jnp.sum(d_pa, axis=0, keepdims=True)
        dbx_ref[...] += jnp.sum(d_px, axis=0, keepdims=True)
        d_pa_b, d_px_b = d_pa.astype(BF16), d_px.astype(BF16)
        xcb = xc.astype(BF16)
        dwa_ref[...] += _dot_tn(xcb, d_pa_b)
        dwx_ref[...] += _dot_tn(xcb, d_px_b)
        d_xc = d_ixc * ig + _dot_nt(d_pa_b, wa_ref[...]) + _dot_nt(d_px_b, wx_ref[...])
        dcb_ref[...] += jnp.sum(d_xc, axis=0, keepdims=True)
        dxc_sc[pl.ds(0, tc), :] = d_xc
        d_xr = jnp.zeros((tc, w), F32)
        for k in range(CONV_WIDTH):
            off = CONV_WIDTH - 1 - k
            d_xr = d_xr + dxc_sc[pl.ds(off, tc), :] * cw_ref[pl.ds(k, 1), :]
            dcw_ref[pl.ds(k, 1), :] += jnp.sum(d_xc * xe_sc[pl.ds(pad - off, tc), :], axis=0, keepdims=True)
        dxr_ref[...] = d_xr
        dxc_sc[pl.ds(tc, pad), :] = dxc_sc[pl.ds(0, pad), :]

    def rev(cidx):
        return nc - 1 - cidx

    tok = pl.BlockSpec((tc, w), lambda c: (rev(c), 0))
    prev = pl.BlockSpec((pad, w), lambda c: (jnp.maximum(rev(c) * per - 1, 0), 0))

    def lay(*shape):
        return pl.BlockSpec((None,) + shape, lambda c: (layer,) + (0,) * len(shape))

    def acc(*shape):
        return pl.BlockSpec(shape, lambda c: (0,) * len(shape))

    big = jax.ShapeDtypeStruct((tp, w), F32)
    vec = jax.ShapeDtypeStruct((1, w), F32)
    return pl.pallas_call(
        body, name=name, grid=(nc,),
        in_specs=[tok, tok, tok, tok, tok, prev, prev, lay(CONV_WIDTH, w), lay(w, w), lay(w, w), lay(1, w), lay(1, w), lay(1, w)],
        out_specs=[tok, tok, acc(CONV_WIDTH, w), acc(1, w), acc(w, w), acc(w, w), acc(1, w), acc(1, w), acc(1, w)],
        out_shape=[big, big, jax.ShapeDtypeStruct((CONV_WIDTH, w), F32), vec, jax.ShapeDtypeStruct((w, w), F32),
                   jax.ShapeDtypeStruct((w, w), F32), vec, vec, vec],
        scratch_shapes=[pltpu.VMEM((tc, w), F32), pltpu.VMEM((tc, w), F32), pltpu.VMEM((tc + pad, w), F32),
                        pltpu.VMEM((tc + pad, w), F32), pltpu.VMEM((tc + pad, w), F32), pltpu.VMEM((1, w), F32)],
        compiler_params=_cparams(("arbitrary",)),
    )(dy, xr, gr, xc, hs, xr, hs, conv_w, wa_bd, wx_bd, b_a, b_x, lam)


def _attn_fwd(q, k, v, tq, name, job=None):
    nh, tp, dqk = q.shape
    vd = v.shape[-1]
    nq = tp // tq
    n_ji = len(job.inputs) if job else 0
    n_jo = len(job.out_shape) if job else 0

    def body(q_ref, k_ref, v_ref, *rest):
        job_in, (o_ref, lse_ref), rest = rest[:n_ji], rest[n_ji:n_ji + 2], rest[n_ji + 2:]
        job_out, sems = rest[:n_jo], rest[n_jo:]
        i = pl.program_id(1)
        if job:
            first = (pl.program_id(0) == 0) & (i == 0)
            pl.when(first)(lambda: job.start(job_in, job_out, sems))
        qb = q_ref[...]

        def absorb(j, width, carry, diagonal=False):
            m, l, acc = carry
            off = pl.multiple_of(j * tq, tq)
            s = _dot_nt(qb, k_ref[pl.ds(off, width), :])
            if diagonal:
                keep = lax.broadcasted_iota(jnp.int32, (tq, width), 1) <= lax.broadcasted_iota(jnp.int32, (tq, width), 0)
                s = jnp.where(keep, s, NEG_BIG)
            m_new = jnp.maximum(m, jnp.max(s, axis=-1, keepdims=True))
            p = jnp.exp2(s - m_new)
            alpha = jnp.exp2(m - m_new)
            l = alpha * l + jnp.sum(p, axis=-1, keepdims=True)
            acc = alpha * acc + jnp.dot(p.astype(BF16), v_ref[pl.ds(off, width), :], preferred_element_type=F32)
            return m_new, l, acc

        carry = (jnp.full((tq, 1), NEG_BIG, F32), jnp.zeros((tq, 1), F32), jnp.zeros((tq, vd), F32))
        carry = lax.fori_loop(0, i // 2, lambda jj, c: absorb(2 * jj, 2 * tq, c), carry)
        carry = lax.cond(i % 2 == 1, lambda c: absorb(i - 1, tq, c), lambda c: c, carry)
        m, l, acc = absorb(i, tq, carry, diagonal=True)
        o_ref[...] = acc / l
        lse_ref[...] = m + jnp.log2(l)
        if job:
            last = (pl.program_id(0) == nh - 1) & (i == nq - 1)
            pl.when(last)(lambda: job.finish(job_in, job_out, sems))

    outs = pl.pallas_call(
        body, name=name, grid=(nh, nq),
        in_specs=[pl.BlockSpec((None, tq, dqk), lambda h, i: (h, i, 0)),
                  pl.BlockSpec((None, tp, dqk), lambda h, i: (h, 0, 0)),
                  pl.BlockSpec((None, tp, vd), lambda h, i: (h, 0, 0))] + [_ANY] * n_ji,
        out_specs=[pl.BlockSpec((None, tq, vd), lambda h, i: (h, i, 0)),
                   pl.BlockSpec((None, tq, 1), lambda h, i: (h, i, 0))] + [_ANY] * n_jo,
        out_shape=[jax.ShapeDtypeStruct((nh, tp, vd), F32), jax.ShapeDtypeStruct((nh, tp, 1), F32)]
        + (list(job.out_shape) if job else []),
        scratch_shapes=list(job.scratch) if job else [],
        compiler_params=_cparams(("arbitrary", "arbitrary") if job else ("parallel", "parallel")),
    )(q, k, v, *(job.inputs if job else ()))
    return outs[0], outs[1], list(outs[2:])


def _attn_bwd(q, k, v, o, do, lse, tq, name, job=None):
    nh, tp, dqk = q.shape
    vd = v.shape[-1]
    scale = dqk ** -0.5
    nq = tp // tq
    n_ji = len(job.inputs) if job else 0
    n_jo = len(job.out_shape) if job else 0

    def body(q_ref, k_ref, v_ref, o_ref, do_ref, lse_ref, *rest):
        job_in, (dq_ref, dk_ref, dv_ref), rest = rest[:n_ji], rest[n_ji:n_ji + 3], rest[n_ji + 3:]
        job_out, (lse_rows, delta_rows, do_bf, dqt_sc), sems = rest[:n_jo], rest[n_jo:n_jo + 4], rest[n_jo + 4:]
        kb_i = pl.program_id(1)
        if job:
            first = (pl.program_id(0) == 0) & (kb_i == 0)
            pl.when(first)(lambda: job.start(job_in, job_out, sems))

        def as_rows(col):
            return jnp.transpose(jnp.broadcast_to(col, (tq, _LANES)))[0:8, :]

        @pl.when(kb_i == 0)
        def _():
            dqt_sc[...] = jnp.zeros_like(dqt_sc)

            def prep(qi, _):
                off = pl.multiple_of(qi * tq, tq)
                dob = do_ref[pl.ds(off, tq), :]
                do_bf[pl.ds(off, tq), :] = dob.astype(BF16)
                delta_rows[qi] = as_rows(jnp.sum(dob * o_ref[pl.ds(off, tq), :], axis=-1, keepdims=True))
                lse_rows[qi] = as_rows(lse_ref[pl.ds(off, tq), :])
                return 0

            lax.fori_loop(0, nq, prep, 0)

        kb = k_ref[...]
        vb = v_ref[...]

        def absorb(qi, n_blk, carry, diagonal=False):
            dk, dv = carry
            off = pl.multiple_of(qi * tq, tq)
            qb = q_ref[pl.ds(off, n_blk * tq), :]
            dob = do_bf[pl.ds(off, n_blk * tq), :]
            lse_row = jnp.concatenate([lse_rows[qi + b, 0:1, :] for b in range(n_blk)], axis=1)
            delta_row = jnp.concatenate([delta_rows[qi + b, 0:1, :] for b in range(n_blk)], axis=1)
            st = _dot_nt(kb, qb)
            if diagonal:
                keep = lax.broadcasted_iota(jnp.int32, (tq, tq), 0) <= lax.broadcasted_iota(jnp.int32, (tq, tq), 1)
                st = jnp.where(keep, st, NEG_BIG)
            pt = jnp.exp2(st - lse_row)
            dpt = _dot_nt(vb, dob)
            dst = (pt * (dpt - delta_row)).astype(BF16)
            dv = dv + jnp.dot(pt.astype(BF16), dob, preferred_element_type=F32)
            dk = dk + jnp.dot(dst, qb, preferred_element_type=F32)
            dqt = _dot_tn(kb, dst)
            for b in range(n_blk):
                dqt_sc[qi + b] += dqt[:, b * tq:(b + 1) * tq]
            return dk, dv

        carry = absorb(kb_i, 1, (jnp.zeros((tq, dqk), F32), jnp.zeros((tq, vd), F32)), diagonal=True)
        later = nq - 1 - kb_i
        carry = lax.fori_loop(0, later // 2, lambda pp, c: absorb(kb_i + 1 + 2 * pp, 2, c), carry)
        dk, dv = lax.cond(later % 2 == 1, lambda c: absorb(nq - 1, 1, c), lambda c: c, carry)
        dk_ref[...] = dk * math.log(2.0)
        dv_ref[...] = dv

        @pl.when(kb_i == nq - 1)
        def _():
            eye = (lax.broadcasted_iota(jnp.int32, (dqk, dqk), 0) == lax.broadcasted_iota(jnp.int32, (dqk, dqk), 1))
            eye = jnp.where(eye, 1.0, 0.0).astype(BF16)
            for qi in range(nq):
                t = dqt_sc[qi]
                hi = t.astype(BF16)
                lo = (t - hi.astype(F32)).astype(BF16)
                dq_ref[pl.ds(qi * tq, tq), :] = scale * (_dot_tn(hi, eye) + _dot_tn(lo, eye))

        if job:
            last = (pl.program_id(0) == nh - 1) & (kb_i == nq - 1)
            pl.when(last)(lambda: job.finish(job_in, job_out, sems))

    def full(n):
        return pl.BlockSpec((None, tp, n), lambda h, j: (h, 0, 0))

    def blk(n):
        return pl.BlockSpec((None, tq, n), lambda h, j: (h, j, 0))

    outs = pl.pallas_call(
        body, name=name, grid=(nh, nq),
        in_specs=[full(dqk), blk(dqk), blk(vd), full(vd), full(vd), full(1)] + [_ANY] * n_ji,
        out_specs=[full(dqk), blk(dqk), blk(vd)] + [_ANY] * n_jo,
        out_shape=[jax.ShapeDtypeStruct((nh, tp, dqk), F32), jax.ShapeDtypeStruct((nh, tp, dqk), F32),
                   jax.ShapeDtypeStruct((nh, tp, vd), F32)] + (list(job.out_shape) if job else []),
        scratch_shapes=[pltpu.VMEM((nq, 8, tq), F32), pltpu.VMEM((nq, 8, tq), F32), pltpu.VMEM((tp, vd), BF16),
                        pltpu.VMEM((nq, dqk, tq), F32)] + (list(job.scratch) if job else []),
        compiler_params=_cparams(("arbitrary", "arbitrary") if job else ("parallel", "arbitrary")),
    )(q, k, v, o, do, lse, *(job.inputs if job else ()))
    return outs[0], outs[1], outs[2], list(outs[3:])


def _mix_out_fwd(h, y_lru, o, g_lo, g_mo, w_lo, w_mo, g_post, layer, tm, name):
    tp, d = h.shape
    lw = y_lru.shape[-1]
    nh, vd = o.shape[0], o.shape[-1]

    def body(h_ref, yl_ref, o_ref, glo_ref, gmo_ref, wlo_ref, wmo_ref, gpost_ref, hout_ref, y_ref):
        nl, _ = _rms(yl_ref[...], glo_ref[...])
        y = _dot(nl, wlo_ref[...])
        ssq = jnp.zeros((tm, 1), F32)
        for hd in range(nh):
            oh = o_ref[hd]
            ssq += jnp.sum(oh * oh, axis=-1, keepdims=True)
        r = lax.rsqrt(ssq / (nh * vd) + EPS)
        for hd in range(nh):
            y += _dot(o_ref[hd] * r * gmo_ref[hd], wmo_ref[hd])
        y_ref[...] = y
        hout_ref[...] = h_ref[...] + _rms(y, gpost_ref[...])[0]

    tok = pl.BlockSpec((tm, d), lambda i: (i, 0))

    def lay(*shape):
        return pl.BlockSpec((None,) + shape, lambda i: (layer,) + (0,) * len(shape))

    return pl.pallas_call(
        body, name=name, grid=(tp // tm,),
        in_specs=[tok, pl.BlockSpec((tm, lw), lambda i: (i, 0)), pl.BlockSpec((nh, tm, vd), lambda i: (0, i, 0)),
                  lay(1, lw), lay(nh, 1, vd), _whole(lw, d), _whole(nh, vd, d), lay(1, d)],
        out_specs=[tok, tok],
        out_shape=[jax.ShapeDtypeStruct((tp, d), F32), jax.ShapeDtypeStruct((tp, d), F32)],
        compiler_params=_cparams(("parallel",)),
    )(h, y_lru, o, g_lo, g_mo, w_lo, w_mo, g_post)


def _mix_out_bwd(dh_out, y, y_lru, o, g_lo, g_mo, w_lo, w_mo, g_post, layer, tm, name):
    tp, d = y.shape
    lw = y_lru.shape[-1]
    nh, vd = o.shape[0], o.shape[-1]

    def body(dho_ref, y_ref, yl_ref, o_ref, glo_ref, gmo_ref, wlo_ref, wmo_ref, gpost_ref,
             dyl_ref, do_ref, dy_ref, nl_ref, nm_ref, dgpost_ref, dglo_ref, dgmo_ref):
        i = pl.program_id(0)

        @pl.when(i == 0)
        def _():
            dgpost_ref[...] = jnp.zeros_like(dgpost_ref)
            dglo_ref[...] = jnp.zeros_like(dglo_ref)
            dgmo_ref[...] = jnp.zeros_like(dgmo_ref)

        dy, dgp = _rms_bwd(y_ref[...], gpost_ref[...], dho_ref[...])
        dgpost_ref[...] += dgp
        dy_b = dy.astype(BF16)
        dy_ref[...] = dy_b
        yl = yl_ref[...]
        nl_ref[...] = _rms(yl, glo_ref[...])[0].astype(BF16)
        dyl, dgl = _rms_bwd(yl, glo_ref[...], _dot_nt(dy_b, wlo_ref[...]))
        dyl_ref[...] = dyl
        dglo_ref[...] += dgl
        ssq = jnp.zeros((tm, 1), F32)
        for hd in range(nh):
            oh = o_ref[hd]
            ssq += jnp.sum(oh * oh, axis=-1, keepdims=True)
        r = lax.rsqrt(ssq / (nh * vd) + EPS)
        dn = []
        dot_sum = jnp.zeros((tm, 1), F32)
        for hd in range(nh):
            oh = o_ref[hd] * r
            nm_ref[hd] = (oh * gmo_ref[hd]).astype(BF16)
            dn_h = _dot_nt(dy_b, wmo_ref[hd])
            dgmo_ref[hd] += jnp.sum(dn_h * oh, axis=0, keepdims=True)
            dn_h = dn_h * gmo_ref[hd]
            dot_sum += jnp.sum(dn_h * oh, axis=-1, keepdims=True)
            dn.append(dn_h)
        dot_mean = dot_sum / (nh * vd)
        for hd in range(nh):
            do_ref[hd] = r * (dn[hd] - o_ref[hd] * r * dot_mean)

    tok = pl.BlockSpec((tm, d), lambda i: (i, 0))
    tokl = pl.BlockSpec((tm, lw), lambda i: (i, 0))
    heads = pl.BlockSpec((nh, tm, vd), lambda i: (0, i, 0))

    def lay(*shape):
        return pl.BlockSpec((None,) + shape, lambda i: (layer,) + (0,) * len(shape))

    def acc(*shape):
        return pl.BlockSpec(shape, lambda i: (0,) * len(shape))

    return pl.pallas_call(
        body, name=name, grid=(tp // tm,),
        in_specs=[tok, tok, tokl, heads, lay(1, lw), lay(nh, 1, vd), _whole(lw, d), _whole(nh, vd, d), lay(1, d)],
        out_specs=[tokl, heads, tok, tokl, heads, acc(1, d), acc(1, lw), acc(nh, 1, vd)],
        out_shape=[jax.ShapeDtypeStruct((tp, lw), F32), jax.ShapeDtypeStruct((nh, tp, vd), F32),
                   jax.ShapeDtypeStruct((tp, d), BF16), jax.ShapeDtypeStruct((tp, lw), BF16),
                   jax.ShapeDtypeStruct((nh, tp, vd), BF16), jax.ShapeDtypeStruct((1, d), F32),
                   jax.ShapeDtypeStruct((1, lw), F32), jax.ShapeDtypeStruct((nh, 1, vd), F32)],
        compiler_params=_cparams(("arbitrary",)),
    )(dh_out, y, y_lru, o, g_lo, g_mo, w_lo, w_mo, g_post)


def _loss_head(h, target, n_real, tm, name):
    tp, d = h.shape
    nt = tp // tm

    def body(h_ref, t_ref, dh_ref, part_ref):
        i = pl.program_id(0)
        row = i * tm + lax.broadcasted_iota(jnp.int32, (tm, 1), 0)
        real = (row >= N_META) & (row < N_META + n_real)
        err = jnp.where(real, h_ref[...] - t_ref[...], 0.0)
        dh_ref[...] = err * (1.0 / d)
        part_ref[...] = jnp.full((1, 1, 128), 0.5 / d * jnp.sum(err * err), F32)

    tok = pl.BlockSpec((tm, d), lambda i: (i, 0))
    return pl.pallas_call(
        body, name=name, grid=(nt,), in_specs=[tok, tok],
        out_specs=[tok, pl.BlockSpec((1, 1, 128), lambda i: (i, 0, 0))],
        out_shape=[jax.ShapeDtypeStruct((tp, d), F32), jax.ShapeDtypeStruct((nt, 1, 128), F32)],
        compiler_params=_cparams(("parallel",)),
    )(h, target)


def _mesh_pos():
    return lax.axis_index("x"), lax.axis_index("y"), lax.axis_index("c")


def _other_chips(x, y):
    return [(1 - x, y), (x, 1 - y), (1 - x, 1 - y)]


class _GatherJob:
    def __init__(self, inputs, picks, shard_shapes, dtypes, split_last=()):
        self.inputs = list(inputs)
        self.picks = list(picks)
        n = len(self.inputs)
        self.out_shape = [jax.ShapeDtypeStruct((N_CHIPS,) + tuple(s), dt) for s, dt in zip(shard_shapes, dtypes)]
        self.split_last = [a in split_last for a in range(n)]
        self.halves = [(s[1] if second else s[0]) // 2 for s, second in zip(shard_shapes, self.split_last)]
        self.scratch = [pltpu.SemaphoreType.DMA((n, 3)), pltpu.SemaphoreType.DMA((n, 3)), pltpu.SemaphoreType.DMA((n, 3)),
                        pltpu.SemaphoreType.DMA((n, 3)), pltpu.SemaphoreType.DMA((n,))]

    def _copies(self, ins, outs, sems):
        ici_s, ici_r, core_s, core_r, loc = sems
        x, y, c = _mesh_pos()
        me = 2 * x + y
        chips = _other_chips(x, y)
        local, ici, fwd = [], [], []
        for a in range(len(ins)):
            src = self.picks[a](ins[a])
            hv = self.halves[a]

            def half(ref, which, a=a, hv=hv):
                return ref.at[:, pl.ds(which * hv, hv)] if self.split_last[a] else ref.at[pl.ds(which * hv, hv)]

            local.append(pltpu.make_async_copy(src, outs[a].at[me], loc.at[a]))
            for p, (px, py) in enumerate(chips):
                ici.append((pltpu.make_async_remote_copy(src_ref=half(src, c), dst_ref=half(outs[a].at[me], c),
                                                         send_sem=ici_s.at[a, p], recv_sem=ici_r.at[a, p],
                                                         device_id=(px, py, c), device_id_type=MESH),
                            pltpu.make_async_remote_copy(src_ref=half(src, c), dst_ref=half(outs[a].at[2 * px + py], c),
                                                         send_sem=ici_s.at[a, p], recv_sem=ici_r.at[a, p],
                                                         device_id=(px, py, c), device_id_type=MESH)))
                landed = half(outs[a].at[2 * px + py], c)
                theirs = half(outs[a].at[2 * px + py], 1 - c)
                fwd.append((pltpu.make_async_remote_copy(src_ref=landed, dst_ref=landed, send_sem=core_s.at[a, p],
                                                         recv_sem=core_r.at[a, p], device_id=(x, y, 1 - c), device_id_type=MESH),
                            pltpu.make_async_remote_copy(src_ref=theirs, dst_ref=theirs, send_sem=core_s.at[a, p],
                                                         recv_sem=core_r.at[a, p], device_id=(x, y, 1 - c), device_id_type=MESH)))
        return local, ici, fwd

    def start(self, ins, outs, sems):
        local, ici, _ = self._copies(ins, outs, sems)
        for cp in local:
            cp.start()
        for send, _ in ici:
            send.start()

    def finish(self, ins, outs, sems):
        local, ici, fwd = self._copies(ins, outs, sems)
        for (_, arrive), (hand_on, _) in zip(ici, fwd):
            arrive.wait_recv()
            hand_on.start()
        for _, arrive in fwd:
            arrive.wait_recv()
        for send, _ in ici:
            send.wait_send()
        for hand_on, _ in fwd:
            hand_on.wait_send()
        for cp in local:
            cp.wait()


class _ReduceJob:
    def __init__(self, inputs):
        self.inputs = list(inputs)
        n = len(self.inputs)
        self.out_shape = [jax.ShapeDtypeStruct(a.shape, a.dtype) for a in self.inputs] * 2
        self.scratch = [pltpu.SemaphoreType.DMA((n, 3)), pltpu.SemaphoreType.DMA((n, 3)), pltpu.SemaphoreType.DMA((n, 4)),
                        pltpu.SemaphoreType.DMA((n, 4)), pltpu.SemaphoreType.DMA((n,))]

    def _copies(self, ins, outs, sems):
        ici_s, ici_r, core_s, core_r, loc = sems
        n = len(ins)
        same, other = outs[:n], outs[n:]
        x, y, c = _mesh_pos()
        me = 2 * x + y
        chips = _other_chips(x, y)
        sib = (x, y, 1 - c)
        local, ici, fwd = [], [], []
        for a in range(n):
            local.append(pltpu.make_async_copy(ins[a].at[me], same[a].at[me], loc.at[a]))
            fwd.append((pltpu.make_async_remote_copy(src_ref=ins[a].at[me], dst_ref=other[a].at[me], send_sem=core_s.at[a, 3],
                                                     recv_sem=core_r.at[a, 3], device_id=sib, device_id_type=MESH), None))
            for p, (px, py) in enumerate(chips):
                cp = 2 * px + py
                ici.append((pltpu.make_async_remote_copy(src_ref=ins[a].at[cp], dst_ref=same[a].at[me], send_sem=ici_s.at[a, p],
                                                         recv_sem=ici_r.at[a, p], device_id=(px, py, c), device_id_type=MESH),
                            pltpu.make_async_remote_copy(src_ref=ins[a].at[cp], dst_ref=same[a].at[cp], send_sem=ici_s.at[a, p],
                                                         recv_sem=ici_r.at[a, p], device_id=(px, py, c), device_id_type=MESH)))
                fwd.append((pltpu.make_async_remote_copy(src_ref=same[a].at[cp], dst_ref=other[a].at[cp], send_sem=core_s.at[a, p],
                                                         recv_sem=core_r.at[a, p], device_id=sib, device_id_type=MESH), p))
        return local, ici, fwd

    def start(self, ins, outs, sems):
        local, ici, fwd = self._copies(ins, outs, sems)
        for cp in local:
            cp.start()
        for send, _ in ici:
            send.start()
        for hand_on, p in fwd:
            if p is None:
                hand_on.start()

    def finish(self, ins, outs, sems):
        local, ici, fwd = self._copies(ins, outs, sems)
        chip_fwd = [f for f in fwd if f[1] is not None]
        for (_, arrive), (hand_on, _) in zip(ici, chip_fwd):
            arrive.wait_recv()
            hand_on.start()
        for hand_on, _ in fwd:
            hand_on.wait()
        for send, _ in ici:
            send.wait_send()
        for cp in local:
            cp.wait()


class _DeviceGatherJob:
    def __init__(self, buf):
        self.inputs = [buf]
        self.out_shape = [jax.ShapeDtypeStruct((8,) + buf.shape, buf.dtype)]
        self.scratch = [pltpu.SemaphoreType.DMA((7,)), pltpu.SemaphoreType.DMA((7,)), pltpu.SemaphoreType.DMA]

    def _copies(self, ins, outs, sems):
        send_s, recv_s, loc = sems
        x, y, c = _mesh_pos()
        me = 4 * x + 2 * y + c
        local = pltpu.make_async_copy(ins[0], outs[0].at[me], loc)
        remote = []
        for rel in range(1, 8):
            fx, fy, fc = (rel >> 2) & 1, (rel >> 1) & 1, rel & 1
            peer = (1 - x if fx else x, 1 - y if fy else y, 1 - c if fc else c)
            remote.append(pltpu.make_async_remote_copy(src_ref=ins[0], dst_ref=outs[0].at[me], send_sem=send_s.at[rel - 1],
                                                       recv_sem=recv_s.at[rel - 1], device_id=peer, device_id_type=MESH))
        return local, remote

    def start(self, ins, outs, sems):
        local, remote = self._copies(ins, outs, sems)
        local.start()
        for rc in remote:
            rc.start()

    def finish(self, ins, outs, sems):
        local, remote = self._copies(ins, outs, sems)
        for rc in remote:
            rc.wait()
        local.wait()


class _JobGroup:
    def __init__(self, jobs):
        self.jobs = list(jobs)
        self.inputs = [a for j in self.jobs for a in j.inputs]
        self.out_shape = [s for j in self.jobs for s in j.out_shape]
        self.scratch = [s for j in self.jobs for s in j.scratch]
        self.n_in = [len(j.inputs) for j in self.jobs]
        self.n_out = [len(j.out_shape) for j in self.jobs]
        self.n_sem = [len(j.scratch) for j in self.jobs]

    def _parts(self, k, ins, outs, sems):
        return (ins[sum(self.n_in[:k]):sum(self.n_in[:k + 1])], outs[sum(self.n_out[:k]):sum(self.n_out[:k + 1])],
                sems[sum(self.n_sem[:k]):sum(self.n_sem[:k + 1])])

    def start(self, ins, outs, sems):
        for k, job in enumerate(self.jobs):
            job.start(*self._parts(k, ins, outs, sems))

    def finish(self, ins, outs, sems):
        for k, job in enumerate(self.jobs):
            job.finish(*self._parts(k, ins, outs, sems))

    def split(self, outs):
        return [list(outs[sum(self.n_out[:k]):sum(self.n_out[:k + 1])]) for k in range(len(self.jobs))]


def _run_jobs(jobs, name):
    group = _JobGroup(jobs)
    n_in, n_out = len(group.inputs), len(group.out_shape)

    def body(*refs):
        ins, outs, sems = refs[:n_in], refs[n_in:n_in + n_out], refs[n_in + n_out:]
        group.start(ins, outs, sems)
        group.finish(ins, outs, sems)

    res = pl.pallas_call(body, name=name, in_specs=[_ANY] * n_in, out_specs=[_ANY] * n_out, out_shape=list(group.out_shape),
                         scratch_shapes=list(group.scratch))(*group.inputs)
    return group.split(res)


def _sum_devices(parts, name):
    n_dev, rows, lanes = parts.shape

    def body(p_ref, o_ref):
        total = p_ref[0]
        for k in range(1, n_dev):
            total = total + p_ref[k]
        o_ref[...] = total

    return pl.pallas_call(body, name=name, out_shape=jax.ShapeDtypeStruct((rows, lanes), F32),
                          compiler_params=pltpu.CompilerParams(vmem_limit_bytes=V7X_VMEM_LIMIT_BYTES))(parts)


def _row_tile(rows, limit=512):
    best = None
    for t in range(16, min(rows, limit) + 1, 16):
        if rows % t == 0:
            best = t
    return best if best is not None else rows


def _adamw_math(w, g, m, v):
    m = ADAM_B1 * m + (1.0 - ADAM_B1) * g
    v = ADAM_B2 * v + (1.0 - ADAM_B2) * (g * g)
    m_hat = m / (1.0 - ADAM_B1 ** ADAM_STEP)
    v_hat = v / (1.0 - ADAM_B2 ** ADAM_STEP)
    delta = -ADAM_LR * (m_hat / (jnp.sqrt(v_hat) + ADAM_EPS) + ADAM_WD * w)
    return delta, m, v


def _adamw_layer(w, m, v, same_plane, other_plane, layer, prev, name):
    nl, r, ccol = w.shape
    ns = same_plane.shape[0]
    tr = _row_tile(r, 256)

    def body(w_ref, m_ref, v_ref, a_ref, b_ref, *rest):
        go_ref, d_ref, mo_ref, vo_ref = rest[-4:]
        sa = a_ref[0].astype(F32)
        sb = b_ref[0].astype(F32)
        for k in range(1, ns):
            sa = sa + a_ref[k].astype(F32)
            sb = sb + b_ref[k].astype(F32)
        g = sa + sb
        go_ref[...] = g
        d_ref[...], mo_ref[...], vo_ref[...] = _adamw_math(w_ref[...], g, m_ref[...], v_ref[...])

    blk = pl.BlockSpec((None, tr, ccol), lambda i: (layer, i, 0))
    plane = pl.BlockSpec((ns, tr, ccol), lambda i: (0, i, 0))
    out = jax.ShapeDtypeStruct((nl, r, ccol), F32)
    n_prev = 4 if prev is not None else 0
    return pl.pallas_call(
        body, name=name, grid=(r // tr,),
        in_specs=[blk, blk, blk, plane, plane] + [_ANY] * n_prev,
        out_specs=[blk, blk, blk, blk], out_shape=[out, out, out, out],
        input_output_aliases={5 + k: k for k in range(n_prev)},
        compiler_params=_cparams(("parallel",)))(w, m, v, same_plane, other_plane, *(prev or ()))


def _adamw_whole(w, g, m, v, name):
    def body(w_ref, g_ref, m_ref, v_ref, d_ref, mo_ref, vo_ref):
        d_ref[...], mo_ref[...], vo_ref[...] = _adamw_math(w_ref[...], g_ref[...], m_ref[...], v_ref[...])

    out = jax.ShapeDtypeStruct(w.shape, F32)
    return pl.pallas_call(body, name=name, out_shape=[out, out, out],
                          compiler_params=pltpu.CompilerParams(vmem_limit_bytes=V7X_VMEM_LIMIT_BYTES))(w, g, m, v)


_W_NAMES = ['meta_tokens', 'ffn1_pre_g', 'ffn1_w_gate', 'ffn1_w_up', 'ffn1_w_down', 'ffn1_post_g', 'mix_pre_g', 'w_in',
            'lru_conv_w', 'lru_conv_b', 'lru_w_a', 'lru_b_a', 'lru_w_x', 'lru_b_x', 'lru_lambda', 'mla_q_norm_g', 'mla_w_uq',
            'mla_kv_norm_g', 'mla_w_ukv', 'lru_out_g', 'mla_out_g', 'w_out', 'mix_post_g', 'ffn2_pre_g', 'ffn2_w_gate',
            'ffn2_w_up', 'ffn2_w_down', 'ffn2_post_g']
_REPLICATED = ['ffn1_pre_g', 'ffn1_post_g', 'mix_pre_g', 'lru_conv_b', 'lru_w_a', 'lru_b_a', 'lru_w_x', 'lru_b_x', 'lru_lambda',
               'mla_q_norm_g', 'mla_kv_norm_g', 'lru_out_g', 'mla_out_g', 'mix_post_g', 'ffn2_pre_g', 'ffn2_post_g']
_LANES = 128


def _pack_rows(arrays):
    flat = jnp.concatenate([a.reshape(-1) for a in arrays])
    total = _round_up(flat.shape[0], 8 * _LANES)
    return jnp.pad(flat, (0, total - flat.shape[0])).reshape(-1, _LANES)


def _unpack_rows(buf, shapes):
    flat = buf.reshape(-1)
    out, off = [], 0
    for shp in shapes:
        n = math.prod(shp)
        out.append(flat[off:off + n].reshape(shp))
        off += n
    return out


def _block_diag(w):
    nl, nh, n, _ = w.shape
    eye = jnp.eye(nh, dtype=w.dtype)
    return (w[:, :, :, None, :] * eye[None, :, None, :, None]).reshape(nl, nh * n, nh * n)


def _diag_blocks(bd, nh):
    n = bd.shape[0] // nh
    b4 = bd.reshape(nh, n, nh, n)
    return jnp.stack([b4[i, :, i, :] for i in range(nh)])


def kernel(x, meta_tokens, ffn1_pre_g, ffn1_w_gate, ffn1_w_up, ffn1_w_down, ffn1_post_g, mix_pre_g, w_in, lru_conv_w, lru_conv_b, lru_w_a, lru_b_a, lru_w_x, lru_b_x, lru_lambda, mla_q_norm_g, mla_w_uq, mla_kv_norm_g, mla_w_ukv, lru_out_g, mla_out_g, w_out, mix_post_g, ffn2_pre_g, ffn2_w_gate, ffn2_w_up, ffn2_w_down, ffn2_post_g, loss_target, m_meta_tokens, m_ffn1_pre_g, m_ffn1_w_gate, m_ffn1_w_up, m_ffn1_w_down, m_ffn1_post_g, m_mix_pre_g, m_w_in, m_lru_conv_w, m_lru_conv_b, m_lru_w_a, m_lru_b_a, m_lru_w_x, m_lru_b_x, m_lru_lambda, m_mla_q_norm_g, m_mla_w_uq, m_mla_kv_norm_g, m_mla_w_ukv, m_lru_out_g, m_mla_out_g, m_w_out, m_mix_post_g, m_ffn2_pre_g, m_ffn2_w_gate, m_ffn2_w_up, m_ffn2_w_down, m_ffn2_post_g, v_meta_tokens, v_ffn1_pre_g, v_ffn1_w_gate, v_ffn1_w_up, v_ffn1_w_down, v_ffn1_post_g, v_mix_pre_g, v_w_in, v_lru_conv_w, v_lru_conv_b, v_lru_w_a, v_lru_b_a, v_lru_w_x, v_lru_b_x, v_lru_lambda, v_mla_q_norm_g, v_mla_w_uq, v_mla_kv_norm_g, v_mla_w_ukv, v_lru_out_g, v_mla_out_g, v_w_out, v_mix_post_g, v_ffn2_pre_g, v_ffn2_w_gate, v_ffn2_w_up, v_ffn2_w_down, v_ffn2_post_g):
    a = dict(locals())
    x2 = x[0]
    s_len, d = x2.shape
    nl = ffn1_pre_g.shape[0]
    lw = lru_conv_b.shape[-1]
    ql, kvl = mla_q_norm_g.shape[-1], mla_kv_norm_g.shape[-1]
    nh = MLA_HEADS
    vd = (d - lw) // nh
    t_real = N_META + s_len
    tp = _round_up(t_real, 384)
    tm = tp // 6
    tmb = tm // 2
    tm_ffn_fwd = tp // 4
    tm_ffn_bwd = tp // 8
    tq = 384 if tp >= 1536 else 128
    xi, yi, _ = _mesh_pos()
    chip = 2 * xi + yi

    def tr(w):
        return jnp.swapaxes(w, 1, 2)

    ffn_loc = jnp.stack([tr(ffn1_w_gate), tr(ffn1_w_up), ffn1_w_down, tr(ffn2_w_gate), tr(ffn2_w_up), ffn2_w_down],
                        axis=1).astype(BF16)
    loc = [ffn_loc, tr(w_in).astype(BF16), tr(mla_w_uq).astype(BF16), mla_w_ukv.astype(BF16), w_out.astype(BF16)]
    loc_shapes, loc_dtypes = [t.shape[1:] for t in loc], [t.dtype for t in loc]

    def layer_gather(l):
        return _GatherJob(loc, [lambda r, l=l: r.at[l]] * len(loc), loc_shapes, loc_dtypes, split_last=(1,))

    half_ffn = (ffn_loc.shape[1] // 2,) + ffn_loc.shape[2:]
    first = _GatherJob([ffn_loc, lru_conv_w, meta_tokens], [lambda r: r.at[0, pl.ds(0, half_ffn[0])], lambda r: r, lambda r: r],
                       [half_ffn, lru_conv_w.shape, meta_tokens.shape], [BF16, F32, F32], split_last=(0,))
    (w_ffn1_l0, g_conv, g_meta), = _run_jobs([first], "gather_first")
    rest_l0 = _GatherJob(loc, [lambda r: r.at[0, pl.ds(half_ffn[0], half_ffn[0])]] + [lambda r: r.at[0]] * (len(loc) - 1),
                         [half_ffn] + loc_shapes[1:], loc_dtypes, split_last=(0, 1))
    o0, o1, o2, o3 = lw, 2 * lw, 2 * lw + ql, 2 * lw + ql + kvl

    def assemble(gw, ffn1=None):
        w_ffn, g_win, g_wuq, g_wukv, g_wout = gw
        w_in_t = g_win.reshape(-1, d)
        w_kr = jnp.pad(w_in_t[o3:], ((QK_NOPE, 0), (0, 0)))
        w_ukv3 = jnp.transpose(g_wukv, (1, 0, 2)).reshape(kvl, nh, QK_NOPE + vd)
        w_out_full = g_wout.reshape(d, d)
        return dict(ffn1=ffn1 or (w_ffn, 0), ffn2=(w_ffn, 0 if ffn1 else 3),
                    parts=(w_in_t[:o0], w_in_t[o0:o1], w_in_t[o1:o2], w_in_t[o2:o3], w_kr),
                    q=g_wuq.reshape(nh, QK_DIM, ql),
                    k=jnp.transpose(jnp.pad(w_ukv3[..., :QK_NOPE], ((0, 0), (0, 0), (0, QK_ROPE))), (1, 0, 2)),
                    v=jnp.transpose(w_ukv3[..., QK_NOPE:], (1, 0, 2)),
                    lo=w_out_full[:lw], mo=w_out_full[lw:].reshape(nh, vd, d))

    conv_full = jnp.transpose(g_conv, (1, 2, 0, 3)).reshape(nl, CONV_WIDTH, lw)
    meta_full = jnp.transpose(g_meta, (1, 0, 2)).reshape(N_META, d)
    wa_bd = _block_diag(lru_w_a).astype(BF16)
    wx_bd = _block_diag(lru_w_x).astype(BF16)

    def gain3(g):
        return g.reshape(nl, 1, g.shape[-1])

    g3 = {n: gain3(a[n]) for n in ('ffn1_pre_g', 'ffn1_post_g', 'mix_pre_g', 'lru_conv_b', 'lru_b_a', 'lru_b_x', 'lru_lambda',
                                   'mla_q_norm_g', 'mla_kv_norm_g', 'lru_out_g', 'mix_post_g', 'ffn2_pre_g', 'ffn2_post_g')}
    g_mo = mla_out_g.reshape(nl, nh, 1, vd)

    pos = jnp.arange(tp, dtype=F32)
    inv_freq = 1.0 / (ROPE_THETA ** (jnp.arange(0, QK_ROPE, 2, dtype=F32) / QK_ROPE))
    ang = pos[:, None] * inv_freq[None, :]
    cos_t = jnp.concatenate([jnp.ones((tp, QK_NOPE), F32), jnp.cos(ang), jnp.cos(ang)], axis=1)
    sin_t = jnp.concatenate([jnp.zeros((tp, QK_NOPE), F32), jnp.sin(ang), jnp.sin(ang)], axis=1)
    half = QK_ROPE // 2
    rot = np.zeros((QK_DIM, QK_DIM), np.float32)
    for i in range(half):
        rot[QK_NOPE + half + i, QK_NOPE + i] = -1.0
        rot[QK_NOPE + i, QK_NOPE + half + i] = 1.0
    rot_b, rot_t = jnp.asarray(rot, BF16), jnp.asarray(rot.T, BF16)

    h = jnp.concatenate([meta_full, x2, jnp.zeros((tp - t_real, d), F32)], axis=0)
    target = jnp.pad(loss_target[0], ((N_META, tp - t_real), (0, 0)))
    saved, weights = [], []
    for l in range(nl):
        if l == 0:
            h1, gate1, up1, f1, gathered = _ffn_fwd(h, g3['ffn1_pre_g'], g3['ffn1_post_g'], w_ffn1_l0, l, 0, 1, 2, tm_ffn_fwd,
                                                    f"ffn1_fwd_{l}", rest_l0)
            wl = assemble(gathered, ffn1=(w_ffn1_l0, 0))
        else:
            wl = assemble(gathered)
            wf, k0 = wl['ffn1']
            h1, gate1, up1, f1, _ = _ffn_fwd(h, g3['ffn1_pre_g'], g3['ffn1_post_g'], wf, l, k0, k0 + 1, k0 + 2, tm_ffn_fwd,
                                             f"ffn1_fwd_{l}")
        weights.append(wl)
        xr, gr, cq, ckv, q, k, v = _mix_in_fwd(h1, g3['mix_pre_g'], wl['parts'], g3['mla_q_norm_g'], g3['mla_kv_norm_g'],
                                               wl['q'], wl['k'], wl['v'], cos_t, sin_t, rot_b, l, tm, f"mix_in_fwd_{l}")
        y_lru, xc, hs = _lru_fwd(xr, gr, conv_full, g3['lru_conv_b'], wa_bd, wx_bd, g3['lru_b_a'], g3['lru_b_x'],
                                 g3['lru_lambda'], l, tm, f"lru_fwd_{l}")
        o, lse, gathered = _attn_fwd(q, k, v, tq, f"attn_fwd_{l}", layer_gather(l + 1) if l + 1 < nl else None)
        h2, y = _mix_out_fwd(h1, y_lru, o, g3['lru_out_g'], g_mo, wl['lo'], wl['mo'], g3['mix_post_g'], l, tm, f"mix_out_fwd_{l}")
        wf, k0 = wl['ffn2']
        h3, gate2, up2, f2, _ = _ffn_fwd(h2, g3['ffn2_pre_g'], g3['ffn2_post_g'], wf, l, k0, k0 + 1, k0 + 2, tm_ffn_fwd,
                                         f"ffn2_fwd_{l}")
        saved.append((h, gate1, up1, f1, h1, xr, gr, cq, ckv, q, k, v, y_lru, xc, hs, o, lse, y, h2, gate2, up2, f2))
        h = h3

    dh, loss_parts = _loss_head(h, target, s_len, tm, "loss_head")
    loss = lax.psum(jnp.sum(loss_parts[:, 0, 0]), ("x", "y", "c"))

    small_order = _REPLICATED + ['lru_conv_w']
    small = {n: [None] * nl for n in small_order}
    planes = {}
    late_names = ['w_in', 'mla_w_uq', 'mla_w_ukv', 'ffn1_w_gate', 'ffn1_w_up', 'ffn1_w_down']
    early_names = ['ffn2_w_gate', 'ffn2_w_up', 'ffn2_w_down', 'w_out']
    late = None

    def keep_planes(keys, outs):
        for i, key in enumerate(keys):
            planes[key] = (outs[i], outs[len(keys) + i])

    for l in reversed(range(nl)):
        (h0, gate1, up1, f1, h1, xr, gr, cq, ckv, q, k, v, y_lru, xc, hs, o, lse, y, h2, gate2, up2, f2) = saved[l]
        wl = weights[l]
        wf, k0 = wl['ffn2']
        dh, dgate, dup, act, df, u, dgpre, dgpost, _ = _ffn_bwd(dh, h2, f2, gate2, up2, g3['ffn2_pre_g'], g3['ffn2_post_g'],
                                                               wf, l, k0, k0 + 1, k0 + 2, tm_ffn_bwd, f"ffn2_bwd_{l}")
        small['ffn2_pre_g'][l], small['ffn2_post_g'][l] = dgpre, dgpost
        d_wg2 = _mm_tn(dgate, u, tm, BF16, f"dw_gate2_{l}")
        d_wu2 = _mm_tn(dup, u, tm, BF16, f"dw_up2_{l}")
        dn2 = _mm_tn(act, df, tm, BF16, f"dw_down2_{l}")

        d_ylru, d_o, dy_b, n_lo, n_mo, dgpost, dglo, dgmo = _mix_out_bwd(dh, y, y_lru, o, g3['lru_out_g'], g_mo, wl['lo'], wl['mo'],
                                                                       g3['mix_post_g'], l, tm, f"mix_out_bwd_{l}")
        small['mix_post_g'][l], small['lru_out_g'][l], small['mla_out_g'][l] = dgpost, dglo, dgmo
        d_wlo = _mm_tn(n_lo, dy_b, tm, F32, f"dw_out_lru_{l}")
        d_wmo = _mm_tn_heads(n_mo, dy_b, tm, f"dw_out_mla_{l}")
        d_wout = jnp.concatenate([d_wlo, d_wmo.reshape(nh * vd, d)], axis=0).reshape(N_CHIPS, d // N_CHIPS, d).astype(BF16)

        keys = [(n, l) for n in early_names] + ([(n, l + 1) for n in late_names] if late else [])
        jobs = [_ReduceJob([d_wg2, d_wu2, dn2, d_wout] + (late or []))]
        if l == 0 and nl > 1:
            deep = _pack_rows([jnp.stack(small[n][1:]) for n in small_order])
            jobs.append(_DeviceGatherJob(deep))
        group = _JobGroup(jobs)
        dq, dk, dv, outs = _attn_bwd(q, k, v, o, d_o, lse, tq, f"attn_bwd_{l}", group)
        outs = group.split(outs)
        keep_planes(keys, outs[0])
        if len(jobs) > 1:
            all_deep = outs[1][0]
        d_xr, d_gr, dcw, dcb, dwa, dwx, dba, dbx, dlam = _lru_bwd(d_ylru, xr, gr, xc, hs, conv_full, wa_bd, wx_bd, g3['lru_b_a'],
                                                                 g3['lru_b_x'], g3['lru_lambda'], l, tm, f"lru_bwd_{l}")
        small['lru_conv_w'][l], small['lru_conv_b'][l] = dcw, dcb
        small['lru_w_a'][l], small['lru_w_x'][l] = _diag_blocks(dwa, LRU_HEADS), _diag_blocks(dwx, LRU_HEADS)
        small['lru_b_a'][l], small['lru_b_x'][l], small['lru_lambda'][l] = dba, dbx, dlam

        dh, u, qn, kvn, dq_pre, d_cq, d_ckv, d_kr, dgpre, dgq, dgkv = _mix_in_bwd(
            dh, h1, cq, ckv, d_xr, d_gr, dq, dk, dv, g3['mix_pre_g'], wl['parts'], g3['mla_q_norm_g'], g3['mla_kv_norm_g'],
            wl['q'], wl['k'], wl['v'], cos_t, sin_t, rot_t, l, tmb, f"mix_in_bwd_{l}")
        small['mix_pre_g'][l], small['mla_q_norm_g'][l], small['mla_kv_norm_g'][l] = dgpre, dgq, dgkv
        d_win = jnp.concatenate([_mm_tn(d_xr, u, tm, F32, f"dw_in_xr_{l}"), _mm_tn(d_gr, u, tm, F32, f"dw_in_gr_{l}"),
                                 _mm_tn(d_cq, u, tm, F32, f"dw_in_cq_{l}"), _mm_tn(d_ckv, u, tm, F32, f"dw_in_ckv_{l}"),
                                 _mm_tn(d_kr, u, tm, F32, f"dw_in_kr_{l}")[QK_NOPE:]], axis=0)
        d_win = d_win.reshape(N_CHIPS, -1, d).astype(BF16)
        d_wuq = _mm_tn_heads(dq_pre, qn, tm, f"dw_uq_{l}").reshape(N_CHIPS, -1, ql).astype(BF16)
        d_wk = _mm_tn_heads(dk, kvn, tm, f"dw_uk_{l}")[:, :QK_NOPE]
        d_wv = _mm_tn_heads(dv, kvn, tm, f"dw_uv_{l}")
        d_wukv = jnp.transpose(jnp.concatenate([d_wk, d_wv], axis=1).reshape(N_CHIPS, -1, kvl), (0, 2, 1)).astype(BF16)

        wf, k0 = wl['ffn1']
        last_layer = l == 0
        dh, dgate, dup, act, df, u, dgpre, dgpost, outs = _ffn_bwd(
            dh, h0, f1, gate1, up1, g3['ffn1_pre_g'], g3['ffn1_post_g'], wf, l, k0, k0 + 1, k0 + 2, tm_ffn_bwd, f"ffn1_bwd_{l}",
            _ReduceJob([d_win, d_wuq, d_wukv]) if last_layer else None)
        small['ffn1_pre_g'][l], small['ffn1_post_g'][l] = dgpre, dgpost
        d_wg1 = _mm_tn(dgate, u, tm, BF16, f"dw_gate1_{l}")
        if last_layer:
            keep_planes([(n, 0) for n in late_names[:3]], outs)
            d_wu1, outs = _mm_tn(dup, u, tm, BF16, f"dw_up1_{l}", _ReduceJob([d_wg1]))
            keep_planes([('ffn1_w_gate', 0)], outs)
            dn1, outs = _mm_tn(act, df, tm, BF16, f"dw_down1_{l}", _ReduceJob([d_wu1]))
            keep_planes([('ffn1_w_up', 0)], outs)
            late = [dn1]
        else:
            d_wu1 = _mm_tn(dup, u, tm, BF16, f"dw_up1_{l}")
            dn1 = _mm_tn(act, df, tm, BF16, f"dw_down1_{l}")
            late = [d_win, d_wuq, d_wukv, d_wg1, d_wu1, dn1]

    grad_x = dh[N_META:t_real][None]

    shallow = _pack_rows([small[n][0] for n in small_order] + [dh[:N_META]])
    late_planes, (all_shallow,) = _run_jobs([_ReduceJob(late), _DeviceGatherJob(shallow)], "reduce_last")
    keep_planes([('ffn1_w_down', 0)], late_planes)

    res = {}
    stored_transposed = ('ffn1_w_gate', 'ffn1_w_up', 'ffn2_w_gate', 'ffn2_w_up', 'w_in', 'mla_w_uq')
    for name in early_names + late_names:
        view = tr if name in stored_transposed else (lambda t: t)
        w_v, m_v, v_v = view(a[name]), view(a['m_' + name]), view(a['v_' + name])
        prev = None
        for l in reversed(range(nl)):
            same_plane, other_plane = planes[(name, l)]
            prev = _adamw_layer(w_v, m_v, v_v, same_plane, other_plane, l, prev, f"adamw_{name}_{l}")
        res[name] = [view(t) for t in prev]

    shapes0 = [small[n][0].shape for n in small_order]
    pieces = _unpack_rows(_sum_devices(all_shallow, "sum_small_last"), shapes0 + [(N_META, d)])
    per_layer = [p[None] for p in pieces[:-1]]
    if nl > 1:
        deeper = _unpack_rows(_sum_devices(all_deep, "sum_small_deep"), [(nl - 1,) + s for s in shapes0])
        per_layer = [jnp.concatenate([p0, p1], axis=0) for p0, p1 in zip(per_layer, deeper)]
    g_small = {n: p.reshape(a[n].shape if n != 'lru_conv_w' else conv_full.shape) for n, p in zip(small_order, per_layer)}
    g_conv_loc = lax.dynamic_slice_in_dim(g_small.pop('lru_conv_w'), chip * lru_conv_w.shape[-1], lru_conv_w.shape[-1], axis=2)
    g_meta_loc = lax.dynamic_slice_in_dim(pieces[-1], chip * meta_tokens.shape[-1], meta_tokens.shape[-1], axis=1)

    shapes = [a[n].shape for n in _REPLICATED]
    dl, mo, vo = _adamw_whole(_pack_rows([a[n] for n in _REPLICATED]), _pack_rows([g_small[n] for n in _REPLICATED]),
                              _pack_rows([a['m_' + n] for n in _REPLICATED]), _pack_rows([a['v_' + n] for n in _REPLICATED]),
                              "adamw_replicated")
    for n, dd, mm, vv in zip(_REPLICATED, _unpack_rows(dl, shapes), _unpack_rows(mo, shapes), _unpack_rows(vo, shapes)):
        res[n] = (g_small[n], dd, mm, vv)
    cshape = lru_conv_w.shape
    c2 = (cshape[0] * cshape[1], cshape[2])
    dd, mm, vv = _adamw_whole(lru_conv_w.reshape(c2), g_conv_loc.reshape(c2), m_lru_conv_w.reshape(c2), v_lru_conv_w.reshape(c2),
                              "adamw_conv_w")
    res['lru_conv_w'] = (g_conv_loc, dd.reshape(cshape), mm.reshape(cshape), vv.reshape(cshape))
    res['meta_tokens'] = (g_meta_loc,) + tuple(_adamw_whole(meta_tokens, g_meta_loc, m_meta_tokens, v_meta_tokens, "adamw_meta"))

    return (loss, grad_x, *[res[n][0] for n in _W_NAMES], *[res[n][1] for n in _W_NAMES],
            *[res[n][2] for n in _W_NAMES], *[res[n][3] for n in _W_NAMES])
```

```python
import functools
import math

import jax
import jax.numpy as jnp
import numpy as np
from jax import lax
from jax.experimental import pallas as pl
from jax.experimental.pallas import tpu as pltpu

F32 = jnp.float32
BF16 = jnp.bfloat16
MESH = pl.DeviceIdType.MESH

EPS = 1e-6
N_META = 16
LRU_HEADS = 8
MLA_HEADS = 8
QK_NOPE = 64
QK_ROPE = 32
QK_DIM = QK_NOPE + QK_ROPE
LRU_C = 8.0
ROPE_THETA = 10000.0
CONV_WIDTH = 4
N_CHIPS = 4

ADAM_LR = 0.001
ADAM_B1 = 0.9
ADAM_B2 = 0.999
ADAM_EPS = 1e-08
ADAM_WD = 0.01
ADAM_STEP = 10

V7X_VMEM_LIMIT_BYTES = 56 * 1024 * 1024
NEG_BIG = -1e30
Q_PRESCALE = QK_DIM ** -0.5 * math.log2(math.e)


def _cparams(sem=None):
    return pltpu.CompilerParams(dimension_semantics=sem, vmem_limit_bytes=V7X_VMEM_LIMIT_BYTES)


def _round_up(a, b):
    return -(-a // b) * b


def _whole(*shape):
    return pl.BlockSpec(shape, lambda *_: (0,) * len(shape))


_ANY = pl.BlockSpec(memory_space=pl.ANY)


def _dot(a, b):
    return jnp.dot(a.astype(BF16), b.astype(BF16), preferred_element_type=F32)


def _dot_nt(a, b):
    return lax.dot_general(a.astype(BF16), b.astype(BF16), (((1,), (1,)), ((), ())), preferred_element_type=F32)


def _dot_tn(a, b):
    return lax.dot_general(a.astype(BF16), b.astype(BF16), (((0,), (0,)), ((), ())), preferred_element_type=F32)


def _dot_split(x, p):
    hi = x.astype(BF16)
    lo = (x - hi.astype(F32)).astype(BF16)
    return jnp.dot(hi, p, preferred_element_type=F32) + jnp.dot(lo, p, preferred_element_type=F32)


def _rms(x, g):
    r = lax.rsqrt(jnp.mean(x * x, axis=-1, keepdims=True) + EPS)
    return x * r * g, r


def _rms_bwd(x, g, dy):
    r = lax.rsqrt(jnp.mean(x * x, axis=-1, keepdims=True) + EPS)
    xh = x * r
    dyg = dy * g
    dx = r * (dyg - xh * jnp.mean(dyg * xh, axis=-1, keepdims=True))
    dg = jnp.sum(dy * xh, axis=0, keepdims=True)
    return dx, dg


def _sigmoid(x):
    return 1.0 / (1.0 + jnp.exp(-x))


def _neg_expm1(x):
    series = -x * (1.0 + x * (0.5 + x * (1.0 / 6.0 + x * (1.0 / 24.0))))
    return jnp.where(jnp.abs(x) < 0.03, series, 1.0 - jnp.exp(x))


_GELU_K = math.sqrt(2.0 / math.pi)


def _gelu(x):
    t = jnp.tanh(_GELU_K * (x + 0.044715 * x * x * x))
    return 0.5 * x * (1.0 + t), t


def _gelu_grad(x, t):
    return 0.5 * (1.0 + t) + 0.5 * x * (1.0 - t * t) * _GELU_K * (1.0 + 3.0 * 0.044715 * x * x)


def _rope(x, cos_t, sin_t, rot):
    return x * cos_t + _dot_split(x, rot) * sin_t


def _ffn_fwd(h, g_pre, g_post, w_ffn, layer, kind_gate, kind_up, kind_dn, tm, name, job=None):
    tp, d = h.shape
    n_slot, fs = w_ffn.shape[0], w_ffn.shape[-2]
    n_tok = tp // tm
    n_ji = len(job.inputs) if job else 0
    n_jo = len(job.out_shape) if job else 0

    def body(h_ref, gpre_ref, gpost_ref, wg_ref, wu_ref, wd_ref, *rest):
        job_in, (hout_ref, gate_ref, up_ref, f_ref), rest = rest[:n_ji], rest[n_ji:n_ji + 4], rest[n_ji + 4:]
        job_out, (u_sc, acc_sc), sems = rest[:n_jo], rest[n_jo:n_jo + 2], rest[n_jo + 2:]
        j = pl.program_id(1)
        if job:
            pl.when((pl.program_id(0) == 0) & (j == 0))(lambda: job.start(job_in, job_out, sems))

        @pl.when(j == 0)
        def _():
            u, _ = _rms(h_ref[...], gpre_ref[...])
            u_sc[...] = u.astype(BF16)
            acc_sc[...] = jnp.zeros_like(acc_sc)

        u = u_sc[...]
        gate = _dot_nt(u, wg_ref[...])
        up = _dot_nt(u, wu_ref[...])
        act = gate * _sigmoid(gate) * up
        gate_ref[...] = gate.astype(BF16)
        up_ref[...] = up.astype(BF16)
        acc_sc[...] += jnp.dot(act.astype(BF16), wd_ref[...], preferred_element_type=F32)

        @pl.when(j == n_slot - 1)
        def _():
            f = acc_sc[...]
            f_ref[...] = f
            n, _ = _rms(f, gpost_ref[...])
            hout_ref[...] = h_ref[...] + 0.5 * n

        if job:
            pl.when((pl.program_id(0) == n_tok - 1) & (j == n_slot - 1))(lambda: job.finish(job_in, job_out, sems))

    tok = pl.BlockSpec((tm, d), lambda i, j: (i, 0))
    gain = pl.BlockSpec((None, 1, d), lambda i, j: (layer, 0, 0))
    slot_act = pl.BlockSpec((None, tm, fs), lambda i, j: (j, i, 0))
    outs = pl.pallas_call(
        body, name=name, grid=(n_tok, n_slot),
        in_specs=[tok, gain, gain] + [pl.BlockSpec((None, None, fs, d), lambda i, j, kind=kind: (j, kind, 0, 0))
                                      for kind in (kind_gate, kind_up, kind_dn)] + [_ANY] * n_ji,
        out_specs=[tok, slot_act, slot_act, tok] + [_ANY] * n_jo,
        out_shape=[jax.ShapeDtypeStruct((tp, d), F32), jax.ShapeDtypeStruct((n_slot, tp, fs), BF16),
                   jax.ShapeDtypeStruct((n_slot, tp, fs), BF16), jax.ShapeDtypeStruct((tp, d), F32)]
        + (list(job.out_shape) if job else []),
        scratch_shapes=[pltpu.VMEM((tm, d), BF16), pltpu.VMEM((tm, d), F32)] + (list(job.scratch) if job else []),
        compiler_params=_cparams(("arbitrary", "arbitrary") if job else ("parallel", "arbitrary")),
    )(h, g_pre, g_post, w_ffn, w_ffn, w_ffn, *(job.inputs if job else ()))
    return outs[0], outs[1], outs[2], outs[3], list(outs[4:])


def _ffn_bwd(dh_out, h, f, gate, up, g_pre, g_post, w_ffn, layer, kind_gate, kind_up, kind_dn, tm, name, job=None):
    tp, d = h.shape
    n_slot, fs = w_ffn.shape[0], w_ffn.shape[-2]
    n_tok = tp // tm
    n_ji = len(job.inputs) if job else 0
    n_jo = len(job.out_shape) if job else 0

    def body(dho_ref, h_ref, f_ref, gate_ref, up_ref, gpre_ref, gpost_ref, wg_ref, wu_ref, wd_ref, *rest):
        job_in, rest = rest[:n_ji], rest[n_ji:]
        (dh_ref, dgate_ref, dup_ref, act_ref, df_ref, u_ref, dgpre_ref, dgpost_ref), rest = rest[:8], rest[8:]
        job_out, du_sc, sems = rest[:n_jo], rest[n_jo], rest[n_jo + 1:]
        i, j = pl.program_id(0), pl.program_id(1)
        if job:
            pl.when((i == 0) & (j == 0))(lambda: job.start(job_in, job_out, sems))

        @pl.when((i == 0) & (j == 0))
        def _():
            dgpre_ref[...] = jnp.zeros_like(dgpre_ref)
            dgpost_ref[...] = jnp.zeros_like(dgpost_ref)

        @pl.when(j == 0)
        def _():
            df, dg = _rms_bwd(f_ref[...], gpost_ref[...], 0.5 * dho_ref[...])
            df_ref[...] = df.astype(BF16)
            dgpost_ref[...] += dg
            u, _ = _rms(h_ref[...], gpre_ref[...])
            u_ref[...] = u.astype(BF16)
            du_sc[...] = jnp.zeros_like(du_sc)

        g = gate_ref[...].astype(F32)
        u_ = up_ref[...].astype(F32)
        sg = _sigmoid(g)
        silu = g * sg
        dact = _dot_nt(df_ref[...], wd_ref[...])
        dup = dact * silu
        dgate = dact * u_ * (sg * (1.0 + g * (1.0 - sg)))
        act_ref[...] = (silu * u_).astype(BF16)
        dup_b = dup.astype(BF16)
        dgate_b = dgate.astype(BF16)
        dup_ref[...] = dup_b
        dgate_ref[...] = dgate_b
        du_sc[...] += _dot(dgate_b, wg_ref[...]) + _dot(dup_b, wu_ref[...])

        @pl.when(j == n_slot - 1)
        def _():
            dx, dg = _rms_bwd(h_ref[...], gpre_ref[...], du_sc[...])
            dh_ref[...] = dho_ref[...] + dx
            dgpre_ref[...] += dg

        if job:
            pl.when((i == n_tok - 1) & (j == n_slot - 1))(lambda: job.finish(job_in, job_out, sems))

    tok = pl.BlockSpec((tm, d), lambda i, j: (i, 0))
    gain = pl.BlockSpec((None, 1, d), lambda i, j: (layer, 0, 0))
    acc = pl.BlockSpec((1, d), lambda i, j: (0, 0))
    slot_act = pl.BlockSpec((None, tm, fs), lambda i, j: (j, i, 0))
    act_shape = jax.ShapeDtypeStruct((n_slot, tp, fs), BF16)
    outs = pl.pallas_call(
        body, name=name, grid=(n_tok, n_slot),
        in_specs=[tok, tok, tok, slot_act, slot_act, gain, gain]
        + [pl.BlockSpec((None, None, fs, d), lambda i, j, kind=kind: (j, kind, 0, 0)) for kind in (kind_gate, kind_up, kind_dn)]
        + [_ANY] * n_ji,
        out_specs=[tok, slot_act, slot_act, slot_act, tok, tok, acc, acc] + [_ANY] * n_jo,
        out_shape=[jax.ShapeDtypeStruct((tp, d), F32), act_shape, act_shape, act_shape,
                   jax.ShapeDtypeStruct((tp, d), BF16), jax.ShapeDtypeStruct((tp, d), BF16),
                   jax.ShapeDtypeStruct((1, d), F32), jax.ShapeDtypeStruct((1, d), F32)] + (list(job.out_shape) if job else []),
        scratch_shapes=[pltpu.VMEM((tm, d), F32)] + (list(job.scratch) if job else []),
        compiler_params=_cparams(("arbitrary", "arbitrary")),
    )(dh_out, h, f, gate, up, g_pre, g_post, w_ffn, w_ffn, w_ffn, *(job.inputs if job else ()))
    return tuple(outs[:8]) + (list(outs[8:]),)


def _mm_tn(a, b, tk, out_dtype, name, job=None):
    ga = a.shape[0] if a.ndim == 3 else None
    gb = b.shape[0] if b.ndim == 3 else None
    groups = ga or gb or 1
    t, m = a.shape[-2:]
    n = b.shape[-1]
    nk = t // tk
    n_ji = len(job.inputs) if job else 0
    n_jo = len(job.out_shape) if job else 0

    def body(a_ref, b_ref, *rest):
        job_in, o_ref, rest = rest[:n_ji], rest[n_ji], rest[n_ji + 1:]
        job_out, acc_sc, sems = rest[:n_jo], rest[n_jo], rest[n_jo + 1:]
        k = pl.program_id(1)
        if job:
            pl.when((pl.program_id(0) == 0) & (k == 0))(lambda: job.start(job_in, job_out, sems))

        @pl.when(k == 0)
        def _():
            acc_sc[...] = jnp.zeros_like(acc_sc)

        acc_sc[...] += _dot_tn(a_ref[...], b_ref[...])

        @pl.when(k == nk - 1)
        def _():
            o_ref[...] = acc_sc[...].astype(out_dtype)

        if job:
            pl.when((pl.program_id(0) == groups - 1) & (k == nk - 1))(lambda: job.finish(job_in, job_out, sems))

    a_spec = (pl.BlockSpec((None, tk, m), lambda g, k: (g, k, 0)) if ga else pl.BlockSpec((tk, m), lambda g, k: (k, 0)))
    b_spec = (pl.BlockSpec((None, tk, n), lambda g, k: (g, k, 0)) if gb else pl.BlockSpec((tk, n), lambda g, k: (k, 0)))
    outs = pl.pallas_call(
        body, name=name, grid=(groups, nk),
        in_specs=[a_spec, b_spec] + [_ANY] * n_ji,
        out_specs=[pl.BlockSpec((None, m, n), lambda g, k: (g, 0, 0))] + [_ANY] * n_jo,
        out_shape=[jax.ShapeDtypeStruct((groups, m, n), out_dtype)] + (list(job.out_shape) if job else []),
        scratch_shapes=[pltpu.VMEM((m, n), F32)] + (list(job.scratch) if job else []),
        compiler_params=_cparams(("arbitrary", "arbitrary") if job else ("parallel", "arbitrary")),
    )(a, b, *(job.inputs if job else ()))
    out = outs[0] if (ga or gb) else outs[0][0]
    return (out, list(outs[1:])) if job else out


def _mm_tn_heads(a, b, tk, name):
    groups, t, m = a.shape
    n = b.shape[-1]
    nk = t // tk

    def body(a_ref, b_ref, o_ref):
        k = pl.program_id(0)

        @pl.when(k == 0)
        def _():
            o_ref[...] = jnp.zeros_like(o_ref)

        bb = b_ref[...]
        for g in range(groups):
            o_ref[g] += _dot_tn(a_ref[g], bb)

    return pl.pallas_call(
        body, name=name, grid=(nk,),
        in_specs=[pl.BlockSpec((groups, tk, m), lambda k: (0, k, 0)), pl.BlockSpec((tk, n), lambda k: (k, 0))],
        out_specs=pl.BlockSpec((groups, m, n), lambda k: (0, 0, 0)),
        out_shape=jax.ShapeDtypeStruct((groups, m, n), F32),
        compiler_params=_cparams(("arbitrary",)),
    )(a, b)


def _mix_in_fwd(h, g_pre, w_parts, g_q, g_kv, w_q, w_k, w_v, cos_t, sin_t, rot, layer, tm, name):
    tp, d = h.shape
    w_xr, w_gr, w_cq, w_ckv, w_kr = w_parts
    lw, ql, kvl = w_xr.shape[0], w_cq.shape[0], w_ckv.shape[0]
    nh, vd = w_v.shape[0], w_v.shape[-1]

    def body(h_ref, gpre_ref, wxr_ref, wgr_ref, wcq_ref, wckv_ref, wkr_ref, gq_ref, gkv_ref, wq_ref, wk_ref, wv_ref,
             cos_ref, sin_ref, rot_ref, xr_ref, gr_ref, cq_ref, ckv_ref, q_ref, k_ref, v_ref):
        u, _ = _rms(h_ref[...], gpre_ref[...])
        u = u.astype(BF16)
        xr_ref[...] = _dot_nt(u, wxr_ref[...])
        gr_ref[...] = _dot_nt(u, wgr_ref[...])
        cq = _dot_nt(u, wcq_ref[...])
        ckv = _dot_nt(u, wckv_ref[...])
        kr = _dot_nt(u, wkr_ref[...])
        cq_ref[...] = cq
        ckv_ref[...] = ckv
        cos_b, sin_b, rot_b = cos_ref[...], sin_ref[...], rot_ref[...]
        qn = _rms(cq, gq_ref[...])[0].astype(BF16)
        kvn = _rms(ckv, gkv_ref[...])[0].astype(BF16)
        k_rope = _rope(kr, cos_b, sin_b, rot_b)
        for hd in range(nh):
            q_pre = _dot_nt(qn, wq_ref[hd])
            q_ref[hd] = (_rope(q_pre, cos_b, sin_b, rot_b) * Q_PRESCALE).astype(BF16)
            k_ref[hd] = (jnp.dot(kvn, wk_ref[hd], preferred_element_type=F32) + k_rope).astype(BF16)
            v_ref[hd] = jnp.dot(kvn, wv_ref[hd], preferred_element_type=F32).astype(BF16)

    def tok(n):
        return pl.BlockSpec((tm, n), lambda i: (i, 0))

    def lay(*shape):
        return pl.BlockSpec((None,) + shape, lambda i: (layer,) + (0,) * len(shape))

    def heads(n):
        return pl.BlockSpec((nh, tm, n), lambda i: (0, i, 0))

    return pl.pallas_call(
        body, name=name, grid=(tp // tm,),
        in_specs=[tok(d), lay(1, d), _whole(lw, d), _whole(lw, d), _whole(ql, d), _whole(kvl, d), _whole(QK_DIM, d),
                  lay(1, ql), lay(1, kvl),
                  _whole(nh, QK_DIM, ql), _whole(nh, kvl, QK_DIM), _whole(nh, kvl, vd), tok(QK_DIM), tok(QK_DIM),
                  pl.BlockSpec((QK_DIM, QK_DIM), lambda i: (0, 0))],
        out_specs=[tok(lw), tok(lw), tok(ql), tok(kvl), heads(QK_DIM), heads(QK_DIM), heads(vd)],
        out_shape=[jax.ShapeDtypeStruct((tp, lw), F32), jax.ShapeDtypeStruct((tp, lw), F32),
                   jax.ShapeDtypeStruct((tp, ql), F32), jax.ShapeDtypeStruct((tp, kvl), F32),
                   jax.ShapeDtypeStruct((nh, tp, QK_DIM), BF16), jax.ShapeDtypeStruct((nh, tp, QK_DIM), BF16),
                   jax.ShapeDtypeStruct((nh, tp, vd), BF16)],
        compiler_params=_cparams(("parallel",)),
    )(h, g_pre, w_xr, w_gr, w_cq, w_ckv, w_kr, g_q, g_kv, w_q, w_k, w_v, cos_t, sin_t, rot)


def _mix_in_bwd(dh_res, h, cq, ckv, d_xr, d_gr, dq, dk, dv, g_pre, w_parts, g_q, g_kv, w_q, w_k, w_v,
                cos_t, sin_t, rot_t, layer, tm, name):
    tp, d = h.shape
    w_xr, w_gr, w_cq, w_ckv, w_kr = w_parts
    lw, ql, kvl = w_xr.shape[0], w_cq.shape[0], w_ckv.shape[0]
    nh, vd = w_v.shape[0], w_v.shape[-1]

    def body(dhr_ref, h_ref, cq_ref, ckv_ref, dxr_ref, dgr_ref, dq_ref, dk_ref, dv_ref, gpre_ref,
             wxr_ref, wgr_ref, wcq_ref, wckv_ref, wkr_ref, gq_ref, gkv_ref, wq_ref, wk_ref, wv_ref,
             cos_ref, sin_ref, rott_ref,
             dh_ref, u_ref, qn_ref, kvn_ref, dqpre_ref, dcq_ref, dckv_ref, dkr_ref, dgpre_ref, dgq_ref, dgkv_ref):
        i = pl.program_id(0)

        @pl.when(i == 0)
        def _():
            dgpre_ref[...] = jnp.zeros_like(dgpre_ref)
            dgq_ref[...] = jnp.zeros_like(dgq_ref)
            dgkv_ref[...] = jnp.zeros_like(dgkv_ref)

        cos_b, sin_b, rott_b = cos_ref[...], sin_ref[...], rott_ref[...]
        hh = h_ref[...]
        u, _ = _rms(hh, gpre_ref[...])
        u_ref[...] = u.astype(BF16)
        cq, ckv = cq_ref[...], ckv_ref[...]
        qn = _rms(cq, gq_ref[...])[0]
        kvn = _rms(ckv, gkv_ref[...])[0]
        qn_ref[...] = qn.astype(BF16)
        kvn_ref[...] = kvn.astype(BF16)
        d_qn = jnp.zeros((tm, ql), F32)
        d_kvn = jnp.zeros((tm, kvl), F32)
        d_krope = jnp.zeros((tm, QK_DIM), F32)
        for hd in range(nh):
            dq_h = dq_ref[hd]
            dq_pre = dq_h * cos_b + _dot_split(dq_h * sin_b, rott_b)
            dq_pre_b = dq_pre.astype(BF16)
            dqpre_ref[hd] = dq_pre_b
            d_qn += _dot(dq_pre_b, wq_ref[hd])
            dk_h = dk_ref[hd]
            d_krope += dk_h
            d_kvn += _dot_nt(dk_h, wk_ref[hd]) + _dot_nt(dv_ref[hd], wv_ref[hd])
        d_kr = d_krope * cos_b + _dot_split(d_krope * sin_b, rott_b)
        d_cq, dgq = _rms_bwd(cq, gq_ref[...], d_qn)
        d_ckv, dgkv = _rms_bwd(ckv, gkv_ref[...], d_kvn)
        dgq_ref[...] += dgq
        dgkv_ref[...] += dgkv
        d_cq_b, d_ckv_b, d_kr_b = d_cq.astype(BF16), d_ckv.astype(BF16), d_kr.astype(BF16)
        dcq_ref[...] = d_cq_b
        dckv_ref[...] = d_ckv_b
        dkr_ref[...] = d_kr_b
        du = (_dot(dxr_ref[...], wxr_ref[...]) + _dot(dgr_ref[...], wgr_ref[...]) + _dot(d_cq_b, wcq_ref[...])
              + _dot(d_ckv_b, wckv_ref[...]) + _dot(d_kr_b, wkr_ref[...]))
        dx, dg = _rms_bwd(hh, gpre_ref[...], du)
        dh_ref[...] = dhr_ref[...] + dx
        dgpre_ref[...] += dg

    def tok(n):
        return pl.BlockSpec((tm, n), lambda i: (i, 0))

    def lay(*shape):
        return pl.BlockSpec((None,) + shape, lambda i: (layer,) + (0,) * len(shape))

    def heads(n):
        return pl.BlockSpec((nh, tm, n), lambda i: (0, i, 0))

    def acc(n):
        return pl.BlockSpec((1, n), lambda i: (0, 0))

    return pl.pallas_call(
        body, name=name, grid=(tp // tm,),
        in_specs=[tok(d), tok(d), tok(ql), tok(kvl), tok(lw), tok(lw), heads(QK_DIM), heads(QK_DIM), heads(vd), lay(1, d),
                  _whole(lw, d), _whole(lw, d), _whole(ql, d), _whole(kvl, d), _whole(QK_DIM, d), lay(1, ql), lay(1, kvl),
                  _whole(nh, QK_DIM, ql), _whole(nh, kvl, QK_DIM), _whole(nh, kvl, vd), tok(QK_DIM), tok(QK_DIM),
                  pl.BlockSpec((QK_DIM, QK_DIM), lambda i: (0, 0))],
        out_specs=[tok(d), tok(d), tok(ql), tok(kvl), heads(QK_DIM), tok(ql), tok(kvl), tok(QK_DIM), acc(d), acc(ql), acc(kvl)],
        out_shape=[jax.ShapeDtypeStruct((tp, d), F32), jax.ShapeDtypeStruct((tp, d), BF16),
                   jax.ShapeDtypeStruct((tp, ql), BF16), jax.ShapeDtypeStruct((tp, kvl), BF16),
                   jax.ShapeDtypeStruct((nh, tp, QK_DIM), BF16), jax.ShapeDtypeStruct((tp, ql), BF16),
                   jax.ShapeDtypeStruct((tp, kvl), BF16), jax.ShapeDtypeStruct((tp, QK_DIM), BF16),
                   jax.ShapeDtypeStruct((1, d), F32), jax.ShapeDtypeStruct((1, ql), F32), jax.ShapeDtypeStruct((1, kvl), F32)],
        compiler_params=_cparams(("arbitrary",)),
    )(dh_res, h, cq, ckv, d_xr, d_gr, dq, dk, dv, g_pre, w_xr, w_gr, w_cq, w_ckv, w_kr, g_q, g_kv, w_q, w_k, w_v,
      cos_t, sin_t, rot_t)


def _lru_gates(xc, wa_ref, wx_ref, ba, bx, sp):
    xcb = xc.astype(BF16)
    r = _sigmoid(jnp.dot(xcb, wa_ref[...], preferred_element_type=F32) + ba)
    ig = _sigmoid(jnp.dot(xcb, wx_ref[...], preferred_element_type=F32) + bx)
    log_a = -LRU_C * r * sp
    a = jnp.exp(log_a)
    sq = jnp.sqrt(_neg_expm1(2.0 * log_a))
    return r, ig, a, sq


def _softplus(x):
    return jnp.maximum(x, 0.0) + jnp.log(1.0 + jnp.exp(-jnp.abs(x)))


def _lru_fwd(xr, gr, conv_w, conv_b, wa_bd, wx_bd, b_a, b_x, lam, layer, tc, name):
    tp, w = xr.shape
    pad = 8

    def body(xr_ref, gr_ref, cw_ref, cb_ref, wa_ref, wx_ref, ba_ref, bx_ref, lam_ref, y_ref, xc_ref, hs_ref,
             xe_sc, a_sc, b_sc, st_sc):
        c = pl.program_id(0)

        @pl.when(c == 0)
        def _():
            xe_sc[pl.ds(0, pad), :] = jnp.zeros((pad, w), F32)
            st_sc[...] = jnp.zeros_like(st_sc)

        xe_sc[pl.ds(pad, tc), :] = xr_ref[...]
        xc = cb_ref[...] + xe_sc[pl.ds(pad, tc), :] * cw_ref[pl.ds(CONV_WIDTH - 1, 1), :]
        for k in range(CONV_WIDTH - 1):
            xc = xc + xe_sc[pl.ds(pad - (CONV_WIDTH - 1) + k, tc), :] * cw_ref[pl.ds(k, 1), :]
        xe_sc[pl.ds(0, pad), :] = xe_sc[pl.ds(tc, pad), :]
        xc_ref[...] = xc
        sp = _softplus(-lam_ref[...])
        _, ig, a, sq = _lru_gates(xc, wa_ref, wx_ref, ba_ref[...], bx_ref[...], sp)
        a_sc[...] = a
        b_sc[...] = sq * (ig * xc)

        def step(t, hcur):
            hnew = a_sc[pl.ds(t, 1), :] * hcur + b_sc[pl.ds(t, 1), :]
            hs_ref[pl.ds(t, 1), :] = hnew
            return hnew

        st_sc[...] = lax.fori_loop(0, tc, step, st_sc[...], unroll=8)
        y_ref[...] = hs_ref[...] * _gelu(gr_ref[...])[0]

    tok = pl.BlockSpec((tc, w), lambda c: (c, 0))

    def lay(*shape):
        return pl.BlockSpec((None,) + shape, lambda c: (layer,) + (0,) * len(shape))

    out = jax.ShapeDtypeStruct((tp, w), F32)
    return pl.pallas_call(
        body, name=name, grid=(tp // tc,),
        in_specs=[tok, tok, lay(CONV_WIDTH, w), lay(1, w), lay(w, w), lay(w, w), lay(1, w), lay(1, w), lay(1, w)],
        out_specs=[tok, tok, tok], out_shape=[out, out, out],
        scratch_shapes=[pltpu.VMEM((tc + pad, w), F32), pltpu.VMEM((tc, w), F32), pltpu.VMEM((tc, w), F32),
                        pltpu.VMEM((1, w), F32)],
        compiler_params=_cparams(("arbitrary",)),
    )(xr, gr, conv_w, conv_b, wa_bd, wx_bd, b_a, b_x, lam)


def _lru_bwd(dy, xr, gr, xc, hs, conv_w, wa_bd, wx_bd, b_a, b_x, lam, layer, tc, name):
    tp, w = xr.shape
    pad = 8
    nc = tp // tc
    per = tc // pad

    def body(dy_ref, xr_ref, gr_ref, xc_ref, hs_ref, xrp_ref, hsp_ref, cw_ref, wa_ref, wx_ref, ba_ref, bx_ref, lam_ref,
             dxr_ref, dgr_ref, dcw_ref, dcb_ref, dwa_ref, dwx_ref, dba_ref, dbx_ref, dlam_ref,
             a_sc, dh_sc, dxc_sc, he_sc, xe_sc, carry_sc):
        s = pl.program_id(0)
        first_chunk = s == nc - 1

        @pl.when(s == 0)
        def _():
            for ref in (dcw_ref, dcb_ref, dwa_ref, dwx_ref, dba_ref, dbx_ref, dlam_ref):
                ref[...] = jnp.zeros_like(ref)
            carry_sc[...] = jnp.zeros_like(carry_sc)
            dxc_sc[pl.ds(tc, pad), :] = jnp.zeros((pad, w), F32)

        keep = jnp.where(first_chunk, 0.0, 1.0)
        he_sc[pl.ds(0, pad), :] = hsp_ref[...] * keep
        he_sc[pl.ds(pad, tc), :] = hs_ref[...]
        xe_sc[pl.ds(0, pad), :] = xrp_ref[...] * keep
        xe_sc[pl.ds(pad, tc), :] = xr_ref[...]

        lam_v = lam_ref[...]
        sp = _softplus(-lam_v)
        xc = xc_ref[...]
        r, ig, a, sq = _lru_gates(xc, wa_ref, wx_ref, ba_ref[...], bx_ref[...], sp)
        a_sc[...] = a
        grv = gr_ref[...]
        gl, th = _gelu(grv)
        dyv = dy_ref[...]
        dgr_ref[...] = dyv * hs_ref[...] * _gelu_grad(grv, th)
        dh_sc[...] = dyv * gl

        def step(n, g):
            t = tc - 1 - n
            dh = dh_sc[pl.ds(t, 1), :] + g
            dh_sc[pl.ds(t, 1), :] = dh
            return a_sc[pl.ds(t, 1), :] * dh

        carry_sc[...] = lax.fori_loop(0, tc, step, carry_sc[...], unroll=8)

        dh = dh_sc[...]
        d_a = dh * he_sc[pl.ds(pad - 1, tc), :]
        d_ixc = dh * sq
        d_sq = dh * (ig * xc)
        a2 = a * a
        d_la = d_a * a - d_sq * a2 / sq
        d_r = d_la * (-LRU_C * sp)
        d_sp = jnp.sum(d_la * (-LRU_C * r), axis=0, keepdims=True)
        dlam_ref[...] += d_sp * (-_sigmoid(-lam_v))
        d_pa = d_r * r * (1.0 - r)
        d_px = d_ixc * xc * ig * (1.0 - ig)
        dba_ref[...] += jnp.sum(d_pa, axis=0, keepdims=True)
        dbx_ref[...] += jnp.sum(d_px, axis=0, keepdims=True)
        d_pa_b, d_px_b = d_pa.astype(BF16), d_px.astype(BF16)
        xcb = xc.astype(BF16)
        dwa_ref[...] += _dot_tn(xcb, d_pa_b)
        dwx_ref[...] += _dot_tn(xcb, d_px_b)
        d_xc = d_ixc * ig + _dot_nt(d_pa_b, wa_ref[...]) + _dot_nt(d_px_b, wx_ref[...])
        dcb_ref[...] += jnp.sum(d_xc, axis=0, keepdims=True)
        dxc_sc[pl.ds(0, tc), :] = d_xc
        d_xr = jnp.zeros((tc, w), F32)
        for k in range(CONV_WIDTH):
            off = CONV_WIDTH - 1 - k
            d_xr = d_xr + dxc_sc[pl.ds(off, tc), :] * cw_ref[pl.ds(k, 1), :]
            dcw_ref[pl.ds(k, 1), :] += jnp.sum(d_xc * xe_sc[pl.ds(pad - off, tc), :], axis=0, keepdims=True)
        dxr_ref[...] = d_xr
        dxc_sc[pl.ds(tc, pad), :] = dxc_sc[pl.ds(0, pad), :]

    def rev(cidx):
        return nc - 1 - cidx

    tok = pl.BlockSpec((tc, w), lambda c: (rev(c), 0))
    prev = pl.BlockSpec((pad, w), lambda c: (jnp.maximum(rev(c) * per - 1, 0), 0))

    def lay(*shape):
        return pl.BlockSpec((None,) + shape, lambda c: (layer,) + (0,) * len(shape))

    def acc(*shape):
        return pl.BlockSpec(shape, lambda c: (0,) * len(shape))

    big = jax.ShapeDtypeStruct((tp, w), F32)
    vec = jax.ShapeDtypeStruct((1, w), F32)
    return pl.pallas_call(
        body, name=name, grid=(nc,),
        in_specs=[tok, tok, tok, tok, tok, prev, prev, lay(CONV_WIDTH, w), lay(w, w), lay(w, w), lay(1, w), lay(1, w), lay(1, w)],
        out_specs=[tok, tok, acc(CONV_WIDTH, w), acc(1, w), acc(w, w), acc(w, w), acc(1, w), acc(1, w), acc(1, w)],
        out_shape=[big, big, jax.ShapeDtypeStruct((CONV_WIDTH, w), F32), vec, jax.ShapeDtypeStruct((w, w), F32),
                   jax.ShapeDtypeStruct((w, w), F32), vec, vec, vec],
        scratch_shapes=[pltpu.VMEM((tc, w), F32), pltpu.VMEM((tc, w), F32), pltpu.VMEM((tc + pad, w), F32),
                        pltpu.VMEM((tc + pad, w), F32), pltpu.VMEM((tc + pad, w), F32), pltpu.VMEM((1, w), F32)],
        compiler_params=_cparams(("arbitrary",)),
    )(dy, xr, gr, xc, hs, xr, hs, conv_w, wa_bd, wx_bd, b_a, b_x, lam)


def _attn_fwd(q, k, v, tq, name, job=None):
    nh, tp, dqk = q.shape
    vd = v.shape[-1]
    nq = tp // tq
    n_ji = len(job.inputs) if job else 0
    n_jo = len(job.out_shape) if job else 0

    def body(q_ref, k_ref, v_ref, *rest):
        job_in, (o_ref, lse_ref), rest = rest[:n_ji], rest[n_ji:n_ji + 2], rest[n_ji + 2:]
        job_out, sems = rest[:n_jo], rest[n_jo:]
        i = pl.program_id(1)
        if job:
            first = (pl.program_id(0) == 0) & (i == 0)
            pl.when(first)(lambda: job.start(job_in, job_out, sems))
        qb = q_ref[...]

        def absorb(j, width, carry, diagonal=False):
            m, l, acc = carry
            off = pl.multiple_of(j * tq, tq)
            s = _dot_nt(qb, k_ref[pl.ds(off, width), :])
            if diagonal:
                keep = lax.broadcasted_iota(jnp.int32, (tq, width), 1) <= lax.broadcasted_iota(jnp.int32, (tq, width), 0)
                s = jnp.where(keep, s, NEG_BIG)
            m_new = jnp.maximum(m, jnp.max(s, axis=-1, keepdims=True))
            p = jnp.exp2(s - m_new)
            alpha = jnp.exp2(m - m_new)
            l = alpha * l + jnp.sum(p, axis=-1, keepdims=True)
            acc = alpha * acc + jnp.dot(p.astype(BF16), v_ref[pl.ds(off, width), :], preferred_element_type=F32)
            return m_new, l, acc

        carry = (jnp.full((tq, 1), NEG_BIG, F32), jnp.zeros((tq, 1), F32), jnp.zeros((tq, vd), F32))
        quads = lax.shift_right_logical(i, 2)
        carry = lax.fori_loop(0, quads, lambda jj, c: absorb(4 * jj, 4 * tq, c), carry)
        carry = lax.cond((i & 2) != 0, lambda c: absorb(4 * quads, 2 * tq, c), lambda c: c, carry)
        carry = lax.cond((i & 1) != 0, lambda c: absorb(i - 1, tq, c), lambda c: c, carry)
        m, l, acc = absorb(i, tq, carry, diagonal=True)
        o_ref[...] = acc / l
        lse_ref[...] = m + jnp.log2(l)
        if job:
            last = (pl.program_id(0) == nh - 1) & (i == nq - 1)
            pl.when(last)(lambda: job.finish(job_in, job_out, sems))

    outs = pl.pallas_call(
        body, name=name, grid=(nh, nq),
        in_specs=[pl.BlockSpec((None, tq, dqk), lambda h, i: (h, i, 0)),
                  pl.BlockSpec((None, tp, dqk), lambda h, i: (h, 0, 0)),
                  pl.BlockSpec((None, tp, vd), lambda h, i: (h, 0, 0))] + [_ANY] * n_ji,
        out_specs=[pl.BlockSpec((None, tq, vd), lambda h, i: (h, i, 0)),
                   pl.BlockSpec((None, tq, 1), lambda h, i: (h, i, 0))] + [_ANY] * n_jo,
        out_shape=[jax.ShapeDtypeStruct((nh, tp, vd), F32), jax.ShapeDtypeStruct((nh, tp, 1), F32)]
        + (list(job.out_shape) if job else []),
        scratch_shapes=list(job.scratch) if job else [],
        compiler_params=_cparams(("arbitrary", "arbitrary") if job else ("parallel", "parallel")),
    )(q, k, v, *(job.inputs if job else ()))
    return outs[0], outs[1], list(outs[2:])


def _attn_bwd(q, k, v, o, do, lse, tq, name, job=None):
    nh, tp, dqk = q.shape
    vd = v.shape[-1]
    scale = dqk ** -0.5
    nq = tp // tq
    n_ji = len(job.inputs) if job else 0
    n_jo = len(job.out_shape) if job else 0

    def body(q_ref, k_ref, v_ref, o_ref, do_ref, lse_ref, *rest):
        job_in, (dq_ref, dk_ref, dv_ref), rest = rest[:n_ji], rest[n_ji:n_ji + 3], rest[n_ji + 3:]
        job_out, (lse_rows, delta_rows, do_bf, dqt_sc), sems = rest[:n_jo], rest[n_jo:n_jo + 4], rest[n_jo + 4:]
        kb_i = pl.program_id(1)
        if job:
            first = (pl.program_id(0) == 0) & (kb_i == 0)
            pl.when(first)(lambda: job.start(job_in, job_out, sems))

        def as_rows(col):
            return jnp.transpose(jnp.broadcast_to(col, (tq, _LANES)))[0:8, :]

        @pl.when(kb_i == 0)
        def _():
            dqt_sc[...] = jnp.zeros_like(dqt_sc)

            def prep(qi, _):
                off = pl.multiple_of(qi * tq, tq)
                dob = do_ref[pl.ds(off, tq), :]
                do_bf[pl.ds(off, tq), :] = dob.astype(BF16)
                delta_rows[qi] = as_rows(jnp.sum(dob * o_ref[pl.ds(off, tq), :], axis=-1, keepdims=True))
                lse_rows[qi] = as_rows(lse_ref[pl.ds(off, tq), :])
                return 0

            lax.fori_loop(0, nq, prep, 0)

        kb = k_ref[...]
        vb = v_ref[...]

        def absorb(qi, n_blk, carry, diagonal=False):
            dk, dv = carry
            off = pl.multiple_of(qi * tq, tq)
            qb = q_ref[pl.ds(off, n_blk * tq), :]
            dob = do_bf[pl.ds(off, n_blk * tq), :]
            lse_row = jnp.concatenate([lse_rows[qi + b, 0:1, :] for b in range(n_blk)], axis=1)
            delta_row = jnp.concatenate([delta_rows[qi + b, 0:1, :] for b in range(n_blk)], axis=1)
            st = _dot_nt(kb, qb)
            if diagonal:
                keep = lax.broadcasted_iota(jnp.int32, (tq, tq), 0) <= lax.broadcasted_iota(jnp.int32, (tq, tq), 1)
                st = jnp.where(keep, st, NEG_BIG)
            pt = jnp.exp2(st - lse_row)
            dpt = _dot_nt(vb, dob)
            dst = (pt * (dpt - delta_row)).astype(BF16)
            dv = dv + jnp.dot(pt.astype(BF16), dob, preferred_element_type=F32)
            dk = dk + jnp.dot(dst, qb, preferred_element_type=F32)
            dqt = _dot_tn(kb, dst)
            for b in range(n_blk):
                dqt_sc[qi + b] += dqt[:, b * tq:(b + 1) * tq]
            return dk, dv

        carry = absorb(kb_i, 1, (jnp.zeros((tq, dqk), F32), jnp.zeros((tq, vd), F32)), diagonal=True)
        later = nq - 1 - kb_i
        carry = lax.fori_loop(0, later // 2, lambda pp, c: absorb(kb_i + 1 + 2 * pp, 2, c), carry)
        dk, dv = lax.cond(later % 2 == 1, lambda c: absorb(nq - 1, 1, c), lambda c: c, carry)
        dk_ref[...] = dk * math.log(2.0)
        dv_ref[...] = dv

        @pl.when(kb_i == nq - 1)
        def _():
            eye = (lax.broadcasted_iota(jnp.int32, (dqk, dqk), 0) == lax.broadcasted_iota(jnp.int32, (dqk, dqk), 1))
            eye = jnp.where(eye, 1.0, 0.0).astype(BF16)
            for qi in range(nq):
                t = dqt_sc[qi]
                hi = t.astype(BF16)
                lo = (t - hi.astype(F32)).astype(BF16)
                dq_ref[pl.ds(qi * tq, tq), :] = scale * (_dot_tn(hi, eye) + _dot_tn(lo, eye))

        if job:
            last = (pl.program_id(0) == nh - 1) & (kb_i == nq - 1)
            pl.when(last)(lambda: job.finish(job_in, job_out, sems))

    def full(n):
        return pl.BlockSpec((None, tp, n), lambda h, j: (h, 0, 0))

    def blk(n):
        return pl.BlockSpec((None, tq, n), lambda h, j: (h, j, 0))

    outs = pl.pallas_call(
        body, name=name, grid=(nh, nq),
        in_specs=[full(dqk), blk(dqk), blk(vd), full(vd), full(vd), full(1)] + [_ANY] * n_ji,
        out_specs=[full(dqk), blk(dqk), blk(vd)] + [_ANY] * n_jo,
        out_shape=[jax.ShapeDtypeStruct((nh, tp, dqk), F32), jax.ShapeDtypeStruct((nh, tp, dqk), F32),
                   jax.ShapeDtypeStruct((nh, tp, vd), F32)] + (list(job.out_shape) if job else []),
        scratch_shapes=[pltpu.VMEM((nq, 8, tq), F32), pltpu.VMEM((nq, 8, tq), F32), pltpu.VMEM((tp, vd), BF16),
                        pltpu.VMEM((nq, dqk, tq), F32)] + (list(job.scratch) if job else []),
        compiler_params=_cparams(("arbitrary", "arbitrary") if job else ("parallel", "arbitrary")),
    )(q, k, v, o, do, lse, *(job.inputs if job else ()))
    return outs[0], outs[1], outs[2], list(outs[3:])


def _mix_out_fwd(h, y_lru, o, g_lo, g_mo, w_lo, w_mo, g_post, layer, tm, name):
    tp, d = h.shape
    lw = y_lru.shape[-1]
    nh, vd = o.shape[0], o.shape[-1]

    def body(h_ref, yl_ref, o_ref, glo_ref, gmo_ref, wlo_ref, wmo_ref, gpost_ref, hout_ref, y_ref):
        nl, _ = _rms(yl_ref[...], glo_ref[...])
        y = _dot(nl, wlo_ref[...])
        ssq = jnp.zeros((tm, 1), F32)
        for hd in range(nh):
            oh = o_ref[hd]
            ssq += jnp.sum(oh * oh, axis=-1, keepdims=True)
        r = lax.rsqrt(ssq / (nh * vd) + EPS)
        for hd in range(nh):
            y += _dot(o_ref[hd] * r * gmo_ref[hd], wmo_ref[hd])
        y_ref[...] = y
        hout_ref[...] = h_ref[...] + _rms(y, gpost_ref[...])[0]

    tok = pl.BlockSpec((tm, d), lambda i: (i, 0))

    def lay(*shape):
        return pl.BlockSpec((None,) + shape, lambda i: (layer,) + (0,) * len(shape))

    return pl.pallas_call(
        body, name=name, grid=(tp // tm,),
        in_specs=[tok, pl.BlockSpec((tm, lw), lambda i: (i, 0)), pl.BlockSpec((nh, tm, vd), lambda i: (0, i, 0)),
                  lay(1, lw), lay(nh, 1, vd), _whole(lw, d), _whole(nh, vd, d), lay(1, d)],
        out_specs=[tok, tok],
        out_shape=[jax.ShapeDtypeStruct((tp, d), F32), jax.ShapeDtypeStruct((tp, d), F32)],
        compiler_params=_cparams(("parallel",)),
    )(h, y_lru, o, g_lo, g_mo, w_lo, w_mo, g_post)


def _mix_out_bwd(dh_out, y, y_lru, o, g_lo, g_mo, w_lo, w_mo, g_post, layer, tm, name):
    tp, d = y.shape
    lw = y_lru.shape[-1]
    nh, vd = o.shape[0], o.shape[-1]

    def body(dho_ref, y_ref, yl_ref, o_ref, glo_ref, gmo_ref, wlo_ref, wmo_ref, gpost_ref,
             dyl_ref, do_ref, dy_ref, nl_ref, nm_ref, dgpost_ref, dglo_ref, dgmo_ref):
        i = pl.program_id(0)

        @pl.when(i == 0)
        def _():
            dgpost_ref[...] = jnp.zeros_like(dgpost_ref)
            dglo_ref[...] = jnp.zeros_like(dglo_ref)
            dgmo_ref[...] = jnp.zeros_like(dgmo_ref)

        dy, dgp = _rms_bwd(y_ref[...], gpost_ref[...], dho_ref[...])
        dgpost_ref[...] += dgp
        dy_b = dy.astype(BF16)
        dy_ref[...] = dy_b
        yl = yl_ref[...]
        nl_ref[...] = _rms(yl, glo_ref[...])[0].astype(BF16)
        dyl, dgl = _rms_bwd(yl, glo_ref[...], _dot_nt(dy_b, wlo_ref[...]))
        dyl_ref[...] = dyl
        dglo_ref[...] += dgl
        ssq = jnp.zeros((tm, 1), F32)
        for hd in range(nh):
            oh = o_ref[hd]
            ssq += jnp.sum(oh * oh, axis=-1, keepdims=True)
        r = lax.rsqrt(ssq / (nh * vd) + EPS)
        dn = []
        dot_sum = jnp.zeros((tm, 1), F32)
        for hd in range(nh):
            oh = o_ref[hd] * r
            nm_ref[hd] = (oh * gmo_ref[hd]).astype(BF16)
            dn_h = _dot_nt(dy_b, wmo_ref[hd])
            dgmo_ref[hd] += jnp.sum(dn_h * oh, axis=0, keepdims=True)
            dn_h = dn_h * gmo_ref[hd]
            dot_sum += jnp.sum(dn_h * oh, axis=-1, keepdims=True)
            dn.append(dn_h)
        dot_mean = dot_sum / (nh * vd)
        for hd in range(nh):
            do_ref[hd] = r * (dn[hd] - o_ref[hd] * r * dot_mean)

    tok = pl.BlockSpec((tm, d), lambda i: (i, 0))
    tokl = pl.BlockSpec((tm, lw), lambda i: (i, 0))
    heads = pl.BlockSpec((nh, tm, vd), lambda i: (0, i, 0))

    def lay(*shape):
        return pl.BlockSpec((None,) + shape, lambda i: (layer,) + (0,) * len(shape))

    def acc(*shape):
        return pl.BlockSpec(shape, lambda i: (0,) * len(shape))

    return pl.pallas_call(
        body, name=name, grid=(tp // tm,),
        in_specs=[tok, tok, tokl, heads, lay(1, lw), lay(nh, 1, vd), _whole(lw, d), _whole(nh, vd, d), lay(1, d)],
        out_specs=[tokl, heads, tok, tokl, heads, acc(1, d), acc(1, lw), acc(nh, 1, vd)],
        out_shape=[jax.ShapeDtypeStruct((tp, lw), F32), jax.ShapeDtypeStruct((nh, tp, vd), F32),
                   jax.ShapeDtypeStruct((tp, d), BF16), jax.ShapeDtypeStruct((tp, lw), BF16),
                   jax.ShapeDtypeStruct((nh, tp, vd), BF16), jax.ShapeDtypeStruct((1, d), F32),
                   jax.ShapeDtypeStruct((1, lw), F32), jax.ShapeDtypeStruct((nh, 1, vd), F32)],
        compiler_params=_cparams(("arbitrary",)),
    )(dh_out, y, y_lru, o, g_lo, g_mo, w_lo, w_mo, g_post)


def _loss_head(h, target, n_real, tm, name):
    tp, d = h.shape
    nt = tp // tm

    def body(h_ref, t_ref, dh_ref, part_ref):
        i = pl.program_id(0)
        row = i * tm + lax.broadcasted_iota(jnp.int32, (tm, 1), 0)
        real = (row >= N_META) & (row < N_META + n_real)
        err = jnp.where(real, h_ref[...] - t_ref[...], 0.0)
        dh_ref[...] = err * (1.0 / d)
        part_ref[...] = jnp.full((1, 1, 128), 0.5 / d * jnp.sum(err * err), F32)

    tok = pl.BlockSpec((tm, d), lambda i: (i, 0))
    return pl.pallas_call(
        body, name=name, grid=(nt,), in_specs=[tok, tok],
        out_specs=[tok, pl.BlockSpec((1, 1, 128), lambda i: (i, 0, 0))],
        out_shape=[jax.ShapeDtypeStruct((tp, d), F32), jax.ShapeDtypeStruct((nt, 1, 128), F32)],
        compiler_params=_cparams(("parallel",)),
    )(h, target)


def _mesh_pos():
    return lax.axis_index("x"), lax.axis_index("y"), lax.axis_index("c")


def _other_chips(x, y):
    return [(1 - x, y), (x, 1 - y), (1 - x, 1 - y)]


class _GatherJob:
    def __init__(self, inputs, picks, shard_shapes, dtypes, split_last=()):
        self.inputs = list(inputs)
        self.picks = list(picks)
        n = len(self.inputs)
        self.out_shape = [jax.ShapeDtypeStruct((N_CHIPS,) + tuple(s), dt) for s, dt in zip(shard_shapes, dtypes)]
        self.split_last = [a in split_last for a in range(n)]
        self.halves = [(s[1] if second else s[0]) // 2 for s, second in zip(shard_shapes, self.split_last)]
        self.scratch = [pltpu.SemaphoreType.DMA((n, 3)), pltpu.SemaphoreType.DMA((n, 3)), pltpu.SemaphoreType.DMA((n, 3)),
                        pltpu.SemaphoreType.DMA((n, 3)), pltpu.SemaphoreType.DMA((n,))]

    def _copies(self, ins, outs, sems):
        ici_s, ici_r, core_s, core_r, loc = sems
        x, y, c = _mesh_pos()
        me = 2 * x + y
        chips = _other_chips(x, y)
        local, ici, fwd = [], [], []
        for a in range(len(ins)):
            src = self.picks[a](ins[a])
            hv = self.halves[a]

            def half(ref, which, a=a, hv=hv):
                return ref.at[:, pl.ds(which * hv, hv)] if self.split_last[a] else ref.at[pl.ds(which * hv, hv)]

            local.append(pltpu.make_async_copy(src, outs[a].at[me], loc.at[a]))
            for p, (px, py) in enumerate(chips):
                ici.append((pltpu.make_async_remote_copy(src_ref=half(src, c), dst_ref=half(outs[a].at[me], c),
                                                         send_sem=ici_s.at[a, p], recv_sem=ici_r.at[a, p],
                                                         device_id=(px, py, c), device_id_type=MESH),
                            pltpu.make_async_remote_copy(src_ref=half(src, c), dst_ref=half(outs[a].at[2 * px + py], c),
                                                         send_sem=ici_s.at[a, p], recv_sem=ici_r.at[a, p],
                                                         device_id=(px, py, c), device_id_type=MESH)))
                landed = half(outs[a].at[2 * px + py], c)
                theirs = half(outs[a].at[2 * px + py], 1 - c)
                fwd.append((pltpu.make_async_remote_copy(src_ref=landed, dst_ref=landed, send_sem=core_s.at[a, p],
                                                         recv_sem=core_r.at[a, p], device_id=(x, y, 1 - c), device_id_type=MESH),
                            pltpu.make_async_remote_copy(src_ref=theirs, dst_ref=theirs, send_sem=core_s.at[a, p],
                                                         recv_sem=core_r.at[a, p], device_id=(x, y, 1 - c), device_id_type=MESH)))
        return local, ici, fwd

    def start(self, ins, outs, sems):
        local, ici, _ = self._copies(ins, outs, sems)
        for cp in local:
            cp.start()
        for send, _ in ici:
            send.start()

    def finish(self, ins, outs, sems):
        local, ici, fwd = self._copies(ins, outs, sems)
        for (_, arrive), (hand_on, _) in zip(ici, fwd):
            arrive.wait_recv()
            hand_on.start()
        for _, arrive in fwd:
            arrive.wait_recv()
        for send, _ in ici:
            send.wait_send()
        for hand_on, _ in fwd:
            hand_on.wait_send()
        for cp in local:
            cp.wait()


class _ReduceJob:
    def __init__(self, inputs):
        self.inputs = list(inputs)
        n = len(self.inputs)
        self.out_shape = [jax.ShapeDtypeStruct(a.shape, a.dtype) for a in self.inputs] * 2
        self.scratch = [pltpu.SemaphoreType.DMA((n, 3)), pltpu.SemaphoreType.DMA((n, 3)), pltpu.SemaphoreType.DMA((n, 4)),
                        pltpu.SemaphoreType.DMA((n, 4)), pltpu.SemaphoreType.DMA((n,))]

    def _copies(self, ins, outs, sems):
        ici_s, ici_r, core_s, core_r, loc = sems
        n = len(ins)
        same, other = outs[:n], outs[n:]
        x, y, c = _mesh_pos()
        me = 2 * x + y
        chips = _other_chips(x, y)
        sib = (x, y, 1 - c)
        local, ici, fwd = [], [], []
        for a in range(n):
            local.append(pltpu.make_async_copy(ins[a].at[me], same[a].at[me], loc.at[a]))
            fwd.append((pltpu.make_async_remote_copy(src_ref=ins[a].at[me], dst_ref=other[a].at[me], send_sem=core_s.at[a, 3],
                                                     recv_sem=core_r.at[a, 3], device_id=sib, device_id_type=MESH), None))
            for p, (px, py) in enumerate(chips):
                cp = 2 * px + py
                ici.append((pltpu.make_async_remote_copy(src_ref=ins[a].at[cp], dst_ref=same[a].at[me], send_sem=ici_s.at[a, p],
                                                         recv_sem=ici_r.at[a, p], device_id=(px, py, c), device_id_type=MESH),
                            pltpu.make_async_remote_copy(src_ref=ins[a].at[cp], dst_ref=same[a].at[cp], send_sem=ici_s.at[a, p],
                                                         recv_sem=ici_r.at[a, p], device_id=(px, py, c), device_id_type=MESH)))
                fwd.append((pltpu.make_async_remote_copy(src_ref=same[a].at[cp], dst_ref=other[a].at[cp], send_sem=core_s.at[a, p],
                                                         recv_sem=core_r.at[a, p], device_id=sib, device_id_type=MESH), p))
        return local, ici, fwd

    def start(self, ins, outs, sems):
        local, ici, fwd = self._copies(ins, outs, sems)
        for cp in local:
            cp.start()
        for send, _ in ici:
            send.start()
        for hand_on, p in fwd:
            if p is None:
                hand_on.start()

    def finish(self, ins, outs, sems):
        local, ici, fwd = self._copies(ins, outs, sems)
        chip_fwd = [f for f in fwd if f[1] is not None]
        for (_, arrive), (hand_on, _) in zip(ici, chip_fwd):
            arrive.wait_recv()
            hand_on.start()
        for hand_on, _ in fwd:
            hand_on.wait()
        for send, _ in ici:
            send.wait_send()
        for cp in local:
            cp.wait()


class _DeviceGatherJob:
    def __init__(self, buf):
        self.inputs = [buf]
        self.out_shape = [jax.ShapeDtypeStruct((8,) + buf.shape, buf.dtype)]
        self.scratch = [pltpu.SemaphoreType.DMA((7,)), pltpu.SemaphoreType.DMA((7,)), pltpu.SemaphoreType.DMA]

    def _copies(self, ins, outs, sems):
        send_s, recv_s, loc = sems
        x, y, c = _mesh_pos()
        me = 4 * x + 2 * y + c
        local = pltpu.make_async_copy(ins[0], outs[0].at[me], loc)
        remote = []
        for rel in range(1, 8):
            fx, fy, fc = (rel >> 2) & 1, (rel >> 1) & 1, rel & 1
            peer = (1 - x if fx else x, 1 - y if fy else y, 1 - c if fc else c)
            remote.append(pltpu.make_async_remote_copy(src_ref=ins[0], dst_ref=outs[0].at[me], send_sem=send_s.at[rel - 1],
                                                       recv_sem=recv_s.at[rel - 1], device_id=peer, device_id_type=MESH))
        return local, remote

    def start(self, ins, outs, sems):
        local, remote = self._copies(ins, outs, sems)
        local.start()
        for rc in remote:
            rc.start()

    def finish(self, ins, outs, sems):
        local, remote = self._copies(ins, outs, sems)
        for rc in remote:
            rc.wait()
        local.wait()


class _JobGroup:
    def __init__(self, jobs):
        self.jobs = list(jobs)
        self.inputs = [a for j in self.jobs for a in j.inputs]
        self.out_shape = [s for j in self.jobs for s in j.out_shape]
        self.scratch = [s for j in self.jobs for s in j.scratch]
        self.n_in = [len(j.inputs) for j in self.jobs]
        self.n_out = [len(j.out_shape) for j in self.jobs]
        self.n_sem = [len(j.scratch) for j in self.jobs]

    def _parts(self, k, ins, outs, sems):
        return (ins[sum(self.n_in[:k]):sum(self.n_in[:k + 1])], outs[sum(self.n_out[:k]):sum(self.n_out[:k + 1])],
                sems[sum(self.n_sem[:k]):sum(self.n_sem[:k + 1])])

    def start(self, ins, outs, sems):
        for k, job in enumerate(self.jobs):
            job.start(*self._parts(k, ins, outs, sems))

    def finish(self, ins, outs, sems):
        for k, job in enumerate(self.jobs):
            job.finish(*self._parts(k, ins, outs, sems))

    def split(self, outs):
        return [list(outs[sum(self.n_out[:k]):sum(self.n_out[:k + 1])]) for k in range(len(self.jobs))]


def _run_jobs(jobs, name):
    group = _JobGroup(jobs)
    n_in, n_out = len(group.inputs), len(group.out_shape)

    def body(*refs):
        ins, outs, sems = refs[:n_in], refs[n_in:n_in + n_out], refs[n_in + n_out:]
        group.start(ins, outs, sems)
        group.finish(ins, outs, sems)

    res = pl.pallas_call(body, name=name, in_specs=[_ANY] * n_in, out_specs=[_ANY] * n_out, out_shape=list(group.out_shape),
                         scratch_shapes=list(group.scratch))(*group.inputs)
    return group.split(res)


def _sum_devices(parts, name):
    n_dev, rows, lanes = parts.shape

    def body(p_ref, o_ref):
        total = p_ref[0]
        for k in range(1, n_dev):
            total = total + p_ref[k]
        o_ref[...] = total

    return pl.pallas_call(body, name=name, out_shape=jax.ShapeDtypeStruct((rows, lanes), F32),
                          compiler_params=pltpu.CompilerParams(vmem_limit_bytes=V7X_VMEM_LIMIT_BYTES))(parts)


def _row_tile(rows, limit=512):
    best = None
    for t in range(16, min(rows, limit) + 1, 16):
        if rows % t == 0:
            best = t
    return best if best is not None else rows


def _adamw_math(w, g, m, v):
    m = ADAM_B1 * m + (1.0 - ADAM_B1) * g
    v = ADAM_B2 * v + (1.0 - ADAM_B2) * (g * g)
    m_hat = m / (1.0 - ADAM_B1 ** ADAM_STEP)
    v_hat = v / (1.0 - ADAM_B2 ** ADAM_STEP)
    delta = -ADAM_LR * (m_hat / (jnp.sqrt(v_hat) + ADAM_EPS) + ADAM_WD * w)
    return delta, m, v


def _adamw_layer(w, m, v, same_plane, other_plane, layer, prev, name):
    nl, r, ccol = w.shape
    ns = same_plane.shape[0]
    tr = _row_tile(r, 256)

    def body(w_ref, m_ref, v_ref, a_ref, b_ref, *rest):
        go_ref, d_ref, mo_ref, vo_ref = rest[-4:]
        sa = a_ref[0].astype(F32)
        sb = b_ref[0].astype(F32)
        for k in range(1, ns):
            sa = sa + a_ref[k].astype(F32)
            sb = sb + b_ref[k].astype(F32)
        g = sa + sb
        go_ref[...] = g
        d_ref[...], mo_ref[...], vo_ref[...] = _adamw_math(w_ref[...], g, m_ref[...], v_ref[...])

    blk = pl.BlockSpec((None, tr, ccol), lambda i: (layer, i, 0))
    plane = pl.BlockSpec((ns, tr, ccol), lambda i: (0, i, 0))
    out = jax.ShapeDtypeStruct((nl, r, ccol), F32)
    n_prev = 4 if prev is not None else 0
    return pl.pallas_call(
        body, name=name, grid=(r // tr,),
        in_specs=[blk, blk, blk, plane, plane] + [_ANY] * n_prev,
        out_specs=[blk, blk, blk, blk], out_shape=[out, out, out, out],
        input_output_aliases={5 + k: k for k in range(n_prev)},
        compiler_params=_cparams(("parallel",)))(w, m, v, same_plane, other_plane, *(prev or ()))


def _adamw_whole(w, g, m, v, name):
    def body(w_ref, g_ref, m_ref, v_ref, d_ref, mo_ref, vo_ref):
        d_ref[...], mo_ref[...], vo_ref[...] = _adamw_math(w_ref[...], g_ref[...], m_ref[...], v_ref[...])

    out = jax.ShapeDtypeStruct(w.shape, F32)
    return pl.pallas_call(body, name=name, out_shape=[out, out, out],
                          compiler_params=pltpu.CompilerParams(vmem_limit_bytes=V7X_VMEM_LIMIT_BYTES))(w, g, m, v)


_W_NAMES = ['meta_tokens', 'ffn1_pre_g', 'ffn1_w_gate', 'ffn1_w_up', 'ffn1_w_down', 'ffn1_post_g', 'mix_pre_g', 'w_in',
            'lru_conv_w', 'lru_conv_b', 'lru_w_a', 'lru_b_a', 'lru_w_x', 'lru_b_x', 'lru_lambda', 'mla_q_norm_g', 'mla_w_uq',
            'mla_kv_norm_g', 'mla_w_ukv', 'lru_out_g', 'mla_out_g', 'w_out', 'mix_post_g', 'ffn2_pre_g', 'ffn2_w_gate',
            'ffn2_w_up', 'ffn2_w_down', 'ffn2_post_g']
_REPLICATED = ['ffn1_pre_g', 'ffn1_post_g', 'mix_pre_g', 'lru_conv_b', 'lru_w_a', 'lru_b_a', 'lru_w_x', 'lru_b_x', 'lru_lambda',
               'mla_q_norm_g', 'mla_kv_norm_g', 'lru_out_g', 'mla_out_g', 'mix_post_g', 'ffn2_pre_g', 'ffn2_post_g']
_LANES = 128


def _pack_rows(arrays):
    flat = jnp.concatenate([a.reshape(-1) for a in arrays])
    total = _round_up(flat.shape[0], 8 * _LANES)
    return jnp.pad(flat, (0, total - flat.shape[0])).reshape(-1, _LANES)


def _unpack_rows(buf, shapes):
    flat = buf.reshape(-1)
    out, off = [], 0
    for shp in shapes:
        n = math.prod(shp)
        out.append(flat[off:off + n].reshape(shp))
        off += n
    return out


def _block_diag(w):
    nl, nh, n, _ = w.shape
    eye = jnp.eye(nh, dtype=w.dtype)
    return (w[:, :, :, None, :] * eye[None, :, None, :, None]).reshape(nl, nh * n, nh * n)


def _diag_blocks(bd, nh):
    n = bd.shape[0] // nh
    b4 = bd.reshape(nh, n, nh, n)
    return jnp.stack([b4[i, :, i, :] for i in range(nh)])


def kernel(x, meta_tokens, ffn1_pre_g, ffn1_w_gate, ffn1_w_up, ffn1_w_down, ffn1_post_g, mix_pre_g, w_in, lru_conv_w, lru_conv_b, lru_w_a, lru_b_a, lru_w_x, lru_b_x, lru_lambda, mla_q_norm_g, mla_w_uq, mla_kv_norm_g, mla_w_ukv, lru_out_g, mla_out_g, w_out, mix_post_g, ffn2_pre_g, ffn2_w_gate, ffn2_w_up, ffn2_w_down, ffn2_post_g, loss_target, m_meta_tokens, m_ffn1_pre_g, m_ffn1_w_gate, m_ffn1_w_up, m_ffn1_w_down, m_ffn1_post_g, m_mix_pre_g, m_w_in, m_lru_conv_w, m_lru_conv_b, m_lru_w_a, m_lru_b_a, m_lru_w_x, m_lru_b_x, m_lru_lambda, m_mla_q_norm_g, m_mla_w_uq, m_mla_kv_norm_g, m_mla_w_ukv, m_lru_out_g, m_mla_out_g, m_w_out, m_mix_post_g, m_ffn2_pre_g, m_ffn2_w_gate, m_ffn2_w_up, m_ffn2_w_down, m_ffn2_post_g, v_meta_tokens, v_ffn1_pre_g, v_ffn1_w_gate, v_ffn1_w_up, v_ffn1_w_down, v_ffn1_post_g, v_mix_pre_g, v_w_in, v_lru_conv_w, v_lru_conv_b, v_lru_w_a, v_lru_b_a, v_lru_w_x, v_lru_b_x, v_lru_lambda, v_mla_q_norm_g, v_mla_w_uq, v_mla_kv_norm_g, v_mla_w_ukv, v_lru_out_g, v_mla_out_g, v_w_out, v_mix_post_g, v_ffn2_pre_g, v_ffn2_w_gate, v_ffn2_w_up, v_ffn2_w_down, v_ffn2_post_g):
    a = dict(locals())
    x2 = x[0]
    s_len, d = x2.shape
    nl = ffn1_pre_g.shape[0]
    lw = lru_conv_b.shape[-1]
    ql, kvl = mla_q_norm_g.shape[-1], mla_kv_norm_g.shape[-1]
    nh = MLA_HEADS
    vd = (d - lw) // nh
    t_real = N_META + s_len
    tp = _round_up(t_real, 384)
    tm = tp // 6
    tmb = tm // 2
    tm_ffn_fwd = tp // 4
    tm_ffn_bwd = tp // 8
    tq = 384 if tp >= 1536 else 128
    xi, yi, _ = _mesh_pos()
    chip = 2 * xi + yi

    def tr(w):
        return jnp.swapaxes(w, 1, 2)

    ffn_loc = jnp.stack([tr(ffn1_w_gate), tr(ffn1_w_up), ffn1_w_down, tr(ffn2_w_gate), tr(ffn2_w_up), ffn2_w_down],
                        axis=1).astype(BF16)
    loc = [ffn_loc, tr(w_in).astype(BF16), tr(mla_w_uq).astype(BF16), mla_w_ukv.astype(BF16), w_out.astype(BF16)]
    loc_shapes, loc_dtypes = [t.shape[1:] for t in loc], [t.dtype for t in loc]

    def layer_gather(l):
        return _GatherJob(loc, [lambda r, l=l: r.at[l]] * len(loc), loc_shapes, loc_dtypes, split_last=(1,))

    half_ffn = (ffn_loc.shape[1] // 2,) + ffn_loc.shape[2:]
    first = _GatherJob([ffn_loc, lru_conv_w, meta_tokens], [lambda r: r.at[0, pl.ds(0, half_ffn[0])], lambda r: r, lambda r: r],
                       [half_ffn, lru_conv_w.shape, meta_tokens.shape], [BF16, F32, F32], split_last=(0,))
    (w_ffn1_l0, g_conv, g_meta), = _run_jobs([first], "gather_first")
    rest_l0 = _GatherJob(loc, [lambda r: r.at[0, pl.ds(half_ffn[0], half_ffn[0])]] + [lambda r: r.at[0]] * (len(loc) - 1),
                         [half_ffn] + loc_shapes[1:], loc_dtypes, split_last=(0, 1))
    o0, o1, o2, o3 = lw, 2 * lw, 2 * lw + ql, 2 * lw + ql + kvl

    def assemble(gw, ffn1=None):
        w_ffn, g_win, g_wuq, g_wukv, g_wout = gw
        w_in_t = g_win.reshape(-1, d)
        w_kr = jnp.pad(w_in_t[o3:], ((QK_NOPE, 0), (0, 0)))
        w_ukv3 = jnp.transpose(g_wukv, (1, 0, 2)).reshape(kvl, nh, QK_NOPE + vd)
        w_out_full = g_wout.reshape(d, d)
        return dict(ffn1=ffn1 or (w_ffn, 0), ffn2=(w_ffn, 0 if ffn1 else 3),
                    parts=(w_in_t[:o0], w_in_t[o0:o1], w_in_t[o1:o2], w_in_t[o2:o3], w_kr),
                    q=g_wuq.reshape(nh, QK_DIM, ql),
                    k=jnp.transpose(jnp.pad(w_ukv3[..., :QK_NOPE], ((0, 0), (0, 0), (0, QK_ROPE))), (1, 0, 2)),
                    v=jnp.transpose(w_ukv3[..., QK_NOPE:], (1, 0, 2)),
                    lo=w_out_full[:lw], mo=w_out_full[lw:].reshape(nh, vd, d))

    conv_full = jnp.transpose(g_conv, (1, 2, 0, 3)).reshape(nl, CONV_WIDTH, lw)
    meta_full = jnp.transpose(g_meta, (1, 0, 2)).reshape(N_META, d)
    wa_bd = _block_diag(lru_w_a).astype(BF16)
    wx_bd = _block_diag(lru_w_x).astype(BF16)

    def gain3(g):
        return g.reshape(nl, 1, g.shape[-1])

    g3 = {n: gain3(a[n]) for n in ('ffn1_pre_g', 'ffn1_post_g', 'mix_pre_g', 'lru_conv_b', 'lru_b_a', 'lru_b_x', 'lru_lambda',
                                   'mla_q_norm_g', 'mla_kv_norm_g', 'lru_out_g', 'mix_post_g', 'ffn2_pre_g', 'ffn2_post_g')}
    g_mo = mla_out_g.reshape(nl, nh, 1, vd)

    pos = jnp.arange(tp, dtype=F32)
    inv_freq = 1.0 / (ROPE_THETA ** (jnp.arange(0, QK_ROPE, 2, dtype=F32) / QK_ROPE))
    ang = pos[:, None] * inv_freq[None, :]
    cos_t = jnp.concatenate([jnp.ones((tp, QK_NOPE), F32), jnp.cos(ang), jnp.cos(ang)], axis=1)
    sin_t = jnp.concatenate([jnp.zeros((tp, QK_NOPE), F32), jnp.sin(ang), jnp.sin(ang)], axis=1)
    half = QK_ROPE // 2
    rot = np.zeros((QK_DIM, QK_DIM), np.float32)
    for i in range(half):
        rot[QK_NOPE + half + i, QK_NOPE + i] = -1.0
        rot[QK_NOPE + i, QK_NOPE + half + i] = 1.0
    rot_b, rot_t = jnp.asarray(rot, BF16), jnp.asarray(rot.T, BF16)

    h = jnp.concatenate([meta_full, x2, jnp.zeros((tp - t_real, d), F32)], axis=0)
    target = jnp.pad(loss_target[0], ((N_META, tp - t_real), (0, 0)))
    saved, weights = [], []
    for l in range(nl):
        if l == 0:
            h1, gate1, up1, f1, gathered = _ffn_fwd(h, g3['ffn1_pre_g'], g3['ffn1_post_g'], w_ffn1_l0, l, 0, 1, 2, tm_ffn_fwd,
                                                    f"ffn1_fwd_{l}", rest_l0)
            wl = assemble(gathered, ffn1=(w_ffn1_l0, 0))
        else:
            wl = assemble(gathered)
            wf, k0 = wl['ffn1']
            h1, gate1, up1, f1, _ = _ffn_fwd(h, g3['ffn1_pre_g'], g3['ffn1_post_g'], wf, l, k0, k0 + 1, k0 + 2, tm_ffn_fwd,
                                             f"ffn1_fwd_{l}")
        weights.append(wl)
        xr, gr, cq, ckv, q, k, v = _mix_in_fwd(h1, g3['mix_pre_g'], wl['parts'], g3['mla_q_norm_g'], g3['mla_kv_norm_g'],
                                               wl['q'], wl['k'], wl['v'], cos_t, sin_t, rot_b, l, tm, f"mix_in_fwd_{l}")
        y_lru, xc, hs = _lru_fwd(xr, gr, conv_full, g3['lru_conv_b'], wa_bd, wx_bd, g3['lru_b_a'], g3['lru_b_x'],
                                 g3['lru_lambda'], l, tm, f"lru_fwd_{l}")
        o, lse, gathered = _attn_fwd(q, k, v, tq, f"attn_fwd_{l}", layer_gather(l + 1) if l + 1 < nl else None)
        h2, y = _mix_out_fwd(h1, y_lru, o, g3['lru_out_g'], g_mo, wl['lo'], wl['mo'], g3['mix_post_g'], l, tm, f"mix_out_fwd_{l}")
        wf, k0 = wl['ffn2']
        h3, gate2, up2, f2, _ = _ffn_fwd(h2, g3['ffn2_pre_g'], g3['ffn2_post_g'], wf, l, k0, k0 + 1, k0 + 2, tm_ffn_fwd,
                                         f"ffn2_fwd_{l}")
        saved.append((h, gate1, up1, f1, h1, xr, gr, cq, ckv, q, k, v, y_lru, xc, hs, o, lse, y, h2, gate2, up2, f2))
        h = h3

    dh, loss_parts = _loss_head(h, target, s_len, tm, "loss_head")
    loss = lax.psum(jnp.sum(loss_parts[:, 0, 0]), ("x", "y", "c"))

    small_order = _REPLICATED + ['lru_conv_w']
    small = {n: [None] * nl for n in small_order}
    planes = {}
    late_names = ['w_in', 'mla_w_uq', 'mla_w_ukv', 'ffn1_w_gate', 'ffn1_w_up', 'ffn1_w_down']
    early_names = ['ffn2_w_gate', 'ffn2_w_up', 'ffn2_w_down', 'w_out']
    late = None

    def keep_planes(keys, outs):
        for i, key in enumerate(keys):
            planes[key] = (outs[i], outs[len(keys) + i])

    for l in reversed(range(nl)):
        (h0, gate1, up1, f1, h1, xr, gr, cq, ckv, q, k, v, y_lru, xc, hs, o, lse, y, h2, gate2, up2, f2) = saved[l]
        wl = weights[l]
        wf, k0 = wl['ffn2']
        dh, dgate, dup, act, df, u, dgpre, dgpost, _ = _ffn_bwd(dh, h2, f2, gate2, up2, g3['ffn2_pre_g'], g3['ffn2_post_g'],
                                                               wf, l, k0, k0 + 1, k0 + 2, tm_ffn_bwd, f"ffn2_bwd_{l}")
        small['ffn2_pre_g'][l], small['ffn2_post_g'][l] = dgpre, dgpost
        d_wg2 = _mm_tn(dgate, u, tm, BF16, f"dw_gate2_{l}")
        d_wu2 = _mm_tn(dup, u, tm, BF16, f"dw_up2_{l}")
        dn2 = _mm_tn(act, df, tm, BF16, f"dw_down2_{l}")

        d_ylru, d_o, dy_b, n_lo, n_mo, dgpost, dglo, dgmo = _mix_out_bwd(dh, y, y_lru, o, g3['lru_out_g'], g_mo, wl['lo'], wl['mo'],
                                                                       g3['mix_post_g'], l, tm, f"mix_out_bwd_{l}")
        small['mix_post_g'][l], small['lru_out_g'][l], small['mla_out_g'][l] = dgpost, dglo, dgmo
        d_wlo = _mm_tn(n_lo, dy_b, tm, F32, f"dw_out_lru_{l}")
        d_wmo = _mm_tn_heads(n_mo, dy_b, tm, f"dw_out_mla_{l}")
        d_wout = jnp.concatenate([d_wlo, d_wmo.reshape(nh * vd, d)], axis=0).reshape(N_CHIPS, d // N_CHIPS, d).astype(BF16)

        keys = [(n, l) for n in early_names] + ([(n, l + 1) for n in late_names] if late else [])
        jobs = [_ReduceJob([d_wg2, d_wu2, dn2, d_wout] + (late or []))]
        if l == 0 and nl > 1:
            deep = _pack_rows([jnp.stack(small[n][1:]) for n in small_order])
            jobs.append(_DeviceGatherJob(deep))
        group = _JobGroup(jobs)
        dq, dk, dv, outs = _attn_bwd(q, k, v, o, d_o, lse, tq, f"attn_bwd_{l}", group)
        outs = group.split(outs)
        keep_planes(keys, outs[0])
        if len(jobs) > 1:
            all_deep = outs[1][0]
        d_xr, d_gr, dcw, dcb, dwa, dwx, dba, dbx, dlam = _lru_bwd(d_ylru, xr, gr, xc, hs, conv_full, wa_bd, wx_bd, g3['lru_b_a'],
                                                                 g3['lru_b_x'], g3['lru_lambda'], l, tm, f"lru_bwd_{l}")
        small['lru_conv_w'][l], small['lru_conv_b'][l] = dcw, dcb
        small['lru_w_a'][l], small['lru_w_x'][l] = _diag_blocks(dwa, LRU_HEADS), _diag_blocks(dwx, LRU_HEADS)
        small['lru_b_a'][l], small['lru_b_x'][l], small['lru_lambda'][l] = dba, dbx, dlam

        dh, u, qn, kvn, dq_pre, d_cq, d_ckv, d_kr, dgpre, dgq, dgkv = _mix_in_bwd(
            dh, h1, cq, ckv, d_xr, d_gr, dq, dk, dv, g3['mix_pre_g'], wl['parts'], g3['mla_q_norm_g'], g3['mla_kv_norm_g'],
            wl['q'], wl['k'], wl['v'], cos_t, sin_t, rot_t, l, tmb, f"mix_in_bwd_{l}")
        small['mix_pre_g'][l], small['mla_q_norm_g'][l], small['mla_kv_norm_g'][l] = dgpre, dgq, dgkv
        d_win = jnp.concatenate([_mm_tn(d_xr, u, tm, F32, f"dw_in_xr_{l}"), _mm_tn(d_gr, u, tm, F32, f"dw_in_gr_{l}"),
                                 _mm_tn(d_cq, u, tm, F32, f"dw_in_cq_{l}"), _mm_tn(d_ckv, u, tm, F32, f"dw_in_ckv_{l}"),
                                 _mm_tn(d_kr, u, tm, F32, f"dw_in_kr_{l}")[QK_NOPE:]], axis=0)
        d_win = d_win.reshape(N_CHIPS, -1, d).astype(BF16)
        d_wuq = _mm_tn_heads(dq_pre, qn, tm, f"dw_uq_{l}").reshape(N_CHIPS, -1, ql).astype(BF16)
        d_wk = _mm_tn_heads(dk, kvn, tm, f"dw_uk_{l}")[:, :QK_NOPE]
        d_wv = _mm_tn_heads(dv, kvn, tm, f"dw_uv_{l}")
        d_wukv = jnp.transpose(jnp.concatenate([d_wk, d_wv], axis=1).reshape(N_CHIPS, -1, kvl), (0, 2, 1)).astype(BF16)

        wf, k0 = wl['ffn1']
        last_layer = l == 0
        dh, dgate, dup, act, df, u, dgpre, dgpost, outs = _ffn_bwd(
            dh, h0, f1, gate1, up1, g3['ffn1_pre_g'], g3['ffn1_post_g'], wf, l, k0, k0 + 1, k0 + 2, tm_ffn_bwd, f"ffn1_bwd_{l}",
            _ReduceJob([d_win, d_wuq, d_wukv]) if last_layer else None)
        small['ffn1_pre_g'][l], small['ffn1_post_g'][l] = dgpre, dgpost
        d_wg1 = _mm_tn(dgate, u, tm, BF16, f"dw_gate1_{l}")
        if last_layer:
            keep_planes([(n, 0) for n in late_names[:3]], outs)
            d_wu1, outs = _mm_tn(dup, u, tm, BF16, f"dw_up1_{l}", _ReduceJob([d_wg1]))
            keep_planes([('ffn1_w_gate', 0)], outs)
            dn1, outs = _mm_tn(act, df, tm, BF16, f"dw_down1_{l}", _ReduceJob([d_wu1]))
            keep_planes([('ffn1_w_up', 0)], outs)
            late = [dn1]
        else:
            d_wu1 = _mm_tn(dup, u, tm, BF16, f"dw_up1_{l}")
            dn1 = _mm_tn(act, df, tm, BF16, f"dw_down1_{l}")
            late = [d_win, d_wuq, d_wukv, d_wg1, d_wu1, dn1]

    grad_x = dh[N_META:t_real][None]

    shallow = _pack_rows([small[n][0] for n in small_order] + [dh[:N_META]])
    late_planes, (all_shallow,) = _run_jobs([_ReduceJob(late), _DeviceGatherJob(shallow)], "reduce_last")
    keep_planes([('ffn1_w_down', 0)], late_planes)

    res = {}
    stored_transposed = ('ffn1_w_gate', 'ffn1_w_up', 'ffn2_w_gate', 'ffn2_w_up', 'w_in', 'mla_w_uq')
    for name in early_names + late_names:
        view = tr if name in stored_transposed else (lambda t: t)
        w_v, m_v, v_v = view(a[name]), view(a['m_' + name]), view(a['v_' + name])
        prev = None
        for l in reversed(range(nl)):
            same_plane, other_plane = planes[(name, l)]
            prev = _adamw_layer(w_v, m_v, v_v, same_plane, other_plane, l, prev, f"adamw_{name}_{l}")
        res[name] = [view(t) for t in prev]

    shapes0 = [small[n][0].shape for n in small_order]
    pieces = _unpack_rows(_sum_devices(all_shallow, "sum_small_last"), shapes0 + [(N_META, d)])
    per_layer = [p[None] for p in pieces[:-1]]
    if nl > 1:
        deeper = _unpack_rows(_sum_devices(all_deep, "sum_small_deep"), [(nl - 1,) + s for s in shapes0])
        per_layer = [jnp.concatenate([p0, p1], axis=0) for p0, p1 in zip(per_layer, deeper)]
    g_small = {n: p.reshape(a[n].shape if n != 'lru_conv_w' else conv_full.shape) for n, p in zip(small_order, per_layer)}
    g_conv_loc = lax.dynamic_slice_in_dim(g_small.pop('lru_conv_w'), chip * lru_conv_w.shape[-1], lru_conv_w.shape[-1], axis=2)
    g_meta_loc = lax.dynamic_slice_in_dim(pieces[-1], chip * meta_tokens.shape[-1], meta_tokens.shape[-1], axis=1)

    shapes = [a[n].shape for n in _REPLICATED]
    dl, mo, vo = _adamw_whole(_pack_rows([a[n] for n in _REPLICATED]), _pack_rows([g_small[n] for n in _REPLICATED]),
                              _pack_rows([a['m_' + n] for n in _REPLICATED]), _pack_rows([a['v_' + n] for n in _REPLICATED]),
                              "adamw_replicated")
    for n, dd, mm, vv in zip(_REPLICATED, _unpack_rows(dl, shapes), _unpack_rows(mo, shapes), _unpack_rows(vo, shapes)):
        res[n] = (g_small[n], dd, mm, vv)
    cshape = lru_conv_w.shape
    c2 = (cshape[0] * cshape[1], cshape[2])
    dd, mm, vv = _adamw_whole(lru_conv_w.reshape(c2), g_conv_loc.reshape(c2), m_lru_conv_w.reshape(c2), v_lru_conv_w.reshape(c2),
                              "adamw_conv_w")
    res['lru_conv_w'] = (g_conv_loc, dd.reshape(cshape), mm.reshape(cshape), vv.reshape(cshape))
    res['meta_tokens'] = (g_meta_loc,) + tuple(_adamw_whole(meta_tokens, g_meta_loc, m_meta_tokens, v_meta_tokens, "adamw_meta"))

    return (loss, grad_x, *[res[n][0] for n in _W_NAMES], *[res[n][1] for n in _W_NAMES],
            *[res[n][2] for n in _W_NAMES], *[res[n][3] for n in _W_NAMES])
```

```python
import functools
import math

import jax
import jax.numpy as jnp
import numpy as np
from jax import lax
from jax.experimental import pallas as pl
from jax.experimental.pallas import tpu as pltpu

F32 = jnp.float32
BF16 = jnp.bfloat16
MESH = pl.DeviceIdType.MESH

EPS = 1e-6
N_META = 16
LRU_HEADS = 8
MLA_HEADS = 8
QK_NOPE = 64
QK_ROPE = 32
QK_DIM = QK_NOPE + QK_ROPE
LRU_C = 8.0
ROPE_THETA = 10000.0
CONV_WIDTH = 4
N_CHIPS = 4

ADAM_LR = 0.001
ADAM_B1 = 0.9
ADAM_B2 = 0.999
ADAM_EPS = 1e-08
ADAM_WD = 0.01
ADAM_STEP = 10

V7X_VMEM_LIMIT_BYTES = 56 * 1024 * 1024
NEG_BIG = -1e30
Q_PRESCALE = QK_DIM ** -0.5 * math.log2(math.e)


def _cparams(sem=None):
    return pltpu.CompilerParams(dimension_semantics=sem, vmem_limit_bytes=V7X_VMEM_LIMIT_BYTES)


def _round_up(a, b):
    return -(-a // b) * b


def _whole(*shape):
    return pl.BlockSpec(shape, lambda *_: (0,) * len(shape))


_ANY = pl.BlockSpec(memory_space=pl.ANY)


def _dot(a, b):
    return jnp.dot(a.astype(BF16), b.astype(BF16), preferred_element_type=F32)


def _dot_nt(a, b):
    return lax.dot_general(a.astype(BF16), b.astype(BF16), (((1,), (1,)), ((), ())), preferred_element_type=F32)


def _dot_tn(a, b):
    return lax.dot_general(a.astype(BF16), b.astype(BF16), (((0,), (0,)), ((), ())), preferred_element_type=F32)


def _dot_split(x, p):
    hi = x.astype(BF16)
    lo = (x - hi.astype(F32)).astype(BF16)
    return jnp.dot(hi, p, preferred_element_type=F32) + jnp.dot(lo, p, preferred_element_type=F32)


def _rms(x, g):
    r = lax.rsqrt(jnp.mean(x * x, axis=-1, keepdims=True) + EPS)
    return x * r * g, r


def _rms_bwd(x, g, dy):
    r = lax.rsqrt(jnp.mean(x * x, axis=-1, keepdims=True) + EPS)
    xh = x * r
    dyg = dy * g
    dx = r * (dyg - xh * jnp.mean(dyg * xh, axis=-1, keepdims=True))
    dg = jnp.sum(dy * xh, axis=0, keepdims=True)
    return dx, dg


def _sigmoid(x):
    return 1.0 / (1.0 + jnp.exp(-x))


def _neg_expm1(x):
    series = -x * (1.0 + x * (0.5 + x * (1.0 / 6.0 + x * (1.0 / 24.0))))
    return jnp.where(jnp.abs(x) < 0.03, series, 1.0 - jnp.exp(x))


_GELU_K = math.sqrt(2.0 / math.pi)


def _gelu(x):
    t = jnp.tanh(_GELU_K * (x + 0.044715 * x * x * x))
    return 0.5 * x * (1.0 + t), t


def _gelu_grad(x, t):
    return 0.5 * (1.0 + t) + 0.5 * x * (1.0 - t * t) * _GELU_K * (1.0 + 3.0 * 0.044715 * x * x)


def _rope(x, cos_t, sin_t, rot):
    return x * cos_t + _dot_split(x, rot) * sin_t


def _ffn_fwd(h, g_pre, g_post, w_ffn, layer, kind_gate, kind_up, kind_dn, tm, name, job=None):
    tp, d = h.shape
    n_slot, fs = w_ffn.shape[0], w_ffn.shape[-2]
    n_tok = tp // tm
    n_ji = len(job.inputs) if job else 0
    n_jo = len(job.out_shape) if job else 0

    def body(h_ref, gpre_ref, gpost_ref, wg_ref, wu_ref, wd_ref, *rest):
        job_in, (hout_ref, gate_ref, up_ref, f_ref), rest = rest[:n_ji], rest[n_ji:n_ji + 4], rest[n_ji + 4:]
        job_out, (u_sc, acc_sc), sems = rest[:n_jo], rest[n_jo:n_jo + 2], rest[n_jo + 2:]
        j = pl.program_id(1)
        if job:
            pl.when((pl.program_id(0) == 0) & (j == 0))(lambda: job.start(job_in, job_out, sems))

        @pl.when(j == 0)
        def _():
            u, _ = _rms(h_ref[...], gpre_ref[...])
            u_sc[...] = u.astype(BF16)
            acc_sc[...] = jnp.zeros_like(acc_sc)

        u = u_sc[...]
        gate = _dot_nt(u, wg_ref[...])
        up = _dot_nt(u, wu_ref[...])
        act = gate * _sigmoid(gate) * up
        gate_ref[...] = gate.astype(BF16)
        up_ref[...] = up.astype(BF16)
        acc_sc[...] += jnp.dot(act.astype(BF16), wd_ref[...], preferred_element_type=F32)

        @pl.when(j == n_slot - 1)
        def _():
            f = acc_sc[...]
            f_ref[...] = f
            n, _ = _rms(f, gpost_ref[...])
            hout_ref[...] = h_ref[...] + 0.5 * n

        if job:
            pl.when((pl.program_id(0) == n_tok - 1) & (j == n_slot - 1))(lambda: job.finish(job_in, job_out, sems))

    tok = pl.BlockSpec((tm, d), lambda i, j: (i, 0))
    gain = pl.BlockSpec((None, 1, d), lambda i, j: (layer, 0, 0))
    slot_act = pl.BlockSpec((None, tm, fs), lambda i, j: (j, i, 0))
    outs = pl.pallas_call(
        body, name=name, grid=(n_tok, n_slot),
        in_specs=[tok, gain, gain] + [pl.BlockSpec((None, None, fs, d), lambda i, j, kind=kind: (j, kind, 0, 0))
                                      for kind in (kind_gate, kind_up, kind_dn)] + [_ANY] * n_ji,
        out_specs=[tok, slot_act, slot_act, tok] + [_ANY] * n_jo,
        out_shape=[jax.ShapeDtypeStruct((tp, d), F32), jax.ShapeDtypeStruct((n_slot, tp, fs), BF16),
                   jax.ShapeDtypeStruct((n_slot, tp, fs), BF16), jax.ShapeDtypeStruct((tp, d), F32)]
        + (list(job.out_shape) if job else []),
        scratch_shapes=[pltpu.VMEM((tm, d), BF16), pltpu.VMEM((tm, d), F32)] + (list(job.scratch) if job else []),
        compiler_params=_cparams(("arbitrary", "arbitrary") if job else ("parallel", "arbitrary")),
    )(h, g_pre, g_post, w_ffn, w_ffn, w_ffn, *(job.inputs if job else ()))
    return outs[0], outs[1], outs[2], outs[3], list(outs[4:])


def _ffn_bwd(dh_out, h, f, gate, up, g_pre, g_post, w_ffn, layer, kind_gate, kind_up, kind_dn, tm, name, job=None):
    tp, d = h.shape
    n_slot, fs = w_ffn.shape[0], w_ffn.shape[-2]
    n_tok = tp // tm
    n_ji = len(job.inputs) if job else 0
    n_jo = len(job.out_shape) if job else 0

    def body(dho_ref, h_ref, f_ref, gate_ref, up_ref, gpre_ref, gpost_ref, wg_ref, wu_ref, wd_ref, *rest):
        job_in, rest = rest[:n_ji], rest[n_ji:]
        (dh_ref, dgate_ref, dup_ref, act_ref, df_ref, u_ref, dgpre_ref, dgpost_ref), rest = rest[:8], rest[8:]
        job_out, du_sc, sems = rest[:n_jo], rest[n_jo], rest[n_jo + 1:]
        i, j = pl.program_id(0), pl.program_id(1)
        if job:
            pl.when((i == 0) & (j == 0))(lambda: job.start(job_in, job_out, sems))

        @pl.when((i == 0) & (j == 0))
        def _():
            dgpre_ref[...] = jnp.zeros_like(dgpre_ref)
            dgpost_ref[...] = jnp.zeros_like(dgpost_ref)

        @pl.when(j == 0)
        def _():
            df, dg = _rms_bwd(f_ref[...], gpost_ref[...], 0.5 * dho_ref[...])
            df_ref[...] = df.astype(BF16)
            dgpost_ref[...] += dg
            u, _ = _rms(h_ref[...], gpre_ref[...])
            u_ref[...] = u.astype(BF16)
            du_sc[...] = jnp.zeros_like(du_sc)

        g = gate_ref[...].astype(F32)
        u_ = up_ref[...].astype(F32)
        sg = _sigmoid(g)
        silu = g * sg
        dact = _dot_nt(df_ref[...], wd_ref[...])
        dup = dact * silu
        dgate = dact * u_ * (sg * (1.0 + g * (1.0 - sg)))
        act_ref[...] = (silu * u_).astype(BF16)
        dup_b = dup.astype(BF16)
        dgate_b = dgate.astype(BF16)
        dup_ref[...] = dup_b
        dgate_ref[...] = dgate_b
        du_sc[...] += _dot(dgate_b, wg_ref[...]) + _dot(dup_b, wu_ref[...])

        @pl.when(j == n_slot - 1)
        def _():
            dx, dg = _rms_bwd(h_ref[...], gpre_ref[...], du_sc[...])
            dh_ref[...] = dho_ref[...] + dx
            dgpre_ref[...] += dg

        if job:
            pl.when((i == n_tok - 1) & (j == n_slot - 1))(lambda: job.finish(job_in, job_out, sems))

    tok = pl.BlockSpec((tm, d), lambda i, j: (i, 0))
    gain = pl.BlockSpec((None, 1, d), lambda i, j: (layer, 0, 0))
    acc = pl.BlockSpec((1, d), lambda i, j: (0, 0))
    slot_act = pl.BlockSpec((None, tm, fs), lambda i, j: (j, i, 0))
    act_shape = jax.ShapeDtypeStruct((n_slot, tp, fs), BF16)
    outs = pl.pallas_call(
        body, name=name, grid=(n_tok, n_slot),
        in_specs=[tok, tok, tok, slot_act, slot_act, gain, gain]
        + [pl.BlockSpec((None, None, fs, d), lambda i, j, kind=kind: (j, kind, 0, 0)) for kind in (kind_gate, kind_up, kind_dn)]
        + [_ANY] * n_ji,
        out_specs=[tok, slot_act, slot_act, slot_act, tok, tok, acc, acc] + [_ANY] * n_jo,
        out_shape=[jax.ShapeDtypeStruct((tp, d), F32), act_shape, act_shape, act_shape,
                   jax.ShapeDtypeStruct((tp, d), BF16), jax.ShapeDtypeStruct((tp, d), BF16),
                   jax.ShapeDtypeStruct((1, d), F32), jax.ShapeDtypeStruct((1, d), F32)] + (list(job.out_shape) if job else []),
        scratch_shapes=[pltpu.VMEM((tm, d), F32)] + (list(job.scratch) if job else []),
        compiler_params=_cparams(("arbitrary", "arbitrary")),
    )(dh_out, h, f, gate, up, g_pre, g_post, w_ffn, w_ffn, w_ffn, *(job.inputs if job else ()))
    return tuple(outs[:8]) + (list(outs[8:]),)


def _mm_tn(a, b, tk, out_dtype, name, job=None):
    ga = a.shape[0] if a.ndim == 3 else None
    gb = b.shape[0] if b.ndim == 3 else None
    groups = ga or gb or 1
    t, m = a.shape[-2:]
    n = b.shape[-1]
    nk = t // tk
    n_ji = len(job.inputs) if job else 0
    n_jo = len(job.out_shape) if job else 0

    def body(a_ref, b_ref, *rest):
        job_in, o_ref, rest = rest[:n_ji], rest[n_ji], rest[n_ji + 1:]
        job_out, acc_sc, sems = rest[:n_jo], rest[n_jo], rest[n_jo + 1:]
        k = pl.program_id(1)
        if job:
            pl.when((pl.program_id(0) == 0) & (k == 0))(lambda: job.start(job_in, job_out, sems))

        @pl.when(k == 0)
        def _():
            acc_sc[...] = jnp.zeros_like(acc_sc)

        acc_sc[...] += _dot_tn(a_ref[...], b_ref[...])

        @pl.when(k == nk - 1)
        def _():
            o_ref[...] = acc_sc[...].astype(out_dtype)

        if job:
            pl.when((pl.program_id(0) == groups - 1) & (k == nk - 1))(lambda: job.finish(job_in, job_out, sems))

    a_spec = (pl.BlockSpec((None, tk, m), lambda g, k: (g, k, 0)) if ga else pl.BlockSpec((tk, m), lambda g, k: (k, 0)))
    b_spec = (pl.BlockSpec((None, tk, n), lambda g, k: (g, k, 0)) if gb else pl.BlockSpec((tk, n), lambda g, k: (k, 0)))
    outs = pl.pallas_call(
        body, name=name, grid=(groups, nk),
        in_specs=[a_spec, b_spec] + [_ANY] * n_ji,
        out_specs=[pl.BlockSpec((None, m, n), lambda g, k: (g, 0, 0))] + [_ANY] * n_jo,
        out_shape=[jax.ShapeDtypeStruct((groups, m, n), out_dtype)] + (list(job.out_shape) if job else []),
        scratch_shapes=[pltpu.VMEM((m, n), F32)] + (list(job.scratch) if job else []),
        compiler_params=_cparams(("arbitrary", "arbitrary") if job else ("parallel", "arbitrary")),
    )(a, b, *(job.inputs if job else ()))
    out = outs[0] if (ga or gb) else outs[0][0]
    return (out, list(outs[1:])) if job else out


def _mm_tn_heads(a, b, tk, name):
    groups, t, m = a.shape
    n = b.shape[-1]
    nk = t // tk

    def body(a_ref, b_ref, o_ref):
        k = pl.program_id(0)

        @pl.when(k == 0)
        def _():
            o_ref[...] = jnp.zeros_like(o_ref)

        bb = b_ref[...]
        for g in range(groups):
            o_ref[g] += _dot_tn(a_ref[g], bb)

    return pl.pallas_call(
        body, name=name, grid=(nk,),
        in_specs=[pl.BlockSpec((groups, tk, m), lambda k: (0, k, 0)), pl.BlockSpec((tk, n), lambda k: (k, 0))],
        out_specs=pl.BlockSpec((groups, m, n), lambda k: (0, 0, 0)),
        out_shape=jax.ShapeDtypeStruct((groups, m, n), F32),
        compiler_params=_cparams(("arbitrary",)),
    )(a, b)


def _mix_in_fwd(h, g_pre, w_parts, g_q, g_kv, w_q, w_k, w_v, cos_t, sin_t, rot, layer, tm, name):
    tp, d = h.shape
    w_xr, w_gr, w_cq, w_ckv, w_kr = w_parts
    lw, ql, kvl = w_xr.shape[0], w_cq.shape[0], w_ckv.shape[0]
    nh, vd = w_v.shape[0], w_v.shape[-1]

    def body(h_ref, gpre_ref, wxr_ref, wgr_ref, wcq_ref, wckv_ref, wkr_ref, gq_ref, gkv_ref, wq_ref, wk_ref, wv_ref,
             cos_ref, sin_ref, rot_ref, xr_ref, gr_ref, cq_ref, ckv_ref, q_ref, k_ref, v_ref):
        u, _ = _rms(h_ref[...], gpre_ref[...])
        u = u.astype(BF16)
        xr_ref[...] = _dot_nt(u, wxr_ref[...])
        gr_ref[...] = _dot_nt(u, wgr_ref[...])
        cq = _dot_nt(u, wcq_ref[...])
        ckv = _dot_nt(u, wckv_ref[...])
        kr = _dot_nt(u, wkr_ref[...])
        cq_ref[...] = cq
        ckv_ref[...] = ckv
        cos_b, sin_b, rot_b = cos_ref[...], sin_ref[...], rot_ref[...]
        qn = _rms(cq, gq_ref[...])[0].astype(BF16)
        kvn = _rms(ckv, gkv_ref[...])[0].astype(BF16)
        k_rope = _rope(kr, cos_b, sin_b, rot_b)
        for hd in range(nh):
            q_pre = _dot_nt(qn, wq_ref[hd])
            q_ref[hd] = (_rope(q_pre, cos_b, sin_b, rot_b) * Q_PRESCALE).astype(BF16)
            k_ref[hd] = (jnp.dot(kvn, wk_ref[hd], preferred_element_type=F32) + k_rope).astype(BF16)
            v_ref[hd] = jnp.dot(kvn, wv_ref[hd], preferred_element_type=F32).astype(BF16)

    def tok(n):
        return pl.BlockSpec((tm, n), lambda i: (i, 0))

    def lay(*shape):
        return pl.BlockSpec((None,) + shape, lambda i: (layer,) + (0,) * len(shape))

    def heads(n):
        return pl.BlockSpec((nh, tm, n), lambda i: (0, i, 0))

    return pl.pallas_call(
        body, name=name, grid=(tp // tm,),
        in_specs=[tok(d), lay(1, d), _whole(lw, d), _whole(lw, d), _whole(ql, d), _whole(kvl, d), _whole(QK_DIM, d),
                  lay(1, ql), lay(1, kvl),
                  _whole(nh, QK_DIM, ql), _whole(nh, kvl, QK_DIM), _whole(nh, kvl, vd), tok(QK_DIM), tok(QK_DIM),
                  pl.BlockSpec((QK_DIM, QK_DIM), lambda i: (0, 0))],
        out_specs=[tok(lw), tok(lw), tok(ql), tok(kvl), heads(QK_DIM), heads(QK_DIM), heads(vd)],
        out_shape=[jax.ShapeDtypeStruct((tp, lw), F32), jax.ShapeDtypeStruct((tp, lw), F32),
                   jax.ShapeDtypeStruct((tp, ql), F32), jax.ShapeDtypeStruct((tp, kvl), F32),
                   jax.ShapeDtypeStruct((nh, tp, QK_DIM), BF16), jax.ShapeDtypeStruct((nh, tp, QK_DIM), BF16),
                   jax.ShapeDtypeStruct((nh, tp, vd), BF16)],
        compiler_params=_cparams(("parallel",)),
    )(h, g_pre, w_xr, w_gr, w_cq, w_ckv, w_kr, g_q, g_kv, w_q, w_k, w_v, cos_t, sin_t, rot)


def _mix_in_bwd(dh_res, h, cq, ckv, d_xr, d_gr, dq, dk, dv, g_pre, w_parts, g_q, g_kv, w_q, w_k, w_v,
                cos_t, sin_t, rot_t, layer, tm, name):
    tp, d = h.shape
    w_xr, w_gr, w_cq, w_ckv, w_kr = w_parts
    lw, ql, kvl = w_xr.shape[0], w_cq.shape[0], w_ckv.shape[0]
    nh, vd = w_v.shape[0], w_v.shape[-1]

    def body(dhr_ref, h_ref, cq_ref, ckv_ref, dxr_ref, dgr_ref, dq_ref, dk_ref, dv_ref, gpre_ref,
             wxr_ref, wgr_ref, wcq_ref, wckv_ref, wkr_ref, gq_ref, gkv_ref, wq_ref, wk_ref, wv_ref,
             cos_ref, sin_ref, rott_ref,
             dh_ref, u_ref, qn_ref, kvn_ref, dqpre_ref, dcq_ref, dckv_ref, dkr_ref, dgpre_ref, dgq_ref, dgkv_ref):
        i = pl.program_id(0)

        @pl.when(i == 0)
        def _():
            dgpre_ref[...] = jnp.zeros_like(dgpre_ref)
            dgq_ref[...] = jnp.zeros_like(dgq_ref)
            dgkv_ref[...] = jnp.zeros_like(dgkv_ref)

        cos_b, sin_b, rott_b = cos_ref[...], sin_ref[...], rott_ref[...]
        hh = h_ref[...]
        u, _ = _rms(hh, gpre_ref[...])
        u_ref[...] = u.astype(BF16)
        cq, ckv = cq_ref[...], ckv_ref[...]
        qn = _rms(cq, gq_ref[...])[0]
        kvn = _rms(ckv, gkv_ref[...])[0]
        qn_ref[...] = qn.astype(BF16)
        kvn_ref[...] = kvn.astype(BF16)
        d_qn = jnp.zeros((tm, ql), F32)
        d_kvn = jnp.zeros((tm, kvl), F32)
        d_krope = jnp.zeros((tm, QK_DIM), F32)
        for hd in range(nh):
            dq_h = dq_ref[hd]
            dq_pre = dq_h * cos_b + _dot_split(dq_h * sin_b, rott_b)
            dq_pre_b = dq_pre.astype(BF16)
            dqpre_ref[hd] = dq_pre_b
            d_qn += _dot(dq_pre_b, wq_ref[hd])
            dk_h = dk_ref[hd]
            d_krope += dk_h
            d_kvn += _dot_nt(dk_h, wk_ref[hd]) + _dot_nt(dv_ref[hd], wv_ref[hd])
        d_kr = d_krope * cos_b + _dot_split(d_krope * sin_b, rott_b)
        d_cq, dgq = _rms_bwd(cq, gq_ref[...], d_qn)
        d_ckv, dgkv = _rms_bwd(ckv, gkv_ref[...], d_kvn)
        dgq_ref[...] += dgq
        dgkv_ref[...] += dgkv
        d_cq_b, d_ckv_b, d_kr_b = d_cq.astype(BF16), d_ckv.astype(BF16), d_kr.astype(BF16)
        dcq_ref[...] = d_cq_b
        dckv_ref[...] = d_ckv_b
        dkr_ref[...] = d_kr_b
        du = (_dot(dxr_ref[...], wxr_ref[...]) + _dot(dgr_ref[...], wgr_ref[...]) + _dot(d_cq_b, wcq_ref[...])
              + _dot(d_ckv_b, wckv_ref[...]) + _dot(d_kr_b, wkr_ref[...]))
        dx, dg = _rms_bwd(hh, gpre_ref[...], du)
        dh_ref[...] = dhr_ref[...] + dx
        dgpre_ref[...] += dg

    def tok(n):
        return pl.BlockSpec((tm, n), lambda i: (i, 0))

    def lay(*shape):
        return pl.BlockSpec((None,) + shape, lambda i: (layer,) + (0,) * len(shape))

    def heads(n):
        return pl.BlockSpec((nh, tm, n), lambda i: (0, i, 0))

    def acc(n):
        return pl.BlockSpec((1, n), lambda i: (0, 0))

    return pl.pallas_call(
        body, name=name, grid=(tp // tm,),
        in_specs=[tok(d), tok(d), tok(ql), tok(kvl), tok(lw), tok(lw), heads(QK_DIM), heads(QK_DIM), heads(vd), lay(1, d),
                  _whole(lw, d), _whole(lw, d), _whole(ql, d), _whole(kvl, d), _whole(QK_DIM, d), lay(1, ql), lay(1, kvl),
                  _whole(nh, QK_DIM, ql), _whole(nh, kvl, QK_DIM), _whole(nh, kvl, vd), tok(QK_DIM), tok(QK_DIM),
                  pl.BlockSpec((QK_DIM, QK_DIM), lambda i: (0, 0))],
        out_specs=[tok(d), tok(d), tok(ql), tok(kvl), heads(QK_DIM), tok(ql), tok(kvl), tok(QK_DIM), acc(d), acc(ql), acc(kvl)],
        out_shape=[jax.ShapeDtypeStruct((tp, d), F32), jax.ShapeDtypeStruct((tp, d), BF16),
                   jax.ShapeDtypeStruct((tp, ql), BF16), jax.ShapeDtypeStruct((tp, kvl), BF16),
                   jax.ShapeDtypeStruct((nh, tp, QK_DIM), BF16), jax.ShapeDtypeStruct((tp, ql), BF16),
                   jax.ShapeDtypeStruct((tp, kvl), BF16), jax.ShapeDtypeStruct((tp, QK_DIM), BF16),
                   jax.ShapeDtypeStruct((1, d), F32), jax.ShapeDtypeStruct((1, ql), F32), jax.ShapeDtypeStruct((1, kvl), F32)],
        compiler_params=_cparams(("arbitrary",)),
    )(dh_res, h, cq, ckv, d_xr, d_gr, dq, dk, dv, g_pre, w_xr, w_gr, w_cq, w_ckv, w_kr, g_q, g_kv, w_q, w_k, w_v,
      cos_t, sin_t, rot_t)


def _lru_gates(xc, wa_ref, wx_ref, ba, bx, sp):
    xcb = xc.astype(BF16)
    r = _sigmoid(jnp.dot(xcb, wa_ref[...], preferred_element_type=F32) + ba)
    ig = _sigmoid(jnp.dot(xcb, wx_ref[...], preferred_element_type=F32) + bx)
    log_a = -LRU_C * r * sp
    a = jnp.exp(log_a)
    sq = jnp.sqrt(_neg_expm1(2.0 * log_a))
    return r, ig, a, sq


def _softplus(x):
    return jnp.maximum(x, 0.0) + jnp.log(1.0 + jnp.exp(-jnp.abs(x)))


def _lru_fwd(xr, gr, conv_w, conv_b, wa_bd, wx_bd, b_a, b_x, lam, layer, tc, name):
    tp, w = xr.shape
    pad = 8

    def body(xr_ref, gr_ref, cw_ref, cb_ref, wa_ref, wx_ref, ba_ref, bx_ref, lam_ref, y_ref, xc_ref, hs_ref,
             xe_sc, a_sc, b_sc, st_sc):
        c = pl.program_id(0)

        @pl.when(c == 0)
        def _():
            xe_sc[pl.ds(0, pad), :] = jnp.zeros((pad, w), F32)
            st_sc[...] = jnp.zeros_like(st_sc)

        xe_sc[pl.ds(pad, tc), :] = xr_ref[...]
        xc = cb_ref[...] + xe_sc[pl.ds(pad, tc), :] * cw_ref[pl.ds(CONV_WIDTH - 1, 1), :]
        for k in range(CONV_WIDTH - 1):
            xc = xc + xe_sc[pl.ds(pad - (CONV_WIDTH - 1) + k, tc), :] * cw_ref[pl.ds(k, 1), :]
        xe_sc[pl.ds(0, pad), :] = xe_sc[pl.ds(tc, pad), :]
        xc_ref[...] = xc
        sp = _softplus(-lam_ref[...])
        _, ig, a, sq = _lru_gates(xc, wa_ref, wx_ref, ba_ref[...], bx_ref[...], sp)
        a_sc[...] = a
        b_sc[...] = sq * (ig * xc)

        def step(t, hcur):
            hnew = a_sc[pl.ds(t, 1), :] * hcur + b_sc[pl.ds(t, 1), :]
            hs_ref[pl.ds(t, 1), :] = hnew
            return hnew

        st_sc[...] = lax.fori_loop(0, tc, step, st_sc[...], unroll=8)
        y_ref[...] = hs_ref[...] * _gelu(gr_ref[...])[0]

    tok = pl.BlockSpec((tc, w), lambda c: (c, 0))

    def lay(*shape):
        return pl.BlockSpec((None,) + shape, lambda c: (layer,) + (0,) * len(shape))

    out = jax.ShapeDtypeStruct((tp, w), F32)
    return pl.pallas_call(
        body, name=name, grid=(tp // tc,),
        in_specs=[tok, tok, lay(CONV_WIDTH, w), lay(1, w), lay(w, w), lay(w, w), lay(1, w), lay(1, w), lay(1, w)],
        out_specs=[tok, tok, tok], out_shape=[out, out, out],
        scratch_shapes=[pltpu.VMEM((tc + pad, w), F32), pltpu.VMEM((tc, w), F32), pltpu.VMEM((tc, w), F32),
                        pltpu.VMEM((1, w), F32)],
        compiler_params=_cparams(("arbitrary",)),
    )(xr, gr, conv_w, conv_b, wa_bd, wx_bd, b_a, b_x, lam)


def _lru_bwd(dy, xr, gr, xc, hs, conv_w, wa_bd, wx_bd, b_a, b_x, lam, layer, tc, name):
    tp, w = xr.shape
    pad = 8
    nc = tp // tc
    per = tc // pad

    def body(dy_ref, xr_ref, gr_ref, xc_ref, hs_ref, xrp_ref, hsp_ref, cw_ref, wa_ref, wx_ref, ba_ref, bx_ref, lam_ref,
             dxr_ref, dgr_ref, dcw_ref, dcb_ref, dwa_ref, dwx_ref, dba_ref, dbx_ref, dlam_ref,
             a_sc, dh_sc, dxc_sc, he_sc, xe_sc, carry_sc):
        s = pl.program_id(0)
        first_chunk = s == nc - 1

        @pl.when(s == 0)
        def _():
            for ref in (dcw_ref, dcb_ref, dwa_ref, dwx_ref, dba_ref, dbx_ref, dlam_ref):
                ref[...] = jnp.zeros_like(ref)
            carry_sc[...] = jnp.zeros_like(carry_sc)
            dxc_sc[pl.ds(tc, pad), :] = jnp.zeros((pad, w), F32)

        keep = jnp.where(first_chunk, 0.0, 1.0)
        he_sc[pl.ds(0, pad), :] = hsp_ref[...] * keep
        he_sc[pl.ds(pad, tc), :] = hs_ref[...]
        xe_sc[pl.ds(0, pad), :] = xrp_ref[...] * keep
        xe_sc[pl.ds(pad, tc), :] = xr_ref[...]

        lam_v = lam_ref[...]
        sp = _softplus(-lam_v)
        xc = xc_ref[...]
        r, ig, a, sq = _lru_gates(xc, wa_ref, wx_ref, ba_ref[...], bx_ref[...], sp)
        a_sc[...] = a
        grv = gr_ref[...]
        gl, th = _gelu(grv)
        dyv = dy_ref[...]
        dgr_ref[...] = dyv * hs_ref[...] * _gelu_grad(grv, th)
        dh_sc[...] = dyv * gl

        def step(n, g):
            t = tc - 1 - n
            dh = dh_sc[pl.ds(t, 1), :] + g
            dh_sc[pl.ds(t, 1), :] = dh
            return a_sc[pl.ds(t, 1), :] * dh

        carry_sc[...] = lax.fori_loop(0, tc, step, carry_sc[...], unroll=8)

        dh = dh_sc[...]
        d_a = dh * he_sc[pl.ds(pad - 1, tc), :]
        d_ixc = dh * sq
        d_sq = dh * (ig * xc)
        a2 = a * a
        d_la = d_a * a - d_sq * a2 / sq
        d_r = d_la * (-LRU_C * sp)
        d_sp = jnp.sum(d_la * (-LRU_C * r), axis=0, keepdims=True)
        dlam_ref[...] += d_sp * (-_sigmoid(-lam_v))
        d_pa = d_r * r * (1.0 - r)
        d_px = d_ixc * xc * ig * (1.0 - ig)
        dba_ref[...] += jnp.sum(d_pa, axis=0, keepdims=True)
        dbx_ref[...] += jnp.sum(d_px, axis=0, keepdims=True)
        d_pa_b, d_px_b = d_pa.astype(BF16), d_px.astype(BF16)
        xcb = xc.astype(BF16)
        dwa_ref[...] += _dot_tn(xcb, d_pa_b)
        dwx_ref[...] += _dot_tn(xcb, d_px_b)
        d_xc = d_ixc * ig + _dot_nt(d_pa_b, wa_ref[...]) + _dot_nt(d_px_b, wx_ref[...])
        dcb_ref[...] += jnp.sum(d_xc, axis=0, keepdims=True)
        dxc_sc[pl.ds(0, tc), :] = d_xc
        d_xr = jnp.zeros((tc, w), F32)
        for k in range(CONV_WIDTH):
            off = CONV_WIDTH - 1 - k
            d_xr = d_xr + dxc_sc[pl.ds(off, tc), :] * cw_ref[pl.ds(k, 1), :]
            dcw_ref[pl.ds(k, 1), :] += jnp.sum(d_xc * xe_sc[pl.ds(pad - off, tc), :], axis=0, keepdims=True)
        dxr_ref[...] = d_xr
        dxc_sc[pl.ds(tc, pad), :] = dxc_sc[pl.ds(0, pad), :]

    def rev(cidx):
        return nc - 1 - cidx

    tok = pl.BlockSpec((tc, w), lambda c: (rev(c), 0))
    prev = pl.BlockSpec((pad, w), lambda c: (jnp.maximum(rev(c) * per - 1, 0), 0))

    def lay(*shape):
        return pl.BlockSpec((None,) + shape, lambda c: (layer,) + (0,) * len(shape))

    def acc(*shape):
        return pl.BlockSpec(shape, lambda c: (0,) * len(shape))

    big = jax.ShapeDtypeStruct((tp, w), F32)
    vec = jax.ShapeDtypeStruct((1, w), F32)
    return pl.pallas_call(
        body, name=name, grid=(nc,),
        in_specs=[tok, tok, tok, tok, tok, prev, prev, lay(CONV_WIDTH, w), lay(w, w), lay(w, w), lay(1, w), lay(1, w), lay(1, w)],
        out_specs=[tok, tok, acc(CONV_WIDTH, w), acc(1, w), acc(w, w), acc(w, w), acc(1, w), acc(1, w), acc(1, w)],
        out_shape=[big, big, jax.ShapeDtypeStruct((CONV_WIDTH, w), F32), vec, jax.ShapeDtypeStruct((w, w), F32),
                   jax.ShapeDtypeStruct((w, w), F32), vec, vec, vec],
        scratch_shapes=[pltpu.VMEM((tc, w), F32), pltpu.VMEM((tc, w), F32), pltpu.VMEM((tc + pad, w), F32),
                        pltpu.VMEM((tc + pad, w), F32), pltpu.VMEM((tc + pad, w), F32), pltpu.VMEM((1, w), F32)],
        compiler_params=_cparams(("arbitrary",)),
    )(dy, xr, gr, xc, hs, xr, hs, conv_w, wa_bd, wx_bd, b_a, b_x, lam)


def _attn_fwd(q, k, v, tq, name, job=None):
    nh, tp, dqk = q.shape
    vd = v.shape[-1]
    nq = tp // tq
    n_ji = len(job.inputs) if job else 0
    n_jo = len(job.out_shape) if job else 0

    def body(q_ref, k_ref, v_ref, *rest):
        job_in, (o_ref, lse_ref), rest = rest[:n_ji], rest[n_ji:n_ji + 2], rest[n_ji + 2:]
        job_out, sems = rest[:n_jo], rest[n_jo:]
        i = pl.program_id(1)
        if job:
            first = (pl.program_id(0) == 0) & (i == 0)
            pl.when(first)(lambda: job.start(job_in, job_out, sems))
        qb = q_ref[...]

        def absorb(j, width, carry, diagonal=False):
            m, l, acc = carry
            off = pl.multiple_of(j * tq, tq)
            s = _dot_nt(qb, k_ref[pl.ds(off, width), :])
            if diagonal:
                keep = lax.broadcasted_iota(jnp.int32, (tq, width), 1) <= lax.broadcasted_iota(jnp.int32, (tq, width), 0)
                s = jnp.where(keep, s, NEG_BIG)
            m_new = jnp.maximum(m, jnp.max(s, axis=-1, keepdims=True))
            p = jnp.exp2(s - m_new)
            alpha = jnp.exp2(m - m_new)
            l = alpha * l + jnp.sum(p, axis=-1, keepdims=True)
            acc = alpha * acc + jnp.dot(p.astype(BF16), v_ref[pl.ds(off, width), :], preferred_element_type=F32)
            return m_new, l, acc

        carry = (jnp.full((tq, 1), NEG_BIG, F32), jnp.zeros((tq, 1), F32), jnp.zeros((tq, vd), F32))
        quads = lax.shift_right_logical(i, 2)
        carry = lax.fori_loop(0, quads, lambda jj, c: absorb(4 * jj, 4 * tq, c), carry)
        carry = lax.cond((i & 2) != 0, lambda c: absorb(4 * quads, 2 * tq, c), lambda c: c, carry)
        carry = lax.cond((i & 1) != 0, lambda c: absorb(i - 1, tq, c), lambda c: c, carry)
        m, l, acc = absorb(i, tq, carry, diagonal=True)
        o_ref[...] = acc / l
        lse_ref[...] = m + jnp.log2(l)
        if job:
            last = (pl.program_id(0) == nh - 1) & (i == nq - 1)
            pl.when(last)(lambda: job.finish(job_in, job_out, sems))

    outs = pl.pallas_call(
        body, name=name, grid=(nh, nq),
        in_specs=[pl.BlockSpec((None, tq, dqk), lambda h, i: (h, i, 0)),
                  pl.BlockSpec((None, tp, dqk), lambda h, i: (h, 0, 0)),
                  pl.BlockSpec((None, tp, vd), lambda h, i: (h, 0, 0))] + [_ANY] * n_ji,
        out_specs=[pl.BlockSpec((None, tq, vd), lambda h, i: (h, i, 0)),
                   pl.BlockSpec((None, tq, 1), lambda h, i: (h, i, 0))] + [_ANY] * n_jo,
        out_shape=[jax.ShapeDtypeStruct((nh, tp, vd), F32), jax.ShapeDtypeStruct((nh, tp, 1), F32)]
        + (list(job.out_shape) if job else []),
        scratch_shapes=list(job.scratch) if job else [],
        compiler_params=_cparams(("arbitrary", "arbitrary") if job else ("parallel", "parallel")),
    )(q, k, v, *(job.inputs if job else ()))
    return outs[0], outs[1], list(outs[2:])


def _attn_bwd(q, k, v, o, do, lse, tq, name, job=None):
    nh, tp, dqk = q.shape
    vd = v.shape[-1]
    scale = dqk ** -0.5
    nq = tp // tq
    n_ji = len(job.inputs) if job else 0
    n_jo = len(job.out_shape) if job else 0

    def body(q_ref, k_ref, v_ref, o_ref, do_ref, lse_ref, *rest):
        job_in, (dq_ref, dk_ref, dv_ref), rest = rest[:n_ji], rest[n_ji:n_ji + 3], rest[n_ji + 3:]
        job_out, (lse_rows, delta_rows, do_bf, dqt_sc), sems = rest[:n_jo], rest[n_jo:n_jo + 4], rest[n_jo + 4:]
        kb_i = pl.program_id(1)
        if job:
            first = (pl.program_id(0) == 0) & (kb_i == 0)
            pl.when(first)(lambda: job.start(job_in, job_out, sems))

        def as_rows(col):
            return jnp.transpose(jnp.broadcast_to(col, (tq, _LANES)))[0:8, :]

        @pl.when(kb_i == 0)
        def _():
            dqt_sc[...] = jnp.zeros_like(dqt_sc)

            def prep(qi, _):
                off = pl.multiple_of(qi * tq, tq)
                dob = do_ref[pl.ds(off, tq), :]
                do_bf[pl.ds(off, tq), :] = dob.astype(BF16)
                delta_rows[qi] = as_rows(jnp.sum(dob * o_ref[pl.ds(off, tq), :], axis=-1, keepdims=True))
                lse_rows[qi] = as_rows(lse_ref[pl.ds(off, tq), :])
                return 0

            lax.fori_loop(0, nq, prep, 0)

        kb = k_ref[...]
        vb = v_ref[...]

        def absorb(qi, n_blk, carry, diagonal=False):
            dk, dv = carry
            off = pl.multiple_of(qi * tq, tq)
            qb = q_ref[pl.ds(off, n_blk * tq), :]
            dob = do_bf[pl.ds(off, n_blk * tq), :]
            lse_row = jnp.concatenate([lse_rows[qi + b, 0:1, :] for b in range(n_blk)], axis=1)
            delta_row = jnp.concatenate([delta_rows[qi + b, 0:1, :] for b in range(n_blk)], axis=1)
            st = _dot_nt(kb, qb)
            if diagonal:
                keep = lax.broadcasted_iota(jnp.int32, (tq, tq), 0) <= lax.broadcasted_iota(jnp.int32, (tq, tq), 1)
                st = jnp.where(keep, st, NEG_BIG)
            pt = jnp.exp2(st - lse_row)
            dpt = _dot_nt(vb, dob)
            dst = (pt * (dpt - delta_row)).astype(BF16)
            dv = dv + jnp.dot(pt.astype(BF16), dob, preferred_element_type=F32)
            dk = dk + jnp.dot(dst, qb, preferred_element_type=F32)
            dqt = _dot_tn(kb, dst)
            for b in range(n_blk):
                dqt_sc[qi + b] += dqt[:, b * tq:(b + 1) * tq]
            return dk, dv

        carry = absorb(kb_i, 1, (jnp.zeros((tq, dqk), F32), jnp.zeros((tq, vd), F32)), diagonal=True)
        later = nq - 1 - kb_i
        carry = lax.fori_loop(0, later // 2, lambda pp, c: absorb(kb_i + 1 + 2 * pp, 2, c), carry)
        dk, dv = lax.cond(later % 2 == 1, lambda c: absorb(nq - 1, 1, c), lambda c: c, carry)
        dk_ref[...] = dk * math.log(2.0)
        dv_ref[...] = dv

        @pl.when(kb_i == nq - 1)
        def _():
            eye = (lax.broadcasted_iota(jnp.int32, (dqk, dqk), 0) == lax.broadcasted_iota(jnp.int32, (dqk, dqk), 1))
            eye = jnp.where(eye, 1.0, 0.0).astype(BF16)
            for qi in range(nq):
                t = dqt_sc[qi]
                hi = t.astype(BF16)
                lo = (t - hi.astype(F32)).astype(BF16)
                dq_ref[pl.ds(qi * tq, tq), :] = scale * (_dot_tn(hi, eye) + _dot_tn(lo, eye))

        if job:
            last = (pl.program_id(0) == nh - 1) & (kb_i == nq - 1)
            pl.when(last)(lambda: job.finish(job_in, job_out, sems))

    def full(n):
        return pl.BlockSpec((None, tp, n), lambda h, j: (h, 0, 0))

    def blk(n):
        return pl.BlockSpec((None, tq, n), lambda h, j: (h, j, 0))

    outs = pl.pallas_call(
        body, name=name, grid=(nh, nq),
        in_specs=[full(dqk), blk(dqk), blk(vd), full(vd), full(vd), full(1)] + [_ANY] * n_ji,
        out_specs=[full(dqk), blk(dqk), blk(vd)] + [_ANY] * n_jo,
        out_shape=[jax.ShapeDtypeStruct((nh, tp, dqk), F32), jax.ShapeDtypeStruct((nh, tp, dqk), F32),
                   jax.ShapeDtypeStruct((nh, tp, vd), F32)] + (list(job.out_shape) if job else []),
        scratch_shapes=[pltpu.VMEM((nq, 8, tq), F32), pltpu.VMEM((nq, 8, tq), F32), pltpu.VMEM((tp, vd), BF16),
                        pltpu.VMEM((nq, dqk, tq), F32)] + (list(job.scratch) if job else []),
        compiler_params=_cparams(("arbitrary", "arbitrary") if job else ("parallel", "arbitrary")),
    )(q, k, v, o, do, lse, *(job.inputs if job else ()))
    return outs[0], outs[1], outs[2], list(outs[3:])


def _mix_out_fwd(h, y_lru, o, g_lo, g_mo, w_lo, w_mo, g_post, layer, tm, name):
    tp, d = h.shape
    lw = y_lru.shape[-1]
    nh, vd = o.shape[0], o.shape[-1]

    def body(h_ref, yl_ref, o_ref, glo_ref, gmo_ref, wlo_ref, wmo_ref, gpost_ref, hout_ref, y_ref):
        nl, _ = _rms(yl_ref[...], glo_ref[...])
        y = _dot(nl, wlo_ref[...])
        ssq = jnp.zeros((tm, 1), F32)
        for hd in range(nh):
            oh = o_ref[hd]
            ssq += jnp.sum(oh * oh, axis=-1, keepdims=True)
        r = lax.rsqrt(ssq / (nh * vd) + EPS)
        for hd in range(nh):
            y += _dot(o_ref[hd] * r * gmo_ref[hd], wmo_ref[hd])
        y_ref[...] = y
        hout_ref[...] = h_ref[...] + _rms(y, gpost_ref[...])[0]

    tok = pl.BlockSpec((tm, d), lambda i: (i, 0))

    def lay(*shape):
        return pl.BlockSpec((None,) + shape, lambda i: (layer,) + (0,) * len(shape))

    return pl.pallas_call(
        body, name=name, grid=(tp // tm,),
        in_specs=[tok, pl.BlockSpec((tm, lw), lambda i: (i, 0)), pl.BlockSpec((nh, tm, vd), lambda i: (0, i, 0)),
                  lay(1, lw), lay(nh, 1, vd), _whole(lw, d), _whole(nh, vd, d), lay(1, d)],
        out_specs=[tok, tok],
        out_shape=[jax.ShapeDtypeStruct((tp, d), F32), jax.ShapeDtypeStruct((tp, d), F32)],
        compiler_params=_cparams(("parallel",)),
    )(h, y_lru, o, g_lo, g_mo, w_lo, w_mo, g_post)


def _mix_out_bwd(dh_out, y, y_lru, o, g_lo, g_mo, w_lo, w_mo, g_post, layer, tm, name):
    tp, d = y.shape
    lw = y_lru.shape[-1]
    nh, vd = o.shape[0], o.shape[-1]

    def body(dho_ref, y_ref, yl_ref, o_ref, glo_ref, gmo_ref, wlo_ref, wmo_ref, gpost_ref,
             dyl_ref, do_ref, dy_ref, nl_ref, nm_ref, dgpost_ref, dglo_ref, dgmo_ref):
        i = pl.program_id(0)

        @pl.when(i == 0)
        def _():
            dgpost_ref[...] = jnp.zeros_like(dgpost_ref)
            dglo_ref[...] = jnp.zeros_like(dglo_ref)
            dgmo_ref[...] = jnp.zeros_like(dgmo_ref)

        dy, dgp = _rms_bwd(y_ref[...], gpost_ref[...], dho_ref[...])
        dgpost_ref[...] += dgp
        dy_b = dy.astype(BF16)
        dy_ref[...] = dy_b
        yl = yl_ref[...]
        nl_ref[...] = _rms(yl, glo_ref[...])[0].astype(BF16)
        dyl, dgl = _rms_bwd(yl, glo_ref[...], _dot_nt(dy_b, wlo_ref[...]))
        dyl_ref[...] = dyl
        dglo_ref[...] += dgl
        ssq = jnp.zeros((tm, 1), F32)
        for hd in range(nh):
            oh = o_ref[hd]
            ssq += jnp.sum(oh * oh, axis=-1, keepdims=True)
        r = lax.rsqrt(ssq / (nh * vd) + EPS)
        dn = []
        dot_sum = jnp.zeros((tm, 1), F32)
        for hd in range(nh):
            oh = o_ref[hd] * r
            nm_ref[hd] = (oh * gmo_ref[hd]).astype(BF16)
            dn_h = _dot_nt(dy_b, wmo_ref[hd])
            dgmo_ref[hd] += jnp.sum(dn_h * oh, axis=0, keepdims=True)
            dn_h = dn_h * gmo_ref[hd]
            dot_sum += jnp.sum(dn_h * oh, axis=-1, keepdims=True)
            dn.append(dn_h)
        dot_mean = dot_sum / (nh * vd)
        for hd in range(nh):
            do_ref[hd] = r * (dn[hd] - o_ref[hd] * r * dot_mean)

    tok = pl.BlockSpec((tm, d), lambda i: (i, 0))
    tokl = pl.BlockSpec((tm, lw), lambda i: (i, 0))
    heads = pl.BlockSpec((nh, tm, vd), lambda i: (0, i, 0))

    def lay(*shape):
        return pl.BlockSpec((None,) + shape, lambda i: (layer,) + (0,) * len(shape))

    def acc(*shape):
        return pl.BlockSpec(shape, lambda i: (0,) * len(shape))

    return pl.pallas_call(
        body, name=name, grid=(tp // tm,),
        in_specs=[tok, tok, tokl, heads, lay(1, lw), lay(nh, 1, vd), _whole(lw, d), _whole(nh, vd, d), lay(1, d)],
        out_specs=[tokl, heads, tok, tokl, heads, acc(1, d), acc(1, lw), acc(nh, 1, vd)],
        out_shape=[jax.ShapeDtypeStruct((tp, lw), F32), jax.ShapeDtypeStruct((nh, tp, vd), F32),
                   jax.ShapeDtypeStruct((tp, d), BF16), jax.ShapeDtypeStruct((tp, lw), BF16),
                   jax.ShapeDtypeStruct((nh, tp, vd), BF16), jax.ShapeDtypeStruct((1, d), F32),
                   jax.ShapeDtypeStruct((1, lw), F32), jax.ShapeDtypeStruct((nh, 1, vd), F32)],
        compiler_params=_cparams(("arbitrary",)),
    )(dh_out, y, y_lru, o, g_lo, g_mo, w_lo, w_mo, g_post)


def _loss_head(h, target, n_real, tm, name):
    tp, d = h.shape
    nt = tp // tm

    def body(h_ref, t_ref, dh_ref, part_ref):
        i = pl.program_id(0)
        row = i * tm + lax.broadcasted_iota(jnp.int32, (tm, 1), 0)
        real = (row >= N_META) & (row < N_META + n_real)
        err = jnp.where(real, h_ref[...] - t_ref[...], 0.0)
        dh_ref[...] = err * (1.0 / d)
        part_ref[...] = jnp.full((1, 1, 128), 0.5 / d * jnp.sum(err * err), F32)

    tok = pl.BlockSpec((tm, d), lambda i: (i, 0))
    return pl.pallas_call(
        body, name=name, grid=(nt,), in_specs=[tok, tok],
        out_specs=[tok, pl.BlockSpec((1, 1, 128), lambda i: (i, 0, 0))],
        out_shape=[jax.ShapeDtypeStruct((tp, d), F32), jax.ShapeDtypeStruct((nt, 1, 128), F32)],
        compiler_params=_cparams(("parallel",)),
    )(h, target)


def _mesh_pos():
    return lax.axis_index("x"), lax.axis_index("y"), lax.axis_index("c")


def _other_chips(x, y):
    return [(1 - x, y), (x, 1 - y), (1 - x, 1 - y)]


class _GatherJob:
    def __init__(self, inputs, picks, shard_shapes, dtypes, split_last=()):
        self.inputs = list(inputs)
        self.picks = list(picks)
        n = len(self.inputs)
        self.out_shape = [jax.ShapeDtypeStruct((N_CHIPS,) + tuple(s), dt) for s, dt in zip(shard_shapes, dtypes)]
        self.split_last = [a in split_last for a in range(n)]
        self.halves = [(s[1] if second else s[0]) // 2 for s, second in zip(shard_shapes, self.split_last)]
        self.scratch = [pltpu.SemaphoreType.DMA((n, 3)), pltpu.SemaphoreType.DMA((n, 3)), pltpu.SemaphoreType.DMA((n, 3)),
                        pltpu.SemaphoreType.DMA((n, 3)), pltpu.SemaphoreType.DMA((n,))]

    def _copies(self, ins, outs, sems):
        ici_s, ici_r, core_s, core_r, loc = sems
        x, y, c = _mesh_pos()
        me = 2 * x + y
        chips = _other_chips(x, y)
        local, ici, fwd = [], [], []
        for a in range(len(ins)):
            src = self.picks[a](ins[a])
            hv = self.halves[a]

            def half(ref, which, a=a, hv=hv):
                return ref.at[:, pl.ds(which * hv, hv)] if self.split_last[a] else ref.at[pl.ds(which * hv, hv)]

            local.append(pltpu.make_async_copy(src, outs[a].at[me], loc.at[a]))
            for p, (px, py) in enumerate(chips):
                ici.append((pltpu.make_async_remote_copy(src_ref=half(src, c), dst_ref=half(outs[a].at[me], c),
                                                         send_sem=ici_s.at[a, p], recv_sem=ici_r.at[a, p],
                                                         device_id=(px, py, c), device_id_type=MESH),
                            pltpu.make_async_remote_copy(src_ref=half(src, c), dst_ref=half(outs[a].at[2 * px + py], c),
                                                         send_sem=ici_s.at[a, p], recv_sem=ici_r.at[a, p],
                                                         device_id=(px, py, c), device_id_type=MESH)))
                landed = half(outs[a].at[2 * px + py], c)
                theirs = half(outs[a].at[2 * px + py], 1 - c)
                fwd.append((pltpu.make_async_remote_copy(src_ref=landed, dst_ref=landed, send_sem=core_s.at[a, p],
                                                         recv_sem=core_r.at[a, p], device_id=(x, y, 1 - c), device_id_type=MESH),
                            pltpu.make_async_remote_copy(src_ref=theirs, dst_ref=theirs, send_sem=core_s.at[a, p],
                                                         recv_sem=core_r.at[a, p], device_id=(x, y, 1 - c), device_id_type=MESH)))
        return local, ici, fwd

    def start(self, ins, outs, sems):
        local, ici, _ = self._copies(ins, outs, sems)
        for cp in local:
            cp.start()
        for send, _ in ici:
            send.start()

    def finish(self, ins, outs, sems):
        local, ici, fwd = self._copies(ins, outs, sems)
        for (_, arrive), (hand_on, _) in zip(ici, fwd):
            arrive.wait_recv()
            hand_on.start()
        for _, arrive in fwd:
            arrive.wait_recv()
        for send, _ in ici:
            send.wait_send()
        for hand_on, _ in fwd:
            hand_on.wait_send()
        for cp in local:
            cp.wait()


class _ReduceJob:
    def __init__(self, inputs):
        self.inputs = list(inputs)
        n = len(self.inputs)
        self.out_shape = [jax.ShapeDtypeStruct(a.shape, a.dtype) for a in self.inputs] * 2
        self.scratch = [pltpu.SemaphoreType.DMA((n, 3)), pltpu.SemaphoreType.DMA((n, 3)), pltpu.SemaphoreType.DMA((n, 4)),
                        pltpu.SemaphoreType.DMA((n, 4)), pltpu.SemaphoreType.DMA((n,))]

    def _copies(self, ins, outs, sems):
        ici_s, ici_r, core_s, core_r, loc = sems
        n = len(ins)
        same, other = outs[:n], outs[n:]
        x, y, c = _mesh_pos()
        me = 2 * x + y
        chips = _other_chips(x, y)
        sib = (x, y, 1 - c)
        local, ici, fwd = [], [], []
        for a in range(n):
            local.append(pltpu.make_async_copy(ins[a].at[me], same[a].at[me], loc.at[a]))
            fwd.append((pltpu.make_async_remote_copy(src_ref=ins[a].at[me], dst_ref=other[a].at[me], send_sem=core_s.at[a, 3],
                                                     recv_sem=core_r.at[a, 3], device_id=sib, device_id_type=MESH), None))
            for p, (px, py) in enumerate(chips):
                cp = 2 * px + py
                ici.append((pltpu.make_async_remote_copy(src_ref=ins[a].at[cp], dst_ref=same[a].at[me], send_sem=ici_s.at[a, p],
                                                         recv_sem=ici_r.at[a, p], device_id=(px, py, c), device_id_type=MESH),
                            pltpu.make_async_remote_copy(src_ref=ins[a].at[cp], dst_ref=same[a].at[cp], send_sem=ici_s.at[a, p],
                                                         recv_sem=ici_r.at[a, p], device_id=(px, py, c), device_id_type=MESH)))
                fwd.append((pltpu.make_async_remote_copy(src_ref=same[a].at[cp], dst_ref=other[a].at[cp], send_sem=core_s.at[a, p],
                                                         recv_sem=core_r.at[a, p], device_id=sib, device_id_type=MESH), p))
        return local, ici, fwd

    def start(self, ins, outs, sems):
        local, ici, fwd = self._copies(ins, outs, sems)
        for cp in local:
            cp.start()
        for send, _ in ici:
            send.start()
        for hand_on, p in fwd:
            if p is None:
                hand_on.start()

    def finish(self, ins, outs, sems):
        local, ici, fwd = self._copies(ins, outs, sems)
        chip_fwd = [f for f in fwd if f[1] is not None]
        for (_, arrive), (hand_on, _) in zip(ici, chip_fwd):
            arrive.wait_recv()
            hand_on.start()
        for hand_on, _ in fwd:
            hand_on.wait()
        for send, _ in ici:
            send.wait_send()
        for cp in local:
            cp.wait()


class _DeviceGatherJob:
    def __init__(self, buf):
        self.inputs = [buf]
        self.out_shape = [jax.ShapeDtypeStruct((8,) + buf.shape, buf.dtype)]
        self.scratch = [pltpu.SemaphoreType.DMA((7,)), pltpu.SemaphoreType.DMA((7,)), pltpu.SemaphoreType.DMA]

    def _copies(self, ins, outs, sems):
        send_s, recv_s, loc = sems
        x, y, c = _mesh_pos()
        me = 4 * x + 2 * y + c
        local = pltpu.make_async_copy(ins[0], outs[0].at[me], loc)
        remote = []
        for rel in range(1, 8):
            fx, fy, fc = (rel >> 2) & 1, (rel >> 1) & 1, rel & 1
            peer = (1 - x if fx else x, 1 - y if fy else y, 1 - c if fc else c)
            remote.append(pltpu.make_async_remote_copy(src_ref=ins[0], dst_ref=outs[0].at[me], send_sem=send_s.at[rel - 1],
                                                       recv_sem=recv_s.at[rel - 1], device_id=peer, device_id_type=MESH))
        return local, remote

    def start(self, ins, outs, sems):
        local, remote = self._copies(ins, outs, sems)
        local.start()
        for rc in remote:
            rc.start()

    def finish(self, ins, outs, sems):
        local, remote = self._copies(ins, outs, sems)
        for rc in remote:
            rc.wait()
        local.wait()


class _JobGroup:
    def __init__(self, jobs):
        self.jobs = list(jobs)
        self.inputs = [a for j in self.jobs for a in j.inputs]
        self.out_shape = [s for j in self.jobs for s in j.out_shape]
        self.scratch = [s for j in self.jobs for s in j.scratch]
        self.n_in = [len(j.inputs) for j in self.jobs]
        self.n_out = [len(j.out_shape) for j in self.jobs]
        self.n_sem = [len(j.scratch) for j in self.jobs]

    def _parts(self, k, ins, outs, sems):
        return (ins[sum(self.n_in[:k]):sum(self.n_in[:k + 1])], outs[sum(self.n_out[:k]):sum(self.n_out[:k + 1])],
                sems[sum(self.n_sem[:k]):sum(self.n_sem[:k + 1])])

    def start(self, ins, outs, sems):
        for k, job in enumerate(self.jobs):
            job.start(*self._parts(k, ins, outs, sems))

    def finish(self, ins, outs, sems):
        for k, job in enumerate(self.jobs):
            job.finish(*self._parts(k, ins, outs, sems))

    def split(self, outs):
        return [list(outs[sum(self.n_out[:k]):sum(self.n_out[:k + 1])]) for k in range(len(self.jobs))]


def _run_jobs(jobs, name):
    group = _JobGroup(jobs)
    n_in, n_out = len(group.inputs), len(group.out_shape)

    def body(*refs):
        ins, outs, sems = refs[:n_in], refs[n_in:n_in + n_out], refs[n_in + n_out:]
        group.start(ins, outs, sems)
        group.finish(ins, outs, sems)

    res = pl.pallas_call(body, name=name, in_specs=[_ANY] * n_in, out_specs=[_ANY] * n_out, out_shape=list(group.out_shape),
                         scratch_shapes=list(group.scratch))(*group.inputs)
    return group.split(res)


def _sum_devices(parts, name):
    n_dev, rows, lanes = parts.shape

    def body(p_ref, o_ref):
        total = p_ref[0]
        for k in range(1, n_dev):
            total = total + p_ref[k]
        o_ref[...] = total

    return pl.pallas_call(body, name=name, out_shape=jax.ShapeDtypeStruct((rows, lanes), F32),
                          compiler_params=pltpu.CompilerParams(vmem_limit_bytes=V7X_VMEM_LIMIT_BYTES))(parts)


def _row_tile(rows, limit=512):
    best = None
    for t in range(16, min(rows, limit) + 1, 16):
        if rows % t == 0:
            best = t
    return best if best is not None else rows


def _adamw_math(w, g, m, v):
    m = ADAM_B1 * m + (1.0 - ADAM_B1) * g
    v = ADAM_B2 * v + (1.0 - ADAM_B2) * (g * g)
    m_hat = m / (1.0 - ADAM_B1 ** ADAM_STEP)
    v_hat = v / (1.0 - ADAM_B2 ** ADAM_STEP)
    delta = -ADAM_LR * (m_hat / (jnp.sqrt(v_hat) + ADAM_EPS) + ADAM_WD * w)
    return delta, m, v


def _adamw_layer(w, m, v, same_plane, other_plane, layer, prev, name):
    nl, r, ccol = w.shape
    ns = same_plane.shape[0]
    tr = _row_tile(r, 512)

    def body(w_ref, m_ref, v_ref, a_ref, b_ref, *rest):
        go_ref, d_ref, mo_ref, vo_ref = rest[-4:]
        sa = a_ref[0].astype(F32)
        sb = b_ref[0].astype(F32)
        for k in range(1, ns):
            sa = sa + a_ref[k].astype(F32)
            sb = sb + b_ref[k].astype(F32)
        g = sa + sb
        go_ref[...] = g
        d_ref[...], mo_ref[...], vo_ref[...] = _adamw_math(w_ref[...], g, m_ref[...], v_ref[...])

    blk = pl.BlockSpec((None, tr, ccol), lambda i: (layer, i, 0))
    plane = pl.BlockSpec((ns, tr, ccol), lambda i: (0, i, 0))
    out = jax.ShapeDtypeStruct((nl, r, ccol), F32)
    n_prev = 4 if prev is not None else 0
    return pl.pallas_call(
        body, name=name, grid=(r // tr,),
        in_specs=[blk, blk, blk, plane, plane] + [_ANY] * n_prev,
        out_specs=[blk, blk, blk, blk], out_shape=[out, out, out, out],
        input_output_aliases={5 + k: k for k in range(n_prev)},
        compiler_params=_cparams(("parallel",)))(w, m, v, same_plane, other_plane, *(prev or ()))


def _adamw_whole(w, g, m, v, name):
    def body(w_ref, g_ref, m_ref, v_ref, d_ref, mo_ref, vo_ref):
        d_ref[...], mo_ref[...], vo_ref[...] = _adamw_math(w_ref[...], g_ref[...], m_ref[...], v_ref[...])

    out = jax.ShapeDtypeStruct(w.shape, F32)
    return pl.pallas_call(body, name=name, out_shape=[out, out, out],
                          compiler_params=pltpu.CompilerParams(vmem_limit_bytes=V7X_VMEM_LIMIT_BYTES))(w, g, m, v)


_W_NAMES = ['meta_tokens', 'ffn1_pre_g', 'ffn1_w_gate', 'ffn1_w_up', 'ffn1_w_down', 'ffn1_post_g', 'mix_pre_g', 'w_in',
            'lru_conv_w', 'lru_conv_b', 'lru_w_a', 'lru_b_a', 'lru_w_x', 'lru_b_x', 'lru_lambda', 'mla_q_norm_g', 'mla_w_uq',
            'mla_kv_norm_g', 'mla_w_ukv', 'lru_out_g', 'mla_out_g', 'w_out', 'mix_post_g', 'ffn2_pre_g', 'ffn2_w_gate',
            'ffn2_w_up', 'ffn2_w_down', 'ffn2_post_g']
_REPLICATED = ['ffn1_pre_g', 'ffn1_post_g', 'mix_pre_g', 'lru_conv_b', 'lru_w_a', 'lru_b_a', 'lru_w_x', 'lru_b_x', 'lru_lambda',
               'mla_q_norm_g', 'mla_kv_norm_g', 'lru_out_g', 'mla_out_g', 'mix_post_g', 'ffn2_pre_g', 'ffn2_post_g']
_LANES = 128


def _pack_rows(arrays):
    flat = jnp.concatenate([a.reshape(-1) for a in arrays])
    total = _round_up(flat.shape[0], 8 * _LANES)
    return jnp.pad(flat, (0, total - flat.shape[0])).reshape(-1, _LANES)


def _unpack_rows(buf, shapes):
    flat = buf.reshape(-1)
    out, off = [], 0
    for shp in shapes:
        n = math.prod(shp)
        out.append(flat[off:off + n].reshape(shp))
        off += n
    return out


def _block_diag(w):
    nl, nh, n, _ = w.shape
    eye = jnp.eye(nh, dtype=w.dtype)
    return (w[:, :, :, None, :] * eye[None, :, None, :, None]).reshape(nl, nh * n, nh * n)


def _diag_blocks(bd, nh):
    n = bd.shape[0] // nh
    b4 = bd.reshape(nh, n, nh, n)
    return jnp.stack([b4[i, :, i, :] for i in range(nh)])


def kernel(x, meta_tokens, ffn1_pre_g, ffn1_w_gate, ffn1_w_up, ffn1_w_down, ffn1_post_g, mix_pre_g, w_in, lru_conv_w, lru_conv_b, lru_w_a, lru_b_a, lru_w_x, lru_b_x, lru_lambda, mla_q_norm_g, mla_w_uq, mla_kv_norm_g, mla_w_ukv, lru_out_g, mla_out_g, w_out, mix_post_g, ffn2_pre_g, ffn2_w_gate, ffn2_w_up, ffn2_w_down, ffn2_post_g, loss_target, m_meta_tokens, m_ffn1_pre_g, m_ffn1_w_gate, m_ffn1_w_up, m_ffn1_w_down, m_ffn1_post_g, m_mix_pre_g, m_w_in, m_lru_conv_w, m_lru_conv_b, m_lru_w_a, m_lru_b_a, m_lru_w_x, m_lru_b_x, m_lru_lambda, m_mla_q_norm_g, m_mla_w_uq, m_mla_kv_norm_g, m_mla_w_ukv, m_lru_out_g, m_mla_out_g, m_w_out, m_mix_post_g, m_ffn2_pre_g, m_ffn2_w_gate, m_ffn2_w_up, m_ffn2_w_down, m_ffn2_post_g, v_meta_tokens, v_ffn1_pre_g, v_ffn1_w_gate, v_ffn1_w_up, v_ffn1_w_down, v_ffn1_post_g, v_mix_pre_g, v_w_in, v_lru_conv_w, v_lru_conv_b, v_lru_w_a, v_lru_b_a, v_lru_w_x, v_lru_b_x, v_lru_lambda, v_mla_q_norm_g, v_mla_w_uq, v_mla_kv_norm_g, v_mla_w_ukv, v_lru_out_g, v_mla_out_g, v_w_out, v_mix_post_g, v_ffn2_pre_g, v_ffn2_w_gate, v_ffn2_w_up, v_ffn2_w_down, v_ffn2_post_g):
    a = dict(locals())
    x2 = x[0]
    s_len, d = x2.shape
    nl = ffn1_pre_g.shape[0]
    lw = lru_conv_b.shape[-1]
    ql, kvl = mla_q_norm_g.shape[-1], mla_kv_norm_g.shape[-1]
    nh = MLA_HEADS
    vd = (d - lw) // nh
    t_real = N_META + s_len
    tp = _round_up(t_real, 384)
    tm = tp // 6
    tmb = tm // 2
    tm_ffn_fwd = tp // 4
    tm_ffn_bwd = tp // 8
    tq = 384 if tp >= 1536 else 128
    xi, yi, _ = _mesh_pos()
    chip = 2 * xi + yi

    def tr(w):
        return jnp.swapaxes(w, 1, 2)

    ffn_loc = jnp.stack([tr(ffn1_w_gate), tr(ffn1_w_up), ffn1_w_down, tr(ffn2_w_gate), tr(ffn2_w_up), ffn2_w_down],
                        axis=1).astype(BF16)
    loc = [ffn_loc, tr(w_in).astype(BF16), tr(mla_w_uq).astype(BF16), mla_w_ukv.astype(BF16), w_out.astype(BF16)]
    loc_shapes, loc_dtypes = [t.shape[1:] for t in loc], [t.dtype for t in loc]

    def layer_gather(l):
        return _GatherJob(loc, [lambda r, l=l: r.at[l]] * len(loc), loc_shapes, loc_dtypes, split_last=(1,))

    half_ffn = (ffn_loc.shape[1] // 2,) + ffn_loc.shape[2:]
    first = _GatherJob([ffn_loc, lru_conv_w, meta_tokens], [lambda r: r.at[0, pl.ds(0, half_ffn[0])], lambda r: r, lambda r: r],
                       [half_ffn, lru_conv_w.shape, meta_tokens.shape], [BF16, F32, F32], split_last=(0,))
    (w_ffn1_l0, g_conv, g_meta), = _run_jobs([first], "gather_first")
    rest_l0 = _GatherJob(loc, [lambda r: r.at[0, pl.ds(half_ffn[0], half_ffn[0])]] + [lambda r: r.at[0]] * (len(loc) - 1),
                         [half_ffn] + loc_shapes[1:], loc_dtypes, split_last=(0, 1))
    o0, o1, o2, o3 = lw, 2 * lw, 2 * lw + ql, 2 * lw + ql + kvl

    def assemble(gw, ffn1=None):
        w_ffn, g_win, g_wuq, g_wukv, g_wout = gw
        w_in_t = g_win.reshape(-1, d)
        w_kr = jnp.pad(w_in_t[o3:], ((QK_NOPE, 0), (0, 0)))
        w_ukv3 = jnp.transpose(g_wukv, (1, 0, 2)).reshape(kvl, nh, QK_NOPE + vd)
        w_out_full = g_wout.reshape(d, d)
        return dict(ffn1=ffn1 or (w_ffn, 0), ffn2=(w_ffn, 0 if ffn1 else 3),
                    parts=(w_in_t[:o0], w_in_t[o0:o1], w_in_t[o1:o2], w_in_t[o2:o3], w_kr),
                    q=g_wuq.reshape(nh, QK_DIM, ql),
                    k=jnp.transpose(jnp.pad(w_ukv3[..., :QK_NOPE], ((0, 0), (0, 0), (0, QK_ROPE))), (1, 0, 2)),
                    v=jnp.transpose(w_ukv3[..., QK_NOPE:], (1, 0, 2)),
                    lo=w_out_full[:lw], mo=w_out_full[lw:].reshape(nh, vd, d))

    conv_full = jnp.transpose(g_conv, (1, 2, 0, 3)).reshape(nl, CONV_WIDTH, lw)
    meta_full = jnp.transpose(g_meta, (1, 0, 2)).reshape(N_META, d)
    wa_bd = _block_diag(lru_w_a).astype(BF16)
    wx_bd = _block_diag(lru_w_x).astype(BF16)

    def gain3(g):
        return g.reshape(nl, 1, g.shape[-1])

    g3 = {n: gain3(a[n]) for n in ('ffn1_pre_g', 'ffn1_post_g', 'mix_pre_g', 'lru_conv_b', 'lru_b_a', 'lru_b_x', 'lru_lambda',
                                   'mla_q_norm_g', 'mla_kv_norm_g', 'lru_out_g', 'mix_post_g', 'ffn2_pre_g', 'ffn2_post_g')}
    g_mo = mla_out_g.reshape(nl, nh, 1, vd)

    pos = jnp.arange(tp, dtype=F32)
    inv_freq = 1.0 / (ROPE_THETA ** (jnp.arange(0, QK_ROPE, 2, dtype=F32) / QK_ROPE))
    ang = pos[:, None] * inv_freq[None, :]
    cos_t = jnp.concatenate([jnp.ones((tp, QK_NOPE), F32), jnp.cos(ang), jnp.cos(ang)], axis=1)
    sin_t = jnp.concatenate([jnp.zeros((tp, QK_NOPE), F32), jnp.sin(ang), jnp.sin(ang)], axis=1)
    half = QK_ROPE // 2
    rot = np.zeros((QK_DIM, QK_DIM), np.float32)
    for i in range(half):
        rot[QK_NOPE + half + i, QK_NOPE + i] = -1.0
        rot[QK_NOPE + i, QK_NOPE + half + i] = 1.0
    rot_b, rot_t = jnp.asarray(rot, BF16), jnp.asarray(rot.T, BF16)

    h = jnp.concatenate([meta_full, x2, jnp.zeros((tp - t_real, d), F32)], axis=0)
    target = jnp.pad(loss_target[0], ((N_META, tp - t_real), (0, 0)))
    saved, weights = [], []
    for l in range(nl):
        if l == 0:
            h1, gate1, up1, f1, gathered = _ffn_fwd(h, g3['ffn1_pre_g'], g3['ffn1_post_g'], w_ffn1_l0, l, 0, 1, 2, tm_ffn_fwd,
                                                    f"ffn1_fwd_{l}", rest_l0)
            wl = assemble(gathered, ffn1=(w_ffn1_l0, 0))
        else:
            wl = assemble(gathered)
            wf, k0 = wl['ffn1']
            h1, gate1, up1, f1, _ = _ffn_fwd(h, g3['ffn1_pre_g'], g3['ffn1_post_g'], wf, l, k0, k0 + 1, k0 + 2, tm_ffn_fwd,
                                             f"ffn1_fwd_{l}")
        weights.append(wl)
        xr, gr, cq, ckv, q, k, v = _mix_in_fwd(h1, g3['mix_pre_g'], wl['parts'], g3['mla_q_norm_g'], g3['mla_kv_norm_g'],
                                               wl['q'], wl['k'], wl['v'], cos_t, sin_t, rot_b, l, tm, f"mix_in_fwd_{l}")
        y_lru, xc, hs = _lru_fwd(xr, gr, conv_full, g3['lru_conv_b'], wa_bd, wx_bd, g3['lru_b_a'], g3['lru_b_x'],
                                 g3['lru_lambda'], l, tm, f"lru_fwd_{l}")
        o, lse, gathered = _attn_fwd(q, k, v, tq, f"attn_fwd_{l}", layer_gather(l + 1) if l + 1 < nl else None)
        h2, y = _mix_out_fwd(h1, y_lru, o, g3['lru_out_g'], g_mo, wl['lo'], wl['mo'], g3['mix_post_g'], l, tm, f"mix_out_fwd_{l}")
        wf, k0 = wl['ffn2']
        h3, gate2, up2, f2, _ = _ffn_fwd(h2, g3['ffn2_pre_g'], g3['ffn2_post_g'], wf, l, k0, k0 + 1, k0 + 2, tm_ffn_fwd,
                                         f"ffn2_fwd_{l}")
        saved.append((h, gate1, up1, f1, h1, xr, gr, cq, ckv, q, k, v, y_lru, xc, hs, o, lse, y, h2, gate2, up2, f2))
        h = h3

    dh, loss_parts = _loss_head(h, target, s_len, tm, "loss_head")
    loss = lax.psum(jnp.sum(loss_parts[:, 0, 0]), ("x", "y", "c"))

    small_order = _REPLICATED + ['lru_conv_w']
    small = {n: [None] * nl for n in small_order}
    planes = {}
    late_names = ['w_in', 'mla_w_uq', 'mla_w_ukv', 'ffn1_w_gate', 'ffn1_w_up', 'ffn1_w_down']
    early_names = ['ffn2_w_gate', 'ffn2_w_up', 'ffn2_w_down', 'w_out']
    late = None

    def keep_planes(keys, outs):
        for i, key in enumerate(keys):
            planes[key] = (outs[i], outs[len(keys) + i])

    for l in reversed(range(nl)):
        (h0, gate1, up1, f1, h1, xr, gr, cq, ckv, q, k, v, y_lru, xc, hs, o, lse, y, h2, gate2, up2, f2) = saved[l]
        wl = weights[l]
        wf, k0 = wl['ffn2']
        dh, dgate, dup, act, df, u, dgpre, dgpost, _ = _ffn_bwd(dh, h2, f2, gate2, up2, g3['ffn2_pre_g'], g3['ffn2_post_g'],
                                                               wf, l, k0, k0 + 1, k0 + 2, tm_ffn_bwd, f"ffn2_bwd_{l}")
        small['ffn2_pre_g'][l], small['ffn2_post_g'][l] = dgpre, dgpost
        d_wg2 = _mm_tn(dgate, u, tm, BF16, f"dw_gate2_{l}")
        d_wu2 = _mm_tn(dup, u, tm, BF16, f"dw_up2_{l}")
        dn2 = _mm_tn(act, df, tm, BF16, f"dw_down2_{l}")

        d_ylru, d_o, dy_b, n_lo, n_mo, dgpost, dglo, dgmo = _mix_out_bwd(dh, y, y_lru, o, g3['lru_out_g'], g_mo, wl['lo'], wl['mo'],
                                                                       g3['mix_post_g'], l, tm, f"mix_out_bwd_{l}")
        small['mix_post_g'][l], small['lru_out_g'][l], small['mla_out_g'][l] = dgpost, dglo, dgmo
        d_wlo = _mm_tn(n_lo, dy_b, tm, F32, f"dw_out_lru_{l}")
        d_wmo = _mm_tn_heads(n_mo, dy_b, tm, f"dw_out_mla_{l}")
        d_wout = jnp.concatenate([d_wlo, d_wmo.reshape(nh * vd, d)], axis=0).reshape(N_CHIPS, d // N_CHIPS, d).astype(BF16)

        keys = [(n, l) for n in early_names] + ([(n, l + 1) for n in late_names] if late else [])
        jobs = [_ReduceJob([d_wg2, d_wu2, dn2, d_wout] + (late or []))]
        if l == 0 and nl > 1:
            deep = _pack_rows([jnp.stack(small[n][1:]) for n in small_order])
            jobs.append(_DeviceGatherJob(deep))
        group = _JobGroup(jobs)
        dq, dk, dv, outs = _attn_bwd(q, k, v, o, d_o, lse, tq, f"attn_bwd_{l}", group)
        outs = group.split(outs)
        keep_planes(keys, outs[0])
        if len(jobs) > 1:
            all_deep = outs[1][0]
        d_xr, d_gr, dcw, dcb, dwa, dwx, dba, dbx, dlam = _lru_bwd(d_ylru, xr, gr, xc, hs, conv_full, wa_bd, wx_bd, g3['lru_b_a'],
                                                                 g3['lru_b_x'], g3['lru_lambda'], l, tm, f"lru_bwd_{l}")
        small['lru_conv_w'][l], small['lru_conv_b'][l] = dcw, dcb
        small['lru_w_a'][l], small['lru_w_x'][l] = _diag_blocks(dwa, LRU_HEADS), _diag_blocks(dwx, LRU_HEADS)
        small['lru_b_a'][l], small['lru_b_x'][l], small['lru_lambda'][l] = dba, dbx, dlam

        dh, u, qn, kvn, dq_pre, d_cq, d_ckv, d_kr, dgpre, dgq, dgkv = _mix_in_bwd(
            dh, h1, cq, ckv, d_xr, d_gr, dq, dk, dv, g3['mix_pre_g'], wl['parts'], g3['mla_q_norm_g'], g3['mla_kv_norm_g'],
            wl['q'], wl['k'], wl['v'], cos_t, sin_t, rot_t, l, tmb, f"mix_in_bwd_{l}")
        small['mix_pre_g'][l], small['mla_q_norm_g'][l], small['mla_kv_norm_g'][l] = dgpre, dgq, dgkv
        d_win = jnp.concatenate([_mm_tn(d_xr, u, tm, F32, f"dw_in_xr_{l}"), _mm_tn(d_gr, u, tm, F32, f"dw_in_gr_{l}"),
                                 _mm_tn(d_cq, u, tm, F32, f"dw_in_cq_{l}"), _mm_tn(d_ckv, u, tm, F32, f"dw_in_ckv_{l}"),
                                 _mm_tn(d_kr, u, tm, F32, f"dw_in_kr_{l}")[QK_NOPE:]], axis=0)
        d_win = d_win.reshape(N_CHIPS, -1, d).astype(BF16)
        d_wuq = _mm_tn_heads(dq_pre, qn, tm, f"dw_uq_{l}").reshape(N_CHIPS, -1, ql).astype(BF16)
        d_wk = _mm_tn_heads(dk, kvn, tm, f"dw_uk_{l}")[:, :QK_NOPE]
        d_wv = _mm_tn_heads(dv, kvn, tm, f"dw_uv_{l}")
        d_wukv = jnp.transpose(jnp.concatenate([d_wk, d_wv], axis=1).reshape(N_CHIPS, -1, kvl), (0, 2, 1)).astype(BF16)

        wf, k0 = wl['ffn1']
        last_layer = l == 0
        dh, dgate, dup, act, df, u, dgpre, dgpost, outs = _ffn_bwd(
            dh, h0, f1, gate1, up1, g3['ffn1_pre_g'], g3['ffn1_post_g'], wf, l, k0, k0 + 1, k0 + 2, tm_ffn_bwd, f"ffn1_bwd_{l}",
            _ReduceJob([d_win, d_wuq, d_wukv]) if last_layer else None)
        small['ffn1_pre_g'][l], small['ffn1_post_g'][l] = dgpre, dgpost
        d_wg1 = _mm_tn(dgate, u, tm, BF16, f"dw_gate1_{l}")
        if last_layer:
            keep_planes([(n, 0) for n in late_names[:3]], outs)
            d_wu1, outs = _mm_tn(dup, u, tm, BF16, f"dw_up1_{l}", _ReduceJob([d_wg1]))
            keep_planes([('ffn1_w_gate', 0)], outs)
            dn1, outs = _mm_tn(act, df, tm, BF16, f"dw_down1_{l}", _ReduceJob([d_wu1]))
            keep_planes([('ffn1_w_up', 0)], outs)
            late = [dn1]
        else:
            d_wu1 = _mm_tn(dup, u, tm, BF16, f"dw_up1_{l}")
            dn1 = _mm_tn(act, df, tm, BF16, f"dw_down1_{l}")
            late = [d_win, d_wuq, d_wukv, d_wg1, d_wu1, dn1]

    grad_x = dh[N_META:t_real][None]

    shallow = _pack_rows([small[n][0] for n in small_order] + [dh[:N_META]])
    late_planes, (all_shallow,) = _run_jobs([_ReduceJob(late), _DeviceGatherJob(shallow)], "reduce_last")
    keep_planes([('ffn1_w_down', 0)], late_planes)

    res = {}
    stored_transposed = ('ffn1_w_gate', 'ffn1_w_up', 'ffn2_w_gate', 'ffn2_w_up', 'w_in', 'mla_w_uq')
    for name in early_names + late_names:
        view = tr if name in stored_transposed else (lambda t: t)
        w_v, m_v, v_v = view(a[name]), view(a['m_' + name]), view(a['v_' + name])
        prev = None
        for l in reversed(range(nl)):
            same_plane, other_plane = planes[(name, l)]
            prev = _adamw_layer(w_v, m_v, v_v, same_plane, other_plane, l, prev, f"adamw_{name}_{l}")
        res[name] = [view(t) for t in prev]

    shapes0 = [small[n][0].shape for n in small_order]
    pieces = _unpack_rows(_sum_devices(all_shallow, "sum_small_last"), shapes0 + [(N_META, d)])
    per_layer = [p[None] for p in pieces[:-1]]
    if nl > 1:
        deeper = _unpack_rows(_sum_devices(all_deep, "sum_small_deep"), [(nl - 1,) + s for s in shapes0])
        per_layer = [jnp.concatenate([p0, p1], axis=0) for p0, p1 in zip(per_layer, deeper)]
    g_small = {n: p.reshape(a[n].shape if n != 'lru_conv_w' else conv_full.shape) for n, p in zip(small_order, per_layer)}
    g_conv_loc = lax.dynamic_slice_in_dim(g_small.pop('lru_conv_w'), chip * lru_conv_w.shape[-1], lru_conv_w.shape[-1], axis=2)
    g_meta_loc = lax.dynamic_slice_in_dim(pieces[-1], chip * meta_tokens.shape[-1], meta_tokens.shape[-1], axis=1)

    shapes = [a[n].shape for n in _REPLICATED]
    dl, mo, vo = _adamw_whole(_pack_rows([a[n] for n in _REPLICATED]), _pack_rows([g_small[n] for n in _REPLICATED]),
                              _pack_rows([a['m_' + n] for n in _REPLICATED]), _pack_rows([a['v_' + n] for n in _REPLICATED]),
                              "adamw_replicated")
    for n, dd, mm, vv in zip(_REPLICATED, _unpack_rows(dl, shapes), _unpack_rows(mo, shapes), _unpack_rows(vo, shapes)):
        res[n] = (g_small[n], dd, mm, vv)
    cshape = lru_conv_w.shape
    c2 = (cshape[0] * cshape[1], cshape[2])
    dd, mm, vv = _adamw_whole(lru_conv_w.reshape(c2), g_conv_loc.reshape(c2), m_lru_conv_w.reshape(c2), v_lru_conv_w.reshape(c2),
                              "adamw_conv_w")
    res['lru_conv_w'] = (g_conv_loc, dd.reshape(cshape), mm.reshape(cshape), vv.reshape(cshape))
    res['meta_tokens'] = (g_meta_loc,) + tuple(_adamw_whole(meta_tokens, g_meta_loc, m_meta_tokens, v_meta_tokens, "adamw_meta"))

    return (loss, grad_x, *[res[n][0] for n in _W_NAMES], *[res[n][1] for n in _W_NAMES],
            *[res[n][2] for n in _W_NAMES], *[res[n][3] for n in _W_NAMES])
```

```python
import functools
import math

import jax
import jax.numpy as jnp
import numpy as np
from jax import lax
from jax.experimental import pallas as pl
from jax.experimental.pallas import tpu as pltpu

F32 = jnp.float32
BF16 = jnp.bfloat16
MESH = pl.DeviceIdType.MESH

EPS = 1e-6
N_META = 16
LRU_HEADS = 8
MLA_HEADS = 8
QK_NOPE = 64
QK_ROPE = 32
QK_DIM = QK_NOPE + QK_ROPE
LRU_C = 8.0
ROPE_THETA = 10000.0
CONV_WIDTH = 4
N_CHIPS = 4

ADAM_LR = 0.001
ADAM_B1 = 0.9
ADAM_B2 = 0.999
ADAM_EPS = 1e-08
ADAM_WD = 0.01
ADAM_STEP = 10

V7X_VMEM_LIMIT_BYTES = 56 * 1024 * 1024
NEG_BIG = -1e30
Q_PRESCALE = QK_DIM ** -0.5 * math.log2(math.e)


def _cparams(sem=None):
    return pltpu.CompilerParams(dimension_semantics=sem, vmem_limit_bytes=V7X_VMEM_LIMIT_BYTES)


def _round_up(a, b):
    return -(-a // b) * b


def _whole(*shape):
    return pl.BlockSpec(shape, lambda *_: (0,) * len(shape))


_ANY = pl.BlockSpec(memory_space=pl.ANY)


def _dot(a, b):
    return jnp.dot(a.astype(BF16), b.astype(BF16), preferred_element_type=F32)


def _dot_nt(a, b):
    return lax.dot_general(a.astype(BF16), b.astype(BF16), (((1,), (1,)), ((), ())), preferred_element_type=F32)


def _dot_tn(a, b):
    return lax.dot_general(a.astype(BF16), b.astype(BF16), (((0,), (0,)), ((), ())), preferred_element_type=F32)


def _dot_split(x, p):
    hi = x.astype(BF16)
    lo = (x - hi.astype(F32)).astype(BF16)
    return jnp.dot(hi, p, preferred_element_type=F32) + jnp.dot(lo, p, preferred_element_type=F32)


def _rms(x, g):
    r = lax.rsqrt(jnp.mean(x * x, axis=-1, keepdims=True) + EPS)
    return x * r * g, r


def _rms_bwd(x, g, dy):
    r = lax.rsqrt(jnp.mean(x * x, axis=-1, keepdims=True) + EPS)
    xh = x * r
    dyg = dy * g
    dx = r * (dyg - xh * jnp.mean(dyg * xh, axis=-1, keepdims=True))
    dg = jnp.sum(dy * xh, axis=0, keepdims=True)
    return dx, dg


def _sigmoid(x):
    return 1.0 / (1.0 + jnp.exp(-x))


def _neg_expm1(x):
    series = -x * (1.0 + x * (0.5 + x * (1.0 / 6.0 + x * (1.0 / 24.0))))
    return jnp.where(jnp.abs(x) < 0.03, series, 1.0 - jnp.exp(x))


_GELU_K = math.sqrt(2.0 / math.pi)


def _gelu(x):
    t = jnp.tanh(_GELU_K * (x + 0.044715 * x * x * x))
    return 0.5 * x * (1.0 + t), t


def _gelu_grad(x, t):
    return 0.5 * (1.0 + t) + 0.5 * x * (1.0 - t * t) * _GELU_K * (1.0 + 3.0 * 0.044715 * x * x)


def _rope(x, cos_t, sin_t, rot):
    return x * cos_t + _dot_split(x, rot) * sin_t


def _ffn_fwd(h, g_pre, g_post, w_ffn, layer, kind_gate, kind_up, kind_dn, tm, name, job=None):
    tp, d = h.shape
    n_slot, fs = w_ffn.shape[0], w_ffn.shape[-2]
    n_tok = tp // tm
    n_ji = len(job.inputs) if job else 0
    n_jo = len(job.out_shape) if job else 0

    def body(h_ref, gpre_ref, gpost_ref, wg_ref, wu_ref, wd_ref, *rest):
        job_in, (hout_ref, gate_ref, up_ref, f_ref), rest = rest[:n_ji], rest[n_ji:n_ji + 4], rest[n_ji + 4:]
        job_out, (u_sc, acc_sc), sems = rest[:n_jo], rest[n_jo:n_jo + 2], rest[n_jo + 2:]
        j = pl.program_id(1)
        if job:
            pl.when((pl.program_id(0) == 0) & (j == 0))(lambda: job.start(job_in, job_out, sems))

        @pl.when(j == 0)
        def _():
            u, _ = _rms(h_ref[...], gpre_ref[...])
            u_sc[...] = u.astype(BF16)
            acc_sc[...] = jnp.zeros_like(acc_sc)

        u = u_sc[...]
        gate = _dot_nt(u, wg_ref[...])
        up = _dot_nt(u, wu_ref[...])
        act = gate * _sigmoid(gate) * up
        gate_ref[...] = gate.astype(BF16)
        up_ref[...] = up.astype(BF16)
        acc_sc[...] += jnp.dot(act.astype(BF16), wd_ref[...], preferred_element_type=F32)

        @pl.when(j == n_slot - 1)
        def _():
            f = acc_sc[...]
            f_ref[...] = f
            n, _ = _rms(f, gpost_ref[...])
            hout_ref[...] = h_ref[...] + 0.5 * n

        if job:
            pl.when((pl.program_id(0) == n_tok - 1) & (j == n_slot - 1))(lambda: job.finish(job_in, job_out, sems))

    tok = pl.BlockSpec((tm, d), lambda i, j: (i, 0))
    gain = pl.BlockSpec((None, 1, d), lambda i, j: (layer, 0, 0))
    slot_act = pl.BlockSpec((None, tm, fs), lambda i, j: (j, i, 0))
    outs = pl.pallas_call(
        body, name=name, grid=(n_tok, n_slot),
        in_specs=[tok, gain, gain] + [pl.BlockSpec((None, None, fs, d), lambda i, j, kind=kind: (j, kind, 0, 0))
                                      for kind in (kind_gate, kind_up, kind_dn)] + [_ANY] * n_ji,
        out_specs=[tok, slot_act, slot_act, tok] + [_ANY] * n_jo,
        out_shape=[jax.ShapeDtypeStruct((tp, d), F32), jax.ShapeDtypeStruct((n_slot, tp, fs), BF16),
                   jax.ShapeDtypeStruct((n_slot, tp, fs), BF16), jax.ShapeDtypeStruct((tp, d), F32)]
        + (list(job.out_shape) if job else []),
        scratch_shapes=[pltpu.VMEM((tm, d), BF16), pltpu.VMEM((tm, d), F32)] + (list(job.scratch) if job else []),
        compiler_params=_cparams(("arbitrary", "arbitrary") if job else ("parallel", "arbitrary")),
    )(h, g_pre, g_post, w_ffn, w_ffn, w_ffn, *(job.inputs if job else ()))
    return outs[0], outs[1], outs[2], outs[3], list(outs[4:])


def _ffn_bwd(dh_out, h, f, gate, up, g_pre, g_post, w_ffn, layer, kind_gate, kind_up, kind_dn, tm, name, job=None):
    tp, d = h.shape
    n_slot, fs = w_ffn.shape[0], w_ffn.shape[-2]
    n_tok = tp // tm
    n_ji = len(job.inputs) if job else 0
    n_jo = len(job.out_shape) if job else 0

    def body(dho_ref, h_ref, f_ref, gate_ref, up_ref, gpre_ref, gpost_ref, wg_ref, wu_ref, wd_ref, *rest):
        job_in, rest = rest[:n_ji], rest[n_ji:]
        (dh_ref, dgate_ref, dup_ref, act_ref, df_ref, u_ref, dgpre_ref, dgpost_ref), rest = rest[:8], rest[8:]
        job_out, du_sc, sems = rest[:n_jo], rest[n_jo], rest[n_jo + 1:]
        i, j = pl.program_id(0), pl.program_id(1)
        if job:
            pl.when((i == 0) & (j == 0))(lambda: job.start(job_in, job_out, sems))

        @pl.when((i == 0) & (j == 0))
        def _():
            dgpre_ref[...] = jnp.zeros_like(dgpre_ref)
            dgpost_ref[...] = jnp.zeros_like(dgpost_ref)

        @pl.when(j == 0)
        def _():
            df, dg = _rms_bwd(f_ref[...], gpost_ref[...], 0.5 * dho_ref[...])
            df_ref[...] = df.astype(BF16)
            dgpost_ref[...] += dg
            u, _ = _rms(h_ref[...], gpre_ref[...])
            u_ref[...] = u.astype(BF16)
            du_sc[...] = jnp.zeros_like(du_sc)

        g = gate_ref[...].astype(F32)
        u_ = up_ref[...].astype(F32)
        sg = _sigmoid(g)
        silu = g * sg
        dact = _dot_nt(df_ref[...], wd_ref[...])
        dup = dact * silu
        dgate = dact * u_ * (sg * (1.0 + g * (1.0 - sg)))
        act_ref[...] = (silu * u_).astype(BF16)
        dup_b = dup.astype(BF16)
        dgate_b = dgate.astype(BF16)
        dup_ref[...] = dup_b
        dgate_ref[...] = dgate_b
        du_sc[...] += _dot(dgate_b, wg_ref[...]) + _dot(dup_b, wu_ref[...])

        @pl.when(j == n_slot - 1)
        def _():
            dx, dg = _rms_bwd(h_ref[...], gpre_ref[...], du_sc[...])
            dh_ref[...] = dho_ref[...] + dx
            dgpre_ref[...] += dg

        if job:
            pl.when((i == n_tok - 1) & (j == n_slot - 1))(lambda: job.finish(job_in, job_out, sems))

    tok = pl.BlockSpec((tm, d), lambda i, j: (i, 0))
    gain = pl.BlockSpec((None, 1, d), lambda i, j: (layer, 0, 0))
    acc = pl.BlockSpec((1, d), lambda i, j: (0, 0))
    slot_act = pl.BlockSpec((None, tm, fs), lambda i, j: (j, i, 0))
    act_shape = jax.ShapeDtypeStruct((n_slot, tp, fs), BF16)
    outs = pl.pallas_call(
        body, name=name, grid=(n_tok, n_slot),
        in_specs=[tok, tok, tok, slot_act, slot_act, gain, gain]
        + [pl.BlockSpec((None, None, fs, d), lambda i, j, kind=kind: (j, kind, 0, 0)) for kind in (kind_gate, kind_up, kind_dn)]
        + [_ANY] * n_ji,
        out_specs=[tok, slot_act, slot_act, slot_act, tok, tok, acc, acc] + [_ANY] * n_jo,
        out_shape=[jax.ShapeDtypeStruct((tp, d), F32), act_shape, act_shape, act_shape,
                   jax.ShapeDtypeStruct((tp, d), BF16), jax.ShapeDtypeStruct((tp, d), BF16),
                   jax.ShapeDtypeStruct((1, d), F32), jax.ShapeDtypeStruct((1, d), F32)] + (list(job.out_shape) if job else []),
        scratch_shapes=[pltpu.VMEM((tm, d), F32)] + (list(job.scratch) if job else []),
        compiler_params=_cparams(("arbitrary", "arbitrary")),
    )(dh_out, h, f, gate, up, g_pre, g_post, w_ffn, w_ffn, w_ffn, *(job.inputs if job else ()))
    return tuple(outs[:8]) + (list(outs[8:]),)


def _mm_tn(a, b, tk, out_dtype, name, job=None):
    ga = a.shape[0] if a.ndim == 3 else None
    gb = b.shape[0] if b.ndim == 3 else None
    groups = ga or gb or 1
    t, m = a.shape[-2:]
    n = b.shape[-1]
    nk = t // tk
    n_ji = len(job.inputs) if job else 0
    n_jo = len(job.out_shape) if job else 0

    def body(a_ref, b_ref, *rest):
        job_in, o_ref, rest = rest[:n_ji], rest[n_ji], rest[n_ji + 1:]
        job_out, acc_sc, sems = rest[:n_jo], rest[n_jo], rest[n_jo + 1:]
        k = pl.program_id(1)
        if job:
            pl.when((pl.program_id(0) == 0) & (k == 0))(lambda: job.start(job_in, job_out, sems))

        @pl.when(k == 0)
        def _():
            acc_sc[...] = jnp.zeros_like(acc_sc)

        acc_sc[...] += _dot_tn(a_ref[...], b_ref[...])

        @pl.when(k == nk - 1)
        def _():
            o_ref[...] = acc_sc[...].astype(out_dtype)

        if job:
            pl.when((pl.program_id(0) == groups - 1) & (k == nk - 1))(lambda: job.finish(job_in, job_out, sems))

    a_spec = (pl.BlockSpec((None, tk, m), lambda g, k: (g, k, 0)) if ga else pl.BlockSpec((tk, m), lambda g, k: (k, 0)))
    b_spec = (pl.BlockSpec((None, tk, n), lambda g, k: (g, k, 0)) if gb else pl.BlockSpec((tk, n), lambda g, k: (k, 0)))
    outs = pl.pallas_call(
        body, name=name, grid=(groups, nk),
        in_specs=[a_spec, b_spec] + [_ANY] * n_ji,
        out_specs=[pl.BlockSpec((None, m, n), lambda g, k: (g, 0, 0))] + [_ANY] * n_jo,
        out_shape=[jax.ShapeDtypeStruct((groups, m, n), out_dtype)] + (list(job.out_shape) if job else []),
        scratch_shapes=[pltpu.VMEM((m, n), F32)] + (list(job.scratch) if job else []),
        compiler_params=_cparams(("arbitrary", "arbitrary") if job else ("parallel", "arbitrary")),
    )(a, b, *(job.inputs if job else ()))
    out = outs[0] if (ga or gb) else outs[0][0]
    return (out, list(outs[1:])) if job else out


def _mm_tn_heads(a, b, tk, name):
    groups, t, m = a.shape
    n = b.shape[-1]
    nk = t // tk

    def body(a_ref, b_ref, o_ref):
        k = pl.program_id(0)

        @pl.when(k == 0)
        def _():
            o_ref[...] = jnp.zeros_like(o_ref)

        bb = b_ref[...]
        for g in range(groups):
            o_ref[g] += _dot_tn(a_ref[g], bb)

    return pl.pallas_call(
        body, name=name, grid=(nk,),
        in_specs=[pl.BlockSpec((groups, tk, m), lambda k: (0, k, 0)), pl.BlockSpec((tk, n), lambda k: (k, 0))],
        out_specs=pl.BlockSpec((groups, m, n), lambda k: (0, 0, 0)),
        out_shape=jax.ShapeDtypeStruct((groups, m, n), F32),
        compiler_params=_cparams(("arbitrary",)),
    )(a, b)


def _mix_in_fwd(h, g_pre, w_parts, g_q, g_kv, w_q, w_k, w_v, cos_t, sin_t, rot, layer, tm, name):
    tp, d = h.shape
    w_xr, w_gr, w_cq, w_ckv, w_kr = w_parts
    lw, ql, kvl = w_xr.shape[0], w_cq.shape[0], w_ckv.shape[0]
    nh, vd = w_v.shape[0], w_v.shape[-1]

    def body(h_ref, gpre_ref, wxr_ref, wgr_ref, wcq_ref, wckv_ref, wkr_ref, gq_ref, gkv_ref, wq_ref, wk_ref, wv_ref,
             cos_ref, sin_ref, rot_ref, xr_ref, gr_ref, cq_ref, ckv_ref, q_ref, k_ref, v_ref):
        u, _ = _rms(h_ref[...], gpre_ref[...])
        u = u.astype(BF16)
        xr_ref[...] = _dot_nt(u, wxr_ref[...])
        gr_ref[...] = _dot_nt(u, wgr_ref[...])
        cq = _dot_nt(u, wcq_ref[...])
        ckv = _dot_nt(u, wckv_ref[...])
        kr = _dot_nt(u, wkr_ref[...])
        cq_ref[...] = cq
        ckv_ref[...] = ckv
        cos_b, sin_b, rot_b = cos_ref[...], sin_ref[...], rot_ref[...]
        qn = _rms(cq, gq_ref[...])[0].astype(BF16)
        kvn = _rms(ckv, gkv_ref[...])[0].astype(BF16)
        k_rope = _rope(kr, cos_b, sin_b, rot_b)
        for hd in range(nh):
            q_pre = _dot_nt(qn, wq_ref[hd])
            q_ref[hd] = (_rope(q_pre, cos_b, sin_b, rot_b) * Q_PRESCALE).astype(BF16)
            k_ref[hd] = (jnp.dot(kvn, wk_ref[hd], preferred_element_type=F32) + k_rope).astype(BF16)
            v_ref[hd] = jnp.dot(kvn, wv_ref[hd], preferred_element_type=F32).astype(BF16)

    def tok(n):
        return pl.BlockSpec((tm, n), lambda i: (i, 0))

    def lay(*shape):
        return pl.BlockSpec((None,) + shape, lambda i: (layer,) + (0,) * len(shape))

    def heads(n):
        return pl.BlockSpec((nh, tm, n), lambda i: (0, i, 0))

    return pl.pallas_call(
        body, name=name, grid=(tp // tm,),
        in_specs=[tok(d), lay(1, d), _whole(lw, d), _whole(lw, d), _whole(ql, d), _whole(kvl, d), _whole(QK_DIM, d),
                  lay(1, ql), lay(1, kvl),
                  _whole(nh, QK_DIM, ql), _whole(nh, kvl, QK_DIM), _whole(nh, kvl, vd), tok(QK_DIM), tok(QK_DIM),
                  pl.BlockSpec((QK_DIM, QK_DIM), lambda i: (0, 0))],
        out_specs=[tok(lw), tok(lw), tok(ql), tok(kvl), heads(QK_DIM), heads(QK_DIM), heads(vd)],
        out_shape=[jax.ShapeDtypeStruct((tp, lw), F32), jax.ShapeDtypeStruct((tp, lw), F32),
                   jax.ShapeDtypeStruct((tp, ql), F32), jax.ShapeDtypeStruct((tp, kvl), F32),
                   jax.ShapeDtypeStruct((nh, tp, QK_DIM), BF16), jax.ShapeDtypeStruct((nh, tp, QK_DIM), BF16),
                   jax.ShapeDtypeStruct((nh, tp, vd), BF16)],
        compiler_params=_cparams(("parallel",)),
    )(h, g_pre, w_xr, w_gr, w_cq, w_ckv, w_kr, g_q, g_kv, w_q, w_k, w_v, cos_t, sin_t, rot)


def _mix_in_bwd(dh_res, h, cq, ckv, d_xr, d_gr, dq, dk, dv, g_pre, w_parts, g_q, g_kv, w_q, w_k, w_v,
                cos_t, sin_t, rot_t, layer, tm, name):
    tp, d = h.shape
    w_xr, w_gr, w_cq, w_ckv, w_kr = w_parts
    lw, ql, kvl = w_xr.shape[0], w_cq.shape[0], w_ckv.shape[0]
    nh, vd = w_v.shape[0], w_v.shape[-1]

    def body(dhr_ref, h_ref, cq_ref, ckv_ref, dxr_ref, dgr_ref, dq_ref, dk_ref, dv_ref, gpre_ref,
             wxr_ref, wgr_ref, wcq_ref, wckv_ref, wkr_ref, gq_ref, gkv_ref, wq_ref, wk_ref, wv_ref,
             cos_ref, sin_ref, rott_ref,
             dh_ref, u_ref, qn_ref, kvn_ref, dqpre_ref, dcq_ref, dckv_ref, dkr_ref, dgpre_ref, dgq_ref, dgkv_ref):
        i = pl.program_id(0)

        @pl.when(i == 0)
        def _():
            dgpre_ref[...] = jnp.zeros_like(dgpre_ref)
            dgq_ref[...] = jnp.zeros_like(dgq_ref)
            dgkv_ref[...] = jnp.zeros_like(dgkv_ref)

        cos_b, sin_b, rott_b = cos_ref[...], sin_ref[...], rott_ref[...]
        hh = h_ref[...]
        u, _ = _rms(hh, gpre_ref[...])
        u_ref[...] = u.astype(BF16)
        cq, ckv = cq_ref[...], ckv_ref[...]
        qn = _rms(cq, gq_ref[...])[0]
        kvn = _rms(ckv, gkv_ref[...])[0]
        qn_ref[...] = qn.astype(BF16)
        kvn_ref[...] = kvn.astype(BF16)
        d_qn = jnp.zeros((tm, ql), F32)
        d_kvn = jnp.zeros((tm, kvl), F32)
        d_krope = jnp.zeros((tm, QK_DIM), F32)
        for hd in range(nh):
            dq_h = dq_ref[hd]
            dq_pre = dq_h * cos_b + _dot_split(dq_h * sin_b, rott_b)
            dq_pre_b = dq_pre.astype(BF16)
            dqpre_ref[hd] = dq_pre_b
            d_qn += _dot(dq_pre_b, wq_ref[hd])
            dk_h = dk_ref[hd]
            d_krope += dk_h
            d_kvn += _dot_nt(dk_h, wk_ref[hd]) + _dot_nt(dv_ref[hd], wv_ref[hd])
        d_kr = d_krope * cos_b + _dot_split(d_krope * sin_b, rott_b)
        d_cq, dgq = _rms_bwd(cq, gq_ref[...], d_qn)
        d_ckv, dgkv = _rms_bwd(ckv, gkv_ref[...], d_kvn)
        dgq_ref[...] += dgq
        dgkv_ref[...] += dgkv
        d_cq_b, d_ckv_b, d_kr_b = d_cq.astype(BF16), d_ckv.astype(BF16), d_kr.astype(BF16)
        dcq_ref[...] = d_cq_b
        dckv_ref[...] = d_ckv_b
        dkr_ref[...] = d_kr_b
        du = (_dot(dxr_ref[...], wxr_ref[...]) + _dot(dgr_ref[...], wgr_ref[...]) + _dot(d_cq_b, wcq_ref[...])
              + _dot(d_ckv_b, wckv_ref[...]) + _dot(d_kr_b, wkr_ref[...]))
        dx, dg = _rms_bwd(hh, gpre_ref[...], du)
        dh_ref[...] = dhr_ref[...] + dx
        dgpre_ref[...] += dg

    def tok(n):
        return pl.BlockSpec((tm, n), lambda i: (i, 0))

    def lay(*shape):
        return pl.BlockSpec((None,) + shape, lambda i: (layer,) + (0,) * len(shape))

    def heads(n):
        return pl.BlockSpec((nh, tm, n), lambda i: (0, i, 0))

    def acc(n):
        return pl.BlockSpec((1, n), lambda i: (0, 0))

    return pl.pallas_call(
        body, name=name, grid=(tp // tm,),
        in_specs=[tok(d), tok(d), tok(ql), tok(kvl), tok(lw), tok(lw), heads(QK_DIM), heads(QK_DIM), heads(vd), lay(1, d),
                  _whole(lw, d), _whole(lw, d), _whole(ql, d), _whole(kvl, d), _whole(QK_DIM, d), lay(1, ql), lay(1, kvl),
                  _whole(nh, QK_DIM, ql), _whole(nh, kvl, QK_DIM), _whole(nh, kvl, vd), tok(QK_DIM), tok(QK_DIM),
                  pl.BlockSpec((QK_DIM, QK_DIM), lambda i: (0, 0))],
        out_specs=[tok(d), tok(d), tok(ql), tok(kvl), heads(QK_DIM), tok(ql), tok(kvl), tok(QK_DIM), acc(d), acc(ql), acc(kvl)],
        out_shape=[jax.ShapeDtypeStruct((tp, d), F32), jax.ShapeDtypeStruct((tp, d), BF16),
                   jax.ShapeDtypeStruct((tp, ql), BF16), jax.ShapeDtypeStruct((tp, kvl), BF16),
                   jax.ShapeDtypeStruct((nh, tp, QK_DIM), BF16), jax.ShapeDtypeStruct((tp, ql), BF16),
                   jax.ShapeDtypeStruct((tp, kvl), BF16), jax.ShapeDtypeStruct((tp, QK_DIM), BF16),
                   jax.ShapeDtypeStruct((1, d), F32), jax.ShapeDtypeStruct((1, ql), F32), jax.ShapeDtypeStruct((1, kvl), F32)],
        compiler_params=_cparams(("arbitrary",)),
    )(dh_res, h, cq, ckv, d_xr, d_gr, dq, dk, dv, g_pre, w_xr, w_gr, w_cq, w_ckv, w_kr, g_q, g_kv, w_q, w_k, w_v,
      cos_t, sin_t, rot_t)


def _lru_gates(xc, wa_ref, wx_ref, ba, bx, sp):
    xcb = xc.astype(BF16)
    r = _sigmoid(jnp.dot(xcb, wa_ref[...], preferred_element_type=F32) + ba)
    ig = _sigmoid(jnp.dot(xcb, wx_ref[...], preferred_element_type=F32) + bx)
    log_a = -LRU_C * r * sp
    a = jnp.exp(log_a)
    sq = jnp.sqrt(_neg_expm1(2.0 * log_a))
    return r, ig, a, sq


def _softplus(x):
    return jnp.maximum(x, 0.0) + jnp.log(1.0 + jnp.exp(-jnp.abs(x)))


def _lru_fwd(xr, gr, conv_w, conv_b, wa_bd, wx_bd, b_a, b_x, lam, layer, tc, name):
    tp, w = xr.shape
    pad = 8

    def body(xr_ref, gr_ref, cw_ref, cb_ref, wa_ref, wx_ref, ba_ref, bx_ref, lam_ref, y_ref, xc_ref, hs_ref,
             xe_sc, a_sc, b_sc, st_sc):
        c = pl.program_id(0)

        @pl.when(c == 0)
        def _():
            xe_sc[pl.ds(0, pad), :] = jnp.zeros((pad, w), F32)
            st_sc[...] = jnp.zeros_like(st_sc)

        xe_sc[pl.ds(pad, tc), :] = xr_ref[...]
        xc = cb_ref[...] + xe_sc[pl.ds(pad, tc), :] * cw_ref[pl.ds(CONV_WIDTH - 1, 1), :]
        for k in range(CONV_WIDTH - 1):
            xc = xc + xe_sc[pl.ds(pad - (CONV_WIDTH - 1) + k, tc), :] * cw_ref[pl.ds(k, 1), :]
        xe_sc[pl.ds(0, pad), :] = xe_sc[pl.ds(tc, pad), :]
        xc_ref[...] = xc
        sp = _softplus(-lam_ref[...])
        _, ig, a, sq = _lru_gates(xc, wa_ref, wx_ref, ba_ref[...], bx_ref[...], sp)
        a_sc[...] = a
        b_sc[...] = sq * (ig * xc)

        def step(t, hcur):
            hnew = a_sc[pl.ds(t, 1), :] * hcur + b_sc[pl.ds(t, 1), :]
            hs_ref[pl.ds(t, 1), :] = hnew
            return hnew

        st_sc[...] = lax.fori_loop(0, tc, step, st_sc[...], unroll=8)
        y_ref[...] = hs_ref[...] * _gelu(gr_ref[...])[0]

    tok = pl.BlockSpec((tc, w), lambda c: (c, 0))

    def lay(*shape):
        return pl.BlockSpec((None,) + shape, lambda c: (layer,) + (0,) * len(shape))

    out = jax.ShapeDtypeStruct((tp, w), F32)
    return pl.pallas_call(
        body, name=name, grid=(tp // tc,),
        in_specs=[tok, tok, lay(CONV_WIDTH, w), lay(1, w), lay(w, w), lay(w, w), lay(1, w), lay(1, w), lay(1, w)],
        out_specs=[tok, tok, tok], out_shape=[out, out, out],
        scratch_shapes=[pltpu.VMEM((tc + pad, w), F32), pltpu.VMEM((tc, w), F32), pltpu.VMEM((tc, w), F32),
                        pltpu.VMEM((1, w), F32)],
        compiler_params=_cparams(("arbitrary",)),
    )(xr, gr, conv_w, conv_b, wa_bd, wx_bd, b_a, b_x, lam)


def _lru_bwd(dy, xr, gr, xc, hs, conv_w, wa_bd, wx_bd, b_a, b_x, lam, layer, tc, name):
    tp, w = xr.shape
    pad = 8
    nc = tp // tc
    per = tc // pad

    def body(dy_ref, xr_ref, gr_ref, xc_ref, hs_ref, xrp_ref, hsp_ref, cw_ref, wa_ref, wx_ref, ba_ref, bx_ref, lam_ref,
             dxr_ref, dgr_ref, dcw_ref, dcb_ref, dwa_ref, dwx_ref, dba_ref, dbx_ref, dlam_ref,
             a_sc, dh_sc, dxc_sc, he_sc, xe_sc, carry_sc):
        s = pl.program_id(0)
        first_chunk = s == nc - 1

        @pl.when(s == 0)
        def _():
            for ref in (dcw_ref, dcb_ref, dwa_ref, dwx_ref, dba_ref, dbx_ref, dlam_ref):
                ref[...] = jnp.zeros_like(ref)
            carry_sc[...] = jnp.zeros_like(carry_sc)
            dxc_sc[pl.ds(tc, pad), :] = jnp.zeros((pad, w), F32)

        keep = jnp.where(first_chunk, 0.0, 1.0)
        he_sc[pl.ds(0, pad), :] = hsp_ref[...] * keep
        he_sc[pl.ds(pad, tc), :] = hs_ref[...]
        xe_sc[pl.ds(0, pad), :] = xrp_ref[...] * keep
        xe_sc[pl.ds(pad, tc), :] = xr_ref[...]

        lam_v = lam_ref[...]
        sp = _softplus(-lam_v)
        xc = xc_ref[...]
        r, ig, a, sq = _lru_gates(xc, wa_ref, wx_ref, ba_ref[...], bx_ref[...], sp)
        a_sc[...] = a
        grv = gr_ref[...]
        gl, th = _gelu(grv)
        dyv = dy_ref[...]
        dgr_ref[...] = dyv * hs_ref[...] * _gelu_grad(grv, th)
        dh_sc[...] = dyv * gl

        def step(n, g):
            t = tc - 1 - n
            dh = dh_sc[pl.ds(t, 1), :] + g
            dh_sc[pl.ds(t, 1), :] = dh
            return a_sc[pl.ds(t, 1), :] * dh

        carry_sc[...] = lax.fori_loop(0, tc, step, carry_sc[...], unroll=8)

        dh = dh_sc[...]
        d_a = dh * he_sc[pl.ds(pad - 1, tc), :]
        d_ixc = dh * sq
        d_sq = dh * (ig * xc)
        a2 = a * a
        d_la = d_a * a - d_sq * a2 / sq
        d_r = d_la * (-LRU_C * sp)
        d_sp = jnp.sum(d_la * (-LRU_C * r), axis=0, keepdims=True)
        dlam_ref[...] += d_sp * (-_sigmoid(-lam_v))
        d_pa = d_r * r * (1.0 - r)
        d_px = d_ixc * xc * ig * (1.0 - ig)
        dba_ref[...] += jnp.sum(d_pa, axis=0, keepdims=True)
        dbx_ref[...] += jnp.sum(d_px, axis=0, keepdims=True)
        d_pa_b, d_px_b = d_pa.astype(BF16), d_px.astype(BF16)
        xcb = xc.astype(BF16)
        dwa_ref[...] += _dot_tn(xcb, d_pa_b)
        dwx_ref[...] += _dot_tn(xcb, d_px_b)
        d_xc = d_ixc * ig + _dot_nt(d_pa_b, wa_ref[...]) + _dot_nt(d_px_b, wx_ref[...])
        dcb_ref[...] += jnp.sum(d_xc, axis=0, keepdims=True)
        dxc_sc[pl.ds(0, tc), :] = d_xc
        d_xr = jnp.zeros((tc, w), F32)
        for k in range(CONV_WIDTH):
            off = CONV_WIDTH - 1 - k
            d_xr = d_xr + dxc_sc[pl.ds(off, tc), :] * cw_ref[pl.ds(k, 1), :]
            dcw_ref[pl.ds(k, 1), :] += jnp.sum(d_xc * xe_sc[pl.ds(pad - off, tc), :], axis=0, keepdims=True)
        dxr_ref[...] = d_xr
        dxc_sc[pl.ds(tc, pad), :] = dxc_sc[pl.ds(0, pad), :]

    def rev(cidx):
        return nc - 1 - cidx

    tok = pl.BlockSpec((tc, w), lambda c: (rev(c), 0))
    prev = pl.BlockSpec((pad, w), lambda c: (jnp.maximum(rev(c) * per - 1, 0), 0))

    def lay(*shape):
        return pl.BlockSpec((None,) + shape, lambda c: (layer,) + (0,) * len(shape))

    def acc(*shape):
        return pl.BlockSpec(shape, lambda c: (0,) * len(shape))

    big = jax.ShapeDtypeStruct((tp, w), F32)
    vec = jax.ShapeDtypeStruct((1, w), F32)
    return pl.pallas_call(
        body, name=name, grid=(nc,),
        in_specs=[tok, tok, tok, tok, tok, prev, prev, lay(CONV_WIDTH, w), lay(w, w), lay(w, w), lay(1, w), lay(1, w), lay(1, w)],
        out_specs=[tok, tok, acc(CONV_WIDTH, w), acc(1, w), acc(w, w), acc(w, w), acc(1, w), acc(1, w), acc(1, w)],
        out_shape=[big, big, jax.ShapeDtypeStruct((CONV_WIDTH, w), F32), vec, jax.ShapeDtypeStruct((w, w), F32),
                   jax.ShapeDtypeStruct((w, w), F32), vec, vec, vec],
        scratch_shapes=[pltpu.VMEM((tc, w), F32), pltpu.VMEM((tc, w), F32), pltpu.VMEM((tc + pad, w), F32),
                        pltpu.VMEM((tc + pad, w), F32), pltpu.VMEM((tc + pad, w), F32), pltpu.VMEM((1, w), F32)],
        compiler_params=_cparams(("arbitrary",)),
    )(dy, xr, gr, xc, hs, xr, hs, conv_w, wa_bd, wx_bd, b_a, b_x, lam)


def _attn_fwd(q, k, v, tq, name, job=None):
    nh, tp, dqk = q.shape
    vd = v.shape[-1]
    nq = tp // tq
    n_ji = len(job.inputs) if job else 0
    n_jo = len(job.out_shape) if job else 0

    def body(q_ref, k_ref, v_ref, *rest):
        job_in, (o_ref, lse_ref), rest = rest[:n_ji], rest[n_ji:n_ji + 2], rest[n_ji + 2:]
        job_out, sems = rest[:n_jo], rest[n_jo:]
        i = pl.program_id(1)
        if job:
            first = (pl.program_id(0) == 0) & (i == 0)
            pl.when(first)(lambda: job.start(job_in, job_out, sems))
        qb = q_ref[...]

        def absorb(j, width, carry, diagonal=False):
            m, l, acc = carry
            off = pl.multiple_of(j * tq, tq)
            s = _dot_nt(qb, k_ref[pl.ds(off, width), :])
            if diagonal:
                keep = lax.broadcasted_iota(jnp.int32, (tq, width), 1) <= lax.broadcasted_iota(jnp.int32, (tq, width), 0)
                s = jnp.where(keep, s, NEG_BIG)
            m_new = jnp.maximum(m, jnp.max(s, axis=-1, keepdims=True))
            p = jnp.exp2(s - m_new)
            alpha = jnp.exp2(m - m_new)
            l = alpha * l + jnp.sum(p, axis=-1, keepdims=True)
            acc = alpha * acc + jnp.dot(p.astype(BF16), v_ref[pl.ds(off, width), :], preferred_element_type=F32)
            return m_new, l, acc

        carry = (jnp.full((tq, 1), NEG_BIG, F32), jnp.zeros((tq, 1), F32), jnp.zeros((tq, vd), F32))
        quads = lax.shift_right_logical(i, 2)
        carry = lax.fori_loop(0, quads, lambda jj, c: absorb(4 * jj, 4 * tq, c), carry)
        carry = lax.cond((i & 2) != 0, lambda c: absorb(4 * quads, 2 * tq, c), lambda c: c, carry)
        carry = lax.cond((i & 1) != 0, lambda c: absorb(i - 1, tq, c), lambda c: c, carry)
        m, l, acc = absorb(i, tq, carry, diagonal=True)
        o_ref[...] = acc / l
        lse_ref[...] = m + jnp.log2(l)
        if job:
            last = (pl.program_id(0) == nh - 1) & (i == nq - 1)
            pl.when(last)(lambda: job.finish(job_in, job_out, sems))

    outs = pl.pallas_call(
        body, name=name, grid=(nh, nq),
        in_specs=[pl.BlockSpec((None, tq, dqk), lambda h, i: (h, i, 0)),
                  pl.BlockSpec((None, tp, dqk), lambda h, i: (h, 0, 0)),
                  pl.BlockSpec((None, tp, vd), lambda h, i: (h, 0, 0))] + [_ANY] * n_ji,
        out_specs=[pl.BlockSpec((None, tq, vd), lambda h, i: (h, i, 0)),
                   pl.BlockSpec((None, tq, 1), lambda h, i: (h, i, 0))] + [_ANY] * n_jo,
        out_shape=[jax.ShapeDtypeStruct((nh, tp, vd), F32), jax.ShapeDtypeStruct((nh, tp, 1), F32)]
        + (list(job.out_shape) if job else []),
        scratch_shapes=list(job.scratch) if job else [],
        compiler_params=_cparams(("arbitrary", "arbitrary") if job else ("parallel", "parallel")),
    )(q, k, v, *(job.inputs if job else ()))
    return outs[0], outs[1], list(outs[2:])


def _attn_bwd(q, k, v, o, do, lse, tq, name, job=None):
    nh, tp, dqk = q.shape
    vd = v.shape[-1]
    scale = dqk ** -0.5
    nq = tp // tq
    n_ji = len(job.inputs) if job else 0
    n_jo = len(job.out_shape) if job else 0

    def body(q_ref, k_ref, v_ref, o_ref, do_ref, lse_ref, *rest):
        job_in, (dq_ref, dk_ref, dv_ref), rest = rest[:n_ji], rest[n_ji:n_ji + 3], rest[n_ji + 3:]
        job_out, (lse_rows, delta_rows, do_bf, dqt_sc), sems = rest[:n_jo], rest[n_jo:n_jo + 4], rest[n_jo + 4:]
        kb_i = pl.program_id(1)
        if job:
            first = (pl.program_id(0) == 0) & (kb_i == 0)
            pl.when(first)(lambda: job.start(job_in, job_out, sems))

        def as_rows(col):
            return jnp.transpose(jnp.broadcast_to(col, (tq, _LANES)))[0:8, :]

        @pl.when(kb_i == 0)
        def _():
            dqt_sc[...] = jnp.zeros_like(dqt_sc)

            def prep(qi, _):
                off = pl.multiple_of(qi * tq, tq)
                dob = do_ref[pl.ds(off, tq), :]
                do_bf[pl.ds(off, tq), :] = dob.astype(BF16)
                delta_rows[qi] = as_rows(jnp.sum(dob * o_ref[pl.ds(off, tq), :], axis=-1, keepdims=True))
                lse_rows[qi] = as_rows(lse_ref[pl.ds(off, tq), :])
                return 0

            lax.fori_loop(0, nq, prep, 0)

        kb = k_ref[...]
        vb = v_ref[...]

        def absorb(qi, n_blk, carry, diagonal=False):
            dk, dv = carry
            off = pl.multiple_of(qi * tq, tq)
            qb = q_ref[pl.ds(off, n_blk * tq), :]
            dob = do_bf[pl.ds(off, n_blk * tq), :]
            lse_row = jnp.concatenate([lse_rows[qi + b, 0:1, :] for b in range(n_blk)], axis=1)
            delta_row = jnp.concatenate([delta_rows[qi + b, 0:1, :] for b in range(n_blk)], axis=1)
            st = _dot_nt(kb, qb)
            if diagonal:
                keep = lax.broadcasted_iota(jnp.int32, (tq, tq), 0) <= lax.broadcasted_iota(jnp.int32, (tq, tq), 1)
                st = jnp.where(keep, st, NEG_BIG)
            pt = jnp.exp2(st - lse_row)
            dpt = _dot_nt(vb, dob)
            dst = (pt * (dpt - delta_row)).astype(BF16)
            dv = dv + jnp.dot(pt.astype(BF16), dob, preferred_element_type=F32)
            dk = dk + jnp.dot(dst, qb, preferred_element_type=F32)
            dqt = _dot_tn(kb, dst)
            for b in range(n_blk):
                dqt_sc[qi + b] += dqt[:, b * tq:(b + 1) * tq]
            return dk, dv

        carry = absorb(kb_i, 1, (jnp.zeros((tq, dqk), F32), jnp.zeros((tq, vd), F32)), diagonal=True)
        later = nq - 1 - kb_i
        quads = lax.shift_right_logical(later, 2)
        carry = lax.fori_loop(0, quads, lambda pp, c: absorb(kb_i + 1 + 4 * pp, 4, c), carry)
        carry = lax.cond((later & 2) != 0, lambda c: absorb(kb_i + 1 + 4 * quads, 2, c), lambda c: c, carry)
        dk, dv = lax.cond((later & 1) != 0, lambda c: absorb(nq - 1, 1, c), lambda c: c, carry)
        dk_ref[...] = dk * math.log(2.0)
        dv_ref[...] = dv

        @pl.when(kb_i == nq - 1)
        def _():
            eye = (lax.broadcasted_iota(jnp.int32, (dqk, dqk), 0) == lax.broadcasted_iota(jnp.int32, (dqk, dqk), 1))
            eye = jnp.where(eye, 1.0, 0.0).astype(BF16)
            for qi in range(nq):
                t = dqt_sc[qi]
                hi = t.astype(BF16)
                lo = (t - hi.astype(F32)).astype(BF16)
                dq_ref[pl.ds(qi * tq, tq), :] = scale * (_dot_tn(hi, eye) + _dot_tn(lo, eye))

        if job:
            last = (pl.program_id(0) == nh - 1) & (kb_i == nq - 1)
            pl.when(last)(lambda: job.finish(job_in, job_out, sems))

    def full(n):
        return pl.BlockSpec((None, tp, n), lambda h, j: (h, 0, 0))

    def blk(n):
        return pl.BlockSpec((None, tq, n), lambda h, j: (h, j, 0))

    outs = pl.pallas_call(
        body, name=name, grid=(nh, nq),
        in_specs=[full(dqk), blk(dqk), blk(vd), full(vd), full(vd), full(1)] + [_ANY] * n_ji,
        out_specs=[full(dqk), blk(dqk), blk(vd)] + [_ANY] * n_jo,
        out_shape=[jax.ShapeDtypeStruct((nh, tp, dqk), F32), jax.ShapeDtypeStruct((nh, tp, dqk), F32),
                   jax.ShapeDtypeStruct((nh, tp, vd), F32)] + (list(job.out_shape) if job else []),
        scratch_shapes=[pltpu.VMEM((nq, 8, tq), F32), pltpu.VMEM((nq, 8, tq), F32), pltpu.VMEM((tp, vd), BF16),
                        pltpu.VMEM((nq, dqk, tq), F32)] + (list(job.scratch) if job else []),
        compiler_params=_cparams(("arbitrary", "arbitrary") if job else ("parallel", "arbitrary")),
    )(q, k, v, o, do, lse, *(job.inputs if job else ()))
    return outs[0], outs[1], outs[2], list(outs[3:])


def _mix_out_fwd(h, y_lru, o, g_lo, g_mo, w_lo, w_mo, g_post, layer, tm, name):
    tp, d = h.shape
    lw = y_lru.shape[-1]
    nh, vd = o.shape[0], o.shape[-1]

    def body(h_ref, yl_ref, o_ref, glo_ref, gmo_ref, wlo_ref, wmo_ref, gpost_ref, hout_ref, y_ref):
        nl, _ = _rms(yl_ref[...], glo_ref[...])
        y = _dot(nl, wlo_ref[...])
        ssq = jnp.zeros((tm, 1), F32)
        for hd in range(nh):
            oh = o_ref[hd]
            ssq += jnp.sum(oh * oh, axis=-1, keepdims=True)
        r = lax.rsqrt(ssq / (nh * vd) + EPS)
        for hd in range(nh):
            y += _dot(o_ref[hd] * r * gmo_ref[hd], wmo_ref[hd])
        y_ref[...] = y
        hout_ref[...] = h_ref[...] + _rms(y, gpost_ref[...])[0]

    tok = pl.BlockSpec((tm, d), lambda i: (i, 0))

    def lay(*shape):
        return pl.BlockSpec((None,) + shape, lambda i: (layer,) + (0,) * len(shape))

    return pl.pallas_call(
        body, name=name, grid=(tp // tm,),
        in_specs=[tok, pl.BlockSpec((tm, lw), lambda i: (i, 0)), pl.BlockSpec((nh, tm, vd), lambda i: (0, i, 0)),
                  lay(1, lw), lay(nh, 1, vd), _whole(lw, d), _whole(nh, vd, d), lay(1, d)],
        out_specs=[tok, tok],
        out_shape=[jax.ShapeDtypeStruct((tp, d), F32), jax.ShapeDtypeStruct((tp, d), F32)],
        compiler_params=_cparams(("parallel",)),
    )(h, y_lru, o, g_lo, g_mo, w_lo, w_mo, g_post)


def _mix_out_bwd(dh_out, y, y_lru, o, g_lo, g_mo, w_lo, w_mo, g_post, layer, tm, name):
    tp, d = y.shape
    lw = y_lru.shape[-1]
    nh, vd = o.shape[0], o.shape[-1]

    def body(dho_ref, y_ref, yl_ref, o_ref, glo_ref, gmo_ref, wlo_ref, wmo_ref, gpost_ref,
             dyl_ref, do_ref, dy_ref, nl_ref, nm_ref, dgpost_ref, dglo_ref, dgmo_ref):
        i = pl.program_id(0)

        @pl.when(i == 0)
        def _():
            dgpost_ref[...] = jnp.zeros_like(dgpost_ref)
            dglo_ref[...] = jnp.zeros_like(dglo_ref)
            dgmo_ref[...] = jnp.zeros_like(dgmo_ref)

        dy, dgp = _rms_bwd(y_ref[...], gpost_ref[...], dho_ref[...])
        dgpost_ref[...] += dgp
        dy_b = dy.astype(BF16)
        dy_ref[...] = dy_b
        yl = yl_ref[...]
        nl_ref[...] = _rms(yl, glo_ref[...])[0].astype(BF16)
        dyl, dgl = _rms_bwd(yl, glo_ref[...], _dot_nt(dy_b, wlo_ref[...]))
        dyl_ref[...] = dyl
        dglo_ref[...] += dgl
        ssq = jnp.zeros((tm, 1), F32)
        for hd in range(nh):
            oh = o_ref[hd]
            ssq += jnp.sum(oh * oh, axis=-1, keepdims=True)
        r = lax.rsqrt(ssq / (nh * vd) + EPS)
        dn = []
        dot_sum = jnp.zeros((tm, 1), F32)
        for hd in range(nh):
            oh = o_ref[hd] * r
            nm_ref[hd] = (oh * gmo_ref[hd]).astype(BF16)
            dn_h = _dot_nt(dy_b, wmo_ref[hd])
            dgmo_ref[hd] += jnp.sum(dn_h * oh, axis=0, keepdims=True)
            dn_h = dn_h * gmo_ref[hd]
            dot_sum += jnp.sum(dn_h * oh, axis=-1, keepdims=True)
            dn.append(dn_h)
        dot_mean = dot_sum / (nh * vd)
        for hd in range(nh):
            do_ref[hd] = r * (dn[hd] - o_ref[hd] * r * dot_mean)

    tok = pl.BlockSpec((tm, d), lambda i: (i, 0))
    tokl = pl.BlockSpec((tm, lw), lambda i: (i, 0))
    heads = pl.BlockSpec((nh, tm, vd), lambda i: (0, i, 0))

    def lay(*shape):
        return pl.BlockSpec((None,) + shape, lambda i: (layer,) + (0,) * len(shape))

    def acc(*shape):
        return pl.BlockSpec(shape, lambda i: (0,) * len(shape))

    return pl.pallas_call(
        body, name=name, grid=(tp // tm,),
        in_specs=[tok, tok, tokl, heads, lay(1, lw), lay(nh, 1, vd), _whole(lw, d), _whole(nh, vd, d), lay(1, d)],
        out_specs=[tokl, heads, tok, tokl, heads, acc(1, d), acc(1, lw), acc(nh, 1, vd)],
        out_shape=[jax.ShapeDtypeStruct((tp, lw), F32), jax.ShapeDtypeStruct((nh, tp, vd), F32),
                   jax.ShapeDtypeStruct((tp, d), BF16), jax.ShapeDtypeStruct((tp, lw), BF16),
                   jax.ShapeDtypeStruct((nh, tp, vd), BF16), jax.ShapeDtypeStruct((1, d), F32),
                   jax.ShapeDtypeStruct((1, lw), F32), jax.ShapeDtypeStruct((nh, 1, vd), F32)],
        compiler_params=_cparams(("arbitrary",)),
    )(dh_out, y, y_lru, o, g_lo, g_mo, w_lo, w_mo, g_post)


def _loss_head(h, target, n_real, tm, name):
    tp, d = h.shape
    nt = tp // tm

    def body(h_ref, t_ref, dh_ref, part_ref):
        i = pl.program_id(0)
        row = i * tm + lax.broadcasted_iota(jnp.int32, (tm, 1), 0)
        real = (row >= N_META) & (row < N_META + n_real)
        err = jnp.where(real, h_ref[...] - t_ref[...], 0.0)
        dh_ref[...] = err * (1.0 / d)
        part_ref[...] = jnp.full((1, 1, 128), 0.5 / d * jnp.sum(err * err), F32)

    tok = pl.BlockSpec((tm, d), lambda i: (i, 0))
    return pl.pallas_call(
        body, name=name, grid=(nt,), in_specs=[tok, tok],
        out_specs=[tok, pl.BlockSpec((1, 1, 128), lambda i: (i, 0, 0))],
        out_shape=[jax.ShapeDtypeStruct((tp, d), F32), jax.ShapeDtypeStruct((nt, 1, 128), F32)],
        compiler_params=_cparams(("parallel",)),
    )(h, target)


def _mesh_pos():
    return lax.axis_index("x"), lax.axis_index("y"), lax.axis_index("c")


def _other_chips(x, y):
    return [(1 - x, y), (x, 1 - y), (1 - x, 1 - y)]


class _GatherJob:
    def __init__(self, inputs, picks, shard_shapes, dtypes, split_last=()):
        self.inputs = list(inputs)
        self.picks = list(picks)
        n = len(self.inputs)
        self.out_shape = [jax.ShapeDtypeStruct((N_CHIPS,) + tuple(s), dt) for s, dt in zip(shard_shapes, dtypes)]
        self.split_last = [a in split_last for a in range(n)]
        self.halves = [(s[1] if second else s[0]) // 2 for s, second in zip(shard_shapes, self.split_last)]
        self.scratch = [pltpu.SemaphoreType.DMA((n, 3)), pltpu.SemaphoreType.DMA((n, 3)), pltpu.SemaphoreType.DMA((n, 3)),
                        pltpu.SemaphoreType.DMA((n, 3)), pltpu.SemaphoreType.DMA((n,))]

    def _copies(self, ins, outs, sems):
        ici_s, ici_r, core_s, core_r, loc = sems
        x, y, c = _mesh_pos()
        me = 2 * x + y
        chips = _other_chips(x, y)
        local, ici, fwd = [], [], []
        for a in range(len(ins)):
            src = self.picks[a](ins[a])
            hv = self.halves[a]

            def half(ref, which, a=a, hv=hv):
                return ref.at[:, pl.ds(which * hv, hv)] if self.split_last[a] else ref.at[pl.ds(which * hv, hv)]

            local.append(pltpu.make_async_copy(src, outs[a].at[me], loc.at[a]))
            for p, (px, py) in enumerate(chips):
                ici.append((pltpu.make_async_remote_copy(src_ref=half(src, c), dst_ref=half(outs[a].at[me], c),
                                                         send_sem=ici_s.at[a, p], recv_sem=ici_r.at[a, p],
                                                         device_id=(px, py, c), device_id_type=MESH),
                            pltpu.make_async_remote_copy(src_ref=half(src, c), dst_ref=half(outs[a].at[2 * px + py], c),
                                                         send_sem=ici_s.at[a, p], recv_sem=ici_r.at[a, p],
                                                         device_id=(px, py, c), device_id_type=MESH)))
                landed = half(outs[a].at[2 * px + py], c)
                theirs = half(outs[a].at[2 * px + py], 1 - c)
                fwd.append((pltpu.make_async_remote_copy(src_ref=landed, dst_ref=landed, send_sem=core_s.at[a, p],
                                                         recv_sem=core_r.at[a, p], device_id=(x, y, 1 - c), device_id_type=MESH),
                            pltpu.make_async_remote_copy(src_ref=theirs, dst_ref=theirs, send_sem=core_s.at[a, p],
                                                         recv_sem=core_r.at[a, p], device_id=(x, y, 1 - c), device_id_type=MESH)))
        return local, ici, fwd

    def start(self, ins, outs, sems):
        local, ici, _ = self._copies(ins, outs, sems)
        for cp in local:
            cp.start()
        for send, _ in ici:
            send.start()

    def finish(self, ins, outs, sems):
        local, ici, fwd = self._copies(ins, outs, sems)
        for (_, arrive), (hand_on, _) in zip(ici, fwd):
            arrive.wait_recv()
            hand_on.start()
        for _, arrive in fwd:
            arrive.wait_recv()
        for send, _ in ici:
            send.wait_send()
        for hand_on, _ in fwd:
            hand_on.wait_send()
        for cp in local:
            cp.wait()


class _ReduceJob:
    def __init__(self, inputs):
        self.inputs = list(inputs)
        n = len(self.inputs)
        self.out_shape = [jax.ShapeDtypeStruct(a.shape, a.dtype) for a in self.inputs] * 2
        self.scratch = [pltpu.SemaphoreType.DMA((n, 3)), pltpu.SemaphoreType.DMA((n, 3)), pltpu.SemaphoreType.DMA((n, 4)),
                        pltpu.SemaphoreType.DMA((n, 4)), pltpu.SemaphoreType.DMA((n,))]

    def _copies(self, ins, outs, sems):
        ici_s, ici_r, core_s, core_r, loc = sems
        n = len(ins)
        same, other = outs[:n], outs[n:]
        x, y, c = _mesh_pos()
        me = 2 * x + y
        chips = _other_chips(x, y)
        sib = (x, y, 1 - c)
        local, ici, fwd = [], [], []
        for a in range(n):
            local.append(pltpu.make_async_copy(ins[a].at[me], same[a].at[me], loc.at[a]))
            fwd.append((pltpu.make_async_remote_copy(src_ref=ins[a].at[me], dst_ref=other[a].at[me], send_sem=core_s.at[a, 3],
                                                     recv_sem=core_r.at[a, 3], device_id=sib, device_id_type=MESH), None))
            for p, (px, py) in enumerate(chips):
                cp = 2 * px + py
                ici.append((pltpu.make_async_remote_copy(src_ref=ins[a].at[cp], dst_ref=same[a].at[me], send_sem=ici_s.at[a, p],
                                                         recv_sem=ici_r.at[a, p], device_id=(px, py, c), device_id_type=MESH),
                            pltpu.make_async_remote_copy(src_ref=ins[a].at[cp], dst_ref=same[a].at[cp], send_sem=ici_s.at[a, p],
                                                         recv_sem=ici_r.at[a, p], device_id=(px, py, c), device_id_type=MESH)))
                fwd.append((pltpu.make_async_remote_copy(src_ref=same[a].at[cp], dst_ref=other[a].at[cp], send_sem=core_s.at[a, p],
                                                         recv_sem=core_r.at[a, p], device_id=sib, device_id_type=MESH), p))
        return local, ici, fwd

    def start(self, ins, outs, sems):
        local, ici, fwd = self._copies(ins, outs, sems)
        for cp in local:
            cp.start()
        for send, _ in ici:
            send.start()
        for hand_on, p in fwd:
            if p is None:
                hand_on.start()

    def finish(self, ins, outs, sems):
        local, ici, fwd = self._copies(ins, outs, sems)
        chip_fwd = [f for f in fwd if f[1] is not None]
        for (_, arrive), (hand_on, _) in zip(ici, chip_fwd):
            arrive.wait_recv()
            hand_on.start()
        for hand_on, _ in fwd:
            hand_on.wait()
        for send, _ in ici:
            send.wait_send()
        for cp in local:
            cp.wait()


class _DeviceGatherJob:
    def __init__(self, buf):
        self.inputs = [buf]
        self.out_shape = [jax.ShapeDtypeStruct((8,) + buf.shape, buf.dtype)]
        self.scratch = [pltpu.SemaphoreType.DMA((7,)), pltpu.SemaphoreType.DMA((7,)), pltpu.SemaphoreType.DMA]

    def _copies(self, ins, outs, sems):
        send_s, recv_s, loc = sems
        x, y, c = _mesh_pos()
        me = 4 * x + 2 * y + c
        local = pltpu.make_async_copy(ins[0], outs[0].at[me], loc)
        remote = []
        for rel in range(1, 8):
            fx, fy, fc = (rel >> 2) & 1, (rel >> 1) & 1, rel & 1
            peer = (1 - x if fx else x, 1 - y if fy else y, 1 - c if fc else c)
            remote.append(pltpu.make_async_remote_copy(src_ref=ins[0], dst_ref=outs[0].at[me], send_sem=send_s.at[rel - 1],
                                                       recv_sem=recv_s.at[rel - 1], device_id=peer, device_id_type=MESH))
        return local, remote

    def start(self, ins, outs, sems):
        local, remote = self._copies(ins, outs, sems)
        local.start()
        for rc in remote:
            rc.start()

    def finish(self, ins, outs, sems):
        local, remote = self._copies(ins, outs, sems)
        for rc in remote:
            rc.wait()
        local.wait()


class _JobGroup:
    def __init__(self, jobs):
        self.jobs = list(jobs)
        self.inputs = [a for j in self.jobs for a in j.inputs]
        self.out_shape = [s for j in self.jobs for s in j.out_shape]
        self.scratch = [s for j in self.jobs for s in j.scratch]
        self.n_in = [len(j.inputs) for j in self.jobs]
        self.n_out = [len(j.out_shape) for j in self.jobs]
        self.n_sem = [len(j.scratch) for j in self.jobs]

    def _parts(self, k, ins, outs, sems):
        return (ins[sum(self.n_in[:k]):sum(self.n_in[:k + 1])], outs[sum(self.n_out[:k]):sum(self.n_out[:k + 1])],
                sems[sum(self.n_sem[:k]):sum(self.n_sem[:k + 1])])

    def start(self, ins, outs, sems):
        for k, job in enumerate(self.jobs):
            job.start(*self._parts(k, ins, outs, sems))

    def finish(self, ins, outs, sems):
        for k, job in enumerate(self.jobs):
            job.finish(*self._parts(k, ins, outs, sems))

    def split(self, outs):
        return [list(outs[sum(self.n_out[:k]):sum(self.n_out[:k + 1])]) for k in range(len(self.jobs))]


def _run_jobs(jobs, name):
    group = _JobGroup(jobs)
    n_in, n_out = len(group.inputs), len(group.out_shape)

    def body(*refs):
        ins, outs, sems = refs[:n_in], refs[n_in:n_in + n_out], refs[n_in + n_out:]
        group.start(ins, outs, sems)
        group.finish(ins, outs, sems)

    res = pl.pallas_call(body, name=name, in_specs=[_ANY] * n_in, out_specs=[_ANY] * n_out, out_shape=list(group.out_shape),
                         scratch_shapes=list(group.scratch))(*group.inputs)
    return group.split(res)


def _sum_devices(parts, name):
    n_dev, rows, lanes = parts.shape

    def body(p_ref, o_ref):
        total = p_ref[0]
        for k in range(1, n_dev):
            total = total + p_ref[k]
        o_ref[...] = total

    return pl.pallas_call(body, name=name, out_shape=jax.ShapeDtypeStruct((rows, lanes), F32),
                          compiler_params=pltpu.CompilerParams(vmem_limit_bytes=V7X_VMEM_LIMIT_BYTES))(parts)


def _row_tile(rows, limit=512):
    best = None
    for t in range(16, min(rows, limit) + 1, 16):
        if rows % t == 0:
            best = t
    return best if best is not None else rows


def _adamw_math(w, g, m, v):
    m = ADAM_B1 * m + (1.0 - ADAM_B1) * g
    v = ADAM_B2 * v + (1.0 - ADAM_B2) * (g * g)
    m_hat = m / (1.0 - ADAM_B1 ** ADAM_STEP)
    v_hat = v / (1.0 - ADAM_B2 ** ADAM_STEP)
    delta = -ADAM_LR * (m_hat / (jnp.sqrt(v_hat) + ADAM_EPS) + ADAM_WD * w)
    return delta, m, v


def _adamw_layer(w, m, v, same_plane, other_plane, layer, prev, name):
    nl, r, ccol = w.shape
    ns = same_plane.shape[0]
    tr = _row_tile(r, 512)

    def body(w_ref, m_ref, v_ref, a_ref, b_ref, *rest):
        go_ref, d_ref, mo_ref, vo_ref = rest[-4:]
        sa = a_ref[0].astype(F32)
        sb = b_ref[0].astype(F32)
        for k in range(1, ns):
            sa = sa + a_ref[k].astype(F32)
            sb = sb + b_ref[k].astype(F32)
        g = sa + sb
        go_ref[...] = g
        d_ref[...], mo_ref[...], vo_ref[...] = _adamw_math(w_ref[...], g, m_ref[...], v_ref[...])

    blk = pl.BlockSpec((None, tr, ccol), lambda i: (layer, i, 0))
    plane = pl.BlockSpec((ns, tr, ccol), lambda i: (0, i, 0))
    out = jax.ShapeDtypeStruct((nl, r, ccol), F32)
    n_prev = 4 if prev is not None else 0
    return pl.pallas_call(
        body, name=name, grid=(r // tr,),
        in_specs=[blk, blk, blk, plane, plane] + [_ANY] * n_prev,
        out_specs=[blk, blk, blk, blk], out_shape=[out, out, out, out],
        input_output_aliases={5 + k: k for k in range(n_prev)},
        compiler_params=_cparams(("parallel",)))(w, m, v, same_plane, other_plane, *(prev or ()))


def _adamw_whole(w, g, m, v, name):
    def body(w_ref, g_ref, m_ref, v_ref, d_ref, mo_ref, vo_ref):
        d_ref[...], mo_ref[...], vo_ref[...] = _adamw_math(w_ref[...], g_ref[...], m_ref[...], v_ref[...])

    out = jax.ShapeDtypeStruct(w.shape, F32)
    return pl.pallas_call(body, name=name, out_shape=[out, out, out],
                          compiler_params=pltpu.CompilerParams(vmem_limit_bytes=V7X_VMEM_LIMIT_BYTES))(w, g, m, v)


_W_NAMES = ['meta_tokens', 'ffn1_pre_g', 'ffn1_w_gate', 'ffn1_w_up', 'ffn1_w_down', 'ffn1_post_g', 'mix_pre_g', 'w_in',
            'lru_conv_w', 'lru_conv_b', 'lru_w_a', 'lru_b_a', 'lru_w_x', 'lru_b_x', 'lru_lambda', 'mla_q_norm_g', 'mla_w_uq',
            'mla_kv_norm_g', 'mla_w_ukv', 'lru_out_g', 'mla_out_g', 'w_out', 'mix_post_g', 'ffn2_pre_g', 'ffn2_w_gate',
            'ffn2_w_up', 'ffn2_w_down', 'ffn2_post_g']
_REPLICATED = ['ffn1_pre_g', 'ffn1_post_g', 'mix_pre_g', 'lru_conv_b', 'lru_w_a', 'lru_b_a', 'lru_w_x', 'lru_b_x', 'lru_lambda',
               'mla_q_norm_g', 'mla_kv_norm_g', 'lru_out_g', 'mla_out_g', 'mix_post_g', 'ffn2_pre_g', 'ffn2_post_g']
_LANES = 128


def _pack_rows(arrays):
    flat = jnp.concatenate([a.reshape(-1) for a in arrays])
    total = _round_up(flat.shape[0], 8 * _LANES)
    return jnp.pad(flat, (0, total - flat.shape[0])).reshape(-1, _LANES)


def _unpack_rows(buf, shapes):
    flat = buf.reshape(-1)
    out, off = [], 0
    for shp in shapes:
        n = math.prod(shp)
        out.append(flat[off:off + n].reshape(shp))
        off += n
    return out


def _block_diag(w):
    nl, nh, n, _ = w.shape
    eye = jnp.eye(nh, dtype=w.dtype)
    return (w[:, :, :, None, :] * eye[None, :, None, :, None]).reshape(nl, nh * n, nh * n)


def _diag_blocks(bd, nh):
    n = bd.shape[0] // nh
    b4 = bd.reshape(nh, n, nh, n)
    return jnp.stack([b4[i, :, i, :] for i in range(nh)])


def kernel(x, meta_tokens, ffn1_pre_g, ffn1_w_gate, ffn1_w_up, ffn1_w_down, ffn1_post_g, mix_pre_g, w_in, lru_conv_w, lru_conv_b, lru_w_a, lru_b_a, lru_w_x, lru_b_x, lru_lambda, mla_q_norm_g, mla_w_uq, mla_kv_norm_g, mla_w_ukv, lru_out_g, mla_out_g, w_out, mix_post_g, ffn2_pre_g, ffn2_w_gate, ffn2_w_up, ffn2_w_down, ffn2_post_g, loss_target, m_meta_tokens, m_ffn1_pre_g, m_ffn1_w_gate, m_ffn1_w_up, m_ffn1_w_down, m_ffn1_post_g, m_mix_pre_g, m_w_in, m_lru_conv_w, m_lru_conv_b, m_lru_w_a, m_lru_b_a, m_lru_w_x, m_lru_b_x, m_lru_lambda, m_mla_q_norm_g, m_mla_w_uq, m_mla_kv_norm_g, m_mla_w_ukv, m_lru_out_g, m_mla_out_g, m_w_out, m_mix_post_g, m_ffn2_pre_g, m_ffn2_w_gate, m_ffn2_w_up, m_ffn2_w_down, m_ffn2_post_g, v_meta_tokens, v_ffn1_pre_g, v_ffn1_w_gate, v_ffn1_w_up, v_ffn1_w_down, v_ffn1_post_g, v_mix_pre_g, v_w_in, v_lru_conv_w, v_lru_conv_b, v_lru_w_a, v_lru_b_a, v_lru_w_x, v_lru_b_x, v_lru_lambda, v_mla_q_norm_g, v_mla_w_uq, v_mla_kv_norm_g, v_mla_w_ukv, v_lru_out_g, v_mla_out_g, v_w_out, v_mix_post_g, v_ffn2_pre_g, v_ffn2_w_gate, v_ffn2_w_up, v_ffn2_w_down, v_ffn2_post_g):
    a = dict(locals())
    x2 = x[0]
    s_len, d = x2.shape
    nl = ffn1_pre_g.shape[0]
    lw = lru_conv_b.shape[-1]
    ql, kvl = mla_q_norm_g.shape[-1], mla_kv_norm_g.shape[-1]
    nh = MLA_HEADS
    vd = (d - lw) // nh
    t_real = N_META + s_len
    tp = _round_up(t_real, 384)
    tm = tp // 6
    tmb = tm // 2
    tm_ffn_fwd = tp // 4
    tm_ffn_bwd = tp // 8
    tq = 384 if tp >= 1536 else 128
    xi, yi, _ = _mesh_pos()
    chip = 2 * xi + yi

    def tr(w):
        return jnp.swapaxes(w, 1, 2)

    ffn_loc = jnp.stack([tr(ffn1_w_gate), tr(ffn1_w_up), ffn1_w_down, tr(ffn2_w_gate), tr(ffn2_w_up), ffn2_w_down],
                        axis=1).astype(BF16)
    loc = [ffn_loc, tr(w_in).astype(BF16), tr(mla_w_uq).astype(BF16), mla_w_ukv.astype(BF16), w_out.astype(BF16)]
    loc_shapes, loc_dtypes = [t.shape[1:] for t in loc], [t.dtype for t in loc]

    def layer_gather(l):
        return _GatherJob(loc, [lambda r, l=l: r.at[l]] * len(loc), loc_shapes, loc_dtypes, split_last=(1,))

    half_ffn = (ffn_loc.shape[1] // 2,) + ffn_loc.shape[2:]
    first = _GatherJob([ffn_loc, lru_conv_w, meta_tokens], [lambda r: r.at[0, pl.ds(0, half_ffn[0])], lambda r: r, lambda r: r],
                       [half_ffn, lru_conv_w.shape, meta_tokens.shape], [BF16, F32, F32], split_last=(0,))
    (w_ffn1_l0, g_conv, g_meta), = _run_jobs([first], "gather_first")
    rest_l0 = _GatherJob(loc, [lambda r: r.at[0, pl.ds(half_ffn[0], half_ffn[0])]] + [lambda r: r.at[0]] * (len(loc) - 1),
                         [half_ffn] + loc_shapes[1:], loc_dtypes, split_last=(0, 1))
    o0, o1, o2, o3 = lw, 2 * lw, 2 * lw + ql, 2 * lw + ql + kvl

    def assemble(gw, ffn1=None):
        w_ffn, g_win, g_wuq, g_wukv, g_wout = gw
        w_in_t = g_win.reshape(-1, d)
        w_kr = jnp.pad(w_in_t[o3:], ((QK_NOPE, 0), (0, 0)))
        w_ukv3 = jnp.transpose(g_wukv, (1, 0, 2)).reshape(kvl, nh, QK_NOPE + vd)
        w_out_full = g_wout.reshape(d, d)
        return dict(ffn1=ffn1 or (w_ffn, 0), ffn2=(w_ffn, 0 if ffn1 else 3),
                    parts=(w_in_t[:o0], w_in_t[o0:o1], w_in_t[o1:o2], w_in_t[o2:o3], w_kr),
                    q=g_wuq.reshape(nh, QK_DIM, ql),
                    k=jnp.transpose(jnp.pad(w_ukv3[..., :QK_NOPE], ((0, 0), (0, 0), (0, QK_ROPE))), (1, 0, 2)),
                    v=jnp.transpose(w_ukv3[..., QK_NOPE:], (1, 0, 2)),
                    lo=w_out_full[:lw], mo=w_out_full[lw:].reshape(nh, vd, d))

    conv_full = jnp.transpose(g_conv, (1, 2, 0, 3)).reshape(nl, CONV_WIDTH, lw)
    meta_full = jnp.transpose(g_meta, (1, 0, 2)).reshape(N_META, d)
    wa_bd = _block_diag(lru_w_a).astype(BF16)
    wx_bd = _block_diag(lru_w_x).astype(BF16)

    def gain3(g):
        return g.reshape(nl, 1, g.shape[-1])

    g3 = {n: gain3(a[n]) for n in ('ffn1_pre_g', 'ffn1_post_g', 'mix_pre_g', 'lru_conv_b', 'lru_b_a', 'lru_b_x', 'lru_lambda',
                                   'mla_q_norm_g', 'mla_kv_norm_g', 'lru_out_g', 'mix_post_g', 'ffn2_pre_g', 'ffn2_post_g')}
    g_mo = mla_out_g.reshape(nl, nh, 1, vd)

    pos = jnp.arange(tp, dtype=F32)
    inv_freq = 1.0 / (ROPE_THETA ** (jnp.arange(0, QK_ROPE, 2, dtype=F32) / QK_ROPE))
    ang = pos[:, None] * inv_freq[None, :]
    cos_t = jnp.concatenate([jnp.ones((tp, QK_NOPE), F32), jnp.cos(ang), jnp.cos(ang)], axis=1)
    sin_t = jnp.concatenate([jnp.zeros((tp, QK_NOPE), F32), jnp.sin(ang), jnp.sin(ang)], axis=1)
    half = QK_ROPE // 2
    rot = np.zeros((QK_DIM, QK_DIM), np.float32)
    for i in range(half):
        rot[QK_NOPE + half + i, QK_NOPE + i] = -1.0
        rot[QK_NOPE + i, QK_NOPE + half + i] = 1.0
    rot_b, rot_t = jnp.asarray(rot, BF16), jnp.asarray(rot.T, BF16)

    h = jnp.concatenate([meta_full, x2, jnp.zeros((tp - t_real, d), F32)], axis=0)
    target = jnp.pad(loss_target[0], ((N_META, tp - t_real), (0, 0)))
    saved, weights = [], []
    for l in range(nl):
        if l == 0:
            h1, gate1, up1, f1, gathered = _ffn_fwd(h, g3['ffn1_pre_g'], g3['ffn1_post_g'], w_ffn1_l0, l, 0, 1, 2, tm_ffn_fwd,
                                                    f"ffn1_fwd_{l}", rest_l0)
            wl = assemble(gathered, ffn1=(w_ffn1_l0, 0))
        else:
            wl = assemble(gathered)
            wf, k0 = wl['ffn1']
            h1, gate1, up1, f1, _ = _ffn_fwd(h, g3['ffn1_pre_g'], g3['ffn1_post_g'], wf, l, k0, k0 + 1, k0 + 2, tm_ffn_fwd,
                                             f"ffn1_fwd_{l}")
        weights.append(wl)
        xr, gr, cq, ckv, q, k, v = _mix_in_fwd(h1, g3['mix_pre_g'], wl['parts'], g3['mla_q_norm_g'], g3['mla_kv_norm_g'],
                                               wl['q'], wl['k'], wl['v'], cos_t, sin_t, rot_b, l, tm, f"mix_in_fwd_{l}")
        y_lru, xc, hs = _lru_fwd(xr, gr, conv_full, g3['lru_conv_b'], wa_bd, wx_bd, g3['lru_b_a'], g3['lru_b_x'],
                                 g3['lru_lambda'], l, tm, f"lru_fwd_{l}")
        o, lse, gathered = _attn_fwd(q, k, v, tq, f"attn_fwd_{l}", layer_gather(l + 1) if l + 1 < nl else None)
        h2, y = _mix_out_fwd(h1, y_lru, o, g3['lru_out_g'], g_mo, wl['lo'], wl['mo'], g3['mix_post_g'], l, tm, f"mix_out_fwd_{l}")
        wf, k0 = wl['ffn2']
        h3, gate2, up2, f2, _ = _ffn_fwd(h2, g3['ffn2_pre_g'], g3['ffn2_post_g'], wf, l, k0, k0 + 1, k0 + 2, tm_ffn_fwd,
                                         f"ffn2_fwd_{l}")
        saved.append((h, gate1, up1, f1, h1, xr, gr, cq, ckv, q, k, v, y_lru, xc, hs, o, lse, y, h2, gate2, up2, f2))
        h = h3

    dh, loss_parts = _loss_head(h, target, s_len, tm, "loss_head")
    loss = lax.psum(jnp.sum(loss_parts[:, 0, 0]), ("x", "y", "c"))

    small_order = _REPLICATED + ['lru_conv_w']
    small = {n: [None] * nl for n in small_order}
    planes = {}
    late_names = ['w_in', 'mla_w_uq', 'mla_w_ukv', 'ffn1_w_gate', 'ffn1_w_up', 'ffn1_w_down']
    early_names = ['ffn2_w_gate', 'ffn2_w_up', 'ffn2_w_down', 'w_out']
    late = None

    def keep_planes(keys, outs):
        for i, key in enumerate(keys):
            planes[key] = (outs[i], outs[len(keys) + i])

    for l in reversed(range(nl)):
        (h0, gate1, up1, f1, h1, xr, gr, cq, ckv, q, k, v, y_lru, xc, hs, o, lse, y, h2, gate2, up2, f2) = saved[l]
        wl = weights[l]
        wf, k0 = wl['ffn2']
        dh, dgate, dup, act, df, u, dgpre, dgpost, _ = _ffn_bwd(dh, h2, f2, gate2, up2, g3['ffn2_pre_g'], g3['ffn2_post_g'],
                                                               wf, l, k0, k0 + 1, k0 + 2, tm_ffn_bwd, f"ffn2_bwd_{l}")
        small['ffn2_pre_g'][l], small['ffn2_post_g'][l] = dgpre, dgpost
        d_wg2 = _mm_tn(dgate, u, tm, BF16, f"dw_gate2_{l}")
        d_wu2 = _mm_tn(dup, u, tm, BF16, f"dw_up2_{l}")
        dn2 = _mm_tn(act, df, tm, BF16, f"dw_down2_{l}")

        d_ylru, d_o, dy_b, n_lo, n_mo, dgpost, dglo, dgmo = _mix_out_bwd(dh, y, y_lru, o, g3['lru_out_g'], g_mo, wl['lo'], wl['mo'],
                                                                       g3['mix_post_g'], l, tm, f"mix_out_bwd_{l}")
        small['mix_post_g'][l], small['lru_out_g'][l], small['mla_out_g'][l] = dgpost, dglo, dgmo
        d_wlo = _mm_tn(n_lo, dy_b, tm, F32, f"dw_out_lru_{l}")
        d_wmo = _mm_tn_heads(n_mo, dy_b, tm, f"dw_out_mla_{l}")
        d_wout = jnp.concatenate([d_wlo, d_wmo.reshape(nh * vd, d)], axis=0).reshape(N_CHIPS, d // N_CHIPS, d).astype(BF16)

        keys = [(n, l) for n in early_names] + ([(n, l + 1) for n in late_names] if late else [])
        jobs = [_ReduceJob([d_wg2, d_wu2, dn2, d_wout] + (late or []))]
        if l == 0 and nl > 1:
            deep = _pack_rows([jnp.stack(small[n][1:]) for n in small_order])
            jobs.append(_DeviceGatherJob(deep))
        group = _JobGroup(jobs)
        dq, dk, dv, outs = _attn_bwd(q, k, v, o, d_o, lse, tq, f"attn_bwd_{l}", group)
        outs = group.split(outs)
        keep_planes(keys, outs[0])
        if len(jobs) > 1:
            all_deep = outs[1][0]
        d_xr, d_gr, dcw, dcb, dwa, dwx, dba, dbx, dlam = _lru_bwd(d_ylru, xr, gr, xc, hs, conv_full, wa_bd, wx_bd, g3['lru_b_a'],
                                                                 g3['lru_b_x'], g3['lru_lambda'], l, tm, f"lru_bwd_{l}")
        small['lru_conv_w'][l], small['lru_conv_b'][l] = dcw, dcb
        small['lru_w_a'][l], small['lru_w_x'][l] = _diag_blocks(dwa, LRU_HEADS), _diag_blocks(dwx, LRU_HEADS)
        small['lru_b_a'][l], small['lru_b_x'][l], small['lru_lambda'][l] = dba, dbx, dlam

        dh, u, qn, kvn, dq_pre, d_cq, d_ckv, d_kr, dgpre, dgq, dgkv = _mix_in_bwd(
            dh, h1, cq, ckv, d_xr, d_gr, dq, dk, dv, g3['mix_pre_g'], wl['parts'], g3['mla_q_norm_g'], g3['mla_kv_norm_g'],
            wl['q'], wl['k'], wl['v'], cos_t, sin_t, rot_t, l, tmb, f"mix_in_bwd_{l}")
        small['mix_pre_g'][l], small['mla_q_norm_g'][l], small['mla_kv_norm_g'][l] = dgpre, dgq, dgkv
        d_win = jnp.concatenate([_mm_tn(d_xr, u, tm, F32, f"dw_in_xr_{l}"), _mm_tn(d_gr, u, tm, F32, f"dw_in_gr_{l}"),
                                 _mm_tn(d_cq, u, tm, F32, f"dw_in_cq_{l}"), _mm_tn(d_ckv, u, tm, F32, f"dw_in_ckv_{l}"),
                                 _mm_tn(d_kr, u, tm, F32, f"dw_in_kr_{l}")[QK_NOPE:]], axis=0)
        d_win = d_win.reshape(N_CHIPS, -1, d).astype(BF16)
        d_wuq = _mm_tn_heads(dq_pre, qn, tm, f"dw_uq_{l}").reshape(N_CHIPS, -1, ql).astype(BF16)
        d_wk = _mm_tn_heads(dk, kvn, tm, f"dw_uk_{l}")[:, :QK_NOPE]
        d_wv = _mm_tn_heads(dv, kvn, tm, f"dw_uv_{l}")
        d_wukv = jnp.transpose(jnp.concatenate([d_wk, d_wv], axis=1).reshape(N_CHIPS, -1, kvl), (0, 2, 1)).astype(BF16)

        wf, k0 = wl['ffn1']
        last_layer = l == 0
        dh, dgate, dup, act, df, u, dgpre, dgpost, outs = _ffn_bwd(
            dh, h0, f1, gate1, up1, g3['ffn1_pre_g'], g3['ffn1_post_g'], wf, l, k0, k0 + 1, k0 + 2, tm_ffn_bwd, f"ffn1_bwd_{l}",
            _ReduceJob([d_win, d_wuq, d_wukv]) if last_layer else None)
        small['ffn1_pre_g'][l], small['ffn1_post_g'][l] = dgpre, dgpost
        d_wg1 = _mm_tn(dgate, u, tm, BF16, f"dw_gate1_{l}")
        if last_layer:
            keep_planes([(n, 0) for n in late_names[:3]], outs)
            d_wu1, outs = _mm_tn(dup, u, tm, BF16, f"dw_up1_{l}", _ReduceJob([d_wg1]))
            keep_planes([('ffn1_w_gate', 0)], outs)
            dn1, outs = _mm_tn(act, df, tm, BF16, f"dw_down1_{l}", _ReduceJob([d_wu1]))
            keep_planes([('ffn1_w_up', 0)], outs)
            late = [dn1]
        else:
            d_wu1 = _mm_tn(dup, u, tm, BF16, f"dw_up1_{l}")
            dn1 = _mm_tn(act, df, tm, BF16, f"dw_down1_{l}")
            late = [d_win, d_wuq, d_wukv, d_wg1, d_wu1, dn1]

    grad_x = dh[N_META:t_real][None]

    shallow = _pack_rows([small[n][0] for n in small_order] + [dh[:N_META]])
    late_planes, (all_shallow,) = _run_jobs([_ReduceJob(late), _DeviceGatherJob(shallow)], "reduce_last")
    keep_planes([('ffn1_w_down', 0)], late_planes)

    res = {}
    stored_transposed = ('ffn1_w_gate', 'ffn1_w_up', 'ffn2_w_gate', 'ffn2_w_up', 'w_in', 'mla_w_uq')
    for name in early_names + late_names:
        view = tr if name in stored_transposed else (lambda t: t)
        w_v, m_v, v_v = view(a[name]), view(a['m_' + name]), view(a['v_' + name])
        prev = None
        for l in reversed(range(nl)):
            same_plane, other_plane = planes[(name, l)]
            prev = _adamw_layer(w_v, m_v, v_v, same_plane, other_plane, l, prev, f"adamw_{name}_{l}")
        res[name] = [view(t) for t in prev]

    shapes0 = [small[n][0].shape for n in small_order]
    pieces = _unpack_rows(_sum_devices(all_shallow, "sum_small_last"), shapes0 + [(N_META, d)])
    per_layer = [p[None] for p in pieces[:-1]]
    if nl > 1:
        deeper = _unpack_rows(_sum_devices(all_deep, "sum_small_deep"), [(nl - 1,) + s for s in shapes0])
        per_layer = [jnp.concatenate([p0, p1], axis=0) for p0, p1 in zip(per_layer, deeper)]
    g_small = {n: p.reshape(a[n].shape if n != 'lru_conv_w' else conv_full.shape) for n, p in zip(small_order, per_layer)}
    g_conv_loc = lax.dynamic_slice_in_dim(g_small.pop('lru_conv_w'), chip * lru_conv_w.shape[-1], lru_conv_w.shape[-1], axis=2)
    g_meta_loc = lax.dynamic_slice_in_dim(pieces[-1], chip * meta_tokens.shape[-1], meta_tokens.shape[-1], axis=1)

    shapes = [a[n].shape for n in _REPLICATED]
    dl, mo, vo = _adamw_whole(_pack_rows([a[n] for n in _REPLICATED]), _pack_rows([g_small[n] for n in _REPLICATED]),
                              _pack_rows([a['m_' + n] for n in _REPLICATED]), _pack_rows([a['v_' + n] for n in _REPLICATED]),
                              "adamw_replicated")
    for n, dd, mm, vv in zip(_REPLICATED, _unpack_rows(dl, shapes), _unpack_rows(mo, shapes), _unpack_rows(vo, shapes)):
        res[n] = (g_small[n], dd, mm, vv)
    cshape = lru_conv_w.shape
    c2 = (cshape[0] * cshape[1], cshape[2])
    dd, mm, vv = _adamw_whole(lru_conv_w.reshape(c2), g_conv_loc.reshape(c2), m_lru_conv_w.reshape(c2), v_lru_conv_w.reshape(c2),
                              "adamw_conv_w")
    res['lru_conv_w'] = (g_conv_loc, dd.reshape(cshape), mm.reshape(cshape), vv.reshape(cshape))
    res['meta_tokens'] = (g_meta_loc,) + tuple(_adamw_whole(meta_tokens, g_meta_loc, m_meta_tokens, v_meta_tokens, "adamw_meta"))

    return (loss, grad_x, *[res[n][0] for n in _W_NAMES], *[res[n][1] for n in _W_NAMES],
            *[res[n][2] for n in _W_NAMES], *[res[n][3] for n in _W_NAMES])
```
